```python
import jax, jax.numpy as jnp
from jax import lax
import numpy as np

D_MODEL = 1024
BATCH = 16
SEQ = 4096
DEPTH = 1

PLE_DIM = 256
HG_WIDTH = D_MODEL // 2
HG_KDIM = 128
HG_HEADS = HG_WIDTH // HG_KDIM
HG_VDIM = HG_WIDTH // HG_HEADS
HG_CHUNK = 64
FOX_WIDTH = D_MODEL // 2
FOX_HDIM = 64
FOX_HEADS = FOX_WIDTH // FOX_HDIM
FOX_BLOCK = 128
D_FF = 4 * D_MODEL
N_IN = 4 * HG_WIDTH + 3 * FOX_WIDTH + FOX_HEADS + 2 * D_MODEL
LN_EPS = 1e-5
RMS_EPS = 1e-6
DEEPNORM_ALPHA = (2.0 * DEPTH) ** 0.25
DEEPNORM_BETA = (8.0 * DEPTH) ** -0.25

kernel_name = "hybrid_hgrn2_fox_deepnorm_block"


def layer_norm(x, g, b):
    xf = x.astype(jnp.float32)
    mu = jnp.mean(xf, axis=-1, keepdims=True)
    var = jnp.mean(jnp.square(xf - mu), axis=-1, keepdims=True)
    y = (xf - mu) * lax.rsqrt(var + LN_EPS)
    return (y * g.astype(jnp.float32) + b.astype(jnp.float32)).astype(x.dtype)


def hgrn2_lower_bound(lb_logits, layer):
    probs = jax.nn.softmax(lb_logits.astype(jnp.float32), axis=0)
    return jnp.cumsum(probs, axis=0)[layer]


def _hgrn2_chunk_step(state, chunk):
    q_c, k_c, v_c, g_c = chunk
    C = q_c.shape[2]
    b = jnp.cumsum(g_c, axis=2)
    causal = jnp.tril(jnp.ones((C, C), dtype=bool))[:, :, None]
    diff = b[:, :, :, None, :] - b[:, :, None, :, :]
    decay = jnp.exp(jnp.where(causal, diff, -jnp.inf))
    scores = jnp.einsum('bhtk,bhtsk,bhsk->bhts', q_c, decay, k_c)
    o = (jnp.einsum('bhts,bhsv->bhtv', scores, v_c)
         + jnp.einsum('bhtk,bhkv->bhtv', q_c * jnp.exp(b), state))
    b_last = b[:, :, -1:, :]
    new_state = (jnp.exp(b_last[:, :, 0, :])[..., None] * state
                 + jnp.einsum('bhsk,bhsv->bhkv', k_c * jnp.exp(b_last - b), v_c))
    return new_state, o


def hgrn2_branch(q_lin, f_lin, i_lin, og_lin, lb, norm_g):
    B, S, _ = q_lin.shape
    n_chunks = S // HG_CHUNK
    q = jax.nn.silu(q_lin.astype(jnp.float32))
    f = lb + (1.0 - lb) * jax.nn.sigmoid(f_lin.astype(jnp.float32))
    log_f = jnp.log(f)
    k = 1.0 - f
    v = i_lin.astype(jnp.float32)

    def to_chunks(t, d):
        return t.reshape(B, n_chunks, HG_CHUNK, HG_HEADS, d).transpose(1, 0, 3, 2, 4)

    xs = (to_chunks(q, HG_KDIM), to_chunks(k, HG_KDIM), to_chunks(v, HG_VDIM), to_chunks(log_f, HG_KDIM))
    state0 = jnp.zeros((B, HG_HEADS, HG_KDIM, HG_VDIM), jnp.float32)
    _, o = lax.scan(_hgrn2_chunk_step, state0, xs)
    o = o.transpose(1, 0, 3, 2, 4).reshape(B, S, HG_HEADS, HG_VDIM)
    o = o * lax.rsqrt(jnp.mean(jnp.square(o), axis=-1, keepdims=True) + RMS_EPS)
    o = o.reshape(B, S, HG_WIDTH) * norm_g.astype(jnp.float32) * jax.nn.silu(og_lin.astype(jnp.float32))
    return o.astype(q_lin.dtype)


def fox_branch(q_lin, k_lin, v_lin, f_lin, f_bias):
    B, S, _ = q_lin.shape
    def heads(t):
        return t.reshape(B, S, FOX_HEADS, FOX_HDIM).transpose(0, 2, 1, 3)
    q, k, v = heads(q_lin), heads(k_lin), heads(v_lin)
    log_f = jax.nn.log_sigmoid(f_lin.astype(jnp.float32) + f_bias.astype(jnp.float32))
    c = jnp.cumsum(log_f.transpose(0, 2, 1), axis=-1)
    scale = FOX_HDIM ** -0.5
    outs = []
    for blk in range(S // FOX_BLOCK):
        start, end = blk * FOX_BLOCK, (blk + 1) * FOX_BLOCK
        qb = q[:, :, start:end]
        kb, vb = k[:, :, :end], v[:, :, :end]
        logits = jnp.einsum('bhqd,bhkd->bhqk', qb, kb).astype(jnp.float32) * scale
        logits = logits + (c[:, :, start:end, None] - c[:, :, None, :end])
        q_pos = jnp.arange(start, end)
        k_pos = jnp.arange(end)
        mask = k_pos[None, :] <= q_pos[:, None]
        probs = jax.nn.softmax(jnp.where(mask, logits, -jnp.inf), axis=-1)
        outs.append(jnp.einsum('bhqk,bhkd->bhqd', probs.astype(vb.dtype), vb))
    o = jnp.concatenate(outs, axis=2)
    return o.transpose(0, 2, 1, 3).reshape(B, S, FOX_WIDTH)


def _fwd_setup_inputs(seed: int = 0) -> dict:
    key = jax.random.key(seed)
    ks = jax.random.split(key, 20)
    f32 = jnp.float32
    nrm = lambda k, shape, s: jax.random.normal(k, shape, f32) * s
    beta = DEEPNORM_BETA
    return {
        "x": nrm(ks[0], (BATCH, SEQ, D_MODEL), 1.0),
        "p": nrm(ks[1], (DEPTH, BATCH, SEQ, PLE_DIM), 1.0),
        "ln0_g": 1.0 + nrm(ks[2], (D_MODEL,), 0.02),
        "ln0_b": nrm(ks[3], (D_MODEL,), 0.02),
        "w_in": nrm(ks[4], (DEPTH, D_MODEL, N_IN), D_MODEL ** -0.5),
        "hg_lb": nrm(ks[5], (DEPTH + 1, HG_WIDTH), 1.0),
        "hg_norm_g": 1.0 + nrm(ks[6], (DEPTH, HG_WIDTH), 0.02),
        "fox_fb": jnp.linspace(1.0, 4.0, FOX_HEADS, dtype=f32)[None, :] + nrm(ks[7], (DEPTH, FOX_HEADS), 0.1),
        "w_a": nrm(ks[8], (DEPTH, HG_WIDTH, D_MODEL), beta * HG_WIDTH ** -0.5),
        "w_b": nrm(ks[9], (DEPTH, FOX_WIDTH, D_MODEL), beta * FOX_WIDTH ** -0.5),
        "w_o": nrm(ks[10], (DEPTH, D_MODEL, D_MODEL), beta * D_MODEL ** -0.5),
        "ln1_g": 1.0 + nrm(ks[11], (DEPTH, D_MODEL), 0.02),
        "ln1_b": nrm(ks[12], (DEPTH, D_MODEL), 0.02),
        "w_ff1": nrm(ks[13], (DEPTH, D_MODEL, D_FF), beta * D_MODEL ** -0.5),
        "w_ff2": nrm(ks[14], (DEPTH, D_FF, D_MODEL), beta * D_FF ** -0.5),
        "w_pg": nrm(ks[15], (DEPTH, D_MODEL, D_MODEL), D_MODEL ** -0.5),
        "w_p": nrm(ks[16], (DEPTH, PLE_DIM, D_MODEL), beta * PLE_DIM ** -0.5),
        "ln2_g": 1.0 + nrm(ks[17], (DEPTH, D_MODEL), 0.02),
        "ln2_b": nrm(ks[18], (DEPTH, D_MODEL), 0.02),
    }


def _fwd_reference(x, p, ln0_g, ln0_b, w_in, hg_lb, hg_norm_g, fox_fb, w_a, w_b, w_o,
              ln1_g, ln1_b, w_ff1, w_ff2, w_pg, w_p, ln2_g, ln2_b):
    sizes = [HG_WIDTH] * 4 + [FOX_WIDTH] * 3 + [FOX_HEADS] + [D_MODEL] * 2
    split_at = [int(s) for s in np.cumsum(sizes)[:-1]]
    h = layer_norm(x, ln0_g, ln0_b)
    for i in range(DEPTH):
        proj = jnp.einsum('bsd,dn->bsn', h, w_in[i])
        (hg_q, hg_f, hg_i, hg_g, fx_q, fx_k, fx_v, fx_f, gate_a, gate_b) = jnp.split(proj, split_at, axis=-1)
        lb = hgrn2_lower_bound(hg_lb, i)
        y_a = hgrn2_branch(hg_q, hg_f, hg_i, hg_g, lb, hg_norm_g[i])
        y_b = fox_branch(fx_q, fx_k, fx_v, fx_f, fox_fb[i])
        merged = (jax.nn.sigmoid(gate_a) * jnp.einsum('bsc,cd->bsd', y_a, w_a[i])
                  + jax.nn.sigmoid(gate_b) * jnp.einsum('bsc,cd->bsd', y_b, w_b[i]))
        mix = jnp.einsum('bsd,de->bse', merged, w_o[i])
        h = layer_norm(DEEPNORM_ALPHA * h + mix, ln1_g[i], ln1_b[i])
        ff = jnp.einsum('bsf,fd->bsd', jnp.square(jax.nn.relu(jnp.einsum('bsd,df->bsf', h, w_ff1[i]))), w_ff2[i])
        ple = jax.nn.sigmoid(jnp.einsum('bsd,de->bse', h, w_pg[i])) * jnp.einsum('bsp,pd->bsd', p[i], w_p[i])
        h = layer_norm(DEEPNORM_ALPHA * h + ff + ple, ln2_g[i], ln2_b[i])
    return h


import jax as _jax
import jax.numpy as _jnp

TWIN_FORMAT = 'train_step'
FWD_PARAMS = ['x', 'p', 'ln0_g', 'ln0_b', 'w_in', 'hg_lb', 'hg_norm_g', 'fox_fb', 'w_a', 'w_b', 'w_o', 'ln1_g', 'ln1_b', 'w_ff1', 'w_ff2', 'w_pg', 'w_p', 'ln2_g', 'ln2_b']
TWIN_WEIGHTS = ['ln0_g', 'ln0_b', 'w_in', 'hg_lb', 'hg_norm_g', 'fox_fb', 'w_a', 'w_b', 'w_o', 'ln1_g', 'ln1_b', 'w_ff1', 'w_ff2', 'w_pg', 'w_p', 'ln2_g', 'ln2_b']
TWIN_DIFF_INPUT = 'x'
TWIN_INPUTS = ['x', 'p', 'ln0_g', 'ln0_b', 'w_in', 'hg_lb', 'hg_norm_g', 'fox_fb', 'w_a', 'w_b', 'w_o', 'ln1_g', 'ln1_b', 'w_ff1', 'w_ff2', 'w_pg', 'w_p', 'ln2_g', 'ln2_b', 'loss_target', 'm_ln0_g', 'm_ln0_b', 'm_w_in', 'm_hg_lb', 'm_hg_norm_g', 'm_fox_fb', 'm_w_a', 'm_w_b', 'm_w_o', 'm_ln1_g', 'm_ln1_b', 'm_w_ff1', 'm_w_ff2', 'm_w_pg', 'm_w_p', 'm_ln2_g', 'm_ln2_b', 'v_ln0_g', 'v_ln0_b', 'v_w_in', 'v_hg_lb', 'v_hg_norm_g', 'v_fox_fb', 'v_w_a', 'v_w_b', 'v_w_o', 'v_ln1_g', 'v_ln1_b', 'v_w_ff1', 'v_w_ff2', 'v_w_pg', 'v_w_p', 'v_ln2_g', 'v_ln2_b']
TWIN_OUTPUTS = ['loss', 'grad_x', 'grad_ln0_g', 'grad_ln0_b', 'grad_w_in', 'grad_hg_lb', 'grad_hg_norm_g', 'grad_fox_fb', 'grad_w_a', 'grad_w_b', 'grad_w_o', 'grad_ln1_g', 'grad_ln1_b', 'grad_w_ff1', 'grad_w_ff2', 'grad_w_pg', 'grad_w_p', 'grad_ln2_g', 'grad_ln2_b', 'delta_ln0_g', 'delta_ln0_b', 'delta_w_in', 'delta_hg_lb', 'delta_hg_norm_g', 'delta_fox_fb', 'delta_w_a', 'delta_w_b', 'delta_w_o', 'delta_ln1_g', 'delta_ln1_b', 'delta_w_ff1', 'delta_w_ff2', 'delta_w_pg', 'delta_w_p', 'delta_ln2_g', 'delta_ln2_b', 'new_m_ln0_g', 'new_m_ln0_b', 'new_m_w_in', 'new_m_hg_lb', 'new_m_hg_norm_g', 'new_m_fox_fb', 'new_m_w_a', 'new_m_w_b', 'new_m_w_o', 'new_m_ln1_g', 'new_m_ln1_b', 'new_m_w_ff1', 'new_m_w_ff2', 'new_m_w_pg', 'new_m_w_p', 'new_m_ln2_g', 'new_m_ln2_b', 'new_v_ln0_g', 'new_v_ln0_b', 'new_v_w_in', 'new_v_hg_lb', 'new_v_hg_norm_g', 'new_v_fox_fb', 'new_v_w_a', 'new_v_w_b', 'new_v_w_o', 'new_v_ln1_g', 'new_v_ln1_b', 'new_v_w_ff1', 'new_v_w_ff2', 'new_v_w_pg', 'new_v_w_p', 'new_v_ln2_g', 'new_v_ln2_b']
TWIN_LEAF_KINDS = {'loss': 'loss', 'grad_x': 'grad_x', 'grad_ln0_g': 'grad_w', 'grad_ln0_b': 'grad_w', 'grad_w_in': 'grad_w', 'grad_hg_lb': 'grad_w', 'grad_hg_norm_g': 'grad_w', 'grad_fox_fb': 'grad_w', 'grad_w_a': 'grad_w', 'grad_w_b': 'grad_w', 'grad_w_o': 'grad_w', 'grad_ln1_g': 'grad_w', 'grad_ln1_b': 'grad_w', 'grad_w_ff1': 'grad_w', 'grad_w_ff2': 'grad_w', 'grad_w_pg': 'grad_w', 'grad_w_p': 'grad_w', 'grad_ln2_g': 'grad_w', 'grad_ln2_b': 'grad_w', 'delta_ln0_g': 'delta_w', 'delta_ln0_b': 'delta_w', 'delta_w_in': 'delta_w', 'delta_hg_lb': 'delta_w', 'delta_hg_norm_g': 'delta_w', 'delta_fox_fb': 'delta_w', 'delta_w_a': 'delta_w', 'delta_w_b': 'delta_w', 'delta_w_o': 'delta_w', 'delta_ln1_g': 'delta_w', 'delta_ln1_b': 'delta_w', 'delta_w_ff1': 'delta_w', 'delta_w_ff2': 'delta_w', 'delta_w_pg': 'delta_w', 'delta_w_p': 'delta_w', 'delta_ln2_g': 'delta_w', 'delta_ln2_b': 'delta_w', 'new_m_ln0_g': 'new_m', 'new_m_ln0_b': 'new_m', 'new_m_w_in': 'new_m', 'new_m_hg_lb': 'new_m', 'new_m_hg_norm_g': 'new_m', 'new_m_fox_fb': 'new_m', 'new_m_w_a': 'new_m', 'new_m_w_b': 'new_m', 'new_m_w_o': 'new_m', 'new_m_ln1_g': 'new_m', 'new_m_ln1_b': 'new_m', 'new_m_w_ff1': 'new_m', 'new_m_w_ff2': 'new_m', 'new_m_w_pg': 'new_m', 'new_m_w_p': 'new_m', 'new_m_ln2_g': 'new_m', 'new_m_ln2_b': 'new_m', 'new_v_ln0_g': 'new_v', 'new_v_ln0_b': 'new_v', 'new_v_w_in': 'new_v', 'new_v_hg_lb': 'new_v', 'new_v_hg_norm_g': 'new_v', 'new_v_fox_fb': 'new_v', 'new_v_w_a': 'new_v', 'new_v_w_b': 'new_v', 'new_v_w_o': 'new_v', 'new_v_ln1_g': 'new_v', 'new_v_ln1_b': 'new_v', 'new_v_w_ff1': 'new_v', 'new_v_w_ff2': 'new_v', 'new_v_w_pg': 'new_v', 'new_v_w_p': 'new_v', 'new_v_ln2_g': 'new_v', 'new_v_ln2_b': 'new_v'}


def _forward(args):
    return _fwd_reference(*[args[k] for k in FWD_PARAMS])


def _output_shape():
    out = _jax.eval_shape(lambda: _forward(_fwd_setup_inputs(0)))
    return out.shape, out.dtype

N_MICROBATCH = 1
ADAM_LR = 0.001
ADAM_B1 = 0.9
ADAM_B2 = 0.999
ADAM_EPS = 1e-08
ADAM_WD = 0.01
ADAM_STEP = 10
PER_EXAMPLE_BATCH_AXIS = {'x': 0, 'p': 1, 'loss_target': 0}
SHARED_INPUTS = []
_WEIGHT_DTYPES = {'ln0_g': _jnp.float32, 'ln0_b': _jnp.float32, 'w_in': _jnp.float32, 'hg_lb': _jnp.float32, 'hg_norm_g': _jnp.float32, 'fox_fb': _jnp.float32, 'w_a': _jnp.float32, 'w_b': _jnp.float32, 'w_o': _jnp.float32, 'ln1_g': _jnp.float32, 'ln1_b': _jnp.float32, 'w_ff1': _jnp.float32, 'w_ff2': _jnp.float32, 'w_pg': _jnp.float32, 'w_p': _jnp.float32, 'ln2_g': _jnp.float32, 'ln2_b': _jnp.float32}
MOMENT_SCALE = {'ln0_g': 1.620902e+00, 'ln0_b': 7.260334e-01, 'w_in': 1.800918e-02, 'hg_lb': 3.182426e-03, 'hg_norm_g': 3.409723e-02, 'fox_fb': 6.780223e-02, 'w_a': 3.971820e-02, 'w_b': 2.583815e-02, 'w_o': 4.695584e-02, 'ln1_g': 1.642003e+00, 'ln1_b': 7.457743e-01, 'w_ff1': 4.990811e-02, 'w_ff2': 1.161146e-01, 'w_pg': 2.506275e-02, 'w_p': 1.081380e-01, 'ln2_g': 6.409757e+01, 'ln2_b': 5.041804e+00}


def _to_microbatches(a, axis):
    t = _jnp.moveaxis(a, axis, 0)
    t = t.reshape((N_MICROBATCH, t.shape[0] // N_MICROBATCH) + t.shape[1:])
    return _jnp.moveaxis(t, 1, axis + 1)


def setup_inputs(seed: int = 0) -> dict:
    inp = _fwd_setup_inputs(seed)
    key = _jax.random.fold_in(_jax.random.key(seed), 7919)
    shape, _ = _output_shape()
    out = dict(inp)
    out["loss_target"] = _jax.random.normal(_jax.random.fold_in(key, 0), shape, _jnp.float32)
    for i, name in enumerate(TWIN_WEIGHTS):
        w = inp[name].astype(_jnp.float32)
        if MOMENT_SCALE is None:
            s = _jnp.sqrt(_jnp.mean(_jnp.square(w)) + 1e-30)
        else:
            s = MOMENT_SCALE[name]
        km, kv = _jax.random.split(_jax.random.fold_in(key, i + 1))
        out[name] = w
        out["m_" + name] = s * _jax.random.normal(km, w.shape, _jnp.float32)
        out["v_" + name] = (s * s) * _jax.random.uniform(kv, w.shape, _jnp.float32, 0.5, 1.5)
    if N_MICROBATCH > 1:
        for name, axis in PER_EXAMPLE_BATCH_AXIS.items():
            out[name] = _to_microbatches(out[name], axis)
    return {'x': out['x'], 'p': out['p'], 'ln0_g': out['ln0_g'], 'ln0_b': out['ln0_b'], 'w_in': out['w_in'], 'hg_lb': out['hg_lb'], 'hg_norm_g': out['hg_norm_g'], 'fox_fb': out['fox_fb'], 'w_a': out['w_a'], 'w_b': out['w_b'], 'w_o': out['w_o'], 'ln1_g': out['ln1_g'], 'ln1_b': out['ln1_b'], 'w_ff1': out['w_ff1'], 'w_ff2': out['w_ff2'], 'w_pg': out['w_pg'], 'w_p': out['w_p'], 'ln2_g': out['ln2_g'], 'ln2_b': out['ln2_b'], 'loss_target': out['loss_target'], 'm_ln0_g': out['m_ln0_g'], 'm_ln0_b': out['m_ln0_b'], 'm_w_in': out['m_w_in'], 'm_hg_lb': out['m_hg_lb'], 'm_hg_norm_g': out['m_hg_norm_g'], 'm_fox_fb': out['m_fox_fb'], 'm_w_a': out['m_w_a'], 'm_w_b': out['m_w_b'], 'm_w_o': out['m_w_o'], 'm_ln1_g': out['m_ln1_g'], 'm_ln1_b': out['m_ln1_b'], 'm_w_ff1': out['m_w_ff1'], 'm_w_ff2': out['m_w_ff2'], 'm_w_pg': out['m_w_pg'], 'm_w_p': out['m_w_p'], 'm_ln2_g': out['m_ln2_g'], 'm_ln2_b': out['m_ln2_b'], 'v_ln0_g': out['v_ln0_g'], 'v_ln0_b': out['v_ln0_b'], 'v_w_in': out['v_w_in'], 'v_hg_lb': out['v_hg_lb'], 'v_hg_norm_g': out['v_hg_norm_g'], 'v_fox_fb': out['v_fox_fb'], 'v_w_a': out['v_w_a'], 'v_w_b': out['v_w_b'], 'v_w_o': out['v_w_o'], 'v_ln1_g': out['v_ln1_g'], 'v_ln1_b': out['v_ln1_b'], 'v_w_ff1': out['v_w_ff1'], 'v_w_ff2': out['v_w_ff2'], 'v_w_pg': out['v_w_pg'], 'v_w_p': out['v_w_p'], 'v_ln2_g': out['v_ln2_g'], 'v_ln2_b': out['v_ln2_b']}


def _loss(weights, diff, rest, loss_target):
    with _jax.named_scope("forward"):
        args = {**rest, TWIN_DIFF_INPUT: diff, **{k: w.astype(_WEIGHT_DTYPES[k]) for k, w in weights.items()}}
        y = _forward(args)
    with _jax.named_scope("loss_head"):
        err = _jnp.square(y.astype(_jnp.float32) - loss_target)
        return 0.5 * _jnp.sum(_jnp.mean(err, axis=-1)) if err.ndim else 0.5 * err


def _adamw(w, g, m, v):
    m = ADAM_B1 * m + (1.0 - ADAM_B1) * g
    v = ADAM_B2 * v + (1.0 - ADAM_B2) * _jnp.square(g)
    m_hat = m / (1.0 - ADAM_B1 ** ADAM_STEP)
    v_hat = v / (1.0 - ADAM_B2 ** ADAM_STEP)
    delta = -ADAM_LR * (m_hat / (_jnp.sqrt(v_hat) + ADAM_EPS) + ADAM_WD * w)
    return delta, m, v


def reference(x, p, ln0_g, ln0_b, w_in, hg_lb, hg_norm_g, fox_fb, w_a, w_b, w_o, ln1_g, ln1_b, w_ff1, w_ff2, w_pg, w_p, ln2_g, ln2_b, loss_target, m_ln0_g, m_ln0_b, m_w_in, m_hg_lb, m_hg_norm_g, m_fox_fb, m_w_a, m_w_b, m_w_o, m_ln1_g, m_ln1_b, m_w_ff1, m_w_ff2, m_w_pg, m_w_p, m_ln2_g, m_ln2_b, v_ln0_g, v_ln0_b, v_w_in, v_hg_lb, v_hg_norm_g, v_fox_fb, v_w_a, v_w_b, v_w_o, v_ln1_g, v_ln1_b, v_w_ff1, v_w_ff2, v_w_pg, v_w_p, v_ln2_g, v_ln2_b):
    given = dict(x=x, p=p, ln0_g=ln0_g, ln0_b=ln0_b, w_in=w_in, hg_lb=hg_lb, hg_norm_g=hg_norm_g, fox_fb=fox_fb, w_a=w_a, w_b=w_b, w_o=w_o, ln1_g=ln1_g, ln1_b=ln1_b, w_ff1=w_ff1, w_ff2=w_ff2, w_pg=w_pg, w_p=w_p, ln2_g=ln2_g, ln2_b=ln2_b, loss_target=loss_target, m_ln0_g=m_ln0_g, m_ln0_b=m_ln0_b, m_w_in=m_w_in, m_hg_lb=m_hg_lb, m_hg_norm_g=m_hg_norm_g, m_fox_fb=m_fox_fb, m_w_a=m_w_a, m_w_b=m_w_b, m_w_o=m_w_o, m_ln1_g=m_ln1_g, m_ln1_b=m_ln1_b, m_w_ff1=m_w_ff1, m_w_ff2=m_w_ff2, m_w_pg=m_w_pg, m_w_p=m_w_p, m_ln2_g=m_ln2_g, m_ln2_b=m_ln2_b, v_ln0_g=v_ln0_g, v_ln0_b=v_ln0_b, v_w_in=v_w_in, v_hg_lb=v_hg_lb, v_hg_norm_g=v_hg_norm_g, v_fox_fb=v_fox_fb, v_w_a=v_w_a, v_w_b=v_w_b, v_w_o=v_w_o, v_ln1_g=v_ln1_g, v_ln1_b=v_ln1_b, v_w_ff1=v_w_ff1, v_w_ff2=v_w_ff2, v_w_pg=v_w_pg, v_w_p=v_w_p, v_ln2_g=v_ln2_g, v_ln2_b=v_ln2_b)
    weights = {n: given[n] for n in TWIN_WEIGHTS}
    shared = {n: given[n] for n in SHARED_INPUTS}
    per_example = {n: given[n] for n in ['x', 'p']}
    grad_fn = _jax.value_and_grad(_loss, argnums=(0, 1))

    def one_microbatch(ex, loss_target):
        ex = dict(ex)
        diff = ex.pop(TWIN_DIFF_INPUT)
        return grad_fn(weights, diff, {**shared, **ex}, loss_target)

    if N_MICROBATCH == 1:
        loss, (grad_w, grad_x) = one_microbatch(per_example, given["loss_target"])
    else:
        def body(carry, xs):
            loss_sum, grad_sum = carry
            l_k, (gw_k, gx_k) = one_microbatch(xs[0], xs[1])
            with _jax.named_scope("update"):
                return (loss_sum + l_k, _jax.tree.map(_jnp.add, grad_sum, gw_k)), gx_k

        init = (_jnp.zeros((), _jnp.float32), _jax.tree.map(_jnp.zeros_like, weights))
        (loss, grad_w), grad_x = _jax.lax.scan(body, init, (per_example, given["loss_target"]))
    with _jax.named_scope("update"):
        delta_w, new_m, new_v = {}, {}, {}
        for n in TWIN_WEIGHTS:
            delta_w[n], new_m[n], new_v[n] = _adamw(weights[n], grad_w[n], given["m_" + n], given["v_" + n])
    return (loss, grad_x, *[grad_w[n] for n in TWIN_WEIGHTS], *[delta_w[n] for n in TWIN_WEIGHTS],
            *[new_m[n] for n in TWIN_WEIGHTS], *[new_v[n] for n in TWIN_WEIGHTS])
```

```python
import functools

import numpy as np
import jax
import jax.numpy as jnp
from jax import lax
from jax.experimental import pallas as pl
from jax.experimental.pallas import tpu as pltpu

F32 = jnp.float32
BF16 = jnp.bfloat16
MESH = pl.DeviceIdType.MESH

VMEM_LIMIT_BYTES = 48 * 1024 * 1024
LANES = 128
HG_HEADS = 4
HG_DIM = 128
HG_BLK = 16
HG_TILE = 256
FOX_HDIM = 64
FOX_AUG = 128
FOX_TQ = 512
LN_EPS = 1e-5
RMS_EPS = 1e-6
DEPTH = 1
ALPHA = (2.0 * DEPTH) ** 0.25
ADAM_LR, ADAM_B1, ADAM_B2, ADAM_EPS, ADAM_WD, ADAM_STEP = 0.001, 0.9, 0.999, 1e-08, 0.01, 10
NEG_INF = -1e30


def _cparams(sem):
    return pltpu.CompilerParams(dimension_semantics=sem, vmem_limit_bytes=VMEM_LIMIT_BYTES)


def _tile(n, cap):
    if n <= cap:
        return n
    best = None
    for t in range(LANES, cap + 1, LANES):
        if n % t == 0:
            best = t
    assert best is not None, (n, cap)
    return best


def matmul_nn(a, w, *, name, out_dtype=F32, epilogue=None, aux=None, tm=512):
    T, K = a.shape
    K2, N = w.shape
    assert K == K2 and T % tm == 0
    tn = _tile(N, 1152)
    tk = _tile(K, 1152)
    nk = K // tk

    def body(*refs):
        if aux is None:
            a_ref, w_ref, o_ref, acc_ref = refs
            x_ref = None
        else:
            a_ref, w_ref, x_ref, o_ref, acc_ref = refs
        k = pl.program_id(2)
        part = jnp.dot(a_ref[...], w_ref[...], preferred_element_type=F32)

        def write(res):
            if epilogue is not None:
                res = epilogue(res) if x_ref is None else epilogue(res, x_ref[...])
            o_ref[...] = res.astype(out_dtype)

        if nk == 1:
            write(part)
        else:
            @pl.when(k == 0)
            def _():
                acc_ref[...] = part

            @pl.when(k > 0)
            def _():
                acc_ref[...] += part

            @pl.when(k == nk - 1)
            def _():
                write(acc_ref[...])

    in_specs = [pl.BlockSpec((tm, tk), lambda n, m, k: (m, k)),
                pl.BlockSpec((tk, tn), lambda n, m, k: (k, n))]
    args = [a, w]
    if aux is not None:
        in_specs.append(pl.BlockSpec((tm, tn), lambda n, m, k: (m, n)))
        args.append(aux)
    return pl.pallas_call(
        body, name=name,
        grid=(N // tn, T // tm, nk),
        in_specs=in_specs,
        out_specs=pl.BlockSpec((tm, tn), lambda n, m, k: (m, n)),
        out_shape=jax.ShapeDtypeStruct((T, N), out_dtype),
        scratch_shapes=[pltpu.VMEM((tm, tn) if nk > 1 else (8, LANES), F32)],
        compiler_params=_cparams(("parallel", "parallel", "arbitrary")),
    )(*args)


def matmul_tn(a, b, *, name, tk=512):
    T, M = a.shape
    T2, N = b.shape
    assert T == T2 and T % tk == 0
    tm = _tile(M, 1024)
    tn = _tile(N, 1152)

    def body(a_ref, b_ref, o_ref):
        k = pl.program_id(2)
        part = lax.dot_general(a_ref[...], b_ref[...], (((0,), (0,)), ((), ())), preferred_element_type=F32)

        @pl.when(k == 0)
        def _():
            o_ref[...] = part

        @pl.when(k > 0)
        def _():
            o_ref[...] += part

    return pl.pallas_call(
        body, name=name,
        grid=(M // tm, N // tn, T // tk),
        in_specs=[pl.BlockSpec((tk, tm), lambda m, n, k: (k, m)),
                  pl.BlockSpec((tk, tn), lambda m, n, k: (k, n))],
        out_specs=pl.BlockSpec((tm, tn), lambda m, n, k: (m, n)),
        out_shape=jax.ShapeDtypeStruct((M, N), F32),
        compiler_params=_cparams(("parallel", "parallel", "arbitrary")),
    )(a, b)


def rowwise(fn, rows, vecs, outs, accs=(), *, name, tm=512):
    rows = [r if isinstance(r, tuple) else (r, 0, r.shape[1]) for r in rows]
    T = rows[0][0].shape[0]
    tm = min(tm, T)
    assert T % tm == 0
    n_rows, n_vecs, n_outs, n_accs = len(rows), len(vecs), len(outs), len(accs)

    def body(*refs):
        row_refs = refs[:n_rows]
        vec_refs = refs[n_rows:n_rows + n_vecs]
        out_refs = refs[n_rows + n_vecs:n_rows + n_vecs + n_outs]
        acc_refs = refs[n_rows + n_vecs + n_outs:]
        out_vals, acc_vals = fn([r[...] for r in row_refs], [v[...] for v in vec_refs])
        assert len(out_vals) == n_outs and len(acc_vals) == n_accs
        for r, val in zip(out_refs, out_vals):
            r[...] = val.astype(r.dtype)
        if n_accs:
            i = pl.program_id(0)

            @pl.when(i == 0)
            def _():
                for r in acc_refs:
                    r[...] = jnp.zeros_like(r)

            for r, val in zip(acc_refs, acc_vals):
                r[...] += val

    in_specs = []
    for arr, off, width in rows:
        assert off % width == 0
        in_specs.append(pl.BlockSpec((tm, width), functools.partial(lambda i, blk: (i, blk), blk=off // width)))
    for v in vecs:
        in_specs.append(pl.BlockSpec(v.shape, lambda i: (0, 0)))
    out_specs = [pl.BlockSpec((tm, w), lambda i: (i, 0)) for w, _ in outs]
    out_specs += [pl.BlockSpec((1, w), lambda i: (0, 0)) for w in accs]
    out_shape = [jax.ShapeDtypeStruct((T, w), dt) for w, dt in outs]
    out_shape += [jax.ShapeDtypeStruct((1, w), F32) for w in accs]
    res = pl.pallas_call(
        body, name=name,
        grid=(T // tm,),
        in_specs=in_specs, out_specs=out_specs, out_shape=out_shape,
        compiler_params=_cparams(("arbitrary",) if n_accs else ("parallel",)),
    )(*[r[0] for r in rows], *vecs)
    return res[:n_outs], res[n_outs:]


def _colsum(x):
    return jnp.sum(x, axis=0, keepdims=True)


def _sigmoid(x):
    return 1.0 / (1.0 + jnp.exp(-x))


def _ln_stats(z):
    mu = jnp.mean(z, axis=-1, keepdims=True)
    zc = z - mu
    var = jnp.mean(zc * zc, axis=-1, keepdims=True)
    return zc * lax.rsqrt(var + LN_EPS)


def _ln_bwd(zhat_src, dy, g):
    mu = jnp.mean(zhat_src, axis=-1, keepdims=True)
    zc = zhat_src - mu
    var = jnp.mean(zc * zc, axis=-1, keepdims=True)
    rstd = lax.rsqrt(var + LN_EPS)
    zh = zc * rstd
    dzh = dy * g
    dz = rstd * (dzh - jnp.mean(dzh, axis=-1, keepdims=True) - zh * jnp.mean(dzh * zh, axis=-1, keepdims=True))
    return dz, _colsum(dy * zh), _colsum(dy)


def _hg_constants():
    r = np.arange(HG_TILE)
    same = (r[:, None] // HG_BLK) == (r[None, :] // HG_BLK)
    lower = (same & (r[None, :] <= r[:, None])).astype(np.float32)
    upper = (same & (r[None, :] >= r[:, None])).astype(np.float32)
    total = same.astype(np.float32)
    w = HG_HEADS * HG_DIM
    c = np.arange(w)
    bd = ((c[:, None] // HG_DIM) == (c[None, :] // HG_DIM)).astype(np.float32)
    n = HG_BLK * HG_BLK
    rr = np.arange(n)
    sel_t = (rr[None, :] // HG_BLK == np.arange(HG_BLK)[:, None]).astype(np.float32)
    sel_s = (rr[None, :] % HG_BLK == np.arange(HG_BLK)[:, None]).astype(np.float32)
    as_bf = lambda m: jnp.asarray(m, dtype=BF16)
    return as_bf(lower), as_bf(upper), as_bf(total), as_bf(bd), as_bf(sel_t), as_bf(sel_s)


def _keep_bf16_bits(x):
    bits = lax.bitcast_convert_type(x, jnp.int32) & jnp.int32(-65536)
    return lax.bitcast_convert_type(bits, F32)


def _split3(x):
    hi = _keep_bf16_bits(x)
    r1 = x - hi
    mid = _keep_bf16_bits(r1)
    lo = _keep_bf16_bits(r1 - mid)
    return hi.astype(BF16), mid.astype(BF16), lo.astype(BF16)


def _dot3(m01, x):
    hi, mid, lo = _split3(x)
    d = lambda p: jnp.dot(m01, p, preferred_element_type=F32)
    return (d(lo) + d(mid)) + d(hi)


def _hg_prologue(hq, hf, lb, lower, total):
    sq = _sigmoid(hq)
    q = hq * sq
    sg = _sigmoid(hf)
    f = lb + (1.0 - lb) * sg
    g = jnp.log(f)
    k = 1.0 - f
    b = _dot3(lower, g)
    bl = _dot3(total, g)
    return q, k, f, sg, sq, b, bl


def _stack16(fn):
    return [fn(t) for t in range(HG_BLK)]


def hgrn2_fwd(proj, offs, lb, n_seq, seq, *, name):
    T = n_seq * seq
    W = HG_HEADS * HG_DIM
    n_tiles = seq // HG_TILE
    nb = HG_TILE // HG_BLK
    lower, _, total, bd, sel_t, _ = _hg_constants()

    def body(hq_ref, hf_ref, hi_ref, lb_ref, lower_ref, total_ref, bd_ref, selt_ref,
             o_ref, st_out_ref,
             st_ref, q_s, k_s, v_s, b_s, qt_s, kt_s, d_s, p_s):
        @pl.when(pl.program_id(1) == 0)
        def _():
            st_ref[...] = jnp.zeros_like(st_ref)

        q, k, _, _, _, b, bl = _hg_prologue(hq_ref[...], hf_ref[...], lb_ref[...], lower_ref[...], total_ref[...])
        q_s[...] = q
        k_s[...] = k
        v_s[...] = hi_ref[...]
        b_s[...] = b
        qt_s[...] = q * jnp.exp(b)
        kt_s[...] = k * jnp.exp(jnp.minimum(bl - b, 0.0))
        d_s[...] = jnp.exp(bl)
        rowi = lax.broadcasted_iota(jnp.int32, (HG_BLK, W), 0)

        def block(i, carry):
            r0 = pl.multiple_of(i * HG_BLK, HG_BLK)
            rows = pl.ds(r0, HG_BLK)
            qi, ki, vi, bi = q_s[rows, :], k_s[rows, :], v_s[rows, :], b_s[rows, :]
            for t in range(HG_BLK):
                e = jnp.where(rowi <= t, jnp.exp(jnp.minimum(bi[t:t + 1, :] - bi, 0.0)), 0.0)
                p_s[pl.ds(t * HG_BLK, HG_BLK), :] = (e * qi[t:t + 1, :] * ki).astype(BF16)
            a_b = jnp.dot(p_s[...], bd_ref[...], preferred_element_type=F32)
            vt = jnp.concatenate([vi] * HG_BLK, axis=0)
            o_blk = jnp.dot(selt_ref[...], (a_b * vt).astype(BF16), preferred_element_type=F32)
            qti, kti, di = qt_s[rows, :], kt_s[rows, :], d_s[rows, :]
            outs = []
            for h in range(HG_HEADS):
                hs = slice(h * HG_DIM, (h + 1) * HG_DIM)
                st_h = st_ref[hs, :]
                st_out_ref[i, hs, :] = st_h
                outs.append(lax.dot_general(qti[:, hs].astype(BF16), st_h.astype(BF16),
                                            (((1,), (1,)), ((), ())), preferred_element_type=F32))
                upd = lax.dot_general(vi[:, hs].astype(BF16), kti[:, hs].astype(BF16),
                                      (((0,), (0,)), ((), ())), preferred_element_type=F32)
                st_ref[hs, :] = st_h * di[0:1, hs] + upd
            o_ref[rows, :] = o_blk + jnp.concatenate(outs, axis=1)
            return carry

        lax.fori_loop(0, nb, block, 0)

    col = lambda off: functools.partial(lambda s, t, blk: (s * n_tiles + t, blk), blk=off // W)
    const = lambda m: pl.BlockSpec(m.shape, lambda s, t: (0, 0))
    tile_f32 = pltpu.VMEM((HG_TILE, W), F32)
    o, states = pl.pallas_call(
        body, name=name,
        grid=(n_seq, n_tiles),
        in_specs=[pl.BlockSpec((HG_TILE, W), col(offs[0])), pl.BlockSpec((HG_TILE, W), col(offs[1])),
                  pl.BlockSpec((HG_TILE, W), col(offs[2])), const(lb), const(lower), const(total), const(bd),
                  const(sel_t)],
        out_specs=[pl.BlockSpec((HG_TILE, W), lambda s, t: (s * n_tiles + t, 0)),
                   pl.BlockSpec((nb, W, HG_DIM), lambda s, t: (s * n_tiles + t, 0, 0))],
        out_shape=[jax.ShapeDtypeStruct((T, W), F32), jax.ShapeDtypeStruct((T // HG_BLK, W, HG_DIM), F32)],
        scratch_shapes=[pltpu.VMEM((W, HG_DIM), F32)] + [tile_f32] * 7
                       + [pltpu.VMEM((HG_BLK * HG_BLK, W), BF16)],
        compiler_params=_cparams(("arbitrary", "arbitrary")),
    )(proj, proj, proj, lb, lower, total, bd, sel_t)
    return o, states


def hgrn2_bwd(proj, offs, lb, do, states, n_seq, seq, *, name):
    T = n_seq * seq
    W = HG_HEADS * HG_DIM
    n_tiles = seq // HG_TILE
    nb = HG_TILE // HG_BLK
    lower, upper, total, bd, sel_t, sel_s = _hg_constants()

    def body(hq_ref, hf_ref, hi_ref, do_ref, st_in_ref, lb_ref, lower_ref, upper_ref, total_ref, bd_ref,
             selt_ref, sels_ref,
             dhq_ref, dhf_ref, dhi_ref, dlb_ref,
             dst_ref, q_s, k_s, v_s, b_s, qt_s, kt_s, d_s, eb_s, ekb_s, dq_s, dk_s, db_s, dv_s,
             p_s, e_s, w_s):
        first = jnp.logical_and(pl.program_id(0) == 0, pl.program_id(1) == 0)

        @pl.when(first)
        def _():
            dlb_ref[...] = jnp.zeros_like(dlb_ref)

        @pl.when(pl.program_id(1) == 0)
        def _():
            dst_ref[...] = jnp.zeros_like(dst_ref)

        hq, lbv = hq_ref[...], lb_ref[...]
        q, k, f, sg, sq, b, bl = _hg_prologue(hq, hf_ref[...], lbv, lower_ref[...], total_ref[...])
        eb = jnp.exp(b)
        ekb = jnp.exp(jnp.minimum(bl - b, 0.0))
        q_s[...] = q
        k_s[...] = k
        v_s[...] = hi_ref[...]
        b_s[...] = b
        eb_s[...] = eb
        ekb_s[...] = ekb
        qt_s[...] = q * eb
        kt_s[...] = k * ekb
        d_s[...] = jnp.exp(bl)
        rowi = lax.broadcasted_iota(jnp.int32, (HG_BLK, W), 0)
        last_row = rowi == HG_BLK - 1

        def block(j, carry):
            i = nb - 1 - j
            r0 = pl.multiple_of(i * HG_BLK, HG_BLK)
            rows = pl.ds(r0, HG_BLK)
            qi, ki, vi, bi, doi = q_s[rows, :], k_s[rows, :], v_s[rows, :], b_s[rows, :], do_ref[rows, :]
            for t in range(HG_BLK):
                sl = pl.ds(t * HG_BLK, HG_BLK)
                e = jnp.where(rowi <= t, jnp.exp(jnp.minimum(bi[t:t + 1, :] - bi, 0.0)), 0.0)
                e_s[sl, :] = e
                p_s[sl, :] = (e * qi[t:t + 1, :] * ki).astype(BF16)
                w_s[sl, :] = (doi[t:t + 1, :] * vi).astype(BF16)
            a_b = jnp.dot(p_s[...], bd_ref[...], preferred_element_type=F32)
            da_b = jnp.dot(w_s[...], bd_ref[...], preferred_element_type=F32)
            x = da_b * e_s[...]
            k_til = jnp.concatenate([ki] * HG_BLK, axis=0)
            q_rep = jnp.concatenate([jnp.broadcast_to(qi[t:t + 1, :], (HG_BLK, W)) for t in range(HG_BLK)], axis=0)
            do_rep = jnp.concatenate([jnp.broadcast_to(doi[t:t + 1, :], (HG_BLK, W)) for t in range(HG_BLK)], axis=0)
            dq_in = jnp.dot(selt_ref[...], (x * k_til).astype(BF16), preferred_element_type=F32)
            dk_in = jnp.dot(sels_ref[...], (x * q_rep).astype(BF16), preferred_element_type=F32)
            dv_in = jnp.dot(sels_ref[...], (a_b * do_rep).astype(BF16), preferred_element_type=F32)
            qti, kti, di = qt_s[rows, :], kt_s[rows, :], d_s[rows, :]
            dqt, dkt, dvt, dd = [], [], [], []
            for h in range(HG_HEADS):
                hs = slice(h * HG_DIM, (h + 1) * HG_DIM)
                st_h = st_in_ref[i, hs, :]
                dst_h = dst_ref[hs, :]
                do_h, v_h = doi[:, hs].astype(BF16), vi[:, hs].astype(BF16)
                dst_b = dst_h.astype(BF16)
                dqt.append(jnp.dot(do_h, st_h.astype(BF16), preferred_element_type=F32))
                dkt.append(jnp.dot(v_h, dst_b, preferred_element_type=F32))
                dvt.append(lax.dot_general(kti[:, hs].astype(BF16), dst_b, (((1,), (1,)), ((), ())),
                                           preferred_element_type=F32))
                dd.append(jnp.sum(dst_h * st_h, axis=0, keepdims=True))
                upd = lax.dot_general(do_h, qti[:, hs].astype(BF16), (((0,), (0,)), ((), ())),
                                      preferred_element_type=F32)
                dst_ref[hs, :] = dst_h * di[0:1, hs] + upd
            dqt = jnp.concatenate(dqt, axis=1)
            dkt = jnp.concatenate(dkt, axis=1)
            dvt = jnp.concatenate(dvt, axis=1)
            dd = jnp.concatenate(dd, axis=1)
            dbl = jnp.sum(dkt * kti, axis=0, keepdims=True) + dd * di[0:1, :]
            db = qi * dq_in - ki * dk_in + dqt * qti - dkt * kti
            db_s[rows, :] = db + jnp.where(last_row, dbl, 0.0)
            dq_s[rows, :] = dq_in + dqt * eb_s[rows, :]
            dk_s[rows, :] = dk_in + dkt * ekb_s[rows, :]
            dv_s[rows, :] = dv_in + dvt
            return carry

        lax.fori_loop(0, nb, block, 0)

        dg = _dot3(upper_ref[...], db_s[...])
        dhq_ref[...] = (dq_s[...] * (sq * (1.0 + hq * (1.0 - sq)))).astype(dhq_ref.dtype)
        df = dg / f - dk_s[...]
        dhf_ref[...] = (df * (1.0 - lbv) * (sg * (1.0 - sg))).astype(dhf_ref.dtype)
        dhi_ref[...] = dv_s[...].astype(dhi_ref.dtype)
        dlb_ref[...] += _colsum(df * (1.0 - sg))

    rev = lambda s, t: s * n_tiles + (n_tiles - 1 - t)
    col = lambda off: functools.partial(lambda s, t, blk: (rev(s, t), blk), blk=off // W)
    const = lambda m: pl.BlockSpec(m.shape, lambda s, t: (0, 0))
    row = pl.BlockSpec((HG_TILE, W), lambda s, t: (rev(s, t), 0))
    tile_f32 = pltpu.VMEM((HG_TILE, W), F32)
    n2 = HG_BLK * HG_BLK
    return pl.pallas_call(
        body, name=name,
        grid=(n_seq, n_tiles),
        in_specs=[pl.BlockSpec((HG_TILE, W), col(offs[0])), pl.BlockSpec((HG_TILE, W), col(offs[1])),
                  pl.BlockSpec((HG_TILE, W), col(offs[2])), row,
                  pl.BlockSpec((nb, W, HG_DIM), lambda s, t: (rev(s, t), 0, 0)),
                  const(lb), const(lower), const(upper), const(total), const(bd), const(sel_t), const(sel_s)],
        out_specs=[row, row, row, pl.BlockSpec((1, W), lambda s, t: (0, 0))],
        out_shape=[jax.ShapeDtypeStruct((T, W), BF16)] * 3 + [jax.ShapeDtypeStruct((1, W), F32)],
        scratch_shapes=[pltpu.VMEM((W, HG_DIM), F32)] + [tile_f32] * 13
                       + [pltpu.VMEM((n2, W), BF16), pltpu.VMEM((n2, W), F32), pltpu.VMEM((n2, W), BF16)],
        compiler_params=_cparams(("arbitrary", "arbitrary")),
    )(proj, proj, proj, do, states, lb, lower, upper, total, bd, sel_t, sel_s)


def _causal_mask(i, j, tq):
    r = lax.broadcasted_iota(jnp.int32, (tq, tq), 0) + i * tq
    c = lax.broadcasted_iota(jnp.int32, (tq, tq), 1) + j * tq
    return c <= r


def _qk(q, k):
    return lax.dot_general(q, k, (((1,), (1,)), ((), ())), preferred_element_type=F32)


def fox_fwd(qa, ka, v, *, name):
    BH, S, _ = qa.shape
    tq = min(FOX_TQ, S)
    n = S // tq

    def body(q_ref, k_ref, v_ref, o_ref, lse_ref, m_s, l_s, acc_s):
        i, j = pl.program_id(1), pl.program_id(2)

        @pl.when(j == 0)
        def _():
            m_s[...] = jnp.full_like(m_s, NEG_INF)
            l_s[...] = jnp.zeros_like(l_s)
            acc_s[...] = jnp.zeros_like(acc_s)

        @pl.when(j <= i)
        def _():
            s = jnp.where(_causal_mask(i, j, tq), _qk(q_ref[...], k_ref[...]), NEG_INF)
            m_prev = m_s[...]
            m_new = jnp.maximum(m_prev, jnp.max(s, axis=-1, keepdims=True))
            alpha = jnp.exp(m_prev - m_new)
            p = jnp.exp(s - m_new[:, 0:1])
            l_s[...] = alpha * l_s[...] + jnp.sum(p, axis=-1, keepdims=True)
            acc_s[...] = alpha[:, 0:FOX_HDIM] * acc_s[...] + jnp.dot(p.astype(BF16), v_ref[...],
                                                                     preferred_element_type=F32)
            m_s[...] = m_new

        @pl.when(j == i)
        def _():
            o_ref[...] = acc_s[...] / l_s[:, 0:FOX_HDIM]
            lse_ref[...] = m_s[...] + jnp.log(l_s[...])

    qspec = lambda d: pl.BlockSpec((None, tq, d), lambda b, i, j: (b, i, 0))
    kspec = lambda d: pl.BlockSpec((None, tq, d), lambda b, i, j: (b, jnp.minimum(j, i), 0))
    return pl.pallas_call(
        body, name=name,
        grid=(BH, n, n),
        in_specs=[qspec(FOX_AUG), kspec(FOX_AUG), kspec(FOX_HDIM)],
        out_specs=[qspec(FOX_HDIM), qspec(LANES)],
        out_shape=[jax.ShapeDtypeStruct((BH, S, FOX_HDIM), F32), jax.ShapeDtypeStruct((BH, S, LANES), F32)],
        scratch_shapes=[pltpu.VMEM((tq, LANES), F32), pltpu.VMEM((tq, LANES), F32),
                        pltpu.VMEM((tq, FOX_HDIM), F32)],
        compiler_params=_cparams(("parallel", "parallel", "arbitrary")),
    )(qa, ka, v)


def _fox_p_dp(q, k, v, do, lse, i, j, tq):
    s = jnp.where(_causal_mask(i, j, tq), _qk(q, k), NEG_INF)
    return jnp.exp(s - lse[:, 0:1]), _qk(do, v)


def fox_bwd_dq(qa, ka, v, do, lse, *, name):
    BH, S, _ = qa.shape
    tq = min(FOX_TQ, S)
    n = S // tq

    def body(q_ref, k_ref, v_ref, do_ref, lse_ref, dq_ref, rsum_ref, delta_ref):
        i, jj = pl.program_id(1), pl.program_id(2)

        @pl.when(jj == 0)
        def _():
            dq_ref[...] = jnp.zeros_like(dq_ref)
            rsum_ref[...] = jnp.zeros_like(rsum_ref)
            delta_ref[...] = jnp.zeros_like(delta_ref)

        @pl.when(jj <= i)
        def _():
            p, dp = _fox_p_dp(q_ref[...], k_ref[...], v_ref[...], do_ref[...], lse_ref[...], i, jj, tq)
            delta_ref[...] += jnp.sum(p * dp, axis=-1, keepdims=True)

        @pl.when(jnp.logical_and(jj >= n, jj - n <= i))
        def _():
            p, dp = _fox_p_dp(q_ref[...], k_ref[...], v_ref[...], do_ref[...], lse_ref[...], i, jj - n, tq)
            ds = p * (dp - delta_ref[:, 0:1])
            dq_ref[...] += jnp.dot(ds.astype(BF16), k_ref[...], preferred_element_type=F32)
            rsum_ref[...] += jnp.sum(ds, axis=-1, keepdims=True)

    qspec = lambda d: pl.BlockSpec((None, tq, d), lambda b, i, jj: (b, i, 0))
    kspec = lambda d: pl.BlockSpec((None, tq, d), lambda b, i, jj: (b, jnp.minimum(jnp.where(jj < n, jj, jj - n), i), 0))
    return pl.pallas_call(
        body, name=name,
        grid=(BH, n, 2 * n),
        in_specs=[qspec(FOX_AUG), kspec(FOX_AUG), kspec(FOX_HDIM), qspec(FOX_HDIM), qspec(LANES)],
        out_specs=[qspec(FOX_AUG), qspec(LANES), qspec(LANES)],
        out_shape=[jax.ShapeDtypeStruct((BH, S, FOX_AUG), F32), jax.ShapeDtypeStruct((BH, S, LANES), F32),
                   jax.ShapeDtypeStruct((BH, S, LANES), F32)],
        compiler_params=_cparams(("parallel", "parallel", "arbitrary")),
    )(qa, ka, v, do, lse)


def fox_bwd_dkv(qa, ka, v, do, delta, lse, *, name):
    BH, S, _ = qa.shape
    tq = min(FOX_TQ, S)
    n = S // tq

    def body(q_ref, k_ref, v_ref, do_ref, delta_ref, lse_ref, dk_ref, dv_ref, dsum_ref):
        j, i = pl.program_id(1), pl.program_id(2)

        @pl.when(i == 0)
        def _():
            dk_ref[...] = jnp.zeros_like(dk_ref)
            dv_ref[...] = jnp.zeros_like(dv_ref)
            dsum_ref[...] = jnp.zeros_like(dsum_ref)

        @pl.when(i >= j)
        def _():
            q, do = q_ref[...], do_ref[...]
            p, dp = _fox_p_dp(q, k_ref[...], v_ref[...], do, lse_ref[...], i, j, tq)
            ds = p * (dp - delta_ref[:, 0:1])
            tn = (((0,), (0,)), ((), ()))
            dv_ref[...] += lax.dot_general(p.astype(BF16), do, tn, preferred_element_type=F32)
            dk_ref[...] += lax.dot_general(ds.astype(BF16), q, tn, preferred_element_type=F32)
            dsum_ref[...] += _colsum(ds)

    qspec = lambda d: pl.BlockSpec((None, tq, d), lambda b, j, i: (b, jnp.maximum(i, j), 0))
    kspec = lambda d: pl.BlockSpec((None, tq, d), lambda b, j, i: (b, j, 0))
    return pl.pallas_call(
        body, name=name,
        grid=(BH, n, n),
        in_specs=[qspec(FOX_AUG), kspec(FOX_AUG), kspec(FOX_HDIM), qspec(FOX_HDIM), qspec(LANES), qspec(LANES)],
        out_specs=[kspec(FOX_AUG), kspec(FOX_HDIM), pl.BlockSpec((None, 1, tq), lambda b, j, i: (b, 0, j))],
        out_shape=[jax.ShapeDtypeStruct((BH, S, FOX_AUG), F32), jax.ShapeDtypeStruct((BH, S, FOX_HDIM), F32),
                   jax.ShapeDtypeStruct((BH, 1, S), F32)],
        compiler_params=_cparams(("parallel", "parallel", "arbitrary")),
    )(qa, ka, v, do, delta, lse)


def seq_cumsum(x, n_seq, seq, *, reverse, name):
    T, C = x.shape
    tb = min(256, seq)
    n = seq // tb
    r = np.arange(tb)
    tri = (r[None, :] >= r[:, None]) if reverse else (r[None, :] <= r[:, None])
    tri = jnp.asarray(tri.astype(np.float32), dtype=BF16)

    def body(x_ref, tri_ref, o_ref, carry_s):
        @pl.when(pl.program_id(1) == 0)
        def _():
            carry_s[...] = jnp.zeros_like(carry_s)

        xv = x_ref[...]
        o_ref[...] = _dot3(tri_ref[...], xv) + carry_s[...]
        carry_s[...] += _colsum(xv)

    blk = (lambda s, t: (s * n + (n - 1 - t), 0)) if reverse else (lambda s, t: (s * n + t, 0))
    return pl.pallas_call(
        body, name=name,
        grid=(n_seq, n),
        in_specs=[pl.BlockSpec((tb, C), blk), pl.BlockSpec((tb, tb), lambda s, t: (0, 0))],
        out_specs=pl.BlockSpec((tb, C), blk),
        out_shape=jax.ShapeDtypeStruct((T, C), F32),
        scratch_shapes=[pltpu.VMEM((1, C), F32)],
        compiler_params=_cparams(("arbitrary", "arbitrary")),
    )(x, tri)


def _place():
    return lax.axis_index("x"), lax.axis_index("y"), lax.axis_index("c")


def _other_chips(x, y):
    return [(1 - x, y), (x, 1 - y), (1 - x, 1 - y)]


def allgather_chips(shard, *, name):
    R, C = shard.shape

    def body(x_ref, o_ref, send_sems, recv_sems, local_sem):
        x, y, c = _place()
        me = 2 * x + y
        own = pltpu.make_async_copy(x_ref, o_ref.at[me], local_sem)
        own.start()
        sends = []
        for j, (px, py) in enumerate(_other_chips(x, y)):
            cp = pltpu.make_async_remote_copy(src_ref=x_ref, dst_ref=o_ref.at[me], send_sem=send_sems.at[j],
                                              recv_sem=recv_sems.at[j], device_id=(px, py, c), device_id_type=MESH)
            cp.start()
            sends.append(cp)
        for j, (px, py) in enumerate(_other_chips(x, y)):
            pltpu.make_async_remote_copy(src_ref=x_ref, dst_ref=o_ref.at[2 * px + py], send_sem=send_sems.at[j],
                                         recv_sem=recv_sems.at[j], device_id=(px, py, c),
                                         device_id_type=MESH).wait_recv()
        for cp in sends:
            cp.wait_send()
        own.wait()

    return pl.pallas_call(
        body, name=name,
        in_specs=[pl.BlockSpec(memory_space=pl.ANY)],
        out_specs=pl.BlockSpec(memory_space=pl.ANY),
        out_shape=jax.ShapeDtypeStruct((4, R, C), shard.dtype),
        scratch_shapes=[pltpu.SemaphoreType.DMA((3,)), pltpu.SemaphoreType.DMA((3,)), pltpu.SemaphoreType.DMA(())],
        compiler_params=pltpu.CompilerParams(has_side_effects=True),
    )(shard)


def scatter_chips(parts, *, name):
    _, R, C = parts.shape

    def body(x_ref, o_ref, send_sems, recv_sems):
        x, y, c = _place()
        sends = []
        for j, (px, py) in enumerate(_other_chips(x, y)):
            cp = pltpu.make_async_remote_copy(src_ref=x_ref.at[2 * px + py], dst_ref=o_ref.at[j],
                                              send_sem=send_sems.at[j], recv_sem=recv_sems.at[j],
                                              device_id=(px, py, c), device_id_type=MESH)
            cp.start()
            sends.append(cp)
        for cp in sends:
            cp.wait_recv()
        for cp in sends:
            cp.wait_send()

    return pl.pallas_call(
        body, name=name,
        in_specs=[pl.BlockSpec(memory_space=pl.ANY)],
        out_specs=pl.BlockSpec(memory_space=pl.ANY),
        out_shape=jax.ShapeDtypeStruct((3, R, C), parts.dtype),
        scratch_shapes=[pltpu.SemaphoreType.DMA((3,)), pltpu.SemaphoreType.DMA((3,))],
        compiler_params=pltpu.CompilerParams(has_side_effects=True),
    )(parts)


def swap_cores(v, *, name):
    def body(x_ref, o_ref, send_sem, recv_sem):
        x, y, c = _place()
        cp = pltpu.make_async_remote_copy(src_ref=x_ref, dst_ref=o_ref, send_sem=send_sem, recv_sem=recv_sem,
                                          device_id=(x, y, 1 - c), device_id_type=MESH)
        cp.start()
        cp.wait()

    return pl.pallas_call(
        body, name=name,
        in_specs=[pl.BlockSpec(memory_space=pl.ANY)],
        out_specs=pl.BlockSpec(memory_space=pl.ANY),
        out_shape=jax.ShapeDtypeStruct(v.shape, v.dtype),
        scratch_shapes=[pltpu.SemaphoreType.DMA(()), pltpu.SemaphoreType.DMA(())],
        compiler_params=pltpu.CompilerParams(has_side_effects=True),
    )(v)


def allreduce_small(v, *, name):
    R, C = v.shape

    def body(x_ref, o_ref, gath_ref, send_sems, recv_sems):
        x, y, c = _place()
        me = 4 * x + 2 * y + c
        gath_ref[me] = x_ref[...]
        flips = [(k >> 2 & 1, k >> 1 & 1, k & 1) for k in range(1, 8)]
        sends = []
        for j, (fx, fy, fc) in enumerate(flips):
            peer = (x ^ fx, y ^ fy, c ^ fc)
            cp = pltpu.make_async_remote_copy(src_ref=x_ref, dst_ref=gath_ref.at[me], send_sem=send_sems.at[j],
                                              recv_sem=recv_sems.at[j], device_id=peer, device_id_type=MESH)
            cp.start()
            sends.append(cp)
        for j, (fx, fy, fc) in enumerate(flips):
            peer = (x ^ fx, y ^ fy, c ^ fc)
            pltpu.make_async_remote_copy(src_ref=x_ref, dst_ref=gath_ref.at[4 * peer[0] + 2 * peer[1] + peer[2]],
                                         send_sem=send_sems.at[j], recv_sem=recv_sems.at[j], device_id=peer,
                                         device_id_type=MESH).wait_recv()
        for cp in sends:
            cp.wait_send()
        total = gath_ref[0]
        for d in range(1, 8):
            total = total + gath_ref[d]
        o_ref[...] = total

    vm = pl.BlockSpec(memory_space=pltpu.VMEM)
    out, _ = pl.pallas_call(
        body, name=name,
        in_specs=[vm], out_specs=[vm, vm],
        out_shape=[jax.ShapeDtypeStruct((R, C), F32), jax.ShapeDtypeStruct((8, R, C), F32)],
        scratch_shapes=[pltpu.SemaphoreType.DMA((7,)), pltpu.SemaphoreType.DMA((7,))],
        compiler_params=pltpu.CompilerParams(has_side_effects=True),
    )(v)
    return out


PACK_W = 1024
ROW_ALIGN = 16
PACK_TILE = 256
BIG_WEIGHTS = (("w_in", 1), ("w_a", 1), ("w_b", 1), ("w_o", 0), ("w_ff1", 1), ("w_ff2", 0), ("w_pg", 0), ("w_p", 1))


def _pack_rows(shape):
    n = shape[0] * shape[1]
    assert n % PACK_W == 0
    rows = n // PACK_W
    return rows, -(-rows // ROW_ALIGN) * ROW_ALIGN


def pack_shards(shards):
    parts = []
    for s in shards:
        rows, padded = _pack_rows(s.shape)
        m = s.reshape(rows, PACK_W)
        if padded != rows:
            m = jnp.concatenate([m, jnp.zeros((padded - rows, PACK_W), m.dtype)], axis=0)
        parts.append(m)
    total = sum(q.shape[0] for q in parts)
    tail = -total % PACK_TILE
    if tail:
        parts.append(jnp.zeros((tail, PACK_W), parts[0].dtype))
    return jnp.concatenate(parts, axis=0)


def unpack_shards(buf, shapes):
    out, off = [], 0
    for shp in shapes:
        rows, padded = _pack_rows(shp)
        out.append(buf[off:off + rows].reshape(shp))
        off += padded
    return out


def _win_layout(d):
    hw = d // 2
    fh = hw // FOX_HDIM
    orig = {"hq": (0, hw), "hf": (hw, hw), "hi": (2 * hw, hw), "hg": (3 * hw, hw), "fq": (4 * hw, hw),
            "fk": (5 * hw, hw), "fv": (6 * hw, hw), "ff": (7 * hw, fh), "ga": (7 * hw + fh, d), "gb": (7 * hw + fh + d, d)}
    order = ["ga", "gb", "hq", "hf", "hi", "hg", "fq", "fk", "fv", "ff"]
    mine, off = {}, 0
    for nm in order:
        width = orig[nm][1] if nm != "ff" else LANES
        mine[nm] = (off, width)
        off += width
    return orig, order, mine, off


def _adam_fn(rows, vecs):
    w, g, m, v = rows
    m2 = ADAM_B1 * m + (1.0 - ADAM_B1) * g
    v2 = ADAM_B2 * v + (1.0 - ADAM_B2) * (g * g)
    m_hat = m2 / (1.0 - ADAM_B1 ** ADAM_STEP)
    v_hat = v2 / (1.0 - ADAM_B2 ** ADAM_STEP)
    delta = -ADAM_LR * (m_hat / (jnp.sqrt(v_hat) + ADAM_EPS) + ADAM_WD * w)
    return [delta, m2, v2], []


def adamw(w, g, m, v, *, name):
    c = w.shape[1]
    (delta, m2, v2), _ = rowwise(_adam_fn, [w, g, m, v], [], [(c, F32)] * 3, name=name, tm=256)
    return delta, m2, v2


def kernel(x, p, ln0_g, ln0_b, w_in, hg_lb, hg_norm_g, fox_fb, w_a, w_b, w_o, ln1_g, ln1_b, w_ff1, w_ff2, w_pg, w_p, ln2_g, ln2_b, loss_target, m_ln0_g, m_ln0_b, m_w_in, m_hg_lb, m_hg_norm_g, m_fox_fb, m_w_a, m_w_b, m_w_o, m_ln1_g, m_ln1_b, m_w_ff1, m_w_ff2, m_w_pg, m_w_p, m_ln2_g, m_ln2_b, v_ln0_g, v_ln0_b, v_w_in, v_hg_lb, v_hg_norm_g, v_fox_fb, v_w_a, v_w_b, v_w_o, v_ln1_g, v_ln1_b, v_w_ff1, v_w_ff2, v_w_pg, v_w_p, v_ln2_g, v_ln2_b):
    n_seq, seq, d = x.shape
    T = n_seq * seq
    hw = d // 2
    fh = hw // FOX_HDIM
    bh = n_seq * fh
    orig, order, mine, n_in = _win_layout(d)

    big = {"w_in": w_in[0], "w_a": w_a[0], "w_b": w_b[0], "w_o": w_o[0], "w_ff1": w_ff1[0], "w_ff2": w_ff2[0],
           "w_pg": w_pg[0], "w_p": w_p[0]}
    big_m = {"w_in": m_w_in[0], "w_a": m_w_a[0], "w_b": m_w_b[0], "w_o": m_w_o[0], "w_ff1": m_w_ff1[0],
             "w_ff2": m_w_ff2[0], "w_pg": m_w_pg[0], "w_p": m_w_p[0]}
    big_v = {"w_in": v_w_in[0], "w_a": v_w_a[0], "w_b": v_w_b[0], "w_o": v_w_o[0], "w_ff1": v_w_ff1[0],
             "w_ff2": v_w_ff2[0], "w_pg": v_w_pg[0], "w_p": v_w_p[0]}
    names = [nm for nm, _ in BIG_WEIGHTS]
    axis = dict(BIG_WEIGHTS)
    shard_shapes = [big[nm].shape for nm in names]

    gathered = allgather_chips(pack_shards([big[nm].astype(BF16) for nm in names]), name="allgather_weights")
    per_chip = [unpack_shards(gathered[s], shard_shapes) for s in range(4)]
    full = {nm: jnp.concatenate([per_chip[s][k] for s in range(4)], axis=axis[nm]) for k, nm in enumerate(names)}
    win = full["w_in"]
    win_mine = jnp.concatenate(
        [win[:, orig[nm][0]:orig[nm][0] + orig[nm][1]] for nm in order]
        + [jnp.zeros((d, LANES - fh), BF16)], axis=1)

    x2 = x.reshape(T, d)
    tgt = loss_target.reshape(T, d)
    p_b = p.reshape(T, p.shape[-1]).astype(BF16)
    vec = lambda a: a.reshape(1, -1)
    probs = jax.nn.softmax(hg_lb, axis=0)
    lb = vec(probs[0])

    def ln0_fn(rows, vecs):
        h = _ln_stats(rows[0]) * vecs[0] + vecs[1]
        return [h, h], []
    (h0, h0b), _ = rowwise(ln0_fn, [x2], [vec(ln0_g), vec(ln0_b)], [(d, F32), (d, BF16)], name="ln0_fwd")
    proj = matmul_nn(h0b, win_mine, name="in_proj")

    o_raw, hg_states = hgrn2_fwd(proj, [mine["hq"][0], mine["hf"][0], mine["hi"][0]], lb, n_seq, seq, name="hgrn2_fwd")

    def ya_fn(rows, vecs):
        o, hg = rows
        outs = []
        for h in range(HG_HEADS):
            oh = o[:, h * HG_DIM:(h + 1) * HG_DIM]
            outs.append(oh * lax.rsqrt(jnp.mean(oh * oh, axis=-1, keepdims=True) + RMS_EPS))
        y = jnp.concatenate(outs, axis=1) * vecs[0] * (hg * _sigmoid(hg))
        return [y], []
    (y_a,), _ = rowwise(ya_fn, [o_raw, (proj,) + mine["hg"]], [hg_norm_g], [(hw, BF16)], name="hgrn2_out_fwd")

    fb_pad = jnp.concatenate([fox_fb, jnp.zeros((1, LANES - fh), F32)], axis=1)

    def lf_fn(rows, vecs):
        u = rows[0] + vecs[0]
        return [jnp.minimum(u, 0.0) - jnp.log(1.0 + jnp.exp(-jnp.abs(u)))], []
    (lf,), _ = rowwise(lf_fn, [(proj,) + mine["ff"]], [fb_pad], [(LANES, F32)], name="fox_logf")
    c_cum = seq_cumsum(lf, n_seq, seq, reverse=False, name="fox_cumsum")[:, :fh]
    c1, c2, c3 = _split3(c_cum)

    def heads(t2d, width):
        return t2d.reshape(n_seq, seq, fh, width).transpose(0, 2, 1, 3).reshape(bh, seq, width)

    def unheads(t3d, width):
        return t3d.reshape(n_seq, fh, seq, width).transpose(0, 2, 1, 3).reshape(T, fh * width)

    cs = [heads(cc, 1) for cc in (c1, c2, c3)]
    ones = jnp.ones((bh, seq, 3), BF16)
    zpad = jnp.zeros((bh, seq, FOX_AUG - FOX_HDIM - 6), BF16)
    fq = proj[:, mine["fq"][0]:mine["fq"][0] + hw].astype(BF16)
    fk = proj[:, mine["fk"][0]:mine["fk"][0] + hw].astype(BF16)
    fv = proj[:, mine["fv"][0]:mine["fv"][0] + hw].astype(BF16)
    scale = FOX_HDIM ** -0.5
    qa = jnp.concatenate([heads(fq, FOX_HDIM) * jnp.asarray(scale, BF16)] + cs + [ones, zpad], axis=-1)
    ka = jnp.concatenate([heads(fk, FOX_HDIM), ones] + [-cc for cc in cs] + [zpad], axis=-1)
    va = heads(fv, FOX_HDIM)
    o_fox, lse = fox_fwd(qa, ka, va, name="fox_fwd")
    y_b = unheads(o_fox, FOX_HDIM).astype(BF16)

    pa = matmul_nn(y_a, full["w_a"], name="proj_a")
    pb = matmul_nn(y_b, full["w_b"], name="proj_b")

    def merge_fn(rows, vecs):
        ga, gb, a, b = rows
        return [_sigmoid(ga) * a + _sigmoid(gb) * b], []
    (merged,), _ = rowwise(merge_fn, [(proj,) + mine["ga"], (proj,) + mine["gb"], pa, pb], [], [(d, BF16)],
                           name="merge_fwd")
    mix = matmul_nn(merged, full["w_o"], name="out_proj")

    def ln1_fn(rows, vecs):
        z = ALPHA * rows[0] + rows[1]
        h = _ln_stats(z) * vecs[0] + vecs[1]
        return [z, h, h], []
    (z1, h1, h1b), _ = rowwise(ln1_fn, [h0, mix], [ln1_g, ln1_b], [(d, F32), (d, F32), (d, BF16)], name="ln1_fwd")

    relu2 = lambda u: jnp.square(jnp.maximum(u, 0.0))
    act = matmul_nn(h1b, full["w_ff1"], name="ff1", out_dtype=BF16, epilogue=relu2)
    ff = matmul_nn(act, full["w_ff2"], name="ff2")
    pg = matmul_nn(h1b, full["w_pg"], name="ple_gate")
    pe = matmul_nn(p_b, full["w_p"], name="ple_embed")

    def head_fn(rows, vecs):
        h1v, ffv, pgv, pev, t = rows
        g2, b2 = vecs
        sp = _sigmoid(pgv)
        z = ALPHA * h1v + ffv + sp * pev
        y = _ln_stats(z) * g2 + b2
        err = y - t
        loss_rows = 0.5 * jnp.mean(err * err, axis=-1, keepdims=True)
        dy = err * (1.0 / d)
        dz, dg2, db2 = _ln_bwd(z, dy, g2)
        loss_acc = jnp.broadcast_to(_colsum(loss_rows), (1, LANES))
        return [dz, dz, dz * pev * (sp * (1.0 - sp)), dz * sp], [dg2, db2, loss_acc]
    (dz2, dz2b, dpg, dpe), (g_ln2_g, g_ln2_b, loss_part) = rowwise(
        head_fn, [h1, ff, pg, pe, tgt], [ln2_g, ln2_b],
        [(d, F32), (d, BF16), (d, BF16), (d, BF16)], [d, d, LANES], name="head_fwd_bwd")

    wt = {nm: full[nm].T for nm in names if nm != "w_in"}
    dact = lambda da, a: da * (2.0 * jnp.sqrt(a.astype(F32)))
    du = matmul_nn(dz2b, wt["w_ff2"], name="d_ff2", out_dtype=BF16, epilogue=dact, aux=act)
    dh1_ff = matmul_nn(du, wt["w_ff1"], name="d_ff1")
    dh1_pg = matmul_nn(dpg, wt["w_pg"], name="d_ple_gate")

    def ln1_bwd_fn(rows, vecs):
        dh1 = ALPHA * rows[0] + rows[1] + rows[2]
        dz, dg, db = _ln_bwd(rows[3], dh1, vecs[0])
        return [dz, dz], [dg, db]
    (dz1, dz1b), (g_ln1_g, g_ln1_b) = rowwise(ln1_bwd_fn, [dz2, dh1_ff, dh1_pg, z1], [ln1_g],
                                              [(d, F32), (d, BF16)], [d, d], name="ln1_bwd")
    dmerged = matmul_nn(dz1b, wt["w_o"], name="d_out_proj")

    def merge_bwd_fn(rows, vecs):
        dm, ga, gb, a, b = rows
        sa, sb = _sigmoid(ga), _sigmoid(gb)
        return [dm * a * (sa * (1.0 - sa)), dm * b * (sb * (1.0 - sb)), dm * sa, dm * sb], []
    (dga, dgb, dma, dmb), _ = rowwise(merge_bwd_fn, [dmerged, (proj,) + mine["ga"], (proj,) + mine["gb"], pa, pb], [],
                                      [(d, BF16)] * 4, name="merge_bwd")
    dya = matmul_nn(dma, wt["w_a"], name="d_proj_a")
    dyb = matmul_nn(dmb, wt["w_b"], name="d_proj_b", out_dtype=BF16)

    def ya_bwd_fn(rows, vecs):
        o, hg, dy = rows
        ng = vecs[0]
        sg = _sigmoid(hg)
        gate = hg * sg
        dn_parts, do_parts, n_parts = [], [], []
        for h in range(HG_HEADS):
            hs = slice(h * HG_DIM, (h + 1) * HG_DIM)
            oh = o[:, hs]
            r = lax.rsqrt(jnp.mean(oh * oh, axis=-1, keepdims=True) + RMS_EPS)
            nh = oh * r
            dn = dy[:, hs] * ng[:, hs] * gate[:, hs]
            do_parts.append(r * (dn - nh * jnp.mean(dn * nh, axis=-1, keepdims=True)))
            n_parts.append(nh)
        nrm = jnp.concatenate(n_parts, axis=1)
        dhg = dy * nrm * ng * (sg * (1.0 + hg * (1.0 - sg)))
        return [jnp.concatenate(do_parts, axis=1), dhg], [_colsum(dy * nrm * gate)]
    (do_raw, dhg), (g_norm_g,) = rowwise(ya_bwd_fn, [o_raw, (proj,) + mine["hg"], dya], [hg_norm_g],
                                         [(hw, F32), (hw, BF16)], [hw], name="hgrn2_out_bwd")
    dhq, dhf, dhi, g_lb = hgrn2_bwd(proj, [mine["hq"][0], mine["hf"][0], mine["hi"][0]], lb, do_raw, hg_states,
                                    n_seq, seq, name="hgrn2_bwd")

    do_fox = heads(dyb, FOX_HDIM)
    dqa, rsum, delta = fox_bwd_dq(qa, ka, va, do_fox, lse, name="fox_bwd_dq")
    dka, dva, dsum = fox_bwd_dkv(qa, ka, va, do_fox, delta, lse, name="fox_bwd_dkv")
    dfq = (unheads(dqa[:, :, :FOX_HDIM], FOX_HDIM) * scale).astype(BF16)
    dfk = unheads(dka[:, :, :FOX_HDIM], FOX_HDIM).astype(BF16)
    dfv = unheads(dva, FOX_HDIM).astype(BF16)
    dc = unheads(rsum[:, :, 0:1], 1) - dsum.reshape(n_seq, fh, seq).transpose(0, 2, 1).reshape(T, fh)
    dc = jnp.concatenate([dc, jnp.zeros((T, LANES - fh), F32)], axis=1)
    dlf = seq_cumsum(dc, n_seq, seq, reverse=True, name="fox_cumsum_bwd")

    def lf_bwd_fn(rows, vecs):
        u = rows[0] + vecs[0]
        du_ = rows[1] * _sigmoid(-u)
        return [du_], [_colsum(du_)]
    (dff_,), (g_fb,) = rowwise(lf_bwd_fn, [(proj,) + mine["ff"], dlf], [fb_pad], [(LANES, BF16)], [LANES],
                               name="fox_logf_bwd")

    dproj = jnp.concatenate([dga, dgb, dhq, dhf, dhi, dhg, dfq, dfk, dfv, dff_], axis=1)
    dh0_in = matmul_nn(dproj, win_mine.T, name="d_in_proj")

    def ln0_bwd_fn(rows, vecs):
        dh0 = rows[0] + ALPHA * rows[1]
        dx, dg, db = _ln_bwd(rows[2], dh0, vecs[0])
        return [dx], [dg, db]
    (dx,), (g_ln0_g, g_ln0_b) = rowwise(ln0_bwd_fn, [dh0_in, dz1, x2], [vec(ln0_g)], [(d, F32)], [d, d],
                                        name="ln0_bwd")

    gw_in_mine = matmul_tn(h0b, dproj, name="g_w_in")
    gw_in = jnp.concatenate([gw_in_mine[:, mine[nm][0]:mine[nm][0] + orig[nm][1]]
                             for nm in ["hq", "hf", "hi", "hg", "fq", "fk", "fv", "ff", "ga", "gb"]], axis=1)
    gfull = {
        "w_in": gw_in,
        "w_a": matmul_tn(y_a, dma, name="g_w_a"),
        "w_b": matmul_tn(y_b, dmb, name="g_w_b"),
        "w_o": matmul_tn(merged, dz1b, name="g_w_o"),
        "w_ff1": matmul_tn(h1b, du, name="g_w_ff1"),
        "w_ff2": matmul_tn(act, dz2b, name="g_w_ff2"),
        "w_pg": matmul_tn(h1b, dpg, name="g_w_pg"),
        "w_p": matmul_tn(p_b, dpe, name="g_w_p"),
    }

    def chip_parts(nm, s):
        g = gfull[nm]
        n = g.shape[axis[nm]] // 4
        return lax.slice_in_dim(g, s * n, (s + 1) * n, axis=axis[nm])
    packed = jnp.stack([pack_shards([chip_parts(nm, s) for nm in names]) for s in range(4)])
    me = 2 * lax.axis_index("x") + lax.axis_index("y")
    own = lax.dynamic_index_in_dim(packed, me, axis=0, keepdims=False)
    got = scatter_chips(packed.astype(BF16), name="scatter_grads")

    def sum4_fn(rows, vecs):
        a, r0, r1, r2 = rows
        return [((a + r0.astype(F32)) + r1.astype(F32)) + r2.astype(F32)], []
    (q_core,), _ = rowwise(sum4_fn, [own, got[0], got[1], got[2]], [], [(PACK_W, F32)], name="sum_chips", tm=256)
    q_other = swap_cores(q_core, name="swap_cores")

    def sum2_fn(rows, vecs):
        return [rows[0] + rows[1]], []
    (g_packed,), _ = rowwise(sum2_fn, [q_core, q_other], [], [(PACK_W, F32)], name="sum_cores", tm=256)
    g_shards = dict(zip(names, unpack_shards(g_packed, shard_shapes)))

    def row1024(*parts):
        r = jnp.concatenate([q.reshape(1, -1) for q in parts], axis=1)
        return jnp.concatenate([r, jnp.zeros((1, PACK_W - r.shape[1]), F32)], axis=1) if r.shape[1] < PACK_W else r
    small_rows = [row1024(g_ln0_g), row1024(g_ln0_b), row1024(g_ln1_g), row1024(g_ln1_b), row1024(g_ln2_g),
                  row1024(g_ln2_b), row1024(g_norm_g, g_lb), row1024(g_fb[:, :fh], loss_part[:, :1])]
    small = allreduce_small(jnp.concatenate(small_rows, axis=0), name="allreduce_small")
    s_ln0_g, s_ln0_b, s_ln1_g, s_ln1_b, s_ln2_g, s_ln2_b = [small[r:r + 1] for r in range(6)]
    s_norm_g, s_lb = small[6:7, :hw], small[6:7, hw:2 * hw]
    s_fb, loss = small[7:8, :fh], small[7, fh]
    p0 = probs[0:1]
    jac = p0 * (1.0 - p0)
    s_hg_lb = jnp.concatenate([s_lb * jac, -s_lb * jac], axis=0)

    small_w = [vec(ln0_g), vec(ln0_b), ln1_g, ln1_b, ln2_g, ln2_b, hg_lb.reshape(1, -1), hg_norm_g, fox_fb]
    small_g = [s_ln0_g, s_ln0_b, s_ln1_g, s_ln1_b, s_ln2_g, s_ln2_b, s_hg_lb.reshape(1, -1), s_norm_g, s_fb]
    small_m = [vec(m_ln0_g), vec(m_ln0_b), m_ln1_g, m_ln1_b, m_ln2_g, m_ln2_b, m_hg_lb.reshape(1, -1), m_hg_norm_g, m_fox_fb]
    small_v = [vec(v_ln0_g), vec(v_ln0_b), v_ln1_g, v_ln1_b, v_ln2_g, v_ln2_b, v_hg_lb.reshape(1, -1), v_hg_norm_g, v_fox_fb]
    pad_rows = lambda lst, fill: jnp.concatenate(
        [row1024(a) if fill == 0.0 else jnp.concatenate([a.reshape(1, -1), jnp.full((1, PACK_W - a.size), fill, F32)], axis=1)
         for a in lst] + [jnp.full((16 - len(lst), PACK_W), fill, F32)], axis=0)
    sd, sm, sv = adamw(pad_rows(small_w, 0.0), pad_rows(small_g, 0.0), pad_rows(small_m, 0.0), pad_rows(small_v, 1.0),
                       name="adamw_small")
    small_shapes = [ln0_g.shape, ln0_b.shape, ln1_g.shape, ln1_b.shape, ln2_g.shape, ln2_b.shape, hg_lb.shape,
                    hg_norm_g.shape, fox_fb.shape]
    take = lambda buf: [buf[r, :int(np.prod(shp))].reshape(shp) for r, shp in enumerate(small_shapes)]
    sg_out, sd_out, sm_out, sv_out = [g.reshape(shp) for g, shp in zip(small_g, small_shapes)], take(sd), take(sm), take(sv)

    big_out = {}
    for nm in names:
        delta, m2, v2 = adamw(big[nm], g_shards[nm], big_m[nm], big_v[nm], name="adamw_" + nm)
        big_out[nm] = (g_shards[nm][None], delta[None], m2[None], v2[None])

    def ordered(k):
        sm_ = [sg_out, sd_out, sm_out, sv_out][k]
        bg = lambda nm: big_out[nm][k]
        return [sm_[0], sm_[1], bg("w_in"), sm_[6], sm_[7], sm_[8], bg("w_a"), bg("w_b"), bg("w_o"), sm_[2], sm_[3],
                bg("w_ff1"), bg("w_ff2"), bg("w_pg"), bg("w_p"), sm_[4], sm_[5]]
    grad_x = dx.reshape(n_seq, seq, d)
    return (loss, grad_x, *ordered(0), *ordered(1), *ordered(2), *ordered(3))
```

```python
import functools

import numpy as np
import jax
import jax.numpy as jnp
from jax import lax
from jax.experimental import pallas as pl
from jax.experimental.pallas import tpu as pltpu

F32 = jnp.float32
BF16 = jnp.bfloat16
MESH = pl.DeviceIdType.MESH

VMEM_LIMIT_BYTES = 48 * 1024 * 1024
LANES = 128
HG_HEADS = 4
HG_DIM = 128
HG_BLK = 16
HG_TILE = 256
FOX_HDIM = 64
FOX_AUG = 128
FOX_TQ = 512
LN_EPS = 1e-5
RMS_EPS = 1e-6
DEPTH = 1
ALPHA = (2.0 * DEPTH) ** 0.25
ADAM_LR, ADAM_B1, ADAM_B2, ADAM_EPS, ADAM_WD, ADAM_STEP = 0.001, 0.9, 0.999, 1e-08, 0.01, 10
NEG_INF = -1e30


def _cparams(sem):
    return pltpu.CompilerParams(dimension_semantics=sem, vmem_limit_bytes=VMEM_LIMIT_BYTES)


def _tile(n, cap):
    if n <= cap:
        return n
    best = None
    for t in range(LANES, cap + 1, LANES):
        if n % t == 0:
            best = t
    assert best is not None, (n, cap)
    return best


def matmul_nn(a, w, *, name, out_dtype=F32, epilogue=None, aux=None, tm=512):
    T, K = a.shape
    K2, N = w.shape
    assert K == K2 and T % tm == 0
    tn = _tile(N, 1152)
    tk = _tile(K, 1152)
    nk = K // tk

    def body(*refs):
        if aux is None:
            a_ref, w_ref, o_ref, acc_ref = refs
            x_ref = None
        else:
            a_ref, w_ref, x_ref, o_ref, acc_ref = refs
        k = pl.program_id(2)
        part = jnp.dot(a_ref[...], w_ref[...], preferred_element_type=F32)

        def write(res):
            if epilogue is not None:
                res = epilogue(res) if x_ref is None else epilogue(res, x_ref[...])
            o_ref[...] = res.astype(out_dtype)

        if nk == 1:
            write(part)
        else:
            @pl.when(k == 0)
            def _():
                acc_ref[...] = part

            @pl.when(k > 0)
            def _():
                acc_ref[...] += part

            @pl.when(k == nk - 1)
            def _():
                write(acc_ref[...])

    in_specs = [pl.BlockSpec((tm, tk), lambda n, m, k: (m, k)),
                pl.BlockSpec((tk, tn), lambda n, m, k: (k, n))]
    args = [a, w]
    if aux is not None:
        in_specs.append(pl.BlockSpec((tm, tn), lambda n, m, k: (m, n)))
        args.append(aux)
    return pl.pallas_call(
        body, name=name,
        grid=(N // tn, T // tm, nk),
        in_specs=in_specs,
        out_specs=pl.BlockSpec((tm, tn), lambda n, m, k: (m, n)),
        out_shape=jax.ShapeDtypeStruct((T, N), out_dtype),
        scratch_shapes=[pltpu.VMEM((tm, tn) if nk > 1 else (8, LANES), F32)],
        compiler_params=_cparams(("parallel", "parallel", "arbitrary")),
    )(*args)


def matmul_tn(a, b, *, name, tk=512):
    T, M = a.shape
    T2, N = b.shape
    assert T == T2 and T % tk == 0
    tm = _tile(M, 1024)
    tn = _tile(N, 1152)

    def body(a_ref, b_ref, o_ref):
        k = pl.program_id(2)
        part = lax.dot_general(a_ref[...], b_ref[...], (((0,), (0,)), ((), ())), preferred_element_type=F32)

        @pl.when(k == 0)
        def _():
            o_ref[...] = part

        @pl.when(k > 0)
        def _():
            o_ref[...] += part

    return pl.pallas_call(
        body, name=name,
        grid=(M // tm, N // tn, T // tk),
        in_specs=[pl.BlockSpec((tk, tm), lambda m, n, k: (k, m)),
                  pl.BlockSpec((tk, tn), lambda m, n, k: (k, n))],
        out_specs=pl.BlockSpec((tm, tn), lambda m, n, k: (m, n)),
        out_shape=jax.ShapeDtypeStruct((M, N), F32),
        compiler_params=_cparams(("parallel", "parallel", "arbitrary")),
    )(a, b)


def rowwise(fn, rows, vecs, outs, accs=(), *, name, tm=512):
    rows = [r if isinstance(r, tuple) else (r, 0, r.shape[1]) for r in rows]
    T = rows[0][0].shape[0]
    tm = min(tm, T)
    assert T % tm == 0
    n_rows, n_vecs, n_outs, n_accs = len(rows), len(vecs), len(outs), len(accs)

    def body(*refs):
        row_refs = refs[:n_rows]
        vec_refs = refs[n_rows:n_rows + n_vecs]
        out_refs = refs[n_rows + n_vecs:n_rows + n_vecs + n_outs]
        acc_refs = refs[n_rows + n_vecs + n_outs:]
        out_vals, acc_vals = fn([r[...] for r in row_refs], [v[...] for v in vec_refs])
        assert len(out_vals) == n_outs and len(acc_vals) == n_accs
        for r, val in zip(out_refs, out_vals):
            r[...] = val.astype(r.dtype)
        if n_accs:
            i = pl.program_id(0)

            @pl.when(i == 0)
            def _():
                for r in acc_refs:
                    r[...] = jnp.zeros_like(r)

            for r, val in zip(acc_refs, acc_vals):
                r[...] += val

    in_specs = []
    for arr, off, width in rows:
        assert off % width == 0
        in_specs.append(pl.BlockSpec((tm, width), functools.partial(lambda i, blk: (i, blk), blk=off // width)))
    for v in vecs:
        in_specs.append(pl.BlockSpec(v.shape, lambda i: (0, 0)))
    out_specs = [pl.BlockSpec((tm, w), lambda i: (i, 0)) for w, _ in outs]
    out_specs += [pl.BlockSpec((1, w), lambda i: (0, 0)) for w in accs]
    out_shape = [jax.ShapeDtypeStruct((T, w), dt) for w, dt in outs]
    out_shape += [jax.ShapeDtypeStruct((1, w), F32) for w in accs]
    res = pl.pallas_call(
        body, name=name,
        grid=(T // tm,),
        in_specs=in_specs, out_specs=out_specs, out_shape=out_shape,
        compiler_params=_cparams(("arbitrary",) if n_accs else ("parallel",)),
    )(*[r[0] for r in rows], *vecs)
    return res[:n_outs], res[n_outs:]


def _colsum(x):
    return jnp.sum(x, axis=0, keepdims=True)


def _sigmoid(x):
    return 1.0 / (1.0 + jnp.exp(-x))


def _ln_stats(z):
    mu = jnp.mean(z, axis=-1, keepdims=True)
    zc = z - mu
    var = jnp.mean(zc * zc, axis=-1, keepdims=True)
    return zc * lax.rsqrt(var + LN_EPS)


def _ln_bwd(zhat_src, dy, g):
    mu = jnp.mean(zhat_src, axis=-1, keepdims=True)
    zc = zhat_src - mu
    var = jnp.mean(zc * zc, axis=-1, keepdims=True)
    rstd = lax.rsqrt(var + LN_EPS)
    zh = zc * rstd
    dzh = dy * g
    dz = rstd * (dzh - jnp.mean(dzh, axis=-1, keepdims=True) - zh * jnp.mean(dzh * zh, axis=-1, keepdims=True))
    return dz, _colsum(dy * zh), _colsum(dy)


def _hg_constants():
    r = np.arange(HG_TILE)
    same = (r[:, None] // HG_BLK) == (r[None, :] // HG_BLK)
    lower = (same & (r[None, :] <= r[:, None])).astype(np.float32)
    upper = (same & (r[None, :] >= r[:, None])).astype(np.float32)
    total = same.astype(np.float32)
    w = HG_HEADS * HG_DIM
    c = np.arange(w)
    bd = ((c[:, None] // HG_DIM) == (c[None, :] // HG_DIM)).astype(np.float32)
    n = HG_BLK * HG_BLK
    rr = np.arange(n)
    sel_t = (rr[None, :] // HG_BLK == np.arange(HG_BLK)[:, None]).astype(np.float32)
    sel_s = (rr[None, :] % HG_BLK == np.arange(HG_BLK)[:, None]).astype(np.float32)
    as_bf = lambda m: jnp.asarray(m, dtype=BF16)
    return as_bf(lower), as_bf(upper), as_bf(total), as_bf(bd), as_bf(sel_t), as_bf(sel_s)


def _keep_bf16_bits(x):
    bits = lax.bitcast_convert_type(x, jnp.int32) & jnp.int32(-65536)
    return lax.bitcast_convert_type(bits, F32)


def _split3(x):
    hi = _keep_bf16_bits(x)
    r1 = x - hi
    mid = _keep_bf16_bits(r1)
    lo = _keep_bf16_bits(r1 - mid)
    return hi.astype(BF16), mid.astype(BF16), lo.astype(BF16)


def _dot3(m01, x):
    hi, mid, lo = _split3(x)
    d = lambda p: jnp.dot(m01, p, preferred_element_type=F32)
    return (d(lo) + d(mid)) + d(hi)


def _hg_prologue(hq, hf, lb, lower, total):
    sq = _sigmoid(hq)
    q = hq * sq
    sg = _sigmoid(hf)
    f = lb + (1.0 - lb) * sg
    g = jnp.log(f)
    k = 1.0 - f
    b = _dot3(lower, g)
    bl = _dot3(total, g)
    return q, k, f, sg, sq, b, bl


def _stack16(fn):
    return [fn(t) for t in range(HG_BLK)]


def hgrn2_fwd(proj, offs, lb, n_seq, seq, *, name):
    T = n_seq * seq
    W = HG_HEADS * HG_DIM
    n_tiles = seq // HG_TILE
    nb = HG_TILE // HG_BLK
    lower, _, total, bd, sel_t, _ = _hg_constants()

    def body(hq_ref, hf_ref, hi_ref, lb_ref, lower_ref, total_ref, bd_ref, selt_ref,
             o_ref, st_out_ref,
             st_ref, q_s, k_s, v_s, b_s, qt_s, kt_s, d_s, p_s):
        @pl.when(pl.program_id(1) == 0)
        def _():
            st_ref[...] = jnp.zeros_like(st_ref)

        q, k, _, _, _, b, bl = _hg_prologue(hq_ref[...], hf_ref[...], lb_ref[...], lower_ref[...], total_ref[...])
        q_s[...] = q
        k_s[...] = k
        v_s[...] = hi_ref[...]
        b_s[...] = b
        qt_s[...] = q * jnp.exp(b)
        kt_s[...] = k * jnp.exp(jnp.minimum(bl - b, 0.0))
        d_s[...] = jnp.exp(bl)
        rowi = lax.broadcasted_iota(jnp.int32, (HG_BLK, W), 0)

        def block(i, carry):
            r0 = pl.multiple_of(i * HG_BLK, HG_BLK)
            rows = pl.ds(r0, HG_BLK)
            qi, ki, vi, bi = q_s[rows, :], k_s[rows, :], v_s[rows, :], b_s[rows, :]
            for t in range(HG_BLK):
                e = jnp.where(rowi <= t, jnp.exp(jnp.minimum(bi[t:t + 1, :] - bi, 0.0)), 0.0)
                p_s[pl.ds(t * HG_BLK, HG_BLK), :] = (e * qi[t:t + 1, :] * ki).astype(BF16)
            a_b = jnp.dot(p_s[...], bd_ref[...], preferred_element_type=F32)
            vt = jnp.concatenate([vi] * HG_BLK, axis=0)
            o_blk = jnp.dot(selt_ref[...], (a_b * vt).astype(BF16), preferred_element_type=F32)
            qti, kti, di = qt_s[rows, :], kt_s[rows, :], d_s[rows, :]
            outs = []
            for h in range(HG_HEADS):
                hs = slice(h * HG_DIM, (h + 1) * HG_DIM)
                st_h = st_ref[hs, :]
                st_out_ref[i, hs, :] = st_h
                outs.append(lax.dot_general(qti[:, hs].astype(BF16), st_h.astype(BF16),
                                            (((1,), (1,)), ((), ())), preferred_element_type=F32))
                upd = lax.dot_general(vi[:, hs].astype(BF16), kti[:, hs].astype(BF16),
                                      (((0,), (0,)), ((), ())), preferred_element_type=F32)
                st_ref[hs, :] = st_h * di[0:1, hs] + upd
            o_ref[rows, :] = o_blk + jnp.concatenate(outs, axis=1)
            return carry

        lax.fori_loop(0, nb, block, 0)

    col = lambda off: functools.partial(lambda s, t, blk: (s * n_tiles + t, blk), blk=off // W)
    const = lambda m: pl.BlockSpec(m.shape, lambda s, t: (0, 0))
    tile_f32 = pltpu.VMEM((HG_TILE, W), F32)
    o, states = pl.pallas_call(
        body, name=name,
        grid=(n_seq, n_tiles),
        in_specs=[pl.BlockSpec((HG_TILE, W), col(offs[0])), pl.BlockSpec((HG_TILE, W), col(offs[1])),
                  pl.BlockSpec((HG_TILE, W), col(offs[2])), const(lb), const(lower), const(total), const(bd),
                  const(sel_t)],
        out_specs=[pl.BlockSpec((HG_TILE, W), lambda s, t: (s * n_tiles + t, 0)),
                   pl.BlockSpec((nb, W, HG_DIM), lambda s, t: (s * n_tiles + t, 0, 0))],
        out_shape=[jax.ShapeDtypeStruct((T, W), F32), jax.ShapeDtypeStruct((T // HG_BLK, W, HG_DIM), F32)],
        scratch_shapes=[pltpu.VMEM((W, HG_DIM), F32)] + [tile_f32] * 7
                       + [pltpu.VMEM((HG_BLK * HG_BLK, W), BF16)],
        compiler_params=_cparams(("arbitrary", "arbitrary")),
    )(proj, proj, proj, lb, lower, total, bd, sel_t)
    return o, states


def hgrn2_bwd(proj, offs, lb, do, states, n_seq, seq, *, name):
    T = n_seq * seq
    W = HG_HEADS * HG_DIM
    n_tiles = seq // HG_TILE
    nb = HG_TILE // HG_BLK
    lower, upper, total, bd, sel_t, sel_s = _hg_constants()

    def body(hq_ref, hf_ref, hi_ref, do_ref, st_in_ref, lb_ref, lower_ref, upper_ref, total_ref, bd_ref,
             selt_ref, sels_ref,
             dhq_ref, dhf_ref, dhi_ref, dlb_ref,
             dst_ref, q_s, k_s, v_s, b_s, qt_s, kt_s, d_s, eb_s, ekb_s, dq_s, dk_s, db_s, dv_s,
             p_s, e_s, w_s):
        first = jnp.logical_and(pl.program_id(0) == 0, pl.program_id(1) == 0)

        @pl.when(first)
        def _():
            dlb_ref[...] = jnp.zeros_like(dlb_ref)

        @pl.when(pl.program_id(1) == 0)
        def _():
            dst_ref[...] = jnp.zeros_like(dst_ref)

        hq, lbv = hq_ref[...], lb_ref[...]
        q, k, f, sg, sq, b, bl = _hg_prologue(hq, hf_ref[...], lbv, lower_ref[...], total_ref[...])
        eb = jnp.exp(b)
        ekb = jnp.exp(jnp.minimum(bl - b, 0.0))
        q_s[...] = q
        k_s[...] = k
        v_s[...] = hi_ref[...]
        b_s[...] = b
        eb_s[...] = eb
        ekb_s[...] = ekb
        qt_s[...] = q * eb
        kt_s[...] = k * ekb
        d_s[...] = jnp.exp(bl)
        rowi = lax.broadcasted_iota(jnp.int32, (HG_BLK, W), 0)
        last_row = rowi == HG_BLK - 1

        def block(j, carry):
            i = nb - 1 - j
            r0 = pl.multiple_of(i * HG_BLK, HG_BLK)
            rows = pl.ds(r0, HG_BLK)
            qi, ki, vi, bi, doi = q_s[rows, :], k_s[rows, :], v_s[rows, :], b_s[rows, :], do_ref[rows, :]
            for t in range(HG_BLK):
                sl = pl.ds(t * HG_BLK, HG_BLK)
                e = jnp.where(rowi <= t, jnp.exp(jnp.minimum(bi[t:t + 1, :] - bi, 0.0)), 0.0)
                e_s[sl, :] = e
                p_s[sl, :] = (e * qi[t:t + 1, :] * ki).astype(BF16)
                w_s[sl, :] = (doi[t:t + 1, :] * vi).astype(BF16)
            a_b = jnp.dot(p_s[...], bd_ref[...], preferred_element_type=F32)
            da_b = jnp.dot(w_s[...], bd_ref[...], preferred_element_type=F32)
            x = da_b * e_s[...]
            k_til = jnp.concatenate([ki] * HG_BLK, axis=0)
            q_rep = jnp.concatenate([jnp.broadcast_to(qi[t:t + 1, :], (HG_BLK, W)) for t in range(HG_BLK)], axis=0)
            do_rep = jnp.concatenate([jnp.broadcast_to(doi[t:t + 1, :], (HG_BLK, W)) for t in range(HG_BLK)], axis=0)
            dq_in = jnp.dot(selt_ref[...], (x * k_til).astype(BF16), preferred_element_type=F32)
            dk_in = jnp.dot(sels_ref[...], (x * q_rep).astype(BF16), preferred_element_type=F32)
            dv_in = jnp.dot(sels_ref[...], (a_b * do_rep).astype(BF16), preferred_element_type=F32)
            qti, kti, di = qt_s[rows, :], kt_s[rows, :], d_s[rows, :]
            dqt, dkt, dvt, dd = [], [], [], []
            for h in range(HG_HEADS):
                hs = slice(h * HG_DIM, (h + 1) * HG_DIM)
                st_h = st_in_ref[i, hs, :]
                dst_h = dst_ref[hs, :]
                do_h, v_h = doi[:, hs].astype(BF16), vi[:, hs].astype(BF16)
                dst_b = dst_h.astype(BF16)
                dqt.append(jnp.dot(do_h, st_h.astype(BF16), preferred_element_type=F32))
                dkt.append(jnp.dot(v_h, dst_b, preferred_element_type=F32))
                dvt.append(lax.dot_general(kti[:, hs].astype(BF16), dst_b, (((1,), (1,)), ((), ())),
                                           preferred_element_type=F32))
                dd.append(jnp.sum(dst_h * st_h, axis=0, keepdims=True))
                upd = lax.dot_general(do_h, qti[:, hs].astype(BF16), (((0,), (0,)), ((), ())),
                                      preferred_element_type=F32)
                dst_ref[hs, :] = dst_h * di[0:1, hs] + upd
            dqt = jnp.concatenate(dqt, axis=1)
            dkt = jnp.concatenate(dkt, axis=1)
            dvt = jnp.concatenate(dvt, axis=1)
            dd = jnp.concatenate(dd, axis=1)
            dbl = jnp.sum(dkt * kti, axis=0, keepdims=True) + dd * di[0:1, :]
            db = qi * dq_in - ki * dk_in + dqt * qti - dkt * kti
            db_s[rows, :] = db + jnp.where(last_row, dbl, 0.0)
            dq_s[rows, :] = dq_in + dqt * eb_s[rows, :]
            dk_s[rows, :] = dk_in + dkt * ekb_s[rows, :]
            dv_s[rows, :] = dv_in + dvt
            return carry

        lax.fori_loop(0, nb, block, 0)

        dg = _dot3(upper_ref[...], db_s[...])
        dhq_ref[...] = (dq_s[...] * (sq * (1.0 + hq * (1.0 - sq)))).astype(dhq_ref.dtype)
        df = dg / f - dk_s[...]
        dhf_ref[...] = (df * (1.0 - lbv) * (sg * (1.0 - sg))).astype(dhf_ref.dtype)
        dhi_ref[...] = dv_s[...].astype(dhi_ref.dtype)
        dlb_ref[...] += _colsum(df * (1.0 - sg))

    rev = lambda s, t: s * n_tiles + (n_tiles - 1 - t)
    col = lambda off: functools.partial(lambda s, t, blk: (rev(s, t), blk), blk=off // W)
    const = lambda m: pl.BlockSpec(m.shape, lambda s, t: (0, 0))
    row = pl.BlockSpec((HG_TILE, W), lambda s, t: (rev(s, t), 0))
    tile_f32 = pltpu.VMEM((HG_TILE, W), F32)
    n2 = HG_BLK * HG_BLK
    return pl.pallas_call(
        body, name=name,
        grid=(n_seq, n_tiles),
        in_specs=[pl.BlockSpec((HG_TILE, W), col(offs[0])), pl.BlockSpec((HG_TILE, W), col(offs[1])),
                  pl.BlockSpec((HG_TILE, W), col(offs[2])), row,
                  pl.BlockSpec((nb, W, HG_DIM), lambda s, t: (rev(s, t), 0, 0)),
                  const(lb), const(lower), const(upper), const(total), const(bd), const(sel_t), const(sel_s)],
        out_specs=[row, row, row, pl.BlockSpec((1, W), lambda s, t: (0, 0))],
        out_shape=[jax.ShapeDtypeStruct((T, W), BF16)] * 3 + [jax.ShapeDtypeStruct((1, W), F32)],
        scratch_shapes=[pltpu.VMEM((W, HG_DIM), F32)] + [tile_f32] * 13
                       + [pltpu.VMEM((n2, W), BF16), pltpu.VMEM((n2, W), F32), pltpu.VMEM((n2, W), BF16)],
        compiler_params=_cparams(("arbitrary", "arbitrary")),
    )(proj, proj, proj, do, states, lb, lower, upper, total, bd, sel_t, sel_s)


def _diag_mask(tq):
    return lax.broadcasted_iota(jnp.int32, (tq, tq), 1) <= lax.broadcasted_iota(jnp.int32, (tq, tq), 0)


def _qk(q, k):
    return lax.dot_general(q, k, (((1,), (1,)), ((), ())), preferred_element_type=F32)


def _causal_pairs(n, sweeps=1, by_key=False):
    if by_key:
        rows = [(i, j, 0) for j in range(n) for i in range(j, n)]
    else:
        rows = [(i, j, s) for i in range(n) for s in range(sweeps) for j in range(i + 1)]
    return tuple(jnp.asarray(np.array([r[c] for r in rows], np.int32)) for c in range(3))


def fox_fwd(qa, ka, v, *, name):
    BH, S, _ = qa.shape
    tq = min(FOX_TQ, S)
    itab, jtab, _ = _causal_pairs(S // tq)

    def body(itab_ref, jtab_ref, q_ref, k_ref, v_ref, o_ref, lse_ref, m_s, l_s, acc_s):
        t = pl.program_id(1)
        i, j = itab_ref[t], jtab_ref[t]

        @pl.when(j == 0)
        def _():
            m_s[...] = jnp.full_like(m_s, NEG_INF)
            l_s[...] = jnp.zeros_like(l_s)
            acc_s[...] = jnp.zeros_like(acc_s)

        def step(on_diagonal):
            s = _qk(q_ref[...], k_ref[...])
            if on_diagonal:
                s = jnp.where(_diag_mask(tq), s, NEG_INF)
            m_prev = m_s[...]
            m_new = jnp.maximum(m_prev, jnp.max(s, axis=-1, keepdims=True))
            alpha = jnp.exp(m_prev - m_new)
            p = jnp.exp(s - m_new[:, 0:1])
            l_s[...] = alpha * l_s[...] + jnp.sum(p, axis=-1, keepdims=True)
            acc_s[...] = alpha[:, 0:FOX_HDIM] * acc_s[...] + jnp.dot(p.astype(BF16), v_ref[...],
                                                                     preferred_element_type=F32)
            m_s[...] = m_new

        @pl.when(j < i)
        def _():
            step(False)

        @pl.when(j == i)
        def _():
            step(True)
            o_ref[...] = acc_s[...] / l_s[:, 0:FOX_HDIM]
            lse_ref[...] = m_s[...] + jnp.log(l_s[...])

    qspec = lambda d: pl.BlockSpec((None, tq, d), lambda b, t, it, jt: (b, it[t], 0))
    kspec = lambda d: pl.BlockSpec((None, tq, d), lambda b, t, it, jt: (b, jt[t], 0))
    return pl.pallas_call(
        body, name=name,
        grid_spec=pltpu.PrefetchScalarGridSpec(
            num_scalar_prefetch=2, grid=(BH, itab.shape[0]),
            in_specs=[qspec(FOX_AUG), kspec(FOX_AUG), kspec(FOX_HDIM)],
            out_specs=[qspec(FOX_HDIM), qspec(LANES)],
            scratch_shapes=[pltpu.VMEM((tq, LANES), F32), pltpu.VMEM((tq, LANES), F32),
                            pltpu.VMEM((tq, FOX_HDIM), F32)]),
        out_shape=[jax.ShapeDtypeStruct((BH, S, FOX_HDIM), F32), jax.ShapeDtypeStruct((BH, S, LANES), F32)],
        compiler_params=_cparams(("parallel", "arbitrary")),
    )(itab, jtab, qa, ka, v)


def _fox_p_dp(q, k, v, do, lse, on_diagonal):
    s = _qk(q, k)
    if on_diagonal:
        s = jnp.where(_diag_mask(s.shape[0]), s, NEG_INF)
    return jnp.exp(s - lse[:, 0:1]), _qk(do, v)


def fox_bwd_dq(qa, ka, v, do, lse, *, name):
    BH, S, _ = qa.shape
    tq = min(FOX_TQ, S)
    itab, jtab, stab = _causal_pairs(S // tq, sweeps=2)

    def body(itab_ref, jtab_ref, stab_ref, q_ref, k_ref, v_ref, do_ref, lse_ref, dq_ref, rsum_ref, delta_ref):
        t = pl.program_id(1)
        i, j, sweep = itab_ref[t], jtab_ref[t], stab_ref[t]

        @pl.when(jnp.logical_and(j == 0, sweep == 0))
        def _():
            dq_ref[...] = jnp.zeros_like(dq_ref)
            rsum_ref[...] = jnp.zeros_like(rsum_ref)
            delta_ref[...] = jnp.zeros_like(delta_ref)

        def step(on_diagonal, second):
            p, dp = _fox_p_dp(q_ref[...], k_ref[...], v_ref[...], do_ref[...], lse_ref[...], on_diagonal)
            if not second:
                delta_ref[...] += jnp.sum(p * dp, axis=-1, keepdims=True)
            else:
                ds = p * (dp - delta_ref[:, 0:1])
                dq_ref[...] += jnp.dot(ds.astype(BF16), k_ref[...], preferred_element_type=F32)
                rsum_ref[...] += jnp.sum(ds, axis=-1, keepdims=True)

        for on_diagonal in (False, True):
            for second in (False, True):
                cond = jnp.logical_and((j == i) if on_diagonal else (j < i), sweep == int(second))
                pl.when(cond)(functools.partial(step, on_diagonal, second))

    qspec = lambda d: pl.BlockSpec((None, tq, d), lambda b, t, it, jt, st: (b, it[t], 0))
    kspec = lambda d: pl.BlockSpec((None, tq, d), lambda b, t, it, jt, st: (b, jt[t], 0))
    return pl.pallas_call(
        body, name=name,
        grid_spec=pltpu.PrefetchScalarGridSpec(
            num_scalar_prefetch=3, grid=(BH, itab.shape[0]),
            in_specs=[qspec(FOX_AUG), kspec(FOX_AUG), kspec(FOX_HDIM), qspec(FOX_HDIM), qspec(LANES)],
            out_specs=[qspec(FOX_AUG), qspec(LANES), qspec(LANES)]),
        out_shape=[jax.ShapeDtypeStruct((BH, S, FOX_AUG), F32), jax.ShapeDtypeStruct((BH, S, LANES), F32),
                   jax.ShapeDtypeStruct((BH, S, LANES), F32)],
        compiler_params=_cparams(("parallel", "arbitrary")),
    )(itab, jtab, stab, qa, ka, v, do, lse)


def fox_bwd_dkv(qa, ka, v, do, delta, lse, *, name):
    BH, S, _ = qa.shape
    tq = min(FOX_TQ, S)
    itab, jtab, _ = _causal_pairs(S // tq, by_key=True)

    def body(itab_ref, jtab_ref, q_ref, k_ref, v_ref, do_ref, delta_ref, lse_ref, dk_ref, dv_ref, dsum_ref):
        t = pl.program_id(1)
        i, j = itab_ref[t], jtab_ref[t]

        def step(on_diagonal):
            q, do = q_ref[...], do_ref[...]
            p, dp = _fox_p_dp(q, k_ref[...], v_ref[...], do, lse_ref[...], on_diagonal)
            ds = p * (dp - delta_ref[:, 0:1])
            tn = (((0,), (0,)), ((), ()))
            dv = lax.dot_general(p.astype(BF16), do, tn, preferred_element_type=F32)
            dk = lax.dot_general(ds.astype(BF16), q, tn, preferred_element_type=F32)
            if on_diagonal:
                dv_ref[...], dk_ref[...], dsum_ref[...] = dv, dk, _colsum(ds)
            else:
                dv_ref[...] += dv
                dk_ref[...] += dk
                dsum_ref[...] += _colsum(ds)

        @pl.when(i == j)
        def _():
            step(True)

        @pl.when(i > j)
        def _():
            step(False)

    qspec = lambda d: pl.BlockSpec((None, tq, d), lambda b, t, it, jt: (b, it[t], 0))
    kspec = lambda d: pl.BlockSpec((None, tq, d), lambda b, t, it, jt: (b, jt[t], 0))
    return pl.pallas_call(
        body, name=name,
        grid_spec=pltpu.PrefetchScalarGridSpec(
            num_scalar_prefetch=2, grid=(BH, itab.shape[0]),
            in_specs=[qspec(FOX_AUG), kspec(FOX_AUG), kspec(FOX_HDIM), qspec(FOX_HDIM), qspec(LANES), qspec(LANES)],
            out_specs=[kspec(FOX_AUG), kspec(FOX_HDIM),
                       pl.BlockSpec((None, 1, tq), lambda b, t, it, jt: (b, 0, jt[t]))]),
        out_shape=[jax.ShapeDtypeStruct((BH, S, FOX_AUG), F32), jax.ShapeDtypeStruct((BH, S, FOX_HDIM), F32),
                   jax.ShapeDtypeStruct((BH, 1, S), F32)],
        compiler_params=_cparams(("parallel", "arbitrary")),
    )(itab, jtab, qa, ka, v, do, delta, lse)


def seq_cumsum(x, n_seq, seq, *, reverse, name):
    T, C = x.shape
    tb = min(256, seq)
    n = seq // tb
    r = np.arange(tb)
    tri = (r[None, :] >= r[:, None]) if reverse else (r[None, :] <= r[:, None])
    tri = jnp.asarray(tri.astype(np.float32), dtype=BF16)

    def body(x_ref, tri_ref, o_ref, carry_s):
        @pl.when(pl.program_id(1) == 0)
        def _():
            carry_s[...] = jnp.zeros_like(carry_s)

        xv = x_ref[...]
        o_ref[...] = _dot3(tri_ref[...], xv) + carry_s[...]
        carry_s[...] += _colsum(xv)

    blk = (lambda s, t: (s * n + (n - 1 - t), 0)) if reverse else (lambda s, t: (s * n + t, 0))
    return pl.pallas_call(
        body, name=name,
        grid=(n_seq, n),
        in_specs=[pl.BlockSpec((tb, C), blk), pl.BlockSpec((tb, tb), lambda s, t: (0, 0))],
        out_specs=pl.BlockSpec((tb, C), blk),
        out_shape=jax.ShapeDtypeStruct((T, C), F32),
        scratch_shapes=[pltpu.VMEM((1, C), F32)],
        compiler_params=_cparams(("arbitrary", "arbitrary")),
    )(x, tri)


def _place():
    return lax.axis_index("x"), lax.axis_index("y"), lax.axis_index("c")


def _other_chips(x, y):
    return [(1 - x, y), (x, 1 - y), (1 - x, 1 - y)]


def allgather_chips(shard, *, name):
    R, C = shard.shape
    H = R // 2
    assert R % (2 * ROW_ALIGN) == 0

    def body(x_ref, o_ref, send_sems, recv_sems, local_sem):
        x, y, c = _place()
        me = 2 * x + y
        mine = pl.ds(c * H, H)
        own = pltpu.make_async_copy(x_ref, o_ref.at[me], local_sem)
        own.start()
        chips = _other_chips(x, y)

        theirs = pl.ds((1 - c) * H, H)

        def copy(k, src, chip, rows, to):
            return pltpu.make_async_remote_copy(src_ref=src, dst_ref=o_ref.at[2 * chip[0] + chip[1], rows],
                                                send_sem=send_sems.at[k], recv_sem=recv_sems.at[k],
                                                device_id=to, device_id_type=MESH)

        first = [copy(j, x_ref.at[mine], (x, y), mine, (*chip, c)) for j, chip in enumerate(chips)]
        for cp in first:
            cp.start()
        passed = [copy(3 + j, o_ref.at[2 * chip[0] + chip[1], mine], chip, mine, (x, y, 1 - c))
                  for j, chip in enumerate(chips)]
        for j, chip in enumerate(chips):
            copy(j, x_ref.at[mine], chip, mine, (*chip, c)).wait_recv()
            passed[j].start()
        for j, chip in enumerate(chips):
            copy(3 + j, x_ref.at[mine], chip, theirs, (x, y, 1 - c)).wait_recv()
        for cp in first + passed:
            cp.wait_send()
        own.wait()

    return pl.pallas_call(
        body, name=name,
        in_specs=[pl.BlockSpec(memory_space=pl.ANY)],
        out_specs=pl.BlockSpec(memory_space=pl.ANY),
        out_shape=jax.ShapeDtypeStruct((4, R, C), shard.dtype),
        scratch_shapes=[pltpu.SemaphoreType.DMA((6,)), pltpu.SemaphoreType.DMA((6,)), pltpu.SemaphoreType.DMA(())],
        compiler_params=pltpu.CompilerParams(has_side_effects=True),
    )(shard)


def scatter_chips(parts, *, name):
    _, R, C = parts.shape

    def body(x_ref, o_ref, send_sems, recv_sems):
        x, y, c = _place()
        sends = []
        for j, (px, py) in enumerate(_other_chips(x, y)):
            cp = pltpu.make_async_remote_copy(src_ref=x_ref.at[2 * px + py], dst_ref=o_ref.at[j],
                                              send_sem=send_sems.at[j], recv_sem=recv_sems.at[j],
                                              device_id=(px, py, c), device_id_type=MESH)
            cp.start()
            sends.append(cp)
        for cp in sends:
            cp.wait_recv()
        for cp in sends:
            cp.wait_send()

    return pl.pallas_call(
        body, name=name,
        in_specs=[pl.BlockSpec(memory_space=pl.ANY)],
        out_specs=pl.BlockSpec(memory_space=pl.ANY),
        out_shape=jax.ShapeDtypeStruct((3, R, C), parts.dtype),
        scratch_shapes=[pltpu.SemaphoreType.DMA((3,)), pltpu.SemaphoreType.DMA((3,))],
        compiler_params=pltpu.CompilerParams(has_side_effects=True),
    )(parts)


def swap_cores(v, *, name):
    def body(x_ref, o_ref, send_sem, recv_sem):
        x, y, c = _place()
        cp = pltpu.make_async_remote_copy(src_ref=x_ref, dst_ref=o_ref, send_sem=send_sem, recv_sem=recv_sem,
                                          device_id=(x, y, 1 - c), device_id_type=MESH)
        cp.start()
        cp.wait()

    return pl.pallas_call(
        body, name=name,
        in_specs=[pl.BlockSpec(memory_space=pl.ANY)],
        out_specs=pl.BlockSpec(memory_space=pl.ANY),
        out_shape=jax.ShapeDtypeStruct(v.shape, v.dtype),
        scratch_shapes=[pltpu.SemaphoreType.DMA(()), pltpu.SemaphoreType.DMA(())],
        compiler_params=pltpu.CompilerParams(has_side_effects=True),
    )(v)


def allreduce_small(v, *, name):
    R, C = v.shape

    def body(x_ref, o_ref, gath_ref, send_sems, recv_sems):
        x, y, c = _place()
        me = 4 * x + 2 * y + c
        gath_ref[me] = x_ref[...]
        flips = [(k >> 2 & 1, k >> 1 & 1, k & 1) for k in range(1, 8)]
        sends = []
        for j, (fx, fy, fc) in enumerate(flips):
            peer = (x ^ fx, y ^ fy, c ^ fc)
            cp = pltpu.make_async_remote_copy(src_ref=x_ref, dst_ref=gath_ref.at[me], send_sem=send_sems.at[j],
                                              recv_sem=recv_sems.at[j], device_id=peer, device_id_type=MESH)
            cp.start()
            sends.append(cp)
        for j, (fx, fy, fc) in enumerate(flips):
            peer = (x ^ fx, y ^ fy, c ^ fc)
            pltpu.make_async_remote_copy(src_ref=x_ref, dst_ref=gath_ref.at[4 * peer[0] + 2 * peer[1] + peer[2]],
                                         send_sem=send_sems.at[j], recv_sem=recv_sems.at[j], device_id=peer,
                                         device_id_type=MESH).wait_recv()
        for cp in sends:
            cp.wait_send()
        total = gath_ref[0]
        for d in range(1, 8):
            total = total + gath_ref[d]
        o_ref[...] = total

    vm = pl.BlockSpec(memory_space=pltpu.VMEM)
    out, _ = pl.pallas_call(
        body, name=name,
        in_specs=[vm], out_specs=[vm, vm],
        out_shape=[jax.ShapeDtypeStruct((R, C), F32), jax.ShapeDtypeStruct((8, R, C), F32)],
        scratch_shapes=[pltpu.SemaphoreType.DMA((7,)), pltpu.SemaphoreType.DMA((7,))],
        compiler_params=pltpu.CompilerParams(has_side_effects=True),
    )(v)
    return out


PACK_W = 1024
ROW_ALIGN = 16
PACK_TILE = 256
BIG_WEIGHTS = (("w_in", 1), ("w_a", 1), ("w_b", 1), ("w_o", 0), ("w_ff1", 1), ("w_ff2", 0), ("w_pg", 0), ("w_p", 1))


def _pack_rows(shape):
    n = shape[0] * shape[1]
    assert n % PACK_W == 0
    rows = n // PACK_W
    return rows, -(-rows // ROW_ALIGN) * ROW_ALIGN


def pack_shards(shards):
    parts = []
    for s in shards:
        rows, padded = _pack_rows(s.shape)
        m = s.reshape(rows, PACK_W)
        if padded != rows:
            m = jnp.concatenate([m, jnp.zeros((padded - rows, PACK_W), m.dtype)], axis=0)
        parts.append(m)
    total = sum(q.shape[0] for q in parts)
    tail = -total % PACK_TILE
    if tail:
        parts.append(jnp.zeros((tail, PACK_W), parts[0].dtype))
    return jnp.concatenate(parts, axis=0)


def unpack_shards(buf, shapes):
    out, off = [], 0
    for shp in shapes:
        rows, padded = _pack_rows(shp)
        out.append(buf[off:off + rows].reshape(shp))
        off += padded
    return out


def _win_layout(d):
    hw = d // 2
    fh = hw // FOX_HDIM
    orig = {"hq": (0, hw), "hf": (hw, hw), "hi": (2 * hw, hw), "hg": (3 * hw, hw), "fq": (4 * hw, hw),
            "fk": (5 * hw, hw), "fv": (6 * hw, hw), "ff": (7 * hw, fh), "ga": (7 * hw + fh, d), "gb": (7 * hw + fh + d, d)}
    order = ["ga", "gb", "hq", "hf", "hi", "hg", "fq", "fk", "fv", "ff"]
    mine, off = {}, 0
    for nm in order:
        width = orig[nm][1] if nm != "ff" else LANES
        mine[nm] = (off, width)
        off += width
    return orig, order, mine, off


def _adam_fn(rows, vecs):
    w, g, m, v = rows
    m2 = ADAM_B1 * m + (1.0 - ADAM_B1) * g
    v2 = ADAM_B2 * v + (1.0 - ADAM_B2) * (g * g)
    m_hat = m2 / (1.0 - ADAM_B1 ** ADAM_STEP)
    v_hat = v2 / (1.0 - ADAM_B2 ** ADAM_STEP)
    delta = -ADAM_LR * (m_hat / (jnp.sqrt(v_hat) + ADAM_EPS) + ADAM_WD * w)
    return [delta, m2, v2], []


def adamw(w, g, m, v, *, name):
    c = w.shape[1]
    (delta, m2, v2), _ = rowwise(_adam_fn, [w, g, m, v], [], [(c, F32)] * 3, name=name, tm=256)
    return delta, m2, v2


def kernel(x, p, ln0_g, ln0_b, w_in, hg_lb, hg_norm_g, fox_fb, w_a, w_b, w_o, ln1_g, ln1_b, w_ff1, w_ff2, w_pg, w_p, ln2_g, ln2_b, loss_target, m_ln0_g, m_ln0_b, m_w_in, m_hg_lb, m_hg_norm_g, m_fox_fb, m_w_a, m_w_b, m_w_o, m_ln1_g, m_ln1_b, m_w_ff1, m_w_ff2, m_w_pg, m_w_p, m_ln2_g, m_ln2_b, v_ln0_g, v_ln0_b, v_w_in, v_hg_lb, v_hg_norm_g, v_fox_fb, v_w_a, v_w_b, v_w_o, v_ln1_g, v_ln1_b, v_w_ff1, v_w_ff2, v_w_pg, v_w_p, v_ln2_g, v_ln2_b):
    n_seq, seq, d = x.shape
    T = n_seq * seq
    hw = d // 2
    fh = hw // FOX_HDIM
    bh = n_seq * fh
    orig, order, mine, n_in = _win_layout(d)

    big = {"w_in": w_in[0], "w_a": w_a[0], "w_b": w_b[0], "w_o": w_o[0], "w_ff1": w_ff1[0], "w_ff2": w_ff2[0],
           "w_pg": w_pg[0], "w_p": w_p[0]}
    big_m = {"w_in": m_w_in[0], "w_a": m_w_a[0], "w_b": m_w_b[0], "w_o": m_w_o[0], "w_ff1": m_w_ff1[0],
             "w_ff2": m_w_ff2[0], "w_pg": m_w_pg[0], "w_p": m_w_p[0]}
    big_v = {"w_in": v_w_in[0], "w_a": v_w_a[0], "w_b": v_w_b[0], "w_o": v_w_o[0], "w_ff1": v_w_ff1[0],
             "w_ff2": v_w_ff2[0], "w_pg": v_w_pg[0], "w_p": v_w_p[0]}
    names = [nm for nm, _ in BIG_WEIGHTS]
    axis = dict(BIG_WEIGHTS)
    shard_shapes = [big[nm].shape for nm in names]

    gathered = allgather_chips(pack_shards([big[nm].astype(BF16) for nm in names]), name="allgather_weights")
    per_chip = [unpack_shards(gathered[s], shard_shapes) for s in range(4)]
    full = {nm: jnp.concatenate([per_chip[s][k] for s in range(4)], axis=axis[nm]) for k, nm in enumerate(names)}
    win = full["w_in"]
    win_mine = jnp.concatenate(
        [win[:, orig[nm][0]:orig[nm][0] + orig[nm][1]] for nm in order]
        + [jnp.zeros((d, LANES - fh), BF16)], axis=1)

    x2 = x.reshape(T, d)
    tgt = loss_target.reshape(T, d)
    p_b = p.reshape(T, p.shape[-1]).astype(BF16)
    vec = lambda a: a.reshape(1, -1)
    probs = jax.nn.softmax(hg_lb, axis=0)
    lb = vec(probs[0])

    def ln0_fn(rows, vecs):
        h = _ln_stats(rows[0]) * vecs[0] + vecs[1]
        return [h, h], []
    (h0, h0b), _ = rowwise(ln0_fn, [x2], [vec(ln0_g), vec(ln0_b)], [(d, F32), (d, BF16)], name="ln0_fwd")
    proj = matmul_nn(h0b, win_mine, name="in_proj")

    o_raw, hg_states = hgrn2_fwd(proj, [mine["hq"][0], mine["hf"][0], mine["hi"][0]], lb, n_seq, seq, name="hgrn2_fwd")

    def ya_fn(rows, vecs):
        o, hg = rows
        outs = []
        for h in range(HG_HEADS):
            oh = o[:, h * HG_DIM:(h + 1) * HG_DIM]
            outs.append(oh * lax.rsqrt(jnp.mean(oh * oh, axis=-1, keepdims=True) + RMS_EPS))
        y = jnp.concatenate(outs, axis=1) * vecs[0] * (hg * _sigmoid(hg))
        return [y], []
    (y_a,), _ = rowwise(ya_fn, [o_raw, (proj,) + mine["hg"]], [hg_norm_g], [(hw, BF16)], name="hgrn2_out_fwd")

    fb_pad = jnp.concatenate([fox_fb, jnp.zeros((1, LANES - fh), F32)], axis=1)

    def lf_fn(rows, vecs):
        u = rows[0] + vecs[0]
        return [jnp.minimum(u, 0.0) - jnp.log(1.0 + jnp.exp(-jnp.abs(u)))], []
    (lf,), _ = rowwise(lf_fn, [(proj,) + mine["ff"]], [fb_pad], [(LANES, F32)], name="fox_logf")
    c_cum = seq_cumsum(lf, n_seq, seq, reverse=False, name="fox_cumsum")[:, :fh]
    c1, c2, c3 = _split3(c_cum)

    def heads(t2d, width):
        return t2d.reshape(n_seq, seq, fh, width).transpose(0, 2, 1, 3).reshape(bh, seq, width)

    def unheads(t3d, width):
        return t3d.reshape(n_seq, fh, seq, width).transpose(0, 2, 1, 3).reshape(T, fh * width)

    cs = [heads(cc, 1) for cc in (c1, c2, c3)]
    ones = jnp.ones((bh, seq, 3), BF16)
    zpad = jnp.zeros((bh, seq, FOX_AUG - FOX_HDIM - 6), BF16)
    fq = proj[:, mine["fq"][0]:mine["fq"][0] + hw].astype(BF16)
    fk = proj[:, mine["fk"][0]:mine["fk"][0] + hw].astype(BF16)
    fv = proj[:, mine["fv"][0]:mine["fv"][0] + hw].astype(BF16)
    scale = FOX_HDIM ** -0.5
    qa = jnp.concatenate([heads(fq, FOX_HDIM) * jnp.asarray(scale, BF16)] + cs + [ones, zpad], axis=-1)
    ka = jnp.concatenate([heads(fk, FOX_HDIM), ones] + [-cc for cc in cs] + [zpad], axis=-1)
    va = heads(fv, FOX_HDIM)
    o_fox, lse = fox_fwd(qa, ka, va, name="fox_fwd")
    y_b = unheads(o_fox, FOX_HDIM).astype(BF16)

    pa = matmul_nn(y_a, full["w_a"], name="proj_a")
    pb = matmul_nn(y_b, full["w_b"], name="proj_b")

    def merge_fn(rows, vecs):
        ga, gb, a, b = rows
        return [_sigmoid(ga) * a + _sigmoid(gb) * b], []
    (merged,), _ = rowwise(merge_fn, [(proj,) + mine["ga"], (proj,) + mine["gb"], pa, pb], [], [(d, BF16)],
                           name="merge_fwd")
    mix = matmul_nn(merged, full["w_o"], name="out_proj")

    def ln1_fn(rows, vecs):
        z = ALPHA * rows[0] + rows[1]
        h = _ln_stats(z) * vecs[0] + vecs[1]
        return [z, h, h], []
    (z1, h1, h1b), _ = rowwise(ln1_fn, [h0, mix], [ln1_g, ln1_b], [(d, F32), (d, F32), (d, BF16)], name="ln1_fwd")

    relu2 = lambda u: jnp.square(jnp.maximum(u, 0.0))
    act = matmul_nn(h1b, full["w_ff1"], name="ff1", out_dtype=BF16, epilogue=relu2)
    ff = matmul_nn(act, full["w_ff2"], name="ff2")
    pg = matmul_nn(h1b, full["w_pg"], name="ple_gate")
    pe = matmul_nn(p_b, full["w_p"], name="ple_embed")

    def head_fn(rows, vecs):
        h1v, ffv, pgv, pev, t = rows
        g2, b2 = vecs
        sp = _sigmoid(pgv)
        z = ALPHA * h1v + ffv + sp * pev
        y = _ln_stats(z) * g2 + b2
        err = y - t
        loss_rows = 0.5 * jnp.mean(err * err, axis=-1, keepdims=True)
        dy = err * (1.0 / d)
        dz, dg2, db2 = _ln_bwd(z, dy, g2)
        loss_acc = jnp.broadcast_to(_colsum(loss_rows), (1, LANES))
        return [dz, dz, dz * pev * (sp * (1.0 - sp)), dz * sp], [dg2, db2, loss_acc]
    (dz2, dz2b, dpg, dpe), (g_ln2_g, g_ln2_b, loss_part) = rowwise(
        head_fn, [h1, ff, pg, pe, tgt], [ln2_g, ln2_b],
        [(d, F32), (d, BF16), (d, BF16), (d, BF16)], [d, d, LANES], name="head_fwd_bwd")

    wt = {nm: full[nm].T for nm in names if nm != "w_in"}
    dact = lambda da, a: da * (2.0 * jnp.sqrt(a.astype(F32)))
    du = matmul_nn(dz2b, wt["w_ff2"], name="d_ff2", out_dtype=BF16, epilogue=dact, aux=act)
    dh1_ff = matmul_nn(du, wt["w_ff1"], name="d_ff1")
    dh1_pg = matmul_nn(dpg, wt["w_pg"], name="d_ple_gate")

    def ln1_bwd_fn(rows, vecs):
        dh1 = ALPHA * rows[0] + rows[1] + rows[2]
        dz, dg, db = _ln_bwd(rows[3], dh1, vecs[0])
        return [dz, dz], [dg, db]
    (dz1, dz1b), (g_ln1_g, g_ln1_b) = rowwise(ln1_bwd_fn, [dz2, dh1_ff, dh1_pg, z1], [ln1_g],
                                              [(d, F32), (d, BF16)], [d, d], name="ln1_bwd")
    dmerged = matmul_nn(dz1b, wt["w_o"], name="d_out_proj")

    def merge_bwd_fn(rows, vecs):
        dm, ga, gb, a, b = rows
        sa, sb = _sigmoid(ga), _sigmoid(gb)
        return [dm * a * (sa * (1.0 - sa)), dm * b * (sb * (1.0 - sb)), dm * sa, dm * sb], []
    (dga, dgb, dma, dmb), _ = rowwise(merge_bwd_fn, [dmerged, (proj,) + mine["ga"], (proj,) + mine["gb"], pa, pb], [],
                                      [(d, BF16)] * 4, name="merge_bwd")
    dya = matmul_nn(dma, wt["w_a"], name="d_proj_a")
    dyb = matmul_nn(dmb, wt["w_b"], name="d_proj_b", out_dtype=BF16)

    def ya_bwd_fn(rows, vecs):
        o, hg, dy = rows
        ng = vecs[0]
        sg = _sigmoid(hg)
        gate = hg * sg
        dn_parts, do_parts, n_parts = [], [], []
        for h in range(HG_HEADS):
            hs = slice(h * HG_DIM, (h + 1) * HG_DIM)
            oh = o[:, hs]
            r = lax.rsqrt(jnp.mean(oh * oh, axis=-1, keepdims=True) + RMS_EPS)
            nh = oh * r
            dn = dy[:, hs] * ng[:, hs] * gate[:, hs]
            do_parts.append(r * (dn - nh * jnp.mean(dn * nh, axis=-1, keepdims=True)))
            n_parts.append(nh)
        nrm = jnp.concatenate(n_parts, axis=1)
        dhg = dy * nrm * ng * (sg * (1.0 + hg * (1.0 - sg)))
        return [jnp.concatenate(do_parts, axis=1), dhg], [_colsum(dy * nrm * gate)]
    (do_raw, dhg), (g_norm_g,) = rowwise(ya_bwd_fn, [o_raw, (proj,) + mine["hg"], dya], [hg_norm_g],
                                         [(hw, F32), (hw, BF16)], [hw], name="hgrn2_out_bwd")
    dhq, dhf, dhi, g_lb = hgrn2_bwd(proj, [mine["hq"][0], mine["hf"][0], mine["hi"][0]], lb, do_raw, hg_states,
                                    n_seq, seq, name="hgrn2_bwd")

    do_fox = heads(dyb, FOX_HDIM)
    dqa, rsum, delta = fox_bwd_dq(qa, ka, va, do_fox, lse, name="fox_bwd_dq")
    dka, dva, dsum = fox_bwd_dkv(qa, ka, va, do_fox, delta, lse, name="fox_bwd_dkv")
    dfq = (unheads(dqa[:, :, :FOX_HDIM], FOX_HDIM) * scale).astype(BF16)
    dfk = unheads(dka[:, :, :FOX_HDIM], FOX_HDIM).astype(BF16)
    dfv = unheads(dva, FOX_HDIM).astype(BF16)
    dc = unheads(rsum[:, :, 0:1], 1) - dsum.reshape(n_seq, fh, seq).transpose(0, 2, 1).reshape(T, fh)
    dc = jnp.concatenate([dc, jnp.zeros((T, LANES - fh), F32)], axis=1)
    dlf = seq_cumsum(dc, n_seq, seq, reverse=True, name="fox_cumsum_bwd")

    def lf_bwd_fn(rows, vecs):
        u = rows[0] + vecs[0]
        du_ = rows[1] * _sigmoid(-u)
        return [du_], [_colsum(du_)]
    (dff_,), (g_fb,) = rowwise(lf_bwd_fn, [(proj,) + mine["ff"], dlf], [fb_pad], [(LANES, BF16)], [LANES],
                               name="fox_logf_bwd")

    dproj = jnp.concatenate([dga, dgb, dhq, dhf, dhi, dhg, dfq, dfk, dfv, dff_], axis=1)
    dh0_in = matmul_nn(dproj, win_mine.T, name="d_in_proj")

    def ln0_bwd_fn(rows, vecs):
        dh0 = rows[0] + ALPHA * rows[1]
        dx, dg, db = _ln_bwd(rows[2], dh0, vecs[0])
        return [dx], [dg, db]
    (dx,), (g_ln0_g, g_ln0_b) = rowwise(ln0_bwd_fn, [dh0_in, dz1, x2], [vec(ln0_g)], [(d, F32)], [d, d],
                                        name="ln0_bwd")

    gw_in_mine = matmul_tn(h0b, dproj, name="g_w_in")
    gw_in = jnp.concatenate([gw_in_mine[:, mine[nm][0]:mine[nm][0] + orig[nm][1]]
                             for nm in ["hq", "hf", "hi", "hg", "fq", "fk", "fv", "ff", "ga", "gb"]], axis=1)
    gfull = {
        "w_in": gw_in,
        "w_a": matmul_tn(y_a, dma, name="g_w_a"),
        "w_b": matmul_tn(y_b, dmb, name="g_w_b"),
        "w_o": matmul_tn(merged, dz1b, name="g_w_o"),
        "w_ff1": matmul_tn(h1b, du, name="g_w_ff1"),
        "w_ff2": matmul_tn(act, dz2b, name="g_w_ff2"),
        "w_pg": matmul_tn(h1b, dpg, name="g_w_pg"),
        "w_p": matmul_tn(p_b, dpe, name="g_w_p"),
    }

    def chip_parts(nm, s):
        g = gfull[nm]
        n = g.shape[axis[nm]] // 4
        return lax.slice_in_dim(g, s * n, (s + 1) * n, axis=axis[nm])
    packed = jnp.stack([pack_shards([chip_parts(nm, s) for nm in names]) for s in range(4)])
    me = 2 * lax.axis_index("x") + lax.axis_index("y")
    core = lax.axis_index("c")
    n_rows = packed.shape[1]
    half = n_rows // 2
    keep = lax.dynamic_slice_in_dim(packed, core * half, half, axis=1)
    give = lax.dynamic_slice_in_dim(packed, (1 - core) * half, half, axis=1).astype(BF16)
    from_core = swap_cores(give, name="swap_partials")

    def sum2_fn(rows, vecs):
        s = rows[0] + rows[1].astype(F32)
        return [s, s], []
    (pair, pair_b), _ = rowwise(sum2_fn, [keep.reshape(4 * half, PACK_W), from_core.reshape(4 * half, PACK_W)], [],
                                [(PACK_W, F32), (PACK_W, BF16)], name="sum_cores", tm=2 * PACK_TILE)
    got = scatter_chips(pair_b.reshape(4, half, PACK_W), name="scatter_grads")
    own = lax.dynamic_index_in_dim(pair.reshape(4, half, PACK_W), me, axis=0, keepdims=False)

    def sum4_fn(rows, vecs):
        a, r0, r1, r2 = rows
        return [((a + r0.astype(F32)) + r1.astype(F32)) + r2.astype(F32)], []
    (q_half,), _ = rowwise(sum4_fn, [own, got[0], got[1], got[2]], [], [(PACK_W, F32)], name="sum_chips",
                           tm=PACK_TILE // 2)
    q_other = swap_cores(q_half, name="swap_halves")
    g_packed = jnp.concatenate([jnp.where(core == 0, q_half, q_other), jnp.where(core == 0, q_other, q_half)], axis=0)
    g_shards = dict(zip(names, unpack_shards(g_packed, shard_shapes)))

    def row1024(*parts):
        r = jnp.concatenate([q.reshape(1, -1) for q in parts], axis=1)
        return jnp.concatenate([r, jnp.zeros((1, PACK_W - r.shape[1]), F32)], axis=1) if r.shape[1] < PACK_W else r
    small_rows = [row1024(g_ln0_g), row1024(g_ln0_b), row1024(g_ln1_g), row1024(g_ln1_b), row1024(g_ln2_g),
                  row1024(g_ln2_b), row1024(g_norm_g, g_lb), row1024(g_fb[:, :fh], loss_part[:, :1])]
    small = allreduce_small(jnp.concatenate(small_rows, axis=0), name="allreduce_small")
    s_ln0_g, s_ln0_b, s_ln1_g, s_ln1_b, s_ln2_g, s_ln2_b = [small[r:r + 1] for r in range(6)]
    s_norm_g, s_lb = small[6:7, :hw], small[6:7, hw:2 * hw]
    s_fb, loss = small[7:8, :fh], small[7, fh]
    p0 = probs[0:1]
    jac = p0 * (1.0 - p0)
    s_hg_lb = jnp.concatenate([s_lb * jac, -s_lb * jac], axis=0)

    small_w = [vec(ln0_g), vec(ln0_b), ln1_g, ln1_b, ln2_g, ln2_b, hg_lb.reshape(1, -1), hg_norm_g, fox_fb]
    small_g = [s_ln0_g, s_ln0_b, s_ln1_g, s_ln1_b, s_ln2_g, s_ln2_b, s_hg_lb.reshape(1, -1), s_norm_g, s_fb]
    small_m = [vec(m_ln0_g), vec(m_ln0_b), m_ln1_g, m_ln1_b, m_ln2_g, m_ln2_b, m_hg_lb.reshape(1, -1), m_hg_norm_g, m_fox_fb]
    small_v = [vec(v_ln0_g), vec(v_ln0_b), v_ln1_g, v_ln1_b, v_ln2_g, v_ln2_b, v_hg_lb.reshape(1, -1), v_hg_norm_g, v_fox_fb]
    pad_rows = lambda lst, fill: jnp.concatenate(
        [row1024(a) if fill == 0.0 else jnp.concatenate([a.reshape(1, -1), jnp.full((1, PACK_W - a.size), fill, F32)], axis=1)
         for a in lst] + [jnp.full((16 - len(lst), PACK_W), fill, F32)], axis=0)
    sd, sm, sv = adamw(pad_rows(small_w, 0.0), pad_rows(small_g, 0.0), pad_rows(small_m, 0.0), pad_rows(small_v, 1.0),
                       name="adamw_small")
    small_shapes = [ln0_g.shape, ln0_b.shape, ln1_g.shape, ln1_b.shape, ln2_g.shape, ln2_b.shape, hg_lb.shape,
                    hg_norm_g.shape, fox_fb.shape]
    take = lambda buf: [buf[r, :int(np.prod(shp))].reshape(shp) for r, shp in enumerate(small_shapes)]
    sg_out, sd_out, sm_out, sv_out = [g.reshape(shp) for g, shp in zip(small_g, small_shapes)], take(sd), take(sm), take(sv)

    big_out = {}
    for nm in names:
        delta, m2, v2 = adamw(big[nm], g_shards[nm], big_m[nm], big_v[nm], name="adamw_" + nm)
        big_out[nm] = (g_shards[nm][None], delta[None], m2[None], v2[None])

    def ordered(k):
        sm_ = [sg_out, sd_out, sm_out, sv_out][k]
        bg = lambda nm: big_out[nm][k]
        return [sm_[0], sm_[1], bg("w_in"), sm_[6], sm_[7], sm_[8], bg("w_a"), bg("w_b"), bg("w_o"), sm_[2], sm_[3],
                bg("w_ff1"), bg("w_ff2"), bg("w_pg"), bg("w_p"), sm_[4], sm_[5]]
    grad_x = dx.reshape(n_seq, seq, d)
    return (loss, grad_x, *ordered(0), *ordered(1), *ordered(2), *ordered(3))
```

```python
import functools

import numpy as np
import jax
import jax.numpy as jnp
from jax import lax
from jax.experimental import pallas as pl
from jax.experimental.pallas import tpu as pltpu

F32 = jnp.float32
BF16 = jnp.bfloat16
MESH = pl.DeviceIdType.MESH

VMEM_LIMIT_BYTES = 48 * 1024 * 1024
LANES = 128
HG_HEADS = 4
HG_DIM = 128
HG_BLK = 16
HG_TILE = 256
FOX_HDIM = 64
FOX_AUG = 128
FOX_TQ = 1024
LN_EPS = 1e-5
RMS_EPS = 1e-6
DEPTH = 1
ALPHA = (2.0 * DEPTH) ** 0.25
ADAM_LR, ADAM_B1, ADAM_B2, ADAM_EPS, ADAM_WD, ADAM_STEP = 0.001, 0.9, 0.999, 1e-08, 0.01, 10
NEG_INF = -1e30


def _cparams(sem):
    return pltpu.CompilerParams(dimension_semantics=sem, vmem_limit_bytes=VMEM_LIMIT_BYTES)


def _tile(n, cap):
    if n <= cap:
        return n
    best = None
    for t in range(LANES, cap + 1, LANES):
        if n % t == 0:
            best = t
    assert best is not None, (n, cap)
    return best


def matmul_nn(a, w, *, name, out_dtype=F32, epilogue=None, aux=None, tm=1024):
    T, K = a.shape
    K2, N = w.shape
    tm = min(tm, T)
    assert K == K2 and T % tm == 0
    tn = _tile(N, 1152)
    tk = _tile(K, 1152)
    nk = K // tk

    def body(*refs):
        if aux is None:
            a_ref, w_ref, o_ref, acc_ref = refs
            x_ref = None
        else:
            a_ref, w_ref, x_ref, o_ref, acc_ref = refs
        k = pl.program_id(2)
        part = jnp.dot(a_ref[...], w_ref[...], preferred_element_type=F32)

        def write(res):
            if epilogue is not None:
                res = epilogue(res) if x_ref is None else epilogue(res, x_ref[...])
            o_ref[...] = res.astype(out_dtype)

        if nk == 1:
            write(part)
        else:
            @pl.when(k == 0)
            def _():
                acc_ref[...] = part

            @pl.when(k > 0)
            def _():
                acc_ref[...] += part

            @pl.when(k == nk - 1)
            def _():
                write(acc_ref[...])

    in_specs = [pl.BlockSpec((tm, tk), lambda n, m, k: (m, k)),
                pl.BlockSpec((tk, tn), lambda n, m, k: (k, n))]
    args = [a, w]
    if aux is not None:
        in_specs.append(pl.BlockSpec((tm, tn), lambda n, m, k: (m, n)))
        args.append(aux)
    return pl.pallas_call(
        body, name=name,
        grid=(N // tn, T // tm, nk),
        in_specs=in_specs,
        out_specs=pl.BlockSpec((tm, tn), lambda n, m, k: (m, n)),
        out_shape=jax.ShapeDtypeStruct((T, N), out_dtype),
        scratch_shapes=[pltpu.VMEM((tm, tn) if nk > 1 else (8, LANES), F32)],
        compiler_params=_cparams(("parallel", "parallel", "arbitrary")),
    )(*args)


def matmul_tn(a, b, *, name, tk=1024):
    T, M = a.shape
    T2, N = b.shape
    tk = min(tk, T)
    assert T == T2 and T % tk == 0
    tm = _tile(M, 1024)
    tn = _tile(N, 1152)

    def body(a_ref, b_ref, o_ref):
        k = pl.program_id(2)
        part = lax.dot_general(a_ref[...], b_ref[...], (((0,), (0,)), ((), ())), preferred_element_type=F32)

        @pl.when(k == 0)
        def _():
            o_ref[...] = part

        @pl.when(k > 0)
        def _():
            o_ref[...] += part

    return pl.pallas_call(
        body, name=name,
        grid=(M // tm, N // tn, T // tk),
        in_specs=[pl.BlockSpec((tk, tm), lambda m, n, k: (k, m)),
                  pl.BlockSpec((tk, tn), lambda m, n, k: (k, n))],
        out_specs=pl.BlockSpec((tm, tn), lambda m, n, k: (m, n)),
        out_shape=jax.ShapeDtypeStruct((M, N), F32),
        compiler_params=_cparams(("parallel", "parallel", "arbitrary")),
    )(a, b)


def rowwise(fn, rows, vecs, outs, accs=(), *, name, tm=512):
    rows = [r if isinstance(r, tuple) else (r, 0, r.shape[1]) for r in rows]
    T = rows[0][0].shape[0]
    tm = min(tm, T)
    assert T % tm == 0
    n_rows, n_vecs, n_outs, n_accs = len(rows), len(vecs), len(outs), len(accs)

    def body(*refs):
        row_refs = refs[:n_rows]
        vec_refs = refs[n_rows:n_rows + n_vecs]
        out_refs = refs[n_rows + n_vecs:n_rows + n_vecs + n_outs]
        acc_refs = refs[n_rows + n_vecs + n_outs:]
        out_vals, acc_vals = fn([r[...] for r in row_refs], [v[...] for v in vec_refs])
        assert len(out_vals) == n_outs and len(acc_vals) == n_accs
        for r, val in zip(out_refs, out_vals):
            r[...] = val.astype(r.dtype)
        if n_accs:
            i = pl.program_id(0)

            @pl.when(i == 0)
            def _():
                for r in acc_refs:
                    r[...] = jnp.zeros_like(r)

            for r, val in zip(acc_refs, acc_vals):
                r[...] += val

    in_specs = []
    for arr, off, width in rows:
        assert off % width == 0
        in_specs.append(pl.BlockSpec((tm, width), functools.partial(lambda i, blk: (i, blk), blk=off // width)))
    for v in vecs:
        in_specs.append(pl.BlockSpec(v.shape, lambda i: (0, 0)))
    out_specs = [pl.BlockSpec((tm, w), lambda i: (i, 0)) for w, _ in outs]
    out_specs += [pl.BlockSpec((1, w), lambda i: (0, 0)) for w in accs]
    out_shape = [jax.ShapeDtypeStruct((T, w), dt) for w, dt in outs]
    out_shape += [jax.ShapeDtypeStruct((1, w), F32) for w in accs]
    res = pl.pallas_call(
        body, name=name,
        grid=(T // tm,),
        in_specs=in_specs, out_specs=out_specs, out_shape=out_shape,
        compiler_params=_cparams(("arbitrary",) if n_accs else ("parallel",)),
    )(*[r[0] for r in rows], *vecs)
    return res[:n_outs], res[n_outs:]


def _colsum(x):
    return jnp.sum(x, axis=0, keepdims=True)


def _sigmoid(x):
    return 1.0 / (1.0 + jnp.exp(-x))


def _ln_stats(z):
    mu = jnp.mean(z, axis=-1, keepdims=True)
    zc = z - mu
    var = jnp.mean(zc * zc, axis=-1, keepdims=True)
    return zc * lax.rsqrt(var + LN_EPS)


def _ln_bwd(zhat_src, dy, g):
    mu = jnp.mean(zhat_src, axis=-1, keepdims=True)
    zc = zhat_src - mu
    var = jnp.mean(zc * zc, axis=-1, keepdims=True)
    rstd = lax.rsqrt(var + LN_EPS)
    zh = zc * rstd
    dzh = dy * g
    dz = rstd * (dzh - jnp.mean(dzh, axis=-1, keepdims=True) - zh * jnp.mean(dzh * zh, axis=-1, keepdims=True))
    return dz, _colsum(dy * zh), _colsum(dy)


def _hg_constants():
    r = np.arange(HG_TILE)
    same = (r[:, None] // HG_BLK) == (r[None, :] // HG_BLK)
    lower = (same & (r[None, :] <= r[:, None])).astype(np.float32)
    upper = (same & (r[None, :] >= r[:, None])).astype(np.float32)
    total = same.astype(np.float32)
    w = HG_HEADS * HG_DIM
    c = np.arange(w)
    bd = ((c[:, None] // HG_DIM) == (c[None, :] // HG_DIM)).astype(np.float32)
    n = HG_BLK * HG_BLK
    rr = np.arange(n)
    sel_t = (rr[None, :] // HG_BLK == np.arange(HG_BLK)[:, None]).astype(np.float32)
    sel_s = (rr[None, :] % HG_BLK == np.arange(HG_BLK)[:, None]).astype(np.float32)
    as_bf = lambda m: jnp.asarray(m, dtype=BF16)
    return as_bf(lower), as_bf(upper), as_bf(total), as_bf(bd), as_bf(sel_t), as_bf(sel_s)


def _keep_bf16_bits(x):
    bits = lax.bitcast_convert_type(x, jnp.int32) & jnp.int32(-65536)
    return lax.bitcast_convert_type(bits, F32)


def _split3(x):
    hi = _keep_bf16_bits(x)
    r1 = x - hi
    mid = _keep_bf16_bits(r1)
    lo = _keep_bf16_bits(r1 - mid)
    return hi.astype(BF16), mid.astype(BF16), lo.astype(BF16)


def _dot3(m01, x):
    hi, mid, lo = _split3(x)
    d = lambda p: jnp.dot(m01, p, preferred_element_type=F32)
    return (d(lo) + d(mid)) + d(hi)


def _hg_prologue(hq, hf, lb, lower, total):
    sq = _sigmoid(hq)
    q = hq * sq
    sg = _sigmoid(hf)
    f = lb + (1.0 - lb) * sg
    g = jnp.log(f)
    k = 1.0 - f
    b = _dot3(lower, g)
    bl = _dot3(total, g)
    return q, k, f, sg, sq, b, bl


def _stack16(fn):
    return [fn(t) for t in range(HG_BLK)]


def hgrn2_fwd(proj, offs, lb, n_seq, seq, *, name):
    T = n_seq * seq
    W = HG_HEADS * HG_DIM
    n_tiles = seq // HG_TILE
    nb = HG_TILE // HG_BLK
    lower, _, total, bd, sel_t, _ = _hg_constants()

    def body(hq_ref, hf_ref, hi_ref, lb_ref, lower_ref, total_ref, bd_ref, selt_ref,
             o_ref, st_out_ref,
             st_ref, q_s, k_s, v_s, b_s, qt_s, kt_s, d_s, p_s):
        @pl.when(pl.program_id(1) == 0)
        def _():
            st_ref[...] = jnp.zeros_like(st_ref)

        q, k, _, _, _, b, bl = _hg_prologue(hq_ref[...], hf_ref[...], lb_ref[...], lower_ref[...], total_ref[...])
        q_s[...] = q
        k_s[...] = k
        v_s[...] = hi_ref[...]
        b_s[...] = b
        qt_s[...] = q * jnp.exp(b)
        kt_s[...] = k * jnp.exp(jnp.minimum(bl - b, 0.0))
        d_s[...] = jnp.exp(bl)
        rowi = lax.broadcasted_iota(jnp.int32, (HG_BLK, W), 0)

        def block(i, carry):
            r0 = pl.multiple_of(i * HG_BLK, HG_BLK)
            rows = pl.ds(r0, HG_BLK)
            qi, ki, vi, bi = q_s[rows, :], k_s[rows, :], v_s[rows, :], b_s[rows, :]
            for t in range(HG_BLK):
                e = jnp.where(rowi <= t, jnp.exp(jnp.minimum(bi[t:t + 1, :] - bi, 0.0)), 0.0)
                p_s[pl.ds(t * HG_BLK, HG_BLK), :] = (e * qi[t:t + 1, :] * ki).astype(BF16)
            a_b = jnp.dot(p_s[...], bd_ref[...], preferred_element_type=F32)
            vt = jnp.concatenate([vi] * HG_BLK, axis=0)
            o_blk = jnp.dot(selt_ref[...], (a_b * vt).astype(BF16), preferred_element_type=F32)
            qti, kti, di = qt_s[rows, :], kt_s[rows, :], d_s[rows, :]
            outs = []
            for h in range(HG_HEADS):
                hs = slice(h * HG_DIM, (h + 1) * HG_DIM)
                st_h = st_ref[hs, :]
                st_out_ref[i, hs, :] = st_h
                outs.append(lax.dot_general(qti[:, hs].astype(BF16), st_h.astype(BF16),
                                            (((1,), (1,)), ((), ())), preferred_element_type=F32))
                upd = lax.dot_general(vi[:, hs].astype(BF16), kti[:, hs].astype(BF16),
                                      (((0,), (0,)), ((), ())), preferred_element_type=F32)
                st_ref[hs, :] = st_h * di[0:1, hs] + upd
            o_ref[rows, :] = o_blk + jnp.concatenate(outs, axis=1)
            return carry

        lax.fori_loop(0, nb, block, 0, unroll=2)

    col = lambda off: functools.partial(lambda s, t, blk: (s * n_tiles + t, blk), blk=off // W)
    const = lambda m: pl.BlockSpec(m.shape, lambda s, t: (0, 0))
    tile_f32 = pltpu.VMEM((HG_TILE, W), F32)
    o, states = pl.pallas_call(
        body, name=name,
        grid=(n_seq, n_tiles),
        in_specs=[pl.BlockSpec((HG_TILE, W), col(offs[0])), pl.BlockSpec((HG_TILE, W), col(offs[1])),
                  pl.BlockSpec((HG_TILE, W), col(offs[2])), const(lb), const(lower), const(total), const(bd),
                  const(sel_t)],
        out_specs=[pl.BlockSpec((HG_TILE, W), lambda s, t: (s * n_tiles + t, 0)),
                   pl.BlockSpec((nb, W, HG_DIM), lambda s, t: (s * n_tiles + t, 0, 0))],
        out_shape=[jax.ShapeDtypeStruct((T, W), F32), jax.ShapeDtypeStruct((T // HG_BLK, W, HG_DIM), F32)],
        scratch_shapes=[pltpu.VMEM((W, HG_DIM), F32)] + [tile_f32] * 7
                       + [pltpu.VMEM((HG_BLK * HG_BLK, W), BF16)],
        compiler_params=_cparams(("arbitrary", "arbitrary")),
    )(proj, proj, proj, lb, lower, total, bd, sel_t)
    return o, states


def hgrn2_bwd(proj, offs, lb, do, states, n_seq, seq, *, name):
    T = n_seq * seq
    W = HG_HEADS * HG_DIM
    n_tiles = seq // HG_TILE
    nb = HG_TILE // HG_BLK
    lower, upper, total, bd, sel_t, sel_s = _hg_constants()

    def body(hq_ref, hf_ref, hi_ref, do_ref, st_in_ref, lb_ref, lower_ref, upper_ref, total_ref, bd_ref,
             selt_ref, sels_ref,
             dhq_ref, dhf_ref, dhi_ref, dlb_ref,
             dst_ref, q_s, k_s, v_s, b_s, qt_s, kt_s, d_s, eb_s, ekb_s, dq_s, dk_s, db_s, dv_s,
             p_s, e_s, w_s):
        first = jnp.logical_and(pl.program_id(0) == 0, pl.program_id(1) == 0)

        @pl.when(first)
        def _():
            dlb_ref[...] = jnp.zeros_like(dlb_ref)

        @pl.when(pl.program_id(1) == 0)
        def _():
            dst_ref[...] = jnp.zeros_like(dst_ref)

        hq, lbv = hq_ref[...], lb_ref[...]
        q, k, f, sg, sq, b, bl = _hg_prologue(hq, hf_ref[...], lbv, lower_ref[...], total_ref[...])
        eb = jnp.exp(b)
        ekb = jnp.exp(jnp.minimum(bl - b, 0.0))
        q_s[...] = q
        k_s[...] = k
        v_s[...] = hi_ref[...]
        b_s[...] = b
        eb_s[...] = eb
        ekb_s[...] = ekb
        qt_s[...] = q * eb
        kt_s[...] = k * ekb
        d_s[...] = jnp.exp(bl)
        rowi = lax.broadcasted_iota(jnp.int32, (HG_BLK, W), 0)
        last_row = rowi == HG_BLK - 1

        def block(j, carry):
            i = nb - 1 - j
            r0 = pl.multiple_of(i * HG_BLK, HG_BLK)
            rows = pl.ds(r0, HG_BLK)
            qi, ki, vi, bi, doi = q_s[rows, :], k_s[rows, :], v_s[rows, :], b_s[rows, :], do_ref[rows, :]
            for t in range(HG_BLK):
                sl = pl.ds(t * HG_BLK, HG_BLK)
                e = jnp.where(rowi <= t, jnp.exp(jnp.minimum(bi[t:t + 1, :] - bi, 0.0)), 0.0)
                e_s[sl, :] = e
                p_s[sl, :] = (e * qi[t:t + 1, :] * ki).astype(BF16)
                w_s[sl, :] = (doi[t:t + 1, :] * vi).astype(BF16)
            a_b = jnp.dot(p_s[...], bd_ref[...], preferred_element_type=F32)
            da_b = jnp.dot(w_s[...], bd_ref[...], preferred_element_type=F32)
            x = da_b * e_s[...]
            k_til = jnp.concatenate([ki] * HG_BLK, axis=0)
            q_rep = jnp.concatenate([jnp.broadcast_to(qi[t:t + 1, :], (HG_BLK, W)) for t in range(HG_BLK)], axis=0)
            do_rep = jnp.concatenate([jnp.broadcast_to(doi[t:t + 1, :], (HG_BLK, W)) for t in range(HG_BLK)], axis=0)
            dq_in = jnp.dot(selt_ref[...], (x * k_til).astype(BF16), preferred_element_type=F32)
            dk_in = jnp.dot(sels_ref[...], (x * q_rep).astype(BF16), preferred_element_type=F32)
            dv_in = jnp.dot(sels_ref[...], (a_b * do_rep).astype(BF16), preferred_element_type=F32)
            qti, kti, di = qt_s[rows, :], kt_s[rows, :], d_s[rows, :]
            dqt, dkt, dvt, dd = [], [], [], []
            for h in range(HG_HEADS):
                hs = slice(h * HG_DIM, (h + 1) * HG_DIM)
                st_h = st_in_ref[i, hs, :]
                dst_h = dst_ref[hs, :]
                do_h, v_h = doi[:, hs].astype(BF16), vi[:, hs].astype(BF16)
                dst_b = dst_h.astype(BF16)
                dqt.append(jnp.dot(do_h, st_h.astype(BF16), preferred_element_type=F32))
                dkt.append(jnp.dot(v_h, dst_b, preferred_element_type=F32))
                dvt.append(lax.dot_general(kti[:, hs].astype(BF16), dst_b, (((1,), (1,)), ((), ())),
                                           preferred_element_type=F32))
                dd.append(jnp.sum(dst_h * st_h, axis=0, keepdims=True))
                upd = lax.dot_general(do_h, qti[:, hs].astype(BF16), (((0,), (0,)), ((), ())),
                                      preferred_element_type=F32)
                dst_ref[hs, :] = dst_h * di[0:1, hs] + upd
            dqt = jnp.concatenate(dqt, axis=1)
            dkt = jnp.concatenate(dkt, axis=1)
            dvt = jnp.concatenate(dvt, axis=1)
            dd = jnp.concatenate(dd, axis=1)
            dbl = jnp.sum(dkt * kti, axis=0, keepdims=True) + dd * di[0:1, :]
            db = qi * dq_in - ki * dk_in + dqt * qti - dkt * kti
            db_s[rows, :] = db + jnp.where(last_row, dbl, 0.0)
            dq_s[rows, :] = dq_in + dqt * eb_s[rows, :]
            dk_s[rows, :] = dk_in + dkt * ekb_s[rows, :]
            dv_s[rows, :] = dv_in + dvt
            return carry

        lax.fori_loop(0, nb, block, 0, unroll=2)

        dg = _dot3(upper_ref[...], db_s[...])
        dhq_ref[...] = (dq_s[...] * (sq * (1.0 + hq * (1.0 - sq)))).astype(dhq_ref.dtype)
        df = dg / f - dk_s[...]
        dhf_ref[...] = (df * (1.0 - lbv) * (sg * (1.0 - sg))).astype(dhf_ref.dtype)
        dhi_ref[...] = dv_s[...].astype(dhi_ref.dtype)
        dlb_ref[...] += _colsum(df * (1.0 - sg))

    rev = lambda s, t: s * n_tiles + (n_tiles - 1 - t)
    col = lambda off: functools.partial(lambda s, t, blk: (rev(s, t), blk), blk=off // W)
    const = lambda m: pl.BlockSpec(m.shape, lambda s, t: (0, 0))
    row = pl.BlockSpec((HG_TILE, W), lambda s, t: (rev(s, t), 0))
    tile_f32 = pltpu.VMEM((HG_TILE, W), F32)
    n2 = HG_BLK * HG_BLK
    return pl.pallas_call(
        body, name=name,
        grid=(n_seq, n_tiles),
        in_specs=[pl.BlockSpec((HG_TILE, W), col(offs[0])), pl.BlockSpec((HG_TILE, W), col(offs[1])),
                  pl.BlockSpec((HG_TILE, W), col(offs[2])), row,
                  pl.BlockSpec((nb, W, HG_DIM), lambda s, t: (rev(s, t), 0, 0)),
                  const(lb), const(lower), const(upper), const(total), const(bd), const(sel_t), const(sel_s)],
        out_specs=[row, row, row, pl.BlockSpec((1, W), lambda s, t: (0, 0))],
        out_shape=[jax.ShapeDtypeStruct((T, W), BF16)] * 3 + [jax.ShapeDtypeStruct((1, W), F32)],
        scratch_shapes=[pltpu.VMEM((W, HG_DIM), F32)] + [tile_f32] * 13
                       + [pltpu.VMEM((n2, W), BF16), pltpu.VMEM((n2, W), F32), pltpu.VMEM((n2, W), BF16)],
        compiler_params=_cparams(("arbitrary", "arbitrary")),
    )(proj, proj, proj, do, states, lb, lower, upper, total, bd, sel_t, sel_s)


def _diag_mask(tq):
    return lax.broadcasted_iota(jnp.int32, (tq, tq), 1) <= lax.broadcasted_iota(jnp.int32, (tq, tq), 0)


FOX_ROW_GROUPS = 1


def _row_groups(tq):
    g = tq // FOX_ROW_GROUPS
    return [slice(r * g, (r + 1) * g) for r in range(FOX_ROW_GROUPS)]


def _qk(q, k):
    return lax.dot_general(q, k, (((1,), (1,)), ((), ())), preferred_element_type=F32)


def _causal_pairs(n, sweeps=1, by_key=False):
    if by_key:
        rows = [(i, j, 0) for j in range(n) for i in range(j, n)]
    else:
        rows = [(i, j, s) for i in range(n) for s in range(sweeps) for j in range(i + 1)]
    return tuple(jnp.asarray(np.array([r[c] for r in rows], np.int32)) for c in range(3))


def fox_fwd(qa, ka, v, *, name):
    BH, S, _ = qa.shape
    tq = min(FOX_TQ, S)
    itab, jtab, _ = _causal_pairs(S // tq)

    def body(itab_ref, jtab_ref, q_ref, k_ref, v_ref, o_ref, lse_ref, m_s, l_s, acc_s):
        t = pl.program_id(1)
        i, j = itab_ref[t], jtab_ref[t]

        @pl.when(j == 0)
        def _():
            m_s[...] = jnp.full_like(m_s, NEG_INF)
            l_s[...] = jnp.zeros_like(l_s)
            acc_s[...] = jnp.zeros_like(acc_s)

        def step(on_diagonal):
            for rows in _row_groups(tq):
                s = _qk(q_ref[rows, :], k_ref[...])
                if on_diagonal:
                    s = jnp.where(_diag_mask(tq)[rows, :], s, NEG_INF)
                m_prev = m_s[rows, :]
                m_new = jnp.maximum(m_prev, jnp.max(s, axis=-1, keepdims=True))
                alpha = jnp.exp(m_prev - m_new)
                p = jnp.exp(s - m_new[:, 0:1])
                l_s[rows, :] = alpha * l_s[rows, :] + jnp.sum(p, axis=-1, keepdims=True)
                acc_s[rows, :] = alpha[:, 0:FOX_HDIM] * acc_s[rows, :] + jnp.dot(p.astype(BF16), v_ref[...],
                                                                                 preferred_element_type=F32)
                m_s[rows, :] = m_new

        @pl.when(j < i)
        def _():
            step(False)

        @pl.when(j == i)
        def _():
            step(True)
            o_ref[...] = acc_s[...] / l_s[:, 0:FOX_HDIM]
            lse_ref[...] = m_s[...] + jnp.log(l_s[...])

    qspec = lambda d: pl.BlockSpec((None, tq, d), lambda b, t, it, jt: (b, it[t], 0))
    kspec = lambda d: pl.BlockSpec((None, tq, d), lambda b, t, it, jt: (b, jt[t], 0))
    return pl.pallas_call(
        body, name=name,
        grid_spec=pltpu.PrefetchScalarGridSpec(
            num_scalar_prefetch=2, grid=(BH, itab.shape[0]),
            in_specs=[qspec(FOX_AUG), kspec(FOX_AUG), kspec(FOX_HDIM)],
            out_specs=[qspec(FOX_HDIM), qspec(LANES)],
            scratch_shapes=[pltpu.VMEM((tq, LANES), F32), pltpu.VMEM((tq, LANES), F32),
                            pltpu.VMEM((tq, FOX_HDIM), F32)]),
        out_shape=[jax.ShapeDtypeStruct((BH, S, FOX_HDIM), F32), jax.ShapeDtypeStruct((BH, S, LANES), F32)],
        compiler_params=_cparams(("parallel", "arbitrary")),
    )(itab, jtab, qa, ka, v)


def _fox_p_dp(q, k, v, do, lse, on_diagonal):
    s = _qk(q, k)
    if on_diagonal:
        s = jnp.where(_diag_mask(s.shape[0]), s, NEG_INF)
    return jnp.exp(s - lse[:, 0:1]), _qk(do, v)


def fox_bwd_dq(qa, ka, v, do, lse, *, name):
    BH, S, _ = qa.shape
    tq = min(FOX_TQ, S)
    itab, jtab, stab = _causal_pairs(S // tq, sweeps=2)

    def body(itab_ref, jtab_ref, stab_ref, q_ref, k_ref, v_ref, do_ref, lse_ref, dq_ref, rsum_ref, delta_ref):
        t = pl.program_id(1)
        i, j, sweep = itab_ref[t], jtab_ref[t], stab_ref[t]

        @pl.when(jnp.logical_and(j == 0, sweep == 0))
        def _():
            dq_ref[...] = jnp.zeros_like(dq_ref)
            rsum_ref[...] = jnp.zeros_like(rsum_ref)
            delta_ref[...] = jnp.zeros_like(delta_ref)

        def step(on_diagonal, second):
            p, dp = _fox_p_dp(q_ref[...], k_ref[...], v_ref[...], do_ref[...], lse_ref[...], on_diagonal)
            if not second:
                delta_ref[...] += jnp.sum(p * dp, axis=-1, keepdims=True)
            else:
                ds = p * (dp - delta_ref[:, 0:1])
                dq_ref[...] += jnp.dot(ds.astype(BF16), k_ref[...], preferred_element_type=F32)
                rsum_ref[...] += jnp.sum(ds, axis=-1, keepdims=True)

        for on_diagonal in (False, True):
            for second in (False, True):
                cond = jnp.logical_and((j == i) if on_diagonal else (j < i), sweep == int(second))
                pl.when(cond)(functools.partial(step, on_diagonal, second))

    qspec = lambda d: pl.BlockSpec((None, tq, d), lambda b, t, it, jt, st: (b, it[t], 0))
    kspec = lambda d: pl.BlockSpec((None, tq, d), lambda b, t, it, jt, st: (b, jt[t], 0))
    return pl.pallas_call(
        body, name=name,
        grid_spec=pltpu.PrefetchScalarGridSpec(
            num_scalar_prefetch=3, grid=(BH, itab.shape[0]),
            in_specs=[qspec(FOX_AUG), kspec(FOX_AUG), kspec(FOX_HDIM), qspec(FOX_HDIM), qspec(LANES)],
            out_specs=[qspec(FOX_AUG), qspec(LANES), qspec(LANES)]),
        out_shape=[jax.ShapeDtypeStruct((BH, S, FOX_AUG), F32), jax.ShapeDtypeStruct((BH, S, LANES), F32),
                   jax.ShapeDtypeStruct((BH, S, LANES), F32)],
        compiler_params=_cparams(("parallel", "arbitrary")),
    )(itab, jtab, stab, qa, ka, v, do, lse)


def fox_bwd_dkv(qa, ka, v, do, delta, lse, *, name):
    BH, S, _ = qa.shape
    tq = min(FOX_TQ, S)
    itab, jtab, _ = _causal_pairs(S // tq, by_key=True)

    def body(itab_ref, jtab_ref, q_ref, k_ref, v_ref, do_ref, delta_ref, lse_ref, dk_ref, dv_ref, dsum_ref):
        t = pl.program_id(1)
        i, j = itab_ref[t], jtab_ref[t]

        def step(on_diagonal):
            q, do = q_ref[...], do_ref[...]
            p, dp = _fox_p_dp(q, k_ref[...], v_ref[...], do, lse_ref[...], on_diagonal)
            ds = p * (dp - delta_ref[:, 0:1])
            tn = (((0,), (0,)), ((), ()))
            dv = lax.dot_general(p.astype(BF16), do, tn, preferred_element_type=F32)
            dk = lax.dot_general(ds.astype(BF16), q, tn, preferred_element_type=F32)
            if on_diagonal:
                dv_ref[...], dk_ref[...], dsum_ref[...] = dv, dk, _colsum(ds)
            else:
                dv_ref[...] += dv
                dk_ref[...] += dk
                dsum_ref[...] += _colsum(ds)

        @pl.when(i == j)
        def _():
            step(True)

        @pl.when(i > j)
        def _():
            step(False)

    qspec = lambda d: pl.BlockSpec((None, tq, d), lambda b, t, it, jt: (b, it[t], 0))
    kspec = lambda d: pl.BlockSpec((None, tq, d), lambda b, t, it, jt: (b, jt[t], 0))
    return pl.pallas_call(
        body, name=name,
        grid_spec=pltpu.PrefetchScalarGridSpec(
            num_scalar_prefetch=2, grid=(BH, itab.shape[0]),
            in_specs=[qspec(FOX_AUG), kspec(FOX_AUG), kspec(FOX_HDIM), qspec(FOX_HDIM), qspec(LANES), qspec(LANES)],
            out_specs=[kspec(FOX_AUG), kspec(FOX_HDIM),
                       pl.BlockSpec((None, 1, tq), lambda b, t, it, jt: (b, 0, jt[t]))]),
        out_shape=[jax.ShapeDtypeStruct((BH, S, FOX_AUG), F32), jax.ShapeDtypeStruct((BH, S, FOX_HDIM), F32),
                   jax.ShapeDtypeStruct((BH, 1, S), F32)],
        compiler_params=_cparams(("parallel", "arbitrary")),
    )(itab, jtab, qa, ka, v, do, delta, lse)


def seq_cumsum(x, n_seq, seq, *, reverse, name):
    T, C = x.shape
    tb = min(256, seq)
    n = seq // tb
    r = np.arange(tb)
    tri = (r[None, :] >= r[:, None]) if reverse else (r[None, :] <= r[:, None])
    tri = jnp.asarray(tri.astype(np.float32), dtype=BF16)

    def body(x_ref, tri_ref, o_ref, carry_s):
        @pl.when(pl.program_id(1) == 0)
        def _():
            carry_s[...] = jnp.zeros_like(carry_s)

        xv = x_ref[...]
        o_ref[...] = _dot3(tri_ref[...], xv) + carry_s[...]
        carry_s[...] += _colsum(xv)

    blk = (lambda s, t: (s * n + (n - 1 - t), 0)) if reverse else (lambda s, t: (s * n + t, 0))
    return pl.pallas_call(
        body, name=name,
        grid=(n_seq, n),
        in_specs=[pl.BlockSpec((tb, C), blk), pl.BlockSpec((tb, tb), lambda s, t: (0, 0))],
        out_specs=pl.BlockSpec((tb, C), blk),
        out_shape=jax.ShapeDtypeStruct((T, C), F32),
        scratch_shapes=[pltpu.VMEM((1, C), F32)],
        compiler_params=_cparams(("arbitrary", "arbitrary")),
    )(x, tri)


def _place():
    return lax.axis_index("x"), lax.axis_index("y"), lax.axis_index("c")


def _other_chips(x, y):
    return [(1 - x, y), (x, 1 - y), (1 - x, 1 - y)]


def allgather_chips(shard, *, name):
    R, C = shard.shape
    H = R // 2
    assert R % (2 * ROW_ALIGN) == 0

    def body(x_ref, o_ref, send_sems, recv_sems, local_sem):
        x, y, c = _place()
        me = 2 * x + y
        mine = pl.ds(c * H, H)
        own = pltpu.make_async_copy(x_ref, o_ref.at[me], local_sem)
        own.start()
        chips = _other_chips(x, y)

        theirs = pl.ds((1 - c) * H, H)

        def copy(k, src, chip, rows, to):
            return pltpu.make_async_remote_copy(src_ref=src, dst_ref=o_ref.at[2 * chip[0] + chip[1], rows],
                                                send_sem=send_sems.at[k], recv_sem=recv_sems.at[k],
                                                device_id=to, device_id_type=MESH)

        first = [copy(j, x_ref.at[mine], (x, y), mine, (*chip, c)) for j, chip in enumerate(chips)]
        for cp in first:
            cp.start()
        passed = [copy(3 + j, o_ref.at[2 * chip[0] + chip[1], mine], chip, mine, (x, y, 1 - c))
                  for j, chip in enumerate(chips)]
        for j, chip in enumerate(chips):
            copy(j, x_ref.at[mine], chip, mine, (*chip, c)).wait_recv()
            passed[j].start()
        for j, chip in enumerate(chips):
            copy(3 + j, x_ref.at[mine], chip, theirs, (x, y, 1 - c)).wait_recv()
        for cp in first + passed:
            cp.wait_send()
        own.wait()

    return pl.pallas_call(
        body, name=name,
        in_specs=[pl.BlockSpec(memory_space=pl.ANY)],
        out_specs=pl.BlockSpec(memory_space=pl.ANY),
        out_shape=jax.ShapeDtypeStruct((4, R, C), shard.dtype),
        scratch_shapes=[pltpu.SemaphoreType.DMA((6,)), pltpu.SemaphoreType.DMA((6,)), pltpu.SemaphoreType.DMA(())],
        compiler_params=pltpu.CompilerParams(has_side_effects=True),
    )(shard)


def scatter_chips(parts, *, name):
    _, R, C = parts.shape

    def body(x_ref, o_ref, send_sems, recv_sems):
        x, y, c = _place()
        sends = []
        for j, (px, py) in enumerate(_other_chips(x, y)):
            cp = pltpu.make_async_remote_copy(src_ref=x_ref.at[2 * px + py], dst_ref=o_ref.at[j],
                                              send_sem=send_sems.at[j], recv_sem=recv_sems.at[j],
                                              device_id=(px, py, c), device_id_type=MESH)
            cp.start()
            sends.append(cp)
        for cp in sends:
            cp.wait_recv()
        for cp in sends:
            cp.wait_send()

    return pl.pallas_call(
        body, name=name,
        in_specs=[pl.BlockSpec(memory_space=pl.ANY)],
        out_specs=pl.BlockSpec(memory_space=pl.ANY),
        out_shape=jax.ShapeDtypeStruct((3, R, C), parts.dtype),
        scratch_shapes=[pltpu.SemaphoreType.DMA((3,)), pltpu.SemaphoreType.DMA((3,))],
        compiler_params=pltpu.CompilerParams(has_side_effects=True),
    )(parts)


def swap_cores(v, *, name):
    def body(x_ref, o_ref, send_sem, recv_sem):
        x, y, c = _place()
        cp = pltpu.make_async_remote_copy(src_ref=x_ref, dst_ref=o_ref, send_sem=send_sem, recv_sem=recv_sem,
                                          device_id=(x, y, 1 - c), device_id_type=MESH)
        cp.start()
        cp.wait()

    return pl.pallas_call(
        body, name=name,
        in_specs=[pl.BlockSpec(memory_space=pl.ANY)],
        out_specs=pl.BlockSpec(memory_space=pl.ANY),
        out_shape=jax.ShapeDtypeStruct(v.shape, v.dtype),
        scratch_shapes=[pltpu.SemaphoreType.DMA(()), pltpu.SemaphoreType.DMA(())],
        compiler_params=pltpu.CompilerParams(has_side_effects=True),
    )(v)


def allreduce_small(v, *, name):
    R, C = v.shape

    def body(x_ref, o_ref, gath_ref, send_sems, recv_sems):
        x, y, c = _place()
        me = 4 * x + 2 * y + c
        gath_ref[me] = x_ref[...]
        flips = [(k >> 2 & 1, k >> 1 & 1, k & 1) for k in range(1, 8)]
        sends = []
        for j, (fx, fy, fc) in enumerate(flips):
            peer = (x ^ fx, y ^ fy, c ^ fc)
            cp = pltpu.make_async_remote_copy(src_ref=x_ref, dst_ref=gath_ref.at[me], send_sem=send_sems.at[j],
                                              recv_sem=recv_sems.at[j], device_id=peer, device_id_type=MESH)
            cp.start()
            sends.append(cp)
        for j, (fx, fy, fc) in enumerate(flips):
            peer = (x ^ fx, y ^ fy, c ^ fc)
            pltpu.make_async_remote_copy(src_ref=x_ref, dst_ref=gath_ref.at[4 * peer[0] + 2 * peer[1] + peer[2]],
                                         send_sem=send_sems.at[j], recv_sem=recv_sems.at[j], device_id=peer,
                                         device_id_type=MESH).wait_recv()
        for cp in sends:
            cp.wait_send()
        total = gath_ref[0]
        for d in range(1, 8):
            total = total + gath_ref[d]
        o_ref[...] = total

    vm = pl.BlockSpec(memory_space=pltpu.VMEM)
    out, _ = pl.pallas_call(
        body, name=name,
        in_specs=[vm], out_specs=[vm, vm],
        out_shape=[jax.ShapeDtypeStruct((R, C), F32), jax.ShapeDtypeStruct((8, R, C), F32)],
        scratch_shapes=[pltpu.SemaphoreType.DMA((7,)), pltpu.SemaphoreType.DMA((7,))],
        compiler_params=pltpu.CompilerParams(has_side_effects=True),
    )(v)
    return out


PACK_W = 1024
ROW_ALIGN = 16
PACK_TILE = 256
BIG_WEIGHTS = (("w_in", 1), ("w_a", 1), ("w_b", 1), ("w_o", 0), ("w_ff1", 1), ("w_ff2", 0), ("w_pg", 0), ("w_p", 1))


def _pack_rows(shape):
    n = shape[0] * shape[1]
    assert n % PACK_W == 0
    rows = n // PACK_W
    return rows, -(-rows // ROW_ALIGN) * ROW_ALIGN


def pack_shards(shards):
    parts = []
    for s in shards:
        rows, padded = _pack_rows(s.shape)
        m = s.reshape(rows, PACK_W)
        if padded != rows:
            m = jnp.concatenate([m, jnp.zeros((padded - rows, PACK_W), m.dtype)], axis=0)
        parts.append(m)
    total = sum(q.shape[0] for q in parts)
    tail = -total % PACK_TILE
    if tail:
        parts.append(jnp.zeros((tail, PACK_W), parts[0].dtype))
    return jnp.concatenate(parts, axis=0)


def unpack_shards(buf, shapes):
    out, off = [], 0
    for shp in shapes:
        rows, padded = _pack_rows(shp)
        out.append(buf[off:off + rows].reshape(shp))
        off += padded
    return out


def _win_layout(d):
    hw = d // 2
    fh = hw // FOX_HDIM
    orig = {"hq": (0, hw), "hf": (hw, hw), "hi": (2 * hw, hw), "hg": (3 * hw, hw), "fq": (4 * hw, hw),
            "fk": (5 * hw, hw), "fv": (6 * hw, hw), "ff": (7 * hw, fh), "ga": (7 * hw + fh, d), "gb": (7 * hw + fh + d, d)}
    order = ["ga", "gb", "hq", "hf", "hi", "hg", "fq", "fk", "fv", "ff"]
    mine, off = {}, 0
    for nm in order:
        width = orig[nm][1] if nm != "ff" else LANES
        mine[nm] = (off, width)
        off += width
    return orig, order, mine, off


def _adam_fn(rows, vecs):
    w, g, m, v = rows
    m2 = ADAM_B1 * m + (1.0 - ADAM_B1) * g
    v2 = ADAM_B2 * v + (1.0 - ADAM_B2) * (g * g)
    m_hat = m2 / (1.0 - ADAM_B1 ** ADAM_STEP)
    v_hat = v2 / (1.0 - ADAM_B2 ** ADAM_STEP)
    delta = -ADAM_LR * (m_hat / (jnp.sqrt(v_hat) + ADAM_EPS) + ADAM_WD * w)
    return [delta, m2, v2], []


def adamw(w, g, m, v, *, name):
    c = w.shape[1]
    (delta, m2, v2), _ = rowwise(_adam_fn, [w, g, m, v], [], [(c, F32)] * 3, name=name, tm=256)
    return delta, m2, v2


def kernel(x, p, ln0_g, ln0_b, w_in, hg_lb, hg_norm_g, fox_fb, w_a, w_b, w_o, ln1_g, ln1_b, w_ff1, w_ff2, w_pg, w_p, ln2_g, ln2_b, loss_target, m_ln0_g, m_ln0_b, m_w_in, m_hg_lb, m_hg_norm_g, m_fox_fb, m_w_a, m_w_b, m_w_o, m_ln1_g, m_ln1_b, m_w_ff1, m_w_ff2, m_w_pg, m_w_p, m_ln2_g, m_ln2_b, v_ln0_g, v_ln0_b, v_w_in, v_hg_lb, v_hg_norm_g, v_fox_fb, v_w_a, v_w_b, v_w_o, v_ln1_g, v_ln1_b, v_w_ff1, v_w_ff2, v_w_pg, v_w_p, v_ln2_g, v_ln2_b):
    n_seq, seq, d = x.shape
    T = n_seq * seq
    hw = d // 2
    fh = hw // FOX_HDIM
    bh = n_seq * fh
    orig, order, mine, n_in = _win_layout(d)

    big = {"w_in": w_in[0], "w_a": w_a[0], "w_b": w_b[0], "w_o": w_o[0], "w_ff1": w_ff1[0], "w_ff2": w_ff2[0],
           "w_pg": w_pg[0], "w_p": w_p[0]}
    big_m = {"w_in": m_w_in[0], "w_a": m_w_a[0], "w_b": m_w_b[0], "w_o": m_w_o[0], "w_ff1": m_w_ff1[0],
             "w_ff2": m_w_ff2[0], "w_pg": m_w_pg[0], "w_p": m_w_p[0]}
    big_v = {"w_in": v_w_in[0], "w_a": v_w_a[0], "w_b": v_w_b[0], "w_o": v_w_o[0], "w_ff1": v_w_ff1[0],
             "w_ff2": v_w_ff2[0], "w_pg": v_w_pg[0], "w_p": v_w_p[0]}
    names = [nm for nm, _ in BIG_WEIGHTS]
    axis = dict(BIG_WEIGHTS)
    shard_shapes = [big[nm].shape for nm in names]

    gathered = allgather_chips(pack_shards([big[nm].astype(BF16) for nm in names]), name="allgather_weights")
    per_chip = [unpack_shards(gathered[s], shard_shapes) for s in range(4)]
    full = {nm: jnp.concatenate([per_chip[s][k] for s in range(4)], axis=axis[nm]) for k, nm in enumerate(names)}
    win = full["w_in"]
    win_mine = jnp.concatenate(
        [win[:, orig[nm][0]:orig[nm][0] + orig[nm][1]] for nm in order]
        + [jnp.zeros((d, LANES - fh), BF16)], axis=1)

    x2 = x.reshape(T, d)
    tgt = loss_target.reshape(T, d)
    p_b = p.reshape(T, p.shape[-1]).astype(BF16)
    vec = lambda a: a.reshape(1, -1)
    probs = jax.nn.softmax(hg_lb, axis=0)
    lb = vec(probs[0])

    def ln0_fn(rows, vecs):
        h = _ln_stats(rows[0]) * vecs[0] + vecs[1]
        return [h, h], []
    (h0, h0b), _ = rowwise(ln0_fn, [x2], [vec(ln0_g), vec(ln0_b)], [(d, F32), (d, BF16)], name="ln0_fwd")
    proj = matmul_nn(h0b, win_mine, name="in_proj")

    o_raw, hg_states = hgrn2_fwd(proj, [mine["hq"][0], mine["hf"][0], mine["hi"][0]], lb, n_seq, seq, name="hgrn2_fwd")

    def ya_fn(rows, vecs):
        o, hg = rows
        outs = []
        for h in range(HG_HEADS):
            oh = o[:, h * HG_DIM:(h + 1) * HG_DIM]
            outs.append(oh * lax.rsqrt(jnp.mean(oh * oh, axis=-1, keepdims=True) + RMS_EPS))
        y = jnp.concatenate(outs, axis=1) * vecs[0] * (hg * _sigmoid(hg))
        return [y], []
    (y_a,), _ = rowwise(ya_fn, [o_raw, (proj,) + mine["hg"]], [hg_norm_g], [(hw, BF16)], name="hgrn2_out_fwd")

    fb_pad = jnp.concatenate([fox_fb, jnp.zeros((1, LANES - fh), F32)], axis=1)

    def lf_fn(rows, vecs):
        u = rows[0] + vecs[0]
        return [jnp.minimum(u, 0.0) - jnp.log(1.0 + jnp.exp(-jnp.abs(u)))], []
    (lf,), _ = rowwise(lf_fn, [(proj,) + mine["ff"]], [fb_pad], [(LANES, F32)], name="fox_logf")
    c_cum = seq_cumsum(lf, n_seq, seq, reverse=False, name="fox_cumsum")[:, :fh]
    c1, c2, c3 = _split3(c_cum)

    def heads(t2d, width):
        return t2d.reshape(n_seq, seq, fh, width).transpose(0, 2, 1, 3).reshape(bh, seq, width)

    def unheads(t3d, width):
        return t3d.reshape(n_seq, fh, seq, width).transpose(0, 2, 1, 3).reshape(T, fh * width)

    cs = [heads(cc, 1) for cc in (c1, c2, c3)]
    ones = jnp.ones((bh, seq, 3), BF16)
    zpad = jnp.zeros((bh, seq, FOX_AUG - FOX_HDIM - 6), BF16)
    fq = proj[:, mine["fq"][0]:mine["fq"][0] + hw].astype(BF16)
    fk = proj[:, mine["fk"][0]:mine["fk"][0] + hw].astype(BF16)
    fv = proj[:, mine["fv"][0]:mine["fv"][0] + hw].astype(BF16)
    scale = FOX_HDIM ** -0.5
    qa = jnp.concatenate([heads(fq, FOX_HDIM) * jnp.asarray(scale, BF16)] + cs + [ones, zpad], axis=-1)
    ka = jnp.concatenate([heads(fk, FOX_HDIM), ones] + [-cc for cc in cs] + [zpad], axis=-1)
    va = heads(fv, FOX_HDIM)
    o_fox, lse = fox_fwd(qa, ka, va, name="fox_fwd")
    y_b = unheads(o_fox, FOX_HDIM).astype(BF16)

    pa = matmul_nn(y_a, full["w_a"], name="proj_a")
    pb = matmul_nn(y_b, full["w_b"], name="proj_b")

    def merge_fn(rows, vecs):
        ga, gb, a, b = rows
        return [_sigmoid(ga) * a + _sigmoid(gb) * b], []
    (merged,), _ = rowwise(merge_fn, [(proj,) + mine["ga"], (proj,) + mine["gb"], pa, pb], [], [(d, BF16)],
                           name="merge_fwd")
    mix = matmul_nn(merged, full["w_o"], name="out_proj")

    def ln1_fn(rows, vecs):
        z = ALPHA * rows[0] + rows[1]
        h = _ln_stats(z) * vecs[0] + vecs[1]
        return [z, h, h], []
    (z1, h1, h1b), _ = rowwise(ln1_fn, [h0, mix], [ln1_g, ln1_b], [(d, F32), (d, F32), (d, BF16)], name="ln1_fwd")

    relu2 = lambda u: jnp.square(jnp.maximum(u, 0.0))
    act = matmul_nn(h1b, full["w_ff1"], name="ff1", out_dtype=BF16, epilogue=relu2)
    ff = matmul_nn(act, full["w_ff2"], name="ff2")
    pg = matmul_nn(h1b, full["w_pg"], name="ple_gate")
    pe = matmul_nn(p_b, full["w_p"], name="ple_embed")

    def head_fn(rows, vecs):
        h1v, ffv, pgv, pev, t = rows
        g2, b2 = vecs
        sp = _sigmoid(pgv)
        z = ALPHA * h1v + ffv + sp * pev
        y = _ln_stats(z) * g2 + b2
        err = y - t
        loss_rows = 0.5 * jnp.mean(err * err, axis=-1, keepdims=True)
        dy = err * (1.0 / d)
        dz, dg2, db2 = _ln_bwd(z, dy, g2)
        loss_acc = jnp.broadcast_to(_colsum(loss_rows), (1, LANES))
        return [dz, dz, dz * pev * (sp * (1.0 - sp)), dz * sp], [dg2, db2, loss_acc]
    (dz2, dz2b, dpg, dpe), (g_ln2_g, g_ln2_b, loss_part) = rowwise(
        head_fn, [h1, ff, pg, pe, tgt], [ln2_g, ln2_b],
        [(d, F32), (d, BF16), (d, BF16), (d, BF16)], [d, d, LANES], name="head_fwd_bwd")

    wt = {nm: full[nm].T for nm in names if nm != "w_in"}
    dact = lambda da, a: da * (2.0 * jnp.sqrt(a.astype(F32)))
    du = matmul_nn(dz2b, wt["w_ff2"], name="d_ff2", out_dtype=BF16, epilogue=dact, aux=act)
    dh1_ff = matmul_nn(du, wt["w_ff1"], name="d_ff1")
    dh1_pg = matmul_nn(dpg, wt["w_pg"], name="d_ple_gate")

    def ln1_bwd_fn(rows, vecs):
        dh1 = ALPHA * rows[0] + rows[1] + rows[2]
        dz, dg, db = _ln_bwd(rows[3], dh1, vecs[0])
        return [dz, dz], [dg, db]
    (dz1, dz1b), (g_ln1_g, g_ln1_b) = rowwise(ln1_bwd_fn, [dz2, dh1_ff, dh1_pg, z1], [ln1_g],
                                              [(d, F32), (d, BF16)], [d, d], name="ln1_bwd")
    dmerged = matmul_nn(dz1b, wt["w_o"], name="d_out_proj")

    def merge_bwd_fn(rows, vecs):
        dm, ga, gb, a, b = rows
        sa, sb = _sigmoid(ga), _sigmoid(gb)
        return [dm * a * (sa * (1.0 - sa)), dm * b * (sb * (1.0 - sb)), dm * sa, dm * sb], []
    (dga, dgb, dma, dmb), _ = rowwise(merge_bwd_fn, [dmerged, (proj,) + mine["ga"], (proj,) + mine["gb"], pa, pb], [],
                                      [(d, BF16)] * 4, name="merge_bwd")
    dya = matmul_nn(dma, wt["w_a"], name="d_proj_a")
    dyb = matmul_nn(dmb, wt["w_b"], name="d_proj_b", out_dtype=BF16)

    def ya_bwd_fn(rows, vecs):
        o, hg, dy = rows
        ng = vecs[0]
        sg = _sigmoid(hg)
        gate = hg * sg
        dn_parts, do_parts, n_parts = [], [], []
        for h in range(HG_HEADS):
            hs = slice(h * HG_DIM, (h + 1) * HG_DIM)
            oh = o[:, hs]
            r = lax.rsqrt(jnp.mean(oh * oh, axis=-1, keepdims=True) + RMS_EPS)
            nh = oh * r
            dn = dy[:, hs] * ng[:, hs] * gate[:, hs]
            do_parts.append(r * (dn - nh * jnp.mean(dn * nh, axis=-1, keepdims=True)))
            n_parts.append(nh)
        nrm = jnp.concatenate(n_parts, axis=1)
        dhg = dy * nrm * ng * (sg * (1.0 + hg * (1.0 - sg)))
        return [jnp.concatenate(do_parts, axis=1), dhg], [_colsum(dy * nrm * gate)]
    (do_raw, dhg), (g_norm_g,) = rowwise(ya_bwd_fn, [o_raw, (proj,) + mine["hg"], dya], [hg_norm_g],
                                         [(hw, F32), (hw, BF16)], [hw], name="hgrn2_out_bwd")
    dhq, dhf, dhi, g_lb = hgrn2_bwd(proj, [mine["hq"][0], mine["hf"][0], mine["hi"][0]], lb, do_raw, hg_states,
                                    n_seq, seq, name="hgrn2_bwd")

    do_fox = heads(dyb, FOX_HDIM)
    dqa, rsum, delta = fox_bwd_dq(qa, ka, va, do_fox, lse, name="fox_bwd_dq")
    dka, dva, dsum = fox_bwd_dkv(qa, ka, va, do_fox, delta, lse, name="fox_bwd_dkv")
    dfq = (unheads(dqa[:, :, :FOX_HDIM], FOX_HDIM) * scale).astype(BF16)
    dfk = unheads(dka[:, :, :FOX_HDIM], FOX_HDIM).astype(BF16)
    dfv = unheads(dva, FOX_HDIM).astype(BF16)
    dc = unheads(rsum[:, :, 0:1], 1) - dsum.reshape(n_seq, fh, seq).transpose(0, 2, 1).reshape(T, fh)
    dc = jnp.concatenate([dc, jnp.zeros((T, LANES - fh), F32)], axis=1)
    dlf = seq_cumsum(dc, n_seq, seq, reverse=True, name="fox_cumsum_bwd")

    def lf_bwd_fn(rows, vecs):
        u = rows[0] + vecs[0]
        du_ = rows[1] * _sigmoid(-u)
        return [du_], [_colsum(du_)]
    (dff_,), (g_fb,) = rowwise(lf_bwd_fn, [(proj,) + mine["ff"], dlf], [fb_pad], [(LANES, BF16)], [LANES],
                               name="fox_logf_bwd")

    dproj = jnp.concatenate([dga, dgb, dhq, dhf, dhi, dhg, dfq, dfk, dfv, dff_], axis=1)
    dh0_in = matmul_nn(dproj, win_mine.T, name="d_in_proj")

    def ln0_bwd_fn(rows, vecs):
        dh0 = rows[0] + ALPHA * rows[1]
        dx, dg, db = _ln_bwd(rows[2], dh0, vecs[0])
        return [dx], [dg, db]
    (dx,), (g_ln0_g, g_ln0_b) = rowwise(ln0_bwd_fn, [dh0_in, dz1, x2], [vec(ln0_g)], [(d, F32)], [d, d],
                                        name="ln0_bwd")

    gw_in_mine = matmul_tn(h0b, dproj, name="g_w_in")
    gw_in = jnp.concatenate([gw_in_mine[:, mine[nm][0]:mine[nm][0] + orig[nm][1]]
                             for nm in ["hq", "hf", "hi", "hg", "fq", "fk", "fv", "ff", "ga", "gb"]], axis=1)
    gfull = {
        "w_in": gw_in,
        "w_a": matmul_tn(y_a, dma, name="g_w_a"),
        "w_b": matmul_tn(y_b, dmb, name="g_w_b"),
        "w_o": matmul_tn(merged, dz1b, name="g_w_o"),
        "w_ff1": matmul_tn(h1b, du, name="g_w_ff1"),
        "w_ff2": matmul_tn(act, dz2b, name="g_w_ff2"),
        "w_pg": matmul_tn(h1b, dpg, name="g_w_pg"),
        "w_p": matmul_tn(p_b, dpe, name="g_w_p"),
    }

    def chip_parts(nm, s):
        g = gfull[nm]
        n = g.shape[axis[nm]] // 4
        return lax.slice_in_dim(g, s * n, (s + 1) * n, axis=axis[nm])
    packed = jnp.stack([pack_shards([chip_parts(nm, s) for nm in names]) for s in range(4)])
    me = 2 * lax.axis_index("x") + lax.axis_index("y")
    core = lax.axis_index("c")
    n_rows = packed.shape[1]
    half = n_rows // 2
    keep = lax.dynamic_slice_in_dim(packed, core * half, half, axis=1)
    give = lax.dynamic_slice_in_dim(packed, (1 - core) * half, half, axis=1).astype(BF16)
    from_core = swap_cores(give, name="swap_partials")

    def sum2_fn(rows, vecs):
        s = rows[0] + rows[1].astype(F32)
        return [s, s], []
    (pair, pair_b), _ = rowwise(sum2_fn, [keep.reshape(4 * half, PACK_W), from_core.reshape(4 * half, PACK_W)], [],
                                [(PACK_W, F32), (PACK_W, BF16)], name="sum_cores", tm=2 * PACK_TILE)
    got = scatter_chips(pair_b.reshape(4, half, PACK_W), name="scatter_grads")
    own = lax.dynamic_index_in_dim(pair.reshape(4, half, PACK_W), me, axis=0, keepdims=False)

    def sum4_fn(rows, vecs):
        a, r0, r1, r2 = rows
        return [((a + r0.astype(F32)) + r1.astype(F32)) + r2.astype(F32)], []
    (q_half,), _ = rowwise(sum4_fn, [own, got[0], got[1], got[2]], [], [(PACK_W, F32)], name="sum_chips",
                           tm=PACK_TILE // 2)
    q_other = swap_cores(q_half, name="swap_halves")
    g_packed = jnp.concatenate([jnp.where(core == 0, q_half, q_other), jnp.where(core == 0, q_other, q_half)], axis=0)
    g_shards = dict(zip(names, unpack_shards(g_packed, shard_shapes)))

    def row1024(*parts):
        r = jnp.concatenate([q.reshape(1, -1) for q in parts], axis=1)
        return jnp.concatenate([r, jnp.zeros((1, PACK_W - r.shape[1]), F32)], axis=1) if r.shape[1] < PACK_W else r
    small_rows = [row1024(g_ln0_g), row1024(g_ln0_b), row1024(g_ln1_g), row1024(g_ln1_b), row1024(g_ln2_g),
                  row1024(g_ln2_b), row1024(g_norm_g, g_lb), row1024(g_fb[:, :fh], loss_part[:, :1])]
    small = allreduce_small(jnp.concatenate(small_rows, axis=0), name="allreduce_small")
    s_ln0_g, s_ln0_b, s_ln1_g, s_ln1_b, s_ln2_g, s_ln2_b = [small[r:r + 1] for r in range(6)]
    s_norm_g, s_lb = small[6:7, :hw], small[6:7, hw:2 * hw]
    s_fb, loss = small[7:8, :fh], small[7, fh]
    p0 = probs[0:1]
    jac = p0 * (1.0 - p0)
    s_hg_lb = jnp.concatenate([s_lb * jac, -s_lb * jac], axis=0)

    small_w = [vec(ln0_g), vec(ln0_b), ln1_g, ln1_b, ln2_g, ln2_b, hg_lb.reshape(1, -1), hg_norm_g, fox_fb]
    small_g = [s_ln0_g, s_ln0_b, s_ln1_g, s_ln1_b, s_ln2_g, s_ln2_b, s_hg_lb.reshape(1, -1), s_norm_g, s_fb]
    small_m = [vec(m_ln0_g), vec(m_ln0_b), m_ln1_g, m_ln1_b, m_ln2_g, m_ln2_b, m_hg_lb.reshape(1, -1), m_hg_norm_g, m_fox_fb]
    small_v = [vec(v_ln0_g), vec(v_ln0_b), v_ln1_g, v_ln1_b, v_ln2_g, v_ln2_b, v_hg_lb.reshape(1, -1), v_hg_norm_g, v_fox_fb]
    pad_rows = lambda lst, fill: jnp.concatenate(
        [row1024(a) if fill == 0.0 else jnp.concatenate([a.reshape(1, -1), jnp.full((1, PACK_W - a.size), fill, F32)], axis=1)
         for a in lst] + [jnp.full((16 - len(lst), PACK_W), fill, F32)], axis=0)
    sd, sm, sv = adamw(pad_rows(small_w, 0.0), pad_rows(small_g, 0.0), pad_rows(small_m, 0.0), pad_rows(small_v, 1.0),
                       name="adamw_small")
    small_shapes = [ln0_g.shape, ln0_b.shape, ln1_g.shape, ln1_b.shape, ln2_g.shape, ln2_b.shape, hg_lb.shape,
                    hg_norm_g.shape, fox_fb.shape]
    take = lambda buf: [buf[r, :int(np.prod(shp))].reshape(shp) for r, shp in enumerate(small_shapes)]
    sg_out, sd_out, sm_out, sv_out = [g.reshape(shp) for g, shp in zip(small_g, small_shapes)], take(sd), take(sm), take(sv)

    big_out = {}
    for nm in names:
        delta, m2, v2 = adamw(big[nm], g_shards[nm], big_m[nm], big_v[nm], name="adamw_" + nm)
        big_out[nm] = (g_shards[nm][None], delta[None], m2[None], v2[None])

    def ordered(k):
        sm_ = [sg_out, sd_out, sm_out, sv_out][k]
        bg = lambda nm: big_out[nm][k]
        return [sm_[0], sm_[1], bg("w_in"), sm_[6], sm_[7], sm_[8], bg("w_a"), bg("w_b"), bg("w_o"), sm_[2], sm_[3],
                bg("w_ff1"), bg("w_ff2"), bg("w_pg"), bg("w_p"), sm_[4], sm_[5]]
    grad_x = dx.reshape(n_seq, seq, d)
    return (loss, grad_x, *ordered(0), *ordered(1), *ordered(2), *ordered(3))
```

```python
import functools
from typing import NamedTuple, Optional

import numpy as np
import jax
import jax.numpy as jnp
from jax import lax
from jax.experimental import pallas as pl
from jax.experimental.pallas import tpu as pltpu

F32 = jnp.float32
BF16 = jnp.bfloat16
MESH = pl.DeviceIdType.MESH

VMEM_LIMIT_BYTES = 48 * 1024 * 1024
LANES = 128
HG_HEADS = 4
HG_DIM = 128
HG_BLK = 16
HG_TILE = 256
FOX_HDIM = 64
FOX_AUG = 128
FOX_TQ = 1024
LN_EPS = 1e-5
RMS_EPS = 1e-6
DEPTH = 1
ALPHA = (2.0 * DEPTH) ** 0.25
ADAM_LR, ADAM_B1, ADAM_B2, ADAM_EPS, ADAM_WD, ADAM_STEP = 0.001, 0.9, 0.999, 1e-08, 0.01, 10
NEG_INF = -1e30


def _cparams(sem):
    return pltpu.CompilerParams(dimension_semantics=sem, vmem_limit_bytes=VMEM_LIMIT_BYTES)


def _tile(n, cap):
    if n <= cap:
        return n
    best = None
    for t in range(LANES, cap + 1, LANES):
        if n % t == 0:
            best = t
    assert best is not None, (n, cap)
    return best


class WView(NamedTuple):
    arr: jax.Array
    r0: int
    c0: int
    k: int
    n: int
    split: Optional[int]


def matmul_nn(a, w, *, name, transpose_rhs=False, out_dtype=F32, epilogue=None, aux=None, tm=1024):
    wv = w if isinstance(w, WView) else WView(w[None], 0, 0, w.shape[0], w.shape[1], None)
    rows_s = wv.k // 4 if wv.split == 0 else wv.k
    cols_s = wv.n // 4 if wv.split == 1 else wv.n
    tr, tc = _tile(rows_s, 1152), _tile(cols_s, 1152)
    assert wv.r0 % tr == 0 and wv.c0 % tc == 0
    T, K = a.shape
    N, tn, tk = (wv.k, tr, tc) if transpose_rhs else (wv.n, tc, tr)
    assert K == (wv.n if transpose_rhs else wv.k)
    tm = min(tm, T)
    assert T % tm == 0
    nk = K // tk

    def w_block(ri, ci):
        if wv.split == 0:
            return (ri * tr) // rows_s, (wv.r0 + (ri * tr) % rows_s) // tr, wv.c0 // tc + ci
        if wv.split == 1:
            return (ci * tc) // cols_s, wv.r0 // tr + ri, (wv.c0 + (ci * tc) % cols_s) // tc
        return 0, wv.r0 // tr + ri, wv.c0 // tc + ci

    def body(*refs):
        if aux is None:
            a_ref, w_ref, o_ref, acc_ref = refs
            x_ref = None
        else:
            a_ref, w_ref, x_ref, o_ref, acc_ref = refs
        k = pl.program_id(2)
        if transpose_rhs:
            part = lax.dot_general(a_ref[...], w_ref[...], (((1,), (1,)), ((), ())), preferred_element_type=F32)
        else:
            part = jnp.dot(a_ref[...], w_ref[...], preferred_element_type=F32)

        def write(res):
            if epilogue is not None:
                res = epilogue(res) if x_ref is None else epilogue(res, x_ref[...])
            o_ref[...] = res.astype(out_dtype)

        if nk == 1:
            write(part)
        else:
            @pl.when(k == 0)
            def _():
                acc_ref[...] = part

            @pl.when(k > 0)
            def _():
                acc_ref[...] += part

            @pl.when(k == nk - 1)
            def _():
                write(acc_ref[...])

    w_index = (lambda n, m, k: w_block(n, k)) if transpose_rhs else (lambda n, m, k: w_block(k, n))
    in_specs = [pl.BlockSpec((tm, tk), lambda n, m, k: (m, k)),
                pl.BlockSpec((None, tr, tc), w_index)]
    args = [a, wv.arr]
    if aux is not None:
        in_specs.append(pl.BlockSpec((tm, tn), lambda n, m, k: (m, n)))
        args.append(aux)
    return pl.pallas_call(
        body, name=name,
        grid=(N // tn, T // tm, nk),
        in_specs=in_specs,
        out_specs=pl.BlockSpec((tm, tn), lambda n, m, k: (m, n)),
        out_shape=jax.ShapeDtypeStruct((T, N), out_dtype),
        scratch_shapes=[pltpu.VMEM((tm, tn) if nk > 1 else (8, LANES), F32)],
        compiler_params=_cparams(("parallel", "parallel", "arbitrary")),
    )(*args)


def matmul_tn(a, b, *, name, tk=1024):
    T, M = a.shape
    T2, N = b.shape
    tk = min(tk, T)
    assert T == T2 and T % tk == 0
    tm = _tile(M, 1024)
    tn = _tile(N, 1152)

    def body(a_ref, b_ref, o_ref):
        k = pl.program_id(2)
        part = lax.dot_general(a_ref[...], b_ref[...], (((0,), (0,)), ((), ())), preferred_element_type=F32)

        @pl.when(k == 0)
        def _():
            o_ref[...] = part

        @pl.when(k > 0)
        def _():
            o_ref[...] += part

    return pl.pallas_call(
        body, name=name,
        grid=(M // tm, N // tn, T // tk),
        in_specs=[pl.BlockSpec((tk, tm), lambda m, n, k: (k, m)),
                  pl.BlockSpec((tk, tn), lambda m, n, k: (k, n))],
        out_specs=pl.BlockSpec((tm, tn), lambda m, n, k: (m, n)),
        out_shape=jax.ShapeDtypeStruct((M, N), F32),
        compiler_params=_cparams(("parallel", "parallel", "arbitrary")),
    )(a, b)


def rowwise(fn, rows, vecs, outs, accs=(), *, name, tm=512):
    rows = [r if isinstance(r, tuple) else (r, 0, r.shape[1]) for r in rows]
    T = rows[0][0].shape[0]
    tm = min(tm, T)
    assert T % tm == 0
    n_rows, n_vecs, n_outs, n_accs = len(rows), len(vecs), len(outs), len(accs)

    def body(*refs):
        row_refs = refs[:n_rows]
        vec_refs = refs[n_rows:n_rows + n_vecs]
        out_refs = refs[n_rows + n_vecs:n_rows + n_vecs + n_outs]
        acc_refs = refs[n_rows + n_vecs + n_outs:]
        out_vals, acc_vals = fn([r[...] for r in row_refs], [v[...] for v in vec_refs])
        assert len(out_vals) == n_outs and len(acc_vals) == n_accs
        for r, val in zip(out_refs, out_vals):
            r[...] = val.astype(r.dtype)
        if n_accs:
            i = pl.program_id(0)

            @pl.when(i == 0)
            def _():
                for r in acc_refs:
                    r[...] = jnp.zeros_like(r)

            for r, val in zip(acc_refs, acc_vals):
                r[...] += val

    in_specs = []
    for arr, off, width in rows:
        assert off % width == 0
        in_specs.append(pl.BlockSpec((tm, width), functools.partial(lambda i, blk: (i, blk), blk=off // width)))
    for v in vecs:
        in_specs.append(pl.BlockSpec(v.shape, lambda i: (0, 0)))
    out_specs = [pl.BlockSpec((tm, w), lambda i: (i, 0)) for w, _ in outs]
    out_specs += [pl.BlockSpec((1, w), lambda i: (0, 0)) for w in accs]
    out_shape = [jax.ShapeDtypeStruct((T, w), dt) for w, dt in outs]
    out_shape += [jax.ShapeDtypeStruct((1, w), F32) for w in accs]
    res = pl.pallas_call(
        body, name=name,
        grid=(T // tm,),
        in_specs=in_specs, out_specs=out_specs, out_shape=out_shape,
        compiler_params=_cparams(("arbitrary",) if n_accs else ("parallel",)),
    )(*[r[0] for r in rows], *vecs)
    return res[:n_outs], res[n_outs:]


def _colsum(x):
    return jnp.sum(x, axis=0, keepdims=True)


def _sigmoid(x):
    return 1.0 / (1.0 + jnp.exp(-x))


def _ln_stats(z):
    mu = jnp.mean(z, axis=-1, keepdims=True)
    zc = z - mu
    var = jnp.mean(zc * zc, axis=-1, keepdims=True)
    return zc * lax.rsqrt(var + LN_EPS)


def _ln_bwd(zhat_src, dy, g):
    mu = jnp.mean(zhat_src, axis=-1, keepdims=True)
    zc = zhat_src - mu
    var = jnp.mean(zc * zc, axis=-1, keepdims=True)
    rstd = lax.rsqrt(var + LN_EPS)
    zh = zc * rstd
    dzh = dy * g
    dz = rstd * (dzh - jnp.mean(dzh, axis=-1, keepdims=True) - zh * jnp.mean(dzh * zh, axis=-1, keepdims=True))
    return dz, _colsum(dy * zh), _colsum(dy)


def _hg_constants():
    r = np.arange(HG_TILE)
    same = (r[:, None] // HG_BLK) == (r[None, :] // HG_BLK)
    lower = (same & (r[None, :] <= r[:, None])).astype(np.float32)
    upper = (same & (r[None, :] >= r[:, None])).astype(np.float32)
    total = same.astype(np.float32)
    w = HG_HEADS * HG_DIM
    c = np.arange(w)
    bd = ((c[:, None] // HG_DIM) == (c[None, :] // HG_DIM)).astype(np.float32)
    n = HG_BLK * HG_BLK
    rr = np.arange(n)
    sel_t = (rr[None, :] // HG_BLK == np.arange(HG_BLK)[:, None]).astype(np.float32)
    sel_s = (rr[None, :] % HG_BLK == np.arange(HG_BLK)[:, None]).astype(np.float32)
    as_bf = lambda m: jnp.asarray(m, dtype=BF16)
    return as_bf(lower), as_bf(upper), as_bf(total), as_bf(bd), as_bf(sel_t), as_bf(sel_s)


def _keep_bf16_bits(x):
    bits = lax.bitcast_convert_type(x, jnp.int32) & jnp.int32(-65536)
    return lax.bitcast_convert_type(bits, F32)


def _split3(x):
    hi = _keep_bf16_bits(x)
    r1 = x - hi
    mid = _keep_bf16_bits(r1)
    lo = _keep_bf16_bits(r1 - mid)
    return hi.astype(BF16), mid.astype(BF16), lo.astype(BF16)


def _dot3(m01, x):
    hi, mid, lo = _split3(x)
    d = lambda p: jnp.dot(m01, p, preferred_element_type=F32)
    return (d(lo) + d(mid)) + d(hi)


def _hg_prologue(hq, hf, lb, lower, total):
    sq = _sigmoid(hq)
    q = hq * sq
    sg = _sigmoid(hf)
    f = lb + (1.0 - lb) * sg
    g = jnp.log(f)
    k = 1.0 - f
    b = _dot3(lower, g)
    bl = _dot3(total, g)
    return q, k, f, sg, sq, b, bl


def _stack16(fn):
    return [fn(t) for t in range(HG_BLK)]


def hgrn2_fwd(proj, offs, lb, n_seq, seq, *, name):
    T = n_seq * seq
    W = HG_HEADS * HG_DIM
    n_tiles = seq // HG_TILE
    nb = HG_TILE // HG_BLK
    lower, _, total, bd, sel_t, _ = _hg_constants()

    def body(hq_ref, hf_ref, hi_ref, lb_ref, lower_ref, total_ref, bd_ref, selt_ref,
             o_ref, st_out_ref,
             st_ref, q_s, k_s, v_s, b_s, qt_s, kt_s, d_s, p_s):
        @pl.when(pl.program_id(1) == 0)
        def _():
            st_ref[...] = jnp.zeros_like(st_ref)

        q, k, _, _, _, b, bl = _hg_prologue(hq_ref[...], hf_ref[...], lb_ref[...], lower_ref[...], total_ref[...])
        q_s[...] = q
        k_s[...] = k
        v_s[...] = hi_ref[...]
        b_s[...] = b
        qt_s[...] = q * jnp.exp(b)
        kt_s[...] = k * jnp.exp(jnp.minimum(bl - b, 0.0))
        d_s[...] = jnp.exp(bl)
        rowi = lax.broadcasted_iota(jnp.int32, (HG_BLK, W), 0)

        def block(i, carry):
            r0 = pl.multiple_of(i * HG_BLK, HG_BLK)
            rows = pl.ds(r0, HG_BLK)
            qi, ki, vi, bi = q_s[rows, :], k_s[rows, :], v_s[rows, :], b_s[rows, :]
            for t in range(HG_BLK):
                e = jnp.where(rowi <= t, jnp.exp(jnp.minimum(bi[t:t + 1, :] - bi, 0.0)), 0.0)
                p_s[pl.ds(t * HG_BLK, HG_BLK), :] = (e * qi[t:t + 1, :] * ki).astype(BF16)
            a_b = jnp.dot(p_s[...], bd_ref[...], preferred_element_type=F32)
            vt = jnp.concatenate([vi] * HG_BLK, axis=0)
            o_blk = jnp.dot(selt_ref[...], (a_b * vt).astype(BF16), preferred_element_type=F32)
            qti, kti, di = qt_s[rows, :], kt_s[rows, :], d_s[rows, :]
            outs = []
            for h in range(HG_HEADS):
                hs = slice(h * HG_DIM, (h + 1) * HG_DIM)
                st_h = st_ref[hs, :]
                st_out_ref[i, hs, :] = st_h
                outs.append(lax.dot_general(qti[:, hs].astype(BF16), st_h.astype(BF16),
                                            (((1,), (1,)), ((), ())), preferred_element_type=F32))
                upd = lax.dot_general(vi[:, hs].astype(BF16), kti[:, hs].astype(BF16),
                                      (((0,), (0,)), ((), ())), preferred_element_type=F32)
                st_ref[hs, :] = st_h * di[0:1, hs] + upd
            o_ref[rows, :] = o_blk + jnp.concatenate(outs, axis=1)
            return carry

        lax.fori_loop(0, nb, block, 0, unroll=2)

    col = lambda off: functools.partial(lambda s, t, blk: (s * n_tiles + t, blk), blk=off // W)
    const = lambda m: pl.BlockSpec(m.shape, lambda s, t: (0, 0))
    tile_f32 = pltpu.VMEM((HG_TILE, W), F32)
    o, states = pl.pallas_call(
        body, name=name,
        grid=(n_seq, n_tiles),
        in_specs=[pl.BlockSpec((HG_TILE, W), col(offs[0])), pl.BlockSpec((HG_TILE, W), col(offs[1])),
                  pl.BlockSpec((HG_TILE, W), col(offs[2])), const(lb), const(lower), const(total), const(bd),
                  const(sel_t)],
        out_specs=[pl.BlockSpec((HG_TILE, W), lambda s, t: (s * n_tiles + t, 0)),
                   pl.BlockSpec((nb, W, HG_DIM), lambda s, t: (s * n_tiles + t, 0, 0))],
        out_shape=[jax.ShapeDtypeStruct((T, W), F32), jax.ShapeDtypeStruct((T // HG_BLK, W, HG_DIM), F32)],
        scratch_shapes=[pltpu.VMEM((W, HG_DIM), F32)] + [tile_f32] * 7
                       + [pltpu.VMEM((HG_BLK * HG_BLK, W), BF16)],
        compiler_params=_cparams(("arbitrary", "arbitrary")),
    )(proj, proj, proj, lb, lower, total, bd, sel_t)
    return o, states


def hgrn2_bwd(proj, offs, lb, do, states, n_seq, seq, *, name):
    T = n_seq * seq
    W = HG_HEADS * HG_DIM
    n_tiles = seq // HG_TILE
    nb = HG_TILE // HG_BLK
    lower, upper, total, bd, sel_t, sel_s = _hg_constants()

    def body(hq_ref, hf_ref, hi_ref, do_ref, st_in_ref, lb_ref, lower_ref, upper_ref, total_ref, bd_ref,
             selt_ref, sels_ref,
             dhq_ref, dhf_ref, dhi_ref, dlb_ref,
             dst_ref, q_s, k_s, v_s, b_s, qt_s, kt_s, d_s, eb_s, ekb_s, dq_s, dk_s, db_s, dv_s,
             p_s, e_s, w_s):
        first = jnp.logical_and(pl.program_id(0) == 0, pl.program_id(1) == 0)

        @pl.when(first)
        def _():
            dlb_ref[...] = jnp.zeros_like(dlb_ref)

        @pl.when(pl.program_id(1) == 0)
        def _():
            dst_ref[...] = jnp.zeros_like(dst_ref)

        hq, lbv = hq_ref[...], lb_ref[...]
        q, k, f, sg, sq, b, bl = _hg_prologue(hq, hf_ref[...], lbv, lower_ref[...], total_ref[...])
        eb = jnp.exp(b)
        ekb = jnp.exp(jnp.minimum(bl - b, 0.0))
        q_s[...] = q
        k_s[...] = k
        v_s[...] = hi_ref[...]
        b_s[...] = b
        eb_s[...] = eb
        ekb_s[...] = ekb
        qt_s[...] = q * eb
        kt_s[...] = k * ekb
        d_s[...] = jnp.exp(bl)
        rowi = lax.broadcasted_iota(jnp.int32, (HG_BLK, W), 0)
        last_row = rowi == HG_BLK - 1

        def block(j, carry):
            i = nb - 1 - j
            r0 = pl.multiple_of(i * HG_BLK, HG_BLK)
            rows = pl.ds(r0, HG_BLK)
            qi, ki, vi, bi, doi = q_s[rows, :], k_s[rows, :], v_s[rows, :], b_s[rows, :], do_ref[rows, :]
            for t in range(HG_BLK):
                sl = pl.ds(t * HG_BLK, HG_BLK)
                e = jnp.where(rowi <= t, jnp.exp(jnp.minimum(bi[t:t + 1, :] - bi, 0.0)), 0.0)
                e_s[sl, :] = e
                p_s[sl, :] = (e * qi[t:t + 1, :] * ki).astype(BF16)
                w_s[sl, :] = (doi[t:t + 1, :] * vi).astype(BF16)
            a_b = jnp.dot(p_s[...], bd_ref[...], preferred_element_type=F32)
            da_b = jnp.dot(w_s[...], bd_ref[...], preferred_element_type=F32)
            x = da_b * e_s[...]
            k_til = jnp.concatenate([ki] * HG_BLK, axis=0)
            q_rep = jnp.concatenate([jnp.broadcast_to(qi[t:t + 1, :], (HG_BLK, W)) for t in range(HG_BLK)], axis=0)
            do_rep = jnp.concatenate([jnp.broadcast_to(doi[t:t + 1, :], (HG_BLK, W)) for t in range(HG_BLK)], axis=0)
            dq_in = jnp.dot(selt_ref[...], (x * k_til).astype(BF16), preferred_element_type=F32)
            dk_in = jnp.dot(sels_ref[...], (x * q_rep).astype(BF16), preferred_element_type=F32)
            dv_in = jnp.dot(sels_ref[...], (a_b * do_rep).astype(BF16), preferred_element_type=F32)
            qti, kti, di = qt_s[rows, :], kt_s[rows, :], d_s[rows, :]
            dqt, dkt, dvt, dd = [], [], [], []
            for h in range(HG_HEADS):
                hs = slice(h * HG_DIM, (h + 1) * HG_DIM)
                st_h = st_in_ref[i, hs, :]
                dst_h = dst_ref[hs, :]
                do_h, v_h = doi[:, hs].astype(BF16), vi[:, hs].astype(BF16)
                dst_b = dst_h.astype(BF16)
                dqt.append(jnp.dot(do_h, st_h.astype(BF16), preferred_element_type=F32))
                dkt.append(jnp.dot(v_h, dst_b, preferred_element_type=F32))
                dvt.append(lax.dot_general(kti[:, hs].astype(BF16), dst_b, (((1,), (1,)), ((), ())),
                                           preferred_element_type=F32))
                dd.append(jnp.sum(dst_h * st_h, axis=0, keepdims=True))
                upd = lax.dot_general(do_h, qti[:, hs].astype(BF16), (((0,), (0,)), ((), ())),
                                      preferred_element_type=F32)
                dst_ref[hs, :] = dst_h * di[0:1, hs] + upd
            dqt = jnp.concatenate(dqt, axis=1)
            dkt = jnp.concatenate(dkt, axis=1)
            dvt = jnp.concatenate(dvt, axis=1)
            dd = jnp.concatenate(dd, axis=1)
            dbl = jnp.sum(dkt * kti, axis=0, keepdims=True) + dd * di[0:1, :]
            db = qi * dq_in - ki * dk_in + dqt * qti - dkt * kti
            db_s[rows, :] = db + jnp.where(last_row, dbl, 0.0)
            dq_s[rows, :] = dq_in + dqt * eb_s[rows, :]
            dk_s[rows, :] = dk_in + dkt * ekb_s[rows, :]
            dv_s[rows, :] = dv_in + dvt
            return carry

        lax.fori_loop(0, nb, block, 0, unroll=2)

        dg = _dot3(upper_ref[...], db_s[...])
        dhq_ref[...] = (dq_s[...] * (sq * (1.0 + hq * (1.0 - sq)))).astype(dhq_ref.dtype)
        df = dg / f - dk_s[...]
        dhf_ref[...] = (df * (1.0 - lbv) * (sg * (1.0 - sg))).astype(dhf_ref.dtype)
        dhi_ref[...] = dv_s[...].astype(dhi_ref.dtype)
        dlb_ref[...] += _colsum(df * (1.0 - sg))

    rev = lambda s, t: s * n_tiles + (n_tiles - 1 - t)
    col = lambda off: functools.partial(lambda s, t, blk: (rev(s, t), blk), blk=off // W)
    const = lambda m: pl.BlockSpec(m.shape, lambda s, t: (0, 0))
    row = pl.BlockSpec((HG_TILE, W), lambda s, t: (rev(s, t), 0))
    tile_f32 = pltpu.VMEM((HG_TILE, W), F32)
    n2 = HG_BLK * HG_BLK
    return pl.pallas_call(
        body, name=name,
        grid=(n_seq, n_tiles),
        in_specs=[pl.BlockSpec((HG_TILE, W), col(offs[0])), pl.BlockSpec((HG_TILE, W), col(offs[1])),
                  pl.BlockSpec((HG_TILE, W), col(offs[2])), row,
                  pl.BlockSpec((nb, W, HG_DIM), lambda s, t: (rev(s, t), 0, 0)),
                  const(lb), const(lower), const(upper), const(total), const(bd), const(sel_t), const(sel_s)],
        out_specs=[row, row, row, pl.BlockSpec((1, W), lambda s, t: (0, 0))],
        out_shape=[jax.ShapeDtypeStruct((T, W), BF16)] * 3 + [jax.ShapeDtypeStruct((1, W), F32)],
        scratch_shapes=[pltpu.VMEM((W, HG_DIM), F32)] + [tile_f32] * 13
                       + [pltpu.VMEM((n2, W), BF16), pltpu.VMEM((n2, W), F32), pltpu.VMEM((n2, W), BF16)],
        compiler_params=_cparams(("arbitrary", "arbitrary")),
    )(proj, proj, proj, do, states, lb, lower, upper, total, bd, sel_t, sel_s)


def _diag_mask(tq):
    return lax.broadcasted_iota(jnp.int32, (tq, tq), 1) <= lax.broadcasted_iota(jnp.int32, (tq, tq), 0)


FOX_ROW_GROUPS = 1


def _row_groups(tq):
    g = tq // FOX_ROW_GROUPS
    return [slice(r * g, (r + 1) * g) for r in range(FOX_ROW_GROUPS)]


def _qk(q, k):
    return lax.dot_general(q, k, (((1,), (1,)), ((), ())), preferred_element_type=F32)


def _causal_pairs(n, sweeps=1, by_key=False):
    if by_key:
        rows = [(i, j, 0) for j in range(n) for i in range(j, n)]
    else:
        rows = [(i, j, s) for i in range(n) for s in range(sweeps) for j in range(i + 1)]
    return tuple(jnp.asarray(np.array([r[c] for r in rows], np.int32)) for c in range(3))


def _fox_placement(fh):
    hw, wa = fh * FOX_HDIM, fh * FOX_AUG
    pq, pk = np.zeros((hw, wa), np.float32), np.zeros((hw, wa), np.float32)
    aq, ak = np.zeros((3 * LANES, wa), np.float32), np.zeros((3 * LANES, wa), np.float32)
    oq, ok = np.zeros((1, wa), np.float32), np.zeros((1, wa), np.float32)
    for h in range(fh):
        src, dst = np.arange(h * FOX_HDIM, (h + 1) * FOX_HDIM), np.arange(h * FOX_AUG, h * FOX_AUG + FOX_HDIM)
        pq[src, dst] = FOX_HDIM ** -0.5
        pk[src, dst] = 1.0
        gate = h * FOX_AUG + FOX_HDIM
        for r in range(3):
            aq[r * LANES + h, gate + r] = 1.0
            ak[r * LANES + h, gate + 3 + r] = -1.0
        oq[0, gate + 3:gate + 6] = 1.0
        ok[0, gate:gate + 3] = 1.0
    bf = lambda m: jnp.asarray(m, dtype=BF16)
    return {"pq": bf(pq), "pk": bf(pk), "aq": bf(aq), "ak": bf(ak), "oq": jnp.asarray(oq), "ok": jnp.asarray(ok),
            "pqt": bf(pq.T), "pkt": bf(pk.T)}


def _fox_specs(tq, fh):
    def spec(tab):
        return pl.BlockSpec((None, tq, FOX_AUG), lambda b, t, *tabs: (b // fh, tabs[tab][t], b % fh))
    return spec(0), spec(1)


def fox_fwd(qa, ka, va, *, name):
    n_seq, S, width = qa.shape
    fh = width // FOX_AUG
    BH = n_seq * fh
    tq = min(FOX_TQ, S)
    itab, jtab, _ = _causal_pairs(S // tq)

    def body(itab_ref, jtab_ref, q_ref, k_ref, v_ref, o_ref, lse_ref, m_s, l_s, acc_s):
        t = pl.program_id(1)
        i, j = itab_ref[t], jtab_ref[t]

        @pl.when(j == 0)
        def _():
            m_s[...] = jnp.full_like(m_s, NEG_INF)
            l_s[...] = jnp.zeros_like(l_s)
            acc_s[...] = jnp.zeros_like(acc_s)

        def step(on_diagonal):
            for rows in _row_groups(tq):
                s = _qk(q_ref[rows, :], k_ref[...])
                if on_diagonal:
                    s = jnp.where(_diag_mask(tq)[rows, :], s, NEG_INF)
                m_prev = m_s[rows, :]
                m_new = jnp.maximum(m_prev, jnp.max(s, axis=-1, keepdims=True))
                alpha = jnp.exp(m_prev - m_new)
                p = jnp.exp(s - m_new[:, 0:1])
                l_s[rows, :] = alpha * l_s[rows, :] + jnp.sum(p, axis=-1, keepdims=True)
                acc_s[rows, :] = alpha * acc_s[rows, :] + jnp.dot(p.astype(BF16), v_ref[...],
                                                                  preferred_element_type=F32)
                m_s[rows, :] = m_new

        @pl.when(j < i)
        def _():
            step(False)

        @pl.when(j == i)
        def _():
            step(True)
            o_ref[...] = (acc_s[...] / l_s[...]).astype(o_ref.dtype)
            lse_ref[...] = m_s[...] + jnp.log(l_s[...])

    qspec, kspec = _fox_specs(tq, fh)
    return pl.pallas_call(
        body, name=name,
        grid_spec=pltpu.PrefetchScalarGridSpec(
            num_scalar_prefetch=2, grid=(BH, itab.shape[0]),
            in_specs=[qspec, kspec, kspec],
            out_specs=[qspec, qspec],
            scratch_shapes=[pltpu.VMEM((tq, LANES), F32)] * 3),
        out_shape=[jax.ShapeDtypeStruct((n_seq, S, width), BF16), jax.ShapeDtypeStruct((n_seq, S, width), F32)],
        compiler_params=_cparams(("parallel", "arbitrary")),
    )(itab, jtab, qa, ka, va)


def _fox_p_dp(q, k, v, do, lse, on_diagonal):
    s = _qk(q, k)
    if on_diagonal:
        s = jnp.where(_diag_mask(s.shape[0]), s, NEG_INF)
    return jnp.exp(s - lse[:, 0:1]), _qk(do, v)


def fox_bwd_dq(qa, ka, va, do, lse, *, name):
    n_seq, S, width = qa.shape
    fh = width // FOX_AUG
    BH = n_seq * fh
    tq = min(FOX_TQ, S)
    itab, jtab, stab = _causal_pairs(S // tq, sweeps=2)

    def body(itab_ref, jtab_ref, stab_ref, q_ref, k_ref, v_ref, do_ref, lse_ref, dq_ref, rsum_ref, delta_ref):
        t = pl.program_id(1)
        i, j, sweep = itab_ref[t], jtab_ref[t], stab_ref[t]

        @pl.when(jnp.logical_and(j == 0, sweep == 0))
        def _():
            dq_ref[...] = jnp.zeros_like(dq_ref)
            rsum_ref[...] = jnp.zeros_like(rsum_ref)
            delta_ref[...] = jnp.zeros_like(delta_ref)

        def step(on_diagonal, second):
            p, dp = _fox_p_dp(q_ref[...], k_ref[...], v_ref[...], do_ref[...], lse_ref[...], on_diagonal)
            if not second:
                delta_ref[...] += jnp.sum(p * dp, axis=-1, keepdims=True)
            else:
                ds = p * (dp - delta_ref[:, 0:1])
                dq_ref[...] += jnp.dot(ds.astype(BF16), k_ref[...], preferred_element_type=F32)
                rsum_ref[...] += jnp.sum(ds, axis=-1, keepdims=True)

        for on_diagonal in (False, True):
            for second in (False, True):
                cond = jnp.logical_and((j == i) if on_diagonal else (j < i), sweep == int(second))
                pl.when(cond)(functools.partial(step, on_diagonal, second))

    qspec, kspec = _fox_specs(tq, fh)
    return pl.pallas_call(
        body, name=name,
        grid_spec=pltpu.PrefetchScalarGridSpec(
            num_scalar_prefetch=3, grid=(BH, itab.shape[0]),
            in_specs=[qspec, kspec, kspec, qspec, qspec],
            out_specs=[qspec, qspec, qspec]),
        out_shape=[jax.ShapeDtypeStruct((n_seq, S, width), F32)] * 3,
        compiler_params=_cparams(("parallel", "arbitrary")),
    )(itab, jtab, stab, qa, ka, va, do, lse)


def fox_bwd_dkv(qa, ka, va, do, delta, lse, *, name):
    n_seq, S, width = qa.shape
    fh = width // FOX_AUG
    BH = n_seq * fh
    tq = min(FOX_TQ, S)
    itab, jtab, _ = _causal_pairs(S // tq, by_key=True)

    def body(itab_ref, jtab_ref, q_ref, k_ref, v_ref, do_ref, delta_ref, lse_ref, dk_ref, dv_ref, dsum_ref):
        t = pl.program_id(1)
        i, j = itab_ref[t], jtab_ref[t]

        def step(on_diagonal):
            q, do = q_ref[...], do_ref[...]
            p, dp = _fox_p_dp(q, k_ref[...], v_ref[...], do, lse_ref[...], on_diagonal)
            ds = p * (dp - delta_ref[:, 0:1])
            tn = (((0,), (0,)), ((), ()))
            dv = lax.dot_general(p.astype(BF16), do, tn, preferred_element_type=F32)
            dk = lax.dot_general(ds.astype(BF16), q, tn, preferred_element_type=F32)
            if on_diagonal:
                dv_ref[...], dk_ref[...], dsum_ref[...] = dv, dk, _colsum(ds)
            else:
                dv_ref[...] += dv
                dk_ref[...] += dk
                dsum_ref[...] += _colsum(ds)

        @pl.when(i == j)
        def _():
            step(True)

        @pl.when(i > j)
        def _():
            step(False)

    qspec, kspec = _fox_specs(tq, fh)
    return pl.pallas_call(
        body, name=name,
        grid_spec=pltpu.PrefetchScalarGridSpec(
            num_scalar_prefetch=2, grid=(BH, itab.shape[0]),
            in_specs=[qspec, kspec, kspec, qspec, qspec, qspec],
            out_specs=[kspec, kspec, pl.BlockSpec((None, 1, tq), lambda b, t, it, jt: (b, 0, jt[t]))]),
        out_shape=[jax.ShapeDtypeStruct((n_seq, S, width), F32), jax.ShapeDtypeStruct((n_seq, S, width), F32),
                   jax.ShapeDtypeStruct((BH, 1, S), F32)],
        compiler_params=_cparams(("parallel", "arbitrary")),
    )(itab, jtab, qa, ka, va, do, delta, lse)


def seq_cumsum(x, n_seq, seq, *, reverse, name):
    T, C = x.shape
    tb = min(256, seq)
    n = seq // tb
    r = np.arange(tb)
    tri = (r[None, :] >= r[:, None]) if reverse else (r[None, :] <= r[:, None])
    tri = jnp.asarray(tri.astype(np.float32), dtype=BF16)

    def body(x_ref, tri_ref, o_ref, carry_s):
        @pl.when(pl.program_id(1) == 0)
        def _():
            carry_s[...] = jnp.zeros_like(carry_s)

        xv = x_ref[...]
        o_ref[...] = _dot3(tri_ref[...], xv) + carry_s[...]
        carry_s[...] += _colsum(xv)

    blk = (lambda s, t: (s * n + (n - 1 - t), 0)) if reverse else (lambda s, t: (s * n + t, 0))
    return pl.pallas_call(
        body, name=name,
        grid=(n_seq, n),
        in_specs=[pl.BlockSpec((tb, C), blk), pl.BlockSpec((tb, tb), lambda s, t: (0, 0))],
        out_specs=pl.BlockSpec((tb, C), blk),
        out_shape=jax.ShapeDtypeStruct((T, C), F32),
        scratch_shapes=[pltpu.VMEM((1, C), F32)],
        compiler_params=_cparams(("arbitrary", "arbitrary")),
    )(x, tri)


def _place():
    return lax.axis_index("x"), lax.axis_index("y"), lax.axis_index("c")


def _other_chips(x, y):
    return [(1 - x, y), (x, 1 - y), (1 - x, 1 - y)]


def _hbm_call(body, ins, out_shape, n_sems, *, name):
    hbm = pl.BlockSpec(memory_space=pl.ANY)
    return pl.pallas_call(
        body, name=name,
        in_specs=[hbm] * len(ins), out_specs=[hbm] * len(out_shape), out_shape=out_shape,
        scratch_shapes=[pltpu.SemaphoreType.DMA((n_sems,)), pltpu.SemaphoreType.DMA((n_sems,)),
                        pltpu.SemaphoreType.DMA((len(ins),))],
        compiler_params=pltpu.CompilerParams(has_side_effects=True),
    )(*ins)


def allgather_chips(shards, *, name):
    nb = len(shards)
    assert all(s.shape[0] % (2 * ROW_ALIGN) == 0 for s in shards)

    def body(*refs):
        x_refs, o_refs = refs[:nb], refs[nb:2 * nb]
        send_sems, recv_sems, local_sems = refs[2 * nb:]
        x, y, c = _place()
        me = 2 * x + y
        chips = _other_chips(x, y)
        own, first, passed, landed, handed = [], [], [], [], []
        for b, (x_ref, o_ref) in enumerate(zip(x_refs, o_refs)):
            half = x_ref.shape[0] // 2
            mine, theirs = pl.ds(c * half, half), pl.ds((1 - c) * half, half)
            own.append(pltpu.make_async_copy(x_ref, o_ref.at[me], local_sems.at[b]))

            def copy(k, src, chip, rows, to, o_ref=o_ref, b=b):
                return pltpu.make_async_remote_copy(src_ref=src, dst_ref=o_ref.at[2 * chip[0] + chip[1], rows],
                                                    send_sem=send_sems.at[6 * b + k], recv_sem=recv_sems.at[6 * b + k],
                                                    device_id=to, device_id_type=MESH)
            for j, chip in enumerate(chips):
                first.append(copy(j, x_ref.at[mine], (x, y), mine, (*chip, c)))
                landed.append(copy(j, x_ref.at[mine], chip, mine, (*chip, c)))
                passed.append(copy(3 + j, o_ref.at[2 * chip[0] + chip[1], mine], chip, mine, (x, y, 1 - c)))
                handed.append(copy(3 + j, x_ref.at[mine], chip, theirs, (x, y, 1 - c)))
        for cp in own + first:
            cp.start()
        for arrived, forward in zip(landed, passed):
            arrived.wait_recv()
            forward.start()
        for cp in handed:
            cp.wait_recv()
        for cp in first + passed:
            cp.wait_send()
        for cp in own:
            cp.wait()

    return _hbm_call(body, shards, [jax.ShapeDtypeStruct((4,) + s.shape, s.dtype) for s in shards], 6 * nb, name=name)


def scatter_chips(parts, *, name):
    nb = len(parts)

    def body(*refs):
        x_refs, o_refs = refs[:nb], refs[nb:2 * nb]
        send_sems, recv_sems, _ = refs[2 * nb:]
        x, y, c = _place()
        sends = []
        for b, (x_ref, o_ref) in enumerate(zip(x_refs, o_refs)):
            for j, (px, py) in enumerate(_other_chips(x, y)):
                sends.append(pltpu.make_async_remote_copy(
                    src_ref=x_ref.at[2 * px + py], dst_ref=o_ref.at[j], send_sem=send_sems.at[3 * b + j],
                    recv_sem=recv_sems.at[3 * b + j], device_id=(px, py, c), device_id_type=MESH))
        for cp in sends:
            cp.start()
        for cp in sends:
            cp.wait_recv()
        for cp in sends:
            cp.wait_send()

    return _hbm_call(body, parts, [jax.ShapeDtypeStruct((3,) + p.shape[1:], p.dtype) for p in parts], 3 * nb, name=name)


def swap_cores(vs, *, name):
    nb = len(vs)

    def body(*refs):
        x_refs, o_refs = refs[:nb], refs[nb:2 * nb]
        send_sems, recv_sems, _ = refs[2 * nb:]
        x, y, c = _place()
        copies = [pltpu.make_async_remote_copy(src_ref=x_ref, dst_ref=o_ref, send_sem=send_sems.at[b],
                                               recv_sem=recv_sems.at[b], device_id=(x, y, 1 - c), device_id_type=MESH)
                  for b, (x_ref, o_ref) in enumerate(zip(x_refs, o_refs))]
        for cp in copies:
            cp.start()
        for cp in copies:
            cp.wait()

    return _hbm_call(body, vs, [jax.ShapeDtypeStruct(v.shape, v.dtype) for v in vs], nb, name=name)


def allreduce_small(v, *, name):
    R, C = v.shape

    def body(x_ref, o_ref, gath_ref, send_sems, recv_sems):
        x, y, c = _place()
        me = 4 * x + 2 * y + c
        gath_ref[me] = x_ref[...]
        flips = [(k >> 2 & 1, k >> 1 & 1, k & 1) for k in range(1, 8)]
        sends = []
        for j, (fx, fy, fc) in enumerate(flips):
            peer = (x ^ fx, y ^ fy, c ^ fc)
            cp = pltpu.make_async_remote_copy(src_ref=x_ref, dst_ref=gath_ref.at[me], send_sem=send_sems.at[j],
                                              recv_sem=recv_sems.at[j], device_id=peer, device_id_type=MESH)
            cp.start()
            sends.append(cp)
        for j, (fx, fy, fc) in enumerate(flips):
            peer = (x ^ fx, y ^ fy, c ^ fc)
            pltpu.make_async_remote_copy(src_ref=x_ref, dst_ref=gath_ref.at[4 * peer[0] + 2 * peer[1] + peer[2]],
                                         send_sem=send_sems.at[j], recv_sem=recv_sems.at[j], device_id=peer,
                                         device_id_type=MESH).wait_recv()
        for cp in sends:
            cp.wait_send()
        total = gath_ref[0]
        for d in range(1, 8):
            total = total + gath_ref[d]
        o_ref[...] = total

    vm = pl.BlockSpec(memory_space=pltpu.VMEM)
    out, _ = pl.pallas_call(
        body, name=name,
        in_specs=[vm], out_specs=[vm, vm],
        out_shape=[jax.ShapeDtypeStruct((R, C), F32), jax.ShapeDtypeStruct((8, R, C), F32)],
        scratch_shapes=[pltpu.SemaphoreType.DMA((7,)), pltpu.SemaphoreType.DMA((7,))],
        compiler_params=pltpu.CompilerParams(has_side_effects=True),
    )(v)
    return out


ROW_ALIGN = 16
PACK_W = 1024
SUM_TILE = 512
BIG_WEIGHTS = (("w_in", 1), ("w_a", 1), ("w_b", 1), ("w_o", 0), ("w_ff1", 1), ("w_ff2", 0), ("w_pg", 0), ("w_p", 1))


def _b_layout(d, ple):
    hw, q = d // 2, d // 4
    small = 2 * d + 2 * q
    lay = {"w_ff1": (0, 0, d, d), "w_ff2": (d, 0, d, d), "w_o": (2 * d, 0, q, d), "w_pg": (2 * d + q, 0, q, d),
           "w_a": (small, 0, hw, q), "w_b": (small, q, hw, q), "w_p": (small, 2 * q, ple, q)}
    return lay, small + hw


def pack_a(w_in_shard):
    rows, cols = w_in_shard.shape
    pad = -cols % LANES
    return jnp.concatenate([w_in_shard, jnp.zeros((rows, pad), w_in_shard.dtype)], axis=1)


def pack_b(shards, d):
    hw, q = d // 2, d // 4
    dt = shards["w_a"].dtype
    wp = shards["w_p"]
    wp = jnp.concatenate([wp, jnp.zeros((hw - wp.shape[0], q), dt)], axis=0)
    small = jnp.concatenate([shards["w_a"], shards["w_b"], wp, jnp.zeros((hw, d - 3 * q), dt)], axis=1)
    return jnp.concatenate([shards["w_ff1"], shards["w_ff2"], shards["w_o"], shards["w_pg"], small], axis=0)


def unpack_b(buf, lay):
    return {nm: buf[r0:r0 + rows, c0:c0 + cols] for nm, (r0, c0, rows, cols) in lay.items()}


def _win_layout(d):
    hw = d // 2
    fh = hw // FOX_HDIM
    orig = {"hq": (0, hw), "hf": (hw, hw), "hi": (2 * hw, hw), "hg": (3 * hw, hw), "fq": (4 * hw, hw),
            "fk": (5 * hw, hw), "fv": (6 * hw, hw), "ff": (7 * hw, fh), "ga": (7 * hw + fh, d), "gb": (7 * hw + fh + d, d)}
    order = ["ga", "gb", "hq", "hf", "hi", "hg", "fq", "fk", "fv", "ff"]
    mine, off = {}, 0
    for nm in order:
        width = orig[nm][1] if nm != "ff" else LANES
        mine[nm] = (off, width)
        off += width
    return orig, order, mine, off


def _adam_fn(rows, vecs):
    w, g, m, v = rows
    m2 = ADAM_B1 * m + (1.0 - ADAM_B1) * g
    v2 = ADAM_B2 * v + (1.0 - ADAM_B2) * (g * g)
    m_hat = m2 / (1.0 - ADAM_B1 ** ADAM_STEP)
    v_hat = v2 / (1.0 - ADAM_B2 ** ADAM_STEP)
    delta = -ADAM_LR * (m_hat / (jnp.sqrt(v_hat) + ADAM_EPS) + ADAM_WD * w)
    return [delta, m2, v2], []


def adamw(w, g, m, v, *, name):
    c = w.shape[1]
    (delta, m2, v2), _ = rowwise(_adam_fn, [w, g, m, v], [], [(c, F32)] * 3, name=name, tm=256)
    return delta, m2, v2


def kernel(x, p, ln0_g, ln0_b, w_in, hg_lb, hg_norm_g, fox_fb, w_a, w_b, w_o, ln1_g, ln1_b, w_ff1, w_ff2, w_pg, w_p, ln2_g, ln2_b, loss_target, m_ln0_g, m_ln0_b, m_w_in, m_hg_lb, m_hg_norm_g, m_fox_fb, m_w_a, m_w_b, m_w_o, m_ln1_g, m_ln1_b, m_w_ff1, m_w_ff2, m_w_pg, m_w_p, m_ln2_g, m_ln2_b, v_ln0_g, v_ln0_b, v_w_in, v_hg_lb, v_hg_norm_g, v_fox_fb, v_w_a, v_w_b, v_w_o, v_ln1_g, v_ln1_b, v_w_ff1, v_w_ff2, v_w_pg, v_w_p, v_ln2_g, v_ln2_b):
    n_seq, seq, d = x.shape
    T = n_seq * seq
    hw = d // 2
    fh = hw // FOX_HDIM
    bh = n_seq * fh
    orig, order, mine, n_in = _win_layout(d)

    big = {"w_in": w_in[0], "w_a": w_a[0], "w_b": w_b[0], "w_o": w_o[0], "w_ff1": w_ff1[0], "w_ff2": w_ff2[0],
           "w_pg": w_pg[0], "w_p": w_p[0]}
    big_m = {"w_in": m_w_in[0], "w_a": m_w_a[0], "w_b": m_w_b[0], "w_o": m_w_o[0], "w_ff1": m_w_ff1[0],
             "w_ff2": m_w_ff2[0], "w_pg": m_w_pg[0], "w_p": m_w_p[0]}
    big_v = {"w_in": v_w_in[0], "w_a": v_w_a[0], "w_b": v_w_b[0], "w_o": v_w_o[0], "w_ff1": v_w_ff1[0],
             "w_ff2": v_w_ff2[0], "w_pg": v_w_pg[0], "w_p": v_w_p[0]}
    names = [nm for nm, _ in BIG_WEIGHTS]
    axis = dict(BIG_WEIGHTS)
    ple = w_p.shape[1]
    lay, b_rows = _b_layout(d, ple)
    in_cols = big["w_in"].shape[1]

    a_all, b_all = allgather_chips(
        [pack_a(big["w_in"].astype(BF16)), pack_b({nm: big[nm].astype(BF16) for nm in names if nm != "w_in"}, d)],
        name="allgather_weights")
    win = jnp.concatenate([a_all[s, :, :in_cols] for s in range(4)], axis=1)
    win_mine = jnp.concatenate(
        [win[:, orig[nm][0]:orig[nm][0] + orig[nm][1]] for nm in order]
        + [jnp.zeros((d, LANES - fh), BF16)], axis=1)
    view = lambda nm, k, n: WView(b_all, lay[nm][0], lay[nm][1], k, n, axis[nm])
    w_ff1_v, w_ff2_v = view("w_ff1", d, 4 * d), view("w_ff2", 4 * d, d)
    w_o_v, w_pg_v = view("w_o", d, d), view("w_pg", d, d)
    w_a_v, w_p_v = view("w_a", hw, d), view("w_p", ple, d)
    rb, cb = lay["w_b"][0], lay["w_b"][1]
    w_b_full = jnp.concatenate([b_all[s, rb:rb + hw, cb:cb + d // 4] for s in range(4)], axis=1)

    x2 = x.reshape(T, d)
    tgt = loss_target.reshape(T, d)
    p_b = p.reshape(T, p.shape[-1]).astype(BF16)
    vec = lambda a: a.reshape(1, -1)
    probs = jax.nn.softmax(hg_lb, axis=0)
    lb = vec(probs[0])

    def ln0_fn(rows, vecs):
        h = _ln_stats(rows[0]) * vecs[0] + vecs[1]
        return [h, h], []
    (h0, h0b), _ = rowwise(ln0_fn, [x2], [vec(ln0_g), vec(ln0_b)], [(d, F32), (d, BF16)], name="ln0_fwd")
    proj = matmul_nn(h0b, win_mine, name="in_proj")

    o_raw, hg_states = hgrn2_fwd(proj, [mine["hq"][0], mine["hf"][0], mine["hi"][0]], lb, n_seq, seq, name="hgrn2_fwd")

    def ya_fn(rows, vecs):
        o, hg = rows
        outs = []
        for h in range(HG_HEADS):
            oh = o[:, h * HG_DIM:(h + 1) * HG_DIM]
            outs.append(oh * lax.rsqrt(jnp.mean(oh * oh, axis=-1, keepdims=True) + RMS_EPS))
        y = jnp.concatenate(outs, axis=1) * vecs[0] * (hg * _sigmoid(hg))
        return [y], []
    (y_a,), _ = rowwise(ya_fn, [o_raw, (proj,) + mine["hg"]], [hg_norm_g], [(hw, BF16)], name="hgrn2_out_fwd")

    fb_pad = jnp.concatenate([fox_fb, jnp.zeros((1, LANES - fh), F32)], axis=1)

    def lf_fn(rows, vecs):
        u = rows[0] + vecs[0]
        return [jnp.minimum(u, 0.0) - jnp.log(1.0 + jnp.exp(-jnp.abs(u)))], []
    (lf,), _ = rowwise(lf_fn, [(proj,) + mine["ff"]], [fb_pad], [(LANES, F32)], name="fox_logf")
    c_cum = seq_cumsum(lf, n_seq, seq, reverse=False, name="fox_cumsum")

    place = _fox_placement(fh)

    def prep_fn(rows, vecs):
        fq_, fk_, fv_, cc = rows
        pq, pk, aq, ak, oq, ok = vecs
        parts = jnp.concatenate(_split3(cc), axis=1)
        mm = lambda a_, b_: jnp.dot(a_, b_, preferred_element_type=F32)
        q_ = mm(fq_.astype(BF16), pq) + mm(parts, aq) + oq
        k_ = mm(fk_.astype(BF16), pk) + mm(parts, ak) + ok
        return [q_, k_, mm(fv_.astype(BF16), pk)], []
    wa = fh * FOX_AUG
    (qa, ka, va), _ = rowwise(prep_fn, [(proj,) + mine["fq"], (proj,) + mine["fk"], (proj,) + mine["fv"], c_cum],
                              [place[nm] for nm in ("pq", "pk", "aq", "ak", "oq", "ok")], [(wa, BF16)] * 3,
                              name="fox_prep")
    as_seq = lambda t2d: t2d.reshape(n_seq, seq, t2d.shape[1])
    o_fox, lse = fox_fwd(as_seq(qa), as_seq(ka), as_seq(va), name="fox_fwd")
    y_b = o_fox.reshape(T, wa)
    wb_pad = jnp.concatenate([w_b_full.reshape(fh, FOX_HDIM, d), jnp.zeros((fh, FOX_AUG - FOX_HDIM, d), BF16)],
                             axis=1).reshape(wa, d)

    pa = matmul_nn(y_a, w_a_v, name="proj_a")
    pb = matmul_nn(y_b, wb_pad, name="proj_b")

    def merge_fn(rows, vecs):
        ga, gb, a, b = rows
        return [_sigmoid(ga) * a + _sigmoid(gb) * b], []
    (merged,), _ = rowwise(merge_fn, [(proj,) + mine["ga"], (proj,) + mine["gb"], pa, pb], [], [(d, BF16)],
                           name="merge_fwd")
    mix = matmul_nn(merged, w_o_v, name="out_proj")

    def ln1_fn(rows, vecs):
        z = ALPHA * rows[0] + rows[1]
        h = _ln_stats(z) * vecs[0] + vecs[1]
        return [z, h, h], []
    (z1, h1, h1b), _ = rowwise(ln1_fn, [h0, mix], [ln1_g, ln1_b], [(d, F32), (d, F32), (d, BF16)], name="ln1_fwd")

    relu2 = lambda u: jnp.square(jnp.maximum(u, 0.0))
    act = matmul_nn(h1b, w_ff1_v, name="ff1", out_dtype=BF16, epilogue=relu2)
    ff = matmul_nn(act, w_ff2_v, name="ff2")
    pg = matmul_nn(h1b, w_pg_v, name="ple_gate")
    pe = matmul_nn(p_b, w_p_v, name="ple_embed")

    def head_fn(rows, vecs):
        h1v, ffv, pgv, pev, t = rows
        g2, b2 = vecs
        sp = _sigmoid(pgv)
        z = ALPHA * h1v + ffv + sp * pev
        y = _ln_stats(z) * g2 + b2
        err = y - t
        loss_rows = 0.5 * jnp.mean(err * err, axis=-1, keepdims=True)
        dy = err * (1.0 / d)
        dz, dg2, db2 = _ln_bwd(z, dy, g2)
        loss_acc = jnp.broadcast_to(_colsum(loss_rows), (1, LANES))
        return [dz, dz, dz * pev * (sp * (1.0 - sp)), dz * sp], [dg2, db2, loss_acc]
    (dz2, dz2b, dpg, dpe), (g_ln2_g, g_ln2_b, loss_part) = rowwise(
        head_fn, [h1, ff, pg, pe, tgt], [ln2_g, ln2_b],
        [(d, F32), (d, BF16), (d, BF16), (d, BF16)], [d, d, LANES], name="head_fwd_bwd")

    dact = lambda da, a: da * (2.0 * jnp.sqrt(a.astype(F32)))
    du = matmul_nn(dz2b, w_ff2_v, transpose_rhs=True, name="d_ff2", out_dtype=BF16, epilogue=dact, aux=act)
    dh1_ff = matmul_nn(du, w_ff1_v, transpose_rhs=True, name="d_ff1")
    dh1_pg = matmul_nn(dpg, w_pg_v, transpose_rhs=True, name="d_ple_gate")

    def ln1_bwd_fn(rows, vecs):
        dh1 = ALPHA * rows[0] + rows[1] + rows[2]
        dz, dg, db = _ln_bwd(rows[3], dh1, vecs[0])
        return [dz, dz], [dg, db]
    (dz1, dz1b), (g_ln1_g, g_ln1_b) = rowwise(ln1_bwd_fn, [dz2, dh1_ff, dh1_pg, z1], [ln1_g],
                                              [(d, F32), (d, BF16)], [d, d], name="ln1_bwd")
    dmerged = matmul_nn(dz1b, w_o_v, transpose_rhs=True, name="d_out_proj")

    def merge_bwd_fn(rows, vecs):
        dm, ga, gb, a, b = rows
        sa, sb = _sigmoid(ga), _sigmoid(gb)
        return [dm * a * (sa * (1.0 - sa)), dm * b * (sb * (1.0 - sb)), dm * sa, dm * sb], []
    (dga, dgb, dma, dmb), _ = rowwise(merge_bwd_fn, [dmerged, (proj,) + mine["ga"], (proj,) + mine["gb"], pa, pb], [],
                                      [(d, BF16)] * 4, name="merge_bwd")
    dya = matmul_nn(dma, w_a_v, transpose_rhs=True, name="d_proj_a")
    dyb = matmul_nn(dmb, wb_pad, transpose_rhs=True, name="d_proj_b", out_dtype=BF16)

    def ya_bwd_fn(rows, vecs):
        o, hg, dy = rows
        ng = vecs[0]
        sg = _sigmoid(hg)
        gate = hg * sg
        dn_parts, do_parts, n_parts = [], [], []
        for h in range(HG_HEADS):
            hs = slice(h * HG_DIM, (h + 1) * HG_DIM)
            oh = o[:, hs]
            r = lax.rsqrt(jnp.mean(oh * oh, axis=-1, keepdims=True) + RMS_EPS)
            nh = oh * r
            dn = dy[:, hs] * ng[:, hs] * gate[:, hs]
            do_parts.append(r * (dn - nh * jnp.mean(dn * nh, axis=-1, keepdims=True)))
            n_parts.append(nh)
        nrm = jnp.concatenate(n_parts, axis=1)
        dhg = dy * nrm * ng * (sg * (1.0 + hg * (1.0 - sg)))
        return [jnp.concatenate(do_parts, axis=1), dhg], [_colsum(dy * nrm * gate)]
    (do_raw, dhg), (g_norm_g,) = rowwise(ya_bwd_fn, [o_raw, (proj,) + mine["hg"], dya], [hg_norm_g],
                                         [(hw, F32), (hw, BF16)], [hw], name="hgrn2_out_bwd")
    dhq, dhf, dhi, g_lb = hgrn2_bwd(proj, [mine["hq"][0], mine["hf"][0], mine["hi"][0]], lb, do_raw, hg_states,
                                    n_seq, seq, name="hgrn2_bwd")

    do_fox = as_seq(dyb)
    dqa, rsum, delta = fox_bwd_dq(as_seq(qa), as_seq(ka), as_seq(va), do_fox, lse, name="fox_bwd_dq")
    dka, dva, dsum = fox_bwd_dkv(as_seq(qa), as_seq(ka), as_seq(va), do_fox, delta, lse, name="fox_bwd_dkv")

    def unprep_fn(rows, vecs):
        mm = lambda a_, b_: jnp.dot(a_.astype(BF16), b_, preferred_element_type=F32)
        return [mm(rows[0], vecs[0]), mm(rows[1], vecs[1]), mm(rows[2], vecs[1])], []
    (dfq, dfk, dfv), _ = rowwise(unprep_fn, [dqa.reshape(T, wa), dka.reshape(T, wa), dva.reshape(T, wa)],
                                 [place["pqt"], place["pkt"]], [(hw, BF16)] * 3, name="fox_unprep")
    dc = rsum.reshape(T, fh, FOX_AUG)[:, :, 0] - dsum.reshape(n_seq, fh, seq).transpose(0, 2, 1).reshape(T, fh)
    dc = jnp.concatenate([dc, jnp.zeros((T, LANES - fh), F32)], axis=1)
    dlf = seq_cumsum(dc, n_seq, seq, reverse=True, name="fox_cumsum_bwd")

    def lf_bwd_fn(rows, vecs):
        u = rows[0] + vecs[0]
        du_ = rows[1] * _sigmoid(-u)
        return [du_], [_colsum(du_)]
    (dff_,), (g_fb,) = rowwise(lf_bwd_fn, [(proj,) + mine["ff"], dlf], [fb_pad], [(LANES, BF16)], [LANES],
                               name="fox_logf_bwd")

    dproj = jnp.concatenate([dga, dgb, dhq, dhf, dhi, dhg, dfq, dfk, dfv, dff_], axis=1)
    dh0_in = matmul_nn(dproj, win_mine, transpose_rhs=True, name="d_in_proj")

    def ln0_bwd_fn(rows, vecs):
        dh0 = rows[0] + ALPHA * rows[1]
        dx, dg, db = _ln_bwd(rows[2], dh0, vecs[0])
        return [dx], [dg, db]
    (dx,), (g_ln0_g, g_ln0_b) = rowwise(ln0_bwd_fn, [dh0_in, dz1, x2], [vec(ln0_g)], [(d, F32)], [d, d],
                                        name="ln0_bwd")

    gw_in_mine = matmul_tn(h0b, dproj, name="g_w_in")
    gw_in = jnp.concatenate([gw_in_mine[:, mine[nm][0]:mine[nm][0] + orig[nm][1]]
                             for nm in ["hq", "hf", "hi", "hg", "fq", "fk", "fv", "ff", "ga", "gb"]], axis=1)
    gfull = {
        "w_in": gw_in,
        "w_a": matmul_tn(y_a, dma, name="g_w_a"),
        "w_b": matmul_tn(y_b, dmb, name="g_w_b").reshape(fh, FOX_AUG, d)[:, :FOX_HDIM].reshape(hw, d),
        "w_o": matmul_tn(merged, dz1b, name="g_w_o"),
        "w_ff1": matmul_tn(h1b, du, name="g_w_ff1"),
        "w_ff2": matmul_tn(act, dz2b, name="g_w_ff2"),
        "w_pg": matmul_tn(h1b, dpg, name="g_w_pg"),
        "w_p": matmul_tn(p_b, dpe, name="g_w_p"),
    }

    def chip_parts(nm, s):
        g = gfull[nm]
        n = g.shape[axis[nm]] // 4
        return lax.slice_in_dim(g, s * n, (s + 1) * n, axis=axis[nm])
    grads = [jnp.stack([pack_a(chip_parts("w_in", s)) for s in range(4)]),
             jnp.stack([pack_b({nm: chip_parts(nm, s) for nm in names if nm != "w_in"}, d) for s in range(4)])]
    me = 2 * lax.axis_index("x") + lax.axis_index("y")
    core = lax.axis_index("c")
    halves = [g.shape[1] // 2 for g in grads]
    keep = [lax.dynamic_slice_in_dim(g, core * h, h, axis=1) for g, h in zip(grads, halves)]
    give = [lax.dynamic_slice_in_dim(g, (1 - core) * h, h, axis=1).astype(BF16) for g, h in zip(grads, halves)]
    from_core = swap_cores(give, name="swap_partials")

    def sum2_fn(rows, vecs):
        s = rows[0] + rows[1].astype(F32)
        return [s, s], []

    def sum4_fn(rows, vecs):
        a, r0, r1, r2 = rows
        return [((a + r0.astype(F32)) + r1.astype(F32)) + r2.astype(F32)], []
    pair, pair_b = [], []
    for b, (kp, fc, h) in enumerate(zip(keep, from_core, halves)):
        cols = kp.shape[2]
        (s32, s16), _ = rowwise(sum2_fn, [kp.reshape(4 * h, cols), fc.reshape(4 * h, cols)], [],
                                [(cols, F32), (cols, BF16)], name=f"sum_cores_{b}", tm=SUM_TILE)
        pair.append(s32.reshape(4, h, cols))
        pair_b.append(s16.reshape(4, h, cols))
    got = scatter_chips(pair_b, name="scatter_grads")
    q_half = []
    for b, (pr, gt) in enumerate(zip(pair, got)):
        own = lax.dynamic_index_in_dim(pr, me, axis=0, keepdims=False)
        (q,), _ = rowwise(sum4_fn, [own, gt[0], gt[1], gt[2]], [], [(own.shape[1], F32)], name=f"sum_chips_{b}",
                          tm=SUM_TILE)
        q_half.append(q)
    q_other = swap_cores(q_half, name="swap_halves")
    g_a, g_b = [jnp.concatenate([jnp.where(core == 0, mine_, other), jnp.where(core == 0, other, mine_)], axis=0)
                for mine_, other in zip(q_half, q_other)]
    g_shards = unpack_b(g_b, lay)
    g_shards["w_in"] = g_a[:, :in_cols]

    def row1024(*parts):
        r = jnp.concatenate([q.reshape(1, -1) for q in parts], axis=1)
        return jnp.concatenate([r, jnp.zeros((1, PACK_W - r.shape[1]), F32)], axis=1) if r.shape[1] < PACK_W else r
    small_rows = [row1024(g_ln0_g), row1024(g_ln0_b), row1024(g_ln1_g), row1024(g_ln1_b), row1024(g_ln2_g),
                  row1024(g_ln2_b), row1024(g_norm_g, g_lb), row1024(g_fb[:, :fh], loss_part[:, :1])]
    small = allreduce_small(jnp.concatenate(small_rows, axis=0), name="allreduce_small")
    s_ln0_g, s_ln0_b, s_ln1_g, s_ln1_b, s_ln2_g, s_ln2_b = [small[r:r + 1] for r in range(6)]
    s_norm_g, s_lb = small[6:7, :hw], small[6:7, hw:2 * hw]
    s_fb, loss = small[7:8, :fh], small[7, fh]
    p0 = probs[0:1]
    jac = p0 * (1.0 - p0)
    s_hg_lb = jnp.concatenate([s_lb * jac, -s_lb * jac], axis=0)

    small_w = [vec(ln0_g), vec(ln0_b), ln1_g, ln1_b, ln2_g, ln2_b, hg_lb.reshape(1, -1), hg_norm_g, fox_fb]
    small_g = [s_ln0_g, s_ln0_b, s_ln1_g, s_ln1_b, s_ln2_g, s_ln2_b, s_hg_lb.reshape(1, -1), s_norm_g, s_fb]
    small_m = [vec(m_ln0_g), vec(m_ln0_b), m_ln1_g, m_ln1_b, m_ln2_g, m_ln2_b, m_hg_lb.reshape(1, -1), m_hg_norm_g, m_fox_fb]
    small_v = [vec(v_ln0_g), vec(v_ln0_b), v_ln1_g, v_ln1_b, v_ln2_g, v_ln2_b, v_hg_lb.reshape(1, -1), v_hg_norm_g, v_fox_fb]
    pad_rows = lambda lst, fill: jnp.concatenate(
        [row1024(a) if fill == 0.0 else jnp.concatenate([a.reshape(1, -1), jnp.full((1, PACK_W - a.size), fill, F32)], axis=1)
         for a in lst] + [jnp.full((16 - len(lst), PACK_W), fill, F32)], axis=0)
    sd, sm, sv = adamw(pad_rows(small_w, 0.0), pad_rows(small_g, 0.0), pad_rows(small_m, 0.0), pad_rows(small_v, 1.0),
                       name="adamw_small")
    small_shapes = [ln0_g.shape, ln0_b.shape, ln1_g.shape, ln1_b.shape, ln2_g.shape, ln2_b.shape, hg_lb.shape,
                    hg_norm_g.shape, fox_fb.shape]
    take = lambda buf: [buf[r, :int(np.prod(shp))].reshape(shp) for r, shp in enumerate(small_shapes)]
    sg_out, sd_out, sm_out, sv_out = [g.reshape(shp) for g, shp in zip(small_g, small_shapes)], take(sd), take(sm), take(sv)

    big_out = {}
    for nm in names:
        delta, m2, v2 = adamw(big[nm], g_shards[nm], big_m[nm], big_v[nm], name="adamw_" + nm)
        big_out[nm] = (g_shards[nm][None], delta[None], m2[None], v2[None])

    def ordered(k):
        sm_ = [sg_out, sd_out, sm_out, sv_out][k]
        bg = lambda nm: big_out[nm][k]
        return [sm_[0], sm_[1], bg("w_in"), sm_[6], sm_[7], sm_[8], bg("w_a"), bg("w_b"), bg("w_o"), sm_[2], sm_[3],
                bg("w_ff1"), bg("w_ff2"), bg("w_pg"), bg("w_p"), sm_[4], sm_[5]]
    grad_x = dx.reshape(n_seq, seq, d)
    return (loss, grad_x, *ordered(0), *ordered(1), *ordered(2), *ordered(3))
```

```python
import functools
from typing import NamedTuple, Optional

import numpy as np
import jax
import jax.numpy as jnp
from jax import lax
from jax.experimental import pallas as pl
from jax.experimental.pallas import tpu as pltpu

F32 = jnp.float32
BF16 = jnp.bfloat16
MESH = pl.DeviceIdType.MESH

VMEM_LIMIT_BYTES = 48 * 1024 * 1024
LANES = 128
HG_HEADS = 4
HG_DIM = 128
HG_BLK = 16
HG_TILE = 256
FOX_HDIM = 64
FOX_AUG = 128
FOX_TQ = 1024
LN_EPS = 1e-5
RMS_EPS = 1e-6
DEPTH = 1
ALPHA = (2.0 * DEPTH) ** 0.25
ADAM_LR, ADAM_B1, ADAM_B2, ADAM_EPS, ADAM_WD, ADAM_STEP = 0.001, 0.9, 0.999, 1e-08, 0.01, 10
NEG_INF = -1e30


def _cparams(sem):
    return pltpu.CompilerParams(dimension_semantics=sem, vmem_limit_bytes=VMEM_LIMIT_BYTES)


def _tile(n, cap):
    if n <= cap:
        return n
    best = None
    for t in range(LANES, cap + 1, LANES):
        if n % t == 0:
            best = t
    assert best is not None, (n, cap)
    return best


class WView(NamedTuple):
    arr: jax.Array
    r0: int
    c0: int
    k: int
    n: int
    split: Optional[int]


def matmul_nn(a, w, *, name, transpose_rhs=False, out_dtype=F32, epilogue=None, aux=None, tm=1024):
    wv = w if isinstance(w, WView) else WView(w[None], 0, 0, w.shape[0], w.shape[1], None)
    rows_s = wv.k // 4 if wv.split == 0 else wv.k
    cols_s = wv.n // 4 if wv.split == 1 else wv.n
    tr, tc = _tile(rows_s, 1152), _tile(cols_s, 1152)
    assert wv.r0 % tr == 0 and wv.c0 % tc == 0
    T, K = a.shape
    N, tn, tk = (wv.k, tr, tc) if transpose_rhs else (wv.n, tc, tr)
    assert K == (wv.n if transpose_rhs else wv.k)
    tm = min(tm, T)
    assert T % tm == 0
    nk = K // tk

    def w_block(ri, ci):
        if wv.split == 0:
            return (ri * tr) // rows_s, (wv.r0 + (ri * tr) % rows_s) // tr, wv.c0 // tc + ci
        if wv.split == 1:
            return (ci * tc) // cols_s, wv.r0 // tr + ri, (wv.c0 + (ci * tc) % cols_s) // tc
        return 0, wv.r0 // tr + ri, wv.c0 // tc + ci

    def body(*refs):
        if aux is None:
            a_ref, w_ref, o_ref, acc_ref = refs
            x_ref = None
        else:
            a_ref, w_ref, x_ref, o_ref, acc_ref = refs
        k = pl.program_id(2)
        if transpose_rhs:
            part = lax.dot_general(a_ref[...], w_ref[...], (((1,), (1,)), ((), ())), preferred_element_type=F32)
        else:
            part = jnp.dot(a_ref[...], w_ref[...], preferred_element_type=F32)

        def write(res):
            if epilogue is not None:
                res = epilogue(res) if x_ref is None else epilogue(res, x_ref[...])
            o_ref[...] = res.astype(out_dtype)

        if nk == 1:
            write(part)
        else:
            @pl.when(k == 0)
            def _():
                acc_ref[...] = part

            @pl.when(k > 0)
            def _():
                acc_ref[...] += part

            @pl.when(k == nk - 1)
            def _():
                write(acc_ref[...])

    w_index = (lambda n, m, k: w_block(n, k)) if transpose_rhs else (lambda n, m, k: w_block(k, n))
    in_specs = [pl.BlockSpec((tm, tk), lambda n, m, k: (m, k)),
                pl.BlockSpec((None, tr, tc), w_index)]
    args = [a, wv.arr]
    if aux is not None:
        in_specs.append(pl.BlockSpec((tm, tn), lambda n, m, k: (m, n)))
        args.append(aux)
    return pl.pallas_call(
        body, name=name,
        grid=(N // tn, T // tm, nk),
        in_specs=in_specs,
        out_specs=pl.BlockSpec((tm, tn), lambda n, m, k: (m, n)),
        out_shape=jax.ShapeDtypeStruct((T, N), out_dtype),
        scratch_shapes=[pltpu.VMEM((tm, tn) if nk > 1 else (8, LANES), F32)],
        compiler_params=_cparams(("parallel", "parallel", "arbitrary")),
    )(*args)


def matmul_tn(a, b, *, name, tk=1024):
    T, M = a.shape
    T2, N = b.shape
    tk = min(tk, T)
    assert T == T2 and T % tk == 0
    tm = _tile(M, 1024)
    tn = _tile(N, 1152)

    def body(a_ref, b_ref, o_ref):
        k = pl.program_id(2)
        part = lax.dot_general(a_ref[...], b_ref[...], (((0,), (0,)), ((), ())), preferred_element_type=F32)

        @pl.when(k == 0)
        def _():
            o_ref[...] = part

        @pl.when(k > 0)
        def _():
            o_ref[...] += part

    return pl.pallas_call(
        body, name=name,
        grid=(M // tm, N // tn, T // tk),
        in_specs=[pl.BlockSpec((tk, tm), lambda m, n, k: (k, m)),
                  pl.BlockSpec((tk, tn), lambda m, n, k: (k, n))],
        out_specs=pl.BlockSpec((tm, tn), lambda m, n, k: (m, n)),
        out_shape=jax.ShapeDtypeStruct((M, N), F32),
        compiler_params=_cparams(("parallel", "parallel", "arbitrary")),
    )(a, b)


def rowwise(fn, rows, vecs, outs, accs=(), *, name, tm=512):
    rows = [r if isinstance(r, tuple) else (r, 0, r.shape[1]) for r in rows]
    T = rows[0][0].shape[0]
    tm = min(tm, T)
    assert T % tm == 0
    n_rows, n_vecs, n_outs, n_accs = len(rows), len(vecs), len(outs), len(accs)

    def body(*refs):
        row_refs = refs[:n_rows]
        vec_refs = refs[n_rows:n_rows + n_vecs]
        out_refs = refs[n_rows + n_vecs:n_rows + n_vecs + n_outs]
        acc_refs = refs[n_rows + n_vecs + n_outs:]
        out_vals, acc_vals = fn([r[...] for r in row_refs], [v[...] for v in vec_refs])
        assert len(out_vals) == n_outs and len(acc_vals) == n_accs
        for r, val in zip(out_refs, out_vals):
            r[...] = val.astype(r.dtype)
        if n_accs:
            i = pl.program_id(0)

            @pl.when(i == 0)
            def _():
                for r in acc_refs:
                    r[...] = jnp.zeros_like(r)

            for r, val in zip(acc_refs, acc_vals):
                r[...] += val

    in_specs = []
    for arr, off, width in rows:
        assert off % width == 0
        in_specs.append(pl.BlockSpec((tm, width), functools.partial(lambda i, blk: (i, blk), blk=off // width)))
    for v in vecs:
        in_specs.append(pl.BlockSpec(v.shape, lambda i: (0, 0)))
    out_specs = [pl.BlockSpec((tm, w), lambda i: (i, 0)) for w, _ in outs]
    out_specs += [pl.BlockSpec((1, w), lambda i: (0, 0)) for w in accs]
    out_shape = [jax.ShapeDtypeStruct((T, w), dt) for w, dt in outs]
    out_shape += [jax.ShapeDtypeStruct((1, w), F32) for w in accs]
    res = pl.pallas_call(
        body, name=name,
        grid=(T // tm,),
        in_specs=in_specs, out_specs=out_specs, out_shape=out_shape,
        compiler_params=_cparams(("arbitrary",) if n_accs else ("parallel",)),
    )(*[r[0] for r in rows], *vecs)
    return res[:n_outs], res[n_outs:]


def _colsum(x):
    return jnp.sum(x, axis=0, keepdims=True)


def _sigmoid(x):
    return 1.0 / (1.0 + jnp.exp(-x))


def _ln_stats(z):
    mu = jnp.mean(z, axis=-1, keepdims=True)
    zc = z - mu
    var = jnp.mean(zc * zc, axis=-1, keepdims=True)
    return zc * lax.rsqrt(var + LN_EPS)


def _ln_bwd(zhat_src, dy, g):
    mu = jnp.mean(zhat_src, axis=-1, keepdims=True)
    zc = zhat_src - mu
    var = jnp.mean(zc * zc, axis=-1, keepdims=True)
    rstd = lax.rsqrt(var + LN_EPS)
    zh = zc * rstd
    dzh = dy * g
    dz = rstd * (dzh - jnp.mean(dzh, axis=-1, keepdims=True) - zh * jnp.mean(dzh * zh, axis=-1, keepdims=True))
    return dz, _colsum(dy * zh), _colsum(dy)


def _hg_constants():
    r = np.arange(HG_TILE)
    same = (r[:, None] // HG_BLK) == (r[None, :] // HG_BLK)
    lower = (same & (r[None, :] <= r[:, None])).astype(np.float32)
    upper = (same & (r[None, :] >= r[:, None])).astype(np.float32)
    total = same.astype(np.float32)
    w = HG_HEADS * HG_DIM
    c = np.arange(w)
    bd = ((c[:, None] // HG_DIM) == (c[None, :] // HG_DIM)).astype(np.float32)
    n = HG_BLK * HG_BLK
    rr = np.arange(n)
    sel_t = (rr[None, :] // HG_BLK == np.arange(HG_BLK)[:, None]).astype(np.float32)
    sel_s = (rr[None, :] % HG_BLK == np.arange(HG_BLK)[:, None]).astype(np.float32)
    as_bf = lambda m: jnp.asarray(m, dtype=BF16)
    return as_bf(lower), as_bf(upper), as_bf(total), as_bf(bd), as_bf(sel_t), as_bf(sel_s)


def _keep_bf16_bits(x):
    bits = lax.bitcast_convert_type(x, jnp.int32) & jnp.int32(-65536)
    return lax.bitcast_convert_type(bits, F32)


def _split3(x):
    hi = _keep_bf16_bits(x)
    r1 = x - hi
    mid = _keep_bf16_bits(r1)
    lo = _keep_bf16_bits(r1 - mid)
    return hi.astype(BF16), mid.astype(BF16), lo.astype(BF16)


def _dot3(m01, x):
    hi, mid, lo = _split3(x)
    d = lambda p: jnp.dot(m01, p, preferred_element_type=F32)
    return (d(lo) + d(mid)) + d(hi)


def _hg_prologue(hq, hf, lb, lower, total):
    sq = _sigmoid(hq)
    q = hq * sq
    sg = _sigmoid(hf)
    f = lb + (1.0 - lb) * sg
    g = jnp.log(f)
    k = 1.0 - f
    b = _dot3(lower, g)
    bl = _dot3(total, g)
    return q, k, f, sg, sq, b, bl


def _stack16(fn):
    return [fn(t) for t in range(HG_BLK)]


def hgrn2_fwd(proj, offs, lb, n_seq, seq, *, name):
    T = n_seq * seq
    W = HG_HEADS * HG_DIM
    n_tiles = seq // HG_TILE
    nb = HG_TILE // HG_BLK
    lower, _, total, bd, sel_t, _ = _hg_constants()

    def body(hq_ref, hf_ref, hi_ref, lb_ref, lower_ref, total_ref, bd_ref, selt_ref,
             o_ref, st_out_ref,
             st_ref, q_s, k_s, v_s, b_s, qt_s, kt_s, d_s, p_s):
        @pl.when(pl.program_id(1) == 0)
        def _():
            st_ref[...] = jnp.zeros_like(st_ref)

        q, k, _, _, _, b, bl = _hg_prologue(hq_ref[...], hf_ref[...], lb_ref[...], lower_ref[...], total_ref[...])
        q_s[...] = q
        k_s[...] = k
        v_s[...] = hi_ref[...]
        b_s[...] = b
        qt_s[...] = q * jnp.exp(b)
        kt_s[...] = k * jnp.exp(jnp.minimum(bl - b, 0.0))
        d_s[...] = jnp.exp(bl)
        rowi = lax.broadcasted_iota(jnp.int32, (HG_BLK, W), 0)

        def block(i, carry):
            r0 = pl.multiple_of(i * HG_BLK, HG_BLK)
            rows = pl.ds(r0, HG_BLK)
            qi, ki, vi, bi = q_s[rows, :], k_s[rows, :], v_s[rows, :], b_s[rows, :]
            for t in range(HG_BLK):
                e = jnp.where(rowi <= t, jnp.exp(jnp.minimum(bi[t:t + 1, :] - bi, 0.0)), 0.0)
                p_s[pl.ds(t * HG_BLK, HG_BLK), :] = (e * qi[t:t + 1, :] * ki).astype(BF16)
            a_b = jnp.dot(p_s[...], bd_ref[...], preferred_element_type=F32)
            vt = jnp.concatenate([vi] * HG_BLK, axis=0)
            o_blk = jnp.dot(selt_ref[...], (a_b * vt).astype(BF16), preferred_element_type=F32)
            qti, kti, di = qt_s[rows, :], kt_s[rows, :], d_s[rows, :]
            outs = []
            for h in range(HG_HEADS):
                hs = slice(h * HG_DIM, (h + 1) * HG_DIM)
                st_h = st_ref[hs, :]
                st_out_ref[i, hs, :] = st_h
                outs.append(lax.dot_general(qti[:, hs].astype(BF16), st_h.astype(BF16),
                                            (((1,), (1,)), ((), ())), preferred_element_type=F32))
                upd = lax.dot_general(vi[:, hs].astype(BF16), kti[:, hs].astype(BF16),
                                      (((0,), (0,)), ((), ())), preferred_element_type=F32)
                st_ref[hs, :] = st_h * di[0:1, hs] + upd
            o_ref[rows, :] = o_blk + jnp.concatenate(outs, axis=1)
            return carry

        lax.fori_loop(0, nb, block, 0, unroll=2)

    col = lambda off: functools.partial(lambda s, t, blk: (s * n_tiles + t, blk), blk=off // W)
    const = lambda m: pl.BlockSpec(m.shape, lambda s, t: (0, 0))
    tile_f32 = pltpu.VMEM((HG_TILE, W), F32)
    o, states = pl.pallas_call(
        body, name=name,
        grid=(n_seq, n_tiles),
        in_specs=[pl.BlockSpec((HG_TILE, W), col(offs[0])), pl.BlockSpec((HG_TILE, W), col(offs[1])),
                  pl.BlockSpec((HG_TILE, W), col(offs[2])), const(lb), const(lower), const(total), const(bd),
                  const(sel_t)],
        out_specs=[pl.BlockSpec((HG_TILE, W), lambda s, t: (s * n_tiles + t, 0)),
                   pl.BlockSpec((nb, W, HG_DIM), lambda s, t: (s * n_tiles + t, 0, 0))],
        out_shape=[jax.ShapeDtypeStruct((T, W), F32), jax.ShapeDtypeStruct((T // HG_BLK, W, HG_DIM), F32)],
        scratch_shapes=[pltpu.VMEM((W, HG_DIM), F32)] + [tile_f32] * 7
                       + [pltpu.VMEM((HG_BLK * HG_BLK, W), BF16)],
        compiler_params=_cparams(("arbitrary", "arbitrary")),
    )(proj, proj, proj, lb, lower, total, bd, sel_t)
    return o, states


def hgrn2_bwd(proj, offs, lb, do, states, n_seq, seq, *, name):
    T = n_seq * seq
    W = HG_HEADS * HG_DIM
    n_tiles = seq // HG_TILE
    nb = HG_TILE // HG_BLK
    lower, upper, total, bd, sel_t, sel_s = _hg_constants()

    def body(hq_ref, hf_ref, hi_ref, do_ref, st_in_ref, lb_ref, lower_ref, upper_ref, total_ref, bd_ref,
             selt_ref, sels_ref,
             dhq_ref, dhf_ref, dhi_ref, dlb_ref,
             dst_ref, q_s, k_s, v_s, b_s, qt_s, kt_s, d_s, eb_s, ekb_s, dq_s, dk_s, db_s, dv_s,
             p_s, e_s, w_s):
        first = jnp.logical_and(pl.program_id(0) == 0, pl.program_id(1) == 0)

        @pl.when(first)
        def _():
            dlb_ref[...] = jnp.zeros_like(dlb_ref)

        @pl.when(pl.program_id(1) == 0)
        def _():
            dst_ref[...] = jnp.zeros_like(dst_ref)

        hq, lbv = hq_ref[...], lb_ref[...]
        q, k, f, sg, sq, b, bl = _hg_prologue(hq, hf_ref[...], lbv, lower_ref[...], total_ref[...])
        eb = jnp.exp(b)
        ekb = jnp.exp(jnp.minimum(bl - b, 0.0))
        q_s[...] = q
        k_s[...] = k
        v_s[...] = hi_ref[...]
        b_s[...] = b
        eb_s[...] = eb
        ekb_s[...] = ekb
        qt_s[...] = q * eb
        kt_s[...] = k * ekb
        d_s[...] = jnp.exp(bl)
        rowi = lax.broadcasted_iota(jnp.int32, (HG_BLK, W), 0)
        last_row = rowi == HG_BLK - 1

        def block(j, carry):
            i = nb - 1 - j
            r0 = pl.multiple_of(i * HG_BLK, HG_BLK)
            rows = pl.ds(r0, HG_BLK)
            qi, ki, vi, bi, doi = q_s[rows, :], k_s[rows, :], v_s[rows, :], b_s[rows, :], do_ref[rows, :]
            for t in range(HG_BLK):
                sl = pl.ds(t * HG_BLK, HG_BLK)
                e = jnp.where(rowi <= t, jnp.exp(jnp.minimum(bi[t:t + 1, :] - bi, 0.0)), 0.0)
                e_s[sl, :] = e
                p_s[sl, :] = (e * qi[t:t + 1, :] * ki).astype(BF16)
                w_s[sl, :] = (doi[t:t + 1, :] * vi).astype(BF16)
            a_b = jnp.dot(p_s[...], bd_ref[...], preferred_element_type=F32)
            da_b = jnp.dot(w_s[...], bd_ref[...], preferred_element_type=F32)
            x = da_b * e_s[...]
            k_til = jnp.concatenate([ki] * HG_BLK, axis=0)
            q_rep = jnp.concatenate([jnp.broadcast_to(qi[t:t + 1, :], (HG_BLK, W)) for t in range(HG_BLK)], axis=0)
            do_rep = jnp.concatenate([jnp.broadcast_to(doi[t:t + 1, :], (HG_BLK, W)) for t in range(HG_BLK)], axis=0)
            dq_in = jnp.dot(selt_ref[...], (x * k_til).astype(BF16), preferred_element_type=F32)
            dk_in = jnp.dot(sels_ref[...], (x * q_rep).astype(BF16), preferred_element_type=F32)
            dv_in = jnp.dot(sels_ref[...], (a_b * do_rep).astype(BF16), preferred_element_type=F32)
            qti, kti, di = qt_s[rows, :], kt_s[rows, :], d_s[rows, :]
            dqt, dkt, dvt, dd = [], [], [], []
            for h in range(HG_HEADS):
                hs = slice(h * HG_DIM, (h + 1) * HG_DIM)
                st_h = st_in_ref[i, hs, :]
                dst_h = dst_ref[hs, :]
                do_h, v_h = doi[:, hs].astype(BF16), vi[:, hs].astype(BF16)
                dst_b = dst_h.astype(BF16)
                dqt.append(jnp.dot(do_h, st_h.astype(BF16), preferred_element_type=F32))
                dkt.append(jnp.dot(v_h, dst_b, preferred_element_type=F32))
                dvt.append(lax.dot_general(kti[:, hs].astype(BF16), dst_b, (((1,), (1,)), ((), ())),
                                           preferred_element_type=F32))
                dd.append(jnp.sum(dst_h * st_h, axis=0, keepdims=True))
                upd = lax.dot_general(do_h, qti[:, hs].astype(BF16), (((0,), (0,)), ((), ())),
                                      preferred_element_type=F32)
                dst_ref[hs, :] = dst_h * di[0:1, hs] + upd
            dqt = jnp.concatenate(dqt, axis=1)
            dkt = jnp.concatenate(dkt, axis=1)
            dvt = jnp.concatenate(dvt, axis=1)
            dd = jnp.concatenate(dd, axis=1)
            dbl = jnp.sum(dkt * kti, axis=0, keepdims=True) + dd * di[0:1, :]
            db = qi * dq_in - ki * dk_in + dqt * qti - dkt * kti
            db_s[rows, :] = db + jnp.where(last_row, dbl, 0.0)
            dq_s[rows, :] = dq_in + dqt * eb_s[rows, :]
            dk_s[rows, :] = dk_in + dkt * ekb_s[rows, :]
            dv_s[rows, :] = dv_in + dvt
            return carry

        lax.fori_loop(0, nb, block, 0, unroll=2)

        dg = _dot3(upper_ref[...], db_s[...])
        dhq_ref[...] = (dq_s[...] * (sq * (1.0 + hq * (1.0 - sq)))).astype(dhq_ref.dtype)
        df = dg / f - dk_s[...]
        dhf_ref[...] = (df * (1.0 - lbv) * (sg * (1.0 - sg))).astype(dhf_ref.dtype)
        dhi_ref[...] = dv_s[...].astype(dhi_ref.dtype)
        dlb_ref[...] += _colsum(df * (1.0 - sg))

    rev = lambda s, t: s * n_tiles + (n_tiles - 1 - t)
    col = lambda off: functools.partial(lambda s, t, blk: (rev(s, t), blk), blk=off // W)
    const = lambda m: pl.BlockSpec(m.shape, lambda s, t: (0, 0))
    row = pl.BlockSpec((HG_TILE, W), lambda s, t: (rev(s, t), 0))
    tile_f32 = pltpu.VMEM((HG_TILE, W), F32)
    n2 = HG_BLK * HG_BLK
    return pl.pallas_call(
        body, name=name,
        grid=(n_seq, n_tiles),
        in_specs=[pl.BlockSpec((HG_TILE, W), col(offs[0])), pl.BlockSpec((HG_TILE, W), col(offs[1])),
                  pl.BlockSpec((HG_TILE, W), col(offs[2])), row,
                  pl.BlockSpec((nb, W, HG_DIM), lambda s, t: (rev(s, t), 0, 0)),
                  const(lb), const(lower), const(upper), const(total), const(bd), const(sel_t), const(sel_s)],
        out_specs=[row, row, row, pl.BlockSpec((1, W), lambda s, t: (0, 0))],
        out_shape=[jax.ShapeDtypeStruct((T, W), BF16)] * 3 + [jax.ShapeDtypeStruct((1, W), F32)],
        scratch_shapes=[pltpu.VMEM((W, HG_DIM), F32)] + [tile_f32] * 13
                       + [pltpu.VMEM((n2, W), BF16), pltpu.VMEM((n2, W), F32), pltpu.VMEM((n2, W), BF16)],
        compiler_params=_cparams(("arbitrary", "arbitrary")),
    )(proj, proj, proj, do, states, lb, lower, upper, total, bd, sel_t, sel_s)


def _diag_mask(tq):
    return lax.broadcasted_iota(jnp.int32, (tq, tq), 1) <= lax.broadcasted_iota(jnp.int32, (tq, tq), 0)


def _qk(q, k):
    return lax.dot_general(q, k, (((1,), (1,)), ((), ())), preferred_element_type=F32)


def _causal_pairs(n, sweeps=1, by_key=False):
    if by_key:
        rows = [(i, j, 0) for j in range(n) for i in range(j, n)]
    else:
        rows = [(i, j, s) for i in range(n) for s in range(sweeps) for j in range(i + 1)]
    return tuple(jnp.asarray(np.array([r[c] for r in rows], np.int32)) for c in range(3))


def _fox_placement(fh):
    hw, wa = fh * FOX_HDIM, fh * FOX_AUG
    pq, pk = np.zeros((hw, wa), np.float32), np.zeros((hw, wa), np.float32)
    aq, ak = np.zeros((3 * LANES, wa), np.float32), np.zeros((3 * LANES, wa), np.float32)
    oq, ok = np.zeros((1, wa), np.float32), np.zeros((1, wa), np.float32)
    for h in range(fh):
        src, dst = np.arange(h * FOX_HDIM, (h + 1) * FOX_HDIM), np.arange(h * FOX_AUG, h * FOX_AUG + FOX_HDIM)
        pq[src, dst] = FOX_HDIM ** -0.5
        pk[src, dst] = 1.0
        gate = h * FOX_AUG + FOX_HDIM
        for r in range(3):
            aq[r * LANES + h, gate + r] = 1.0
            ak[r * LANES + h, gate + 3 + r] = -1.0
        oq[0, gate + 3:gate + 6] = 1.0
        ok[0, gate:gate + 3] = 1.0
    bf = lambda m: jnp.asarray(m, dtype=BF16)
    return {"pq": bf(pq), "pk": bf(pk), "aq": bf(aq), "ak": bf(ak), "oq": jnp.asarray(oq), "ok": jnp.asarray(ok),
            "pqt": bf(pq.T), "pkt": bf(pk.T)}


def _fox_specs(tq, fh):
    def spec(tab):
        return pl.BlockSpec((None, tq, FOX_AUG), lambda b, t, *tabs: (b // fh, tabs[tab][t], b % fh))
    return spec(0), spec(1)


def fox_fwd(qa, ka, va, *, name):
    n_seq, S, width = qa.shape
    fh = width // FOX_AUG
    BH = n_seq * fh
    tq = min(FOX_TQ, S)
    itab, jtab, _ = _causal_pairs(S // tq)

    def body(itab_ref, jtab_ref, q_ref, k_ref, v_ref, o_ref, ox_ref, lse_ref, m_s, l_s, acc_s, acc_lo_s):
        t = pl.program_id(1)
        i, j = itab_ref[t], jtab_ref[t]

        @pl.when(j == 0)
        def _():
            m_s[...] = jnp.full_like(m_s, NEG_INF)
            l_s[...] = jnp.zeros_like(l_s)
            acc_s[...] = jnp.zeros_like(acc_s)
            acc_lo_s[...] = jnp.zeros_like(acc_lo_s)

        def step(on_diagonal):
            s = _qk(q_ref[...], k_ref[...])
            if on_diagonal:
                s = jnp.where(_diag_mask(tq), s, NEG_INF)
            m_prev = m_s[...]
            m_new = jnp.maximum(m_prev, jnp.max(s, axis=-1, keepdims=True))
            alpha = jnp.exp(m_prev - m_new)
            p = jnp.exp(s - m_new[:, 0:1])
            p_hi = p.astype(BF16)
            p_lo = (p - p_hi.astype(F32)).astype(BF16)
            v = v_ref[...]
            l_s[...] = alpha * l_s[...] + jnp.sum(p, axis=-1, keepdims=True)
            acc_s[...] = alpha * acc_s[...] + jnp.dot(p_hi, v, preferred_element_type=F32)
            acc_lo_s[...] = alpha * acc_lo_s[...] + jnp.dot(p_lo, v, preferred_element_type=F32)
            m_s[...] = m_new

        @pl.when(j < i)
        def _():
            step(False)

        @pl.when(j == i)
        def _():
            step(True)
            inv_l = 1.0 / l_s[...]
            o_ref[...] = (acc_s[...] * inv_l).astype(o_ref.dtype)
            ox_ref[...] = (acc_s[...] + acc_lo_s[...]) * inv_l
            lse_ref[...] = m_s[...] + jnp.log(l_s[...])

    qspec, kspec = _fox_specs(tq, fh)
    wide = jax.ShapeDtypeStruct((n_seq, S, width), F32)
    return pl.pallas_call(
        body, name=name,
        grid_spec=pltpu.PrefetchScalarGridSpec(
            num_scalar_prefetch=2, grid=(BH, itab.shape[0]),
            in_specs=[qspec, kspec, kspec],
            out_specs=[qspec, qspec, qspec],
            scratch_shapes=[pltpu.VMEM((tq, LANES), F32)] * 4),
        out_shape=[jax.ShapeDtypeStruct((n_seq, S, width), BF16), wide, wide],
        compiler_params=_cparams(("parallel", "arbitrary")),
    )(itab, jtab, qa, ka, va)


def _fox_ds(q, k, v, do, ox, lse, on_diagonal):
    s = _qk(q, k)
    if on_diagonal:
        s = jnp.where(_diag_mask(s.shape[0]), s, NEG_INF)
    p = jnp.exp(s - lse[:, 0:1])
    delta = jnp.sum(do.astype(F32) * ox, axis=-1, keepdims=True)
    return p, p * (_qk(do, v) - delta)


def fox_bwd_dq(qa, ka, va, do, ox, lse, *, name):
    n_seq, S, width = qa.shape
    fh = width // FOX_AUG
    BH = n_seq * fh
    tq = min(FOX_TQ, S)
    itab, jtab, _ = _causal_pairs(S // tq)

    def body(itab_ref, jtab_ref, q_ref, k_ref, v_ref, do_ref, ox_ref, lse_ref, dq_ref):
        t = pl.program_id(1)
        i, j = itab_ref[t], jtab_ref[t]

        def step(on_diagonal):
            _, ds = _fox_ds(q_ref[...], k_ref[...], v_ref[...], do_ref[...], ox_ref[...], lse_ref[...], on_diagonal)
            dq = jnp.dot(ds.astype(BF16), k_ref[...], preferred_element_type=F32)

            @pl.when(j == 0)
            def _():
                dq_ref[...] = dq

            @pl.when(j > 0)
            def _():
                dq_ref[...] += dq

        @pl.when(j < i)
        def _():
            step(False)

        @pl.when(j == i)
        def _():
            step(True)

    qspec, kspec = _fox_specs(tq, fh)
    return pl.pallas_call(
        body, name=name,
        grid_spec=pltpu.PrefetchScalarGridSpec(
            num_scalar_prefetch=2, grid=(BH, itab.shape[0]),
            in_specs=[qspec, kspec, kspec, qspec, qspec, qspec],
            out_specs=qspec),
        out_shape=jax.ShapeDtypeStruct((n_seq, S, width), F32),
        compiler_params=_cparams(("parallel", "arbitrary")),
    )(itab, jtab, qa, ka, va, do, ox, lse)


def fox_bwd_dkv(qa, ka, va, do, ox, lse, *, name):
    n_seq, S, width = qa.shape
    fh = width // FOX_AUG
    BH = n_seq * fh
    tq = min(FOX_TQ, S)
    itab, jtab, _ = _causal_pairs(S // tq, by_key=True)

    def body(itab_ref, jtab_ref, q_ref, k_ref, v_ref, do_ref, ox_ref, lse_ref, dk_ref, dv_ref, dsum_ref):
        t = pl.program_id(1)
        i, j = itab_ref[t], jtab_ref[t]

        def step(on_diagonal):
            q, do = q_ref[...], do_ref[...]
            p, ds = _fox_ds(q, k_ref[...], v_ref[...], do, ox_ref[...], lse_ref[...], on_diagonal)
            tn = (((0,), (0,)), ((), ()))
            dv = lax.dot_general(p.astype(BF16), do, tn, preferred_element_type=F32)
            dk = lax.dot_general(ds.astype(BF16), q, tn, preferred_element_type=F32)
            if on_diagonal:
                dv_ref[...], dk_ref[...], dsum_ref[...] = dv, dk, _colsum(ds)
            else:
                dv_ref[...] += dv
                dk_ref[...] += dk
                dsum_ref[...] += _colsum(ds)

        @pl.when(i == j)
        def _():
            step(True)

        @pl.when(i > j)
        def _():
            step(False)

    qspec, kspec = _fox_specs(tq, fh)
    return pl.pallas_call(
        body, name=name,
        grid_spec=pltpu.PrefetchScalarGridSpec(
            num_scalar_prefetch=2, grid=(BH, itab.shape[0]),
            in_specs=[qspec, kspec, kspec, qspec, qspec, qspec],
            out_specs=[kspec, kspec, pl.BlockSpec((None, 1, tq), lambda b, t, it, jt: (b, 0, jt[t]))]),
        out_shape=[jax.ShapeDtypeStruct((n_seq, S, width), F32), jax.ShapeDtypeStruct((n_seq, S, width), F32),
                   jax.ShapeDtypeStruct((BH, 1, S), F32)],
        compiler_params=_cparams(("parallel", "arbitrary")),
    )(itab, jtab, qa, ka, va, do, ox, lse)


def seq_cumsum(x, n_seq, seq, *, reverse, name):
    T, C = x.shape
    tb = min(256, seq)
    n = seq // tb
    r = np.arange(tb)
    tri = (r[None, :] >= r[:, None]) if reverse else (r[None, :] <= r[:, None])
    tri = jnp.asarray(tri.astype(np.float32), dtype=BF16)

    def body(x_ref, tri_ref, o_ref, carry_s):
        @pl.when(pl.program_id(1) == 0)
        def _():
            carry_s[...] = jnp.zeros_like(carry_s)

        xv = x_ref[...]
        o_ref[...] = _dot3(tri_ref[...], xv) + carry_s[...]
        carry_s[...] += _colsum(xv)

    blk = (lambda s, t: (s * n + (n - 1 - t), 0)) if reverse else (lambda s, t: (s * n + t, 0))
    return pl.pallas_call(
        body, name=name,
        grid=(n_seq, n),
        in_specs=[pl.BlockSpec((tb, C), blk), pl.BlockSpec((tb, tb), lambda s, t: (0, 0))],
        out_specs=pl.BlockSpec((tb, C), blk),
        out_shape=jax.ShapeDtypeStruct((T, C), F32),
        scratch_shapes=[pltpu.VMEM((1, C), F32)],
        compiler_params=_cparams(("arbitrary", "arbitrary")),
    )(x, tri)


def _place():
    return lax.axis_index("x"), lax.axis_index("y"), lax.axis_index("c")


def _other_chips(x, y):
    return [(1 - x, y), (x, 1 - y), (1 - x, 1 - y)]


def _hbm_call(body, ins, out_shape, n_sems, *, name):
    hbm = pl.BlockSpec(memory_space=pl.ANY)
    return pl.pallas_call(
        body, name=name,
        in_specs=[hbm] * len(ins), out_specs=[hbm] * len(out_shape), out_shape=out_shape,
        scratch_shapes=[pltpu.SemaphoreType.DMA((n_sems,)), pltpu.SemaphoreType.DMA((n_sems,)),
                        pltpu.SemaphoreType.DMA((len(ins),))],
        compiler_params=pltpu.CompilerParams(has_side_effects=True),
    )(*ins)


def allgather_chips(shards, *, name):
    nb = len(shards)
    assert all(s.shape[0] % (2 * ROW_ALIGN) == 0 for s in shards)

    def body(*refs):
        x_refs, o_refs = refs[:nb], refs[nb:2 * nb]
        send_sems, recv_sems, local_sems = refs[2 * nb:]
        x, y, c = _place()
        me = 2 * x + y
        chips = _other_chips(x, y)
        own, first, passed, landed, handed = [], [], [], [], []
        for b, (x_ref, o_ref) in enumerate(zip(x_refs, o_refs)):
            half = x_ref.shape[0] // 2
            mine, theirs = pl.ds(c * half, half), pl.ds((1 - c) * half, half)
            own.append(pltpu.make_async_copy(x_ref, o_ref.at[me], local_sems.at[b]))

            def copy(k, src, chip, rows, to, o_ref=o_ref, b=b):
                return pltpu.make_async_remote_copy(src_ref=src, dst_ref=o_ref.at[2 * chip[0] + chip[1], rows],
                                                    send_sem=send_sems.at[6 * b + k], recv_sem=recv_sems.at[6 * b + k],
                                                    device_id=to, device_id_type=MESH)
            for j, chip in enumerate(chips):
                first.append(copy(j, x_ref.at[mine], (x, y), mine, (*chip, c)))
                landed.append(copy(j, x_ref.at[mine], chip, mine, (*chip, c)))
                passed.append(copy(3 + j, o_ref.at[2 * chip[0] + chip[1], mine], chip, mine, (x, y, 1 - c)))
                handed.append(copy(3 + j, x_ref.at[mine], chip, theirs, (x, y, 1 - c)))
        for cp in own + first:
            cp.start()
        for arrived, forward in zip(landed, passed):
            arrived.wait_recv()
            forward.start()
        for cp in handed:
            cp.wait_recv()
        for cp in first + passed:
            cp.wait_send()
        for cp in own:
            cp.wait()

    return _hbm_call(body, shards, [jax.ShapeDtypeStruct((4,) + s.shape, s.dtype) for s in shards], 6 * nb, name=name)


def scatter_chips(parts, *, name):
    nb = len(parts)

    def body(*refs):
        x_refs, o_refs = refs[:nb], refs[nb:2 * nb]
        send_sems, recv_sems, _ = refs[2 * nb:]
        x, y, c = _place()
        sends = []
        for b, (x_ref, o_ref) in enumerate(zip(x_refs, o_refs)):
            for j, (px, py) in enumerate(_other_chips(x, y)):
                sends.append(pltpu.make_async_remote_copy(
                    src_ref=x_ref.at[2 * px + py], dst_ref=o_ref.at[j], send_sem=send_sems.at[3 * b + j],
                    recv_sem=recv_sems.at[3 * b + j], device_id=(px, py, c), device_id_type=MESH))
        for cp in sends:
            cp.start()
        for cp in sends:
            cp.wait_recv()
        for cp in sends:
            cp.wait_send()

    return _hbm_call(body, parts, [jax.ShapeDtypeStruct((3,) + p.shape[1:], p.dtype) for p in parts], 3 * nb, name=name)


def swap_cores(vs, *, name):
    nb = len(vs)

    def body(*refs):
        x_refs, o_refs = refs[:nb], refs[nb:2 * nb]
        send_sems, recv_sems, _ = refs[2 * nb:]
        x, y, c = _place()
        copies = [pltpu.make_async_remote_copy(src_ref=x_ref, dst_ref=o_ref, send_sem=send_sems.at[b],
                                               recv_sem=recv_sems.at[b], device_id=(x, y, 1 - c), device_id_type=MESH)
                  for b, (x_ref, o_ref) in enumerate(zip(x_refs, o_refs))]
        for cp in copies:
            cp.start()
        for cp in copies:
            cp.wait()

    return _hbm_call(body, vs, [jax.ShapeDtypeStruct(v.shape, v.dtype) for v in vs], nb, name=name)


def allreduce_small(v, *, name):
    R, C = v.shape

    def body(x_ref, o_ref, gath_ref, send_sems, recv_sems):
        x, y, c = _place()
        me = 4 * x + 2 * y + c
        gath_ref[me] = x_ref[...]
        flips = [(k >> 2 & 1, k >> 1 & 1, k & 1) for k in range(1, 8)]
        sends = []
        for j, (fx, fy, fc) in enumerate(flips):
            peer = (x ^ fx, y ^ fy, c ^ fc)
            cp = pltpu.make_async_remote_copy(src_ref=x_ref, dst_ref=gath_ref.at[me], send_sem=send_sems.at[j],
                                              recv_sem=recv_sems.at[j], device_id=peer, device_id_type=MESH)
            cp.start()
            sends.append(cp)
        for j, (fx, fy, fc) in enumerate(flips):
            peer = (x ^ fx, y ^ fy, c ^ fc)
            pltpu.make_async_remote_copy(src_ref=x_ref, dst_ref=gath_ref.at[4 * peer[0] + 2 * peer[1] + peer[2]],
                                         send_sem=send_sems.at[j], recv_sem=recv_sems.at[j], device_id=peer,
                                         device_id_type=MESH).wait_recv()
        for cp in sends:
            cp.wait_send()
        total = gath_ref[0]
        for d in range(1, 8):
            total = total + gath_ref[d]
        o_ref[...] = total

    vm = pl.BlockSpec(memory_space=pltpu.VMEM)
    out, _ = pl.pallas_call(
        body, name=name,
        in_specs=[vm], out_specs=[vm, vm],
        out_shape=[jax.ShapeDtypeStruct((R, C), F32), jax.ShapeDtypeStruct((8, R, C), F32)],
        scratch_shapes=[pltpu.SemaphoreType.DMA((7,)), pltpu.SemaphoreType.DMA((7,))],
        compiler_params=pltpu.CompilerParams(has_side_effects=True),
    )(v)
    return out


ROW_ALIGN = 16
PACK_W = 1024
SUM_TILE = 512
BIG_WEIGHTS = (("w_in", 1), ("w_a", 1), ("w_b", 1), ("w_o", 0), ("w_ff1", 1), ("w_ff2", 0), ("w_pg", 0), ("w_p", 1))


def _b_layout(d, ple):
    hw, q = d // 2, d // 4
    small = 2 * d + 2 * q
    lay = {"w_ff1": (0, 0, d, d), "w_ff2": (d, 0, d, d), "w_o": (2 * d, 0, q, d), "w_pg": (2 * d + q, 0, q, d),
           "w_a": (small, 0, hw, q), "w_b": (small, q, hw, q), "w_p": (small, 2 * q, ple, q)}
    return lay, small + hw


def pack_a(w_in_shard):
    rows, cols = w_in_shard.shape
    pad = -cols % LANES
    return jnp.concatenate([w_in_shard, jnp.zeros((rows, pad), w_in_shard.dtype)], axis=1)


def pack_b(shards, d):
    hw, q = d // 2, d // 4
    dt = shards["w_a"].dtype
    wp = shards["w_p"]
    wp = jnp.concatenate([wp, jnp.zeros((hw - wp.shape[0], q), dt)], axis=0)
    small = jnp.concatenate([shards["w_a"], shards["w_b"], wp, jnp.zeros((hw, d - 3 * q), dt)], axis=1)
    return jnp.concatenate([shards["w_ff1"], shards["w_ff2"], shards["w_o"], shards["w_pg"], small], axis=0)


def unpack_b(buf, lay):
    return {nm: buf[r0:r0 + rows, c0:c0 + cols] for nm, (r0, c0, rows, cols) in lay.items()}


def _win_layout(d):
    hw = d // 2
    fh = hw // FOX_HDIM
    orig = {"hq": (0, hw), "hf": (hw, hw), "hi": (2 * hw, hw), "hg": (3 * hw, hw), "fq": (4 * hw, hw),
            "fk": (5 * hw, hw), "fv": (6 * hw, hw), "ff": (7 * hw, fh), "ga": (7 * hw + fh, d), "gb": (7 * hw + fh + d, d)}
    order = ["ga", "gb", "hq", "hf", "hi", "hg", "fq", "fk", "fv", "ff"]
    mine, off = {}, 0
    for nm in order:
        width = orig[nm][1] if nm != "ff" else LANES
        mine[nm] = (off, width)
        off += width
    return orig, order, mine, off


def _adam_fn(rows, vecs):
    w, g, m, v = rows
    m2 = ADAM_B1 * m + (1.0 - ADAM_B1) * g
    v2 = ADAM_B2 * v + (1.0 - ADAM_B2) * (g * g)
    m_hat = m2 / (1.0 - ADAM_B1 ** ADAM_STEP)
    v_hat = v2 / (1.0 - ADAM_B2 ** ADAM_STEP)
    delta = -ADAM_LR * (m_hat / (jnp.sqrt(v_hat) + ADAM_EPS) + ADAM_WD * w)
    return [delta, m2, v2], []


def adamw(w, g, m, v, *, name):
    c = w.shape[1]
    (delta, m2, v2), _ = rowwise(_adam_fn, [w, g, m, v], [], [(c, F32)] * 3, name=name, tm=256)
    return delta, m2, v2


def kernel(x, p, ln0_g, ln0_b, w_in, hg_lb, hg_norm_g, fox_fb, w_a, w_b, w_o, ln1_g, ln1_b, w_ff1, w_ff2, w_pg, w_p, ln2_g, ln2_b, loss_target, m_ln0_g, m_ln0_b, m_w_in, m_hg_lb, m_hg_norm_g, m_fox_fb, m_w_a, m_w_b, m_w_o, m_ln1_g, m_ln1_b, m_w_ff1, m_w_ff2, m_w_pg, m_w_p, m_ln2_g, m_ln2_b, v_ln0_g, v_ln0_b, v_w_in, v_hg_lb, v_hg_norm_g, v_fox_fb, v_w_a, v_w_b, v_w_o, v_ln1_g, v_ln1_b, v_w_ff1, v_w_ff2, v_w_pg, v_w_p, v_ln2_g, v_ln2_b):
    n_seq, seq, d = x.shape
    T = n_seq * seq
    hw = d // 2
    fh = hw // FOX_HDIM
    bh = n_seq * fh
    orig, order, mine, n_in = _win_layout(d)

    big = {"w_in": w_in[0], "w_a": w_a[0], "w_b": w_b[0], "w_o": w_o[0], "w_ff1": w_ff1[0], "w_ff2": w_ff2[0],
           "w_pg": w_pg[0], "w_p": w_p[0]}
    big_m = {"w_in": m_w_in[0], "w_a": m_w_a[0], "w_b": m_w_b[0], "w_o": m_w_o[0], "w_ff1": m_w_ff1[0],
             "w_ff2": m_w_ff2[0], "w_pg": m_w_pg[0], "w_p": m_w_p[0]}
    big_v = {"w_in": v_w_in[0], "w_a": v_w_a[0], "w_b": v_w_b[0], "w_o": v_w_o[0], "w_ff1": v_w_ff1[0],
             "w_ff2": v_w_ff2[0], "w_pg": v_w_pg[0], "w_p": v_w_p[0]}
    names = [nm for nm, _ in BIG_WEIGHTS]
    axis = dict(BIG_WEIGHTS)
    ple = w_p.shape[1]
    lay, b_rows = _b_layout(d, ple)
    in_cols = big["w_in"].shape[1]

    a_all, b_all = allgather_chips(
        [pack_a(big["w_in"].astype(BF16)), pack_b({nm: big[nm].astype(BF16) for nm in names if nm != "w_in"}, d)],
        name="allgather_weights")
    win = jnp.concatenate([a_all[s, :, :in_cols] for s in range(4)], axis=1)
    win_mine = jnp.concatenate(
        [win[:, orig[nm][0]:orig[nm][0] + orig[nm][1]] for nm in order]
        + [jnp.zeros((d, LANES - fh), BF16)], axis=1)
    view = lambda nm, k, n: WView(b_all, lay[nm][0], lay[nm][1], k, n, axis[nm])
    w_ff1_v, w_ff2_v = view("w_ff1", d, 4 * d), view("w_ff2", 4 * d, d)
    def whole(nm):
        r0, c0, rows, cols = lay[nm]
        return jnp.concatenate([b_all[s, r0:r0 + rows, c0:c0 + cols] for s in range(4)], axis=axis[nm])
    w_o_v, w_pg_v, w_a_v, w_p_v, w_b_full = whole("w_o"), whole("w_pg"), whole("w_a"), whole("w_p"), whole("w_b")

    x2 = x.reshape(T, d)
    tgt = loss_target.reshape(T, d)
    p_b = p.reshape(T, p.shape[-1]).astype(BF16)
    vec = lambda a: a.reshape(1, -1)
    probs = jax.nn.softmax(hg_lb, axis=0)
    lb = vec(probs[0])

    def ln0_fn(rows, vecs):
        h = _ln_stats(rows[0]) * vecs[0] + vecs[1]
        return [h, h], []
    (h0, h0b), _ = rowwise(ln0_fn, [x2], [vec(ln0_g), vec(ln0_b)], [(d, F32), (d, BF16)], name="ln0_fwd")
    proj = matmul_nn(h0b, win_mine, name="in_proj")

    o_raw, hg_states = hgrn2_fwd(proj, [mine["hq"][0], mine["hf"][0], mine["hi"][0]], lb, n_seq, seq, name="hgrn2_fwd")

    def ya_fn(rows, vecs):
        o, hg = rows
        outs = []
        for h in range(HG_HEADS):
            oh = o[:, h * HG_DIM:(h + 1) * HG_DIM]
            outs.append(oh * lax.rsqrt(jnp.mean(oh * oh, axis=-1, keepdims=True) + RMS_EPS))
        y = jnp.concatenate(outs, axis=1) * vecs[0] * (hg * _sigmoid(hg))
        return [y], []
    (y_a,), _ = rowwise(ya_fn, [o_raw, (proj,) + mine["hg"]], [hg_norm_g], [(hw, BF16)], name="hgrn2_out_fwd")

    fb_pad = jnp.concatenate([fox_fb, jnp.zeros((1, LANES - fh), F32)], axis=1)

    def lf_fn(rows, vecs):
        u = rows[0] + vecs[0]
        return [jnp.minimum(u, 0.0) - jnp.log(1.0 + jnp.exp(-jnp.abs(u)))], []
    (lf,), _ = rowwise(lf_fn, [(proj,) + mine["ff"]], [fb_pad], [(LANES, F32)], name="fox_logf")
    c_cum = seq_cumsum(lf, n_seq, seq, reverse=False, name="fox_cumsum")

    place = _fox_placement(fh)

    def prep_fn(rows, vecs):
        fq_, fk_, fv_, cc = rows
        pq, pk, aq, ak, oq, ok = vecs
        parts = jnp.concatenate(_split3(cc), axis=1)
        mm = lambda a_, b_: jnp.dot(a_, b_, preferred_element_type=F32)
        q_ = mm(fq_.astype(BF16), pq) + mm(parts, aq) + oq
        k_ = mm(fk_.astype(BF16), pk) + mm(parts, ak) + ok
        return [q_, k_, mm(fv_.astype(BF16), pk)], []
    wa = fh * FOX_AUG
    (qa, ka, va), _ = rowwise(prep_fn, [(proj,) + mine["fq"], (proj,) + mine["fk"], (proj,) + mine["fv"], c_cum],
                              [place[nm] for nm in ("pq", "pk", "aq", "ak", "oq", "ok")], [(wa, BF16)] * 3,
                              name="fox_prep")
    as_seq = lambda t2d: t2d.reshape(n_seq, seq, t2d.shape[1])
    o_fox, ox_fox, lse = fox_fwd(as_seq(qa), as_seq(ka), as_seq(va), name="fox_fwd")
    y_b = o_fox.reshape(T, wa)
    wb_pad = jnp.concatenate([w_b_full.reshape(fh, FOX_HDIM, d), jnp.zeros((fh, FOX_AUG - FOX_HDIM, d), BF16)],
                             axis=1).reshape(wa, d)

    pa = matmul_nn(y_a, w_a_v, name="proj_a")
    pb = matmul_nn(y_b, wb_pad, name="proj_b")

    def merge_fn(rows, vecs):
        ga, gb, a, b = rows
        return [_sigmoid(ga) * a + _sigmoid(gb) * b], []
    (merged,), _ = rowwise(merge_fn, [(proj,) + mine["ga"], (proj,) + mine["gb"], pa, pb], [], [(d, BF16)],
                           name="merge_fwd")
    mix = matmul_nn(merged, w_o_v, name="out_proj")

    def ln1_fn(rows, vecs):
        z = ALPHA * rows[0] + rows[1]
        h = _ln_stats(z) * vecs[0] + vecs[1]
        return [z, h, h], []
    (z1, h1, h1b), _ = rowwise(ln1_fn, [h0, mix], [ln1_g, ln1_b], [(d, F32), (d, F32), (d, BF16)], name="ln1_fwd")

    relu2 = lambda u: jnp.square(jnp.maximum(u, 0.0))
    act = matmul_nn(h1b, w_ff1_v, name="ff1", out_dtype=BF16, epilogue=relu2)
    ff = matmul_nn(act, w_ff2_v, name="ff2")
    pg = matmul_nn(h1b, w_pg_v, name="ple_gate")
    pe = matmul_nn(p_b, w_p_v, name="ple_embed")

    def head_fn(rows, vecs):
        h1v, ffv, pgv, pev, t = rows
        g2, b2 = vecs
        sp = _sigmoid(pgv)
        z = ALPHA * h1v + ffv + sp * pev
        y = _ln_stats(z) * g2 + b2
        err = y - t
        loss_rows = 0.5 * jnp.mean(err * err, axis=-1, keepdims=True)
        dy = err * (1.0 / d)
        dz, dg2, db2 = _ln_bwd(z, dy, g2)
        loss_acc = jnp.broadcast_to(_colsum(loss_rows), (1, LANES))
        return [dz, dz, dz * pev * (sp * (1.0 - sp)), dz * sp], [dg2, db2, loss_acc]
    (dz2, dz2b, dpg, dpe), (g_ln2_g, g_ln2_b, loss_part) = rowwise(
        head_fn, [h1, ff, pg, pe, tgt], [ln2_g, ln2_b],
        [(d, F32), (d, BF16), (d, BF16), (d, BF16)], [d, d, LANES], name="head_fwd_bwd")

    dact = lambda da, a: da * (2.0 * jnp.sqrt(a.astype(F32)))
    du = matmul_nn(dz2b, w_ff2_v, transpose_rhs=True, name="d_ff2", out_dtype=BF16, epilogue=dact, aux=act)
    dh1_ff = matmul_nn(du, w_ff1_v, transpose_rhs=True, name="d_ff1")
    dh1_pg = matmul_nn(dpg, w_pg_v, transpose_rhs=True, name="d_ple_gate")

    def ln1_bwd_fn(rows, vecs):
        dh1 = ALPHA * rows[0] + rows[1] + rows[2]
        dz, dg, db = _ln_bwd(rows[3], dh1, vecs[0])
        return [dz, dz], [dg, db]
    (dz1, dz1b), (g_ln1_g, g_ln1_b) = rowwise(ln1_bwd_fn, [dz2, dh1_ff, dh1_pg, z1], [ln1_g],
                                              [(d, F32), (d, BF16)], [d, d], name="ln1_bwd")
    dmerged = matmul_nn(dz1b, w_o_v, transpose_rhs=True, name="d_out_proj")

    def merge_bwd_fn(rows, vecs):
        dm, ga, gb, a, b = rows
        sa, sb = _sigmoid(ga), _sigmoid(gb)
        return [dm * a * (sa * (1.0 - sa)), dm * b * (sb * (1.0 - sb)), dm * sa, dm * sb], []
    (dga, dgb, dma, dmb), _ = rowwise(merge_bwd_fn, [dmerged, (proj,) + mine["ga"], (proj,) + mine["gb"], pa, pb], [],
                                      [(d, BF16)] * 4, name="merge_bwd")
    dya = matmul_nn(dma, w_a_v, transpose_rhs=True, name="d_proj_a")
    dyb = matmul_nn(dmb, wb_pad, transpose_rhs=True, name="d_proj_b", out_dtype=BF16)

    def ya_bwd_fn(rows, vecs):
        o, hg, dy = rows
        ng = vecs[0]
        sg = _sigmoid(hg)
        gate = hg * sg
        dn_parts, do_parts, n_parts = [], [], []
        for h in range(HG_HEADS):
            hs = slice(h * HG_DIM, (h + 1) * HG_DIM)
            oh = o[:, hs]
            r = lax.rsqrt(jnp.mean(oh * oh, axis=-1, keepdims=True) + RMS_EPS)
            nh = oh * r
            dn = dy[:, hs] * ng[:, hs] * gate[:, hs]
            do_parts.append(r * (dn - nh * jnp.mean(dn * nh, axis=-1, keepdims=True)))
            n_parts.append(nh)
        nrm = jnp.concatenate(n_parts, axis=1)
        dhg = dy * nrm * ng * (sg * (1.0 + hg * (1.0 - sg)))
        return [jnp.concatenate(do_parts, axis=1), dhg], [_colsum(dy * nrm * gate)]
    (do_raw, dhg), (g_norm_g,) = rowwise(ya_bwd_fn, [o_raw, (proj,) + mine["hg"], dya], [hg_norm_g],
                                         [(hw, F32), (hw, BF16)], [hw], name="hgrn2_out_bwd")
    dhq, dhf, dhi, g_lb = hgrn2_bwd(proj, [mine["hq"][0], mine["hf"][0], mine["hi"][0]], lb, do_raw, hg_states,
                                    n_seq, seq, name="hgrn2_bwd")

    do_fox = as_seq(dyb)
    dqa = fox_bwd_dq(as_seq(qa), as_seq(ka), as_seq(va), do_fox, ox_fox, lse, name="fox_bwd_dq")
    dka, dva, dsum = fox_bwd_dkv(as_seq(qa), as_seq(ka), as_seq(va), do_fox, ox_fox, lse, name="fox_bwd_dkv")

    def unprep_fn(rows, vecs):
        mm = lambda a_, b_: jnp.dot(a_.astype(BF16), b_, preferred_element_type=F32)
        return [mm(rows[0], vecs[0]), mm(rows[1], vecs[1]), mm(rows[2], vecs[1])], []
    (dfq, dfk, dfv), _ = rowwise(unprep_fn, [dqa.reshape(T, wa), dka.reshape(T, wa), dva.reshape(T, wa)],
                                 [place["pqt"], place["pkt"]], [(hw, BF16)] * 3, name="fox_unprep")
    dc = -dsum.reshape(n_seq, fh, seq).transpose(0, 2, 1).reshape(T, fh)
    dc = jnp.concatenate([dc, jnp.zeros((T, LANES - fh), F32)], axis=1)
    dlf = seq_cumsum(dc, n_seq, seq, reverse=True, name="fox_cumsum_bwd")

    def lf_bwd_fn(rows, vecs):
        u = rows[0] + vecs[0]
        du_ = rows[1] * _sigmoid(-u)
        return [du_], [_colsum(du_)]
    (dff_,), (g_fb,) = rowwise(lf_bwd_fn, [(proj,) + mine["ff"], dlf], [fb_pad], [(LANES, BF16)], [LANES],
                               name="fox_logf_bwd")

    dproj = jnp.concatenate([dga, dgb, dhq, dhf, dhi, dhg, dfq, dfk, dfv, dff_], axis=1)
    dh0_in = matmul_nn(dproj, win_mine, transpose_rhs=True, name="d_in_proj")

    def ln0_bwd_fn(rows, vecs):
        dh0 = rows[0] + ALPHA * rows[1]
        dx, dg, db = _ln_bwd(rows[2], dh0, vecs[0])
        return [dx], [dg, db]
    (dx,), (g_ln0_g, g_ln0_b) = rowwise(ln0_bwd_fn, [dh0_in, dz1, x2], [vec(ln0_g)], [(d, F32)], [d, d],
                                        name="ln0_bwd")

    gw_in_mine = matmul_tn(h0b, dproj, name="g_w_in")
    gw_in = jnp.concatenate([gw_in_mine[:, mine[nm][0]:mine[nm][0] + orig[nm][1]]
                             for nm in ["hq", "hf", "hi", "hg", "fq", "fk", "fv", "ff", "ga", "gb"]], axis=1)
    gfull = {
        "w_in": gw_in,
        "w_a": matmul_tn(y_a, dma, name="g_w_a"),
        "w_b": matmul_tn(y_b, dmb, name="g_w_b").reshape(fh, FOX_AUG, d)[:, :FOX_HDIM].reshape(hw, d),
        "w_o": matmul_tn(merged, dz1b, name="g_w_o"),
        "w_ff1": matmul_tn(h1b, du, name="g_w_ff1"),
        "w_ff2": matmul_tn(act, dz2b, name="g_w_ff2"),
        "w_pg": matmul_tn(h1b, dpg, name="g_w_pg"),
        "w_p": matmul_tn(p_b, dpe, name="g_w_p"),
    }

    def chip_parts(nm, s):
        g = gfull[nm]
        n = g.shape[axis[nm]] // 4
        return lax.slice_in_dim(g, s * n, (s + 1) * n, axis=axis[nm])
    grads = [jnp.stack([pack_a(chip_parts("w_in", s)) for s in range(4)]),
             jnp.stack([pack_b({nm: chip_parts(nm, s) for nm in names if nm != "w_in"}, d) for s in range(4)])]
    me = 2 * lax.axis_index("x") + lax.axis_index("y")
    core = lax.axis_index("c")
    halves = [g.shape[1] // 2 for g in grads]
    keep = [lax.dynamic_slice_in_dim(g, core * h, h, axis=1) for g, h in zip(grads, halves)]
    give = [lax.dynamic_slice_in_dim(g, (1 - core) * h, h, axis=1).astype(BF16) for g, h in zip(grads, halves)]
    from_core = swap_cores(give, name="swap_partials")

    def sum2_fn(rows, vecs):
        s = rows[0] + rows[1].astype(F32)
        return [s, s], []

    def sum4_fn(rows, vecs):
        a, r0, r1, r2 = rows
        return [((a + r0.astype(F32)) + r1.astype(F32)) + r2.astype(F32)], []
    pair, pair_b = [], []
    for b, (kp, fc, h) in enumerate(zip(keep, from_core, halves)):
        cols = kp.shape[2]
        (s32, s16), _ = rowwise(sum2_fn, [kp.reshape(4 * h, cols), fc.reshape(4 * h, cols)], [],
                                [(cols, F32), (cols, BF16)], name=f"sum_cores_{b}", tm=SUM_TILE)
        pair.append(s32.reshape(4, h, cols))
        pair_b.append(s16.reshape(4, h, cols))
    got = scatter_chips(pair_b, name="scatter_grads")
    q_half = []
    for b, (pr, gt) in enumerate(zip(pair, got)):
        own = lax.dynamic_index_in_dim(pr, me, axis=0, keepdims=False)
        (q,), _ = rowwise(sum4_fn, [own, gt[0], gt[1], gt[2]], [], [(own.shape[1], F32)], name=f"sum_chips_{b}",
                          tm=SUM_TILE)
        q_half.append(q)
    q_other = swap_cores(q_half, name="swap_halves")
    g_a, g_b = [jnp.concatenate([jnp.where(core == 0, mine_, other), jnp.where(core == 0, other, mine_)], axis=0)
                for mine_, other in zip(q_half, q_other)]
    g_shards = unpack_b(g_b, lay)
    g_shards["w_in"] = g_a[:, :in_cols]

    def row1024(*parts):
        r = jnp.concatenate([q.reshape(1, -1) for q in parts], axis=1)
        return jnp.concatenate([r, jnp.zeros((1, PACK_W - r.shape[1]), F32)], axis=1) if r.shape[1] < PACK_W else r
    small_rows = [row1024(g_ln0_g), row1024(g_ln0_b), row1024(g_ln1_g), row1024(g_ln1_b), row1024(g_ln2_g),
                  row1024(g_ln2_b), row1024(g_norm_g, g_lb), row1024(g_fb[:, :fh], loss_part[:, :1])]
    small = allreduce_small(jnp.concatenate(small_rows, axis=0), name="allreduce_small")
    s_ln0_g, s_ln0_b, s_ln1_g, s_ln1_b, s_ln2_g, s_ln2_b = [small[r:r + 1] for r in range(6)]
    s_norm_g, s_lb = small[6:7, :hw], small[6:7, hw:2 * hw]
    s_fb, loss = small[7:8, :fh], small[7, fh]
    p0 = probs[0:1]
    jac = p0 * (1.0 - p0)
    s_hg_lb = jnp.concatenate([s_lb * jac, -s_lb * jac], axis=0)

    small_w = [vec(ln0_g), vec(ln0_b), ln1_g, ln1_b, ln2_g, ln2_b, hg_lb.reshape(1, -1), hg_norm_g, fox_fb]
    small_g = [s_ln0_g, s_ln0_b, s_ln1_g, s_ln1_b, s_ln2_g, s_ln2_b, s_hg_lb.reshape(1, -1), s_norm_g, s_fb]
    small_m = [vec(m_ln0_g), vec(m_ln0_b), m_ln1_g, m_ln1_b, m_ln2_g, m_ln2_b, m_hg_lb.reshape(1, -1), m_hg_norm_g, m_fox_fb]
    small_v = [vec(v_ln0_g), vec(v_ln0_b), v_ln1_g, v_ln1_b, v_ln2_g, v_ln2_b, v_hg_lb.reshape(1, -1), v_hg_norm_g, v_fox_fb]
    pad_rows = lambda lst, fill: jnp.concatenate(
        [row1024(a) if fill == 0.0 else jnp.concatenate([a.reshape(1, -1), jnp.full((1, PACK_W - a.size), fill, F32)], axis=1)
         for a in lst] + [jnp.full((16 - len(lst), PACK_W), fill, F32)], axis=0)
    sd, sm, sv = adamw(pad_rows(small_w, 0.0), pad_rows(small_g, 0.0), pad_rows(small_m, 0.0), pad_rows(small_v, 1.0),
                       name="adamw_small")
    small_shapes = [ln0_g.shape, ln0_b.shape, ln1_g.shape, ln1_b.shape, ln2_g.shape, ln2_b.shape, hg_lb.shape,
                    hg_norm_g.shape, fox_fb.shape]
    take = lambda buf: [buf[r, :int(np.prod(shp))].reshape(shp) for r, shp in enumerate(small_shapes)]
    sg_out, sd_out, sm_out, sv_out = [g.reshape(shp) for g, shp in zip(small_g, small_shapes)], take(sd), take(sm), take(sv)

    big_out = {}
    for nm in names:
        delta, m2, v2 = adamw(big[nm], g_shards[nm], big_m[nm], big_v[nm], name="adamw_" + nm)
        big_out[nm] = (g_shards[nm][None], delta[None], m2[None], v2[None])

    def ordered(k):
        sm_ = [sg_out, sd_out, sm_out, sv_out][k]
        bg = lambda nm: big_out[nm][k]
        return [sm_[0], sm_[1], bg("w_in"), sm_[6], sm_[7], sm_[8], bg("w_a"), bg("w_b"), bg("w_o"), sm_[2], sm_[3],
                bg("w_ff1"), bg("w_ff2"), bg("w_pg"), bg("w_p"), sm_[4], sm_[5]]
    grad_x = dx.reshape(n_seq, seq, d)
    return (loss, grad_x, *ordered(0), *ordered(1), *ordered(2), *ordered(3))
```

```python
import functools
from typing import NamedTuple, Optional

import numpy as np
import jax
import jax.numpy as jnp
from jax import lax
from jax.experimental import pallas as pl
from jax.experimental.pallas import tpu as pltpu

F32 = jnp.float32
BF16 = jnp.bfloat16
MESH = pl.DeviceIdType.MESH

VMEM_LIMIT_BYTES = 48 * 1024 * 1024
LANES = 128
HG_HEADS = 4
HG_DIM = 128
HG_BLK = 16
HG_TILE = 256
FOX_HDIM = 64
FOX_AUG = 128
FOX_TQ = 1024
LN_EPS = 1e-5
RMS_EPS = 1e-6
DEPTH = 1
ALPHA = (2.0 * DEPTH) ** 0.25
ADAM_LR, ADAM_B1, ADAM_B2, ADAM_EPS, ADAM_WD, ADAM_STEP = 0.001, 0.9, 0.999, 1e-08, 0.01, 10
NEG_INF = -1e30


def _cparams(sem):
    return pltpu.CompilerParams(dimension_semantics=sem, vmem_limit_bytes=VMEM_LIMIT_BYTES)


def _tile(n, cap):
    if n <= cap:
        return n
    best = None
    for t in range(LANES, cap + 1, LANES):
        if n % t == 0:
            best = t
    assert best is not None, (n, cap)
    return best


class WView(NamedTuple):
    arr: jax.Array
    r0: int
    c0: int
    k: int
    n: int
    split: Optional[int]


def matmul_nn(a, w, *, name, transpose_rhs=False, out_dtype=F32, epilogue=None, aux=None, tm=1024):
    wv = w if isinstance(w, WView) else WView(w[None], 0, 0, w.shape[0], w.shape[1], None)
    rows_s = wv.k // 4 if wv.split == 0 else wv.k
    cols_s = wv.n // 4 if wv.split == 1 else wv.n
    tr, tc = _tile(rows_s, 1152), _tile(cols_s, 1152)
    assert wv.r0 % tr == 0 and wv.c0 % tc == 0
    T, K = a.shape
    N, tn, tk = (wv.k, tr, tc) if transpose_rhs else (wv.n, tc, tr)
    assert K == (wv.n if transpose_rhs else wv.k)
    tm = min(tm, T)
    assert T % tm == 0
    nk = K // tk

    def w_block(ri, ci):
        if wv.split == 0:
            return (ri * tr) // rows_s, (wv.r0 + (ri * tr) % rows_s) // tr, wv.c0 // tc + ci
        if wv.split == 1:
            return (ci * tc) // cols_s, wv.r0 // tr + ri, (wv.c0 + (ci * tc) % cols_s) // tc
        return 0, wv.r0 // tr + ri, wv.c0 // tc + ci

    def body(*refs):
        if aux is None:
            a_ref, w_ref, o_ref, acc_ref = refs
            x_ref = None
        else:
            a_ref, w_ref, x_ref, o_ref, acc_ref = refs
        k = pl.program_id(2)
        if transpose_rhs:
            part = lax.dot_general(a_ref[...], w_ref[...], (((1,), (1,)), ((), ())), preferred_element_type=F32)
        else:
            part = jnp.dot(a_ref[...], w_ref[...], preferred_element_type=F32)

        def write(res):
            if epilogue is not None:
                res = epilogue(res) if x_ref is None else epilogue(res, x_ref[...])
            o_ref[...] = res.astype(out_dtype)

        if nk == 1:
            write(part)
        else:
            @pl.when(k == 0)
            def _():
                acc_ref[...] = part

            @pl.when(k > 0)
            def _():
                acc_ref[...] += part

            @pl.when(k == nk - 1)
            def _():
                write(acc_ref[...])

    w_index = (lambda n, m, k: w_block(n, k)) if transpose_rhs else (lambda n, m, k: w_block(k, n))
    in_specs = [pl.BlockSpec((tm, tk), lambda n, m, k: (m, k)),
                pl.BlockSpec((None, tr, tc), w_index)]
    args = [a, wv.arr]
    if aux is not None:
        in_specs.append(pl.BlockSpec((tm, tn), lambda n, m, k: (m, n)))
        args.append(aux)
    return pl.pallas_call(
        body, name=name,
        grid=(N // tn, T // tm, nk),
        in_specs=in_specs,
        out_specs=pl.BlockSpec((tm, tn), lambda n, m, k: (m, n)),
        out_shape=jax.ShapeDtypeStruct((T, N), out_dtype),
        scratch_shapes=[pltpu.VMEM((tm, tn) if nk > 1 else (8, LANES), F32)],
        compiler_params=_cparams(("parallel", "parallel", "arbitrary")),
    )(*args)


def matmul_tn(a, b, *, name, tk=1024, rider=None):
    T, M = a.shape
    T2, N = b.shape
    tk = min(tk, T)
    assert T == T2 and T % tk == 0
    tm = _tile(M, 1024)
    tn = _tile(N, 1152)

    def body(a_ref, b_ref, o_ref):
        k = pl.program_id(2)
        part = lax.dot_general(a_ref[...], b_ref[...], (((0,), (0,)), ((), ())), preferred_element_type=F32)

        @pl.when(k == 0)
        def _():
            o_ref[...] = part

        @pl.when(k > 0)
        def _():
            o_ref[...] += part

    in_specs = [pl.BlockSpec((tk, tm), lambda m, n, k: (k, m)), pl.BlockSpec((tk, tn), lambda m, n, k: (k, n))]
    out_specs = [pl.BlockSpec((tm, tn), lambda m, n, k: (m, n))]
    out_shape = [jax.ShapeDtypeStruct((M, N), F32)]
    grid = (M // tm, N // tn, T // tk)
    if rider is None:
        return pl.pallas_call(body, name=name, grid=grid, in_specs=in_specs, out_specs=out_specs, out_shape=out_shape,
                              compiler_params=_cparams(("parallel", "parallel", "arbitrary")))(a, b)[0]
    r_in, r_out, r_sems = rider.specs()
    res = pl.pallas_call(
        rider.wrap(body, 2, 1, 3), name=name, grid=grid, in_specs=in_specs + r_in, out_specs=out_specs + r_out,
        out_shape=out_shape + rider.out_shape, scratch_shapes=r_sems,
        compiler_params=pltpu.CompilerParams(dimension_semantics=("arbitrary",) * 3,
                                             vmem_limit_bytes=VMEM_LIMIT_BYTES, has_side_effects=True),
    )(a, b, *rider.ins)
    return res[0], list(res[1:])


def rowwise(fn, rows, vecs, outs, accs=(), *, name, tm=512):
    rows = [r if isinstance(r, tuple) else (r, 0, r.shape[1]) for r in rows]
    T = rows[0][0].shape[0]
    tm = min(tm, T)
    assert T % tm == 0
    n_rows, n_vecs, n_outs, n_accs = len(rows), len(vecs), len(outs), len(accs)

    def body(*refs):
        row_refs = refs[:n_rows]
        vec_refs = refs[n_rows:n_rows + n_vecs]
        out_refs = refs[n_rows + n_vecs:n_rows + n_vecs + n_outs]
        acc_refs = refs[n_rows + n_vecs + n_outs:]
        out_vals, acc_vals = fn([r[...] for r in row_refs], [v[...] for v in vec_refs])
        assert len(out_vals) == n_outs and len(acc_vals) == n_accs
        for r, val in zip(out_refs, out_vals):
            r[...] = val.astype(r.dtype)
        if n_accs:
            i = pl.program_id(0)

            @pl.when(i == 0)
            def _():
                for r in acc_refs:
                    r[...] = jnp.zeros_like(r)

            for r, val in zip(acc_refs, acc_vals):
                r[...] += val

    in_specs = []
    for arr, off, width in rows:
        assert off % width == 0
        in_specs.append(pl.BlockSpec((tm, width), functools.partial(lambda i, blk: (i, blk), blk=off // width)))
    for v in vecs:
        in_specs.append(pl.BlockSpec(v.shape, lambda i: (0, 0)))
    out_specs = [pl.BlockSpec((tm, w), lambda i: (i, 0)) for w, _ in outs]
    out_specs += [pl.BlockSpec((1, w), lambda i: (0, 0)) for w in accs]
    out_shape = [jax.ShapeDtypeStruct((T, w), dt) for w, dt in outs]
    out_shape += [jax.ShapeDtypeStruct((1, w), F32) for w in accs]
    res = pl.pallas_call(
        body, name=name,
        grid=(T // tm,),
        in_specs=in_specs, out_specs=out_specs, out_shape=out_shape,
        compiler_params=_cparams(("arbitrary",) if n_accs else ("parallel",)),
    )(*[r[0] for r in rows], *vecs)
    return res[:n_outs], res[n_outs:]


def _colsum(x):
    return jnp.sum(x, axis=0, keepdims=True)


def _sigmoid(x):
    return 1.0 / (1.0 + jnp.exp(-x))


def _ln_stats(z):
    mu = jnp.mean(z, axis=-1, keepdims=True)
    zc = z - mu
    var = jnp.mean(zc * zc, axis=-1, keepdims=True)
    return zc * lax.rsqrt(var + LN_EPS)


def _ln_bwd(zhat_src, dy, g):
    mu = jnp.mean(zhat_src, axis=-1, keepdims=True)
    zc = zhat_src - mu
    var = jnp.mean(zc * zc, axis=-1, keepdims=True)
    rstd = lax.rsqrt(var + LN_EPS)
    zh = zc * rstd
    dzh = dy * g
    dz = rstd * (dzh - jnp.mean(dzh, axis=-1, keepdims=True) - zh * jnp.mean(dzh * zh, axis=-1, keepdims=True))
    return dz, _colsum(dy * zh), _colsum(dy)


def _hg_constants():
    r = np.arange(HG_TILE)
    same = (r[:, None] // HG_BLK) == (r[None, :] // HG_BLK)
    lower = (same & (r[None, :] <= r[:, None])).astype(np.float32)
    upper = (same & (r[None, :] >= r[:, None])).astype(np.float32)
    total = same.astype(np.float32)
    w = HG_HEADS * HG_DIM
    c = np.arange(w)
    bd = ((c[:, None] // HG_DIM) == (c[None, :] // HG_DIM)).astype(np.float32)
    n = HG_BLK * HG_BLK
    rr = np.arange(n)
    sel_t = (rr[None, :] // HG_BLK == np.arange(HG_BLK)[:, None]).astype(np.float32)
    sel_s = (rr[None, :] % HG_BLK == np.arange(HG_BLK)[:, None]).astype(np.float32)
    as_bf = lambda m: jnp.asarray(m, dtype=BF16)
    return as_bf(lower), as_bf(upper), as_bf(total), as_bf(bd), as_bf(sel_t), as_bf(sel_s)


def _keep_bf16_bits(x):
    bits = lax.bitcast_convert_type(x, jnp.int32) & jnp.int32(-65536)
    return lax.bitcast_convert_type(bits, F32)


def _split3(x):
    hi = _keep_bf16_bits(x)
    r1 = x - hi
    mid = _keep_bf16_bits(r1)
    lo = _keep_bf16_bits(r1 - mid)
    return hi.astype(BF16), mid.astype(BF16), lo.astype(BF16)


def _dot3(m01, x):
    hi, mid, lo = _split3(x)
    d = lambda p: jnp.dot(m01, p, preferred_element_type=F32)
    return (d(lo) + d(mid)) + d(hi)


def _hg_prologue(hq, hf, lb, lower, total):
    sq = _sigmoid(hq)
    q = hq * sq
    sg = _sigmoid(hf)
    f = lb + (1.0 - lb) * sg
    g = jnp.log(f)
    k = 1.0 - f
    b = _dot3(lower, g)
    bl = _dot3(total, g)
    return q, k, f, sg, sq, b, bl


def _stack16(fn):
    return [fn(t) for t in range(HG_BLK)]


def hgrn2_fwd(proj, offs, lb, n_seq, seq, *, name, rider=None):
    T = n_seq * seq
    W = HG_HEADS * HG_DIM
    n_tiles = seq // HG_TILE
    nb = HG_TILE // HG_BLK
    lower, _, total, bd, sel_t, _ = _hg_constants()

    def body(hq_ref, hf_ref, hi_ref, lb_ref, lower_ref, total_ref, bd_ref, selt_ref,
             o_ref, st_out_ref,
             st_ref, q_s, k_s, v_s, b_s, qt_s, kt_s, d_s, p_s):
        @pl.when(pl.program_id(1) == 0)
        def _():
            st_ref[...] = jnp.zeros_like(st_ref)

        q, k, _, _, _, b, bl = _hg_prologue(hq_ref[...], hf_ref[...], lb_ref[...], lower_ref[...], total_ref[...])
        q_s[...] = q
        k_s[...] = k
        v_s[...] = hi_ref[...]
        b_s[...] = b
        qt_s[...] = q * jnp.exp(b)
        kt_s[...] = k * jnp.exp(jnp.minimum(bl - b, 0.0))
        d_s[...] = jnp.exp(bl)
        rowi = lax.broadcasted_iota(jnp.int32, (HG_BLK, W), 0)

        def block(i, carry):
            r0 = pl.multiple_of(i * HG_BLK, HG_BLK)
            rows = pl.ds(r0, HG_BLK)
            qi, ki, vi, bi = q_s[rows, :], k_s[rows, :], v_s[rows, :], b_s[rows, :]
            for t in range(HG_BLK):
                e = jnp.where(rowi <= t, jnp.exp(jnp.minimum(bi[t:t + 1, :] - bi, 0.0)), 0.0)
                p_s[pl.ds(t * HG_BLK, HG_BLK), :] = (e * qi[t:t + 1, :] * ki).astype(BF16)
            a_b = jnp.dot(p_s[...], bd_ref[...], preferred_element_type=F32)
            vt = jnp.concatenate([vi] * HG_BLK, axis=0)
            o_blk = jnp.dot(selt_ref[...], (a_b * vt).astype(BF16), preferred_element_type=F32)
            qti, kti, di = qt_s[rows, :], kt_s[rows, :], d_s[rows, :]
            outs = []
            for h in range(HG_HEADS):
                hs = slice(h * HG_DIM, (h + 1) * HG_DIM)
                st_h = st_ref[hs, :]
                st_out_ref[i, hs, :] = st_h
                outs.append(lax.dot_general(qti[:, hs].astype(BF16), st_h.astype(BF16),
                                            (((1,), (1,)), ((), ())), preferred_element_type=F32))
                upd = lax.dot_general(vi[:, hs].astype(BF16), kti[:, hs].astype(BF16),
                                      (((0,), (0,)), ((), ())), preferred_element_type=F32)
                st_ref[hs, :] = st_h * di[0:1, hs] + upd
            o_ref[rows, :] = o_blk + jnp.concatenate(outs, axis=1)
            return carry

        lax.fori_loop(0, nb, block, 0, unroll=2)

    col = lambda off: functools.partial(lambda s, t, blk: (s * n_tiles + t, blk), blk=off // W)
    const = lambda m: pl.BlockSpec(m.shape, lambda s, t: (0, 0))
    tile_f32 = pltpu.VMEM((HG_TILE, W), F32)
    in_specs = [pl.BlockSpec((HG_TILE, W), col(offs[0])), pl.BlockSpec((HG_TILE, W), col(offs[1])),
                pl.BlockSpec((HG_TILE, W), col(offs[2])), const(lb), const(lower), const(total), const(bd),
                const(sel_t)]
    out_specs = [pl.BlockSpec((HG_TILE, W), lambda s, t: (s * n_tiles + t, 0)),
                 pl.BlockSpec((nb, W, HG_DIM), lambda s, t: (s * n_tiles + t, 0, 0))]
    out_shape = [jax.ShapeDtypeStruct((T, W), F32), jax.ShapeDtypeStruct((T // HG_BLK, W, HG_DIM), F32)]
    scratch = [pltpu.VMEM((W, HG_DIM), F32)] + [tile_f32] * 7 + [pltpu.VMEM((HG_BLK * HG_BLK, W), BF16)]
    args = [proj, proj, proj, lb, lower, total, bd, sel_t]
    params = _cparams(("arbitrary", "arbitrary"))
    if rider is not None:
        r_in, r_out, r_sems = rider.specs()
        body = rider.wrap(body, len(in_specs), len(out_specs), 2)
        in_specs, out_specs, out_shape = in_specs + r_in, out_specs + r_out, out_shape + rider.out_shape
        scratch, args = scratch + r_sems, args + rider.ins
        params = pltpu.CompilerParams(dimension_semantics=("arbitrary", "arbitrary"),
                                      vmem_limit_bytes=VMEM_LIMIT_BYTES, has_side_effects=True)
    res = pl.pallas_call(body, name=name, grid=(n_seq, n_tiles), in_specs=in_specs, out_specs=out_specs,
                         out_shape=out_shape, scratch_shapes=scratch, compiler_params=params)(*args)
    return res[0], res[1], list(res[2:])


def hgrn2_bwd(proj, offs, lb, do, states, n_seq, seq, *, name):
    T = n_seq * seq
    W = HG_HEADS * HG_DIM
    n_tiles = seq // HG_TILE
    nb = HG_TILE // HG_BLK
    lower, upper, total, bd, sel_t, sel_s = _hg_constants()

    def body(hq_ref, hf_ref, hi_ref, do_ref, st_in_ref, lb_ref, lower_ref, upper_ref, total_ref, bd_ref,
             selt_ref, sels_ref,
             dhq_ref, dhf_ref, dhi_ref, dlb_ref,
             dst_ref, q_s, k_s, v_s, b_s, qt_s, kt_s, d_s, eb_s, ekb_s, dq_s, dk_s, db_s, dv_s,
             p_s, e_s, w_s):
        first = jnp.logical_and(pl.program_id(0) == 0, pl.program_id(1) == 0)

        @pl.when(first)
        def _():
            dlb_ref[...] = jnp.zeros_like(dlb_ref)

        @pl.when(pl.program_id(1) == 0)
        def _():
            dst_ref[...] = jnp.zeros_like(dst_ref)

        hq, lbv = hq_ref[...], lb_ref[...]
        q, k, f, sg, sq, b, bl = _hg_prologue(hq, hf_ref[...], lbv, lower_ref[...], total_ref[...])
        eb = jnp.exp(b)
        ekb = jnp.exp(jnp.minimum(bl - b, 0.0))
        q_s[...] = q
        k_s[...] = k
        v_s[...] = hi_ref[...]
        b_s[...] = b
        eb_s[...] = eb
        ekb_s[...] = ekb
        qt_s[...] = q * eb
        kt_s[...] = k * ekb
        d_s[...] = jnp.exp(bl)
        rowi = lax.broadcasted_iota(jnp.int32, (HG_BLK, W), 0)
        last_row = rowi == HG_BLK - 1

        def block(j, carry):
            i = nb - 1 - j
            r0 = pl.multiple_of(i * HG_BLK, HG_BLK)
            rows = pl.ds(r0, HG_BLK)
            qi, ki, vi, bi, doi = q_s[rows, :], k_s[rows, :], v_s[rows, :], b_s[rows, :], do_ref[rows, :]
            for t in range(HG_BLK):
                sl = pl.ds(t * HG_BLK, HG_BLK)
                e = jnp.where(rowi <= t, jnp.exp(jnp.minimum(bi[t:t + 1, :] - bi, 0.0)), 0.0)
                e_s[sl, :] = e
                p_s[sl, :] = (e * qi[t:t + 1, :] * ki).astype(BF16)
                w_s[sl, :] = (doi[t:t + 1, :] * vi).astype(BF16)
            a_b = jnp.dot(p_s[...], bd_ref[...], preferred_element_type=F32)
            da_b = jnp.dot(w_s[...], bd_ref[...], preferred_element_type=F32)
            x = da_b * e_s[...]
            k_til = jnp.concatenate([ki] * HG_BLK, axis=0)
            q_rep = jnp.concatenate([jnp.broadcast_to(qi[t:t + 1, :], (HG_BLK, W)) for t in range(HG_BLK)], axis=0)
            do_rep = jnp.concatenate([jnp.broadcast_to(doi[t:t + 1, :], (HG_BLK, W)) for t in range(HG_BLK)], axis=0)
            dq_in = jnp.dot(selt_ref[...], (x * k_til).astype(BF16), preferred_element_type=F32)
            dk_in = jnp.dot(sels_ref[...], (x * q_rep).astype(BF16), preferred_element_type=F32)
            dv_in = jnp.dot(sels_ref[...], (a_b * do_rep).astype(BF16), preferred_element_type=F32)
            qti, kti, di = qt_s[rows, :], kt_s[rows, :], d_s[rows, :]
            dqt, dkt, dvt, dd = [], [], [], []
            for h in range(HG_HEADS):
                hs = slice(h * HG_DIM, (h + 1) * HG_DIM)
                st_h = st_in_ref[i, hs, :]
                dst_h = dst_ref[hs, :]
                do_h, v_h = doi[:, hs].astype(BF16), vi[:, hs].astype(BF16)
                dst_b = dst_h.astype(BF16)
                dqt.append(jnp.dot(do_h, st_h.astype(BF16), preferred_element_type=F32))
                dkt.append(jnp.dot(v_h, dst_b, preferred_element_type=F32))
                dvt.append(lax.dot_general(kti[:, hs].astype(BF16), dst_b, (((1,), (1,)), ((), ())),
                                           preferred_element_type=F32))
                dd.append(jnp.sum(dst_h * st_h, axis=0, keepdims=True))
                upd = lax.dot_general(do_h, qti[:, hs].astype(BF16), (((0,), (0,)), ((), ())),
                                      preferred_element_type=F32)
                dst_ref[hs, :] = dst_h * di[0:1, hs] + upd
            dqt = jnp.concatenate(dqt, axis=1)
            dkt = jnp.concatenate(dkt, axis=1)
            dvt = jnp.concatenate(dvt, axis=1)
            dd = jnp.concatenate(dd, axis=1)
            dbl = jnp.sum(dkt * kti, axis=0, keepdims=True) + dd * di[0:1, :]
            db = qi * dq_in - ki * dk_in + dqt * qti - dkt * kti
            db_s[rows, :] = db + jnp.where(last_row, dbl, 0.0)
            dq_s[rows, :] = dq_in + dqt * eb_s[rows, :]
            dk_s[rows, :] = dk_in + dkt * ekb_s[rows, :]
            dv_s[rows, :] = dv_in + dvt
            return carry

        lax.fori_loop(0, nb, block, 0, unroll=2)

        dg = _dot3(upper_ref[...], db_s[...])
        dhq_ref[...] = (dq_s[...] * (sq * (1.0 + hq * (1.0 - sq)))).astype(dhq_ref.dtype)
        df = dg / f - dk_s[...]
        dhf_ref[...] = (df * (1.0 - lbv) * (sg * (1.0 - sg))).astype(dhf_ref.dtype)
        dhi_ref[...] = dv_s[...].astype(dhi_ref.dtype)
        dlb_ref[...] += _colsum(df * (1.0 - sg))

    rev = lambda s, t: s * n_tiles + (n_tiles - 1 - t)
    col = lambda off: functools.partial(lambda s, t, blk: (rev(s, t), blk), blk=off // W)
    const = lambda m: pl.BlockSpec(m.shape, lambda s, t: (0, 0))
    row = pl.BlockSpec((HG_TILE, W), lambda s, t: (rev(s, t), 0))
    tile_f32 = pltpu.VMEM((HG_TILE, W), F32)
    n2 = HG_BLK * HG_BLK
    return pl.pallas_call(
        body, name=name,
        grid=(n_seq, n_tiles),
        in_specs=[pl.BlockSpec((HG_TILE, W), col(offs[0])), pl.BlockSpec((HG_TILE, W), col(offs[1])),
                  pl.BlockSpec((HG_TILE, W), col(offs[2])), row,
                  pl.BlockSpec((nb, W, HG_DIM), lambda s, t: (rev(s, t), 0, 0)),
                  const(lb), const(lower), const(upper), const(total), const(bd), const(sel_t), const(sel_s)],
        out_specs=[row, row, row, pl.BlockSpec((1, W), lambda s, t: (0, 0))],
        out_shape=[jax.ShapeDtypeStruct((T, W), BF16)] * 3 + [jax.ShapeDtypeStruct((1, W), F32)],
        scratch_shapes=[pltpu.VMEM((W, HG_DIM), F32)] + [tile_f32] * 13
                       + [pltpu.VMEM((n2, W), BF16), pltpu.VMEM((n2, W), F32), pltpu.VMEM((n2, W), BF16)],
        compiler_params=_cparams(("arbitrary", "arbitrary")),
    )(proj, proj, proj, do, states, lb, lower, upper, total, bd, sel_t, sel_s)


def _diag_mask(tq):
    return lax.broadcasted_iota(jnp.int32, (tq, tq), 1) <= lax.broadcasted_iota(jnp.int32, (tq, tq), 0)


def _qk(q, k):
    return lax.dot_general(q, k, (((1,), (1,)), ((), ())), preferred_element_type=F32)


def _causal_pairs(n, sweeps=1, by_key=False):
    if by_key:
        rows = [(i, j, 0) for j in range(n) for i in range(j, n)]
    else:
        rows = [(i, j, s) for i in range(n) for s in range(sweeps) for j in range(i + 1)]
    return tuple(jnp.asarray(np.array([r[c] for r in rows], np.int32)) for c in range(3))


def _fox_placement(fh):
    hw, wa = fh * FOX_HDIM, fh * FOX_AUG
    pq, pk = np.zeros((hw, wa), np.float32), np.zeros((hw, wa), np.float32)
    aq, ak = np.zeros((3 * LANES, wa), np.float32), np.zeros((3 * LANES, wa), np.float32)
    oq, ok = np.zeros((1, wa), np.float32), np.zeros((1, wa), np.float32)
    for h in range(fh):
        src, dst = np.arange(h * FOX_HDIM, (h + 1) * FOX_HDIM), np.arange(h * FOX_AUG, h * FOX_AUG + FOX_HDIM)
        pq[src, dst] = FOX_HDIM ** -0.5
        pk[src, dst] = 1.0
        gate = h * FOX_AUG + FOX_HDIM
        for r in range(3):
            aq[r * LANES + h, gate + r] = 1.0
            ak[r * LANES + h, gate + 3 + r] = -1.0
        oq[0, gate + 3:gate + 6] = 1.0
        ok[0, gate:gate + 3] = 1.0
    bf = lambda m: jnp.asarray(m, dtype=BF16)
    return {"pq": bf(pq), "pk": bf(pk), "aq": bf(aq), "ak": bf(ak), "oq": jnp.asarray(oq), "ok": jnp.asarray(ok),
            "pqt": bf(pq.T), "pkt": bf(pk.T)}


def _fox_specs(tq, fh):
    def spec(tab):
        return pl.BlockSpec((None, tq, FOX_AUG), lambda b, t, *tabs: (b // fh, tabs[tab][t], b % fh))
    return spec(0), spec(1)


def fox_fwd(qa, ka, va, *, name):
    n_seq, S, width = qa.shape
    fh = width // FOX_AUG
    BH = n_seq * fh
    tq = min(FOX_TQ, S)
    itab, jtab, _ = _causal_pairs(S // tq)

    def body(itab_ref, jtab_ref, q_ref, k_ref, v_ref, o_ref, ox_ref, lse_ref, m_s, l_s, acc_s, acc_lo_s):
        t = pl.program_id(1)
        i, j = itab_ref[t], jtab_ref[t]

        @pl.when(j == 0)
        def _():
            m_s[...] = jnp.full_like(m_s, NEG_INF)
            l_s[...] = jnp.zeros_like(l_s)
            acc_s[...] = jnp.zeros_like(acc_s)
            acc_lo_s[...] = jnp.zeros_like(acc_lo_s)

        def step(on_diagonal):
            s = _qk(q_ref[...], k_ref[...])
            if on_diagonal:
                s = jnp.where(_diag_mask(tq), s, NEG_INF)
            m_prev = m_s[...]
            m_new = jnp.maximum(m_prev, jnp.max(s, axis=-1, keepdims=True))
            alpha = jnp.exp(m_prev - m_new)
            p = jnp.exp(s - m_new[:, 0:1])
            p_hi = p.astype(BF16)
            p_lo = (p - p_hi.astype(F32)).astype(BF16)
            v = v_ref[...]
            l_s[...] = alpha * l_s[...] + jnp.sum(p, axis=-1, keepdims=True)
            acc_s[...] = alpha * acc_s[...] + jnp.dot(p_hi, v, preferred_element_type=F32)
            acc_lo_s[...] = alpha * acc_lo_s[...] + jnp.dot(p_lo, v, preferred_element_type=F32)
            m_s[...] = m_new

        @pl.when(j < i)
        def _():
            step(False)

        @pl.when(j == i)
        def _():
            step(True)
            inv_l = 1.0 / l_s[...]
            o_ref[...] = (acc_s[...] * inv_l).astype(o_ref.dtype)
            ox_ref[...] = (acc_s[...] + acc_lo_s[...]) * inv_l
            lse_ref[...] = m_s[...] + jnp.log(l_s[...])

    qspec, kspec = _fox_specs(tq, fh)
    wide = jax.ShapeDtypeStruct((n_seq, S, width), F32)
    return pl.pallas_call(
        body, name=name,
        grid_spec=pltpu.PrefetchScalarGridSpec(
            num_scalar_prefetch=2, grid=(BH, itab.shape[0]),
            in_specs=[qspec, kspec, kspec],
            out_specs=[qspec, qspec, qspec],
            scratch_shapes=[pltpu.VMEM((tq, LANES), F32)] * 4),
        out_shape=[jax.ShapeDtypeStruct((n_seq, S, width), BF16), wide, wide],
        compiler_params=_cparams(("parallel", "arbitrary")),
    )(itab, jtab, qa, ka, va)


def _fox_ds(q, k, v, do, ox, lse, on_diagonal):
    s = _qk(q, k)
    if on_diagonal:
        s = jnp.where(_diag_mask(s.shape[0]), s, NEG_INF)
    p = jnp.exp(s - lse[:, 0:1])
    delta = jnp.sum(do.astype(F32) * ox, axis=-1, keepdims=True)
    return p, p * (_qk(do, v) - delta)


def fox_bwd_dq(qa, ka, va, do, ox, lse, *, name):
    n_seq, S, width = qa.shape
    fh = width // FOX_AUG
    BH = n_seq * fh
    tq = min(FOX_TQ, S)
    itab, jtab, _ = _causal_pairs(S // tq)

    def body(itab_ref, jtab_ref, q_ref, k_ref, v_ref, do_ref, ox_ref, lse_ref, dq_ref):
        t = pl.program_id(1)
        i, j = itab_ref[t], jtab_ref[t]

        def step(on_diagonal):
            _, ds = _fox_ds(q_ref[...], k_ref[...], v_ref[...], do_ref[...], ox_ref[...], lse_ref[...], on_diagonal)
            dq = jnp.dot(ds.astype(BF16), k_ref[...], preferred_element_type=F32)

            @pl.when(j == 0)
            def _():
                dq_ref[...] = dq

            @pl.when(j > 0)
            def _():
                dq_ref[...] += dq

        @pl.when(j < i)
        def _():
            step(False)

        @pl.when(j == i)
        def _():
            step(True)

    qspec, kspec = _fox_specs(tq, fh)
    return pl.pallas_call(
        body, name=name,
        grid_spec=pltpu.PrefetchScalarGridSpec(
            num_scalar_prefetch=2, grid=(BH, itab.shape[0]),
            in_specs=[qspec, kspec, kspec, qspec, qspec, qspec],
            out_specs=qspec),
        out_shape=jax.ShapeDtypeStruct((n_seq, S, width), F32),
        compiler_params=_cparams(("parallel", "arbitrary")),
    )(itab, jtab, qa, ka, va, do, ox, lse)


def fox_bwd_dkv(qa, ka, va, do, ox, lse, *, name):
    n_seq, S, width = qa.shape
    fh = width // FOX_AUG
    BH = n_seq * fh
    tq = min(FOX_TQ, S)
    itab, jtab, _ = _causal_pairs(S // tq, by_key=True)

    def body(itab_ref, jtab_ref, q_ref, k_ref, v_ref, do_ref, ox_ref, lse_ref, dk_ref, dv_ref, dsum_ref):
        t = pl.program_id(1)
        i, j = itab_ref[t], jtab_ref[t]

        def step(on_diagonal):
            q, do = q_ref[...], do_ref[...]
            p, ds = _fox_ds(q, k_ref[...], v_ref[...], do, ox_ref[...], lse_ref[...], on_diagonal)
            tn = (((0,), (0,)), ((), ()))
            dv = lax.dot_general(p.astype(BF16), do, tn, preferred_element_type=F32)
            dk = lax.dot_general(ds.astype(BF16), q, tn, preferred_element_type=F32)
            if on_diagonal:
                dv_ref[...], dk_ref[...], dsum_ref[...] = dv, dk, _colsum(ds)
            else:
                dv_ref[...] += dv
                dk_ref[...] += dk
                dsum_ref[...] += _colsum(ds)

        @pl.when(i == j)
        def _():
            step(True)

        @pl.when(i > j)
        def _():
            step(False)

    qspec, kspec = _fox_specs(tq, fh)
    return pl.pallas_call(
        body, name=name,
        grid_spec=pltpu.PrefetchScalarGridSpec(
            num_scalar_prefetch=2, grid=(BH, itab.shape[0]),
            in_specs=[qspec, kspec, kspec, qspec, qspec, qspec],
            out_specs=[kspec, kspec, pl.BlockSpec((None, 1, tq), lambda b, t, it, jt: (b, 0, jt[t]))]),
        out_shape=[jax.ShapeDtypeStruct((n_seq, S, width), F32), jax.ShapeDtypeStruct((n_seq, S, width), F32),
                   jax.ShapeDtypeStruct((BH, 1, S), F32)],
        compiler_params=_cparams(("parallel", "arbitrary")),
    )(itab, jtab, qa, ka, va, do, ox, lse)


def seq_cumsum(x, n_seq, seq, *, reverse, name):
    T, C = x.shape
    tb = min(256, seq)
    n = seq // tb
    r = np.arange(tb)
    tri = (r[None, :] >= r[:, None]) if reverse else (r[None, :] <= r[:, None])
    tri = jnp.asarray(tri.astype(np.float32), dtype=BF16)

    def body(x_ref, tri_ref, o_ref, carry_s):
        @pl.when(pl.program_id(1) == 0)
        def _():
            carry_s[...] = jnp.zeros_like(carry_s)

        xv = x_ref[...]
        o_ref[...] = _dot3(tri_ref[...], xv) + carry_s[...]
        carry_s[...] += _colsum(xv)

    blk = (lambda s, t: (s * n + (n - 1 - t), 0)) if reverse else (lambda s, t: (s * n + t, 0))
    return pl.pallas_call(
        body, name=name,
        grid=(n_seq, n),
        in_specs=[pl.BlockSpec((tb, C), blk), pl.BlockSpec((tb, tb), lambda s, t: (0, 0))],
        out_specs=pl.BlockSpec((tb, C), blk),
        out_shape=jax.ShapeDtypeStruct((T, C), F32),
        scratch_shapes=[pltpu.VMEM((1, C), F32)],
        compiler_params=_cparams(("arbitrary", "arbitrary")),
    )(x, tri)


def _place():
    return lax.axis_index("x"), lax.axis_index("y"), lax.axis_index("c")


def _other_chips(x, y):
    return [(1 - x, y), (x, 1 - y), (1 - x, 1 - y)]


def _hbm_call(body, ins, out_shape, n_sems, *, name):
    hbm = pl.BlockSpec(memory_space=pl.ANY)
    return pl.pallas_call(
        body, name=name,
        in_specs=[hbm] * len(ins), out_specs=[hbm] * len(out_shape), out_shape=out_shape,
        scratch_shapes=[pltpu.SemaphoreType.DMA((n_sems,)), pltpu.SemaphoreType.DMA((n_sems,)),
                        pltpu.SemaphoreType.DMA((len(ins),))],
        compiler_params=pltpu.CompilerParams(has_side_effects=True),
    )(*ins)


def allgather_chips(shards, *, name):
    return _exchange_call(allgather_rider(shards), name=name)


def _allgather_ops(x_refs, o_refs, send_sems, recv_sems, local_sems):
    def copies():
        x, y, c = _place()
        me = 2 * x + y
        chips = _other_chips(x, y)
        own, first, passed, landed, handed = [], [], [], [], []
        for b, (x_ref, o_ref) in enumerate(zip(x_refs, o_refs)):
            half = x_ref.shape[0] // 2
            mine, theirs = pl.ds(c * half, half), pl.ds((1 - c) * half, half)
            own.append(pltpu.make_async_copy(x_ref, o_ref.at[me], local_sems.at[b]))

            def copy(k, src, chip, rows, to, o_ref=o_ref, b=b):
                return pltpu.make_async_remote_copy(src_ref=src, dst_ref=o_ref.at[2 * chip[0] + chip[1], rows],
                                                    send_sem=send_sems.at[6 * b + k], recv_sem=recv_sems.at[6 * b + k],
                                                    device_id=to, device_id_type=MESH)
            for j, chip in enumerate(chips):
                first.append(copy(j, x_ref.at[mine], (x, y), mine, (*chip, c)))
                landed.append(copy(j, x_ref.at[mine], chip, mine, (*chip, c)))
                passed.append(copy(3 + j, o_ref.at[2 * chip[0] + chip[1], mine], chip, mine, (x, y, 1 - c)))
                handed.append(copy(3 + j, x_ref.at[mine], chip, theirs, (x, y, 1 - c)))
        return own, first, passed, landed, handed

    def start():
        own, first, _, _, _ = copies()
        for cp in own + first:
            cp.start()

    def finish():
        own, first, passed, landed, handed = copies()
        for arrived, forward in zip(landed, passed):
            arrived.wait_recv()
            forward.start()
        for cp in handed:
            cp.wait_recv()
        for cp in first + passed:
            cp.wait_send()
        for cp in own:
            cp.wait()
    return start, finish


def _scatter_ops(x_refs, o_refs, send_sems, recv_sems, local_sems):
    def copies():
        x, y, c = _place()
        return [pltpu.make_async_remote_copy(
            src_ref=x_ref.at[2 * px + py], dst_ref=o_ref.at[j], send_sem=send_sems.at[3 * b + j],
            recv_sem=recv_sems.at[3 * b + j], device_id=(px, py, c), device_id_type=MESH)
            for b, (x_ref, o_ref) in enumerate(zip(x_refs, o_refs)) for j, (px, py) in enumerate(_other_chips(x, y))]

    def start():
        for cp in copies():
            cp.start()

    def finish():
        sends = copies()
        for cp in sends:
            cp.wait_recv()
        for cp in sends:
            cp.wait_send()
    return start, finish


class Rider(NamedTuple):
    ins: list
    out_shape: list
    n_sems: int
    ops: object

    def specs(self):
        hbm = pl.BlockSpec(memory_space=pl.ANY)
        sems = [pltpu.SemaphoreType.DMA((self.n_sems,)), pltpu.SemaphoreType.DMA((self.n_sems,)),
                pltpu.SemaphoreType.DMA((len(self.ins),))]
        return [hbm] * len(self.ins), [hbm] * len(self.out_shape), sems

    def wrap(self, body, n_in, n_out, grid_rank):
        k_in, k_out = len(self.ins), len(self.out_shape)

        def carried(*refs):
            ins, r_ins = refs[:n_in], refs[n_in:n_in + k_in]
            outs = refs[n_in + k_in:n_in + k_in + n_out]
            r_outs = refs[n_in + k_in + n_out:n_in + k_in + n_out + k_out]
            scratch, sems = refs[n_in + k_in + n_out + k_out:-3], refs[-3:]
            first = functools.reduce(jnp.logical_and, [pl.program_id(a) == 0 for a in range(grid_rank)])
            last = functools.reduce(jnp.logical_and,
                                    [pl.program_id(a) == pl.num_programs(a) - 1 for a in range(grid_rank)])
            pl.when(first)(lambda: self.ops(r_ins, r_outs, *sems)[0]())
            body(*ins, *outs, *scratch)
            pl.when(last)(lambda: self.ops(r_ins, r_outs, *sems)[1]())
        return carried


def _exchange_call(rider, *, name):
    def body(*refs):
        k = len(rider.ins)
        start, finish = rider.ops(refs[:k], refs[k:k + len(rider.out_shape)], *refs[-3:])
        start()
        finish()
    in_specs, out_specs, sems = rider.specs()
    return pl.pallas_call(body, name=name, in_specs=in_specs, out_specs=out_specs, out_shape=rider.out_shape,
                          scratch_shapes=sems, compiler_params=pltpu.CompilerParams(has_side_effects=True))(*rider.ins)


def allgather_rider(shards):
    assert all(s.shape[0] % (2 * ROW_ALIGN) == 0 for s in shards)
    return Rider(list(shards), [jax.ShapeDtypeStruct((4,) + s.shape, s.dtype) for s in shards], 6 * len(shards),
                 _allgather_ops)


def scatter_rider(parts):
    return Rider(list(parts), [jax.ShapeDtypeStruct((3,) + p.shape[1:], p.dtype) for p in parts], 3 * len(parts),
                 _scatter_ops)


def scatter_chips(parts, *, name):
    return _exchange_call(scatter_rider(parts), name=name)


def swap_cores(vs, *, name):
    nb = len(vs)

    def body(*refs):
        x_refs, o_refs = refs[:nb], refs[nb:2 * nb]
        send_sems, recv_sems, _ = refs[2 * nb:]
        x, y, c = _place()
        copies = [pltpu.make_async_remote_copy(src_ref=x_ref, dst_ref=o_ref, send_sem=send_sems.at[b],
                                               recv_sem=recv_sems.at[b], device_id=(x, y, 1 - c), device_id_type=MESH)
                  for b, (x_ref, o_ref) in enumerate(zip(x_refs, o_refs))]
        for cp in copies:
            cp.start()
        for cp in copies:
            cp.wait()

    return _hbm_call(body, vs, [jax.ShapeDtypeStruct(v.shape, v.dtype) for v in vs], nb, name=name)


def allreduce_small(v, *, name):
    R, C = v.shape

    def body(x_ref, o_ref, gath_ref, send_sems, recv_sems):
        x, y, c = _place()
        me = 4 * x + 2 * y + c
        gath_ref[me] = x_ref[...]
        flips = [(k >> 2 & 1, k >> 1 & 1, k & 1) for k in range(1, 8)]
        sends = []
        for j, (fx, fy, fc) in enumerate(flips):
            peer = (x ^ fx, y ^ fy, c ^ fc)
            cp = pltpu.make_async_remote_copy(src_ref=x_ref, dst_ref=gath_ref.at[me], send_sem=send_sems.at[j],
                                              recv_sem=recv_sems.at[j], device_id=peer, device_id_type=MESH)
            cp.start()
            sends.append(cp)
        for j, (fx, fy, fc) in enumerate(flips):
            peer = (x ^ fx, y ^ fy, c ^ fc)
            pltpu.make_async_remote_copy(src_ref=x_ref, dst_ref=gath_ref.at[4 * peer[0] + 2 * peer[1] + peer[2]],
                                         send_sem=send_sems.at[j], recv_sem=recv_sems.at[j], device_id=peer,
                                         device_id_type=MESH).wait_recv()
        for cp in sends:
            cp.wait_send()
        total = gath_ref[0]
        for d in range(1, 8):
            total = total + gath_ref[d]
        o_ref[...] = total

    vm = pl.BlockSpec(memory_space=pltpu.VMEM)
    out, _ = pl.pallas_call(
        body, name=name,
        in_specs=[vm], out_specs=[vm, vm],
        out_shape=[jax.ShapeDtypeStruct((R, C), F32), jax.ShapeDtypeStruct((8, R, C), F32)],
        scratch_shapes=[pltpu.SemaphoreType.DMA((7,)), pltpu.SemaphoreType.DMA((7,))],
        compiler_params=pltpu.CompilerParams(has_side_effects=True),
    )(v)
    return out


ROW_ALIGN = 16
PACK_W = 1024
SUM_TILE = 512
BIG_WEIGHTS = (("w_in", 1), ("w_a", 1), ("w_b", 1), ("w_o", 0), ("w_ff1", 1), ("w_ff2", 0), ("w_pg", 0), ("w_p", 1))


def _b_layout(d, ple):
    hw, q = d // 2, d // 4
    small = 2 * d + 2 * q
    lay = {"w_ff1": (0, 0, d, d), "w_ff2": (d, 0, d, d), "w_o": (2 * d, 0, q, d), "w_pg": (2 * d + q, 0, q, d),
           "w_a": (small, 0, hw, q), "w_b": (small, q, hw, q), "w_p": (small, 2 * q, ple, q)}
    return lay, small + hw


def pack_a(w_in_shard):
    rows, cols = w_in_shard.shape
    pad = -cols % LANES
    return jnp.concatenate([w_in_shard, jnp.zeros((rows, pad), w_in_shard.dtype)], axis=1)


def pack_b(shards, d):
    hw, q = d // 2, d // 4
    dt = shards["w_a"].dtype
    wp = shards["w_p"]
    wp = jnp.concatenate([wp, jnp.zeros((hw - wp.shape[0], q), dt)], axis=0)
    small = jnp.concatenate([shards["w_a"], shards["w_b"], wp, jnp.zeros((hw, d - 3 * q), dt)], axis=1)
    return jnp.concatenate([shards["w_ff1"], shards["w_ff2"], shards["w_o"], shards["w_pg"], small], axis=0)


def unpack_b(buf, lay):
    return {nm: buf[r0:r0 + rows, c0:c0 + cols] for nm, (r0, c0, rows, cols) in lay.items()}


def _win_layout(d):
    hw = d // 2
    fh = hw // FOX_HDIM
    orig = {"hq": (0, hw), "hf": (hw, hw), "hi": (2 * hw, hw), "hg": (3 * hw, hw), "fq": (4 * hw, hw),
            "fk": (5 * hw, hw), "fv": (6 * hw, hw), "ff": (7 * hw, fh), "ga": (7 * hw + fh, d), "gb": (7 * hw + fh + d, d)}
    order = ["ga", "gb", "hq", "hf", "hi", "hg", "fq", "fk", "fv", "ff"]
    mine, off = {}, 0
    for nm in order:
        width = orig[nm][1] if nm != "ff" else LANES
        mine[nm] = (off, width)
        off += width
    return orig, order, mine, off


def _adam_fn(rows, vecs):
    w, g, m, v = rows
    m2 = ADAM_B1 * m + (1.0 - ADAM_B1) * g
    v2 = ADAM_B2 * v + (1.0 - ADAM_B2) * (g * g)
    m_hat = m2 / (1.0 - ADAM_B1 ** ADAM_STEP)
    v_hat = v2 / (1.0 - ADAM_B2 ** ADAM_STEP)
    delta = -ADAM_LR * (m_hat / (jnp.sqrt(v_hat) + ADAM_EPS) + ADAM_WD * w)
    return [delta, m2, v2], []


def adamw(w, g, m, v, *, name):
    c = w.shape[1]
    (delta, m2, v2), _ = rowwise(_adam_fn, [w, g, m, v], [], [(c, F32)] * 3, name=name, tm=256)
    return delta, m2, v2


def kernel(x, p, ln0_g, ln0_b, w_in, hg_lb, hg_norm_g, fox_fb, w_a, w_b, w_o, ln1_g, ln1_b, w_ff1, w_ff2, w_pg, w_p, ln2_g, ln2_b, loss_target, m_ln0_g, m_ln0_b, m_w_in, m_hg_lb, m_hg_norm_g, m_fox_fb, m_w_a, m_w_b, m_w_o, m_ln1_g, m_ln1_b, m_w_ff1, m_w_ff2, m_w_pg, m_w_p, m_ln2_g, m_ln2_b, v_ln0_g, v_ln0_b, v_w_in, v_hg_lb, v_hg_norm_g, v_fox_fb, v_w_a, v_w_b, v_w_o, v_ln1_g, v_ln1_b, v_w_ff1, v_w_ff2, v_w_pg, v_w_p, v_ln2_g, v_ln2_b):
    n_seq, seq, d = x.shape
    T = n_seq * seq
    hw = d // 2
    fh = hw // FOX_HDIM
    bh = n_seq * fh
    orig, order, mine, n_in = _win_layout(d)

    big = {"w_in": w_in[0], "w_a": w_a[0], "w_b": w_b[0], "w_o": w_o[0], "w_ff1": w_ff1[0], "w_ff2": w_ff2[0],
           "w_pg": w_pg[0], "w_p": w_p[0]}
    big_m = {"w_in": m_w_in[0], "w_a": m_w_a[0], "w_b": m_w_b[0], "w_o": m_w_o[0], "w_ff1": m_w_ff1[0],
             "w_ff2": m_w_ff2[0], "w_pg": m_w_pg[0], "w_p": m_w_p[0]}
    big_v = {"w_in": v_w_in[0], "w_a": v_w_a[0], "w_b": v_w_b[0], "w_o": v_w_o[0], "w_ff1": v_w_ff1[0],
             "w_ff2": v_w_ff2[0], "w_pg": v_w_pg[0], "w_p": v_w_p[0]}
    names = [nm for nm, _ in BIG_WEIGHTS]
    axis = dict(BIG_WEIGHTS)
    ple = w_p.shape[1]
    lay, b_rows = _b_layout(d, ple)
    in_cols = big["w_in"].shape[1]

    (a_all,) = allgather_chips([pack_a(big["w_in"].astype(BF16))], name="allgather_w_in")
    gather_rest = allgather_rider([pack_b({nm: big[nm].astype(BF16) for nm in names if nm != "w_in"}, d)])
    win = jnp.concatenate([a_all[s, :, :in_cols] for s in range(4)], axis=1)
    win_mine = jnp.concatenate(
        [win[:, orig[nm][0]:orig[nm][0] + orig[nm][1]] for nm in order]
        + [jnp.zeros((d, LANES - fh), BF16)], axis=1)

    x2 = x.reshape(T, d)
    tgt = loss_target.reshape(T, d)
    p_b = p.reshape(T, p.shape[-1]).astype(BF16)
    vec = lambda a: a.reshape(1, -1)
    probs = jax.nn.softmax(hg_lb, axis=0)
    lb = vec(probs[0])

    def ln0_fn(rows, vecs):
        h = _ln_stats(rows[0]) * vecs[0] + vecs[1]
        return [h, h], []
    (h0, h0b), _ = rowwise(ln0_fn, [x2], [vec(ln0_g), vec(ln0_b)], [(d, F32), (d, BF16)], name="ln0_fwd")
    proj = matmul_nn(h0b, win_mine, name="in_proj")

    o_raw, hg_states, (b_all,) = hgrn2_fwd(proj, [mine["hq"][0], mine["hf"][0], mine["hi"][0]], lb, n_seq, seq,
                                           name="hgrn2_fwd", rider=gather_rest)
    view = lambda nm, k, n: WView(b_all, lay[nm][0], lay[nm][1], k, n, axis[nm])
    w_ff1_v, w_ff2_v = view("w_ff1", d, 4 * d), view("w_ff2", 4 * d, d)

    def whole(nm):
        r0, c0, rows, cols = lay[nm]
        return jnp.concatenate([b_all[s, r0:r0 + rows, c0:c0 + cols] for s in range(4)], axis=axis[nm])
    w_o_v, w_pg_v, w_a_v, w_p_v, w_b_full = whole("w_o"), whole("w_pg"), whole("w_a"), whole("w_p"), whole("w_b")

    def ya_fn(rows, vecs):
        o, hg = rows
        outs = []
        for h in range(HG_HEADS):
            oh = o[:, h * HG_DIM:(h + 1) * HG_DIM]
            outs.append(oh * lax.rsqrt(jnp.mean(oh * oh, axis=-1, keepdims=True) + RMS_EPS))
        y = jnp.concatenate(outs, axis=1) * vecs[0] * (hg * _sigmoid(hg))
        return [y], []
    (y_a,), _ = rowwise(ya_fn, [o_raw, (proj,) + mine["hg"]], [hg_norm_g], [(hw, BF16)], name="hgrn2_out_fwd")

    fb_pad = jnp.concatenate([fox_fb, jnp.zeros((1, LANES - fh), F32)], axis=1)

    def lf_fn(rows, vecs):
        u = rows[0] + vecs[0]
        return [jnp.minimum(u, 0.0) - jnp.log(1.0 + jnp.exp(-jnp.abs(u)))], []
    (lf,), _ = rowwise(lf_fn, [(proj,) + mine["ff"]], [fb_pad], [(LANES, F32)], name="fox_logf")
    c_cum = seq_cumsum(lf, n_seq, seq, reverse=False, name="fox_cumsum")

    place = _fox_placement(fh)

    def prep_fn(rows, vecs):
        fq_, fk_, fv_, cc = rows
        pq, pk, aq, ak, oq, ok = vecs
        parts = jnp.concatenate(_split3(cc), axis=1)
        mm = lambda a_, b_: jnp.dot(a_, b_, preferred_element_type=F32)
        q_ = mm(fq_.astype(BF16), pq) + mm(parts, aq) + oq
        k_ = mm(fk_.astype(BF16), pk) + mm(parts, ak) + ok
        return [q_, k_, mm(fv_.astype(BF16), pk)], []
    wa = fh * FOX_AUG
    (qa, ka, va), _ = rowwise(prep_fn, [(proj,) + mine["fq"], (proj,) + mine["fk"], (proj,) + mine["fv"], c_cum],
                              [place[nm] for nm in ("pq", "pk", "aq", "ak", "oq", "ok")], [(wa, BF16)] * 3,
                              name="fox_prep")
    as_seq = lambda t2d: t2d.reshape(n_seq, seq, t2d.shape[1])
    o_fox, ox_fox, lse = fox_fwd(as_seq(qa), as_seq(ka), as_seq(va), name="fox_fwd")
    y_b = o_fox.reshape(T, wa)
    wb_pad = jnp.concatenate([w_b_full.reshape(fh, FOX_HDIM, d), jnp.zeros((fh, FOX_AUG - FOX_HDIM, d), BF16)],
                             axis=1).reshape(wa, d)

    pa = matmul_nn(y_a, w_a_v, name="proj_a")
    pb = matmul_nn(y_b, wb_pad, name="proj_b")

    def merge_fn(rows, vecs):
        ga, gb, a, b = rows
        return [_sigmoid(ga) * a + _sigmoid(gb) * b], []
    (merged,), _ = rowwise(merge_fn, [(proj,) + mine["ga"], (proj,) + mine["gb"], pa, pb], [], [(d, BF16)],
                           name="merge_fwd")
    mix = matmul_nn(merged, w_o_v, name="out_proj")

    def ln1_fn(rows, vecs):
        z = ALPHA * rows[0] + rows[1]
        h = _ln_stats(z) * vecs[0] + vecs[1]
        return [z, h, h], []
    (z1, h1, h1b), _ = rowwise(ln1_fn, [h0, mix], [ln1_g, ln1_b], [(d, F32), (d, F32), (d, BF16)], name="ln1_fwd")

    relu2 = lambda u: jnp.square(jnp.maximum(u, 0.0))
    act = matmul_nn(h1b, w_ff1_v, name="ff1", out_dtype=BF16, epilogue=relu2)
    ff = matmul_nn(act, w_ff2_v, name="ff2")
    pg = matmul_nn(h1b, w_pg_v, name="ple_gate")
    pe = matmul_nn(p_b, w_p_v, name="ple_embed")

    def head_fn(rows, vecs):
        h1v, ffv, pgv, pev, t = rows
        g2, b2 = vecs
        sp = _sigmoid(pgv)
        z = ALPHA * h1v + ffv + sp * pev
        y = _ln_stats(z) * g2 + b2
        err = y - t
        loss_rows = 0.5 * jnp.mean(err * err, axis=-1, keepdims=True)
        dy = err * (1.0 / d)
        dz, dg2, db2 = _ln_bwd(z, dy, g2)
        loss_acc = jnp.broadcast_to(_colsum(loss_rows), (1, LANES))
        return [dz, dz, dz * pev * (sp * (1.0 - sp)), dz * sp], [dg2, db2, loss_acc]
    (dz2, dz2b, dpg, dpe), (g_ln2_g, g_ln2_b, loss_part) = rowwise(
        head_fn, [h1, ff, pg, pe, tgt], [ln2_g, ln2_b],
        [(d, F32), (d, BF16), (d, BF16), (d, BF16)], [d, d, LANES], name="head_fwd_bwd")

    dact = lambda da, a: da * (2.0 * jnp.sqrt(a.astype(F32)))
    du = matmul_nn(dz2b, w_ff2_v, transpose_rhs=True, name="d_ff2", out_dtype=BF16, epilogue=dact, aux=act)
    dh1_ff = matmul_nn(du, w_ff1_v, transpose_rhs=True, name="d_ff1")
    dh1_pg = matmul_nn(dpg, w_pg_v, transpose_rhs=True, name="d_ple_gate")

    def ln1_bwd_fn(rows, vecs):
        dh1 = ALPHA * rows[0] + rows[1] + rows[2]
        dz, dg, db = _ln_bwd(rows[3], dh1, vecs[0])
        return [dz, dz], [dg, db]
    (dz1, dz1b), (g_ln1_g, g_ln1_b) = rowwise(ln1_bwd_fn, [dz2, dh1_ff, dh1_pg, z1], [ln1_g],
                                              [(d, F32), (d, BF16)], [d, d], name="ln1_bwd")
    dmerged = matmul_nn(dz1b, w_o_v, transpose_rhs=True, name="d_out_proj")

    def merge_bwd_fn(rows, vecs):
        dm, ga, gb, a, b = rows
        sa, sb = _sigmoid(ga), _sigmoid(gb)
        return [dm * a * (sa * (1.0 - sa)), dm * b * (sb * (1.0 - sb)), dm * sa, dm * sb], []
    (dga, dgb, dma, dmb), _ = rowwise(merge_bwd_fn, [dmerged, (proj,) + mine["ga"], (proj,) + mine["gb"], pa, pb], [],
                                      [(d, BF16)] * 4, name="merge_bwd")
    dya = matmul_nn(dma, w_a_v, transpose_rhs=True, name="d_proj_a")
    dyb = matmul_nn(dmb, wb_pad, transpose_rhs=True, name="d_proj_b", out_dtype=BF16)

    def ya_bwd_fn(rows, vecs):
        o, hg, dy = rows
        ng = vecs[0]
        sg = _sigmoid(hg)
        gate = hg * sg
        dn_parts, do_parts, n_parts = [], [], []
        for h in range(HG_HEADS):
            hs = slice(h * HG_DIM, (h + 1) * HG_DIM)
            oh = o[:, hs]
            r = lax.rsqrt(jnp.mean(oh * oh, axis=-1, keepdims=True) + RMS_EPS)
            nh = oh * r
            dn = dy[:, hs] * ng[:, hs] * gate[:, hs]
            do_parts.append(r * (dn - nh * jnp.mean(dn * nh, axis=-1, keepdims=True)))
            n_parts.append(nh)
        nrm = jnp.concatenate(n_parts, axis=1)
        dhg = dy * nrm * ng * (sg * (1.0 + hg * (1.0 - sg)))
        return [jnp.concatenate(do_parts, axis=1), dhg], [_colsum(dy * nrm * gate)]
    (do_raw, dhg), (g_norm_g,) = rowwise(ya_bwd_fn, [o_raw, (proj,) + mine["hg"], dya], [hg_norm_g],
                                         [(hw, F32), (hw, BF16)], [hw], name="hgrn2_out_bwd")
    dhq, dhf, dhi, g_lb = hgrn2_bwd(proj, [mine["hq"][0], mine["hf"][0], mine["hi"][0]], lb, do_raw, hg_states,
                                    n_seq, seq, name="hgrn2_bwd")

    do_fox = as_seq(dyb)
    dqa = fox_bwd_dq(as_seq(qa), as_seq(ka), as_seq(va), do_fox, ox_fox, lse, name="fox_bwd_dq")
    dka, dva, dsum = fox_bwd_dkv(as_seq(qa), as_seq(ka), as_seq(va), do_fox, ox_fox, lse, name="fox_bwd_dkv")

    def unprep_fn(rows, vecs):
        mm = lambda a_, b_: jnp.dot(a_.astype(BF16), b_, preferred_element_type=F32)
        return [mm(rows[0], vecs[0]), mm(rows[1], vecs[1]), mm(rows[2], vecs[1])], []
    (dfq, dfk, dfv), _ = rowwise(unprep_fn, [dqa.reshape(T, wa), dka.reshape(T, wa), dva.reshape(T, wa)],
                                 [place["pqt"], place["pkt"]], [(hw, BF16)] * 3, name="fox_unprep")
    dc = -dsum.reshape(n_seq, fh, seq).transpose(0, 2, 1).reshape(T, fh)
    dc = jnp.concatenate([dc, jnp.zeros((T, LANES - fh), F32)], axis=1)
    dlf = seq_cumsum(dc, n_seq, seq, reverse=True, name="fox_cumsum_bwd")

    def lf_bwd_fn(rows, vecs):
        u = rows[0] + vecs[0]
        du_ = rows[1] * _sigmoid(-u)
        return [du_], [_colsum(du_)]
    (dff_,), (g_fb,) = rowwise(lf_bwd_fn, [(proj,) + mine["ff"], dlf], [fb_pad], [(LANES, BF16)], [LANES],
                               name="fox_logf_bwd")

    dproj = jnp.concatenate([dga, dgb, dhq, dhf, dhi, dhg, dfq, dfk, dfv, dff_], axis=1)
    dh0_in = matmul_nn(dproj, win_mine, transpose_rhs=True, name="d_in_proj")

    def ln0_bwd_fn(rows, vecs):
        dh0 = rows[0] + ALPHA * rows[1]
        dx, dg, db = _ln_bwd(rows[2], dh0, vecs[0])
        return [dx], [dg, db]
    (dx,), (g_ln0_g, g_ln0_b) = rowwise(ln0_bwd_fn, [dh0_in, dz1, x2], [vec(ln0_g)], [(d, F32)], [d, d],
                                        name="ln0_bwd")

    gfull = {
        "w_a": matmul_tn(y_a, dma, name="g_w_a"),
        "w_b": matmul_tn(y_b, dmb, name="g_w_b").reshape(fh, FOX_AUG, d)[:, :FOX_HDIM].reshape(hw, d),
        "w_o": matmul_tn(merged, dz1b, name="g_w_o"),
        "w_ff1": matmul_tn(h1b, du, name="g_w_ff1"),
        "w_ff2": matmul_tn(act, dz2b, name="g_w_ff2"),
        "w_pg": matmul_tn(h1b, dpg, name="g_w_pg"),
        "w_p": matmul_tn(p_b, dpe, name="g_w_p"),
    }

    def chip_parts(nm, s):
        g = gfull[nm]
        n = g.shape[axis[nm]] // 4
        return lax.slice_in_dim(g, s * n, (s + 1) * n, axis=axis[nm])
    me = 2 * lax.axis_index("x") + lax.axis_index("y")
    core = lax.axis_index("c")

    def sum2_fn(rows, vecs):
        s = rows[0] + rows[1].astype(F32)
        return [s, s], []

    def sum4_fn(rows, vecs):
        a, r0, r1, r2 = rows
        return [((a + r0.astype(F32)) + r1.astype(F32)) + r2.astype(F32)], []

    def chip_pair_sum(g, tag):
        h, cols = g.shape[1] // 2, g.shape[2]
        keep = lax.dynamic_slice_in_dim(g, core * h, h, axis=1)
        give = lax.dynamic_slice_in_dim(g, (1 - core) * h, h, axis=1).astype(BF16)
        (from_core,) = swap_cores([give], name="swap_partials_" + tag)
        (s32, s16), _ = rowwise(sum2_fn, [keep.reshape(4 * h, cols), from_core.reshape(4 * h, cols)], [],
                                [(cols, F32), (cols, BF16)], name="sum_cores_" + tag, tm=SUM_TILE)
        return s32.reshape(4, h, cols), s16.reshape(4, h, cols)

    def chip_sum(pr, gt, tag):
        own = lax.dynamic_index_in_dim(pr, me, axis=0, keepdims=False)
        (q,), _ = rowwise(sum4_fn, [own, gt[0], gt[1], gt[2]], [], [(own.shape[1], F32)], name="sum_chips_" + tag,
                          tm=SUM_TILE)
        return q

    grads_b = jnp.stack([pack_b({nm: chip_parts(nm, s) for nm in names if nm != "w_in"}, d) for s in range(4)])
    pair_rest, pair_rest_b = chip_pair_sum(grads_b, "rest")
    gw_in_mine, (got_rest,) = matmul_tn(h0b, dproj, name="g_w_in", rider=scatter_rider([pair_rest_b]))
    gfull["w_in"] = jnp.concatenate([gw_in_mine[:, mine[nm][0]:mine[nm][0] + orig[nm][1]]
                                     for nm in ["hq", "hf", "hi", "hg", "fq", "fk", "fv", "ff", "ga", "gb"]], axis=1)
    grads_a = jnp.stack([pack_a(chip_parts("w_in", s)) for s in range(4)])
    pair_in, pair_in_b = chip_pair_sum(grads_a, "w_in")
    (got_in,) = scatter_chips([pair_in_b], name="scatter_w_in")
    q_half = [chip_sum(pair_in, got_in, "w_in"), chip_sum(pair_rest, got_rest, "rest")]
    q_other = swap_cores(q_half, name="swap_halves")
    g_a, g_b = [jnp.concatenate([jnp.where(core == 0, mine_, other), jnp.where(core == 0, other, mine_)], axis=0)
                for mine_, other in zip(q_half, q_other)]
    g_shards = unpack_b(g_b, lay)
    g_shards["w_in"] = g_a[:, :in_cols]

    def row1024(*parts):
        r = jnp.concatenate([q.reshape(1, -1) for q in parts], axis=1)
        return jnp.concatenate([r, jnp.zeros((1, PACK_W - r.shape[1]), F32)], axis=1) if r.shape[1] < PACK_W else r
    small_rows = [row1024(g_ln0_g), row1024(g_ln0_b), row1024(g_ln1_g), row1024(g_ln1_b), row1024(g_ln2_g),
                  row1024(g_ln2_b), row1024(g_norm_g, g_lb), row1024(g_fb[:, :fh], loss_part[:, :1])]
    small = allreduce_small(jnp.concatenate(small_rows, axis=0), name="allreduce_small")
    s_ln0_g, s_ln0_b, s_ln1_g, s_ln1_b, s_ln2_g, s_ln2_b = [small[r:r + 1] for r in range(6)]
    s_norm_g, s_lb = small[6:7, :hw], small[6:7, hw:2 * hw]
    s_fb, loss = small[7:8, :fh], small[7, fh]
    p0 = probs[0:1]
    jac = p0 * (1.0 - p0)
    s_hg_lb = jnp.concatenate([s_lb * jac, -s_lb * jac], axis=0)

    small_w = [vec(ln0_g), vec(ln0_b), ln1_g, ln1_b, ln2_g, ln2_b, hg_lb.reshape(1, -1), hg_norm_g, fox_fb]
    small_g = [s_ln0_g, s_ln0_b, s_ln1_g, s_ln1_b, s_ln2_g, s_ln2_b, s_hg_lb.reshape(1, -1), s_norm_g, s_fb]
    small_m = [vec(m_ln0_g), vec(m_ln0_b), m_ln1_g, m_ln1_b, m_ln2_g, m_ln2_b, m_hg_lb.reshape(1, -1), m_hg_norm_g, m_fox_fb]
    small_v = [vec(v_ln0_g), vec(v_ln0_b), v_ln1_g, v_ln1_b, v_ln2_g, v_ln2_b, v_hg_lb.reshape(1, -1), v_hg_norm_g, v_fox_fb]
    pad_rows = lambda lst, fill: jnp.concatenate(
        [row1024(a) if fill == 0.0 else jnp.concatenate([a.reshape(1, -1), jnp.full((1, PACK_W - a.size), fill, F32)], axis=1)
         for a in lst] + [jnp.full((16 - len(lst), PACK_W), fill, F32)], axis=0)
    sd, sm, sv = adamw(pad_rows(small_w, 0.0), pad_rows(small_g, 0.0), pad_rows(small_m, 0.0), pad_rows(small_v, 1.0),
                       name="adamw_small")
    small_shapes = [ln0_g.shape, ln0_b.shape, ln1_g.shape, ln1_b.shape, ln2_g.shape, ln2_b.shape, hg_lb.shape,
                    hg_norm_g.shape, fox_fb.shape]
    take = lambda buf: [buf[r, :int(np.prod(shp))].reshape(shp) for r, shp in enumerate(small_shapes)]
    sg_out, sd_out, sm_out, sv_out = [g.reshape(shp) for g, shp in zip(small_g, small_shapes)], take(sd), take(sm), take(sv)

    big_out = {}
    for nm in names:
        delta, m2, v2 = adamw(big[nm], g_shards[nm], big_m[nm], big_v[nm], name="adamw_" + nm)
        big_out[nm] = (g_shards[nm][None], delta[None], m2[None], v2[None])

    def ordered(k):
        sm_ = [sg_out, sd_out, sm_out, sv_out][k]
        bg = lambda nm: big_out[nm][k]
        return [sm_[0], sm_[1], bg("w_in"), sm_[6], sm_[7], sm_[8], bg("w_a"), bg("w_b"), bg("w_o"), sm_[2], sm_[3],
                bg("w_ff1"), bg("w_ff2"), bg("w_pg"), bg("w_p"), sm_[4], sm_[5]]
    grad_x = dx.reshape(n_seq, seq, d)
    return (loss, grad_x, *ordered(0), *ordered(1), *ordered(2), *ordered(3))
```

```python
import functools
from typing import NamedTuple, Optional

import numpy as np
import jax
import jax.numpy as jnp
from jax import lax
from jax.experimental import pallas as pl
from jax.experimental.pallas import tpu as pltpu

F32 = jnp.float32
BF16 = jnp.bfloat16
MESH = pl.DeviceIdType.MESH

VMEM_LIMIT_BYTES = 48 * 1024 * 1024
LANES = 128
HG_HEADS = 4
HG_DIM = 128
HG_BLK = 16
HG_TILE = 256
FOX_HDIM = 64
FOX_AUG = 128
FOX_TQ = 1024
LN_EPS = 1e-5
RMS_EPS = 1e-6
DEPTH = 1
ALPHA = (2.0 * DEPTH) ** 0.25
ADAM_LR, ADAM_B1, ADAM_B2, ADAM_EPS, ADAM_WD, ADAM_STEP = 0.001, 0.9, 0.999, 1e-08, 0.01, 10
NEG_INF = -1e30


def _cparams(sem):
    return pltpu.CompilerParams(dimension_semantics=sem, vmem_limit_bytes=VMEM_LIMIT_BYTES)


def _tile(n, cap):
    if n <= cap:
        return n
    best = None
    for t in range(LANES, cap + 1, LANES):
        if n % t == 0:
            best = t
    assert best is not None, (n, cap)
    return best


class WView(NamedTuple):
    arr: jax.Array
    r0: int
    c0: int
    k: int
    n: int
    split: Optional[int]


def matmul_nn(a, w, *, name, transpose_rhs=False, out_dtype=F32, epilogue=None, aux=None, tm=1024, rider=None):
    wv = w if isinstance(w, WView) else WView(w[None], 0, 0, w.shape[0], w.shape[1], None)
    rows_s = wv.k // 4 if wv.split == 0 else wv.k
    cols_s = wv.n // 4 if wv.split == 1 else wv.n
    tr, tc = _tile(rows_s, 1152), _tile(cols_s, 1152)
    assert wv.r0 % tr == 0 and wv.c0 % tc == 0
    T, K = a.shape
    N, tn, tk = (wv.k, tr, tc) if transpose_rhs else (wv.n, tc, tr)
    assert K == (wv.n if transpose_rhs else wv.k)
    tm = min(tm, T)
    assert T % tm == 0
    nk = K // tk

    def w_block(ri, ci):
        if wv.split == 0:
            return (ri * tr) // rows_s, (wv.r0 + (ri * tr) % rows_s) // tr, wv.c0 // tc + ci
        if wv.split == 1:
            return (ci * tc) // cols_s, wv.r0 // tr + ri, (wv.c0 + (ci * tc) % cols_s) // tc
        return 0, wv.r0 // tr + ri, wv.c0 // tc + ci

    def body(*refs):
        if aux is None:
            a_ref, w_ref, o_ref, acc_ref = refs
            x_ref = None
        else:
            a_ref, w_ref, x_ref, o_ref, acc_ref = refs
        k = pl.program_id(2)
        if transpose_rhs:
            part = lax.dot_general(a_ref[...], w_ref[...], (((1,), (1,)), ((), ())), preferred_element_type=F32)
        else:
            part = jnp.dot(a_ref[...], w_ref[...], preferred_element_type=F32)

        def write(res):
            if epilogue is not None:
                res = epilogue(res) if x_ref is None else epilogue(res, x_ref[...])
            o_ref[...] = res.astype(out_dtype)

        if nk == 1:
            write(part)
        else:
            @pl.when(k == 0)
            def _():
                acc_ref[...] = part

            @pl.when(k > 0)
            def _():
                acc_ref[...] += part

            @pl.when(k == nk - 1)
            def _():
                write(acc_ref[...])

    w_index = (lambda n, m, k: w_block(n, k)) if transpose_rhs else (lambda n, m, k: w_block(k, n))
    in_specs = [pl.BlockSpec((tm, tk), lambda n, m, k: (m, k)),
                pl.BlockSpec((None, tr, tc), w_index)]
    args = [a, wv.arr]
    if aux is not None:
        in_specs.append(pl.BlockSpec((tm, tn), lambda n, m, k: (m, n)))
        args.append(aux)
    out_specs = [pl.BlockSpec((tm, tn), lambda n, m, k: (m, n))]
    out_shape = [jax.ShapeDtypeStruct((T, N), out_dtype)]
    scratch = [pltpu.VMEM((tm, tn) if nk > 1 else (8, LANES), F32)]
    grid = (N // tn, T // tm, nk)
    if rider is None:
        return pl.pallas_call(body, name=name, grid=grid, in_specs=in_specs, out_specs=out_specs, out_shape=out_shape,
                              scratch_shapes=scratch,
                              compiler_params=_cparams(("parallel", "parallel", "arbitrary")))(*args)[0]
    r_in, r_out, r_sems = rider.specs()
    res = pl.pallas_call(
        rider.wrap(body, len(in_specs), 1, 3), name=name, grid=grid, in_specs=in_specs + r_in,
        out_specs=out_specs + r_out, out_shape=out_shape + rider.out_shape, scratch_shapes=scratch + r_sems,
        compiler_params=pltpu.CompilerParams(dimension_semantics=("arbitrary",) * 3,
                                             vmem_limit_bytes=VMEM_LIMIT_BYTES, has_side_effects=True),
    )(*args, *rider.ins)
    return res[0], list(res[1:])


def matmul_tn(a, b, *, name, tk=1024, rider=None):
    T, M = a.shape
    T2, N = b.shape
    tk = min(tk, T)
    assert T == T2 and T % tk == 0
    tm = _tile(M, 1024)
    tn = _tile(N, 1152)

    def body(a_ref, b_ref, o_ref):
        k = pl.program_id(2)
        part = lax.dot_general(a_ref[...], b_ref[...], (((0,), (0,)), ((), ())), preferred_element_type=F32)

        @pl.when(k == 0)
        def _():
            o_ref[...] = part

        @pl.when(k > 0)
        def _():
            o_ref[...] += part

    in_specs = [pl.BlockSpec((tk, tm), lambda m, n, k: (k, m)), pl.BlockSpec((tk, tn), lambda m, n, k: (k, n))]
    out_specs = [pl.BlockSpec((tm, tn), lambda m, n, k: (m, n))]
    out_shape = [jax.ShapeDtypeStruct((M, N), F32)]
    grid = (M // tm, N // tn, T // tk)
    if rider is None:
        return pl.pallas_call(body, name=name, grid=grid, in_specs=in_specs, out_specs=out_specs, out_shape=out_shape,
                              compiler_params=_cparams(("parallel", "parallel", "arbitrary")))(a, b)[0]
    r_in, r_out, r_sems = rider.specs()
    res = pl.pallas_call(
        rider.wrap(body, 2, 1, 3), name=name, grid=grid, in_specs=in_specs + r_in, out_specs=out_specs + r_out,
        out_shape=out_shape + rider.out_shape, scratch_shapes=r_sems,
        compiler_params=pltpu.CompilerParams(dimension_semantics=("arbitrary",) * 3,
                                             vmem_limit_bytes=VMEM_LIMIT_BYTES, has_side_effects=True),
    )(a, b, *rider.ins)
    return res[0], list(res[1:])


def rowwise(fn, rows, vecs, outs, accs=(), *, name, tm=512):
    rows = [r if isinstance(r, tuple) else (r, 0, r.shape[1]) for r in rows]
    T = rows[0][0].shape[0]
    tm = min(tm, T)
    assert T % tm == 0
    n_rows, n_vecs, n_outs, n_accs = len(rows), len(vecs), len(outs), len(accs)

    def body(*refs):
        row_refs = refs[:n_rows]
        vec_refs = refs[n_rows:n_rows + n_vecs]
        out_refs = refs[n_rows + n_vecs:n_rows + n_vecs + n_outs]
        acc_refs = refs[n_rows + n_vecs + n_outs:]
        out_vals, acc_vals = fn([r[...] for r in row_refs], [v[...] for v in vec_refs])
        assert len(out_vals) == n_outs and len(acc_vals) == n_accs
        for r, val in zip(out_refs, out_vals):
            r[...] = val.astype(r.dtype)
        if n_accs:
            i = pl.program_id(0)

            @pl.when(i == 0)
            def _():
                for r in acc_refs:
                    r[...] = jnp.zeros_like(r)

            for r, val in zip(acc_refs, acc_vals):
                r[...] += val

    in_specs = []
    for arr, off, width in rows:
        assert off % width == 0
        in_specs.append(pl.BlockSpec((tm, width), functools.partial(lambda i, blk: (i, blk), blk=off // width)))
    for v in vecs:
        in_specs.append(pl.BlockSpec(v.shape, lambda i: (0, 0)))
    out_specs = [pl.BlockSpec((tm, w), lambda i: (i, 0)) for w, _ in outs]
    out_specs += [pl.BlockSpec((1, w), lambda i: (0, 0)) for w in accs]
    out_shape = [jax.ShapeDtypeStruct((T, w), dt) for w, dt in outs]
    out_shape += [jax.ShapeDtypeStruct((1, w), F32) for w in accs]
    res = pl.pallas_call(
        body, name=name,
        grid=(T // tm,),
        in_specs=in_specs, out_specs=out_specs, out_shape=out_shape,
        compiler_params=_cparams(("arbitrary",) if n_accs else ("parallel",)),
    )(*[r[0] for r in rows], *vecs)
    return res[:n_outs], res[n_outs:]


def _colsum(x):
    return jnp.sum(x, axis=0, keepdims=True)


def _sigmoid(x):
    return 1.0 / (1.0 + jnp.exp(-x))


def _ln_stats(z):
    mu = jnp.mean(z, axis=-1, keepdims=True)
    zc = z - mu
    var = jnp.mean(zc * zc, axis=-1, keepdims=True)
    return zc * lax.rsqrt(var + LN_EPS)


def _ln_bwd(zhat_src, dy, g):
    mu = jnp.mean(zhat_src, axis=-1, keepdims=True)
    zc = zhat_src - mu
    var = jnp.mean(zc * zc, axis=-1, keepdims=True)
    rstd = lax.rsqrt(var + LN_EPS)
    zh = zc * rstd
    dzh = dy * g
    dz = rstd * (dzh - jnp.mean(dzh, axis=-1, keepdims=True) - zh * jnp.mean(dzh * zh, axis=-1, keepdims=True))
    return dz, _colsum(dy * zh), _colsum(dy)


def _hg_constants():
    r = np.arange(HG_TILE)
    same = (r[:, None] // HG_BLK) == (r[None, :] // HG_BLK)
    lower = (same & (r[None, :] <= r[:, None])).astype(np.float32)
    upper = (same & (r[None, :] >= r[:, None])).astype(np.float32)
    total = same.astype(np.float32)
    w = HG_HEADS * HG_DIM
    c = np.arange(w)
    bd = ((c[:, None] // HG_DIM) == (c[None, :] // HG_DIM)).astype(np.float32)
    n = HG_BLK * HG_BLK
    rr = np.arange(n)
    sel_t = (rr[None, :] // HG_BLK == np.arange(HG_BLK)[:, None]).astype(np.float32)
    sel_s = (rr[None, :] % HG_BLK == np.arange(HG_BLK)[:, None]).astype(np.float32)
    as_bf = lambda m: jnp.asarray(m, dtype=BF16)
    return as_bf(lower), as_bf(upper), as_bf(total), as_bf(bd), as_bf(sel_t), as_bf(sel_s)


def _keep_bf16_bits(x):
    bits = lax.bitcast_convert_type(x, jnp.int32) & jnp.int32(-65536)
    return lax.bitcast_convert_type(bits, F32)


def _split3(x):
    hi = _keep_bf16_bits(x)
    r1 = x - hi
    mid = _keep_bf16_bits(r1)
    lo = _keep_bf16_bits(r1 - mid)
    return hi.astype(BF16), mid.astype(BF16), lo.astype(BF16)


def _dot3(m01, x):
    hi, mid, lo = _split3(x)
    d = lambda p: jnp.dot(m01, p, preferred_element_type=F32)
    return (d(lo) + d(mid)) + d(hi)


def _hg_prologue(hq, hf, lb, lower, total):
    sq = _sigmoid(hq)
    q = hq * sq
    sg = _sigmoid(hf)
    f = lb + (1.0 - lb) * sg
    g = jnp.log(f)
    k = 1.0 - f
    b = _dot3(lower, g)
    bl = _dot3(total, g)
    return q, k, f, sg, sq, b, bl


def _stack16(fn):
    return [fn(t) for t in range(HG_BLK)]


def hgrn2_fwd(proj, offs, lb, n_seq, seq, *, name, rider=None):
    T = n_seq * seq
    W = HG_HEADS * HG_DIM
    n_tiles = seq // HG_TILE
    nb = HG_TILE // HG_BLK
    lower, _, total, bd, sel_t, _ = _hg_constants()

    def body(hq_ref, hf_ref, hi_ref, lb_ref, lower_ref, total_ref, bd_ref, selt_ref,
             o_ref, st_out_ref,
             st_ref, q_s, k_s, v_s, b_s, qt_s, kt_s, d_s, p_s):
        @pl.when(pl.program_id(1) == 0)
        def _():
            st_ref[...] = jnp.zeros_like(st_ref)

        q, k, _, _, _, b, bl = _hg_prologue(hq_ref[...], hf_ref[...], lb_ref[...], lower_ref[...], total_ref[...])
        q_s[...] = q
        k_s[...] = k
        v_s[...] = hi_ref[...]
        b_s[...] = b
        qt_s[...] = q * jnp.exp(b)
        kt_s[...] = k * jnp.exp(jnp.minimum(bl - b, 0.0))
        d_s[...] = jnp.exp(bl)
        rowi = lax.broadcasted_iota(jnp.int32, (HG_BLK, W), 0)

        def block(i, carry):
            r0 = pl.multiple_of(i * HG_BLK, HG_BLK)
            rows = pl.ds(r0, HG_BLK)
            qi, ki, vi, bi = q_s[rows, :], k_s[rows, :], v_s[rows, :], b_s[rows, :]
            for t in range(HG_BLK):
                e = jnp.where(rowi <= t, jnp.exp(jnp.minimum(bi[t:t + 1, :] - bi, 0.0)), 0.0)
                p_s[pl.ds(t * HG_BLK, HG_BLK), :] = (e * qi[t:t + 1, :] * ki).astype(BF16)
            a_b = jnp.dot(p_s[...], bd_ref[...], preferred_element_type=F32)
            vt = jnp.concatenate([vi] * HG_BLK, axis=0)
            o_blk = jnp.dot(selt_ref[...], (a_b * vt).astype(BF16), preferred_element_type=F32)
            qti, kti, di = qt_s[rows, :], kt_s[rows, :], d_s[rows, :]
            outs = []
            for h in range(HG_HEADS):
                hs = slice(h * HG_DIM, (h + 1) * HG_DIM)
                st_h = st_ref[hs, :]
                st_out_ref[i, hs, :] = st_h
                outs.append(lax.dot_general(qti[:, hs].astype(BF16), st_h.astype(BF16),
                                            (((1,), (1,)), ((), ())), preferred_element_type=F32))
                upd = lax.dot_general(vi[:, hs].astype(BF16), kti[:, hs].astype(BF16),
                                      (((0,), (0,)), ((), ())), preferred_element_type=F32)
                st_ref[hs, :] = st_h * di[0:1, hs] + upd
            o_ref[rows, :] = o_blk + jnp.concatenate(outs, axis=1)
            return carry

        lax.fori_loop(0, nb, block, 0, unroll=2)

    col = lambda off: functools.partial(lambda s, t, blk: (s * n_tiles + t, blk), blk=off // W)
    const = lambda m: pl.BlockSpec(m.shape, lambda s, t: (0, 0))
    tile_f32 = pltpu.VMEM((HG_TILE, W), F32)
    in_specs = [pl.BlockSpec((HG_TILE, W), col(offs[0])), pl.BlockSpec((HG_TILE, W), col(offs[1])),
                pl.BlockSpec((HG_TILE, W), col(offs[2])), const(lb), const(lower), const(total), const(bd),
                const(sel_t)]
    out_specs = [pl.BlockSpec((HG_TILE, W), lambda s, t: (s * n_tiles + t, 0)),
                 pl.BlockSpec((nb, W, HG_DIM), lambda s, t: (s * n_tiles + t, 0, 0))]
    out_shape = [jax.ShapeDtypeStruct((T, W), F32), jax.ShapeDtypeStruct((T // HG_BLK, W, HG_DIM), F32)]
    scratch = [pltpu.VMEM((W, HG_DIM), F32)] + [tile_f32] * 7 + [pltpu.VMEM((HG_BLK * HG_BLK, W), BF16)]
    args = [proj, proj, proj, lb, lower, total, bd, sel_t]
    params = _cparams(("arbitrary", "arbitrary"))
    if rider is not None:
        r_in, r_out, r_sems = rider.specs()
        body = rider.wrap(body, len(in_specs), len(out_specs), 2)
        in_specs, out_specs, out_shape = in_specs + r_in, out_specs + r_out, out_shape + rider.out_shape
        scratch, args = scratch + r_sems, args + rider.ins
        params = pltpu.CompilerParams(dimension_semantics=("arbitrary", "arbitrary"),
                                      vmem_limit_bytes=VMEM_LIMIT_BYTES, has_side_effects=True)
    res = pl.pallas_call(body, name=name, grid=(n_seq, n_tiles), in_specs=in_specs, out_specs=out_specs,
                         out_shape=out_shape, scratch_shapes=scratch, compiler_params=params)(*args)
    return res[0], res[1], list(res[2:])


def hgrn2_bwd(proj, offs, lb, do, states, n_seq, seq, *, name):
    T = n_seq * seq
    W = HG_HEADS * HG_DIM
    n_tiles = seq // HG_TILE
    nb = HG_TILE // HG_BLK
    lower, upper, total, bd, sel_t, sel_s = _hg_constants()

    def body(hq_ref, hf_ref, hi_ref, do_ref, st_in_ref, lb_ref, lower_ref, upper_ref, total_ref, bd_ref,
             selt_ref, sels_ref,
             dhq_ref, dhf_ref, dhi_ref, dlb_ref,
             dst_ref, q_s, k_s, v_s, b_s, qt_s, kt_s, d_s, eb_s, ekb_s, dq_s, dk_s, db_s, dv_s,
             p_s, e_s, w_s):
        first = jnp.logical_and(pl.program_id(0) == 0, pl.program_id(1) == 0)

        @pl.when(first)
        def _():
            dlb_ref[...] = jnp.zeros_like(dlb_ref)

        @pl.when(pl.program_id(1) == 0)
        def _():
            dst_ref[...] = jnp.zeros_like(dst_ref)

        hq, lbv = hq_ref[...], lb_ref[...]
        q, k, f, sg, sq, b, bl = _hg_prologue(hq, hf_ref[...], lbv, lower_ref[...], total_ref[...])
        eb = jnp.exp(b)
        ekb = jnp.exp(jnp.minimum(bl - b, 0.0))
        q_s[...] = q
        k_s[...] = k
        v_s[...] = hi_ref[...]
        b_s[...] = b
        eb_s[...] = eb
        ekb_s[...] = ekb
        qt_s[...] = q * eb
        kt_s[...] = k * ekb
        d_s[...] = jnp.exp(bl)
        rowi = lax.broadcasted_iota(jnp.int32, (HG_BLK, W), 0)
        last_row = rowi == HG_BLK - 1

        def block(j, carry):
            i = nb - 1 - j
            r0 = pl.multiple_of(i * HG_BLK, HG_BLK)
            rows = pl.ds(r0, HG_BLK)
            qi, ki, vi, bi, doi = q_s[rows, :], k_s[rows, :], v_s[rows, :], b_s[rows, :], do_ref[rows, :]
            for t in range(HG_BLK):
                sl = pl.ds(t * HG_BLK, HG_BLK)
                e = jnp.where(rowi <= t, jnp.exp(jnp.minimum(bi[t:t + 1, :] - bi, 0.0)), 0.0)
                e_s[sl, :] = e
                p_s[sl, :] = (e * qi[t:t + 1, :] * ki).astype(BF16)
                w_s[sl, :] = (doi[t:t + 1, :] * vi).astype(BF16)
            a_b = jnp.dot(p_s[...], bd_ref[...], preferred_element_type=F32)
            da_b = jnp.dot(w_s[...], bd_ref[...], preferred_element_type=F32)
            x = da_b * e_s[...]
            k_til = jnp.concatenate([ki] * HG_BLK, axis=0)
            q_rep = jnp.concatenate([jnp.broadcast_to(qi[t:t + 1, :], (HG_BLK, W)) for t in range(HG_BLK)], axis=0)
            do_rep = jnp.concatenate([jnp.broadcast_to(doi[t:t + 1, :], (HG_BLK, W)) for t in range(HG_BLK)], axis=0)
            dq_in = jnp.dot(selt_ref[...], (x * k_til).astype(BF16), preferred_element_type=F32)
            dk_in = jnp.dot(sels_ref[...], (x * q_rep).astype(BF16), preferred_element_type=F32)
            dv_in = jnp.dot(sels_ref[...], (a_b * do_rep).astype(BF16), preferred_element_type=F32)
            qti, kti, di = qt_s[rows, :], kt_s[rows, :], d_s[rows, :]
            dqt, dkt, dvt, dd = [], [], [], []
            for h in range(HG_HEADS):
                hs = slice(h * HG_DIM, (h + 1) * HG_DIM)
                st_h = st_in_ref[i, hs, :]
                dst_h = dst_ref[hs, :]
                do_h, v_h = doi[:, hs].astype(BF16), vi[:, hs].astype(BF16)
                dst_b = dst_h.astype(BF16)
                dqt.append(jnp.dot(do_h, st_h.astype(BF16), preferred_element_type=F32))
                dkt.append(jnp.dot(v_h, dst_b, preferred_element_type=F32))
                dvt.append(lax.dot_general(kti[:, hs].astype(BF16), dst_b, (((1,), (1,)), ((), ())),
                                           preferred_element_type=F32))
                dd.append(jnp.sum(dst_h * st_h, axis=0, keepdims=True))
                upd = lax.dot_general(do_h, qti[:, hs].astype(BF16), (((0,), (0,)), ((), ())),
                                      preferred_element_type=F32)
                dst_ref[hs, :] = dst_h * di[0:1, hs] + upd
            dqt = jnp.concatenate(dqt, axis=1)
            dkt = jnp.concatenate(dkt, axis=1)
            dvt = jnp.concatenate(dvt, axis=1)
            dd = jnp.concatenate(dd, axis=1)
            dbl = jnp.sum(dkt * kti, axis=0, keepdims=True) + dd * di[0:1, :]
            db = qi * dq_in - ki * dk_in + dqt * qti - dkt * kti
            db_s[rows, :] = db + jnp.where(last_row, dbl, 0.0)
            dq_s[rows, :] = dq_in + dqt * eb_s[rows, :]
            dk_s[rows, :] = dk_in + dkt * ekb_s[rows, :]
            dv_s[rows, :] = dv_in + dvt
            return carry

        lax.fori_loop(0, nb, block, 0, unroll=2)

        dg = _dot3(upper_ref[...], db_s[...])
        dhq_ref[...] = (dq_s[...] * (sq * (1.0 + hq * (1.0 - sq)))).astype(dhq_ref.dtype)
        df = dg / f - dk_s[...]
        dhf_ref[...] = (df * (1.0 - lbv) * (sg * (1.0 - sg))).astype(dhf_ref.dtype)
        dhi_ref[...] = dv_s[...].astype(dhi_ref.dtype)
        dlb_ref[...] += _colsum(df * (1.0 - sg))

    rev = lambda s, t: s * n_tiles + (n_tiles - 1 - t)
    col = lambda off: functools.partial(lambda s, t, blk: (rev(s, t), blk), blk=off // W)
    const = lambda m: pl.BlockSpec(m.shape, lambda s, t: (0, 0))
    row = pl.BlockSpec((HG_TILE, W), lambda s, t: (rev(s, t), 0))
    tile_f32 = pltpu.VMEM((HG_TILE, W), F32)
    n2 = HG_BLK * HG_BLK
    return pl.pallas_call(
        body, name=name,
        grid=(n_seq, n_tiles),
        in_specs=[pl.BlockSpec((HG_TILE, W), col(offs[0])), pl.BlockSpec((HG_TILE, W), col(offs[1])),
                  pl.BlockSpec((HG_TILE, W), col(offs[2])), row,
                  pl.BlockSpec((nb, W, HG_DIM), lambda s, t: (rev(s, t), 0, 0)),
                  const(lb), const(lower), const(upper), const(total), const(bd), const(sel_t), const(sel_s)],
        out_specs=[row, row, row, pl.BlockSpec((1, W), lambda s, t: (0, 0))],
        out_shape=[jax.ShapeDtypeStruct((T, W), BF16)] * 3 + [jax.ShapeDtypeStruct((1, W), F32)],
        scratch_shapes=[pltpu.VMEM((W, HG_DIM), F32)] + [tile_f32] * 13
                       + [pltpu.VMEM((n2, W), BF16), pltpu.VMEM((n2, W), F32), pltpu.VMEM((n2, W), BF16)],
        compiler_params=_cparams(("arbitrary", "arbitrary")),
    )(proj, proj, proj, do, states, lb, lower, upper, total, bd, sel_t, sel_s)


def _diag_mask(tq):
    return lax.broadcasted_iota(jnp.int32, (tq, tq), 1) <= lax.broadcasted_iota(jnp.int32, (tq, tq), 0)


def _qk(q, k):
    return lax.dot_general(q, k, (((1,), (1,)), ((), ())), preferred_element_type=F32)


def _causal_pairs(n, sweeps=1, by_key=False):
    if by_key:
        rows = [(i, j, 0) for j in range(n) for i in range(j, n)]
    else:
        rows = [(i, j, s) for i in range(n) for s in range(sweeps) for j in range(i + 1)]
    return tuple(jnp.asarray(np.array([r[c] for r in rows], np.int32)) for c in range(3))


def _fox_placement(fh):
    hw, wa = fh * FOX_HDIM, fh * FOX_AUG
    pq, pk = np.zeros((hw, wa), np.float32), np.zeros((hw, wa), np.float32)
    aq, ak = np.zeros((3 * LANES, wa), np.float32), np.zeros((3 * LANES, wa), np.float32)
    oq, ok = np.zeros((1, wa), np.float32), np.zeros((1, wa), np.float32)
    for h in range(fh):
        src, dst = np.arange(h * FOX_HDIM, (h + 1) * FOX_HDIM), np.arange(h * FOX_AUG, h * FOX_AUG + FOX_HDIM)
        pq[src, dst] = FOX_HDIM ** -0.5
        pk[src, dst] = 1.0
        gate = h * FOX_AUG + FOX_HDIM
        for r in range(3):
            aq[r * LANES + h, gate + r] = 1.0
            ak[r * LANES + h, gate + 3 + r] = -1.0
        oq[0, gate + 3:gate + 6] = 1.0
        ok[0, gate:gate + 3] = 1.0
    bf = lambda m: jnp.asarray(m, dtype=BF16)
    return {"pq": bf(pq), "pk": bf(pk), "aq": bf(aq), "ak": bf(ak), "oq": jnp.asarray(oq), "ok": jnp.asarray(ok),
            "pqt": bf(pq.T), "pkt": bf(pk.T)}


def _fox_specs(tq, fh):
    def spec(tab):
        return pl.BlockSpec((None, tq, FOX_AUG), lambda b, t, *tabs: (b // fh, tabs[tab][t], b % fh))
    return spec(0), spec(1)


def fox_fwd(qa, ka, va, *, name):
    n_seq, S, width = qa.shape
    fh = width // FOX_AUG
    BH = n_seq * fh
    tq = min(FOX_TQ, S)
    itab, jtab, _ = _causal_pairs(S // tq)

    def body(itab_ref, jtab_ref, q_ref, k_ref, v_ref, o_ref, ox_ref, lse_ref, m_s, l_s, acc_s, acc_lo_s):
        t = pl.program_id(1)
        i, j = itab_ref[t], jtab_ref[t]

        @pl.when(j == 0)
        def _():
            m_s[...] = jnp.full_like(m_s, NEG_INF)
            l_s[...] = jnp.zeros_like(l_s)
            acc_s[...] = jnp.zeros_like(acc_s)
            acc_lo_s[...] = jnp.zeros_like(acc_lo_s)

        def step(on_diagonal):
            s = _qk(q_ref[...], k_ref[...])
            if on_diagonal:
                s = jnp.where(_diag_mask(tq), s, NEG_INF)
            m_prev = m_s[...]
            m_new = jnp.maximum(m_prev, jnp.max(s, axis=-1, keepdims=True))
            alpha = jnp.exp(m_prev - m_new)
            p = jnp.exp(s - m_new[:, 0:1])
            p_hi = p.astype(BF16)
            p_lo = (p - p_hi.astype(F32)).astype(BF16)
            v = v_ref[...]
            l_s[...] = alpha * l_s[...] + jnp.sum(p, axis=-1, keepdims=True)
            acc_s[...] = alpha * acc_s[...] + jnp.dot(p_hi, v, preferred_element_type=F32)
            acc_lo_s[...] = alpha * acc_lo_s[...] + jnp.dot(p_lo, v, preferred_element_type=F32)
            m_s[...] = m_new

        @pl.when(j < i)
        def _():
            step(False)

        @pl.when(j == i)
        def _():
            step(True)
            inv_l = 1.0 / l_s[...]
            o_ref[...] = (acc_s[...] * inv_l).astype(o_ref.dtype)
            ox_ref[...] = (acc_s[...] + acc_lo_s[...]) * inv_l
            lse_ref[...] = m_s[...] + jnp.log(l_s[...])

    qspec, kspec = _fox_specs(tq, fh)
    wide = jax.ShapeDtypeStruct((n_seq, S, width), F32)
    return pl.pallas_call(
        body, name=name,
        grid_spec=pltpu.PrefetchScalarGridSpec(
            num_scalar_prefetch=2, grid=(BH, itab.shape[0]),
            in_specs=[qspec, kspec, kspec],
            out_specs=[qspec, qspec, qspec],
            scratch_shapes=[pltpu.VMEM((tq, LANES), F32)] * 4),
        out_shape=[jax.ShapeDtypeStruct((n_seq, S, width), BF16), wide, wide],
        compiler_params=_cparams(("parallel", "arbitrary")),
    )(itab, jtab, qa, ka, va)


def _fox_ds(q, k, v, do, ox, lse, on_diagonal):
    s = _qk(q, k)
    if on_diagonal:
        s = jnp.where(_diag_mask(s.shape[0]), s, NEG_INF)
    p = jnp.exp(s - lse[:, 0:1])
    delta = jnp.sum(do.astype(F32) * ox, axis=-1, keepdims=True)
    return p, p * (_qk(do, v) - delta)


def fox_bwd(qa, ka, va, do, ox, lse, *, name):
    n_seq, S, width = qa.shape
    fh = width // FOX_AUG
    BH = n_seq * fh
    tq = min(FOX_TQ, S)
    itab, jtab, _ = _causal_pairs(S // tq)

    def body(itab_ref, jtab_ref, q_ref, k_ref, v_ref, do_ref, ox_ref, lse_ref, dq_ref, dk_ref, dv_ref, dsum_ref):
        t = pl.program_id(1)
        i, j = itab_ref[t], jtab_ref[t]

        @pl.when(t == 0)
        def _():
            dq_ref[...] = jnp.zeros_like(dq_ref)
            dk_ref[...] = jnp.zeros_like(dk_ref)
            dv_ref[...] = jnp.zeros_like(dv_ref)
            dsum_ref[...] = jnp.zeros_like(dsum_ref)

        q_rows = pl.ds(pl.multiple_of(i * tq, tq), tq)
        k_rows = pl.ds(pl.multiple_of(j * tq, tq), tq)

        def step(on_diagonal):
            q, k, do = q_ref[...], k_ref[...], do_ref[...]
            p, ds = _fox_ds(q, k, v_ref[...], do, ox_ref[...], lse_ref[...], on_diagonal)
            ds_b = ds.astype(BF16)
            tn = (((0,), (0,)), ((), ()))
            dq_ref[q_rows, :] += jnp.dot(ds_b, k, preferred_element_type=F32)
            dk_ref[k_rows, :] += lax.dot_general(ds_b, q, tn, preferred_element_type=F32)
            dv_ref[k_rows, :] += lax.dot_general(p.astype(BF16), do, tn, preferred_element_type=F32)
            dsum_ref[:, k_rows] += _colsum(ds)

        @pl.when(j < i)
        def _():
            step(False)

        @pl.when(j == i)
        def _():
            step(True)

    qspec, kspec = _fox_specs(tq, fh)
    whole = pl.BlockSpec((None, S, FOX_AUG), lambda b, t, it, jt: (b // fh, 0, b % fh))
    wide = jax.ShapeDtypeStruct((n_seq, S, width), F32)
    return pl.pallas_call(
        body, name=name,
        grid_spec=pltpu.PrefetchScalarGridSpec(
            num_scalar_prefetch=2, grid=(BH, itab.shape[0]),
            in_specs=[qspec, kspec, kspec, qspec, qspec, qspec],
            out_specs=[whole, whole, whole, pl.BlockSpec((None, 1, S), lambda b, t, it, jt: (b, 0, 0))]),
        out_shape=[wide, wide, wide, jax.ShapeDtypeStruct((BH, 1, S), F32)],
        compiler_params=_cparams(("parallel", "arbitrary")),
    )(itab, jtab, qa, ka, va, do, ox, lse)


def seq_cumsum(x, n_seq, seq, *, reverse, name):
    T, C = x.shape
    tb = min(256, seq)
    n = seq // tb
    r = np.arange(tb)
    tri = (r[None, :] >= r[:, None]) if reverse else (r[None, :] <= r[:, None])
    tri = jnp.asarray(tri.astype(np.float32), dtype=BF16)

    def body(x_ref, tri_ref, o_ref, carry_s):
        @pl.when(pl.program_id(1) == 0)
        def _():
            carry_s[...] = jnp.zeros_like(carry_s)

        xv = x_ref[...]
        o_ref[...] = _dot3(tri_ref[...], xv) + carry_s[...]
        carry_s[...] += _colsum(xv)

    blk = (lambda s, t: (s * n + (n - 1 - t), 0)) if reverse else (lambda s, t: (s * n + t, 0))
    return pl.pallas_call(
        body, name=name,
        grid=(n_seq, n),
        in_specs=[pl.BlockSpec((tb, C), blk), pl.BlockSpec((tb, tb), lambda s, t: (0, 0))],
        out_specs=pl.BlockSpec((tb, C), blk),
        out_shape=jax.ShapeDtypeStruct((T, C), F32),
        scratch_shapes=[pltpu.VMEM((1, C), F32)],
        compiler_params=_cparams(("arbitrary", "arbitrary")),
    )(x, tri)


def _place():
    return lax.axis_index("x"), lax.axis_index("y"), lax.axis_index("c")


def _other_chips(x, y):
    return [(1 - x, y), (x, 1 - y), (1 - x, 1 - y)]


def _hbm_call(body, ins, out_shape, n_sems, *, name):
    hbm = pl.BlockSpec(memory_space=pl.ANY)
    return pl.pallas_call(
        body, name=name,
        in_specs=[hbm] * len(ins), out_specs=[hbm] * len(out_shape), out_shape=out_shape,
        scratch_shapes=[pltpu.SemaphoreType.DMA((n_sems,)), pltpu.SemaphoreType.DMA((n_sems,)),
                        pltpu.SemaphoreType.DMA((len(ins),))],
        compiler_params=pltpu.CompilerParams(has_side_effects=True),
    )(*ins)


def allgather_chips(shards, *, name):
    return _exchange_call(allgather_rider(shards), name=name)


def _allgather_ops(x_refs, o_refs, send_sems, recv_sems, local_sems):
    def copies():
        x, y, c = _place()
        me = 2 * x + y
        chips = _other_chips(x, y)
        own, first, passed, landed, handed = [], [], [], [], []
        for b, (x_ref, o_ref) in enumerate(zip(x_refs, o_refs)):
            half = x_ref.shape[0] // 2
            mine, theirs = pl.ds(c * half, half), pl.ds((1 - c) * half, half)
            own.append(pltpu.make_async_copy(x_ref, o_ref.at[me], local_sems.at[b]))

            def copy(k, src, chip, rows, to, o_ref=o_ref, b=b):
                return pltpu.make_async_remote_copy(src_ref=src, dst_ref=o_ref.at[2 * chip[0] + chip[1], rows],
                                                    send_sem=send_sems.at[6 * b + k], recv_sem=recv_sems.at[6 * b + k],
                                                    device_id=to, device_id_type=MESH)
            for j, chip in enumerate(chips):
                first.append(copy(j, x_ref.at[mine], (x, y), mine, (*chip, c)))
                landed.append(copy(j, x_ref.at[mine], chip, mine, (*chip, c)))
                passed.append(copy(3 + j, o_ref.at[2 * chip[0] + chip[1], mine], chip, mine, (x, y, 1 - c)))
                handed.append(copy(3 + j, x_ref.at[mine], chip, theirs, (x, y, 1 - c)))
        return own, first, passed, landed, handed

    def start():
        own, first, _, _, _ = copies()
        for cp in own + first:
            cp.start()

    def finish():
        own, first, passed, landed, handed = copies()
        for arrived, forward in zip(landed, passed):
            arrived.wait_recv()
            forward.start()
        for cp in handed:
            cp.wait_recv()
        for cp in first + passed:
            cp.wait_send()
        for cp in own:
            cp.wait()
    return start, finish


def _scatter_ops(x_refs, o_refs, send_sems, recv_sems, local_sems):
    def copies():
        x, y, c = _place()
        return [pltpu.make_async_remote_copy(
            src_ref=x_ref.at[2 * px + py], dst_ref=o_ref.at[j], send_sem=send_sems.at[3 * b + j],
            recv_sem=recv_sems.at[3 * b + j], device_id=(px, py, c), device_id_type=MESH)
            for b, (x_ref, o_ref) in enumerate(zip(x_refs, o_refs)) for j, (px, py) in enumerate(_other_chips(x, y))]

    def start():
        for cp in copies():
            cp.start()

    def finish():
        sends = copies()
        for cp in sends:
            cp.wait_recv()
        for cp in sends:
            cp.wait_send()
    return start, finish


class Rider(NamedTuple):
    ins: list
    out_shape: list
    n_sems: int
    ops: object

    def specs(self):
        hbm = pl.BlockSpec(memory_space=pl.ANY)
        sems = [pltpu.SemaphoreType.DMA((self.n_sems,)), pltpu.SemaphoreType.DMA((self.n_sems,)),
                pltpu.SemaphoreType.DMA((len(self.ins),))]
        return [hbm] * len(self.ins), [hbm] * len(self.out_shape), sems

    def wrap(self, body, n_in, n_out, grid_rank):
        k_in, k_out = len(self.ins), len(self.out_shape)

        def carried(*refs):
            ins, r_ins = refs[:n_in], refs[n_in:n_in + k_in]
            outs = refs[n_in + k_in:n_in + k_in + n_out]
            r_outs = refs[n_in + k_in + n_out:n_in + k_in + n_out + k_out]
            scratch, sems = refs[n_in + k_in + n_out + k_out:-3], refs[-3:]
            first = functools.reduce(jnp.logical_and, [pl.program_id(a) == 0 for a in range(grid_rank)])
            last = functools.reduce(jnp.logical_and,
                                    [pl.program_id(a) == pl.num_programs(a) - 1 for a in range(grid_rank)])
            pl.when(first)(lambda: self.ops(r_ins, r_outs, *sems)[0]())
            body(*ins, *outs, *scratch)
            pl.when(last)(lambda: self.ops(r_ins, r_outs, *sems)[1]())
        return carried


def _exchange_call(rider, *, name):
    def body(*refs):
        k = len(rider.ins)
        start, finish = rider.ops(refs[:k], refs[k:k + len(rider.out_shape)], *refs[-3:])
        start()
        finish()
    in_specs, out_specs, sems = rider.specs()
    return pl.pallas_call(body, name=name, in_specs=in_specs, out_specs=out_specs, out_shape=rider.out_shape,
                          scratch_shapes=sems, compiler_params=pltpu.CompilerParams(has_side_effects=True))(*rider.ins)


def allgather_rider(shards):
    assert all(s.shape[0] % (2 * ROW_ALIGN) == 0 for s in shards)
    return Rider(list(shards), [jax.ShapeDtypeStruct((4,) + s.shape, s.dtype) for s in shards], 6 * len(shards),
                 _allgather_ops)


def scatter_rider(parts):
    return Rider(list(parts), [jax.ShapeDtypeStruct((3,) + p.shape[1:], p.dtype) for p in parts], 3 * len(parts),
                 _scatter_ops)


def scatter_chips(parts, *, name):
    return _exchange_call(scatter_rider(parts), name=name)


def swap_cores(vs, *, name):
    nb = len(vs)

    def body(*refs):
        x_refs, o_refs = refs[:nb], refs[nb:2 * nb]
        send_sems, recv_sems, _ = refs[2 * nb:]
        x, y, c = _place()
        copies = [pltpu.make_async_remote_copy(src_ref=x_ref, dst_ref=o_ref, send_sem=send_sems.at[b],
                                               recv_sem=recv_sems.at[b], device_id=(x, y, 1 - c), device_id_type=MESH)
                  for b, (x_ref, o_ref) in enumerate(zip(x_refs, o_refs))]
        for cp in copies:
            cp.start()
        for cp in copies:
            cp.wait()

    return _hbm_call(body, vs, [jax.ShapeDtypeStruct(v.shape, v.dtype) for v in vs], nb, name=name)


def allreduce_small(v, *, name):
    R, C = v.shape

    def body(x_ref, o_ref, gath_ref, send_sems, recv_sems):
        x, y, c = _place()
        me = 4 * x + 2 * y + c
        gath_ref[me] = x_ref[...]
        flips = [(k >> 2 & 1, k >> 1 & 1, k & 1) for k in range(1, 8)]
        sends = []
        for j, (fx, fy, fc) in enumerate(flips):
            peer = (x ^ fx, y ^ fy, c ^ fc)
            cp = pltpu.make_async_remote_copy(src_ref=x_ref, dst_ref=gath_ref.at[me], send_sem=send_sems.at[j],
                                              recv_sem=recv_sems.at[j], device_id=peer, device_id_type=MESH)
            cp.start()
            sends.append(cp)
        for j, (fx, fy, fc) in enumerate(flips):
            peer = (x ^ fx, y ^ fy, c ^ fc)
            pltpu.make_async_remote_copy(src_ref=x_ref, dst_ref=gath_ref.at[4 * peer[0] + 2 * peer[1] + peer[2]],
                                         send_sem=send_sems.at[j], recv_sem=recv_sems.at[j], device_id=peer,
                                         device_id_type=MESH).wait_recv()
        for cp in sends:
            cp.wait_send()
        total = gath_ref[0]
        for d in range(1, 8):
            total = total + gath_ref[d]
        o_ref[...] = total

    vm = pl.BlockSpec(memory_space=pltpu.VMEM)
    out, _ = pl.pallas_call(
        body, name=name,
        in_specs=[vm], out_specs=[vm, vm],
        out_shape=[jax.ShapeDtypeStruct((R, C), F32), jax.ShapeDtypeStruct((8, R, C), F32)],
        scratch_shapes=[pltpu.SemaphoreType.DMA((7,)), pltpu.SemaphoreType.DMA((7,))],
        compiler_params=pltpu.CompilerParams(has_side_effects=True),
    )(v)
    return out


ROW_ALIGN = 16
PACK_W = 1024
SUM_TILE = 512
BIG_WEIGHTS = (("w_in", 1), ("w_a", 1), ("w_b", 1), ("w_o", 0), ("w_ff1", 1), ("w_ff2", 0), ("w_pg", 0), ("w_p", 1))


def _b_layout(d, ple):
    hw, q = d // 2, d // 4
    small = 2 * d + 2 * q
    lay = {"w_ff1": (0, 0, d, d), "w_ff2": (d, 0, d, d), "w_o": (2 * d, 0, q, d), "w_pg": (2 * d + q, 0, q, d),
           "w_a": (small, 0, hw, q), "w_b": (small, q, hw, q), "w_p": (small, 2 * q, ple, q)}
    return lay, small + hw


def pack_a(w_in_shard):
    rows, cols = w_in_shard.shape
    pad = -cols % LANES
    return jnp.concatenate([w_in_shard, jnp.zeros((rows, pad), w_in_shard.dtype)], axis=1)


def pack_b(shards, d):
    hw, q = d // 2, d // 4
    dt = shards["w_a"].dtype
    wp = shards["w_p"]
    wp = jnp.concatenate([wp, jnp.zeros((hw - wp.shape[0], q), dt)], axis=0)
    small = jnp.concatenate([shards["w_a"], shards["w_b"], wp, jnp.zeros((hw, d - 3 * q), dt)], axis=1)
    return jnp.concatenate([shards["w_ff1"], shards["w_ff2"], shards["w_o"], shards["w_pg"], small], axis=0)


def unpack_b(buf, lay):
    return {nm: buf[r0:r0 + rows, c0:c0 + cols] for nm, (r0, c0, rows, cols) in lay.items()}


def _win_layout(d):
    hw = d // 2
    fh = hw // FOX_HDIM
    orig = {"hq": (0, hw), "hf": (hw, hw), "hi": (2 * hw, hw), "hg": (3 * hw, hw), "fq": (4 * hw, hw),
            "fk": (5 * hw, hw), "fv": (6 * hw, hw), "ff": (7 * hw, fh), "ga": (7 * hw + fh, d), "gb": (7 * hw + fh + d, d)}
    order = ["ga", "gb", "hq", "hf", "hi", "hg", "fq", "fk", "fv", "ff"]
    mine, off = {}, 0
    for nm in order:
        width = orig[nm][1] if nm != "ff" else LANES
        mine[nm] = (off, width)
        off += width
    return orig, order, mine, off


def _adam_fn(rows, vecs):
    w, g, m, v = rows
    m2 = ADAM_B1 * m + (1.0 - ADAM_B1) * g
    v2 = ADAM_B2 * v + (1.0 - ADAM_B2) * (g * g)
    m_hat = m2 / (1.0 - ADAM_B1 ** ADAM_STEP)
    v_hat = v2 / (1.0 - ADAM_B2 ** ADAM_STEP)
    delta = -ADAM_LR * (m_hat / (jnp.sqrt(v_hat) + ADAM_EPS) + ADAM_WD * w)
    return [delta, m2, v2], []


def adamw(w, g, m, v, *, name):
    c = w.shape[1]
    (delta, m2, v2), _ = rowwise(_adam_fn, [w, g, m, v], [], [(c, F32)] * 3, name=name, tm=256)
    return delta, m2, v2


def kernel(x, p, ln0_g, ln0_b, w_in, hg_lb, hg_norm_g, fox_fb, w_a, w_b, w_o, ln1_g, ln1_b, w_ff1, w_ff2, w_pg, w_p, ln2_g, ln2_b, loss_target, m_ln0_g, m_ln0_b, m_w_in, m_hg_lb, m_hg_norm_g, m_fox_fb, m_w_a, m_w_b, m_w_o, m_ln1_g, m_ln1_b, m_w_ff1, m_w_ff2, m_w_pg, m_w_p, m_ln2_g, m_ln2_b, v_ln0_g, v_ln0_b, v_w_in, v_hg_lb, v_hg_norm_g, v_fox_fb, v_w_a, v_w_b, v_w_o, v_ln1_g, v_ln1_b, v_w_ff1, v_w_ff2, v_w_pg, v_w_p, v_ln2_g, v_ln2_b):
    n_seq, seq, d = x.shape
    T = n_seq * seq
    hw = d // 2
    fh = hw // FOX_HDIM
    bh = n_seq * fh
    orig, order, mine, n_in = _win_layout(d)

    big = {"w_in": w_in[0], "w_a": w_a[0], "w_b": w_b[0], "w_o": w_o[0], "w_ff1": w_ff1[0], "w_ff2": w_ff2[0],
           "w_pg": w_pg[0], "w_p": w_p[0]}
    big_m = {"w_in": m_w_in[0], "w_a": m_w_a[0], "w_b": m_w_b[0], "w_o": m_w_o[0], "w_ff1": m_w_ff1[0],
             "w_ff2": m_w_ff2[0], "w_pg": m_w_pg[0], "w_p": m_w_p[0]}
    big_v = {"w_in": v_w_in[0], "w_a": v_w_a[0], "w_b": v_w_b[0], "w_o": v_w_o[0], "w_ff1": v_w_ff1[0],
             "w_ff2": v_w_ff2[0], "w_pg": v_w_pg[0], "w_p": v_w_p[0]}
    names = [nm for nm, _ in BIG_WEIGHTS]
    axis = dict(BIG_WEIGHTS)
    ple = w_p.shape[1]
    lay, b_rows = _b_layout(d, ple)
    in_cols = big["w_in"].shape[1]

    (a_all,) = allgather_chips([pack_a(big["w_in"].astype(BF16))], name="allgather_w_in")
    gather_rest = allgather_rider([pack_b({nm: big[nm].astype(BF16) for nm in names if nm != "w_in"}, d)])
    win = jnp.concatenate([a_all[s, :, :in_cols] for s in range(4)], axis=1)
    win_mine = jnp.concatenate(
        [win[:, orig[nm][0]:orig[nm][0] + orig[nm][1]] for nm in order]
        + [jnp.zeros((d, LANES - fh), BF16)], axis=1)

    x2 = x.reshape(T, d)
    tgt = loss_target.reshape(T, d)
    p_b = p.reshape(T, p.shape[-1]).astype(BF16)
    vec = lambda a: a.reshape(1, -1)
    probs = jax.nn.softmax(hg_lb, axis=0)
    lb = vec(probs[0])

    def ln0_fn(rows, vecs):
        h = _ln_stats(rows[0]) * vecs[0] + vecs[1]
        return [h, h], []
    (h0, h0b), _ = rowwise(ln0_fn, [x2], [vec(ln0_g), vec(ln0_b)], [(d, F32), (d, BF16)], name="ln0_fwd")
    proj = matmul_nn(h0b, win_mine, name="in_proj")

    o_raw, hg_states, (b_all,) = hgrn2_fwd(proj, [mine["hq"][0], mine["hf"][0], mine["hi"][0]], lb, n_seq, seq,
                                           name="hgrn2_fwd", rider=gather_rest)
    view = lambda nm, k, n: WView(b_all, lay[nm][0], lay[nm][1], k, n, axis[nm])
    w_ff1_v, w_ff2_v = view("w_ff1", d, 4 * d), view("w_ff2", 4 * d, d)

    def whole(nm):
        r0, c0, rows, cols = lay[nm]
        return jnp.concatenate([b_all[s, r0:r0 + rows, c0:c0 + cols] for s in range(4)], axis=axis[nm])
    w_o_v, w_pg_v, w_a_v, w_p_v, w_b_full = whole("w_o"), whole("w_pg"), whole("w_a"), whole("w_p"), whole("w_b")

    def ya_fn(rows, vecs):
        o, hg = rows
        outs = []
        for h in range(HG_HEADS):
            oh = o[:, h * HG_DIM:(h + 1) * HG_DIM]
            outs.append(oh * lax.rsqrt(jnp.mean(oh * oh, axis=-1, keepdims=True) + RMS_EPS))
        y = jnp.concatenate(outs, axis=1) * vecs[0] * (hg * _sigmoid(hg))
        return [y], []
    (y_a,), _ = rowwise(ya_fn, [o_raw, (proj,) + mine["hg"]], [hg_norm_g], [(hw, BF16)], name="hgrn2_out_fwd")

    fb_pad = jnp.concatenate([fox_fb, jnp.zeros((1, LANES - fh), F32)], axis=1)

    def lf_fn(rows, vecs):
        u = rows[0] + vecs[0]
        return [jnp.minimum(u, 0.0) - jnp.log(1.0 + jnp.exp(-jnp.abs(u)))], []
    (lf,), _ = rowwise(lf_fn, [(proj,) + mine["ff"]], [fb_pad], [(LANES, F32)], name="fox_logf")
    c_cum = seq_cumsum(lf, n_seq, seq, reverse=False, name="fox_cumsum")

    place = _fox_placement(fh)

    def prep_fn(rows, vecs):
        fq_, fk_, fv_, cc = rows
        pq, pk, aq, ak, oq, ok = vecs
        parts = jnp.concatenate(_split3(cc), axis=1)
        mm = lambda a_, b_: jnp.dot(a_, b_, preferred_element_type=F32)
        q_ = mm(fq_.astype(BF16), pq) + mm(parts, aq) + oq
        k_ = mm(fk_.astype(BF16), pk) + mm(parts, ak) + ok
        return [q_, k_, mm(fv_.astype(BF16), pk)], []
    wa = fh * FOX_AUG
    (qa, ka, va), _ = rowwise(prep_fn, [(proj,) + mine["fq"], (proj,) + mine["fk"], (proj,) + mine["fv"], c_cum],
                              [place[nm] for nm in ("pq", "pk", "aq", "ak", "oq", "ok")], [(wa, BF16)] * 3,
                              name="fox_prep")
    as_seq = lambda t2d: t2d.reshape(n_seq, seq, t2d.shape[1])
    o_fox, ox_fox, lse = fox_fwd(as_seq(qa), as_seq(ka), as_seq(va), name="fox_fwd")
    y_b = o_fox.reshape(T, wa)
    wb_pad = jnp.concatenate([w_b_full.reshape(fh, FOX_HDIM, d), jnp.zeros((fh, FOX_AUG - FOX_HDIM, d), BF16)],
                             axis=1).reshape(wa, d)

    pa = matmul_nn(y_a, w_a_v, name="proj_a")
    pb = matmul_nn(y_b, wb_pad, name="proj_b")

    def merge_fn(rows, vecs):
        ga, gb, a, b = rows
        return [_sigmoid(ga) * a + _sigmoid(gb) * b], []
    (merged,), _ = rowwise(merge_fn, [(proj,) + mine["ga"], (proj,) + mine["gb"], pa, pb], [], [(d, BF16)],
                           name="merge_fwd")
    mix = matmul_nn(merged, w_o_v, name="out_proj")

    def ln1_fn(rows, vecs):
        z = ALPHA * rows[0] + rows[1]
        h = _ln_stats(z) * vecs[0] + vecs[1]
        return [z, h, h], []
    (z1, h1, h1b), _ = rowwise(ln1_fn, [h0, mix], [ln1_g, ln1_b], [(d, F32), (d, F32), (d, BF16)], name="ln1_fwd")

    relu2 = lambda u: jnp.square(jnp.maximum(u, 0.0))
    act = matmul_nn(h1b, w_ff1_v, name="ff1", out_dtype=BF16, epilogue=relu2)
    ff = matmul_nn(act, w_ff2_v, name="ff2")
    pg = matmul_nn(h1b, w_pg_v, name="ple_gate")
    pe = matmul_nn(p_b, w_p_v, name="ple_embed")

    def head_fn(rows, vecs):
        h1v, ffv, pgv, pev, t = rows
        g2, b2 = vecs
        sp = _sigmoid(pgv)
        z = ALPHA * h1v + ffv + sp * pev
        y = _ln_stats(z) * g2 + b2
        err = y - t
        loss_rows = 0.5 * jnp.mean(err * err, axis=-1, keepdims=True)
        dy = err * (1.0 / d)
        dz, dg2, db2 = _ln_bwd(z, dy, g2)
        loss_acc = jnp.broadcast_to(_colsum(loss_rows), (1, LANES))
        return [dz, dz, dz * pev * (sp * (1.0 - sp)), dz * sp], [dg2, db2, loss_acc]
    (dz2, dz2b, dpg, dpe), (g_ln2_g, g_ln2_b, loss_part) = rowwise(
        head_fn, [h1, ff, pg, pe, tgt], [ln2_g, ln2_b],
        [(d, F32), (d, BF16), (d, BF16), (d, BF16)], [d, d, LANES], name="head_fwd_bwd")

    dact = lambda da, a: da * (2.0 * jnp.sqrt(a.astype(F32)))
    du = matmul_nn(dz2b, w_ff2_v, transpose_rhs=True, name="d_ff2", out_dtype=BF16, epilogue=dact, aux=act)
    dh1_ff = matmul_nn(du, w_ff1_v, transpose_rhs=True, name="d_ff1")
    dh1_pg = matmul_nn(dpg, w_pg_v, transpose_rhs=True, name="d_ple_gate")

    def ln1_bwd_fn(rows, vecs):
        dh1 = ALPHA * rows[0] + rows[1] + rows[2]
        dz, dg, db = _ln_bwd(rows[3], dh1, vecs[0])
        return [dz, dz], [dg, db]
    (dz1, dz1b), (g_ln1_g, g_ln1_b) = rowwise(ln1_bwd_fn, [dz2, dh1_ff, dh1_pg, z1], [ln1_g],
                                              [(d, F32), (d, BF16)], [d, d], name="ln1_bwd")
    dmerged = matmul_nn(dz1b, w_o_v, transpose_rhs=True, name="d_out_proj")

    def merge_bwd_fn(rows, vecs):
        dm, ga, gb, a, b = rows
        sa, sb = _sigmoid(ga), _sigmoid(gb)
        return [dm * a * (sa * (1.0 - sa)), dm * b * (sb * (1.0 - sb)), dm * sa, dm * sb], []
    (dga, dgb, dma, dmb), _ = rowwise(merge_bwd_fn, [dmerged, (proj,) + mine["ga"], (proj,) + mine["gb"], pa, pb], [],
                                      [(d, BF16)] * 4, name="merge_bwd")
    dya = matmul_nn(dma, w_a_v, transpose_rhs=True, name="d_proj_a")
    dyb = matmul_nn(dmb, wb_pad, transpose_rhs=True, name="d_proj_b", out_dtype=BF16)

    def ya_bwd_fn(rows, vecs):
        o, hg, dy = rows
        ng = vecs[0]
        sg = _sigmoid(hg)
        gate = hg * sg
        dn_parts, do_parts, n_parts = [], [], []
        for h in range(HG_HEADS):
            hs = slice(h * HG_DIM, (h + 1) * HG_DIM)
            oh = o[:, hs]
            r = lax.rsqrt(jnp.mean(oh * oh, axis=-1, keepdims=True) + RMS_EPS)
            nh = oh * r
            dn = dy[:, hs] * ng[:, hs] * gate[:, hs]
            do_parts.append(r * (dn - nh * jnp.mean(dn * nh, axis=-1, keepdims=True)))
            n_parts.append(nh)
        nrm = jnp.concatenate(n_parts, axis=1)
        dhg = dy * nrm * ng * (sg * (1.0 + hg * (1.0 - sg)))
        return [jnp.concatenate(do_parts, axis=1), dhg], [_colsum(dy * nrm * gate)]
    (do_raw, dhg), (g_norm_g,) = rowwise(ya_bwd_fn, [o_raw, (proj,) + mine["hg"], dya], [hg_norm_g],
                                         [(hw, F32), (hw, BF16)], [hw], name="hgrn2_out_bwd")
    dhq, dhf, dhi, g_lb = hgrn2_bwd(proj, [mine["hq"][0], mine["hf"][0], mine["hi"][0]], lb, do_raw, hg_states,
                                    n_seq, seq, name="hgrn2_bwd")

    do_fox = as_seq(dyb)
    dqa, dka, dva, dsum = fox_bwd(as_seq(qa), as_seq(ka), as_seq(va), do_fox, ox_fox, lse, name="fox_bwd")

    def unprep_fn(rows, vecs):
        mm = lambda a_, b_: jnp.dot(a_.astype(BF16), b_, preferred_element_type=F32)
        return [mm(rows[0], vecs[0]), mm(rows[1], vecs[1]), mm(rows[2], vecs[1])], []
    (dfq, dfk, dfv), _ = rowwise(unprep_fn, [dqa.reshape(T, wa), dka.reshape(T, wa), dva.reshape(T, wa)],
                                 [place["pqt"], place["pkt"]], [(hw, BF16)] * 3, name="fox_unprep")
    dc = -dsum.reshape(n_seq, fh, seq).transpose(0, 2, 1).reshape(T, fh)
    dc = jnp.concatenate([dc, jnp.zeros((T, LANES - fh), F32)], axis=1)
    dlf = seq_cumsum(dc, n_seq, seq, reverse=True, name="fox_cumsum_bwd")

    def lf_bwd_fn(rows, vecs):
        u = rows[0] + vecs[0]
        du_ = rows[1] * _sigmoid(-u)
        return [du_], [_colsum(du_)]
    (dff_,), (g_fb,) = rowwise(lf_bwd_fn, [(proj,) + mine["ff"], dlf], [fb_pad], [(LANES, BF16)], [LANES],
                               name="fox_logf_bwd")

    dproj = jnp.concatenate([dga, dgb, dhq, dhf, dhi, dhg, dfq, dfk, dfv, dff_], axis=1)

    gfull = {
        "w_a": matmul_tn(y_a, dma, name="g_w_a"),
        "w_b": matmul_tn(y_b, dmb, name="g_w_b").reshape(fh, FOX_AUG, d)[:, :FOX_HDIM].reshape(hw, d),
        "w_o": matmul_tn(merged, dz1b, name="g_w_o"),
        "w_ff1": matmul_tn(h1b, du, name="g_w_ff1"),
        "w_ff2": matmul_tn(act, dz2b, name="g_w_ff2"),
        "w_pg": matmul_tn(h1b, dpg, name="g_w_pg"),
        "w_p": matmul_tn(p_b, dpe, name="g_w_p"),
    }

    def chip_parts(nm, s):
        g = gfull[nm]
        n = g.shape[axis[nm]] // 4
        return lax.slice_in_dim(g, s * n, (s + 1) * n, axis=axis[nm])
    me = 2 * lax.axis_index("x") + lax.axis_index("y")
    core = lax.axis_index("c")

    def sum2_fn(rows, vecs):
        s = rows[0] + rows[1].astype(F32)
        return [s, s], []

    def sum4_fn(rows, vecs):
        a, r0, r1, r2 = rows
        return [((a + r0.astype(F32)) + r1.astype(F32)) + r2.astype(F32)], []

    def chip_pair_sum(g, tag):
        h, cols = g.shape[1] // 2, g.shape[2]
        keep = lax.dynamic_slice_in_dim(g, core * h, h, axis=1)
        give = lax.dynamic_slice_in_dim(g, (1 - core) * h, h, axis=1).astype(BF16)
        (from_core,) = swap_cores([give], name="swap_partials_" + tag)
        (s32, s16), _ = rowwise(sum2_fn, [keep.reshape(4 * h, cols), from_core.reshape(4 * h, cols)], [],
                                [(cols, F32), (cols, BF16)], name="sum_cores_" + tag, tm=SUM_TILE)
        return s32.reshape(4, h, cols), s16.reshape(4, h, cols)

    def chip_sum(pr, gt, tag):
        own = lax.dynamic_index_in_dim(pr, me, axis=0, keepdims=False)
        (q,), _ = rowwise(sum4_fn, [own, gt[0], gt[1], gt[2]], [], [(own.shape[1], F32)], name="sum_chips_" + tag,
                          tm=SUM_TILE)
        return q

    grads_b = jnp.stack([pack_b({nm: chip_parts(nm, s) for nm in names if nm != "w_in"}, d) for s in range(4)])
    pair_rest, pair_rest_b = chip_pair_sum(grads_b, "rest")
    gw_in_mine, (got_rest,) = matmul_tn(h0b, dproj, name="g_w_in", rider=scatter_rider([pair_rest_b]))
    gfull["w_in"] = jnp.concatenate([gw_in_mine[:, mine[nm][0]:mine[nm][0] + orig[nm][1]]
                                     for nm in ["hq", "hf", "hi", "hg", "fq", "fk", "fv", "ff", "ga", "gb"]], axis=1)
    grads_a = jnp.stack([pack_a(chip_parts("w_in", s)) for s in range(4)])
    pair_in, pair_in_b = chip_pair_sum(grads_a, "w_in")
    dh0_in, (got_in,) = matmul_nn(dproj, win_mine, transpose_rhs=True, name="d_in_proj",
                                  rider=scatter_rider([pair_in_b]))

    def ln0_bwd_fn(rows, vecs):
        dh0 = rows[0] + ALPHA * rows[1]
        dx, dg, db = _ln_bwd(rows[2], dh0, vecs[0])
        return [dx], [dg, db]
    (dx,), (g_ln0_g, g_ln0_b) = rowwise(ln0_bwd_fn, [dh0_in, dz1, x2], [vec(ln0_g)], [(d, F32)], [d, d],
                                        name="ln0_bwd")
    q_half = [chip_sum(pair_in, got_in, "w_in"), chip_sum(pair_rest, got_rest, "rest")]
    q_other = swap_cores(q_half, name="swap_halves")
    g_a, g_b = [jnp.concatenate([jnp.where(core == 0, mine_, other), jnp.where(core == 0, other, mine_)], axis=0)
                for mine_, other in zip(q_half, q_other)]
    g_shards = unpack_b(g_b, lay)
    g_shards["w_in"] = g_a[:, :in_cols]

    def row1024(*parts):
        r = jnp.concatenate([q.reshape(1, -1) for q in parts], axis=1)
        return jnp.concatenate([r, jnp.zeros((1, PACK_W - r.shape[1]), F32)], axis=1) if r.shape[1] < PACK_W else r
    small_rows = [row1024(g_ln0_g), row1024(g_ln0_b), row1024(g_ln1_g), row1024(g_ln1_b), row1024(g_ln2_g),
                  row1024(g_ln2_b), row1024(g_norm_g, g_lb), row1024(g_fb[:, :fh], loss_part[:, :1])]
    small = allreduce_small(jnp.concatenate(small_rows, axis=0), name="allreduce_small")
    s_ln0_g, s_ln0_b, s_ln1_g, s_ln1_b, s_ln2_g, s_ln2_b = [small[r:r + 1] for r in range(6)]
    s_norm_g, s_lb = small[6:7, :hw], small[6:7, hw:2 * hw]
    s_fb, loss = small[7:8, :fh], small[7, fh]
    p0 = probs[0:1]
    jac = p0 * (1.0 - p0)
    s_hg_lb = jnp.concatenate([s_lb * jac, -s_lb * jac], axis=0)

    small_w = [vec(ln0_g), vec(ln0_b), ln1_g, ln1_b, ln2_g, ln2_b, hg_lb.reshape(1, -1), hg_norm_g, fox_fb]
    small_g = [s_ln0_g, s_ln0_b, s_ln1_g, s_ln1_b, s_ln2_g, s_ln2_b, s_hg_lb.reshape(1, -1), s_norm_g, s_fb]
    small_m = [vec(m_ln0_g), vec(m_ln0_b), m_ln1_g, m_ln1_b, m_ln2_g, m_ln2_b, m_hg_lb.reshape(1, -1), m_hg_norm_g, m_fox_fb]
    small_v = [vec(v_ln0_g), vec(v_ln0_b), v_ln1_g, v_ln1_b, v_ln2_g, v_ln2_b, v_hg_lb.reshape(1, -1), v_hg_norm_g, v_fox_fb]
    pad_rows = lambda lst, fill: jnp.concatenate(
        [row1024(a) if fill == 0.0 else jnp.concatenate([a.reshape(1, -1), jnp.full((1, PACK_W - a.size), fill, F32)], axis=1)
         for a in lst] + [jnp.full((16 - len(lst), PACK_W), fill, F32)], axis=0)
    sd, sm, sv = adamw(pad_rows(small_w, 0.0), pad_rows(small_g, 0.0), pad_rows(small_m, 0.0), pad_rows(small_v, 1.0),
                       name="adamw_small")
    small_shapes = [ln0_g.shape, ln0_b.shape, ln1_g.shape, ln1_b.shape, ln2_g.shape, ln2_b.shape, hg_lb.shape,
                    hg_norm_g.shape, fox_fb.shape]
    take = lambda buf: [buf[r, :int(np.prod(shp))].reshape(shp) for r, shp in enumerate(small_shapes)]
    sg_out, sd_out, sm_out, sv_out = [g.reshape(shp) for g, shp in zip(small_g, small_shapes)], take(sd), take(sm), take(sv)

    big_out = {}
    for nm in names:
        delta, m2, v2 = adamw(big[nm], g_shards[nm], big_m[nm], big_v[nm], name="adamw_" + nm)
        big_out[nm] = (g_shards[nm][None], delta[None], m2[None], v2[None])

    def ordered(k):
        sm_ = [sg_out, sd_out, sm_out, sv_out][k]
        bg = lambda nm: big_out[nm][k]
        return [sm_[0], sm_[1], bg("w_in"), sm_[6], sm_[7], sm_[8], bg("w_a"), bg("w_b"), bg("w_o"), sm_[2], sm_[3],
                bg("w_ff1"), bg("w_ff2"), bg("w_pg"), bg("w_p"), sm_[4], sm_[5]]
    grad_x = dx.reshape(n_seq, seq, d)
    return (loss, grad_x, *ordered(0), *ordered(1), *ordered(2), *ordered(3))
```

```python
import functools
from typing import NamedTuple, Optional

import numpy as np
import jax
import jax.numpy as jnp
from jax import lax
from jax.experimental import pallas as pl
from jax.experimental.pallas import tpu as pltpu

F32 = jnp.float32
BF16 = jnp.bfloat16
MESH = pl.DeviceIdType.MESH

VMEM_LIMIT_BYTES = 48 * 1024 * 1024
LANES = 128
HG_HEADS = 4
HG_DIM = 128
HG_BLK = 16
HG_TILE = 256
HG_SLOTS = 4
FOX_HDIM = 64
FOX_AUG = 128
FOX_TQ = 1024
LN_EPS = 1e-5
RMS_EPS = 1e-6
DEPTH = 1
ALPHA = (2.0 * DEPTH) ** 0.25
ADAM_LR, ADAM_B1, ADAM_B2, ADAM_EPS, ADAM_WD, ADAM_STEP = 0.001, 0.9, 0.999, 1e-08, 0.01, 10
NEG_INF = -1e30


def _cparams(sem):
    return pltpu.CompilerParams(dimension_semantics=sem, vmem_limit_bytes=VMEM_LIMIT_BYTES)


def _tile(n, cap):
    if n <= cap:
        return n
    best = None
    for t in range(LANES, cap + 1, LANES):
        if n % t == 0:
            best = t
    assert best is not None, (n, cap)
    return best


class WView(NamedTuple):
    arr: jax.Array
    r0: int
    c0: int
    k: int
    n: int
    split: Optional[int]


def matmul_nn(a, w, *, name, transpose_rhs=False, out_dtype=F32, epilogue=None, aux=None, tm=1024, rider=None):
    wv = w if isinstance(w, WView) else WView(w[None], 0, 0, w.shape[0], w.shape[1], None)
    rows_s = wv.k // 4 if wv.split == 0 else wv.k
    cols_s = wv.n // 4 if wv.split == 1 else wv.n
    tr, tc = _tile(rows_s, 1152), _tile(cols_s, 1152)
    assert wv.r0 % tr == 0 and wv.c0 % tc == 0
    T, K = a.shape
    N, tn, tk = (wv.k, tr, tc) if transpose_rhs else (wv.n, tc, tr)
    assert K == (wv.n if transpose_rhs else wv.k)
    tm = min(tm, T)
    assert T % tm == 0
    nk = K // tk

    def w_block(ri, ci):
        if wv.split == 0:
            return (ri * tr) // rows_s, (wv.r0 + (ri * tr) % rows_s) // tr, wv.c0 // tc + ci
        if wv.split == 1:
            return (ci * tc) // cols_s, wv.r0 // tr + ri, (wv.c0 + (ci * tc) % cols_s) // tc
        return 0, wv.r0 // tr + ri, wv.c0 // tc + ci

    def body(*refs):
        if aux is None:
            a_ref, w_ref, o_ref, acc_ref = refs
            x_ref = None
        else:
            a_ref, w_ref, x_ref, o_ref, acc_ref = refs
        k = pl.program_id(2)
        if transpose_rhs:
            part = lax.dot_general(a_ref[...], w_ref[...], (((1,), (1,)), ((), ())), preferred_element_type=F32)
        else:
            part = jnp.dot(a_ref[...], w_ref[...], preferred_element_type=F32)

        def write(res):
            if epilogue is not None:
                res = epilogue(res) if x_ref is None else epilogue(res, x_ref[...])
            o_ref[...] = res.astype(out_dtype)

        if nk == 1:
            write(part)
        else:
            @pl.when(k == 0)
            def _():
                acc_ref[...] = part

            @pl.when(k > 0)
            def _():
                acc_ref[...] += part

            @pl.when(k == nk - 1)
            def _():
                write(acc_ref[...])

    w_index = (lambda n, m, k: w_block(n, k)) if transpose_rhs else (lambda n, m, k: w_block(k, n))
    in_specs = [pl.BlockSpec((tm, tk), lambda n, m, k: (m, k)),
                pl.BlockSpec((None, tr, tc), w_index)]
    args = [a, wv.arr]
    if aux is not None:
        in_specs.append(pl.BlockSpec((tm, tn), lambda n, m, k: (m, n)))
        args.append(aux)
    out_specs = [pl.BlockSpec((tm, tn), lambda n, m, k: (m, n))]
    out_shape = [jax.ShapeDtypeStruct((T, N), out_dtype)]
    scratch = [pltpu.VMEM((tm, tn) if nk > 1 else (8, LANES), F32)]
    grid = (N // tn, T // tm, nk)
    if rider is None:
        return pl.pallas_call(body, name=name, grid=grid, in_specs=in_specs, out_specs=out_specs, out_shape=out_shape,
                              scratch_shapes=scratch,
                              compiler_params=_cparams(("parallel", "parallel", "arbitrary")))(*args)[0]
    r_in, r_out, r_sems = rider.specs()
    res = pl.pallas_call(
        rider.wrap(body, len(in_specs), 1, 3), name=name, grid=grid, in_specs=in_specs + r_in,
        out_specs=out_specs + r_out, out_shape=out_shape + rider.out_shape, scratch_shapes=scratch + r_sems,
        compiler_params=pltpu.CompilerParams(dimension_semantics=("arbitrary",) * 3,
                                             vmem_limit_bytes=VMEM_LIMIT_BYTES, has_side_effects=True),
    )(*args, *rider.ins)
    return res[0], list(res[1:])


def matmul_tn(a, b, *, name, tk=1024, rider=None):
    T, M = a.shape
    T2, N = b.shape
    tk = min(tk, T)
    assert T == T2 and T % tk == 0
    tm = _tile(M, 1024)
    tn = _tile(N, 1152)

    def body(a_ref, b_ref, o_ref):
        k = pl.program_id(2)
        part = lax.dot_general(a_ref[...], b_ref[...], (((0,), (0,)), ((), ())), preferred_element_type=F32)

        @pl.when(k == 0)
        def _():
            o_ref[...] = part

        @pl.when(k > 0)
        def _():
            o_ref[...] += part

    in_specs = [pl.BlockSpec((tk, tm), lambda m, n, k: (k, m)), pl.BlockSpec((tk, tn), lambda m, n, k: (k, n))]
    out_specs = [pl.BlockSpec((tm, tn), lambda m, n, k: (m, n))]
    out_shape = [jax.ShapeDtypeStruct((M, N), F32)]
    grid = (M // tm, N // tn, T // tk)
    if rider is None:
        return pl.pallas_call(body, name=name, grid=grid, in_specs=in_specs, out_specs=out_specs, out_shape=out_shape,
                              compiler_params=_cparams(("parallel", "parallel", "arbitrary")))(a, b)[0]
    r_in, r_out, r_sems = rider.specs()
    res = pl.pallas_call(
        rider.wrap(body, 2, 1, 3), name=name, grid=grid, in_specs=in_specs + r_in, out_specs=out_specs + r_out,
        out_shape=out_shape + rider.out_shape, scratch_shapes=r_sems,
        compiler_params=pltpu.CompilerParams(dimension_semantics=("arbitrary",) * 3,
                                             vmem_limit_bytes=VMEM_LIMIT_BYTES, has_side_effects=True),
    )(a, b, *rider.ins)
    return res[0], list(res[1:])


def rowwise(fn, rows, vecs, outs, accs=(), *, name, tm=512):
    rows = [r if isinstance(r, tuple) else (r, 0, r.shape[1]) for r in rows]
    T = rows[0][0].shape[0]
    tm = min(tm, T)
    assert T % tm == 0
    n_rows, n_vecs, n_outs, n_accs = len(rows), len(vecs), len(outs), len(accs)

    def body(*refs):
        row_refs = refs[:n_rows]
        vec_refs = refs[n_rows:n_rows + n_vecs]
        out_refs = refs[n_rows + n_vecs:n_rows + n_vecs + n_outs]
        acc_refs = refs[n_rows + n_vecs + n_outs:]
        out_vals, acc_vals = fn([r[...] for r in row_refs], [v[...] for v in vec_refs])
        assert len(out_vals) == n_outs and len(acc_vals) == n_accs
        for r, val in zip(out_refs, out_vals):
            r[...] = val.astype(r.dtype)
        if n_accs:
            i = pl.program_id(0)

            @pl.when(i == 0)
            def _():
                for r in acc_refs:
                    r[...] = jnp.zeros_like(r)

            for r, val in zip(acc_refs, acc_vals):
                r[...] += val

    in_specs = []
    for arr, off, width in rows:
        assert off % width == 0
        in_specs.append(pl.BlockSpec((tm, width), functools.partial(lambda i, blk: (i, blk), blk=off // width)))
    for v in vecs:
        in_specs.append(pl.BlockSpec(v.shape, lambda i: (0, 0)))
    out_specs = [pl.BlockSpec((tm, w), lambda i: (i, 0)) for w, _ in outs]
    out_specs += [pl.BlockSpec((1, w), lambda i: (0, 0)) for w in accs]
    out_shape = [jax.ShapeDtypeStruct((T, w), dt) for w, dt in outs]
    out_shape += [jax.ShapeDtypeStruct((1, w), F32) for w in accs]
    res = pl.pallas_call(
        body, name=name,
        grid=(T // tm,),
        in_specs=in_specs, out_specs=out_specs, out_shape=out_shape,
        compiler_params=_cparams(("arbitrary",) if n_accs else ("parallel",)),
    )(*[r[0] for r in rows], *vecs)
    return res[:n_outs], res[n_outs:]


def _colsum(x):
    return jnp.sum(x, axis=0, keepdims=True)


def _sigmoid(x):
    return 1.0 / (1.0 + jnp.exp(-x))


def _ln_stats(z):
    mu = jnp.mean(z, axis=-1, keepdims=True)
    zc = z - mu
    var = jnp.mean(zc * zc, axis=-1, keepdims=True)
    return zc * lax.rsqrt(var + LN_EPS)


def _ln_bwd(zhat_src, dy, g):
    mu = jnp.mean(zhat_src, axis=-1, keepdims=True)
    zc = zhat_src - mu
    var = jnp.mean(zc * zc, axis=-1, keepdims=True)
    rstd = lax.rsqrt(var + LN_EPS)
    zh = zc * rstd
    dzh = dy * g
    dz = rstd * (dzh - jnp.mean(dzh, axis=-1, keepdims=True) - zh * jnp.mean(dzh * zh, axis=-1, keepdims=True))
    return dz, _colsum(dy * zh), _colsum(dy)


def _hg_constants():
    r = np.arange(HG_TILE)
    same = (r[:, None] // HG_BLK) == (r[None, :] // HG_BLK)
    lower = (same & (r[None, :] <= r[:, None])).astype(np.float32)
    upper = (same & (r[None, :] >= r[:, None])).astype(np.float32)
    total = same.astype(np.float32)
    c = np.arange(2 * HG_DIM)
    bd = ((c[:, None] // HG_DIM) == (c[None, :] // HG_DIM)).astype(np.float32)
    n = HG_BLK * HG_BLK
    rr = np.arange(n)
    sel_t = (rr[None, :] // HG_BLK == np.arange(HG_BLK)[:, None]).astype(np.float32)
    sel_s = (rr[None, :] % HG_BLK == np.arange(HG_BLK)[:, None]).astype(np.float32)
    as_bf = lambda m: jnp.asarray(m, dtype=BF16)
    return as_bf(lower), as_bf(upper), as_bf(total), as_bf(bd), as_bf(sel_t), as_bf(sel_s)


def _keep_bf16_bits(x):
    bits = lax.bitcast_convert_type(x, jnp.int32) & jnp.int32(-65536)
    return lax.bitcast_convert_type(bits, F32)


def _head_sums(stack_ref, slot, bd):
    pair = bd.shape[0]
    return jnp.concatenate([jnp.dot(stack_ref[slot, :, c0:c0 + pair], bd, preferred_element_type=F32)
                            for c0 in range(0, stack_ref.shape[2], pair)], axis=1)


def _split3(x):
    hi = _keep_bf16_bits(x)
    r1 = x - hi
    mid = _keep_bf16_bits(r1)
    lo = _keep_bf16_bits(r1 - mid)
    return hi.astype(BF16), mid.astype(BF16), lo.astype(BF16)


def _dot3(m01, x):
    hi, mid, lo = _split3(x)
    d = lambda p: jnp.dot(m01, p, preferred_element_type=F32)
    return (d(lo) + d(mid)) + d(hi)


def _hg_prologue(hq, hf, lb, lower, total):
    sq = _sigmoid(hq)
    q = hq * sq
    sg = _sigmoid(hf)
    f = lb + (1.0 - lb) * sg
    g = jnp.log(f)
    k = 1.0 - f
    b = _dot3(lower, g)
    bl = _dot3(total, g)
    return q, k, f, sg, sq, b, bl


def _stack16(fn):
    return [fn(t) for t in range(HG_BLK)]


def hgrn2_fwd(proj, offs, lb, n_seq, seq, *, name, rider=None):
    T = n_seq * seq
    W = HG_HEADS * HG_DIM
    n_tiles = seq // HG_TILE
    nb = HG_TILE // HG_BLK
    lower, _, total, bd, sel_t, _ = _hg_constants()

    def body(hq_ref, hf_ref, hi_ref, lb_ref, lower_ref, total_ref, bd_ref, selt_ref,
             o_ref, st_out_ref,
             st_ref, q_s, k_s, v_s, b_s, qt_s, kt_s, d_s, p_s):
        @pl.when(pl.program_id(1) == 0)
        def _():
            st_ref[...] = jnp.zeros_like(st_ref)

        q, k, _, _, _, b, bl = _hg_prologue(hq_ref[...], hf_ref[...], lb_ref[...], lower_ref[...], total_ref[...])
        q_s[...] = q
        k_s[...] = k
        v_s[...] = hi_ref[...]
        b_s[...] = b
        qt_s[...] = q * jnp.exp(b)
        kt_s[...] = k * jnp.exp(jnp.minimum(bl - b, 0.0))
        d_s[...] = jnp.exp(bl)
        rowi = lax.broadcasted_iota(jnp.int32, (HG_BLK, W), 0)

        def block(i, slot):
            r0 = pl.multiple_of(i * HG_BLK, HG_BLK)
            rows = pl.ds(r0, HG_BLK)
            qi, ki, vi, bi = q_s[rows, :], k_s[rows, :], v_s[rows, :], b_s[rows, :]
            for t in range(HG_BLK):
                e = jnp.where(rowi <= t, jnp.exp(jnp.minimum(bi[t:t + 1, :] - bi, 0.0)), 0.0)
                p_s[slot, pl.ds(t * HG_BLK, HG_BLK), :] = (e * qi[t:t + 1, :] * ki).astype(BF16)
            a_b = _head_sums(p_s, slot, bd_ref[...])
            vt = jnp.concatenate([vi] * HG_BLK, axis=0)
            o_blk = jnp.dot(selt_ref[...], (a_b * vt).astype(BF16), preferred_element_type=F32)
            qti, kti, di = qt_s[rows, :], kt_s[rows, :], d_s[rows, :]
            outs = []
            for h in range(HG_HEADS):
                hs = slice(h * HG_DIM, (h + 1) * HG_DIM)
                st_h = st_ref[hs, :]
                st_out_ref[i, hs, :] = st_h
                outs.append(lax.dot_general(qti[:, hs].astype(BF16), st_h.astype(BF16),
                                            (((1,), (1,)), ((), ())), preferred_element_type=F32))
                upd = lax.dot_general(vi[:, hs].astype(BF16), kti[:, hs].astype(BF16),
                                      (((0,), (0,)), ((), ())), preferred_element_type=F32)
                st_ref[hs, :] = st_h * di[0:1, hs] + upd
            o_ref[rows, :] = o_blk + jnp.concatenate(outs, axis=1)

        def some_blocks(jj, carry):
            for slot in range(HG_SLOTS):
                block(HG_SLOTS * jj + slot, slot)
            return carry

        lax.fori_loop(0, nb // HG_SLOTS, some_blocks, 0)

    col = lambda off: functools.partial(lambda s, t, blk: (s * n_tiles + t, blk), blk=off // W)
    const = lambda m: pl.BlockSpec(m.shape, lambda s, t: (0, 0))
    tile_f32 = pltpu.VMEM((HG_TILE, W), F32)
    in_specs = [pl.BlockSpec((HG_TILE, W), col(offs[0])), pl.BlockSpec((HG_TILE, W), col(offs[1])),
                pl.BlockSpec((HG_TILE, W), col(offs[2])), const(lb), const(lower), const(total), const(bd),
                const(sel_t)]
    out_specs = [pl.BlockSpec((HG_TILE, W), lambda s, t: (s * n_tiles + t, 0)),
                 pl.BlockSpec((nb, W, HG_DIM), lambda s, t: (s * n_tiles + t, 0, 0))]
    out_shape = [jax.ShapeDtypeStruct((T, W), F32), jax.ShapeDtypeStruct((T // HG_BLK, W, HG_DIM), F32)]
    scratch = [pltpu.VMEM((W, HG_DIM), F32)] + [tile_f32] * 7 + [pltpu.VMEM((HG_SLOTS, HG_BLK * HG_BLK, W), BF16)]
    args = [proj, proj, proj, lb, lower, total, bd, sel_t]
    params = _cparams(("arbitrary", "arbitrary"))
    if rider is not None:
        r_in, r_out, r_sems = rider.specs()
        body = rider.wrap(body, len(in_specs), len(out_specs), 2)
        in_specs, out_specs, out_shape = in_specs + r_in, out_specs + r_out, out_shape + rider.out_shape
        scratch, args = scratch + r_sems, args + rider.ins
        params = pltpu.CompilerParams(dimension_semantics=("arbitrary", "arbitrary"),
                                      vmem_limit_bytes=VMEM_LIMIT_BYTES, has_side_effects=True)
    res = pl.pallas_call(body, name=name, grid=(n_seq, n_tiles), in_specs=in_specs, out_specs=out_specs,
                         out_shape=out_shape, scratch_shapes=scratch, compiler_params=params)(*args)
    return res[0], res[1], list(res[2:])


def hgrn2_bwd(proj, offs, lb, do, states, n_seq, seq, *, name):
    T = n_seq * seq
    W = HG_HEADS * HG_DIM
    n_tiles = seq // HG_TILE
    nb = HG_TILE // HG_BLK
    lower, upper, total, bd, sel_t, sel_s = _hg_constants()

    def body(hq_ref, hf_ref, hi_ref, do_ref, st_in_ref, lb_ref, lower_ref, upper_ref, total_ref, bd_ref,
             selt_ref, sels_ref,
             dhq_ref, dhf_ref, dhi_ref, dlb_ref,
             dst_ref, q_s, k_s, v_s, b_s, qt_s, kt_s, d_s, eb_s, ekb_s, dq_s, dk_s, db_s, dv_s,
             p_s, e_s, w_s):
        first = jnp.logical_and(pl.program_id(0) == 0, pl.program_id(1) == 0)

        @pl.when(first)
        def _():
            dlb_ref[...] = jnp.zeros_like(dlb_ref)

        @pl.when(pl.program_id(1) == 0)
        def _():
            dst_ref[...] = jnp.zeros_like(dst_ref)

        hq, lbv = hq_ref[...], lb_ref[...]
        q, k, f, sg, sq, b, bl = _hg_prologue(hq, hf_ref[...], lbv, lower_ref[...], total_ref[...])
        eb = jnp.exp(b)
        ekb = jnp.exp(jnp.minimum(bl - b, 0.0))
        q_s[...] = q
        k_s[...] = k
        v_s[...] = hi_ref[...]
        b_s[...] = b
        eb_s[...] = eb
        ekb_s[...] = ekb
        qt_s[...] = q * eb
        kt_s[...] = k * ekb
        d_s[...] = jnp.exp(bl)
        rowi = lax.broadcasted_iota(jnp.int32, (HG_BLK, W), 0)
        last_row = rowi == HG_BLK - 1

        def block(i, slot):
            r0 = pl.multiple_of(i * HG_BLK, HG_BLK)
            rows = pl.ds(r0, HG_BLK)
            qi, ki, vi, bi, doi = q_s[rows, :], k_s[rows, :], v_s[rows, :], b_s[rows, :], do_ref[rows, :]
            for t in range(HG_BLK):
                sl = pl.ds(t * HG_BLK, HG_BLK)
                e = jnp.where(rowi <= t, jnp.exp(jnp.minimum(bi[t:t + 1, :] - bi, 0.0)), 0.0)
                e_s[slot, sl, :] = e
                p_s[slot, sl, :] = (e * qi[t:t + 1, :] * ki).astype(BF16)
                w_s[slot, sl, :] = (doi[t:t + 1, :] * vi).astype(BF16)
            a_b = _head_sums(p_s, slot, bd_ref[...])
            da_b = _head_sums(w_s, slot, bd_ref[...])
            x = da_b * e_s[slot]
            k_til = jnp.concatenate([ki] * HG_BLK, axis=0)
            q_rep = jnp.concatenate([jnp.broadcast_to(qi[t:t + 1, :], (HG_BLK, W)) for t in range(HG_BLK)], axis=0)
            do_rep = jnp.concatenate([jnp.broadcast_to(doi[t:t + 1, :], (HG_BLK, W)) for t in range(HG_BLK)], axis=0)
            dq_in = jnp.dot(selt_ref[...], (x * k_til).astype(BF16), preferred_element_type=F32)
            dk_in = jnp.dot(sels_ref[...], (x * q_rep).astype(BF16), preferred_element_type=F32)
            dv_in = jnp.dot(sels_ref[...], (a_b * do_rep).astype(BF16), preferred_element_type=F32)
            qti, kti, di = qt_s[rows, :], kt_s[rows, :], d_s[rows, :]
            dqt, dkt, dvt, dd = [], [], [], []
            for h in range(HG_HEADS):
                hs = slice(h * HG_DIM, (h + 1) * HG_DIM)
                st_h = st_in_ref[i, hs, :]
                dst_h = dst_ref[hs, :]
                do_h, v_h = doi[:, hs].astype(BF16), vi[:, hs].astype(BF16)
                dst_b = dst_h.astype(BF16)
                dqt.append(jnp.dot(do_h, st_h.astype(BF16), preferred_element_type=F32))
                dkt.append(jnp.dot(v_h, dst_b, preferred_element_type=F32))
                dvt.append(lax.dot_general(kti[:, hs].astype(BF16), dst_b, (((1,), (1,)), ((), ())),
                                           preferred_element_type=F32))
                dd.append(jnp.sum(dst_h * st_h, axis=0, keepdims=True))
                upd = lax.dot_general(do_h, qti[:, hs].astype(BF16), (((0,), (0,)), ((), ())),
                                      preferred_element_type=F32)
                dst_ref[hs, :] = dst_h * di[0:1, hs] + upd
            dqt = jnp.concatenate(dqt, axis=1)
            dkt = jnp.concatenate(dkt, axis=1)
            dvt = jnp.concatenate(dvt, axis=1)
            dd = jnp.concatenate(dd, axis=1)
            dbl = jnp.sum(dkt * kti, axis=0, keepdims=True) + dd * di[0:1, :]
            db = qi * dq_in - ki * dk_in + dqt * qti - dkt * kti
            db_s[rows, :] = db + jnp.where(last_row, dbl, 0.0)
            dq_s[rows, :] = dq_in + dqt * eb_s[rows, :]
            dk_s[rows, :] = dk_in + dkt * ekb_s[rows, :]
            dv_s[rows, :] = dv_in + dvt

        def some_blocks(jj, carry):
            for slot in range(HG_SLOTS):
                block(nb - 1 - slot - HG_SLOTS * jj, slot)
            return carry

        lax.fori_loop(0, nb // HG_SLOTS, some_blocks, 0)

        dg = _dot3(upper_ref[...], db_s[...])
        dhq_ref[...] = (dq_s[...] * (sq * (1.0 + hq * (1.0 - sq)))).astype(dhq_ref.dtype)
        df = dg / f - dk_s[...]
        dhf_ref[...] = (df * (1.0 - lbv) * (sg * (1.0 - sg))).astype(dhf_ref.dtype)
        dhi_ref[...] = dv_s[...].astype(dhi_ref.dtype)
        dlb_ref[...] += _colsum(df * (1.0 - sg))

    rev = lambda s, t: s * n_tiles + (n_tiles - 1 - t)
    col = lambda off: functools.partial(lambda s, t, blk: (rev(s, t), blk), blk=off // W)
    const = lambda m: pl.BlockSpec(m.shape, lambda s, t: (0, 0))
    row = pl.BlockSpec((HG_TILE, W), lambda s, t: (rev(s, t), 0))
    tile_f32 = pltpu.VMEM((HG_TILE, W), F32)
    n2 = HG_BLK * HG_BLK
    return pl.pallas_call(
        body, name=name,
        grid=(n_seq, n_tiles),
        in_specs=[pl.BlockSpec((HG_TILE, W), col(offs[0])), pl.BlockSpec((HG_TILE, W), col(offs[1])),
                  pl.BlockSpec((HG_TILE, W), col(offs[2])), row,
                  pl.BlockSpec((nb, W, HG_DIM), lambda s, t: (rev(s, t), 0, 0)),
                  const(lb), const(lower), const(upper), const(total), const(bd), const(sel_t), const(sel_s)],
        out_specs=[row, row, row, pl.BlockSpec((1, W), lambda s, t: (0, 0))],
        out_shape=[jax.ShapeDtypeStruct((T, W), BF16)] * 3 + [jax.ShapeDtypeStruct((1, W), F32)],
        scratch_shapes=[pltpu.VMEM((W, HG_DIM), F32)] + [tile_f32] * 13
                       + [pltpu.VMEM((HG_SLOTS, n2, W), BF16), pltpu.VMEM((HG_SLOTS, n2, W), F32),
                          pltpu.VMEM((HG_SLOTS, n2, W), BF16)],
        compiler_params=_cparams(("arbitrary", "arbitrary")),
    )(proj, proj, proj, do, states, lb, lower, upper, total, bd, sel_t, sel_s)


def _diag_mask(tq):
    return lax.broadcasted_iota(jnp.int32, (tq, tq), 1) <= lax.broadcasted_iota(jnp.int32, (tq, tq), 0)


def _qk(q, k):
    return lax.dot_general(q, k, (((1,), (1,)), ((), ())), preferred_element_type=F32)


def _causal_pairs(n, sweeps=1, by_key=False):
    if by_key:
        rows = [(i, j, 0) for j in range(n) for i in range(j, n)]
    else:
        rows = [(i, j, s) for i in range(n) for s in range(sweeps) for j in range(i + 1)]
    return tuple(jnp.asarray(np.array([r[c] for r in rows], np.int32)) for c in range(3))


def _fox_placement(fh):
    hw, wa = fh * FOX_HDIM, fh * FOX_AUG
    pq, pk = np.zeros((hw, wa), np.float32), np.zeros((hw, wa), np.float32)
    aq, ak = np.zeros((3 * LANES, wa), np.float32), np.zeros((3 * LANES, wa), np.float32)
    oq, ok = np.zeros((1, wa), np.float32), np.zeros((1, wa), np.float32)
    for h in range(fh):
        src, dst = np.arange(h * FOX_HDIM, (h + 1) * FOX_HDIM), np.arange(h * FOX_AUG, h * FOX_AUG + FOX_HDIM)
        pq[src, dst] = FOX_HDIM ** -0.5
        pk[src, dst] = 1.0
        gate = h * FOX_AUG + FOX_HDIM
        for r in range(3):
            aq[r * LANES + h, gate + r] = 1.0
            ak[r * LANES + h, gate + 3 + r] = -1.0
        oq[0, gate + 3:gate + 6] = 1.0
        ok[0, gate:gate + 3] = 1.0
    bf = lambda m: jnp.asarray(m, dtype=BF16)
    return {"pq": bf(pq), "pk": bf(pk), "aq": bf(aq), "ak": bf(ak), "oq": jnp.asarray(oq), "ok": jnp.asarray(ok),
            "pqt": bf(pq.T), "pkt": bf(pk.T)}


def _fox_specs(tq, fh):
    def spec(tab):
        return pl.BlockSpec((None, tq, FOX_AUG), lambda b, t, *tabs: (b // fh, tabs[tab][t], b % fh))
    return spec(0), spec(1)


def fox_fwd(qa, ka, va, *, name):
    n_seq, S, width = qa.shape
    fh = width // FOX_AUG
    BH = n_seq * fh
    tq = min(FOX_TQ, S)
    itab, jtab, _ = _causal_pairs(S // tq)

    def body(itab_ref, jtab_ref, q_ref, k_ref, v_ref, o_ref, ox_ref, lse_ref, m_s, l_s, acc_s, acc_lo_s):
        t = pl.program_id(1)
        i, j = itab_ref[t], jtab_ref[t]

        @pl.when(j == 0)
        def _():
            m_s[...] = jnp.full_like(m_s, NEG_INF)
            l_s[...] = jnp.zeros_like(l_s)
            acc_s[...] = jnp.zeros_like(acc_s)
            acc_lo_s[...] = jnp.zeros_like(acc_lo_s)

        def step(on_diagonal):
            s = _qk(q_ref[...], k_ref[...])
            if on_diagonal:
                s = jnp.where(_diag_mask(tq), s, NEG_INF)
            m_prev = m_s[...]
            m_new = jnp.maximum(m_prev, jnp.max(s, axis=-1, keepdims=True))
            alpha = jnp.exp(m_prev - m_new)
            p = jnp.exp(s - m_new[:, 0:1])
            p_hi = p.astype(BF16)
            p_lo = (p - p_hi.astype(F32)).astype(BF16)
            v = v_ref[...]
            l_s[...] = alpha * l_s[...] + jnp.sum(p, axis=-1, keepdims=True)
            acc_s[...] = alpha * acc_s[...] + jnp.dot(p_hi, v, preferred_element_type=F32)
            acc_lo_s[...] = alpha * acc_lo_s[...] + jnp.dot(p_lo, v, preferred_element_type=F32)
            m_s[...] = m_new

        @pl.when(j < i)
        def _():
            step(False)

        @pl.when(j == i)
        def _():
            step(True)
            inv_l = 1.0 / l_s[...]
            o_ref[...] = (acc_s[...] * inv_l).astype(o_ref.dtype)
            ox_ref[...] = (acc_s[...] + acc_lo_s[...]) * inv_l
            lse_ref[...] = m_s[...] + jnp.log(l_s[...])

    qspec, kspec = _fox_specs(tq, fh)
    wide = jax.ShapeDtypeStruct((n_seq, S, width), F32)
    return pl.pallas_call(
        body, name=name,
        grid_spec=pltpu.PrefetchScalarGridSpec(
            num_scalar_prefetch=2, grid=(BH, itab.shape[0]),
            in_specs=[qspec, kspec, kspec],
            out_specs=[qspec, qspec, qspec],
            scratch_shapes=[pltpu.VMEM((tq, LANES), F32)] * 4),
        out_shape=[jax.ShapeDtypeStruct((n_seq, S, width), BF16), wide, wide],
        compiler_params=_cparams(("parallel", "arbitrary")),
    )(itab, jtab, qa, ka, va)


def _fox_ds(q, k, v, do, ox, lse, on_diagonal):
    s = _qk(q, k)
    if on_diagonal:
        s = jnp.where(_diag_mask(s.shape[0]), s, NEG_INF)
    p = jnp.exp(s - lse[:, 0:1])
    delta = jnp.sum(do.astype(F32) * ox, axis=-1, keepdims=True)
    return p, p * (_qk(do, v) - delta)


def fox_bwd(qa, ka, va, do, ox, lse, *, name):
    n_seq, S, width = qa.shape
    fh = width // FOX_AUG
    BH = n_seq * fh
    tq = min(FOX_TQ, S)
    itab, jtab, _ = _causal_pairs(S // tq)

    def body(itab_ref, jtab_ref, q_ref, k_ref, v_ref, do_ref, ox_ref, lse_ref, dq_ref, dk_ref, dv_ref, dsum_ref):
        t = pl.program_id(1)
        i, j = itab_ref[t], jtab_ref[t]

        @pl.when(t == 0)
        def _():
            dq_ref[...] = jnp.zeros_like(dq_ref)
            dk_ref[...] = jnp.zeros_like(dk_ref)
            dv_ref[...] = jnp.zeros_like(dv_ref)
            dsum_ref[...] = jnp.zeros_like(dsum_ref)

        q_rows = pl.ds(pl.multiple_of(i * tq, tq), tq)
        k_rows = pl.ds(pl.multiple_of(j * tq, tq), tq)

        def step(on_diagonal):
            q, k, do = q_ref[...], k_ref[...], do_ref[...]
            p, ds = _fox_ds(q, k, v_ref[...], do, ox_ref[...], lse_ref[...], on_diagonal)
            ds_b = ds.astype(BF16)
            tn = (((0,), (0,)), ((), ()))
            dq_ref[q_rows, :] += jnp.dot(ds_b, k, preferred_element_type=F32)
            dk_ref[k_rows, :] += lax.dot_general(ds_b, q, tn, preferred_element_type=F32)
            dv_ref[k_rows, :] += lax.dot_general(p.astype(BF16), do, tn, preferred_element_type=F32)
            dsum_ref[:, k_rows] += _colsum(ds)

        @pl.when(j < i)
        def _():
            step(False)

        @pl.when(j == i)
        def _():
            step(True)

    qspec, kspec = _fox_specs(tq, fh)
    whole = pl.BlockSpec((None, S, FOX_AUG), lambda b, t, it, jt: (b // fh, 0, b % fh))
    wide = jax.ShapeDtypeStruct((n_seq, S, width), F32)
    return pl.pallas_call(
        body, name=name,
        grid_spec=pltpu.PrefetchScalarGridSpec(
            num_scalar_prefetch=2, grid=(BH, itab.shape[0]),
            in_specs=[qspec, kspec, kspec, qspec, qspec, qspec],
            out_specs=[whole, whole, whole, pl.BlockSpec((None, 1, S), lambda b, t, it, jt: (b, 0, 0))]),
        out_shape=[wide, wide, wide, jax.ShapeDtypeStruct((BH, 1, S), F32)],
        compiler_params=_cparams(("parallel", "arbitrary")),
    )(itab, jtab, qa, ka, va, do, ox, lse)


def seq_cumsum(x, n_seq, seq, *, reverse, name):
    T, C = x.shape
    tb = min(256, seq)
    n = seq // tb
    r = np.arange(tb)
    tri = (r[None, :] >= r[:, None]) if reverse else (r[None, :] <= r[:, None])
    tri = jnp.asarray(tri.astype(np.float32), dtype=BF16)

    def body(x_ref, tri_ref, o_ref, carry_s):
        @pl.when(pl.program_id(1) == 0)
        def _():
            carry_s[...] = jnp.zeros_like(carry_s)

        xv = x_ref[...]
        o_ref[...] = _dot3(tri_ref[...], xv) + carry_s[...]
        carry_s[...] += _colsum(xv)

    blk = (lambda s, t: (s * n + (n - 1 - t), 0)) if reverse else (lambda s, t: (s * n + t, 0))
    return pl.pallas_call(
        body, name=name,
        grid=(n_seq, n),
        in_specs=[pl.BlockSpec((tb, C), blk), pl.BlockSpec((tb, tb), lambda s, t: (0, 0))],
        out_specs=pl.BlockSpec((tb, C), blk),
        out_shape=jax.ShapeDtypeStruct((T, C), F32),
        scratch_shapes=[pltpu.VMEM((1, C), F32)],
        compiler_params=_cparams(("arbitrary", "arbitrary")),
    )(x, tri)


def _place():
    return lax.axis_index("x"), lax.axis_index("y"), lax.axis_index("c")


def _other_chips(x, y):
    return [(1 - x, y), (x, 1 - y), (1 - x, 1 - y)]


def _hbm_call(body, ins, out_shape, n_sems, *, name):
    hbm = pl.BlockSpec(memory_space=pl.ANY)
    return pl.pallas_call(
        body, name=name,
        in_specs=[hbm] * len(ins), out_specs=[hbm] * len(out_shape), out_shape=out_shape,
        scratch_shapes=[pltpu.SemaphoreType.DMA((n_sems,)), pltpu.SemaphoreType.DMA((n_sems,)),
                        pltpu.SemaphoreType.DMA((len(ins),))],
        compiler_params=pltpu.CompilerParams(has_side_effects=True),
    )(*ins)


def allgather_chips(shards, *, name):
    return _exchange_call(allgather_rider(shards), name=name)


def _allgather_ops(x_refs, o_refs, send_sems, recv_sems, local_sems):
    def copies():
        x, y, c = _place()
        me = 2 * x + y
        chips = _other_chips(x, y)
        own, first, passed, landed, handed = [], [], [], [], []
        for b, (x_ref, o_ref) in enumerate(zip(x_refs, o_refs)):
            half = x_ref.shape[0] // 2
            mine, theirs = pl.ds(c * half, half), pl.ds((1 - c) * half, half)
            own.append(pltpu.make_async_copy(x_ref, o_ref.at[me], local_sems.at[b]))

            def copy(k, src, chip, rows, to, o_ref=o_ref, b=b):
                return pltpu.make_async_remote_copy(src_ref=src, dst_ref=o_ref.at[2 * chip[0] + chip[1], rows],
                                                    send_sem=send_sems.at[6 * b + k], recv_sem=recv_sems.at[6 * b + k],
                                                    device_id=to, device_id_type=MESH)
            for j, chip in enumerate(chips):
                first.append(copy(j, x_ref.at[mine], (x, y), mine, (*chip, c)))
                landed.append(copy(j, x_ref.at[mine], chip, mine, (*chip, c)))
                passed.append(copy(3 + j, o_ref.at[2 * chip[0] + chip[1], mine], chip, mine, (x, y, 1 - c)))
                handed.append(copy(3 + j, x_ref.at[mine], chip, theirs, (x, y, 1 - c)))
        return own, first, passed, landed, handed

    def start():
        own, first, _, _, _ = copies()
        for cp in own + first:
            cp.start()

    def finish():
        own, first, passed, landed, handed = copies()
        for arrived, forward in zip(landed, passed):
            arrived.wait_recv()
            forward.start()
        for cp in handed:
            cp.wait_recv()
        for cp in first + passed:
            cp.wait_send()
        for cp in own:
            cp.wait()
    return start, finish


def _scatter_ops(x_refs, o_refs, send_sems, recv_sems, local_sems):
    def copies():
        x, y, c = _place()
        return [pltpu.make_async_remote_copy(
            src_ref=x_ref.at[2 * px + py], dst_ref=o_ref.at[j], send_sem=send_sems.at[3 * b + j],
            recv_sem=recv_sems.at[3 * b + j], device_id=(px, py, c), device_id_type=MESH)
            for b, (x_ref, o_ref) in enumerate(zip(x_refs, o_refs)) for j, (px, py) in enumerate(_other_chips(x, y))]

    def start():
        for cp in copies():
            cp.start()

    def finish():
        sends = copies()
        for cp in sends:
            cp.wait_recv()
        for cp in sends:
            cp.wait_send()
    return start, finish


class Rider(NamedTuple):
    ins: list
    out_shape: list
    n_sems: int
    ops: object

    def specs(self):
        hbm = pl.BlockSpec(memory_space=pl.ANY)
        sems = [pltpu.SemaphoreType.DMA((self.n_sems,)), pltpu.SemaphoreType.DMA((self.n_sems,)),
                pltpu.SemaphoreType.DMA((len(self.ins),))]
        return [hbm] * len(self.ins), [hbm] * len(self.out_shape), sems

    def wrap(self, body, n_in, n_out, grid_rank):
        k_in, k_out = len(self.ins), len(self.out_shape)

        def carried(*refs):
            ins, r_ins = refs[:n_in], refs[n_in:n_in + k_in]
            outs = refs[n_in + k_in:n_in + k_in + n_out]
            r_outs = refs[n_in + k_in + n_out:n_in + k_in + n_out + k_out]
            scratch, sems = refs[n_in + k_in + n_out + k_out:-3], refs[-3:]
            first = functools.reduce(jnp.logical_and, [pl.program_id(a) == 0 for a in range(grid_rank)])
            last = functools.reduce(jnp.logical_and,
                                    [pl.program_id(a) == pl.num_programs(a) - 1 for a in range(grid_rank)])
            pl.when(first)(lambda: self.ops(r_ins, r_outs, *sems)[0]())
            body(*ins, *outs, *scratch)
            pl.when(last)(lambda: self.ops(r_ins, r_outs, *sems)[1]())
        return carried


def _exchange_call(rider, *, name):
    def body(*refs):
        k = len(rider.ins)
        start, finish = rider.ops(refs[:k], refs[k:k + len(rider.out_shape)], *refs[-3:])
        start()
        finish()
    in_specs, out_specs, sems = rider.specs()
    return pl.pallas_call(body, name=name, in_specs=in_specs, out_specs=out_specs, out_shape=rider.out_shape,
                          scratch_shapes=sems, compiler_params=pltpu.CompilerParams(has_side_effects=True))(*rider.ins)


def allgather_rider(shards):
    assert all(s.shape[0] % (2 * ROW_ALIGN) == 0 for s in shards)
    return Rider(list(shards), [jax.ShapeDtypeStruct((4,) + s.shape, s.dtype) for s in shards], 6 * len(shards),
                 _allgather_ops)


def scatter_rider(parts):
    return Rider(list(parts), [jax.ShapeDtypeStruct((3,) + p.shape[1:], p.dtype) for p in parts], 3 * len(parts),
                 _scatter_ops)


def scatter_chips(parts, *, name):
    return _exchange_call(scatter_rider(parts), name=name)


def swap_cores(vs, *, name):
    nb = len(vs)

    def body(*refs):
        x_refs, o_refs = refs[:nb], refs[nb:2 * nb]
        send_sems, recv_sems, _ = refs[2 * nb:]
        x, y, c = _place()
        copies = [pltpu.make_async_remote_copy(src_ref=x_ref, dst_ref=o_ref, send_sem=send_sems.at[b],
                                               recv_sem=recv_sems.at[b], device_id=(x, y, 1 - c), device_id_type=MESH)
                  for b, (x_ref, o_ref) in enumerate(zip(x_refs, o_refs))]
        for cp in copies:
            cp.start()
        for cp in copies:
            cp.wait()

    return _hbm_call(body, vs, [jax.ShapeDtypeStruct(v.shape, v.dtype) for v in vs], nb, name=name)


def allreduce_small(v, *, name):
    R, C = v.shape

    def body(x_ref, o_ref, gath_ref, send_sems, recv_sems):
        x, y, c = _place()
        me = 4 * x + 2 * y + c
        gath_ref[me] = x_ref[...]
        flips = [(k >> 2 & 1, k >> 1 & 1, k & 1) for k in range(1, 8)]
        sends = []
        for j, (fx, fy, fc) in enumerate(flips):
            peer = (x ^ fx, y ^ fy, c ^ fc)
            cp = pltpu.make_async_remote_copy(src_ref=x_ref, dst_ref=gath_ref.at[me], send_sem=send_sems.at[j],
                                              recv_sem=recv_sems.at[j], device_id=peer, device_id_type=MESH)
            cp.start()
            sends.append(cp)
        for j, (fx, fy, fc) in enumerate(flips):
            peer = (x ^ fx, y ^ fy, c ^ fc)
            pltpu.make_async_remote_copy(src_ref=x_ref, dst_ref=gath_ref.at[4 * peer[0] + 2 * peer[1] + peer[2]],
                                         send_sem=send_sems.at[j], recv_sem=recv_sems.at[j], device_id=peer,
                                         device_id_type=MESH).wait_recv()
        for cp in sends:
            cp.wait_send()
        total = gath_ref[0]
        for d in range(1, 8):
            total = total + gath_ref[d]
        o_ref[...] = total

    vm = pl.BlockSpec(memory_space=pltpu.VMEM)
    out, _ = pl.pallas_call(
        body, name=name,
        in_specs=[vm], out_specs=[vm, vm],
        out_shape=[jax.ShapeDtypeStruct((R, C), F32), jax.ShapeDtypeStruct((8, R, C), F32)],
        scratch_shapes=[pltpu.SemaphoreType.DMA((7,)), pltpu.SemaphoreType.DMA((7,))],
        compiler_params=pltpu.CompilerParams(has_side_effects=True),
    )(v)
    return out


ROW_ALIGN = 16
PACK_W = 1024
SUM_TILE = 512
BIG_WEIGHTS = (("w_in", 1), ("w_a", 1), ("w_b", 1), ("w_o", 0), ("w_ff1", 1), ("w_ff2", 0), ("w_pg", 0), ("w_p", 1))


def _b_layout(d, ple):
    hw, q = d // 2, d // 4
    small = 2 * d + 2 * q
    lay = {"w_ff1": (0, 0, d, d), "w_ff2": (d, 0, d, d), "w_o": (2 * d, 0, q, d), "w_pg": (2 * d + q, 0, q, d),
           "w_a": (small, 0, hw, q), "w_b": (small, q, hw, q), "w_p": (small, 2 * q, ple, q)}
    return lay, small + hw


def pack_a(w_in_shard):
    rows, cols = w_in_shard.shape
    pad = -cols % LANES
    return jnp.concatenate([w_in_shard, jnp.zeros((rows, pad), w_in_shard.dtype)], axis=1)


def pack_b(shards, d):
    hw, q = d // 2, d // 4
    dt = shards["w_a"].dtype
    wp = shards["w_p"]
    wp = jnp.concatenate([wp, jnp.zeros((hw - wp.shape[0], q), dt)], axis=0)
    small = jnp.concatenate([shards["w_a"], shards["w_b"], wp, jnp.zeros((hw, d - 3 * q), dt)], axis=1)
    return jnp.concatenate([shards["w_ff1"], shards["w_ff2"], shards["w_o"], shards["w_pg"], small], axis=0)


def unpack_b(buf, lay):
    return {nm: buf[r0:r0 + rows, c0:c0 + cols] for nm, (r0, c0, rows, cols) in lay.items()}


def _win_layout(d):
    hw = d // 2
    fh = hw // FOX_HDIM
    orig = {"hq": (0, hw), "hf": (hw, hw), "hi": (2 * hw, hw), "hg": (3 * hw, hw), "fq": (4 * hw, hw),
            "fk": (5 * hw, hw), "fv": (6 * hw, hw), "ff": (7 * hw, fh), "ga": (7 * hw + fh, d), "gb": (7 * hw + fh + d, d)}
    order = ["ga", "gb", "hq", "hf", "hi", "hg", "fq", "fk", "fv", "ff"]
    mine, off = {}, 0
    for nm in order:
        width = orig[nm][1] if nm != "ff" else LANES
        mine[nm] = (off, width)
        off += width
    return orig, order, mine, off


def _adam_fn(rows, vecs):
    w, g, m, v = rows
    m2 = ADAM_B1 * m + (1.0 - ADAM_B1) * g
    v2 = ADAM_B2 * v + (1.0 - ADAM_B2) * (g * g)
    m_hat = m2 / (1.0 - ADAM_B1 ** ADAM_STEP)
    v_hat = v2 / (1.0 - ADAM_B2 ** ADAM_STEP)
    delta = -ADAM_LR * (m_hat / (jnp.sqrt(v_hat) + ADAM_EPS) + ADAM_WD * w)
    return [delta, m2, v2], []


def adamw(w, g, m, v, *, name):
    c = w.shape[1]
    (delta, m2, v2), _ = rowwise(_adam_fn, [w, g, m, v], [], [(c, F32)] * 3, name=name, tm=256)
    return delta, m2, v2


def kernel(x, p, ln0_g, ln0_b, w_in, hg_lb, hg_norm_g, fox_fb, w_a, w_b, w_o, ln1_g, ln1_b, w_ff1, w_ff2, w_pg, w_p, ln2_g, ln2_b, loss_target, m_ln0_g, m_ln0_b, m_w_in, m_hg_lb, m_hg_norm_g, m_fox_fb, m_w_a, m_w_b, m_w_o, m_ln1_g, m_ln1_b, m_w_ff1, m_w_ff2, m_w_pg, m_w_p, m_ln2_g, m_ln2_b, v_ln0_g, v_ln0_b, v_w_in, v_hg_lb, v_hg_norm_g, v_fox_fb, v_w_a, v_w_b, v_w_o, v_ln1_g, v_ln1_b, v_w_ff1, v_w_ff2, v_w_pg, v_w_p, v_ln2_g, v_ln2_b):
    n_seq, seq, d = x.shape
    T = n_seq * seq
    hw = d // 2
    fh = hw // FOX_HDIM
    bh = n_seq * fh
    orig, order, mine, n_in = _win_layout(d)

    big = {"w_in": w_in[0], "w_a": w_a[0], "w_b": w_b[0], "w_o": w_o[0], "w_ff1": w_ff1[0], "w_ff2": w_ff2[0],
           "w_pg": w_pg[0], "w_p": w_p[0]}
    big_m = {"w_in": m_w_in[0], "w_a": m_w_a[0], "w_b": m_w_b[0], "w_o": m_w_o[0], "w_ff1": m_w_ff1[0],
             "w_ff2": m_w_ff2[0], "w_pg": m_w_pg[0], "w_p": m_w_p[0]}
    big_v = {"w_in": v_w_in[0], "w_a": v_w_a[0], "w_b": v_w_b[0], "w_o": v_w_o[0], "w_ff1": v_w_ff1[0],
             "w_ff2": v_w_ff2[0], "w_pg": v_w_pg[0], "w_p": v_w_p[0]}
    names = [nm for nm, _ in BIG_WEIGHTS]
    axis = dict(BIG_WEIGHTS)
    ple = w_p.shape[1]
    lay, b_rows = _b_layout(d, ple)
    in_cols = big["w_in"].shape[1]

    (a_all,) = allgather_chips([pack_a(big["w_in"].astype(BF16))], name="allgather_w_in")
    gather_rest = allgather_rider([pack_b({nm: big[nm].astype(BF16) for nm in names if nm != "w_in"}, d)])
    win = jnp.concatenate([a_all[s, :, :in_cols] for s in range(4)], axis=1)
    win_mine = jnp.concatenate(
        [win[:, orig[nm][0]:orig[nm][0] + orig[nm][1]] for nm in order]
        + [jnp.zeros((d, LANES - fh), BF16)], axis=1)

    x2 = x.reshape(T, d)
    tgt = loss_target.reshape(T, d)
    p_b = p.reshape(T, p.shape[-1]).astype(BF16)
    vec = lambda a: a.reshape(1, -1)
    probs = jax.nn.softmax(hg_lb, axis=0)
    lb = vec(probs[0])

    def ln0_fn(rows, vecs):
        h = _ln_stats(rows[0]) * vecs[0] + vecs[1]
        return [h, h], []
    (h0, h0b), _ = rowwise(ln0_fn, [x2], [vec(ln0_g), vec(ln0_b)], [(d, F32), (d, BF16)], name="ln0_fwd")
    proj = matmul_nn(h0b, win_mine, name="in_proj")

    o_raw, hg_states, (b_all,) = hgrn2_fwd(proj, [mine["hq"][0], mine["hf"][0], mine["hi"][0]], lb, n_seq, seq,
                                           name="hgrn2_fwd", rider=gather_rest)
    view = lambda nm, k, n: WView(b_all, lay[nm][0], lay[nm][1], k, n, axis[nm])
    w_ff1_v, w_ff2_v = view("w_ff1", d, 4 * d), view("w_ff2", 4 * d, d)

    def whole(nm):
        r0, c0, rows, cols = lay[nm]
        return jnp.concatenate([b_all[s, r0:r0 + rows, c0:c0 + cols] for s in range(4)], axis=axis[nm])
    w_o_v, w_pg_v, w_a_v, w_p_v, w_b_full = whole("w_o"), whole("w_pg"), whole("w_a"), whole("w_p"), whole("w_b")

    def ya_fn(rows, vecs):
        o, hg = rows
        outs = []
        for h in range(HG_HEADS):
            oh = o[:, h * HG_DIM:(h + 1) * HG_DIM]
            outs.append(oh * lax.rsqrt(jnp.mean(oh * oh, axis=-1, keepdims=True) + RMS_EPS))
        y = jnp.concatenate(outs, axis=1) * vecs[0] * (hg * _sigmoid(hg))
        return [y], []
    (y_a,), _ = rowwise(ya_fn, [o_raw, (proj,) + mine["hg"]], [hg_norm_g], [(hw, BF16)], name="hgrn2_out_fwd")

    fb_pad = jnp.concatenate([fox_fb, jnp.zeros((1, LANES - fh), F32)], axis=1)

    def lf_fn(rows, vecs):
        u = rows[0] + vecs[0]
        return [jnp.minimum(u, 0.0) - jnp.log(1.0 + jnp.exp(-jnp.abs(u)))], []
    (lf,), _ = rowwise(lf_fn, [(proj,) + mine["ff"]], [fb_pad], [(LANES, F32)], name="fox_logf")
    c_cum = seq_cumsum(lf, n_seq, seq, reverse=False, name="fox_cumsum")

    place = _fox_placement(fh)

    def prep_fn(rows, vecs):
        fq_, fk_, fv_, cc = rows
        pq, pk, aq, ak, oq, ok = vecs
        parts = jnp.concatenate(_split3(cc), axis=1)
        mm = lambda a_, b_: jnp.dot(a_, b_, preferred_element_type=F32)
        q_ = mm(fq_.astype(BF16), pq) + mm(parts, aq) + oq
        k_ = mm(fk_.astype(BF16), pk) + mm(parts, ak) + ok
        return [q_, k_, mm(fv_.astype(BF16), pk)], []
    wa = fh * FOX_AUG
    (qa, ka, va), _ = rowwise(prep_fn, [(proj,) + mine["fq"], (proj,) + mine["fk"], (proj,) + mine["fv"], c_cum],
                              [place[nm] for nm in ("pq", "pk", "aq", "ak", "oq", "ok")], [(wa, BF16)] * 3,
                              name="fox_prep")
    as_seq = lambda t2d: t2d.reshape(n_seq, seq, t2d.shape[1])
    o_fox, ox_fox, lse = fox_fwd(as_seq(qa), as_seq(ka), as_seq(va), name="fox_fwd")
    y_b = o_fox.reshape(T, wa)
    wb_pad = jnp.concatenate([w_b_full.reshape(fh, FOX_HDIM, d), jnp.zeros((fh, FOX_AUG - FOX_HDIM, d), BF16)],
                             axis=1).reshape(wa, d)

    pa = matmul_nn(y_a, w_a_v, name="proj_a")
    pb = matmul_nn(y_b, wb_pad, name="proj_b")

    def merge_fn(rows, vecs):
        ga, gb, a, b = rows
        return [_sigmoid(ga) * a + _sigmoid(gb) * b], []
    (merged,), _ = rowwise(merge_fn, [(proj,) + mine["ga"], (proj,) + mine["gb"], pa, pb], [], [(d, BF16)],
                           name="merge_fwd")
    mix = matmul_nn(merged, w_o_v, name="out_proj")

    def ln1_fn(rows, vecs):
        z = ALPHA * rows[0] + rows[1]
        h = _ln_stats(z) * vecs[0] + vecs[1]
        return [z, h, h], []
    (z1, h1, h1b), _ = rowwise(ln1_fn, [h0, mix], [ln1_g, ln1_b], [(d, F32), (d, F32), (d, BF16)], name="ln1_fwd")

    relu2 = lambda u: jnp.square(jnp.maximum(u, 0.0))
    act = matmul_nn(h1b, w_ff1_v, name="ff1", out_dtype=BF16, epilogue=relu2)
    ff = matmul_nn(act, w_ff2_v, name="ff2")
    pg = matmul_nn(h1b, w_pg_v, name="ple_gate")
    pe = matmul_nn(p_b, w_p_v, name="ple_embed")

    def head_fn(rows, vecs):
        h1v, ffv, pgv, pev, t = rows
        g2, b2 = vecs
        sp = _sigmoid(pgv)
        z = ALPHA * h1v + ffv + sp * pev
        y = _ln_stats(z) * g2 + b2
        err = y - t
        loss_rows = 0.5 * jnp.mean(err * err, axis=-1, keepdims=True)
        dy = err * (1.0 / d)
        dz, dg2, db2 = _ln_bwd(z, dy, g2)
        loss_acc = jnp.broadcast_to(_colsum(loss_rows), (1, LANES))
        return [dz, dz, dz * pev * (sp * (1.0 - sp)), dz * sp], [dg2, db2, loss_acc]
    (dz2, dz2b, dpg, dpe), (g_ln2_g, g_ln2_b, loss_part) = rowwise(
        head_fn, [h1, ff, pg, pe, tgt], [ln2_g, ln2_b],
        [(d, F32), (d, BF16), (d, BF16), (d, BF16)], [d, d, LANES], name="head_fwd_bwd")

    dact = lambda da, a: da * (2.0 * jnp.sqrt(a.astype(F32)))
    du = matmul_nn(dz2b, w_ff2_v, transpose_rhs=True, name="d_ff2", out_dtype=BF16, epilogue=dact, aux=act)
    dh1_ff = matmul_nn(du, w_ff1_v, transpose_rhs=True, name="d_ff1")
    dh1_pg = matmul_nn(dpg, w_pg_v, transpose_rhs=True, name="d_ple_gate")

    def ln1_bwd_fn(rows, vecs):
        dh1 = ALPHA * rows[0] + rows[1] + rows[2]
        dz, dg, db = _ln_bwd(rows[3], dh1, vecs[0])
        return [dz, dz], [dg, db]
    (dz1, dz1b), (g_ln1_g, g_ln1_b) = rowwise(ln1_bwd_fn, [dz2, dh1_ff, dh1_pg, z1], [ln1_g],
                                              [(d, F32), (d, BF16)], [d, d], name="ln1_bwd")
    dmerged = matmul_nn(dz1b, w_o_v, transpose_rhs=True, name="d_out_proj")

    def merge_bwd_fn(rows, vecs):
        dm, ga, gb, a, b = rows
        sa, sb = _sigmoid(ga), _sigmoid(gb)
        return [dm * a * (sa * (1.0 - sa)), dm * b * (sb * (1.0 - sb)), dm * sa, dm * sb], []
    (dga, dgb, dma, dmb), _ = rowwise(merge_bwd_fn, [dmerged, (proj,) + mine["ga"], (proj,) + mine["gb"], pa, pb], [],
                                      [(d, BF16)] * 4, name="merge_bwd")
    dya = matmul_nn(dma, w_a_v, transpose_rhs=True, name="d_proj_a")
    dyb = matmul_nn(dmb, wb_pad, transpose_rhs=True, name="d_proj_b", out_dtype=BF16)

    def ya_bwd_fn(rows, vecs):
        o, hg, dy = rows
        ng = vecs[0]
        sg = _sigmoid(hg)
        gate = hg * sg
        dn_parts, do_parts, n_parts = [], [], []
        for h in range(HG_HEADS):
            hs = slice(h * HG_DIM, (h + 1) * HG_DIM)
            oh = o[:, hs]
            r = lax.rsqrt(jnp.mean(oh * oh, axis=-1, keepdims=True) + RMS_EPS)
            nh = oh * r
            dn = dy[:, hs] * ng[:, hs] * gate[:, hs]
            do_parts.append(r * (dn - nh * jnp.mean(dn * nh, axis=-1, keepdims=True)))
            n_parts.append(nh)
        nrm = jnp.concatenate(n_parts, axis=1)
        dhg = dy * nrm * ng * (sg * (1.0 + hg * (1.0 - sg)))
        return [jnp.concatenate(do_parts, axis=1), dhg], [_colsum(dy * nrm * gate)]
    (do_raw, dhg), (g_norm_g,) = rowwise(ya_bwd_fn, [o_raw, (proj,) + mine["hg"], dya], [hg_norm_g],
                                         [(hw, F32), (hw, BF16)], [hw], name="hgrn2_out_bwd")
    dhq, dhf, dhi, g_lb = hgrn2_bwd(proj, [mine["hq"][0], mine["hf"][0], mine["hi"][0]], lb, do_raw, hg_states,
                                    n_seq, seq, name="hgrn2_bwd")

    do_fox = as_seq(dyb)
    dqa, dka, dva, dsum = fox_bwd(as_seq(qa), as_seq(ka), as_seq(va), do_fox, ox_fox, lse, name="fox_bwd")

    def unprep_fn(rows, vecs):
        mm = lambda a_, b_: jnp.dot(a_.astype(BF16), b_, preferred_element_type=F32)
        return [mm(rows[0], vecs[0]), mm(rows[1], vecs[1]), mm(rows[2], vecs[1])], []
    (dfq, dfk, dfv), _ = rowwise(unprep_fn, [dqa.reshape(T, wa), dka.reshape(T, wa), dva.reshape(T, wa)],
                                 [place["pqt"], place["pkt"]], [(hw, BF16)] * 3, name="fox_unprep")
    dc = -dsum.reshape(n_seq, fh, seq).transpose(0, 2, 1).reshape(T, fh)
    dc = jnp.concatenate([dc, jnp.zeros((T, LANES - fh), F32)], axis=1)
    dlf = seq_cumsum(dc, n_seq, seq, reverse=True, name="fox_cumsum_bwd")

    def lf_bwd_fn(rows, vecs):
        u = rows[0] + vecs[0]
        du_ = rows[1] * _sigmoid(-u)
        return [du_], [_colsum(du_)]
    (dff_,), (g_fb,) = rowwise(lf_bwd_fn, [(proj,) + mine["ff"], dlf], [fb_pad], [(LANES, BF16)], [LANES],
                               name="fox_logf_bwd")

    dproj = jnp.concatenate([dga, dgb, dhq, dhf, dhi, dhg, dfq, dfk, dfv, dff_], axis=1)

    gfull = {
        "w_a": matmul_tn(y_a, dma, name="g_w_a"),
        "w_b": matmul_tn(y_b, dmb, name="g_w_b").reshape(fh, FOX_AUG, d)[:, :FOX_HDIM].reshape(hw, d),
        "w_o": matmul_tn(merged, dz1b, name="g_w_o"),
        "w_ff1": matmul_tn(h1b, du, name="g_w_ff1"),
        "w_ff2": matmul_tn(act, dz2b, name="g_w_ff2"),
        "w_pg": matmul_tn(h1b, dpg, name="g_w_pg"),
        "w_p": matmul_tn(p_b, dpe, name="g_w_p"),
    }

    def chip_parts(nm, s):
        g = gfull[nm]
        n = g.shape[axis[nm]] // 4
        return lax.slice_in_dim(g, s * n, (s + 1) * n, axis=axis[nm])
    me = 2 * lax.axis_index("x") + lax.axis_index("y")
    core = lax.axis_index("c")

    def sum2_fn(rows, vecs):
        s = rows[0] + rows[1].astype(F32)
        return [s, s], []

    def sum4_fn(rows, vecs):
        a, r0, r1, r2 = rows
        return [((a + r0.astype(F32)) + r1.astype(F32)) + r2.astype(F32)], []

    def chip_pair_sum(g, tag):
        h, cols = g.shape[1] // 2, g.shape[2]
        keep = lax.dynamic_slice_in_dim(g, core * h, h, axis=1)
        give = lax.dynamic_slice_in_dim(g, (1 - core) * h, h, axis=1).astype(BF16)
        (from_core,) = swap_cores([give], name="swap_partials_" + tag)
        (s32, s16), _ = rowwise(sum2_fn, [keep.reshape(4 * h, cols), from_core.reshape(4 * h, cols)], [],
                                [(cols, F32), (cols, BF16)], name="sum_cores_" + tag, tm=SUM_TILE)
        return s32.reshape(4, h, cols), s16.reshape(4, h, cols)

    def chip_sum(pr, gt, tag):
        own = lax.dynamic_index_in_dim(pr, me, axis=0, keepdims=False)
        (q,), _ = rowwise(sum4_fn, [own, gt[0], gt[1], gt[2]], [], [(own.shape[1], F32)], name="sum_chips_" + tag,
                          tm=SUM_TILE)
        return q

    grads_b = jnp.stack([pack_b({nm: chip_parts(nm, s) for nm in names if nm != "w_in"}, d) for s in range(4)])
    pair_rest, pair_rest_b = chip_pair_sum(grads_b, "rest")
    gw_in_mine, (got_rest,) = matmul_tn(h0b, dproj, name="g_w_in", rider=scatter_rider([pair_rest_b]))
    gfull["w_in"] = jnp.concatenate([gw_in_mine[:, mine[nm][0]:mine[nm][0] + orig[nm][1]]
                                     for nm in ["hq", "hf", "hi", "hg", "fq", "fk", "fv", "ff", "ga", "gb"]], axis=1)
    grads_a = jnp.stack([pack_a(chip_parts("w_in", s)) for s in range(4)])
    pair_in, pair_in_b = chip_pair_sum(grads_a, "w_in")
    dh0_in, (got_in,) = matmul_nn(dproj, win_mine, transpose_rhs=True, name="d_in_proj",
                                  rider=scatter_rider([pair_in_b]))

    def ln0_bwd_fn(rows, vecs):
        dh0 = rows[0] + ALPHA * rows[1]
        dx, dg, db = _ln_bwd(rows[2], dh0, vecs[0])
        return [dx], [dg, db]
    (dx,), (g_ln0_g, g_ln0_b) = rowwise(ln0_bwd_fn, [dh0_in, dz1, x2], [vec(ln0_g)], [(d, F32)], [d, d],
                                        name="ln0_bwd")
    q_half = [chip_sum(pair_in, got_in, "w_in"), chip_sum(pair_rest, got_rest, "rest")]
    q_other = swap_cores(q_half, name="swap_halves")
    g_a, g_b = [jnp.concatenate([jnp.where(core == 0, mine_, other), jnp.where(core == 0, other, mine_)], axis=0)
                for mine_, other in zip(q_half, q_other)]
    g_shards = unpack_b(g_b, lay)
    g_shards["w_in"] = g_a[:, :in_cols]

    def row1024(*parts):
        r = jnp.concatenate([q.reshape(1, -1) for q in parts], axis=1)
        return jnp.concatenate([r, jnp.zeros((1, PACK_W - r.shape[1]), F32)], axis=1) if r.shape[1] < PACK_W else r
    small_rows = [row1024(g_ln0_g), row1024(g_ln0_b), row1024(g_ln1_g), row1024(g_ln1_b), row1024(g_ln2_g),
                  row1024(g_ln2_b), row1024(g_norm_g, g_lb), row1024(g_fb[:, :fh], loss_part[:, :1])]
    small = allreduce_small(jnp.concatenate(small_rows, axis=0), name="allreduce_small")
    s_ln0_g, s_ln0_b, s_ln1_g, s_ln1_b, s_ln2_g, s_ln2_b = [small[r:r + 1] for r in range(6)]
    s_norm_g, s_lb = small[6:7, :hw], small[6:7, hw:2 * hw]
    s_fb, loss = small[7:8, :fh], small[7, fh]
    p0 = probs[0:1]
    jac = p0 * (1.0 - p0)
    s_hg_lb = jnp.concatenate([s_lb * jac, -s_lb * jac], axis=0)

    small_w = [vec(ln0_g), vec(ln0_b), ln1_g, ln1_b, ln2_g, ln2_b, hg_lb.reshape(1, -1), hg_norm_g, fox_fb]
    small_g = [s_ln0_g, s_ln0_b, s_ln1_g, s_ln1_b, s_ln2_g, s_ln2_b, s_hg_lb.reshape(1, -1), s_norm_g, s_fb]
    small_m = [vec(m_ln0_g), vec(m_ln0_b), m_ln1_g, m_ln1_b, m_ln2_g, m_ln2_b, m_hg_lb.reshape(1, -1), m_hg_norm_g, m_fox_fb]
    small_v = [vec(v_ln0_g), vec(v_ln0_b), v_ln1_g, v_ln1_b, v_ln2_g, v_ln2_b, v_hg_lb.reshape(1, -1), v_hg_norm_g, v_fox_fb]
    pad_rows = lambda lst, fill: jnp.concatenate(
        [row1024(a) if fill == 0.0 else jnp.concatenate([a.reshape(1, -1), jnp.full((1, PACK_W - a.size), fill, F32)], axis=1)
         for a in lst] + [jnp.full((16 - len(lst), PACK_W), fill, F32)], axis=0)
    sd, sm, sv = adamw(pad_rows(small_w, 0.0), pad_rows(small_g, 0.0), pad_rows(small_m, 0.0), pad_rows(small_v, 1.0),
                       name="adamw_small")
    small_shapes = [ln0_g.shape, ln0_b.shape, ln1_g.shape, ln1_b.shape, ln2_g.shape, ln2_b.shape, hg_lb.shape,
                    hg_norm_g.shape, fox_fb.shape]
    take = lambda buf: [buf[r, :int(np.prod(shp))].reshape(shp) for r, shp in enumerate(small_shapes)]
    sg_out, sd_out, sm_out, sv_out = [g.reshape(shp) for g, shp in zip(small_g, small_shapes)], take(sd), take(sm), take(sv)

    big_out = {}
    for nm in names:
        delta, m2, v2 = adamw(big[nm], g_shards[nm], big_m[nm], big_v[nm], name="adamw_" + nm)
        big_out[nm] = (g_shards[nm][None], delta[None], m2[None], v2[None])

    def ordered(k):
        sm_ = [sg_out, sd_out, sm_out, sv_out][k]
        bg = lambda nm: big_out[nm][k]
        return [sm_[0], sm_[1], bg("w_in"), sm_[6], sm_[7], sm_[8], bg("w_a"), bg("w_b"), bg("w_o"), sm_[2], sm_[3],
                bg("w_ff1"), bg("w_ff2"), bg("w_pg"), bg("w_p"), sm_[4], sm_[5]]
    grad_x = dx.reshape(n_seq, seq, d)
    return (loss, grad_x, *ordered(0), *ordered(1), *ordered(2), *ordered(3))
```

```python
import functools
from typing import NamedTuple, Optional

import numpy as np
import jax
import jax.numpy as jnp
from jax import lax
from jax.experimental import pallas as pl
from jax.experimental.pallas import tpu as pltpu

F32 = jnp.float32
BF16 = jnp.bfloat16
MESH = pl.DeviceIdType.MESH

VMEM_LIMIT_BYTES = 48 * 1024 * 1024
LANES = 128
HG_HEADS = 4
HG_DIM = 128
HG_BLK = 16
HG_TILE = 256
HG_SLOTS = 4
FOX_HDIM = 64
FOX_AUG = 128
FOX_TQ = 1024
FOX_FWD_HEADS = 1
LN_EPS = 1e-5
RMS_EPS = 1e-6
DEPTH = 1
ALPHA = (2.0 * DEPTH) ** 0.25
ADAM_LR, ADAM_B1, ADAM_B2, ADAM_EPS, ADAM_WD, ADAM_STEP = 0.001, 0.9, 0.999, 1e-08, 0.01, 10
NEG_INF = -1e30


def _cparams(sem):
    return pltpu.CompilerParams(dimension_semantics=sem, vmem_limit_bytes=VMEM_LIMIT_BYTES)


def _tile(n, cap):
    if n <= cap:
        return n
    best = None
    for t in range(LANES, cap + 1, LANES):
        if n % t == 0:
            best = t
    assert best is not None, (n, cap)
    return best


class WView(NamedTuple):
    arr: jax.Array
    r0: int
    c0: int
    k: int
    n: int
    split: Optional[int]


def matmul_nn(a, w, *, name, transpose_rhs=False, out_dtype=F32, epilogue=None, aux=None, tm=1024, rider=None,
              post=None, post_aux=(), post_vecs=(), post_outs=(), post_accs=()):
    wv = w if isinstance(w, WView) else WView(w[None], 0, 0, w.shape[0], w.shape[1], None)
    rows_s = wv.k // 4 if wv.split == 0 else wv.k
    cols_s = wv.n // 4 if wv.split == 1 else wv.n
    tr, tc = _tile(rows_s, 1152), _tile(cols_s, 1152)
    assert wv.r0 % tr == 0 and wv.c0 % tc == 0
    T, K = a.shape
    N, tn, tk = (wv.k, tr, tc) if transpose_rhs else (wv.n, tc, tr)
    assert K == (wv.n if transpose_rhs else wv.k)
    tm = min(tm, T)
    assert T % tm == 0
    nk = K // tk

    def w_block(ri, ci):
        if wv.split == 0:
            return (ri * tr) // rows_s, (wv.r0 + (ri * tr) % rows_s) // tr, wv.c0 // tc + ci
        if wv.split == 1:
            return (ci * tc) // cols_s, wv.r0 // tr + ri, (wv.c0 + (ci * tc) % cols_s) // tc
        return 0, wv.r0 // tr + ri, wv.c0 // tc + ci

    fused = post is not None
    assert not fused or N == tn
    aux_list = list(post_aux) if fused else ([aux] if aux is not None else [])
    aux_list = [x if isinstance(x, tuple) else (x, 0, x.shape[1]) for x in aux_list]
    vec_list = list(post_vecs)
    out_dtypes = list(post_outs) if fused else [out_dtype]
    n_aux, n_vec, n_out, n_acc = len(aux_list), len(vec_list), len(out_dtypes), len(post_accs)

    def body(*refs):
        a_ref, w_ref = refs[:2]
        aux_refs = refs[2:2 + n_aux]
        vec_refs = refs[2 + n_aux:2 + n_aux + n_vec]
        out_refs = refs[2 + n_aux + n_vec:2 + n_aux + n_vec + n_out]
        sum_refs = refs[2 + n_aux + n_vec + n_out:2 + n_aux + n_vec + n_out + n_acc]
        acc_ref = refs[-1]
        m, k = pl.program_id(1), pl.program_id(2)
        if transpose_rhs:
            part = lax.dot_general(a_ref[...], w_ref[...], (((1,), (1,)), ((), ())), preferred_element_type=F32)
        else:
            part = jnp.dot(a_ref[...], w_ref[...], preferred_element_type=F32)

        def write(res):
            if not fused:
                if epilogue is not None:
                    res = epilogue(res) if not aux_refs else epilogue(res, aux_refs[0][...])
                out_refs[0][...] = res.astype(out_dtype)
                return
            outs, sums = post(res, [r[...] for r in aux_refs], [v[...] for v in vec_refs])
            assert len(outs) == n_out and len(sums) == n_acc
            for r, val in zip(out_refs, outs):
                r[...] = val.astype(r.dtype)
            for r, val in zip(sum_refs, sums):
                def first_rows(r=r, val=val):
                    r[...] = val

                def later_rows(r=r, val=val):
                    r[...] += val
                pl.when(m == 0)(first_rows)
                pl.when(m > 0)(later_rows)

        if nk == 1:
            write(part)
        else:
            @pl.when(k == 0)
            def _():
                acc_ref[...] = part

            @pl.when(k > 0)
            def _():
                acc_ref[...] += part

            @pl.when(k == nk - 1)
            def _():
                write(acc_ref[...])

    w_index = (lambda n, m, k: w_block(n, k)) if transpose_rhs else (lambda n, m, k: w_block(k, n))
    in_specs = [pl.BlockSpec((tm, tk), lambda n, m, k: (m, k)),
                pl.BlockSpec((None, tr, tc), w_index)]
    args = [a, wv.arr]
    for arr, off, width in aux_list:
        assert width == N and off % tn == 0
        in_specs.append(pl.BlockSpec((tm, tn), functools.partial(lambda n, m, k, blk: (m, blk + n), blk=off // tn)))
        args.append(arr)
    for v in vec_list:
        in_specs.append(pl.BlockSpec(v.shape, lambda n, m, k: (0, 0)))
        args.append(v)
    out_specs = [pl.BlockSpec((tm, tn), lambda n, m, k: (m, n)) for _ in out_dtypes]
    out_specs += [pl.BlockSpec((1, tn), lambda n, m, k: (0, 0)) for _ in post_accs]
    out_shape = [jax.ShapeDtypeStruct((T, N), dt) for dt in out_dtypes]
    out_shape += [jax.ShapeDtypeStruct((1, N), F32) for _ in post_accs]
    scratch = [pltpu.VMEM((tm, tn) if nk > 1 else (8, LANES), F32)]
    grid = (N // tn, T // tm, nk)
    sem = ("arbitrary",) * 3 if (n_acc or rider is not None) else ("parallel", "parallel", "arbitrary")
    params = pltpu.CompilerParams(dimension_semantics=sem, vmem_limit_bytes=VMEM_LIMIT_BYTES,
                                  has_side_effects=rider is not None)
    if rider is not None:
        r_in, r_out, r_sems = rider.specs()
        body = rider.wrap(body, len(in_specs), len(out_specs), 3)
        in_specs, out_specs, out_shape = in_specs + r_in, out_specs + r_out, out_shape + rider.out_shape
        scratch, args = scratch + r_sems, args + list(rider.ins)
    res = pl.pallas_call(body, name=name, grid=grid, in_specs=in_specs, out_specs=out_specs, out_shape=out_shape,
                         scratch_shapes=scratch, compiler_params=params)(*args)
    main = (list(res[:n_out]), list(res[n_out:n_out + n_acc])) if fused else res[0]
    return main if rider is None else (main, list(res[n_out + n_acc:]))


def matmul_tn(a, b, *, name, tk=1024, rider=None):
    T, M = a.shape
    T2, N = b.shape
    tk = min(tk, T)
    assert T == T2 and T % tk == 0
    tm = _tile(M, 1024)
    tn = _tile(N, 1152)

    def body(a_ref, b_ref, o_ref):
        k = pl.program_id(2)
        part = lax.dot_general(a_ref[...], b_ref[...], (((0,), (0,)), ((), ())), preferred_element_type=F32)

        @pl.when(k == 0)
        def _():
            o_ref[...] = part

        @pl.when(k > 0)
        def _():
            o_ref[...] += part

    in_specs = [pl.BlockSpec((tk, tm), lambda m, n, k: (k, m)), pl.BlockSpec((tk, tn), lambda m, n, k: (k, n))]
    out_specs = [pl.BlockSpec((tm, tn), lambda m, n, k: (m, n))]
    out_shape = [jax.ShapeDtypeStruct((M, N), F32)]
    grid = (M // tm, N // tn, T // tk)
    if rider is None:
        return pl.pallas_call(body, name=name, grid=grid, in_specs=in_specs, out_specs=out_specs, out_shape=out_shape,
                              compiler_params=_cparams(("parallel", "parallel", "arbitrary")))(a, b)[0]
    r_in, r_out, r_sems = rider.specs()
    res = pl.pallas_call(
        rider.wrap(body, 2, 1, 3), name=name, grid=grid, in_specs=in_specs + r_in, out_specs=out_specs + r_out,
        out_shape=out_shape + rider.out_shape, scratch_shapes=r_sems,
        compiler_params=pltpu.CompilerParams(dimension_semantics=("arbitrary",) * 3,
                                             vmem_limit_bytes=VMEM_LIMIT_BYTES, has_side_effects=True),
    )(a, b, *rider.ins)
    return res[0], list(res[1:])


def rowwise(fn, rows, vecs, outs, accs=(), *, name, tm=512):
    rows = [r if isinstance(r, tuple) else (r, 0, r.shape[1]) for r in rows]
    T = rows[0][0].shape[0]
    tm = min(tm, T)
    assert T % tm == 0
    n_rows, n_vecs, n_outs, n_accs = len(rows), len(vecs), len(outs), len(accs)

    def body(*refs):
        row_refs = refs[:n_rows]
        vec_refs = refs[n_rows:n_rows + n_vecs]
        out_refs = refs[n_rows + n_vecs:n_rows + n_vecs + n_outs]
        acc_refs = refs[n_rows + n_vecs + n_outs:]
        out_vals, acc_vals = fn([r[...] for r in row_refs], [v[...] for v in vec_refs])
        assert len(out_vals) == n_outs and len(acc_vals) == n_accs
        for r, val in zip(out_refs, out_vals):
            r[...] = val.astype(r.dtype)
        if n_accs:
            i = pl.program_id(0)

            @pl.when(i == 0)
            def _():
                for r in acc_refs:
                    r[...] = jnp.zeros_like(r)

            for r, val in zip(acc_refs, acc_vals):
                r[...] += val

    in_specs = []
    for arr, off, width in rows:
        assert off % width == 0
        in_specs.append(pl.BlockSpec((tm, width), functools.partial(lambda i, blk: (i, blk), blk=off // width)))
    for v in vecs:
        in_specs.append(pl.BlockSpec(v.shape, lambda i: (0, 0)))
    out_specs = [pl.BlockSpec((tm, w), lambda i: (i, 0)) for w, _ in outs]
    out_specs += [pl.BlockSpec((1, w), lambda i: (0, 0)) for w in accs]
    out_shape = [jax.ShapeDtypeStruct((T, w), dt) for w, dt in outs]
    out_shape += [jax.ShapeDtypeStruct((1, w), F32) for w in accs]
    res = pl.pallas_call(
        body, name=name,
        grid=(T // tm,),
        in_specs=in_specs, out_specs=out_specs, out_shape=out_shape,
        compiler_params=_cparams(("arbitrary",) if n_accs else ("parallel",)),
    )(*[r[0] for r in rows], *vecs)
    return res[:n_outs], res[n_outs:]


def _colsum(x):
    return jnp.sum(x, axis=0, keepdims=True)


def _sigmoid(x):
    return 1.0 / (1.0 + jnp.exp(-x))


def _ln_stats(z):
    mu = jnp.mean(z, axis=-1, keepdims=True)
    zc = z - mu
    var = jnp.mean(zc * zc, axis=-1, keepdims=True)
    return zc * lax.rsqrt(var + LN_EPS)


def _ln_bwd(zhat_src, dy, g):
    mu = jnp.mean(zhat_src, axis=-1, keepdims=True)
    zc = zhat_src - mu
    var = jnp.mean(zc * zc, axis=-1, keepdims=True)
    rstd = lax.rsqrt(var + LN_EPS)
    zh = zc * rstd
    dzh = dy * g
    dz = rstd * (dzh - jnp.mean(dzh, axis=-1, keepdims=True) - zh * jnp.mean(dzh * zh, axis=-1, keepdims=True))
    return dz, _colsum(dy * zh), _colsum(dy)


def _hg_constants():
    r = np.arange(HG_TILE)
    same = (r[:, None] // HG_BLK) == (r[None, :] // HG_BLK)
    lower = (same & (r[None, :] <= r[:, None])).astype(np.float32)
    upper = (same & (r[None, :] >= r[:, None])).astype(np.float32)
    total = same.astype(np.float32)
    c = np.arange(2 * HG_DIM)
    bd = ((c[:, None] // HG_DIM) == (c[None, :] // HG_DIM)).astype(np.float32)
    n = HG_BLK * HG_BLK
    rr = np.arange(n)
    sel_t = (rr[None, :] // HG_BLK == np.arange(HG_BLK)[:, None]).astype(np.float32)
    sel_s = (rr[None, :] % HG_BLK == np.arange(HG_BLK)[:, None]).astype(np.float32)
    as_bf = lambda m: jnp.asarray(m, dtype=BF16)
    return as_bf(lower), as_bf(upper), as_bf(total), as_bf(bd), as_bf(sel_t), as_bf(sel_s)


def _keep_bf16_bits(x):
    bits = lax.bitcast_convert_type(x, jnp.int32) & jnp.int32(-65536)
    return lax.bitcast_convert_type(bits, F32)


def _head_sums(stack_ref, slot, bd):
    pair = bd.shape[0]
    return jnp.concatenate([jnp.dot(stack_ref[slot, :, c0:c0 + pair], bd, preferred_element_type=F32)
                            for c0 in range(0, stack_ref.shape[2], pair)], axis=1)


def _split3(x):
    hi = _keep_bf16_bits(x)
    r1 = x - hi
    mid = _keep_bf16_bits(r1)
    lo = _keep_bf16_bits(r1 - mid)
    return hi.astype(BF16), mid.astype(BF16), lo.astype(BF16)


def _dot3(m01, x):
    hi, mid, lo = _split3(x)
    d = lambda p: jnp.dot(m01, p, preferred_element_type=F32)
    return (d(lo) + d(mid)) + d(hi)


def _hg_prologue(hq, hf, lb, lower, total):
    sq = _sigmoid(hq)
    q = hq * sq
    sg = _sigmoid(hf)
    f = lb + (1.0 - lb) * sg
    g = jnp.log(f)
    k = 1.0 - f
    b = _dot3(lower, g)
    bl = _dot3(total, g)
    return q, k, f, sg, sq, b, bl


def _stack16(fn):
    return [fn(t) for t in range(HG_BLK)]


def hgrn2_fwd(proj, offs, lb, n_seq, seq, *, name, rider=None):
    T = n_seq * seq
    W = HG_HEADS * HG_DIM
    n_tiles = seq // HG_TILE
    nb = HG_TILE // HG_BLK
    lower, _, total, bd, sel_t, _ = _hg_constants()

    def body(hq_ref, hf_ref, hi_ref, lb_ref, lower_ref, total_ref, bd_ref, selt_ref,
             o_ref, st_out_ref,
             st_ref, q_s, k_s, v_s, b_s, qt_s, kt_s, d_s, p_s):
        @pl.when(pl.program_id(1) == 0)
        def _():
            st_ref[...] = jnp.zeros_like(st_ref)

        q, k, _, _, _, b, bl = _hg_prologue(hq_ref[...], hf_ref[...], lb_ref[...], lower_ref[...], total_ref[...])
        q_s[...] = q
        k_s[...] = k
        v_s[...] = hi_ref[...]
        b_s[...] = b
        qt_s[...] = q * jnp.exp(b)
        kt_s[...] = k * jnp.exp(jnp.minimum(bl - b, 0.0))
        d_s[...] = jnp.exp(bl)
        rowi = lax.broadcasted_iota(jnp.int32, (HG_BLK, W), 0)

        def block(i, slot):
            r0 = pl.multiple_of(i * HG_BLK, HG_BLK)
            rows = pl.ds(r0, HG_BLK)
            qi, ki, vi, bi = q_s[rows, :], k_s[rows, :], v_s[rows, :], b_s[rows, :]
            for t in range(HG_BLK):
                e = jnp.where(rowi <= t, jnp.exp(jnp.minimum(bi[t:t + 1, :] - bi, 0.0)), 0.0)
                p_s[slot, pl.ds(t * HG_BLK, HG_BLK), :] = (e * qi[t:t + 1, :] * ki).astype(BF16)
            a_b = _head_sums(p_s, slot, bd_ref[...])
            vt = jnp.concatenate([vi] * HG_BLK, axis=0)
            o_blk = jnp.dot(selt_ref[...], (a_b * vt).astype(BF16), preferred_element_type=F32)
            qti, kti, di = qt_s[rows, :], kt_s[rows, :], d_s[rows, :]
            outs = []
            for h in range(HG_HEADS):
                hs = slice(h * HG_DIM, (h + 1) * HG_DIM)
                st_h = st_ref[hs, :]
                st_out_ref[i, hs, :] = st_h
                outs.append(lax.dot_general(qti[:, hs].astype(BF16), st_h.astype(BF16),
                                            (((1,), (1,)), ((), ())), preferred_element_type=F32))
                upd = lax.dot_general(vi[:, hs].astype(BF16), kti[:, hs].astype(BF16),
                                      (((0,), (0,)), ((), ())), preferred_element_type=F32)
                st_ref[hs, :] = st_h * di[0:1, hs] + upd
            o_ref[rows, :] = o_blk + jnp.concatenate(outs, axis=1)

        def some_blocks(jj, carry):
            for slot in range(HG_SLOTS):
                block(HG_SLOTS * jj + slot, slot)
            return carry

        lax.fori_loop(0, nb // HG_SLOTS, some_blocks, 0)

    col = lambda off: functools.partial(lambda s, t, blk: (s * n_tiles + t, blk), blk=off // W)
    const = lambda m: pl.BlockSpec(m.shape, lambda s, t: (0, 0))
    tile_f32 = pltpu.VMEM((HG_TILE, W), F32)
    in_specs = [pl.BlockSpec((HG_TILE, W), col(offs[0])), pl.BlockSpec((HG_TILE, W), col(offs[1])),
                pl.BlockSpec((HG_TILE, W), col(offs[2])), const(lb), const(lower), const(total), const(bd),
                const(sel_t)]
    out_specs = [pl.BlockSpec((HG_TILE, W), lambda s, t: (s * n_tiles + t, 0)),
                 pl.BlockSpec((nb, W, HG_DIM), lambda s, t: (s * n_tiles + t, 0, 0))]
    out_shape = [jax.ShapeDtypeStruct((T, W), F32), jax.ShapeDtypeStruct((T // HG_BLK, W, HG_DIM), F32)]
    scratch = [pltpu.VMEM((W, HG_DIM), F32)] + [tile_f32] * 7 + [pltpu.VMEM((HG_SLOTS, HG_BLK * HG_BLK, W), BF16)]
    args = [proj, proj, proj, lb, lower, total, bd, sel_t]
    params = _cparams(("arbitrary", "arbitrary"))
    if rider is not None:
        r_in, r_out, r_sems = rider.specs()
        body = rider.wrap(body, len(in_specs), len(out_specs), 2)
        in_specs, out_specs, out_shape = in_specs + r_in, out_specs + r_out, out_shape + rider.out_shape
        scratch, args = scratch + r_sems, args + rider.ins
        params = pltpu.CompilerParams(dimension_semantics=("arbitrary", "arbitrary"),
                                      vmem_limit_bytes=VMEM_LIMIT_BYTES, has_side_effects=True)
    res = pl.pallas_call(body, name=name, grid=(n_seq, n_tiles), in_specs=in_specs, out_specs=out_specs,
                         out_shape=out_shape, scratch_shapes=scratch, compiler_params=params)(*args)
    return res[0], res[1], list(res[2:])


def hgrn2_bwd(proj, offs, lb, do, states, n_seq, seq, *, name):
    T = n_seq * seq
    W = HG_HEADS * HG_DIM
    n_tiles = seq // HG_TILE
    nb = HG_TILE // HG_BLK
    lower, upper, total, bd, sel_t, sel_s = _hg_constants()

    def body(hq_ref, hf_ref, hi_ref, do_ref, st_in_ref, lb_ref, lower_ref, upper_ref, total_ref, bd_ref,
             selt_ref, sels_ref,
             dhq_ref, dhf_ref, dhi_ref, dlb_ref,
             dst_ref, q_s, k_s, v_s, b_s, qt_s, kt_s, d_s, eb_s, ekb_s, dq_s, dk_s, db_s, dv_s,
             p_s, e_s, w_s):
        first = jnp.logical_and(pl.program_id(0) == 0, pl.program_id(1) == 0)

        @pl.when(first)
        def _():
            dlb_ref[...] = jnp.zeros_like(dlb_ref)

        @pl.when(pl.program_id(1) == 0)
        def _():
            dst_ref[...] = jnp.zeros_like(dst_ref)

        hq, lbv = hq_ref[...], lb_ref[...]
        q, k, f, sg, sq, b, bl = _hg_prologue(hq, hf_ref[...], lbv, lower_ref[...], total_ref[...])
        eb = jnp.exp(b)
        ekb = jnp.exp(jnp.minimum(bl - b, 0.0))
        q_s[...] = q
        k_s[...] = k
        v_s[...] = hi_ref[...]
        b_s[...] = b
        eb_s[...] = eb
        ekb_s[...] = ekb
        qt_s[...] = q * eb
        kt_s[...] = k * ekb
        d_s[...] = jnp.exp(bl)
        rowi = lax.broadcasted_iota(jnp.int32, (HG_BLK, W), 0)
        last_row = rowi == HG_BLK - 1

        def block(i, slot):
            r0 = pl.multiple_of(i * HG_BLK, HG_BLK)
            rows = pl.ds(r0, HG_BLK)
            qi, ki, vi, bi, doi = q_s[rows, :], k_s[rows, :], v_s[rows, :], b_s[rows, :], do_ref[rows, :]
            for t in range(HG_BLK):
                sl = pl.ds(t * HG_BLK, HG_BLK)
                e = jnp.where(rowi <= t, jnp.exp(jnp.minimum(bi[t:t + 1, :] - bi, 0.0)), 0.0)
                e_s[slot, sl, :] = e
                p_s[slot, sl, :] = (e * qi[t:t + 1, :] * ki).astype(BF16)
                w_s[slot, sl, :] = (doi[t:t + 1, :] * vi).astype(BF16)
            a_b = _head_sums(p_s, slot, bd_ref[...])
            da_b = _head_sums(w_s, slot, bd_ref[...])
            x = da_b * e_s[slot]
            k_til = jnp.concatenate([ki] * HG_BLK, axis=0)
            q_rep = jnp.concatenate([jnp.broadcast_to(qi[t:t + 1, :], (HG_BLK, W)) for t in range(HG_BLK)], axis=0)
            do_rep = jnp.concatenate([jnp.broadcast_to(doi[t:t + 1, :], (HG_BLK, W)) for t in range(HG_BLK)], axis=0)
            dq_in = jnp.dot(selt_ref[...], (x * k_til).astype(BF16), preferred_element_type=F32)
            dk_in = jnp.dot(sels_ref[...], (x * q_rep).astype(BF16), preferred_element_type=F32)
            dv_in = jnp.dot(sels_ref[...], (a_b * do_rep).astype(BF16), preferred_element_type=F32)
            qti, kti, di = qt_s[rows, :], kt_s[rows, :], d_s[rows, :]
            dqt, dkt, dvt, dd = [], [], [], []
            for h in range(HG_HEADS):
                hs = slice(h * HG_DIM, (h + 1) * HG_DIM)
                st_h = st_in_ref[i, hs, :]
                dst_h = dst_ref[hs, :]
                do_h, v_h = doi[:, hs].astype(BF16), vi[:, hs].astype(BF16)
                dst_b = dst_h.astype(BF16)
                dqt.append(jnp.dot(do_h, st_h.astype(BF16), preferred_element_type=F32))
                dkt.append(jnp.dot(v_h, dst_b, preferred_element_type=F32))
                dvt.append(lax.dot_general(kti[:, hs].astype(BF16), dst_b, (((1,), (1,)), ((), ())),
                                           preferred_element_type=F32))
                dd.append(jnp.sum(dst_h * st_h, axis=0, keepdims=True))
                upd = lax.dot_general(do_h, qti[:, hs].astype(BF16), (((0,), (0,)), ((), ())),
                                      preferred_element_type=F32)
                dst_ref[hs, :] = dst_h * di[0:1, hs] + upd
            dqt = jnp.concatenate(dqt, axis=1)
            dkt = jnp.concatenate(dkt, axis=1)
            dvt = jnp.concatenate(dvt, axis=1)
            dd = jnp.concatenate(dd, axis=1)
            dbl = jnp.sum(dkt * kti, axis=0, keepdims=True) + dd * di[0:1, :]
            db = qi * dq_in - ki * dk_in + dqt * qti - dkt * kti
            db_s[rows, :] = db + jnp.where(last_row, dbl, 0.0)
            dq_s[rows, :] = dq_in + dqt * eb_s[rows, :]
            dk_s[rows, :] = dk_in + dkt * ekb_s[rows, :]
            dv_s[rows, :] = dv_in + dvt

        def some_blocks(jj, carry):
            for slot in range(HG_SLOTS):
                block(nb - 1 - slot - HG_SLOTS * jj, slot)
            return carry

        lax.fori_loop(0, nb // HG_SLOTS, some_blocks, 0)

        dg = _dot3(upper_ref[...], db_s[...])
        dhq_ref[...] = (dq_s[...] * (sq * (1.0 + hq * (1.0 - sq)))).astype(dhq_ref.dtype)
        df = dg / f - dk_s[...]
        dhf_ref[...] = (df * (1.0 - lbv) * (sg * (1.0 - sg))).astype(dhf_ref.dtype)
        dhi_ref[...] = dv_s[...].astype(dhi_ref.dtype)
        dlb_ref[...] += _colsum(df * (1.0 - sg))

    rev = lambda s, t: s * n_tiles + (n_tiles - 1 - t)
    col = lambda off: functools.partial(lambda s, t, blk: (rev(s, t), blk), blk=off // W)
    const = lambda m: pl.BlockSpec(m.shape, lambda s, t: (0, 0))
    row = pl.BlockSpec((HG_TILE, W), lambda s, t: (rev(s, t), 0))
    tile_f32 = pltpu.VMEM((HG_TILE, W), F32)
    n2 = HG_BLK * HG_BLK
    return pl.pallas_call(
        body, name=name,
        grid=(n_seq, n_tiles),
        in_specs=[pl.BlockSpec((HG_TILE, W), col(offs[0])), pl.BlockSpec((HG_TILE, W), col(offs[1])),
                  pl.BlockSpec((HG_TILE, W), col(offs[2])), row,
                  pl.BlockSpec((nb, W, HG_DIM), lambda s, t: (rev(s, t), 0, 0)),
                  const(lb), const(lower), const(upper), const(total), const(bd), const(sel_t), const(sel_s)],
        out_specs=[row, row, row, pl.BlockSpec((1, W), lambda s, t: (0, 0))],
        out_shape=[jax.ShapeDtypeStruct((T, W), BF16)] * 3 + [jax.ShapeDtypeStruct((1, W), F32)],
        scratch_shapes=[pltpu.VMEM((W, HG_DIM), F32)] + [tile_f32] * 13
                       + [pltpu.VMEM((HG_SLOTS, n2, W), BF16), pltpu.VMEM((HG_SLOTS, n2, W), F32),
                          pltpu.VMEM((HG_SLOTS, n2, W), BF16)],
        compiler_params=_cparams(("arbitrary", "arbitrary")),
    )(proj, proj, proj, do, states, lb, lower, upper, total, bd, sel_t, sel_s)


def _diag_mask(tq):
    return lax.broadcasted_iota(jnp.int32, (tq, tq), 1) <= lax.broadcasted_iota(jnp.int32, (tq, tq), 0)


def _qk(q, k):
    return lax.dot_general(q, k, (((1,), (1,)), ((), ())), preferred_element_type=F32)


def _causal_pairs(n, sweeps=1, by_key=False):
    if by_key:
        rows = [(i, j, 0) for j in range(n) for i in range(j, n)]
    else:
        rows = [(i, j, s) for i in range(n) for s in range(sweeps) for j in range(i + 1)]
    return tuple(jnp.asarray(np.array([r[c] for r in rows], np.int32)) for c in range(3))


def _fox_placement(fh):
    hw, wa = fh * FOX_HDIM, fh * FOX_AUG
    pq, pk = np.zeros((hw, wa), np.float32), np.zeros((hw, wa), np.float32)
    aq, ak = np.zeros((3 * LANES, wa), np.float32), np.zeros((3 * LANES, wa), np.float32)
    oq, ok = np.zeros((1, wa), np.float32), np.zeros((1, wa), np.float32)
    for h in range(fh):
        src, dst = np.arange(h * FOX_HDIM, (h + 1) * FOX_HDIM), np.arange(h * FOX_AUG, h * FOX_AUG + FOX_HDIM)
        pq[src, dst] = FOX_HDIM ** -0.5
        pk[src, dst] = 1.0
        gate = h * FOX_AUG + FOX_HDIM
        for r in range(3):
            aq[r * LANES + h, gate + r] = 1.0
            ak[r * LANES + h, gate + 3 + r] = -1.0
        oq[0, gate + 3:gate + 6] = 1.0
        ok[0, gate:gate + 3] = 1.0
    bf = lambda m: jnp.asarray(m, dtype=BF16)
    return {"pq": bf(pq), "pk": bf(pk), "aq": bf(aq), "ak": bf(ak), "oq": jnp.asarray(oq), "ok": jnp.asarray(ok),
            "pqt": bf(pq.T), "pkt": bf(pk.T)}


def _fox_specs(tq, fh, heads=1):
    groups = fh // heads

    def spec(tab):
        return pl.BlockSpec((None, tq, heads * FOX_AUG), lambda b, t, *tabs: (b // groups, tabs[tab][t], b % groups))
    return spec(0), spec(1)


def fox_fwd(qa, ka, va, *, name):
    n_seq, S, width = qa.shape
    fh = width // FOX_AUG
    nh = FOX_FWD_HEADS
    BH = n_seq * fh // nh
    tq = min(FOX_TQ, S)
    itab, jtab, _ = _causal_pairs(S // tq)

    def body(itab_ref, jtab_ref, q_ref, k_ref, v_ref, o_ref, ox_ref, lse_ref, *scratch):
        t = pl.program_id(1)
        i, j = itab_ref[t], jtab_ref[t]
        per_head = [scratch[4 * h:4 * h + 4] for h in range(nh)]

        @pl.when(j == 0)
        def _():
            for m_s, l_s, acc_s, acc_lo_s in per_head:
                m_s[...] = jnp.full_like(m_s, NEG_INF)
                l_s[...] = jnp.zeros_like(l_s)
                acc_s[...] = jnp.zeros_like(acc_s)
                acc_lo_s[...] = jnp.zeros_like(acc_lo_s)

        def step(on_diagonal):
            for h, (m_s, l_s, acc_s, acc_lo_s) in enumerate(per_head):
                lanes = slice(h * FOX_AUG, (h + 1) * FOX_AUG)
                s = _qk(q_ref[:, lanes], k_ref[:, lanes])
                if on_diagonal:
                    s = jnp.where(_diag_mask(tq), s, NEG_INF)
                m_prev = m_s[...]
                m_new = jnp.maximum(m_prev, jnp.max(s, axis=-1, keepdims=True))
                alpha = jnp.exp(m_prev - m_new)
                p = jnp.exp(s - m_new[:, 0:1])
                p_hi = p.astype(BF16)
                p_lo = (p - p_hi.astype(F32)).astype(BF16)
                v = v_ref[:, lanes]
                l_s[...] = alpha * l_s[...] + jnp.sum(p, axis=-1, keepdims=True)
                acc_s[...] = alpha * acc_s[...] + jnp.dot(p_hi, v, preferred_element_type=F32)
                acc_lo_s[...] = alpha * acc_lo_s[...] + jnp.dot(p_lo, v, preferred_element_type=F32)
                m_s[...] = m_new

        @pl.when(j < i)
        def _():
            step(False)

        @pl.when(j == i)
        def _():
            step(True)
            for h, (m_s, l_s, acc_s, acc_lo_s) in enumerate(per_head):
                lanes = slice(h * FOX_AUG, (h + 1) * FOX_AUG)
                inv_l = 1.0 / l_s[...]
                o_ref[:, lanes] = (acc_s[...] * inv_l).astype(o_ref.dtype)
                ox_ref[:, lanes] = (acc_s[...] + acc_lo_s[...]) * inv_l
                lse_ref[:, lanes] = m_s[...] + jnp.log(l_s[...])

    qspec, kspec = _fox_specs(tq, fh, nh)
    wide = jax.ShapeDtypeStruct((n_seq, S, width), F32)
    return pl.pallas_call(
        body, name=name,
        grid_spec=pltpu.PrefetchScalarGridSpec(
            num_scalar_prefetch=2, grid=(BH, itab.shape[0]),
            in_specs=[qspec, kspec, kspec],
            out_specs=[qspec, qspec, qspec],
            scratch_shapes=[pltpu.VMEM((tq, LANES), F32)] * (4 * nh)),
        out_shape=[jax.ShapeDtypeStruct((n_seq, S, width), BF16), wide, wide],
        compiler_params=_cparams(("parallel", "arbitrary")),
    )(itab, jtab, qa, ka, va)


def _fox_ds(q, k, v, do, ox, lse, on_diagonal):
    s = _qk(q, k)
    if on_diagonal:
        s = jnp.where(_diag_mask(s.shape[0]), s, NEG_INF)
    p = jnp.exp(s - lse[:, 0:1])
    delta = jnp.sum(do.astype(F32) * ox, axis=-1, keepdims=True)
    return p, p * (_qk(do, v) - delta)


def fox_bwd(qa, ka, va, do, ox, lse, *, name):
    n_seq, S, width = qa.shape
    fh = width // FOX_AUG
    BH = n_seq * fh
    tq = min(FOX_TQ, S)
    itab, jtab, _ = _causal_pairs(S // tq)

    def body(itab_ref, jtab_ref, q_ref, k_ref, v_ref, do_ref, ox_ref, lse_ref, dq_ref, dk_ref, dv_ref, dsum_ref):
        t = pl.program_id(1)
        i, j = itab_ref[t], jtab_ref[t]

        @pl.when(t == 0)
        def _():
            dq_ref[...] = jnp.zeros_like(dq_ref)
            dk_ref[...] = jnp.zeros_like(dk_ref)
            dv_ref[...] = jnp.zeros_like(dv_ref)
            dsum_ref[...] = jnp.zeros_like(dsum_ref)

        q_rows = pl.ds(pl.multiple_of(i * tq, tq), tq)
        k_rows = pl.ds(pl.multiple_of(j * tq, tq), tq)

        def step(on_diagonal):
            q, k, do = q_ref[...], k_ref[...], do_ref[...]
            p, ds = _fox_ds(q, k, v_ref[...], do, ox_ref[...], lse_ref[...], on_diagonal)
            ds_b = ds.astype(BF16)
            tn = (((0,), (0,)), ((), ()))
            dq_ref[q_rows, :] += jnp.dot(ds_b, k, preferred_element_type=F32)
            dk_ref[k_rows, :] += lax.dot_general(ds_b, q, tn, preferred_element_type=F32)
            dv_ref[k_rows, :] += lax.dot_general(p.astype(BF16), do, tn, preferred_element_type=F32)
            dsum_ref[:, k_rows] += _colsum(ds)

        @pl.when(j < i)
        def _():
            step(False)

        @pl.when(j == i)
        def _():
            step(True)

    qspec, kspec = _fox_specs(tq, fh)
    whole = pl.BlockSpec((None, S, FOX_AUG), lambda b, t, it, jt: (b // fh, 0, b % fh))
    wide = jax.ShapeDtypeStruct((n_seq, S, width), F32)
    return pl.pallas_call(
        body, name=name,
        grid_spec=pltpu.PrefetchScalarGridSpec(
            num_scalar_prefetch=2, grid=(BH, itab.shape[0]),
            in_specs=[qspec, kspec, kspec, qspec, qspec, qspec],
            out_specs=[whole, whole, whole, pl.BlockSpec((None, 1, S), lambda b, t, it, jt: (b, 0, 0))]),
        out_shape=[wide, wide, wide, jax.ShapeDtypeStruct((BH, 1, S), F32)],
        compiler_params=_cparams(("parallel", "arbitrary")),
    )(itab, jtab, qa, ka, va, do, ox, lse)


def seq_cumsum(x, n_seq, seq, *, reverse, name):
    T, C = x.shape
    tb = min(256, seq)
    n = seq // tb
    r = np.arange(tb)
    tri = (r[None, :] >= r[:, None]) if reverse else (r[None, :] <= r[:, None])
    tri = jnp.asarray(tri.astype(np.float32), dtype=BF16)

    def body(x_ref, tri_ref, o_ref, carry_s):
        @pl.when(pl.program_id(1) == 0)
        def _():
            carry_s[...] = jnp.zeros_like(carry_s)

        xv = x_ref[...]
        o_ref[...] = _dot3(tri_ref[...], xv) + carry_s[...]
        carry_s[...] += _colsum(xv)

    blk = (lambda s, t: (s * n + (n - 1 - t), 0)) if reverse else (lambda s, t: (s * n + t, 0))
    return pl.pallas_call(
        body, name=name,
        grid=(n_seq, n),
        in_specs=[pl.BlockSpec((tb, C), blk), pl.BlockSpec((tb, tb), lambda s, t: (0, 0))],
        out_specs=pl.BlockSpec((tb, C), blk),
        out_shape=jax.ShapeDtypeStruct((T, C), F32),
        scratch_shapes=[pltpu.VMEM((1, C), F32)],
        compiler_params=_cparams(("arbitrary", "arbitrary")),
    )(x, tri)


def _place():
    return lax.axis_index("x"), lax.axis_index("y"), lax.axis_index("c")


def _other_chips(x, y):
    return [(1 - x, y), (x, 1 - y), (1 - x, 1 - y)]


def _hbm_call(body, ins, out_shape, n_sems, *, name):
    hbm = pl.BlockSpec(memory_space=pl.ANY)
    return pl.pallas_call(
        body, name=name,
        in_specs=[hbm] * len(ins), out_specs=[hbm] * len(out_shape), out_shape=out_shape,
        scratch_shapes=[pltpu.SemaphoreType.DMA((n_sems,)), pltpu.SemaphoreType.DMA((n_sems,)),
                        pltpu.SemaphoreType.DMA((len(ins),))],
        compiler_params=pltpu.CompilerParams(has_side_effects=True),
    )(*ins)


def allgather_chips(shards, *, name):
    return _exchange_call(allgather_rider(shards), name=name)


def _allgather_ops(x_refs, o_refs, send_sems, recv_sems, local_sems):
    def copies():
        x, y, c = _place()
        me = 2 * x + y
        chips = _other_chips(x, y)
        own, first, passed, landed, handed = [], [], [], [], []
        for b, (x_ref, o_ref) in enumerate(zip(x_refs, o_refs)):
            half = x_ref.shape[0] // 2
            mine, theirs = pl.ds(c * half, half), pl.ds((1 - c) * half, half)
            own.append(pltpu.make_async_copy(x_ref, o_ref.at[me], local_sems.at[b]))

            def copy(k, src, chip, rows, to, o_ref=o_ref, b=b):
                return pltpu.make_async_remote_copy(src_ref=src, dst_ref=o_ref.at[2 * chip[0] + chip[1], rows],
                                                    send_sem=send_sems.at[6 * b + k], recv_sem=recv_sems.at[6 * b + k],
                                                    device_id=to, device_id_type=MESH)
            for j, chip in enumerate(chips):
                first.append(copy(j, x_ref.at[mine], (x, y), mine, (*chip, c)))
                landed.append(copy(j, x_ref.at[mine], chip, mine, (*chip, c)))
                passed.append(copy(3 + j, o_ref.at[2 * chip[0] + chip[1], mine], chip, mine, (x, y, 1 - c)))
                handed.append(copy(3 + j, x_ref.at[mine], chip, theirs, (x, y, 1 - c)))
        return own, first, passed, landed, handed

    def start():
        own, first, _, _, _ = copies()
        for cp in own + first:
            cp.start()

    def finish():
        own, first, passed, landed, handed = copies()
        for arrived, forward in zip(landed, passed):
            arrived.wait_recv()
            forward.start()
        for cp in handed:
            cp.wait_recv()
        for cp in first + passed:
            cp.wait_send()
        for cp in own:
            cp.wait()
    return start, finish


def _scatter_ops(x_refs, o_refs, send_sems, recv_sems, local_sems):
    def copies():
        x, y, c = _place()
        return [pltpu.make_async_remote_copy(
            src_ref=x_ref.at[2 * px + py], dst_ref=o_ref.at[j], send_sem=send_sems.at[3 * b + j],
            recv_sem=recv_sems.at[3 * b + j], device_id=(px, py, c), device_id_type=MESH)
            for b, (x_ref, o_ref) in enumerate(zip(x_refs, o_refs)) for j, (px, py) in enumerate(_other_chips(x, y))]

    def start():
        for cp in copies():
            cp.start()

    def finish():
        sends = copies()
        for cp in sends:
            cp.wait_recv()
        for cp in sends:
            cp.wait_send()
    return start, finish


class Rider(NamedTuple):
    ins: list
    out_shape: list
    n_sems: int
    ops: object

    def specs(self):
        hbm = pl.BlockSpec(memory_space=pl.ANY)
        sems = [pltpu.SemaphoreType.DMA((self.n_sems,)), pltpu.SemaphoreType.DMA((self.n_sems,)),
                pltpu.SemaphoreType.DMA((len(self.ins),))]
        return [hbm] * len(self.ins), [hbm] * len(self.out_shape), sems

    def wrap(self, body, n_in, n_out, grid_rank):
        k_in, k_out = len(self.ins), len(self.out_shape)

        def carried(*refs):
            ins, r_ins = refs[:n_in], refs[n_in:n_in + k_in]
            outs = refs[n_in + k_in:n_in + k_in + n_out]
            r_outs = refs[n_in + k_in + n_out:n_in + k_in + n_out + k_out]
            scratch, sems = refs[n_in + k_in + n_out + k_out:-3], refs[-3:]
            first = functools.reduce(jnp.logical_and, [pl.program_id(a) == 0 for a in range(grid_rank)])
            last = functools.reduce(jnp.logical_and,
                                    [pl.program_id(a) == pl.num_programs(a) - 1 for a in range(grid_rank)])
            pl.when(first)(lambda: self.ops(r_ins, r_outs, *sems)[0]())
            body(*ins, *outs, *scratch)
            pl.when(last)(lambda: self.ops(r_ins, r_outs, *sems)[1]())
        return carried


def _exchange_call(rider, *, name):
    def body(*refs):
        k = len(rider.ins)
        start, finish = rider.ops(refs[:k], refs[k:k + len(rider.out_shape)], *refs[-3:])
        start()
        finish()
    in_specs, out_specs, sems = rider.specs()
    return pl.pallas_call(body, name=name, in_specs=in_specs, out_specs=out_specs, out_shape=rider.out_shape,
                          scratch_shapes=sems, compiler_params=pltpu.CompilerParams(has_side_effects=True))(*rider.ins)


def allgather_rider(shards):
    assert all(s.shape[0] % (2 * ROW_ALIGN) == 0 for s in shards)
    return Rider(list(shards), [jax.ShapeDtypeStruct((4,) + s.shape, s.dtype) for s in shards], 6 * len(shards),
                 _allgather_ops)


def scatter_rider(parts):
    return Rider(list(parts), [jax.ShapeDtypeStruct((3,) + p.shape[1:], p.dtype) for p in parts], 3 * len(parts),
                 _scatter_ops)


def scatter_chips(parts, *, name):
    return _exchange_call(scatter_rider(parts), name=name)


def swap_cores(vs, *, name):
    nb = len(vs)

    def body(*refs):
        x_refs, o_refs = refs[:nb], refs[nb:2 * nb]
        send_sems, recv_sems, _ = refs[2 * nb:]
        x, y, c = _place()
        copies = [pltpu.make_async_remote_copy(src_ref=x_ref, dst_ref=o_ref, send_sem=send_sems.at[b],
                                               recv_sem=recv_sems.at[b], device_id=(x, y, 1 - c), device_id_type=MESH)
                  for b, (x_ref, o_ref) in enumerate(zip(x_refs, o_refs))]
        for cp in copies:
            cp.start()
        for cp in copies:
            cp.wait()

    return _hbm_call(body, vs, [jax.ShapeDtypeStruct(v.shape, v.dtype) for v in vs], nb, name=name)


def allreduce_small(v, *, name):
    R, C = v.shape

    def body(x_ref, o_ref, gath_ref, send_sems, recv_sems):
        x, y, c = _place()
        me = 4 * x + 2 * y + c
        gath_ref[me] = x_ref[...]
        flips = [(k >> 2 & 1, k >> 1 & 1, k & 1) for k in range(1, 8)]
        sends = []
        for j, (fx, fy, fc) in enumerate(flips):
            peer = (x ^ fx, y ^ fy, c ^ fc)
            cp = pltpu.make_async_remote_copy(src_ref=x_ref, dst_ref=gath_ref.at[me], send_sem=send_sems.at[j],
                                              recv_sem=recv_sems.at[j], device_id=peer, device_id_type=MESH)
            cp.start()
            sends.append(cp)
        for j, (fx, fy, fc) in enumerate(flips):
            peer = (x ^ fx, y ^ fy, c ^ fc)
            pltpu.make_async_remote_copy(src_ref=x_ref, dst_ref=gath_ref.at[4 * peer[0] + 2 * peer[1] + peer[2]],
                                         send_sem=send_sems.at[j], recv_sem=recv_sems.at[j], device_id=peer,
                                         device_id_type=MESH).wait_recv()
        for cp in sends:
            cp.wait_send()
        total = gath_ref[0]
        for d in range(1, 8):
            total = total + gath_ref[d]
        o_ref[...] = total

    vm = pl.BlockSpec(memory_space=pltpu.VMEM)
    out, _ = pl.pallas_call(
        body, name=name,
        in_specs=[vm], out_specs=[vm, vm],
        out_shape=[jax.ShapeDtypeStruct((R, C), F32), jax.ShapeDtypeStruct((8, R, C), F32)],
        scratch_shapes=[pltpu.SemaphoreType.DMA((7,)), pltpu.SemaphoreType.DMA((7,))],
        compiler_params=pltpu.CompilerParams(has_side_effects=True),
    )(v)
    return out


ROW_ALIGN = 16
PACK_W = 1024
SUM_TILE = 512
BIG_WEIGHTS = (("w_in", 1), ("w_a", 1), ("w_b", 1), ("w_o", 0), ("w_ff1", 1), ("w_ff2", 0), ("w_pg", 0), ("w_p", 1))


def _b_layout(d, ple):
    hw, q = d // 2, d // 4
    small = 2 * d + 2 * q
    lay = {"w_ff1": (0, 0, d, d), "w_ff2": (d, 0, d, d), "w_o": (2 * d, 0, q, d), "w_pg": (2 * d + q, 0, q, d),
           "w_a": (small, 0, hw, q), "w_b": (small, q, hw, q), "w_p": (small, 2 * q, ple, q)}
    return lay, small + hw


def pack_a(w_in_shard):
    rows, cols = w_in_shard.shape
    pad = -cols % LANES
    return jnp.concatenate([w_in_shard, jnp.zeros((rows, pad), w_in_shard.dtype)], axis=1)


def pack_b(shards, d):
    hw, q = d // 2, d // 4
    dt = shards["w_a"].dtype
    wp = shards["w_p"]
    wp = jnp.concatenate([wp, jnp.zeros((hw - wp.shape[0], q), dt)], axis=0)
    small = jnp.concatenate([shards["w_a"], shards["w_b"], wp, jnp.zeros((hw, d - 3 * q), dt)], axis=1)
    return jnp.concatenate([shards["w_ff1"], shards["w_ff2"], shards["w_o"], shards["w_pg"], small], axis=0)


def unpack_b(buf, lay):
    return {nm: buf[r0:r0 + rows, c0:c0 + cols] for nm, (r0, c0, rows, cols) in lay.items()}


def _win_layout(d):
    hw = d // 2
    fh = hw // FOX_HDIM
    orig = {"hq": (0, hw), "hf": (hw, hw), "hi": (2 * hw, hw), "hg": (3 * hw, hw), "fq": (4 * hw, hw),
            "fk": (5 * hw, hw), "fv": (6 * hw, hw), "ff": (7 * hw, fh), "ga": (7 * hw + fh, d), "gb": (7 * hw + fh + d, d)}
    order = ["ga", "gb", "hq", "hf", "hi", "hg", "fq", "fk", "fv", "ff"]
    mine, off = {}, 0
    for nm in order:
        width = orig[nm][1] if nm != "ff" else LANES
        mine[nm] = (off, width)
        off += width
    return orig, order, mine, off


def _adam_fn(rows, vecs):
    w, g, m, v = rows
    m2 = ADAM_B1 * m + (1.0 - ADAM_B1) * g
    v2 = ADAM_B2 * v + (1.0 - ADAM_B2) * (g * g)
    m_hat = m2 / (1.0 - ADAM_B1 ** ADAM_STEP)
    v_hat = v2 / (1.0 - ADAM_B2 ** ADAM_STEP)
    delta = -ADAM_LR * (m_hat / (jnp.sqrt(v_hat) + ADAM_EPS) + ADAM_WD * w)
    return [delta, m2, v2], []


def adamw(w, g, m, v, *, name):
    c = w.shape[1]
    (delta, m2, v2), _ = rowwise(_adam_fn, [w, g, m, v], [], [(c, F32)] * 3, name=name, tm=256)
    return delta, m2, v2


def kernel(x, p, ln0_g, ln0_b, w_in, hg_lb, hg_norm_g, fox_fb, w_a, w_b, w_o, ln1_g, ln1_b, w_ff1, w_ff2, w_pg, w_p, ln2_g, ln2_b, loss_target, m_ln0_g, m_ln0_b, m_w_in, m_hg_lb, m_hg_norm_g, m_fox_fb, m_w_a, m_w_b, m_w_o, m_ln1_g, m_ln1_b, m_w_ff1, m_w_ff2, m_w_pg, m_w_p, m_ln2_g, m_ln2_b, v_ln0_g, v_ln0_b, v_w_in, v_hg_lb, v_hg_norm_g, v_fox_fb, v_w_a, v_w_b, v_w_o, v_ln1_g, v_ln1_b, v_w_ff1, v_w_ff2, v_w_pg, v_w_p, v_ln2_g, v_ln2_b):
    n_seq, seq, d = x.shape
    T = n_seq * seq
    hw = d // 2
    fh = hw // FOX_HDIM
    bh = n_seq * fh
    orig, order, mine, n_in = _win_layout(d)

    big = {"w_in": w_in[0], "w_a": w_a[0], "w_b": w_b[0], "w_o": w_o[0], "w_ff1": w_ff1[0], "w_ff2": w_ff2[0],
           "w_pg": w_pg[0], "w_p": w_p[0]}
    big_m = {"w_in": m_w_in[0], "w_a": m_w_a[0], "w_b": m_w_b[0], "w_o": m_w_o[0], "w_ff1": m_w_ff1[0],
             "w_ff2": m_w_ff2[0], "w_pg": m_w_pg[0], "w_p": m_w_p[0]}
    big_v = {"w_in": v_w_in[0], "w_a": v_w_a[0], "w_b": v_w_b[0], "w_o": v_w_o[0], "w_ff1": v_w_ff1[0],
             "w_ff2": v_w_ff2[0], "w_pg": v_w_pg[0], "w_p": v_w_p[0]}
    names = [nm for nm, _ in BIG_WEIGHTS]
    axis = dict(BIG_WEIGHTS)
    ple = w_p.shape[1]
    lay, b_rows = _b_layout(d, ple)
    in_cols = big["w_in"].shape[1]

    (a_all,) = allgather_chips([pack_a(big["w_in"].astype(BF16))], name="allgather_w_in")
    gather_rest = allgather_rider([pack_b({nm: big[nm].astype(BF16) for nm in names if nm != "w_in"}, d)])
    win = jnp.concatenate([a_all[s, :, :in_cols] for s in range(4)], axis=1)
    win_mine = jnp.concatenate(
        [win[:, orig[nm][0]:orig[nm][0] + orig[nm][1]] for nm in order]
        + [jnp.zeros((d, LANES - fh), BF16)], axis=1)

    x2 = x.reshape(T, d)
    tgt = loss_target.reshape(T, d)
    p_b = p.reshape(T, p.shape[-1]).astype(BF16)
    vec = lambda a: a.reshape(1, -1)
    probs = jax.nn.softmax(hg_lb, axis=0)
    lb = vec(probs[0])

    def ln0_fn(rows, vecs):
        h = _ln_stats(rows[0]) * vecs[0] + vecs[1]
        return [h, h], []
    (h0, h0b), _ = rowwise(ln0_fn, [x2], [vec(ln0_g), vec(ln0_b)], [(d, F32), (d, BF16)], name="ln0_fwd")
    proj = matmul_nn(h0b, win_mine, name="in_proj")

    o_raw, hg_states, (b_all,) = hgrn2_fwd(proj, [mine["hq"][0], mine["hf"][0], mine["hi"][0]], lb, n_seq, seq,
                                           name="hgrn2_fwd", rider=gather_rest)
    view = lambda nm, k, n: WView(b_all, lay[nm][0], lay[nm][1], k, n, axis[nm])
    w_ff1_v, w_ff2_v = view("w_ff1", d, 4 * d), view("w_ff2", 4 * d, d)

    def whole(nm):
        r0, c0, rows, cols = lay[nm]
        return jnp.concatenate([b_all[s, r0:r0 + rows, c0:c0 + cols] for s in range(4)], axis=axis[nm])
    w_o_v, w_pg_v, w_a_v, w_p_v, w_b_full = whole("w_o"), whole("w_pg"), whole("w_a"), whole("w_p"), whole("w_b")

    def ya_fn(rows, vecs):
        o, hg = rows
        outs = []
        for h in range(HG_HEADS):
            oh = o[:, h * HG_DIM:(h + 1) * HG_DIM]
            outs.append(oh * lax.rsqrt(jnp.mean(oh * oh, axis=-1, keepdims=True) + RMS_EPS))
        y = jnp.concatenate(outs, axis=1) * vecs[0] * (hg * _sigmoid(hg))
        return [y], []
    (y_a,), _ = rowwise(ya_fn, [o_raw, (proj,) + mine["hg"]], [hg_norm_g], [(hw, BF16)], name="hgrn2_out_fwd")

    fb_pad = jnp.concatenate([fox_fb, jnp.zeros((1, LANES - fh), F32)], axis=1)

    def lf_fn(rows, vecs):
        u = rows[0] + vecs[0]
        return [jnp.minimum(u, 0.0) - jnp.log(1.0 + jnp.exp(-jnp.abs(u)))], []
    (lf,), _ = rowwise(lf_fn, [(proj,) + mine["ff"]], [fb_pad], [(LANES, F32)], name="fox_logf")
    c_cum = seq_cumsum(lf, n_seq, seq, reverse=False, name="fox_cumsum")

    place = _fox_placement(fh)

    def prep_fn(rows, vecs):
        fq_, fk_, fv_, cc = rows
        pq, pk, aq, ak, oq, ok = vecs
        parts = jnp.concatenate(_split3(cc), axis=1)
        mm = lambda a_, b_: jnp.dot(a_, b_, preferred_element_type=F32)
        q_ = mm(fq_.astype(BF16), pq) + mm(parts, aq) + oq
        k_ = mm(fk_.astype(BF16), pk) + mm(parts, ak) + ok
        return [q_, k_, mm(fv_.astype(BF16), pk)], []
    wa = fh * FOX_AUG
    (qa, ka, va), _ = rowwise(prep_fn, [(proj,) + mine["fq"], (proj,) + mine["fk"], (proj,) + mine["fv"], c_cum],
                              [place[nm] for nm in ("pq", "pk", "aq", "ak", "oq", "ok")], [(wa, BF16)] * 3,
                              name="fox_prep")
    as_seq = lambda t2d: t2d.reshape(n_seq, seq, t2d.shape[1])
    o_fox, ox_fox, lse = fox_fwd(as_seq(qa), as_seq(ka), as_seq(va), name="fox_fwd")
    y_b = o_fox.reshape(T, wa)
    wb_pad = jnp.concatenate([w_b_full.reshape(fh, FOX_HDIM, d), jnp.zeros((fh, FOX_AUG - FOX_HDIM, d), BF16)],
                             axis=1).reshape(wa, d)

    pa = matmul_nn(y_a, w_a_v, name="proj_a")
    pb = matmul_nn(y_b, wb_pad, name="proj_b")

    def merge_fn(rows, vecs):
        ga, gb, a, b = rows
        return [_sigmoid(ga) * a + _sigmoid(gb) * b], []
    (merged,), _ = rowwise(merge_fn, [(proj,) + mine["ga"], (proj,) + mine["gb"], pa, pb], [], [(d, BF16)],
                           name="merge_fwd")
    fused_tm = 512

    def ln1_post(mix, aux, vecs):
        z = ALPHA * aux[0] + mix
        h = _ln_stats(z) * vecs[0] + vecs[1]
        return [z, h, h], []
    (z1, h1, h1b), _ = matmul_nn(merged, w_o_v, name="out_proj_ln1", tm=fused_tm, post=ln1_post, post_aux=[h0],
                                 post_vecs=[ln1_g, ln1_b], post_outs=[F32, F32, BF16])

    relu2 = lambda u: jnp.square(jnp.maximum(u, 0.0))
    act = matmul_nn(h1b, w_ff1_v, name="ff1", out_dtype=BF16, epilogue=relu2)
    pg = matmul_nn(h1b, w_pg_v, name="ple_gate")
    pe = matmul_nn(p_b, w_p_v, name="ple_embed")

    def head_post(ffv, aux, vecs):
        h1v, pgv, pev, t = aux
        g2, b2 = vecs
        sp = _sigmoid(pgv)
        z = ALPHA * h1v + ffv + sp * pev
        y = _ln_stats(z) * g2 + b2
        err = y - t
        loss_rows = 0.5 * jnp.mean(err * err, axis=-1, keepdims=True)
        dy = err * (1.0 / d)
        dz, dg2, db2 = _ln_bwd(z, dy, g2)
        loss_acc = jnp.broadcast_to(_colsum(loss_rows), (1, d))
        return [dz, dz, dz * pev * (sp * (1.0 - sp)), dz * sp], [dg2, db2, loss_acc]
    (dz2, dz2b, dpg, dpe), (g_ln2_g, g_ln2_b, loss_part) = matmul_nn(
        act, w_ff2_v, name="ff2_head", tm=fused_tm, post=head_post, post_aux=[h1, pg, pe, tgt],
        post_vecs=[ln2_g, ln2_b], post_outs=[F32, BF16, BF16, BF16], post_accs=[d, d, d])

    dact = lambda da, a: da * (2.0 * jnp.sqrt(a.astype(F32)))
    du = matmul_nn(dz2b, w_ff2_v, transpose_rhs=True, name="d_ff2", out_dtype=BF16, epilogue=dact, aux=act)
    dh1_pg = matmul_nn(dpg, w_pg_v, transpose_rhs=True, name="d_ple_gate")

    def ln1_bwd_post(dh1_ff, aux, vecs):
        dh1 = ALPHA * aux[0] + dh1_ff + aux[1]
        dz, dg, db = _ln_bwd(aux[2], dh1, vecs[0])
        return [dz, dz], [dg, db]
    (dz1, dz1b), (g_ln1_g, g_ln1_b) = matmul_nn(
        du, w_ff1_v, transpose_rhs=True, name="d_ff1_ln1", tm=fused_tm, post=ln1_bwd_post, post_aux=[dz2, dh1_pg, z1],
        post_vecs=[ln1_g], post_outs=[F32, BF16], post_accs=[d, d])

    def merge_bwd_post(dm, aux, vecs):
        ga, gb, a, b = aux
        sa, sb = _sigmoid(ga), _sigmoid(gb)
        return [dm * a * (sa * (1.0 - sa)), dm * b * (sb * (1.0 - sb)), dm * sa, dm * sb], []
    (dga, dgb, dma, dmb), _ = matmul_nn(
        dz1b, w_o_v, transpose_rhs=True, name="d_out_proj_merge", tm=fused_tm, post=merge_bwd_post,
        post_aux=[(proj,) + mine["ga"], (proj,) + mine["gb"], pa, pb], post_outs=[BF16] * 4)
    dya = matmul_nn(dma, w_a_v, transpose_rhs=True, name="d_proj_a")
    dyb = matmul_nn(dmb, wb_pad, transpose_rhs=True, name="d_proj_b", out_dtype=BF16)

    def ya_bwd_fn(rows, vecs):
        o, hg, dy = rows
        ng = vecs[0]
        sg = _sigmoid(hg)
        gate = hg * sg
        dn_parts, do_parts, n_parts = [], [], []
        for h in range(HG_HEADS):
            hs = slice(h * HG_DIM, (h + 1) * HG_DIM)
            oh = o[:, hs]
            r = lax.rsqrt(jnp.mean(oh * oh, axis=-1, keepdims=True) + RMS_EPS)
            nh = oh * r
            dn = dy[:, hs] * ng[:, hs] * gate[:, hs]
            do_parts.append(r * (dn - nh * jnp.mean(dn * nh, axis=-1, keepdims=True)))
            n_parts.append(nh)
        nrm = jnp.concatenate(n_parts, axis=1)
        dhg = dy * nrm * ng * (sg * (1.0 + hg * (1.0 - sg)))
        return [jnp.concatenate(do_parts, axis=1), dhg], [_colsum(dy * nrm * gate)]
    (do_raw, dhg), (g_norm_g,) = rowwise(ya_bwd_fn, [o_raw, (proj,) + mine["hg"], dya], [hg_norm_g],
                                         [(hw, F32), (hw, BF16)], [hw], name="hgrn2_out_bwd")
    dhq, dhf, dhi, g_lb = hgrn2_bwd(proj, [mine["hq"][0], mine["hf"][0], mine["hi"][0]], lb, do_raw, hg_states,
                                    n_seq, seq, name="hgrn2_bwd")

    do_fox = as_seq(dyb)
    dqa, dka, dva, dsum = fox_bwd(as_seq(qa), as_seq(ka), as_seq(va), do_fox, ox_fox, lse, name="fox_bwd")

    def unprep_fn(rows, vecs):
        mm = lambda a_, b_: jnp.dot(a_.astype(BF16), b_, preferred_element_type=F32)
        return [mm(rows[0], vecs[0]), mm(rows[1], vecs[1]), mm(rows[2], vecs[1])], []
    (dfq, dfk, dfv), _ = rowwise(unprep_fn, [dqa.reshape(T, wa), dka.reshape(T, wa), dva.reshape(T, wa)],
                                 [place["pqt"], place["pkt"]], [(hw, BF16)] * 3, name="fox_unprep")
    dc = -dsum.reshape(n_seq, fh, seq).transpose(0, 2, 1).reshape(T, fh)
    dc = jnp.concatenate([dc, jnp.zeros((T, LANES - fh), F32)], axis=1)
    dlf = seq_cumsum(dc, n_seq, seq, reverse=True, name="fox_cumsum_bwd")

    def lf_bwd_fn(rows, vecs):
        u = rows[0] + vecs[0]
        du_ = rows[1] * _sigmoid(-u)
        return [du_], [_colsum(du_)]
    (dff_,), (g_fb,) = rowwise(lf_bwd_fn, [(proj,) + mine["ff"], dlf], [fb_pad], [(LANES, BF16)], [LANES],
                               name="fox_logf_bwd")

    dproj = jnp.concatenate([dga, dgb, dhq, dhf, dhi, dhg, dfq, dfk, dfv, dff_], axis=1)

    gfull = {
        "w_a": matmul_tn(y_a, dma, name="g_w_a"),
        "w_b": matmul_tn(y_b, dmb, name="g_w_b").reshape(fh, FOX_AUG, d)[:, :FOX_HDIM].reshape(hw, d),
        "w_o": matmul_tn(merged, dz1b, name="g_w_o"),
        "w_ff1": matmul_tn(h1b, du, name="g_w_ff1"),
        "w_ff2": matmul_tn(act, dz2b, name="g_w_ff2"),
        "w_pg": matmul_tn(h1b, dpg, name="g_w_pg"),
        "w_p": matmul_tn(p_b, dpe, name="g_w_p"),
    }

    def chip_parts(nm, s):
        g = gfull[nm]
        n = g.shape[axis[nm]] // 4
        return lax.slice_in_dim(g, s * n, (s + 1) * n, axis=axis[nm])
    me = 2 * lax.axis_index("x") + lax.axis_index("y")
    core = lax.axis_index("c")

    def sum2_fn(rows, vecs):
        s = rows[0] + rows[1].astype(F32)
        return [s, s], []

    def sum4_fn(rows, vecs):
        a, r0, r1, r2 = rows
        return [((a + r0.astype(F32)) + r1.astype(F32)) + r2.astype(F32)], []

    def chip_pair_sum(g, tag):
        h, cols = g.shape[1] // 2, g.shape[2]
        keep = lax.dynamic_slice_in_dim(g, core * h, h, axis=1)
        give = lax.dynamic_slice_in_dim(g, (1 - core) * h, h, axis=1).astype(BF16)
        (from_core,) = swap_cores([give], name="swap_partials_" + tag)
        (s32, s16), _ = rowwise(sum2_fn, [keep.reshape(4 * h, cols), from_core.reshape(4 * h, cols)], [],
                                [(cols, F32), (cols, BF16)], name="sum_cores_" + tag, tm=SUM_TILE)
        return s32.reshape(4, h, cols), s16.reshape(4, h, cols)

    def chip_sum(pr, gt, tag):
        own = lax.dynamic_index_in_dim(pr, me, axis=0, keepdims=False)
        (q,), _ = rowwise(sum4_fn, [own, gt[0], gt[1], gt[2]], [], [(own.shape[1], F32)], name="sum_chips_" + tag,
                          tm=SUM_TILE)
        return q

    grads_b = jnp.stack([pack_b({nm: chip_parts(nm, s) for nm in names if nm != "w_in"}, d) for s in range(4)])
    pair_rest, pair_rest_b = chip_pair_sum(grads_b, "rest")
    gw_in_mine, (got_rest,) = matmul_tn(h0b, dproj, name="g_w_in", rider=scatter_rider([pair_rest_b]))
    gfull["w_in"] = jnp.concatenate([gw_in_mine[:, mine[nm][0]:mine[nm][0] + orig[nm][1]]
                                     for nm in ["hq", "hf", "hi", "hg", "fq", "fk", "fv", "ff", "ga", "gb"]], axis=1)
    grads_a = jnp.stack([pack_a(chip_parts("w_in", s)) for s in range(4)])
    pair_in, pair_in_b = chip_pair_sum(grads_a, "w_in")
    def ln0_bwd_post(dh0_in, aux, vecs):
        dx, dg, db = _ln_bwd(aux[1], dh0_in + ALPHA * aux[0], vecs[0])
        return [dx], [dg, db]
    ((dx,), (g_ln0_g, g_ln0_b)), (got_in,) = matmul_nn(
        dproj, win_mine, transpose_rhs=True, name="d_in_proj_ln0", tm=fused_tm, post=ln0_bwd_post,
        post_aux=[dz1, x2], post_vecs=[vec(ln0_g)], post_outs=[F32], post_accs=[d, d],
        rider=scatter_rider([pair_in_b]))
    q_half = [chip_sum(pair_in, got_in, "w_in"), chip_sum(pair_rest, got_rest, "rest")]
    q_other = swap_cores(q_half, name="swap_halves")
    g_a, g_b = [jnp.concatenate([jnp.where(core == 0, mine_, other), jnp.where(core == 0, other, mine_)], axis=0)
                for mine_, other in zip(q_half, q_other)]
    g_shards = unpack_b(g_b, lay)
    g_shards["w_in"] = g_a[:, :in_cols]

    def row1024(*parts):
        r = jnp.concatenate([q.reshape(1, -1) for q in parts], axis=1)
        return jnp.concatenate([r, jnp.zeros((1, PACK_W - r.shape[1]), F32)], axis=1) if r.shape[1] < PACK_W else r
    small_rows = [row1024(g_ln0_g), row1024(g_ln0_b), row1024(g_ln1_g), row1024(g_ln1_b), row1024(g_ln2_g),
                  row1024(g_ln2_b), row1024(g_norm_g, g_lb), row1024(g_fb[:, :fh], loss_part[:, :1])]
    small = allreduce_small(jnp.concatenate(small_rows, axis=0), name="allreduce_small")
    s_ln0_g, s_ln0_b, s_ln1_g, s_ln1_b, s_ln2_g, s_ln2_b = [small[r:r + 1] for r in range(6)]
    s_norm_g, s_lb = small[6:7, :hw], small[6:7, hw:2 * hw]
    s_fb, loss = small[7:8, :fh], small[7, fh]
    p0 = probs[0:1]
    jac = p0 * (1.0 - p0)
    s_hg_lb = jnp.concatenate([s_lb * jac, -s_lb * jac], axis=0)

    small_w = [vec(ln0_g), vec(ln0_b), ln1_g, ln1_b, ln2_g, ln2_b, hg_lb.reshape(1, -1), hg_norm_g, fox_fb]
    small_g = [s_ln0_g, s_ln0_b, s_ln1_g, s_ln1_b, s_ln2_g, s_ln2_b, s_hg_lb.reshape(1, -1), s_norm_g, s_fb]
    small_m = [vec(m_ln0_g), vec(m_ln0_b), m_ln1_g, m_ln1_b, m_ln2_g, m_ln2_b, m_hg_lb.reshape(1, -1), m_hg_norm_g, m_fox_fb]
    small_v = [vec(v_ln0_g), vec(v_ln0_b), v_ln1_g, v_ln1_b, v_ln2_g, v_ln2_b, v_hg_lb.reshape(1, -1), v_hg_norm_g, v_fox_fb]
    pad_rows = lambda lst, fill: jnp.concatenate(
        [row1024(a) if fill == 0.0 else jnp.concatenate([a.reshape(1, -1), jnp.full((1, PACK_W - a.size), fill, F32)], axis=1)
         for a in lst] + [jnp.full((16 - len(lst), PACK_W), fill, F32)], axis=0)
    sd, sm, sv = adamw(pad_rows(small_w, 0.0), pad_rows(small_g, 0.0), pad_rows(small_m, 0.0), pad_rows(small_v, 1.0),
                       name="adamw_small")
    small_shapes = [ln0_g.shape, ln0_b.shape, ln1_g.shape, ln1_b.shape, ln2_g.shape, ln2_b.shape, hg_lb.shape,
                    hg_norm_g.shape, fox_fb.shape]
    take = lambda buf: [buf[r, :int(np.prod(shp))].reshape(shp) for r, shp in enumerate(small_shapes)]
    sg_out, sd_out, sm_out, sv_out = [g.reshape(shp) for g, shp in zip(small_g, small_shapes)], take(sd), take(sm), take(sv)

    big_out = {}
    for nm in names:
        delta, m2, v2 = adamw(big[nm], g_shards[nm], big_m[nm], big_v[nm], name="adamw_" + nm)
        big_out[nm] = (g_shards[nm][None], delta[None], m2[None], v2[None])

    def ordered(k):
        sm_ = [sg_out, sd_out, sm_out, sv_out][k]
        bg = lambda nm: big_out[nm][k]
        return [sm_[0], sm_[1], bg("w_in"), sm_[6], sm_[7], sm_[8], bg("w_a"), bg("w_b"), bg("w_o"), sm_[2], sm_[3],
                bg("w_ff1"), bg("w_ff2"), bg("w_pg"), bg("w_p"), sm_[4], sm_[5]]
    grad_x = dx.reshape(n_seq, seq, d)
    return (loss, grad_x, *ordered(0), *ordered(1), *ordered(2), *ordered(3))
```

```python
import functools
from typing import NamedTuple, Optional

import numpy as np
import jax
import jax.numpy as jnp
from jax import lax
from jax.experimental import pallas as pl
from jax.experimental.pallas import tpu as pltpu

F32 = jnp.float32
BF16 = jnp.bfloat16
MESH = pl.DeviceIdType.MESH

VMEM_LIMIT_BYTES = 48 * 1024 * 1024
LANES = 128
HG_HEADS = 4
HG_DIM = 128
HG_BLK = 16
HG_TILE = 256
HG_SLOTS = 4
FOX_HDIM = 64
FOX_AUG = 128
FOX_TQ = 1024
FOX_FWD_HEADS = 1
LN_EPS = 1e-5
RMS_EPS = 1e-6
DEPTH = 1
ALPHA = (2.0 * DEPTH) ** 0.25
ADAM_LR, ADAM_B1, ADAM_B2, ADAM_EPS, ADAM_WD, ADAM_STEP = 0.001, 0.9, 0.999, 1e-08, 0.01, 10
NEG_INF = -1e30


def _cparams(sem):
    return pltpu.CompilerParams(dimension_semantics=sem, vmem_limit_bytes=VMEM_LIMIT_BYTES)


def _tile(n, cap):
    if n <= cap:
        return n
    best = None
    for t in range(LANES, cap + 1, LANES):
        if n % t == 0:
            best = t
    assert best is not None, (n, cap)
    return best


class WView(NamedTuple):
    arr: jax.Array
    r0: int
    c0: int
    k: int
    n: int
    split: Optional[int]


def matmul_nn(a, w, *, name, transpose_rhs=False, out_dtype=F32, epilogue=None, aux=None, tm=1024, rider=None,
              post=None, post_aux=(), post_vecs=(), post_outs=(), post_accs=()):
    wv = w if isinstance(w, WView) else WView(w[None], 0, 0, w.shape[0], w.shape[1], None)
    rows_s = wv.k // 4 if wv.split == 0 else wv.k
    cols_s = wv.n // 4 if wv.split == 1 else wv.n
    tr, tc = _tile(rows_s, 1152), _tile(cols_s, 1152)
    assert wv.r0 % tr == 0 and wv.c0 % tc == 0
    T, K = a.shape
    N, tn, tk = (wv.k, tr, tc) if transpose_rhs else (wv.n, tc, tr)
    assert K == (wv.n if transpose_rhs else wv.k)
    tm = min(tm, T)
    assert T % tm == 0
    nk = K // tk

    def w_block(ri, ci):
        if wv.split == 0:
            return (ri * tr) // rows_s, (wv.r0 + (ri * tr) % rows_s) // tr, wv.c0 // tc + ci
        if wv.split == 1:
            return (ci * tc) // cols_s, wv.r0 // tr + ri, (wv.c0 + (ci * tc) % cols_s) // tc
        return 0, wv.r0 // tr + ri, wv.c0 // tc + ci

    fused = post is not None
    assert not fused or N == tn
    aux_list = list(post_aux) if fused else ([aux] if aux is not None else [])
    aux_list = [x if isinstance(x, tuple) else (x, 0, x.shape[1]) for x in aux_list]
    vec_list = list(post_vecs)
    out_dtypes = list(post_outs) if fused else [out_dtype]
    n_aux, n_vec, n_out, n_acc = len(aux_list), len(vec_list), len(out_dtypes), len(post_accs)

    def body(*refs):
        a_ref, w_ref = refs[:2]
        aux_refs = refs[2:2 + n_aux]
        vec_refs = refs[2 + n_aux:2 + n_aux + n_vec]
        out_refs = refs[2 + n_aux + n_vec:2 + n_aux + n_vec + n_out]
        sum_refs = refs[2 + n_aux + n_vec + n_out:2 + n_aux + n_vec + n_out + n_acc]
        acc_ref = refs[-1]
        m, k = pl.program_id(1), pl.program_id(2)
        if transpose_rhs:
            part = lax.dot_general(a_ref[...], w_ref[...], (((1,), (1,)), ((), ())), preferred_element_type=F32)
        else:
            part = jnp.dot(a_ref[...], w_ref[...], preferred_element_type=F32)

        def write(res):
            if not fused:
                if epilogue is not None:
                    res = epilogue(res) if not aux_refs else epilogue(res, aux_refs[0][...])
                out_refs[0][...] = res.astype(out_dtype)
                return
            outs, sums = post(res, [r[...] for r in aux_refs], [v[...] for v in vec_refs])
            assert len(outs) == n_out and len(sums) == n_acc
            for r, val in zip(out_refs, outs):
                r[...] = val.astype(r.dtype)
            for r, val in zip(sum_refs, sums):
                def first_rows(r=r, val=val):
                    r[...] = val

                def later_rows(r=r, val=val):
                    r[...] += val
                pl.when(m == 0)(first_rows)
                pl.when(m > 0)(later_rows)

        if nk == 1:
            write(part)
        else:
            @pl.when(k == 0)
            def _():
                acc_ref[...] = part

            @pl.when(k > 0)
            def _():
                acc_ref[...] += part

            @pl.when(k == nk - 1)
            def _():
                write(acc_ref[...])

    w_index = (lambda n, m, k: w_block(n, k)) if transpose_rhs else (lambda n, m, k: w_block(k, n))
    in_specs = [pl.BlockSpec((tm, tk), lambda n, m, k: (m, k)),
                pl.BlockSpec((None, tr, tc), w_index)]
    args = [a, wv.arr]
    for arr, off, width in aux_list:
        assert width == N and off % tn == 0
        in_specs.append(pl.BlockSpec((tm, tn), functools.partial(lambda n, m, k, blk: (m, blk + n), blk=off // tn)))
        args.append(arr)
    for v in vec_list:
        in_specs.append(pl.BlockSpec(v.shape, lambda n, m, k: (0, 0)))
        args.append(v)
    out_specs = [pl.BlockSpec((tm, tn), lambda n, m, k: (m, n)) for _ in out_dtypes]
    out_specs += [pl.BlockSpec((1, tn), lambda n, m, k: (0, 0)) for _ in post_accs]
    out_shape = [jax.ShapeDtypeStruct((T, N), dt) for dt in out_dtypes]
    out_shape += [jax.ShapeDtypeStruct((1, N), F32) for _ in post_accs]
    scratch = [pltpu.VMEM((tm, tn) if nk > 1 else (8, LANES), F32)]
    grid = (N // tn, T // tm, nk)
    sem = ("arbitrary",) * 3 if (n_acc or rider is not None) else ("parallel", "parallel", "arbitrary")
    params = pltpu.CompilerParams(dimension_semantics=sem, vmem_limit_bytes=VMEM_LIMIT_BYTES,
                                  has_side_effects=rider is not None)
    if rider is not None:
        r_in, r_out, r_sems = rider.specs()
        body = rider.wrap(body, len(in_specs), len(out_specs), 3)
        in_specs, out_specs, out_shape = in_specs + r_in, out_specs + r_out, out_shape + rider.out_shape
        scratch, args = scratch + r_sems, args + list(rider.ins)
    res = pl.pallas_call(body, name=name, grid=grid, in_specs=in_specs, out_specs=out_specs, out_shape=out_shape,
                         scratch_shapes=scratch, compiler_params=params)(*args)
    main = (list(res[:n_out]), list(res[n_out:n_out + n_acc])) if fused else res[0]
    return main if rider is None else (main, list(res[n_out + n_acc:]))


def matmul_tn(a, b, *, name, tk=1024, rider=None):
    T, M = a.shape
    T2, N = b.shape
    tk = min(tk, T)
    assert T == T2 and T % tk == 0
    tm = _tile(M, 1024)
    tn = _tile(N, 1152)

    def body(a_ref, b_ref, o_ref):
        k = pl.program_id(2)
        part = lax.dot_general(a_ref[...], b_ref[...], (((0,), (0,)), ((), ())), preferred_element_type=F32)

        @pl.when(k == 0)
        def _():
            o_ref[...] = part

        @pl.when(k > 0)
        def _():
            o_ref[...] += part

    in_specs = [pl.BlockSpec((tk, tm), lambda m, n, k: (k, m)), pl.BlockSpec((tk, tn), lambda m, n, k: (k, n))]
    out_specs = [pl.BlockSpec((tm, tn), lambda m, n, k: (m, n))]
    out_shape = [jax.ShapeDtypeStruct((M, N), F32)]
    grid = (M // tm, N // tn, T // tk)
    if rider is None:
        return pl.pallas_call(body, name=name, grid=grid, in_specs=in_specs, out_specs=out_specs, out_shape=out_shape,
                              compiler_params=_cparams(("parallel", "parallel", "arbitrary")))(a, b)[0]
    r_in, r_out, r_sems = rider.specs()
    res = pl.pallas_call(
        rider.wrap(body, 2, 1, 3), name=name, grid=grid, in_specs=in_specs + r_in, out_specs=out_specs + r_out,
        out_shape=out_shape + rider.out_shape, scratch_shapes=r_sems,
        compiler_params=pltpu.CompilerParams(dimension_semantics=("arbitrary",) * 3,
                                             vmem_limit_bytes=VMEM_LIMIT_BYTES, has_side_effects=True),
    )(a, b, *rider.ins)
    return res[0], list(res[1:])


def rowwise(fn, rows, vecs, outs, accs=(), *, name, tm=512, rider=None):
    rows = [r if isinstance(r, tuple) else (r, 0, r.shape[1]) for r in rows]
    T = rows[0][0].shape[0]
    tm = min(tm, T)
    assert T % tm == 0
    n_rows, n_vecs, n_outs, n_accs = len(rows), len(vecs), len(outs), len(accs)

    def body(*refs):
        row_refs = refs[:n_rows]
        vec_refs = refs[n_rows:n_rows + n_vecs]
        out_refs = refs[n_rows + n_vecs:n_rows + n_vecs + n_outs]
        acc_refs = refs[n_rows + n_vecs + n_outs:]
        out_vals, acc_vals = fn([r[...] for r in row_refs], [v[...] for v in vec_refs])
        assert len(out_vals) == n_outs and len(acc_vals) == n_accs
        for r, val in zip(out_refs, out_vals):
            r[...] = val.astype(r.dtype)
        if n_accs:
            i = pl.program_id(0)

            @pl.when(i == 0)
            def _():
                for r in acc_refs:
                    r[...] = jnp.zeros_like(r)

            for r, val in zip(acc_refs, acc_vals):
                r[...] += val

    in_specs = []
    for arr, off, width in rows:
        assert off % width == 0
        in_specs.append(pl.BlockSpec((tm, width), functools.partial(lambda i, blk: (i, blk), blk=off // width)))
    for v in vecs:
        in_specs.append(pl.BlockSpec(v.shape, lambda i: (0, 0)))
    out_specs = [pl.BlockSpec((tm, w), lambda i: (i, 0)) for w, _ in outs]
    out_specs += [pl.BlockSpec((1, w), lambda i: (0, 0)) for w in accs]
    out_shape = [jax.ShapeDtypeStruct((T, w), dt) for w, dt in outs]
    out_shape += [jax.ShapeDtypeStruct((1, w), F32) for w in accs]
    args = [r[0] for r in rows] + list(vecs)
    if rider is None:
        res = pl.pallas_call(body, name=name, grid=(T // tm,), in_specs=in_specs, out_specs=out_specs,
                             out_shape=out_shape,
                             compiler_params=_cparams(("arbitrary",) if n_accs else ("parallel",)))(*args)
        return res[:n_outs], res[n_outs:]
    r_in, r_out, r_sems = rider.specs()
    res = pl.pallas_call(
        rider.wrap(body, len(in_specs), len(out_specs), 1), name=name, grid=(T // tm,), in_specs=in_specs + r_in,
        out_specs=out_specs + r_out, out_shape=out_shape + rider.out_shape, scratch_shapes=r_sems,
        compiler_params=pltpu.CompilerParams(dimension_semantics=("arbitrary",), vmem_limit_bytes=VMEM_LIMIT_BYTES,
                                             has_side_effects=True),
    )(*args, *rider.ins)
    return res[:n_outs], res[n_outs:n_outs + n_accs], list(res[n_outs + n_accs:])


def _colsum(x):
    return jnp.sum(x, axis=0, keepdims=True)


def _sigmoid(x):
    return 1.0 / (1.0 + jnp.exp(-x))


def _ln_stats(z):
    mu = jnp.mean(z, axis=-1, keepdims=True)
    zc = z - mu
    var = jnp.mean(zc * zc, axis=-1, keepdims=True)
    return zc * lax.rsqrt(var + LN_EPS)


def _ln_bwd(zhat_src, dy, g):
    mu = jnp.mean(zhat_src, axis=-1, keepdims=True)
    zc = zhat_src - mu
    var = jnp.mean(zc * zc, axis=-1, keepdims=True)
    rstd = lax.rsqrt(var + LN_EPS)
    zh = zc * rstd
    dzh = dy * g
    dz = rstd * (dzh - jnp.mean(dzh, axis=-1, keepdims=True) - zh * jnp.mean(dzh * zh, axis=-1, keepdims=True))
    return dz, _colsum(dy * zh), _colsum(dy)


def _hg_constants():
    r = np.arange(HG_TILE)
    same = (r[:, None] // HG_BLK) == (r[None, :] // HG_BLK)
    lower = (same & (r[None, :] <= r[:, None])).astype(np.float32)
    upper = (same & (r[None, :] >= r[:, None])).astype(np.float32)
    total = same.astype(np.float32)
    c = np.arange(2 * HG_DIM)
    bd = ((c[:, None] // HG_DIM) == (c[None, :] // HG_DIM)).astype(np.float32)
    pair_t = np.array([t for t, _ in _HG_PAIRS])
    pair_s = np.array([s for _, s in _HG_PAIRS])
    sel_t = (pair_t[None, :] == np.arange(HG_BLK)[:, None]).astype(np.float32)
    sel_s = (pair_s[None, :] == np.arange(HG_BLK)[:, None]).astype(np.float32)
    as_bf = lambda m: jnp.asarray(m, dtype=BF16)
    return as_bf(lower), as_bf(upper), as_bf(total), as_bf(bd), as_bf(sel_t), as_bf(sel_s)


_HG_HALF = HG_BLK // 2
_HG_PAIRS = ([(t, s) for t in range(_HG_HALF, HG_BLK) for s in range(HG_BLK)]
             + [(t, s) for t in range(_HG_HALF) for s in range(_HG_HALF)])
HG_STACK = len(_HG_PAIRS)
_HG_SLABS = ([((t - _HG_HALF) * HG_BLK, (t,), HG_BLK) for t in range(_HG_HALF, HG_BLK)]
             + [(_HG_HALF * HG_BLK + t * _HG_HALF, (t, t + 1), _HG_HALF) for t in range(0, _HG_HALF, 2)])


def _stack_by_s(x):
    return jnp.concatenate([x] * _HG_HALF + [x[:_HG_HALF]] * _HG_HALF, axis=0)


def _stack_by_t(x):
    w = x.shape[1]
    return jnp.concatenate([jnp.broadcast_to(x[t:t + 1], (HG_BLK, w)) for t in range(_HG_HALF, HG_BLK)]
                           + [jnp.broadcast_to(x[t:t + 1], (_HG_HALF, w)) for t in range(_HG_HALF)], axis=0)


def _keep_bf16_bits(x):
    bits = lax.bitcast_convert_type(x, jnp.int32) & jnp.int32(-65536)
    return lax.bitcast_convert_type(bits, F32)


def _head_sums(stack_ref, slot, bd):
    pair = bd.shape[0]
    return jnp.concatenate([jnp.dot(stack_ref[slot, :, c0:c0 + pair], bd, preferred_element_type=F32)
                            for c0 in range(0, stack_ref.shape[2], pair)], axis=1)


def _split3(x):
    hi = _keep_bf16_bits(x)
    r1 = x - hi
    mid = _keep_bf16_bits(r1)
    lo = _keep_bf16_bits(r1 - mid)
    return hi.astype(BF16), mid.astype(BF16), lo.astype(BF16)


def _dot3(m01, x):
    hi, mid, lo = _split3(x)
    d = lambda p: jnp.dot(m01, p, preferred_element_type=F32)
    return (d(lo) + d(mid)) + d(hi)


def _hg_prologue(hq, hf, lb, lower, total):
    sq = _sigmoid(hq)
    q = hq * sq
    sg = _sigmoid(hf)
    f = lb + (1.0 - lb) * sg
    g = jnp.log(f)
    k = 1.0 - f
    b = _dot3(lower, g)
    bl = _dot3(total, g)
    return q, k, f, sg, sq, b, bl


def _stack16(fn):
    return [fn(t) for t in range(HG_BLK)]


def hgrn2_fwd(proj, offs, lb, n_seq, seq, *, name, rider=None):
    T = n_seq * seq
    W = HG_HEADS * HG_DIM
    n_tiles = seq // HG_TILE
    nb = HG_TILE // HG_BLK
    lower, _, total, bd, sel_t, _ = _hg_constants()

    def body(hq_ref, hf_ref, hi_ref, lb_ref, lower_ref, total_ref, bd_ref, selt_ref,
             o_ref, st_out_ref,
             st_ref, q_s, k_s, v_s, b_s, qt_s, kt_s, d_s, p_s):
        @pl.when(pl.program_id(1) == 0)
        def _():
            st_ref[...] = jnp.zeros_like(st_ref)

        q, k, _, _, _, b, bl = _hg_prologue(hq_ref[...], hf_ref[...], lb_ref[...], lower_ref[...], total_ref[...])
        q_s[...] = q
        k_s[...] = k
        v_s[...] = hi_ref[...]
        b_s[...] = b
        qt_s[...] = q * jnp.exp(b)
        kt_s[...] = k * jnp.exp(jnp.minimum(bl - b, 0.0))
        d_s[...] = jnp.exp(bl)
        rowi = lax.broadcasted_iota(jnp.int32, (HG_BLK, W), 0)

        def block(i, slot):
            r0 = pl.multiple_of(i * HG_BLK, HG_BLK)
            rows = pl.ds(r0, HG_BLK)
            qi, ki, vi, bi = q_s[rows, :], k_s[rows, :], v_s[rows, :], b_s[rows, :]
            for off, ts, n in _HG_SLABS:
                slab = [jnp.where(rowi[:n] <= t, jnp.exp(jnp.minimum(bi[t:t + 1, :] - bi[:n], 0.0)), 0.0)
                        * qi[t:t + 1, :] * ki[:n] for t in ts]
                p_s[slot, pl.ds(off, HG_BLK), :] = jnp.concatenate(slab, axis=0).astype(BF16)
            a_b = _head_sums(p_s, slot, bd_ref[...])
            o_blk = jnp.dot(selt_ref[...], (a_b * _stack_by_s(vi)).astype(BF16), preferred_element_type=F32)
            qti, kti, di = qt_s[rows, :], kt_s[rows, :], d_s[rows, :]
            outs = []
            for h in range(HG_HEADS):
                hs = slice(h * HG_DIM, (h + 1) * HG_DIM)
                st_h = st_ref[hs, :]
                st_out_ref[i, hs, :] = st_h
                outs.append(lax.dot_general(qti[:, hs].astype(BF16), st_h.astype(BF16),
                                            (((1,), (1,)), ((), ())), preferred_element_type=F32))
                upd = lax.dot_general(vi[:, hs].astype(BF16), kti[:, hs].astype(BF16),
                                      (((0,), (0,)), ((), ())), preferred_element_type=F32)
                st_ref[hs, :] = st_h * di[0:1, hs] + upd
            o_ref[rows, :] = o_blk + jnp.concatenate(outs, axis=1)

        def some_blocks(jj, carry):
            for slot in range(HG_SLOTS):
                block(HG_SLOTS * jj + slot, slot)
            return carry

        lax.fori_loop(0, nb // HG_SLOTS, some_blocks, 0)

    col = lambda off: functools.partial(lambda s, t, blk: (s * n_tiles + t, blk), blk=off // W)
    const = lambda m: pl.BlockSpec(m.shape, lambda s, t: (0, 0))
    tile_f32 = pltpu.VMEM((HG_TILE, W), F32)
    in_specs = [pl.BlockSpec((HG_TILE, W), col(offs[0])), pl.BlockSpec((HG_TILE, W), col(offs[1])),
                pl.BlockSpec((HG_TILE, W), col(offs[2])), const(lb), const(lower), const(total), const(bd),
                const(sel_t)]
    out_specs = [pl.BlockSpec((HG_TILE, W), lambda s, t: (s * n_tiles + t, 0)),
                 pl.BlockSpec((nb, W, HG_DIM), lambda s, t: (s * n_tiles + t, 0, 0))]
    out_shape = [jax.ShapeDtypeStruct((T, W), F32), jax.ShapeDtypeStruct((T // HG_BLK, W, HG_DIM), F32)]
    scratch = [pltpu.VMEM((W, HG_DIM), F32)] + [tile_f32] * 7 + [pltpu.VMEM((HG_SLOTS, HG_STACK, W), BF16)]
    args = [proj, proj, proj, lb, lower, total, bd, sel_t]
    params = _cparams(("arbitrary", "arbitrary"))
    if rider is not None:
        r_in, r_out, r_sems = rider.specs()
        body = rider.wrap(body, len(in_specs), len(out_specs), 2)
        in_specs, out_specs, out_shape = in_specs + r_in, out_specs + r_out, out_shape + rider.out_shape
        scratch, args = scratch + r_sems, args + rider.ins
        params = pltpu.CompilerParams(dimension_semantics=("arbitrary", "arbitrary"),
                                      vmem_limit_bytes=VMEM_LIMIT_BYTES, has_side_effects=True)
    res = pl.pallas_call(body, name=name, grid=(n_seq, n_tiles), in_specs=in_specs, out_specs=out_specs,
                         out_shape=out_shape, scratch_shapes=scratch, compiler_params=params)(*args)
    return res[0], res[1], list(res[2:])


def hgrn2_bwd(proj, offs, lb, do, states, n_seq, seq, *, name):
    T = n_seq * seq
    W = HG_HEADS * HG_DIM
    n_tiles = seq // HG_TILE
    nb = HG_TILE // HG_BLK
    lower, upper, total, bd, sel_t, sel_s = _hg_constants()

    def body(hq_ref, hf_ref, hi_ref, do_ref, st_in_ref, lb_ref, lower_ref, upper_ref, total_ref, bd_ref,
             selt_ref, sels_ref,
             dhq_ref, dhf_ref, dhi_ref, dlb_ref,
             dst_ref, q_s, k_s, v_s, b_s, qt_s, kt_s, d_s, eb_s, ekb_s, dq_s, dk_s, db_s, dv_s,
             p_s, e_s, w_s):
        first = jnp.logical_and(pl.program_id(0) == 0, pl.program_id(1) == 0)

        @pl.when(first)
        def _():
            dlb_ref[...] = jnp.zeros_like(dlb_ref)

        @pl.when(pl.program_id(1) == 0)
        def _():
            dst_ref[...] = jnp.zeros_like(dst_ref)

        hq, lbv = hq_ref[...], lb_ref[...]
        q, k, f, sg, sq, b, bl = _hg_prologue(hq, hf_ref[...], lbv, lower_ref[...], total_ref[...])
        eb = jnp.exp(b)
        ekb = jnp.exp(jnp.minimum(bl - b, 0.0))
        q_s[...] = q
        k_s[...] = k
        v_s[...] = hi_ref[...]
        b_s[...] = b
        eb_s[...] = eb
        ekb_s[...] = ekb
        qt_s[...] = q * eb
        kt_s[...] = k * ekb
        d_s[...] = jnp.exp(bl)
        rowi = lax.broadcasted_iota(jnp.int32, (HG_BLK, W), 0)
        last_row = rowi == HG_BLK - 1

        def block(i, slot):
            r0 = pl.multiple_of(i * HG_BLK, HG_BLK)
            rows = pl.ds(r0, HG_BLK)
            qi, ki, vi, bi, doi = q_s[rows, :], k_s[rows, :], v_s[rows, :], b_s[rows, :], do_ref[rows, :]
            for off, ts, n in _HG_SLABS:
                es = [jnp.where(rowi[:n] <= t, jnp.exp(jnp.minimum(bi[t:t + 1, :] - bi[:n], 0.0)), 0.0) for t in ts]
                sl = pl.ds(off, HG_BLK)
                e_s[slot, sl, :] = jnp.concatenate(es, axis=0)
                p_s[slot, sl, :] = jnp.concatenate([e * qi[t:t + 1, :] * ki[:n] for e, t in zip(es, ts)],
                                                   axis=0).astype(BF16)
                w_s[slot, sl, :] = jnp.concatenate([doi[t:t + 1, :] * vi[:n] for t in ts], axis=0).astype(BF16)
            a_b = _head_sums(p_s, slot, bd_ref[...])
            da_b = _head_sums(w_s, slot, bd_ref[...])
            x = da_b * e_s[slot]
            dq_in = jnp.dot(selt_ref[...], (x * _stack_by_s(ki)).astype(BF16), preferred_element_type=F32)
            dk_in = jnp.dot(sels_ref[...], (x * _stack_by_t(qi)).astype(BF16), preferred_element_type=F32)
            dv_in = jnp.dot(sels_ref[...], (a_b * _stack_by_t(doi)).astype(BF16), preferred_element_type=F32)
            qti, kti, di = qt_s[rows, :], kt_s[rows, :], d_s[rows, :]
            dqt, dkt, dvt, dd = [], [], [], []
            for h in range(HG_HEADS):
                hs = slice(h * HG_DIM, (h + 1) * HG_DIM)
                st_h = st_in_ref[i, hs, :]
                dst_h = dst_ref[hs, :]
                do_h, v_h = doi[:, hs].astype(BF16), vi[:, hs].astype(BF16)
                dst_b = dst_h.astype(BF16)
                dqt.append(jnp.dot(do_h, st_h.astype(BF16), preferred_element_type=F32))
                dkt.append(jnp.dot(v_h, dst_b, preferred_element_type=F32))
                dvt.append(lax.dot_general(kti[:, hs].astype(BF16), dst_b, (((1,), (1,)), ((), ())),
                                           preferred_element_type=F32))
                dd.append(jnp.sum(dst_h * st_h, axis=0, keepdims=True))
                upd = lax.dot_general(do_h, qti[:, hs].astype(BF16), (((0,), (0,)), ((), ())),
                                      preferred_element_type=F32)
                dst_ref[hs, :] = dst_h * di[0:1, hs] + upd
            dqt = jnp.concatenate(dqt, axis=1)
            dkt = jnp.concatenate(dkt, axis=1)
            dvt = jnp.concatenate(dvt, axis=1)
            dd = jnp.concatenate(dd, axis=1)
            dbl = jnp.sum(dkt * kti, axis=0, keepdims=True) + dd * di[0:1, :]
            db = qi * dq_in - ki * dk_in + dqt * qti - dkt * kti
            db_s[rows, :] = db + jnp.where(last_row, dbl, 0.0)
            dq_s[rows, :] = dq_in + dqt * eb_s[rows, :]
            dk_s[rows, :] = dk_in + dkt * ekb_s[rows, :]
            dv_s[rows, :] = dv_in + dvt

        def some_blocks(jj, carry):
            for slot in range(HG_SLOTS):
                block(nb - 1 - slot - HG_SLOTS * jj, slot)
            return carry

        lax.fori_loop(0, nb // HG_SLOTS, some_blocks, 0)

        dg = _dot3(upper_ref[...], db_s[...])
        dhq_ref[...] = (dq_s[...] * (sq * (1.0 + hq * (1.0 - sq)))).astype(dhq_ref.dtype)
        df = dg / f - dk_s[...]
        dhf_ref[...] = (df * (1.0 - lbv) * (sg * (1.0 - sg))).astype(dhf_ref.dtype)
        dhi_ref[...] = dv_s[...].astype(dhi_ref.dtype)
        dlb_ref[...] += _colsum(df * (1.0 - sg))

    rev = lambda s, t: s * n_tiles + (n_tiles - 1 - t)
    col = lambda off: functools.partial(lambda s, t, blk: (rev(s, t), blk), blk=off // W)
    const = lambda m: pl.BlockSpec(m.shape, lambda s, t: (0, 0))
    row = pl.BlockSpec((HG_TILE, W), lambda s, t: (rev(s, t), 0))
    tile_f32 = pltpu.VMEM((HG_TILE, W), F32)
    n2 = HG_STACK
    return pl.pallas_call(
        body, name=name,
        grid=(n_seq, n_tiles),
        in_specs=[pl.BlockSpec((HG_TILE, W), col(offs[0])), pl.BlockSpec((HG_TILE, W), col(offs[1])),
                  pl.BlockSpec((HG_TILE, W), col(offs[2])), row,
                  pl.BlockSpec((nb, W, HG_DIM), lambda s, t: (rev(s, t), 0, 0)),
                  const(lb), const(lower), const(upper), const(total), const(bd), const(sel_t), const(sel_s)],
        out_specs=[row, row, row, pl.BlockSpec((1, W), lambda s, t: (0, 0))],
        out_shape=[jax.ShapeDtypeStruct((T, W), BF16)] * 3 + [jax.ShapeDtypeStruct((1, W), F32)],
        scratch_shapes=[pltpu.VMEM((W, HG_DIM), F32)] + [tile_f32] * 13
                       + [pltpu.VMEM((HG_SLOTS, n2, W), BF16), pltpu.VMEM((HG_SLOTS, n2, W), F32),
                          pltpu.VMEM((HG_SLOTS, n2, W), BF16)],
        compiler_params=_cparams(("arbitrary", "arbitrary")),
    )(proj, proj, proj, do, states, lb, lower, upper, total, bd, sel_t, sel_s)


def _diag_mask(tq):
    return lax.broadcasted_iota(jnp.int32, (tq, tq), 1) <= lax.broadcasted_iota(jnp.int32, (tq, tq), 0)


def _qk(q, k):
    return lax.dot_general(q, k, (((1,), (1,)), ((), ())), preferred_element_type=F32)


def _causal_pairs(n, sweeps=1, by_key=False):
    if by_key:
        rows = [(i, j, 0) for j in range(n) for i in range(j, n)]
    else:
        rows = [(i, j, s) for i in range(n) for s in range(sweeps) for j in range(i + 1)]
    return tuple(jnp.asarray(np.array([r[c] for r in rows], np.int32)) for c in range(3))


def _fox_placement(fh):
    hw, wa = fh * FOX_HDIM, fh * FOX_AUG
    pq, pk = np.zeros((hw, wa), np.float32), np.zeros((hw, wa), np.float32)
    aq, ak = np.zeros((3 * LANES, wa), np.float32), np.zeros((3 * LANES, wa), np.float32)
    oq, ok = np.zeros((1, wa), np.float32), np.zeros((1, wa), np.float32)
    for h in range(fh):
        src, dst = np.arange(h * FOX_HDIM, (h + 1) * FOX_HDIM), np.arange(h * FOX_AUG, h * FOX_AUG + FOX_HDIM)
        pq[src, dst] = FOX_HDIM ** -0.5
        pk[src, dst] = 1.0
        gate = h * FOX_AUG + FOX_HDIM
        for r in range(3):
            aq[r * LANES + h, gate + r] = 1.0
            ak[r * LANES + h, gate + 3 + r] = -1.0
        oq[0, gate + 3:gate + 6] = 1.0
        ok[0, gate:gate + 3] = 1.0
    bf = lambda m: jnp.asarray(m, dtype=BF16)
    return {"pq": bf(pq), "pk": bf(pk), "aq": bf(aq), "ak": bf(ak), "oq": jnp.asarray(oq), "ok": jnp.asarray(ok),
            "pqt": bf(pq.T), "pkt": bf(pk.T)}


def _fox_specs(tq, fh, heads=1):
    groups = fh // heads

    def spec(tab):
        return pl.BlockSpec((None, tq, heads * FOX_AUG), lambda b, t, *tabs: (b // groups, tabs[tab][t], b % groups))
    return spec(0), spec(1)


def fox_fwd(qa, ka, va, *, name):
    n_seq, S, width = qa.shape
    fh = width // FOX_AUG
    nh = FOX_FWD_HEADS
    BH = n_seq * fh // nh
    tq = min(FOX_TQ, S)
    itab, jtab, _ = _causal_pairs(S // tq)

    def body(itab_ref, jtab_ref, q_ref, k_ref, v_ref, o_ref, ox_ref, lse_ref, *scratch):
        t = pl.program_id(1)
        i, j = itab_ref[t], jtab_ref[t]
        per_head = [scratch[4 * h:4 * h + 4] for h in range(nh)]

        @pl.when(j == 0)
        def _():
            for m_s, l_s, acc_s, acc_lo_s in per_head:
                m_s[...] = jnp.full_like(m_s, NEG_INF)
                l_s[...] = jnp.zeros_like(l_s)
                acc_s[...] = jnp.zeros_like(acc_s)
                acc_lo_s[...] = jnp.zeros_like(acc_lo_s)

        def step(on_diagonal):
            for h, (m_s, l_s, acc_s, acc_lo_s) in enumerate(per_head):
                lanes = slice(h * FOX_AUG, (h + 1) * FOX_AUG)
                s = _qk(q_ref[:, lanes], k_ref[:, lanes])
                if on_diagonal:
                    s = jnp.where(_diag_mask(tq), s, NEG_INF)
                m_prev = m_s[...]
                m_new = jnp.maximum(m_prev, jnp.max(s, axis=-1, keepdims=True))
                alpha = jnp.exp(m_prev - m_new)
                p = jnp.exp(s - m_new[:, 0:1])
                p_hi = p.astype(BF16)
                p_lo = (p - p_hi.astype(F32)).astype(BF16)
                v = v_ref[:, lanes]
                l_s[...] = alpha * l_s[...] + jnp.sum(p, axis=-1, keepdims=True)
                acc_s[...] = alpha * acc_s[...] + jnp.dot(p_hi, v, preferred_element_type=F32)
                acc_lo_s[...] = alpha * acc_lo_s[...] + jnp.dot(p_lo, v, preferred_element_type=F32)
                m_s[...] = m_new

        @pl.when(j < i)
        def _():
            step(False)

        @pl.when(j == i)
        def _():
            step(True)
            for h, (m_s, l_s, acc_s, acc_lo_s) in enumerate(per_head):
                lanes = slice(h * FOX_AUG, (h + 1) * FOX_AUG)
                inv_l = 1.0 / l_s[...]
                o_ref[:, lanes] = (acc_s[...] * inv_l).astype(o_ref.dtype)
                ox_ref[:, lanes] = (acc_s[...] + acc_lo_s[...]) * inv_l
                lse_ref[:, lanes] = m_s[...] + jnp.log(l_s[...])

    qspec, kspec = _fox_specs(tq, fh, nh)
    wide = jax.ShapeDtypeStruct((n_seq, S, width), F32)
    return pl.pallas_call(
        body, name=name,
        grid_spec=pltpu.PrefetchScalarGridSpec(
            num_scalar_prefetch=2, grid=(BH, itab.shape[0]),
            in_specs=[qspec, kspec, kspec],
            out_specs=[qspec, qspec, qspec],
            scratch_shapes=[pltpu.VMEM((tq, LANES), F32)] * (4 * nh)),
        out_shape=[jax.ShapeDtypeStruct((n_seq, S, width), BF16), wide, wide],
        compiler_params=_cparams(("parallel", "arbitrary")),
    )(itab, jtab, qa, ka, va)


def _fox_ds(q, k, v, do, ox, lse, on_diagonal):
    s = _qk(q, k)
    if on_diagonal:
        s = jnp.where(_diag_mask(s.shape[0]), s, NEG_INF)
    p = jnp.exp(s - lse[:, 0:1])
    delta = jnp.sum(do.astype(F32) * ox, axis=-1, keepdims=True)
    return p, p * (_qk(do, v) - delta)


def fox_bwd(qa, ka, va, do, ox, lse, *, name):
    n_seq, S, width = qa.shape
    fh = width // FOX_AUG
    BH = n_seq * fh
    tq = min(FOX_TQ, S)
    itab, jtab, _ = _causal_pairs(S // tq)

    def body(itab_ref, jtab_ref, q_ref, k_ref, v_ref, do_ref, ox_ref, lse_ref, dq_ref, dk_ref, dv_ref, dsum_ref):
        t = pl.program_id(1)
        i, j = itab_ref[t], jtab_ref[t]

        @pl.when(t == 0)
        def _():
            dq_ref[...] = jnp.zeros_like(dq_ref)
            dk_ref[...] = jnp.zeros_like(dk_ref)
            dv_ref[...] = jnp.zeros_like(dv_ref)
            dsum_ref[...] = jnp.zeros_like(dsum_ref)

        q_rows = pl.ds(pl.multiple_of(i * tq, tq), tq)
        k_rows = pl.ds(pl.multiple_of(j * tq, tq), tq)

        def step(on_diagonal):
            q, k, do = q_ref[...], k_ref[...], do_ref[...]
            p, ds = _fox_ds(q, k, v_ref[...], do, ox_ref[...], lse_ref[...], on_diagonal)
            ds_b = ds.astype(BF16)
            tn = (((0,), (0,)), ((), ()))
            dq_ref[q_rows, :] += jnp.dot(ds_b, k, preferred_element_type=F32)
            dk_ref[k_rows, :] += lax.dot_general(ds_b, q, tn, preferred_element_type=F32)
            dv_ref[k_rows, :] += lax.dot_general(p.astype(BF16), do, tn, preferred_element_type=F32)
            dsum_ref[:, k_rows] += _colsum(ds)

        @pl.when(j < i)
        def _():
            step(False)

        @pl.when(j == i)
        def _():
            step(True)

    qspec, kspec = _fox_specs(tq, fh)
    whole = pl.BlockSpec((None, S, FOX_AUG), lambda b, t, it, jt: (b // fh, 0, b % fh))
    wide = jax.ShapeDtypeStruct((n_seq, S, width), F32)
    return pl.pallas_call(
        body, name=name,
        grid_spec=pltpu.PrefetchScalarGridSpec(
            num_scalar_prefetch=2, grid=(BH, itab.shape[0]),
            in_specs=[qspec, kspec, kspec, qspec, qspec, qspec],
            out_specs=[whole, whole, whole, pl.BlockSpec((None, 1, S), lambda b, t, it, jt: (b, 0, 0))]),
        out_shape=[wide, wide, wide, jax.ShapeDtypeStruct((BH, 1, S), F32)],
        compiler_params=_cparams(("parallel", "arbitrary")),
    )(itab, jtab, qa, ka, va, do, ox, lse)


def seq_cumsum(x, n_seq, seq, *, reverse, name):
    T, C = x.shape
    tb = min(256, seq)
    n = seq // tb
    r = np.arange(tb)
    tri = (r[None, :] >= r[:, None]) if reverse else (r[None, :] <= r[:, None])
    tri = jnp.asarray(tri.astype(np.float32), dtype=BF16)

    def body(x_ref, tri_ref, o_ref, carry_s):
        @pl.when(pl.program_id(1) == 0)
        def _():
            carry_s[...] = jnp.zeros_like(carry_s)

        xv = x_ref[...]
        o_ref[...] = _dot3(tri_ref[...], xv) + carry_s[...]
        carry_s[...] += _colsum(xv)

    blk = (lambda s, t: (s * n + (n - 1 - t), 0)) if reverse else (lambda s, t: (s * n + t, 0))
    return pl.pallas_call(
        body, name=name,
        grid=(n_seq, n),
        in_specs=[pl.BlockSpec((tb, C), blk), pl.BlockSpec((tb, tb), lambda s, t: (0, 0))],
        out_specs=pl.BlockSpec((tb, C), blk),
        out_shape=jax.ShapeDtypeStruct((T, C), F32),
        scratch_shapes=[pltpu.VMEM((1, C), F32)],
        compiler_params=_cparams(("arbitrary", "arbitrary")),
    )(x, tri)


def _place():
    return lax.axis_index("x"), lax.axis_index("y"), lax.axis_index("c")


def _other_chips(x, y):
    return [(1 - x, y), (x, 1 - y), (1 - x, 1 - y)]


def _hbm_call(body, ins, out_shape, n_sems, *, name):
    hbm = pl.BlockSpec(memory_space=pl.ANY)
    return pl.pallas_call(
        body, name=name,
        in_specs=[hbm] * len(ins), out_specs=[hbm] * len(out_shape), out_shape=out_shape,
        scratch_shapes=[pltpu.SemaphoreType.DMA((n_sems,)), pltpu.SemaphoreType.DMA((n_sems,)),
                        pltpu.SemaphoreType.DMA((len(ins),))],
        compiler_params=pltpu.CompilerParams(has_side_effects=True),
    )(*ins)


def allgather_chips(shards, *, name):
    return _exchange_call(allgather_rider(shards), name=name)


def _allgather_ops(x_refs, o_refs, send_sems, recv_sems, local_sems):
    def copies():
        x, y, c = _place()
        me = 2 * x + y
        chips = _other_chips(x, y)
        own, first, passed, landed, handed = [], [], [], [], []
        for b, (x_ref, o_ref) in enumerate(zip(x_refs, o_refs)):
            half = x_ref.shape[0] // 2
            mine, theirs = pl.ds(c * half, half), pl.ds((1 - c) * half, half)
            own.append(pltpu.make_async_copy(x_ref, o_ref.at[me], local_sems.at[b]))

            def copy(k, src, chip, rows, to, o_ref=o_ref, b=b):
                return pltpu.make_async_remote_copy(src_ref=src, dst_ref=o_ref.at[2 * chip[0] + chip[1], rows],
                                                    send_sem=send_sems.at[6 * b + k], recv_sem=recv_sems.at[6 * b + k],
                                                    device_id=to, device_id_type=MESH)
            for j, chip in enumerate(chips):
                first.append(copy(j, x_ref.at[mine], (x, y), mine, (*chip, c)))
                landed.append(copy(j, x_ref.at[mine], chip, mine, (*chip, c)))
                passed.append(copy(3 + j, o_ref.at[2 * chip[0] + chip[1], mine], chip, mine, (x, y, 1 - c)))
                handed.append(copy(3 + j, x_ref.at[mine], chip, theirs, (x, y, 1 - c)))
        return own, first, passed, landed, handed

    def start():
        own, first, _, _, _ = copies()
        for cp in own + first:
            cp.start()

    def finish():
        own, first, passed, landed, handed = copies()
        for arrived, forward in zip(landed, passed):
            arrived.wait_recv()
            forward.start()
        for cp in handed:
            cp.wait_recv()
        for cp in first + passed:
            cp.wait_send()
        for cp in own:
            cp.wait()
    return start, finish


def _scatter_ops(x_refs, o_refs, send_sems, recv_sems, local_sems):
    def copies():
        x, y, c = _place()
        return [pltpu.make_async_remote_copy(
            src_ref=x_ref.at[2 * px + py], dst_ref=o_ref.at[j], send_sem=send_sems.at[3 * b + j],
            recv_sem=recv_sems.at[3 * b + j], device_id=(px, py, c), device_id_type=MESH)
            for b, (x_ref, o_ref) in enumerate(zip(x_refs, o_refs)) for j, (px, py) in enumerate(_other_chips(x, y))]

    def start():
        for cp in copies():
            cp.start()

    def finish():
        sends = copies()
        for cp in sends:
            cp.wait_recv()
        for cp in sends:
            cp.wait_send()
    return start, finish


class Rider(NamedTuple):
    ins: list
    out_shape: list
    n_sems: int
    ops: object

    def specs(self):
        hbm = pl.BlockSpec(memory_space=pl.ANY)
        sems = [pltpu.SemaphoreType.DMA((self.n_sems,)), pltpu.SemaphoreType.DMA((self.n_sems,)),
                pltpu.SemaphoreType.DMA((len(self.ins),))]
        return [hbm] * len(self.ins), [hbm] * len(self.out_shape), sems

    def wrap(self, body, n_in, n_out, grid_rank):
        k_in, k_out = len(self.ins), len(self.out_shape)

        def carried(*refs):
            ins, r_ins = refs[:n_in], refs[n_in:n_in + k_in]
            outs = refs[n_in + k_in:n_in + k_in + n_out]
            r_outs = refs[n_in + k_in + n_out:n_in + k_in + n_out + k_out]
            scratch, sems = refs[n_in + k_in + n_out + k_out:-3], refs[-3:]
            first = functools.reduce(jnp.logical_and, [pl.program_id(a) == 0 for a in range(grid_rank)])
            last = functools.reduce(jnp.logical_and,
                                    [pl.program_id(a) == pl.num_programs(a) - 1 for a in range(grid_rank)])
            pl.when(first)(lambda: self.ops(r_ins, r_outs, *sems)[0]())
            body(*ins, *outs, *scratch)
            pl.when(last)(lambda: self.ops(r_ins, r_outs, *sems)[1]())
        return carried


def _exchange_call(rider, *, name):
    def body(*refs):
        k = len(rider.ins)
        start, finish = rider.ops(refs[:k], refs[k:k + len(rider.out_shape)], *refs[-3:])
        start()
        finish()
    in_specs, out_specs, sems = rider.specs()
    return pl.pallas_call(body, name=name, in_specs=in_specs, out_specs=out_specs, out_shape=rider.out_shape,
                          scratch_shapes=sems, compiler_params=pltpu.CompilerParams(has_side_effects=True))(*rider.ins)


def allgather_rider(shards):
    assert all(s.shape[0] % (2 * ROW_ALIGN) == 0 for s in shards)
    return Rider(list(shards), [jax.ShapeDtypeStruct((4,) + s.shape, s.dtype) for s in shards], 6 * len(shards),
                 _allgather_ops)


def scatter_rider(parts):
    return Rider(list(parts), [jax.ShapeDtypeStruct((3,) + p.shape[1:], p.dtype) for p in parts], 3 * len(parts),
                 _scatter_ops)


def scatter_chips(parts, *, name):
    return _exchange_call(scatter_rider(parts), name=name)


def swap_cores(vs, *, name):
    nb = len(vs)

    def body(*refs):
        x_refs, o_refs = refs[:nb], refs[nb:2 * nb]
        send_sems, recv_sems, _ = refs[2 * nb:]
        x, y, c = _place()
        copies = [pltpu.make_async_remote_copy(src_ref=x_ref, dst_ref=o_ref, send_sem=send_sems.at[b],
                                               recv_sem=recv_sems.at[b], device_id=(x, y, 1 - c), device_id_type=MESH)
                  for b, (x_ref, o_ref) in enumerate(zip(x_refs, o_refs))]
        for cp in copies:
            cp.start()
        for cp in copies:
            cp.wait()

    return _hbm_call(body, vs, [jax.ShapeDtypeStruct(v.shape, v.dtype) for v in vs], nb, name=name)


def allreduce_small(v, *, name):
    R, C = v.shape

    def body(x_ref, o_ref, gath_ref, send_sems, recv_sems):
        x, y, c = _place()
        me = 4 * x + 2 * y + c
        gath_ref[me] = x_ref[...]
        flips = [(k >> 2 & 1, k >> 1 & 1, k & 1) for k in range(1, 8)]
        sends = []
        for j, (fx, fy, fc) in enumerate(flips):
            peer = (x ^ fx, y ^ fy, c ^ fc)
            cp = pltpu.make_async_remote_copy(src_ref=x_ref, dst_ref=gath_ref.at[me], send_sem=send_sems.at[j],
                                              recv_sem=recv_sems.at[j], device_id=peer, device_id_type=MESH)
            cp.start()
            sends.append(cp)
        for j, (fx, fy, fc) in enumerate(flips):
            peer = (x ^ fx, y ^ fy, c ^ fc)
            pltpu.make_async_remote_copy(src_ref=x_ref, dst_ref=gath_ref.at[4 * peer[0] + 2 * peer[1] + peer[2]],
                                         send_sem=send_sems.at[j], recv_sem=recv_sems.at[j], device_id=peer,
                                         device_id_type=MESH).wait_recv()
        for cp in sends:
            cp.wait_send()
        total = gath_ref[0]
        for d in range(1, 8):
            total = total + gath_ref[d]
        o_ref[...] = total

    vm = pl.BlockSpec(memory_space=pltpu.VMEM)
    out, _ = pl.pallas_call(
        body, name=name,
        in_specs=[vm], out_specs=[vm, vm],
        out_shape=[jax.ShapeDtypeStruct((R, C), F32), jax.ShapeDtypeStruct((8, R, C), F32)],
        scratch_shapes=[pltpu.SemaphoreType.DMA((7,)), pltpu.SemaphoreType.DMA((7,))],
        compiler_params=pltpu.CompilerParams(has_side_effects=True),
    )(v)
    return out


ROW_ALIGN = 16
PACK_W = 1024
SUM_TILE = 512
BIG_WEIGHTS = (("w_in", 1), ("w_a", 1), ("w_b", 1), ("w_o", 0), ("w_ff1", 1), ("w_ff2", 0), ("w_pg", 0), ("w_p", 1))


def _b_layout(d, ple):
    hw, q = d // 2, d // 4
    small = 2 * d + 2 * q
    lay = {"w_ff1": (0, 0, d, d), "w_ff2": (d, 0, d, d), "w_o": (2 * d, 0, q, d), "w_pg": (2 * d + q, 0, q, d),
           "w_a": (small, 0, hw, q), "w_b": (small, q, hw, q), "w_p": (small, 2 * q, ple, q)}
    return lay, small + hw


def pack_a(w_in_shard):
    rows, cols = w_in_shard.shape
    pad = -cols % LANES
    return jnp.concatenate([w_in_shard, jnp.zeros((rows, pad), w_in_shard.dtype)], axis=1)


def pack_b(shards, d):
    hw, q = d // 2, d // 4
    dt = shards["w_a"].dtype
    wp = shards["w_p"]
    wp = jnp.concatenate([wp, jnp.zeros((hw - wp.shape[0], q), dt)], axis=0)
    small = jnp.concatenate([shards["w_a"], shards["w_b"], wp, jnp.zeros((hw, d - 3 * q), dt)], axis=1)
    return jnp.concatenate([shards["w_ff1"], shards["w_ff2"], shards["w_o"], shards["w_pg"], small], axis=0)


def unpack_b(buf, lay):
    return {nm: buf[r0:r0 + rows, c0:c0 + cols] for nm, (r0, c0, rows, cols) in lay.items()}


def _win_layout(d):
    hw = d // 2
    fh = hw // FOX_HDIM
    orig = {"hq": (0, hw), "hf": (hw, hw), "hi": (2 * hw, hw), "hg": (3 * hw, hw), "fq": (4 * hw, hw),
            "fk": (5 * hw, hw), "fv": (6 * hw, hw), "ff": (7 * hw, fh), "ga": (7 * hw + fh, d), "gb": (7 * hw + fh + d, d)}
    order = ["ga", "gb", "hq", "hf", "hi", "hg", "fq", "fk", "fv", "ff"]
    mine, off = {}, 0
    for nm in order:
        width = orig[nm][1] if nm != "ff" else LANES
        mine[nm] = (off, width)
        off += width
    return orig, order, mine, off


def _adam_fn(rows, vecs):
    w, g, m, v = rows
    m2 = ADAM_B1 * m + (1.0 - ADAM_B1) * g
    v2 = ADAM_B2 * v + (1.0 - ADAM_B2) * (g * g)
    m_hat = m2 / (1.0 - ADAM_B1 ** ADAM_STEP)
    v_hat = v2 / (1.0 - ADAM_B2 ** ADAM_STEP)
    delta = -ADAM_LR * (m_hat / (jnp.sqrt(v_hat) + ADAM_EPS) + ADAM_WD * w)
    return [delta, m2, v2], []


def adamw(w, g, m, v, *, name):
    c = w.shape[1]
    (delta, m2, v2), _ = rowwise(_adam_fn, [w, g, m, v], [], [(c, F32)] * 3, name=name, tm=256)
    return delta, m2, v2


def kernel(x, p, ln0_g, ln0_b, w_in, hg_lb, hg_norm_g, fox_fb, w_a, w_b, w_o, ln1_g, ln1_b, w_ff1, w_ff2, w_pg, w_p, ln2_g, ln2_b, loss_target, m_ln0_g, m_ln0_b, m_w_in, m_hg_lb, m_hg_norm_g, m_fox_fb, m_w_a, m_w_b, m_w_o, m_ln1_g, m_ln1_b, m_w_ff1, m_w_ff2, m_w_pg, m_w_p, m_ln2_g, m_ln2_b, v_ln0_g, v_ln0_b, v_w_in, v_hg_lb, v_hg_norm_g, v_fox_fb, v_w_a, v_w_b, v_w_o, v_ln1_g, v_ln1_b, v_w_ff1, v_w_ff2, v_w_pg, v_w_p, v_ln2_g, v_ln2_b):
    n_seq, seq, d = x.shape
    T = n_seq * seq
    hw = d // 2
    fh = hw // FOX_HDIM
    bh = n_seq * fh
    orig, order, mine, n_in = _win_layout(d)

    big = {"w_in": w_in[0], "w_a": w_a[0], "w_b": w_b[0], "w_o": w_o[0], "w_ff1": w_ff1[0], "w_ff2": w_ff2[0],
           "w_pg": w_pg[0], "w_p": w_p[0]}
    big_m = {"w_in": m_w_in[0], "w_a": m_w_a[0], "w_b": m_w_b[0], "w_o": m_w_o[0], "w_ff1": m_w_ff1[0],
             "w_ff2": m_w_ff2[0], "w_pg": m_w_pg[0], "w_p": m_w_p[0]}
    big_v = {"w_in": v_w_in[0], "w_a": v_w_a[0], "w_b": v_w_b[0], "w_o": v_w_o[0], "w_ff1": v_w_ff1[0],
             "w_ff2": v_w_ff2[0], "w_pg": v_w_pg[0], "w_p": v_w_p[0]}
    names = [nm for nm, _ in BIG_WEIGHTS]
    axis = dict(BIG_WEIGHTS)
    ple = w_p.shape[1]
    lay, b_rows = _b_layout(d, ple)
    in_cols = big["w_in"].shape[1]

    gather_w_in = allgather_rider([pack_a(big["w_in"].astype(BF16))])
    gather_rest = allgather_rider([pack_b({nm: big[nm].astype(BF16) for nm in names if nm != "w_in"}, d)])

    x2 = x.reshape(T, d)
    tgt = loss_target.reshape(T, d)
    p_b = p.reshape(T, p.shape[-1]).astype(BF16)
    vec = lambda a: a.reshape(1, -1)
    probs = jax.nn.softmax(hg_lb, axis=0)
    lb = vec(probs[0])

    def ln0_fn(rows, vecs):
        h = _ln_stats(rows[0]) * vecs[0] + vecs[1]
        return [h, h], []
    (h0, h0b), _, (a_all,) = rowwise(ln0_fn, [x2], [vec(ln0_g), vec(ln0_b)], [(d, F32), (d, BF16)], name="ln0_fwd",
                                     rider=gather_w_in)
    win = jnp.concatenate([a_all[s, :, :in_cols] for s in range(4)], axis=1)
    win_mine = jnp.concatenate(
        [win[:, orig[nm][0]:orig[nm][0] + orig[nm][1]] for nm in order]
        + [jnp.zeros((d, LANES - fh), BF16)], axis=1)
    proj = matmul_nn(h0b, win_mine, name="in_proj")

    o_raw, hg_states, (b_all,) = hgrn2_fwd(proj, [mine["hq"][0], mine["hf"][0], mine["hi"][0]], lb, n_seq, seq,
                                           name="hgrn2_fwd", rider=gather_rest)
    view = lambda nm, k, n: WView(b_all, lay[nm][0], lay[nm][1], k, n, axis[nm])
    w_ff1_v, w_ff2_v = view("w_ff1", d, 4 * d), view("w_ff2", 4 * d, d)

    def whole(nm):
        r0, c0, rows, cols = lay[nm]
        return jnp.concatenate([b_all[s, r0:r0 + rows, c0:c0 + cols] for s in range(4)], axis=axis[nm])
    w_o_v, w_pg_v, w_a_v, w_p_v, w_b_full = whole("w_o"), whole("w_pg"), whole("w_a"), whole("w_p"), whole("w_b")

    def ya_fn(rows, vecs):
        o, hg = rows
        outs = []
        for h in range(HG_HEADS):
            oh = o[:, h * HG_DIM:(h + 1) * HG_DIM]
            outs.append(oh * lax.rsqrt(jnp.mean(oh * oh, axis=-1, keepdims=True) + RMS_EPS))
        y = jnp.concatenate(outs, axis=1) * vecs[0] * (hg * _sigmoid(hg))
        return [y], []
    (y_a,), _ = rowwise(ya_fn, [o_raw, (proj,) + mine["hg"]], [hg_norm_g], [(hw, BF16)], name="hgrn2_out_fwd")

    fb_pad = jnp.concatenate([fox_fb, jnp.zeros((1, LANES - fh), F32)], axis=1)

    def lf_fn(rows, vecs):
        u = rows[0] + vecs[0]
        return [jnp.minimum(u, 0.0) - jnp.log(1.0 + jnp.exp(-jnp.abs(u)))], []
    (lf,), _ = rowwise(lf_fn, [(proj,) + mine["ff"]], [fb_pad], [(LANES, F32)], name="fox_logf")
    c_cum = seq_cumsum(lf, n_seq, seq, reverse=False, name="fox_cumsum")

    place = _fox_placement(fh)

    def prep_fn(rows, vecs):
        fq_, fk_, fv_, cc = rows
        pq, pk, aq, ak, oq, ok = vecs
        parts = jnp.concatenate(_split3(cc), axis=1)
        mm = lambda a_, b_: jnp.dot(a_, b_, preferred_element_type=F32)
        q_ = mm(fq_.astype(BF16), pq) + mm(parts, aq) + oq
        k_ = mm(fk_.astype(BF16), pk) + mm(parts, ak) + ok
        return [q_, k_, mm(fv_.astype(BF16), pk)], []
    wa = fh * FOX_AUG
    (qa, ka, va), _ = rowwise(prep_fn, [(proj,) + mine["fq"], (proj,) + mine["fk"], (proj,) + mine["fv"], c_cum],
                              [place[nm] for nm in ("pq", "pk", "aq", "ak", "oq", "ok")], [(wa, BF16)] * 3,
                              name="fox_prep")
    as_seq = lambda t2d: t2d.reshape(n_seq, seq, t2d.shape[1])
    o_fox, ox_fox, lse = fox_fwd(as_seq(qa), as_seq(ka), as_seq(va), name="fox_fwd")
    y_b = o_fox.reshape(T, wa)
    wb_pad = jnp.concatenate([w_b_full.reshape(fh, FOX_HDIM, d), jnp.zeros((fh, FOX_AUG - FOX_HDIM, d), BF16)],
                             axis=1).reshape(wa, d)

    pa = matmul_nn(y_a, w_a_v, name="proj_a")
    pb = matmul_nn(y_b, wb_pad, name="proj_b")

    def merge_fn(rows, vecs):
        ga, gb, a, b = rows
        return [_sigmoid(ga) * a + _sigmoid(gb) * b], []
    (merged,), _ = rowwise(merge_fn, [(proj,) + mine["ga"], (proj,) + mine["gb"], pa, pb], [], [(d, BF16)],
                           name="merge_fwd")
    fused_tm = 512

    def ln1_post(mix, aux, vecs):
        z = ALPHA * aux[0] + mix
        h = _ln_stats(z) * vecs[0] + vecs[1]
        return [z, h, h], []
    (z1, h1, h1b), _ = matmul_nn(merged, w_o_v, name="out_proj_ln1", tm=fused_tm, post=ln1_post, post_aux=[h0],
                                 post_vecs=[ln1_g, ln1_b], post_outs=[F32, F32, BF16])

    relu2 = lambda u: jnp.square(jnp.maximum(u, 0.0))
    act = matmul_nn(h1b, w_ff1_v, name="ff1", out_dtype=BF16, epilogue=relu2)
    pg = matmul_nn(h1b, w_pg_v, name="ple_gate")
    pe = matmul_nn(p_b, w_p_v, name="ple_embed")

    def head_post(ffv, aux, vecs):
        h1v, pgv, pev, t = aux
        g2, b2 = vecs
        sp = _sigmoid(pgv)
        z = ALPHA * h1v + ffv + sp * pev
        y = _ln_stats(z) * g2 + b2
        err = y - t
        loss_rows = 0.5 * jnp.mean(err * err, axis=-1, keepdims=True)
        dy = err * (1.0 / d)
        dz, dg2, db2 = _ln_bwd(z, dy, g2)
        loss_acc = jnp.broadcast_to(_colsum(loss_rows), (1, d))
        return [dz, dz, dz * pev * (sp * (1.0 - sp)), dz * sp], [dg2, db2, loss_acc]
    (dz2, dz2b, dpg, dpe), (g_ln2_g, g_ln2_b, loss_part) = matmul_nn(
        act, w_ff2_v, name="ff2_head", tm=fused_tm, post=head_post, post_aux=[h1, pg, pe, tgt],
        post_vecs=[ln2_g, ln2_b], post_outs=[F32, BF16, BF16, BF16], post_accs=[d, d, d])

    dact = lambda da, a: da * (2.0 * jnp.sqrt(a.astype(F32)))
    du = matmul_nn(dz2b, w_ff2_v, transpose_rhs=True, name="d_ff2", out_dtype=BF16, epilogue=dact, aux=act)
    dh1_pg = matmul_nn(dpg, w_pg_v, transpose_rhs=True, name="d_ple_gate")

    def ln1_bwd_post(dh1_ff, aux, vecs):
        dh1 = ALPHA * aux[0] + dh1_ff + aux[1]
        dz, dg, db = _ln_bwd(aux[2], dh1, vecs[0])
        return [dz, dz], [dg, db]
    (dz1, dz1b), (g_ln1_g, g_ln1_b) = matmul_nn(
        du, w_ff1_v, transpose_rhs=True, name="d_ff1_ln1", tm=fused_tm, post=ln1_bwd_post, post_aux=[dz2, dh1_pg, z1],
        post_vecs=[ln1_g], post_outs=[F32, BF16], post_accs=[d, d])

    def merge_bwd_post(dm, aux, vecs):
        ga, gb, a, b = aux
        sa, sb = _sigmoid(ga), _sigmoid(gb)
        return [dm * a * (sa * (1.0 - sa)), dm * b * (sb * (1.0 - sb)), dm * sa, dm * sb], []
    (dga, dgb, dma, dmb), _ = matmul_nn(
        dz1b, w_o_v, transpose_rhs=True, name="d_out_proj_merge", tm=fused_tm, post=merge_bwd_post,
        post_aux=[(proj,) + mine["ga"], (proj,) + mine["gb"], pa, pb], post_outs=[BF16] * 4)
    dya = matmul_nn(dma, w_a_v, transpose_rhs=True, name="d_proj_a")
    dyb = matmul_nn(dmb, wb_pad, transpose_rhs=True, name="d_proj_b", out_dtype=BF16)

    def ya_bwd_fn(rows, vecs):
        o, hg, dy = rows
        ng = vecs[0]
        sg = _sigmoid(hg)
        gate = hg * sg
        dn_parts, do_parts, n_parts = [], [], []
        for h in range(HG_HEADS):
            hs = slice(h * HG_DIM, (h + 1) * HG_DIM)
            oh = o[:, hs]
            r = lax.rsqrt(jnp.mean(oh * oh, axis=-1, keepdims=True) + RMS_EPS)
            nh = oh * r
            dn = dy[:, hs] * ng[:, hs] * gate[:, hs]
            do_parts.append(r * (dn - nh * jnp.mean(dn * nh, axis=-1, keepdims=True)))
            n_parts.append(nh)
        nrm = jnp.concatenate(n_parts, axis=1)
        dhg = dy * nrm * ng * (sg * (1.0 + hg * (1.0 - sg)))
        return [jnp.concatenate(do_parts, axis=1), dhg], [_colsum(dy * nrm * gate)]
    (do_raw, dhg), (g_norm_g,) = rowwise(ya_bwd_fn, [o_raw, (proj,) + mine["hg"], dya], [hg_norm_g],
                                         [(hw, F32), (hw, BF16)], [hw], name="hgrn2_out_bwd")
    dhq, dhf, dhi, g_lb = hgrn2_bwd(proj, [mine["hq"][0], mine["hf"][0], mine["hi"][0]], lb, do_raw, hg_states,
                                    n_seq, seq, name="hgrn2_bwd")

    do_fox = as_seq(dyb)
    dqa, dka, dva, dsum = fox_bwd(as_seq(qa), as_seq(ka), as_seq(va), do_fox, ox_fox, lse, name="fox_bwd")

    def unprep_fn(rows, vecs):
        mm = lambda a_, b_: jnp.dot(a_.astype(BF16), b_, preferred_element_type=F32)
        return [mm(rows[0], vecs[0]), mm(rows[1], vecs[1]), mm(rows[2], vecs[1])], []
    (dfq, dfk, dfv), _ = rowwise(unprep_fn, [dqa.reshape(T, wa), dka.reshape(T, wa), dva.reshape(T, wa)],
                                 [place["pqt"], place["pkt"]], [(hw, BF16)] * 3, name="fox_unprep")
    dc = -dsum.reshape(n_seq, fh, seq).transpose(0, 2, 1).reshape(T, fh)
    dc = jnp.concatenate([dc, jnp.zeros((T, LANES - fh), F32)], axis=1)
    dlf = seq_cumsum(dc, n_seq, seq, reverse=True, name="fox_cumsum_bwd")

    def lf_bwd_fn(rows, vecs):
        u = rows[0] + vecs[0]
        du_ = rows[1] * _sigmoid(-u)
        return [du_], [_colsum(du_)]
    (dff_,), (g_fb,) = rowwise(lf_bwd_fn, [(proj,) + mine["ff"], dlf], [fb_pad], [(LANES, BF16)], [LANES],
                               name="fox_logf_bwd")

    dproj = jnp.concatenate([dga, dgb, dhq, dhf, dhi, dhg, dfq, dfk, dfv, dff_], axis=1)

    gfull = {
        "w_a": matmul_tn(y_a, dma, name="g_w_a"),
        "w_b": matmul_tn(y_b, dmb, name="g_w_b").reshape(fh, FOX_AUG, d)[:, :FOX_HDIM].reshape(hw, d),
        "w_o": matmul_tn(merged, dz1b, name="g_w_o"),
        "w_ff1": matmul_tn(h1b, du, name="g_w_ff1"),
        "w_ff2": matmul_tn(act, dz2b, name="g_w_ff2"),
        "w_pg": matmul_tn(h1b, dpg, name="g_w_pg"),
        "w_p": matmul_tn(p_b, dpe, name="g_w_p"),
    }

    def chip_parts(nm, s):
        g = gfull[nm]
        n = g.shape[axis[nm]] // 4
        return lax.slice_in_dim(g, s * n, (s + 1) * n, axis=axis[nm])
    me = 2 * lax.axis_index("x") + lax.axis_index("y")
    core = lax.axis_index("c")

    def sum2_fn(rows, vecs):
        s = rows[0] + rows[1].astype(F32)
        return [s, s], []

    def sum4_fn(rows, vecs):
        a, r0, r1, r2 = rows
        return [((a + r0.astype(F32)) + r1.astype(F32)) + r2.astype(F32)], []

    def chip_pair_sum(g, tag):
        h, cols = g.shape[1] // 2, g.shape[2]
        keep = lax.dynamic_slice_in_dim(g, core * h, h, axis=1)
        give = lax.dynamic_slice_in_dim(g, (1 - core) * h, h, axis=1).astype(BF16)
        (from_core,) = swap_cores([give], name="swap_partials_" + tag)
        (s32, s16), _ = rowwise(sum2_fn, [keep.reshape(4 * h, cols), from_core.reshape(4 * h, cols)], [],
                                [(cols, F32), (cols, BF16)], name="sum_cores_" + tag, tm=SUM_TILE)
        return s32.reshape(4, h, cols), s16.reshape(4, h, cols)

    def chip_sum(pr, gt, tag):
        own = lax.dynamic_index_in_dim(pr, me, axis=0, keepdims=False)
        (q,), _ = rowwise(sum4_fn, [own, gt[0], gt[1], gt[2]], [], [(own.shape[1], F32)], name="sum_chips_" + tag,
                          tm=SUM_TILE)
        return q

    grads_b = jnp.stack([pack_b({nm: chip_parts(nm, s) for nm in names if nm != "w_in"}, d) for s in range(4)])
    pair_rest, pair_rest_b = chip_pair_sum(grads_b, "rest")
    gw_in_mine, (got_rest,) = matmul_tn(h0b, dproj, name="g_w_in", rider=scatter_rider([pair_rest_b]))
    gfull["w_in"] = jnp.concatenate([gw_in_mine[:, mine[nm][0]:mine[nm][0] + orig[nm][1]]
                                     for nm in ["hq", "hf", "hi", "hg", "fq", "fk", "fv", "ff", "ga", "gb"]], axis=1)
    grads_a = jnp.stack([pack_a(chip_parts("w_in", s)) for s in range(4)])
    pair_in, pair_in_b = chip_pair_sum(grads_a, "w_in")
    def ln0_bwd_post(dh0_in, aux, vecs):
        dx, dg, db = _ln_bwd(aux[1], dh0_in + ALPHA * aux[0], vecs[0])
        return [dx], [dg, db]
    ((dx,), (g_ln0_g, g_ln0_b)), (got_in,) = matmul_nn(
        dproj, win_mine, transpose_rhs=True, name="d_in_proj_ln0", tm=fused_tm, post=ln0_bwd_post,
        post_aux=[dz1, x2], post_vecs=[vec(ln0_g)], post_outs=[F32], post_accs=[d, d],
        rider=scatter_rider([pair_in_b]))
    q_half = [chip_sum(pair_in, got_in, "w_in"), chip_sum(pair_rest, got_rest, "rest")]
    q_other = swap_cores(q_half, name="swap_halves")
    g_a, g_b = [jnp.concatenate([jnp.where(core == 0, mine_, other), jnp.where(core == 0, other, mine_)], axis=0)
                for mine_, other in zip(q_half, q_other)]
    g_shards = unpack_b(g_b, lay)
    g_shards["w_in"] = g_a[:, :in_cols]

    def row1024(*parts):
        r = jnp.concatenate([q.reshape(1, -1) for q in parts], axis=1)
        return jnp.concatenate([r, jnp.zeros((1, PACK_W - r.shape[1]), F32)], axis=1) if r.shape[1] < PACK_W else r
    small_rows = [row1024(g_ln0_g), row1024(g_ln0_b), row1024(g_ln1_g), row1024(g_ln1_b), row1024(g_ln2_g),
                  row1024(g_ln2_b), row1024(g_norm_g, g_lb), row1024(g_fb[:, :fh], loss_part[:, :1])]
    small = allreduce_small(jnp.concatenate(small_rows, axis=0), name="allreduce_small")
    s_ln0_g, s_ln0_b, s_ln1_g, s_ln1_b, s_ln2_g, s_ln2_b = [small[r:r + 1] for r in range(6)]
    s_norm_g, s_lb = small[6:7, :hw], small[6:7, hw:2 * hw]
    s_fb, loss = small[7:8, :fh], small[7, fh]
    p0 = probs[0:1]
    jac = p0 * (1.0 - p0)
    s_hg_lb = jnp.concatenate([s_lb * jac, -s_lb * jac], axis=0)

    small_w = [vec(ln0_g), vec(ln0_b), ln1_g, ln1_b, ln2_g, ln2_b, hg_lb.reshape(1, -1), hg_norm_g, fox_fb]
    small_g = [s_ln0_g, s_ln0_b, s_ln1_g, s_ln1_b, s_ln2_g, s_ln2_b, s_hg_lb.reshape(1, -1), s_norm_g, s_fb]
    small_m = [vec(m_ln0_g), vec(m_ln0_b), m_ln1_g, m_ln1_b, m_ln2_g, m_ln2_b, m_hg_lb.reshape(1, -1), m_hg_norm_g, m_fox_fb]
    small_v = [vec(v_ln0_g), vec(v_ln0_b), v_ln1_g, v_ln1_b, v_ln2_g, v_ln2_b, v_hg_lb.reshape(1, -1), v_hg_norm_g, v_fox_fb]
    pad_rows = lambda lst, fill: jnp.concatenate(
        [row1024(a) if fill == 0.0 else jnp.concatenate([a.reshape(1, -1), jnp.full((1, PACK_W - a.size), fill, F32)], axis=1)
         for a in lst] + [jnp.full((16 - len(lst), PACK_W), fill, F32)], axis=0)
    sd, sm, sv = adamw(pad_rows(small_w, 0.0), pad_rows(small_g, 0.0), pad_rows(small_m, 0.0), pad_rows(small_v, 1.0),
                       name="adamw_small")
    small_shapes = [ln0_g.shape, ln0_b.shape, ln1_g.shape, ln1_b.shape, ln2_g.shape, ln2_b.shape, hg_lb.shape,
                    hg_norm_g.shape, fox_fb.shape]
    take = lambda buf: [buf[r, :int(np.prod(shp))].reshape(shp) for r, shp in enumerate(small_shapes)]
    sg_out, sd_out, sm_out, sv_out = [g.reshape(shp) for g, shp in zip(small_g, small_shapes)], take(sd), take(sm), take(sv)

    big_out = {}
    for nm in names:
        delta, m2, v2 = adamw(big[nm], g_shards[nm], big_m[nm], big_v[nm], name="adamw_" + nm)
        big_out[nm] = (g_shards[nm][None], delta[None], m2[None], v2[None])

    def ordered(k):
        sm_ = [sg_out, sd_out, sm_out, sv_out][k]
        bg = lambda nm: big_out[nm][k]
        return [sm_[0], sm_[1], bg("w_in"), sm_[6], sm_[7], sm_[8], bg("w_a"), bg("w_b"), bg("w_o"), sm_[2], sm_[3],
                bg("w_ff1"), bg("w_ff2"), bg("w_pg"), bg("w_p"), sm_[4], sm_[5]]
    grad_x = dx.reshape(n_seq, seq, d)
    return (loss, grad_x, *ordered(0), *ordered(1), *ordered(2), *ordered(3))
```

```python
import functools
from typing import NamedTuple, Optional

import numpy as np
import jax
import jax.numpy as jnp
from jax import lax
from jax.experimental import pallas as pl
from jax.experimental.pallas import tpu as pltpu

F32 = jnp.float32
BF16 = jnp.bfloat16
MESH = pl.DeviceIdType.MESH

VMEM_LIMIT_BYTES = 48 * 1024 * 1024
LANES = 128
HG_HEADS = 4
HG_DIM = 128
HG_BLK = 16
HG_TILE = 256
HG_SLOTS = 4
FOX_HDIM = 64
FOX_AUG = 128
FOX_TQ = 1024
FOX_FWD_HEADS = 1
LN_EPS = 1e-5
RMS_EPS = 1e-6
DEPTH = 1
ALPHA = (2.0 * DEPTH) ** 0.25
ADAM_LR, ADAM_B1, ADAM_B2, ADAM_EPS, ADAM_WD, ADAM_STEP = 0.001, 0.9, 0.999, 1e-08, 0.01, 10
NEG_INF = -1e30


def _cparams(sem):
    return pltpu.CompilerParams(dimension_semantics=sem, vmem_limit_bytes=VMEM_LIMIT_BYTES)


def _tile(n, cap):
    if n <= cap:
        return n
    best = None
    for t in range(LANES, cap + 1, LANES):
        if n % t == 0:
            best = t
    assert best is not None, (n, cap)
    return best


class WView(NamedTuple):
    arr: jax.Array
    r0: int
    c0: int
    k: int
    n: int
    split: Optional[int]


def matmul_nn(a, w, *, name, transpose_rhs=False, out_dtype=F32, epilogue=None, aux=None, tm=1024, rider=None,
              post=None, post_aux=(), post_vecs=(), post_outs=(), post_accs=()):
    wv = w if isinstance(w, WView) else WView(w[None], 0, 0, w.shape[0], w.shape[1], None)
    rows_s = wv.k // 4 if wv.split == 0 else wv.k
    cols_s = wv.n // 4 if wv.split == 1 else wv.n
    tr, tc = _tile(rows_s, 1152), _tile(cols_s, 1152)
    assert wv.r0 % tr == 0 and wv.c0 % tc == 0
    T, K = a.shape
    N, tn, tk = (wv.k, tr, tc) if transpose_rhs else (wv.n, tc, tr)
    assert K == (wv.n if transpose_rhs else wv.k)
    tm = min(tm, T)
    assert T % tm == 0
    nk = K // tk

    def w_block(ri, ci):
        if wv.split == 0:
            return (ri * tr) // rows_s, (wv.r0 + (ri * tr) % rows_s) // tr, wv.c0 // tc + ci
        if wv.split == 1:
            return (ci * tc) // cols_s, wv.r0 // tr + ri, (wv.c0 + (ci * tc) % cols_s) // tc
        return 0, wv.r0 // tr + ri, wv.c0 // tc + ci

    fused = post is not None
    assert not fused or N == tn
    aux_list = list(post_aux) if fused else ([aux] if aux is not None else [])
    aux_list = [x if isinstance(x, tuple) else (x, 0, x.shape[1]) for x in aux_list]
    vec_list = list(post_vecs)
    out_dtypes = list(post_outs) if fused else [out_dtype]
    n_aux, n_vec, n_out, n_acc = len(aux_list), len(vec_list), len(out_dtypes), len(post_accs)

    def body(*refs):
        a_ref, w_ref = refs[:2]
        aux_refs = refs[2:2 + n_aux]
        vec_refs = refs[2 + n_aux:2 + n_aux + n_vec]
        out_refs = refs[2 + n_aux + n_vec:2 + n_aux + n_vec + n_out]
        sum_refs = refs[2 + n_aux + n_vec + n_out:2 + n_aux + n_vec + n_out + n_acc]
        acc_ref = refs[-1]
        m, k = pl.program_id(1), pl.program_id(2)
        if transpose_rhs:
            part = lax.dot_general(a_ref[...], w_ref[...], (((1,), (1,)), ((), ())), preferred_element_type=F32)
        else:
            part = jnp.dot(a_ref[...], w_ref[...], preferred_element_type=F32)

        def write(res):
            if not fused:
                if epilogue is not None:
                    res = epilogue(res) if not aux_refs else epilogue(res, aux_refs[0][...])
                out_refs[0][...] = res.astype(out_dtype)
                return
            outs, sums = post(res, [r[...] for r in aux_refs], [v[...] for v in vec_refs])
            assert len(outs) == n_out and len(sums) == n_acc
            for r, val in zip(out_refs, outs):
                r[...] = val.astype(r.dtype)
            for r, val in zip(sum_refs, sums):
                def first_rows(r=r, val=val):
                    r[...] = val

                def later_rows(r=r, val=val):
                    r[...] += val
                pl.when(m == 0)(first_rows)
                pl.when(m > 0)(later_rows)

        if nk == 1:
            write(part)
        else:
            @pl.when(k == 0)
            def _():
                acc_ref[...] = part

            @pl.when(k > 0)
            def _():
                acc_ref[...] += part

            @pl.when(k == nk - 1)
            def _():
                write(acc_ref[...])

    w_index = (lambda n, m, k: w_block(n, k)) if transpose_rhs else (lambda n, m, k: w_block(k, n))
    in_specs = [pl.BlockSpec((tm, tk), lambda n, m, k: (m, k)),
                pl.BlockSpec((None, tr, tc), w_index)]
    args = [a, wv.arr]
    for arr, off, width in aux_list:
        assert width == N and off % tn == 0
        in_specs.append(pl.BlockSpec((tm, tn), functools.partial(lambda n, m, k, blk: (m, blk + n), blk=off // tn)))
        args.append(arr)
    for v in vec_list:
        in_specs.append(pl.BlockSpec(v.shape, lambda n, m, k: (0, 0)))
        args.append(v)
    out_specs = [pl.BlockSpec((tm, tn), lambda n, m, k: (m, n)) for _ in out_dtypes]
    out_specs += [pl.BlockSpec((1, tn), lambda n, m, k: (0, 0)) for _ in post_accs]
    out_shape = [jax.ShapeDtypeStruct((T, N), dt) for dt in out_dtypes]
    out_shape += [jax.ShapeDtypeStruct((1, N), F32) for _ in post_accs]
    scratch = [pltpu.VMEM((tm, tn) if nk > 1 else (8, LANES), F32)]
    grid = (N // tn, T // tm, nk)
    sem = ("arbitrary",) * 3 if (n_acc or rider is not None) else ("parallel", "parallel", "arbitrary")
    params = pltpu.CompilerParams(dimension_semantics=sem, vmem_limit_bytes=VMEM_LIMIT_BYTES,
                                  has_side_effects=rider is not None)
    if rider is not None:
        r_in, r_out, r_sems = rider.specs()
        body = rider.wrap(body, len(in_specs), len(out_specs), 3)
        in_specs, out_specs, out_shape = in_specs + r_in, out_specs + r_out, out_shape + rider.out_shape
        scratch, args = scratch + r_sems, args + list(rider.ins)
    res = pl.pallas_call(body, name=name, grid=grid, in_specs=in_specs, out_specs=out_specs, out_shape=out_shape,
                         scratch_shapes=scratch, compiler_params=params)(*args)
    main = (list(res[:n_out]), list(res[n_out:n_out + n_acc])) if fused else res[0]
    return main if rider is None else (main, list(res[n_out + n_acc:]))


def matmul_tn(a, b, *, name, tk=1024, rider=None):
    T, M = a.shape
    T2, N = b.shape
    tk = min(tk, T)
    assert T == T2 and T % tk == 0
    tm = _tile(M, 1024)
    tn = _tile(N, 1152)

    def body(a_ref, b_ref, o_ref):
        k = pl.program_id(2)
        part = lax.dot_general(a_ref[...], b_ref[...], (((0,), (0,)), ((), ())), preferred_element_type=F32)

        @pl.when(k == 0)
        def _():
            o_ref[...] = part

        @pl.when(k > 0)
        def _():
            o_ref[...] += part

    in_specs = [pl.BlockSpec((tk, tm), lambda m, n, k: (k, m)), pl.BlockSpec((tk, tn), lambda m, n, k: (k, n))]
    out_specs = [pl.BlockSpec((tm, tn), lambda m, n, k: (m, n))]
    out_shape = [jax.ShapeDtypeStruct((M, N), F32)]
    grid = (M // tm, N // tn, T // tk)
    if rider is None:
        return pl.pallas_call(body, name=name, grid=grid, in_specs=in_specs, out_specs=out_specs, out_shape=out_shape,
                              compiler_params=_cparams(("parallel", "parallel", "arbitrary")))(a, b)[0]
    r_in, r_out, r_sems = rider.specs()
    res = pl.pallas_call(
        rider.wrap(body, 2, 1, 3), name=name, grid=grid, in_specs=in_specs + r_in, out_specs=out_specs + r_out,
        out_shape=out_shape + rider.out_shape, scratch_shapes=r_sems,
        compiler_params=pltpu.CompilerParams(dimension_semantics=("arbitrary",) * 3,
                                             vmem_limit_bytes=VMEM_LIMIT_BYTES, has_side_effects=True),
    )(a, b, *rider.ins)
    return res[0], list(res[1:])


def rowwise(fn, rows, vecs, outs, accs=(), *, name, tm=512, rider=None):
    rows = [r if isinstance(r, tuple) else (r, 0, r.shape[1]) for r in rows]
    T = rows[0][0].shape[0]
    tm = min(tm, T)
    assert T % tm == 0
    n_rows, n_vecs, n_outs, n_accs = len(rows), len(vecs), len(outs), len(accs)

    def body(*refs):
        row_refs = refs[:n_rows]
        vec_refs = refs[n_rows:n_rows + n_vecs]
        out_refs = refs[n_rows + n_vecs:n_rows + n_vecs + n_outs]
        acc_refs = refs[n_rows + n_vecs + n_outs:]
        out_vals, acc_vals = fn([r[...] for r in row_refs], [v[...] for v in vec_refs])
        assert len(out_vals) == n_outs and len(acc_vals) == n_accs
        for r, val in zip(out_refs, out_vals):
            r[...] = val.astype(r.dtype)
        if n_accs:
            i = pl.program_id(0)

            @pl.when(i == 0)
            def _():
                for r in acc_refs:
                    r[...] = jnp.zeros_like(r)

            for r, val in zip(acc_refs, acc_vals):
                r[...] += val

    in_specs = []
    for arr, off, width in rows:
        assert off % width == 0
        in_specs.append(pl.BlockSpec((tm, width), functools.partial(lambda i, blk: (i, blk), blk=off // width)))
    for v in vecs:
        in_specs.append(pl.BlockSpec(v.shape, lambda i: (0, 0)))
    out_specs = [pl.BlockSpec((tm, w), lambda i: (i, 0)) for w, _ in outs]
    out_specs += [pl.BlockSpec((1, w), lambda i: (0, 0)) for w in accs]
    out_shape = [jax.ShapeDtypeStruct((T, w), dt) for w, dt in outs]
    out_shape += [jax.ShapeDtypeStruct((1, w), F32) for w in accs]
    args = [r[0] for r in rows] + list(vecs)
    if rider is None:
        res = pl.pallas_call(body, name=name, grid=(T // tm,), in_specs=in_specs, out_specs=out_specs,
                             out_shape=out_shape,
                             compiler_params=_cparams(("arbitrary",) if n_accs else ("parallel",)))(*args)
        return res[:n_outs], res[n_outs:]
    r_in, r_out, r_sems = rider.specs()
    res = pl.pallas_call(
        rider.wrap(body, len(in_specs), len(out_specs), 1), name=name, grid=(T // tm,), in_specs=in_specs + r_in,
        out_specs=out_specs + r_out, out_shape=out_shape + rider.out_shape, scratch_shapes=r_sems,
        compiler_params=pltpu.CompilerParams(dimension_semantics=("arbitrary",), vmem_limit_bytes=VMEM_LIMIT_BYTES,
                                             has_side_effects=True),
    )(*args, *rider.ins)
    return res[:n_outs], res[n_outs:n_outs + n_accs], list(res[n_outs + n_accs:])


def _colsum(x):
    return jnp.sum(x, axis=0, keepdims=True)


def _sigmoid(x):
    return 1.0 / (1.0 + jnp.exp(-x))


def _ln_stats(z):
    mu = jnp.mean(z, axis=-1, keepdims=True)
    zc = z - mu
    var = jnp.mean(zc * zc, axis=-1, keepdims=True)
    return zc * lax.rsqrt(var + LN_EPS)


def _ln_bwd(zhat_src, dy, g):
    mu = jnp.mean(zhat_src, axis=-1, keepdims=True)
    zc = zhat_src - mu
    var = jnp.mean(zc * zc, axis=-1, keepdims=True)
    rstd = lax.rsqrt(var + LN_EPS)
    zh = zc * rstd
    dzh = dy * g
    dz = rstd * (dzh - jnp.mean(dzh, axis=-1, keepdims=True) - zh * jnp.mean(dzh * zh, axis=-1, keepdims=True))
    return dz, _colsum(dy * zh), _colsum(dy)


def _hg_constants():
    r = np.arange(HG_TILE)
    same = (r[:, None] // HG_BLK) == (r[None, :] // HG_BLK)
    lower = (same & (r[None, :] <= r[:, None])).astype(np.float32)
    upper = (same & (r[None, :] >= r[:, None])).astype(np.float32)
    total = same.astype(np.float32)
    c = np.arange(2 * HG_DIM)
    bd = ((c[:, None] // HG_DIM) == (c[None, :] // HG_DIM)).astype(np.float32)
    pair_t = np.array([t for t, _ in _HG_PAIRS])
    pair_s = np.array([s for _, s in _HG_PAIRS])
    sel_t = (pair_t[None, :] == np.arange(HG_BLK)[:, None]).astype(np.float32)
    sel_s = (pair_s[None, :] == np.arange(HG_BLK)[:, None]).astype(np.float32)
    as_bf = lambda m: jnp.asarray(m, dtype=BF16)
    return as_bf(lower), as_bf(upper), as_bf(total), as_bf(bd), as_bf(sel_t), as_bf(sel_s)


_HG_HALF = HG_BLK // 2
_HG_PAIRS = ([(t, s) for t in range(_HG_HALF, HG_BLK) for s in range(HG_BLK)]
             + [(t, s) for t in range(_HG_HALF) for s in range(_HG_HALF)])
HG_STACK = len(_HG_PAIRS)
_HG_SLABS = ([((t - _HG_HALF) * HG_BLK, (t,), HG_BLK) for t in range(_HG_HALF, HG_BLK)]
             + [(_HG_HALF * HG_BLK + t * _HG_HALF, (t, t + 1), _HG_HALF) for t in range(0, _HG_HALF, 2)])


def _stack_by_s(x):
    return jnp.concatenate([x] * _HG_HALF + [x[:_HG_HALF]] * _HG_HALF, axis=0)


def _stack_by_t(x):
    w = x.shape[1]
    return jnp.concatenate([jnp.broadcast_to(x[t:t + 1], (HG_BLK, w)) for t in range(_HG_HALF, HG_BLK)]
                           + [jnp.broadcast_to(x[t:t + 1], (_HG_HALF, w)) for t in range(_HG_HALF)], axis=0)


def _keep_bf16_bits(x):
    bits = lax.bitcast_convert_type(x, jnp.int32) & jnp.int32(-65536)
    return lax.bitcast_convert_type(bits, F32)


def _head_sums(stack_ref, slot, bd):
    pair = bd.shape[0]
    return jnp.concatenate([jnp.dot(stack_ref[slot, :, c0:c0 + pair], bd, preferred_element_type=F32)
                            for c0 in range(0, stack_ref.shape[2], pair)], axis=1)


def _split3(x):
    hi = _keep_bf16_bits(x)
    r1 = x - hi
    mid = _keep_bf16_bits(r1)
    lo = _keep_bf16_bits(r1 - mid)
    return hi.astype(BF16), mid.astype(BF16), lo.astype(BF16)


def _dot3(m01, x):
    hi, mid, lo = _split3(x)
    d = lambda p: jnp.dot(m01, p, preferred_element_type=F32)
    return (d(lo) + d(mid)) + d(hi)


def _hg_prologue(hq, hf, lb, lower, total):
    sq = _sigmoid(hq)
    q = hq * sq
    sg = _sigmoid(hf)
    f = lb + (1.0 - lb) * sg
    g = jnp.log(f)
    k = 1.0 - f
    b = _dot3(lower, g)
    bl = _dot3(total, g)
    return q, k, f, sg, sq, b, bl


def _stack16(fn):
    return [fn(t) for t in range(HG_BLK)]


def hgrn2_fwd(proj, offs, lb, n_seq, seq, *, name, rider=None):
    T = n_seq * seq
    W = HG_HEADS * HG_DIM
    n_tiles = seq // HG_TILE
    nb = HG_TILE // HG_BLK
    lower, _, total, bd, sel_t, _ = _hg_constants()

    def body(hq_ref, hf_ref, hi_ref, lb_ref, lower_ref, total_ref, bd_ref, selt_ref,
             o_ref, st_out_ref,
             st_ref, q_s, k_s, v_s, b_s, qt_s, kt_s, d_s, p_s):
        @pl.when(pl.program_id(1) == 0)
        def _():
            st_ref[...] = jnp.zeros_like(st_ref)

        q, k, _, _, _, b, bl = _hg_prologue(hq_ref[...], hf_ref[...], lb_ref[...], lower_ref[...], total_ref[...])
        q_s[...] = q
        k_s[...] = k
        v_s[...] = hi_ref[...]
        b_s[...] = b
        qt_s[...] = q * jnp.exp(b)
        kt_s[...] = k * jnp.exp(jnp.minimum(bl - b, 0.0))
        d_s[...] = jnp.exp(bl)
        rowi = lax.broadcasted_iota(jnp.int32, (HG_BLK, W), 0)

        def block(i, slot):
            r0 = pl.multiple_of(i * HG_BLK, HG_BLK)
            rows = pl.ds(r0, HG_BLK)
            qi, ki, vi, bi = q_s[rows, :], k_s[rows, :], v_s[rows, :], b_s[rows, :]
            for off, ts, n in _HG_SLABS:
                slab = [jnp.where(rowi[:n] <= t, jnp.exp(jnp.minimum(bi[t:t + 1, :] - bi[:n], 0.0)), 0.0)
                        * qi[t:t + 1, :] * ki[:n] for t in ts]
                p_s[slot, pl.ds(off, HG_BLK), :] = jnp.concatenate(slab, axis=0).astype(BF16)
            a_b = _head_sums(p_s, slot, bd_ref[...])
            o_blk = jnp.dot(selt_ref[...], (a_b * _stack_by_s(vi)).astype(BF16), preferred_element_type=F32)
            qti, kti, di = qt_s[rows, :], kt_s[rows, :], d_s[rows, :]
            outs = []
            for h in range(HG_HEADS):
                hs = slice(h * HG_DIM, (h + 1) * HG_DIM)
                st_h = st_ref[hs, :]
                st_out_ref[i, hs, :] = st_h
                outs.append(lax.dot_general(qti[:, hs].astype(BF16), st_h.astype(BF16),
                                            (((1,), (1,)), ((), ())), preferred_element_type=F32))
                upd = lax.dot_general(vi[:, hs].astype(BF16), kti[:, hs].astype(BF16),
                                      (((0,), (0,)), ((), ())), preferred_element_type=F32)
                st_ref[hs, :] = st_h * di[0:1, hs] + upd
            o_ref[rows, :] = o_blk + jnp.concatenate(outs, axis=1)

        def some_blocks(jj, carry):
            for slot in range(HG_SLOTS):
                block(HG_SLOTS * jj + slot, slot)
            return carry

        lax.fori_loop(0, nb // HG_SLOTS, some_blocks, 0)

    col = lambda off: functools.partial(lambda s, t, blk: (s * n_tiles + t, blk), blk=off // W)
    const = lambda m: pl.BlockSpec(m.shape, lambda s, t: (0, 0))
    tile_f32 = pltpu.VMEM((HG_TILE, W), F32)
    in_specs = [pl.BlockSpec((HG_TILE, W), col(offs[0])), pl.BlockSpec((HG_TILE, W), col(offs[1])),
                pl.BlockSpec((HG_TILE, W), col(offs[2])), const(lb), const(lower), const(total), const(bd),
                const(sel_t)]
    out_specs = [pl.BlockSpec((HG_TILE, W), lambda s, t: (s * n_tiles + t, 0)),
                 pl.BlockSpec((nb, W, HG_DIM), lambda s, t: (s * n_tiles + t, 0, 0))]
    out_shape = [jax.ShapeDtypeStruct((T, W), F32), jax.ShapeDtypeStruct((T // HG_BLK, W, HG_DIM), F32)]
    scratch = [pltpu.VMEM((W, HG_DIM), F32)] + [tile_f32] * 7 + [pltpu.VMEM((HG_SLOTS, HG_STACK, W), BF16)]
    args = [proj, proj, proj, lb, lower, total, bd, sel_t]
    params = _cparams(("arbitrary", "arbitrary"))
    if rider is not None:
        r_in, r_out, r_sems = rider.specs()
        body = rider.wrap(body, len(in_specs), len(out_specs), 2)
        in_specs, out_specs, out_shape = in_specs + r_in, out_specs + r_out, out_shape + rider.out_shape
        scratch, args = scratch + r_sems, args + rider.ins
        params = pltpu.CompilerParams(dimension_semantics=("arbitrary", "arbitrary"),
                                      vmem_limit_bytes=VMEM_LIMIT_BYTES, has_side_effects=True)
    res = pl.pallas_call(body, name=name, grid=(n_seq, n_tiles), in_specs=in_specs, out_specs=out_specs,
                         out_shape=out_shape, scratch_shapes=scratch, compiler_params=params)(*args)
    return res[0], res[1], list(res[2:])


def hgrn2_bwd(proj, offs, lb, do, states, n_seq, seq, *, name):
    T = n_seq * seq
    W = HG_HEADS * HG_DIM
    n_tiles = seq // HG_TILE
    nb = HG_TILE // HG_BLK
    lower, upper, total, bd, sel_t, sel_s = _hg_constants()

    def body(hq_ref, hf_ref, hi_ref, do_ref, st_in_ref, lb_ref, lower_ref, upper_ref, total_ref, bd_ref,
             selt_ref, sels_ref,
             dhq_ref, dhf_ref, dhi_ref, dlb_ref,
             dst_ref, q_s, k_s, v_s, b_s, qt_s, kt_s, d_s, eb_s, ekb_s, dq_s, dk_s, db_s, dv_s,
             p_s, e_s, w_s):
        first = jnp.logical_and(pl.program_id(0) == 0, pl.program_id(1) == 0)

        @pl.when(first)
        def _():
            dlb_ref[...] = jnp.zeros_like(dlb_ref)

        @pl.when(pl.program_id(1) == 0)
        def _():
            dst_ref[...] = jnp.zeros_like(dst_ref)

        hq, lbv = hq_ref[...], lb_ref[...]
        q, k, f, sg, sq, b, bl = _hg_prologue(hq, hf_ref[...], lbv, lower_ref[...], total_ref[...])
        eb = jnp.exp(b)
        ekb = jnp.exp(jnp.minimum(bl - b, 0.0))
        q_s[...] = q
        k_s[...] = k
        v_s[...] = hi_ref[...]
        b_s[...] = b
        eb_s[...] = eb
        ekb_s[...] = ekb
        qt_s[...] = q * eb
        kt_s[...] = k * ekb
        d_s[...] = jnp.exp(bl)
        rowi = lax.broadcasted_iota(jnp.int32, (HG_BLK, W), 0)
        last_row = rowi == HG_BLK - 1

        def block(i, slot):
            r0 = pl.multiple_of(i * HG_BLK, HG_BLK)
            rows = pl.ds(r0, HG_BLK)
            qi, ki, vi, bi, doi = q_s[rows, :], k_s[rows, :], v_s[rows, :], b_s[rows, :], do_ref[rows, :]
            for off, ts, n in _HG_SLABS:
                es = [jnp.where(rowi[:n] <= t, jnp.exp(jnp.minimum(bi[t:t + 1, :] - bi[:n], 0.0)), 0.0) for t in ts]
                sl = pl.ds(off, HG_BLK)
                e_s[slot, sl, :] = jnp.concatenate(es, axis=0)
                p_s[slot, sl, :] = jnp.concatenate([e * qi[t:t + 1, :] * ki[:n] for e, t in zip(es, ts)],
                                                   axis=0).astype(BF16)
                w_s[slot, sl, :] = jnp.concatenate([doi[t:t + 1, :] * vi[:n] for t in ts], axis=0).astype(BF16)
            a_b = _head_sums(p_s, slot, bd_ref[...])
            da_b = _head_sums(w_s, slot, bd_ref[...])
            x = da_b * e_s[slot]
            dq_in = jnp.dot(selt_ref[...], (x * _stack_by_s(ki)).astype(BF16), preferred_element_type=F32)
            dk_in = jnp.dot(sels_ref[...], (x * _stack_by_t(qi)).astype(BF16), preferred_element_type=F32)
            dv_in = jnp.dot(sels_ref[...], (a_b * _stack_by_t(doi)).astype(BF16), preferred_element_type=F32)
            qti, kti, di = qt_s[rows, :], kt_s[rows, :], d_s[rows, :]
            dqt, dkt, dvt, dd = [], [], [], []
            for h in range(HG_HEADS):
                hs = slice(h * HG_DIM, (h + 1) * HG_DIM)
                st_h = st_in_ref[i, hs, :]
                dst_h = dst_ref[hs, :]
                do_h, v_h = doi[:, hs].astype(BF16), vi[:, hs].astype(BF16)
                dst_b = dst_h.astype(BF16)
                dqt.append(jnp.dot(do_h, st_h.astype(BF16), preferred_element_type=F32))
                dkt.append(jnp.dot(v_h, dst_b, preferred_element_type=F32))
                dvt.append(lax.dot_general(kti[:, hs].astype(BF16), dst_b, (((1,), (1,)), ((), ())),
                                           preferred_element_type=F32))
                dd.append(jnp.sum(dst_h * st_h, axis=0, keepdims=True))
                upd = lax.dot_general(do_h, qti[:, hs].astype(BF16), (((0,), (0,)), ((), ())),
                                      preferred_element_type=F32)
                dst_ref[hs, :] = dst_h * di[0:1, hs] + upd
            dqt = jnp.concatenate(dqt, axis=1)
            dkt = jnp.concatenate(dkt, axis=1)
            dvt = jnp.concatenate(dvt, axis=1)
            dd = jnp.concatenate(dd, axis=1)
            dbl = jnp.sum(dkt * kti, axis=0, keepdims=True) + dd * di[0:1, :]
            db = qi * dq_in - ki * dk_in + dqt * qti - dkt * kti
            db_s[rows, :] = db + jnp.where(last_row, dbl, 0.0)
            dq_s[rows, :] = dq_in + dqt * eb_s[rows, :]
            dk_s[rows, :] = dk_in + dkt * ekb_s[rows, :]
            dv_s[rows, :] = dv_in + dvt

        def some_blocks(jj, carry):
            for slot in range(HG_SLOTS):
                block(nb - 1 - slot - HG_SLOTS * jj, slot)
            return carry

        lax.fori_loop(0, nb // HG_SLOTS, some_blocks, 0)

        dg = _dot3(upper_ref[...], db_s[...])
        dhq_ref[...] = (dq_s[...] * (sq * (1.0 + hq * (1.0 - sq)))).astype(dhq_ref.dtype)
        df = dg / f - dk_s[...]
        dhf_ref[...] = (df * (1.0 - lbv) * (sg * (1.0 - sg))).astype(dhf_ref.dtype)
        dhi_ref[...] = dv_s[...].astype(dhi_ref.dtype)
        dlb_ref[...] += _colsum(df * (1.0 - sg))

    rev = lambda s, t: s * n_tiles + (n_tiles - 1 - t)
    col = lambda off: functools.partial(lambda s, t, blk: (rev(s, t), blk), blk=off // W)
    const = lambda m: pl.BlockSpec(m.shape, lambda s, t: (0, 0))
    row = pl.BlockSpec((HG_TILE, W), lambda s, t: (rev(s, t), 0))
    tile_f32 = pltpu.VMEM((HG_TILE, W), F32)
    n2 = HG_STACK
    return pl.pallas_call(
        body, name=name,
        grid=(n_seq, n_tiles),
        in_specs=[pl.BlockSpec((HG_TILE, W), col(offs[0])), pl.BlockSpec((HG_TILE, W), col(offs[1])),
                  pl.BlockSpec((HG_TILE, W), col(offs[2])), row,
                  pl.BlockSpec((nb, W, HG_DIM), lambda s, t: (rev(s, t), 0, 0)),
                  const(lb), const(lower), const(upper), const(total), const(bd), const(sel_t), const(sel_s)],
        out_specs=[row, row, row, pl.BlockSpec((1, W), lambda s, t: (0, 0))],
        out_shape=[jax.ShapeDtypeStruct((T, W), BF16)] * 3 + [jax.ShapeDtypeStruct((1, W), F32)],
        scratch_shapes=[pltpu.VMEM((W, HG_DIM), F32)] + [tile_f32] * 13
                       + [pltpu.VMEM((HG_SLOTS, n2, W), BF16), pltpu.VMEM((HG_SLOTS, n2, W), F32),
                          pltpu.VMEM((HG_SLOTS, n2, W), BF16)],
        compiler_params=_cparams(("arbitrary", "arbitrary")),
    )(proj, proj, proj, do, states, lb, lower, upper, total, bd, sel_t, sel_s)


def _diag_mask(tq):
    return lax.broadcasted_iota(jnp.int32, (tq, tq), 1) <= lax.broadcasted_iota(jnp.int32, (tq, tq), 0)


def _qk(q, k):
    return lax.dot_general(q, k, (((1,), (1,)), ((), ())), preferred_element_type=F32)


def _causal_pairs(n, sweeps=1, by_key=False):
    if by_key:
        rows = [(i, j, 0) for j in range(n) for i in range(j, n)]
    else:
        rows = [(i, j, s) for i in range(n) for s in range(sweeps) for j in range(i + 1)]
    return tuple(jnp.asarray(np.array([r[c] for r in rows], np.int32)) for c in range(3))


def _fox_placement(fh):
    hw, wa = fh * FOX_HDIM, fh * FOX_AUG
    pq, pk = np.zeros((hw, wa), np.float32), np.zeros((hw, wa), np.float32)
    aq, ak = np.zeros((3 * LANES, wa), np.float32), np.zeros((3 * LANES, wa), np.float32)
    oq, ok = np.zeros((1, wa), np.float32), np.zeros((1, wa), np.float32)
    for h in range(fh):
        src, dst = np.arange(h * FOX_HDIM, (h + 1) * FOX_HDIM), np.arange(h * FOX_AUG, h * FOX_AUG + FOX_HDIM)
        pq[src, dst] = FOX_HDIM ** -0.5
        pk[src, dst] = 1.0
        gate = h * FOX_AUG + FOX_HDIM
        for r in range(3):
            aq[r * LANES + h, gate + r] = 1.0
            ak[r * LANES + h, gate + 3 + r] = -1.0
        oq[0, gate + 3:gate + 6] = 1.0
        ok[0, gate:gate + 3] = 1.0
    bf = lambda m: jnp.asarray(m, dtype=BF16)
    return {"pq": bf(pq), "pk": bf(pk), "aq": bf(aq), "ak": bf(ak), "oq": jnp.asarray(oq), "ok": jnp.asarray(ok),
            "pqt": bf(pq.T), "pkt": bf(pk.T)}


def _fox_specs(tq, fh, heads=1):
    groups = fh // heads

    def spec(tab):
        return pl.BlockSpec((None, tq, heads * FOX_AUG), lambda b, t, *tabs: (b // groups, tabs[tab][t], b % groups))
    return spec(0), spec(1)


def fox_fwd(qa, ka, va, *, name):
    n_seq, S, width = qa.shape
    fh = width // FOX_AUG
    nh = FOX_FWD_HEADS
    BH = n_seq * fh // nh
    tq = min(FOX_TQ, S)
    itab, jtab, _ = _causal_pairs(S // tq)

    def body(itab_ref, jtab_ref, q_ref, k_ref, v_ref, o_ref, ox_ref, lse_ref, *scratch):
        t = pl.program_id(1)
        i, j = itab_ref[t], jtab_ref[t]
        per_head = [scratch[4 * h:4 * h + 4] for h in range(nh)]

        @pl.when(j == 0)
        def _():
            for m_s, l_s, acc_s, acc_lo_s in per_head:
                m_s[...] = jnp.full_like(m_s, NEG_INF)
                l_s[...] = jnp.zeros_like(l_s)
                acc_s[...] = jnp.zeros_like(acc_s)
                acc_lo_s[...] = jnp.zeros_like(acc_lo_s)

        def step(on_diagonal):
            for h, (m_s, l_s, acc_s, acc_lo_s) in enumerate(per_head):
                lanes = slice(h * FOX_AUG, (h + 1) * FOX_AUG)
                s = _qk(q_ref[:, lanes], k_ref[:, lanes])
                if on_diagonal:
                    s = jnp.where(_diag_mask(tq), s, NEG_INF)
                m_prev = m_s[...]
                m_new = jnp.maximum(m_prev, jnp.max(s, axis=-1, keepdims=True))
                alpha = jnp.exp(m_prev - m_new)
                p = jnp.exp(s - m_new[:, 0:1])
                p_hi = p.astype(BF16)
                p_lo = (p - p_hi.astype(F32)).astype(BF16)
                v = v_ref[:, lanes]
                l_s[...] = alpha * l_s[...] + jnp.sum(p, axis=-1, keepdims=True)
                acc_s[...] = alpha * acc_s[...] + jnp.dot(p_hi, v, preferred_element_type=F32)
                acc_lo_s[...] = alpha * acc_lo_s[...] + jnp.dot(p_lo, v, preferred_element_type=F32)
                m_s[...] = m_new

        @pl.when(j < i)
        def _():
            step(False)

        @pl.when(j == i)
        def _():
            step(True)
            for h, (m_s, l_s, acc_s, acc_lo_s) in enumerate(per_head):
                lanes = slice(h * FOX_AUG, (h + 1) * FOX_AUG)
                inv_l = 1.0 / l_s[...]
                o_ref[:, lanes] = (acc_s[...] * inv_l).astype(o_ref.dtype)
                ox_ref[:, lanes] = (acc_s[...] + acc_lo_s[...]) * inv_l
                lse_ref[:, lanes] = m_s[...] + jnp.log(l_s[...])

    qspec, kspec = _fox_specs(tq, fh, nh)
    wide = jax.ShapeDtypeStruct((n_seq, S, width), F32)
    return pl.pallas_call(
        body, name=name,
        grid_spec=pltpu.PrefetchScalarGridSpec(
            num_scalar_prefetch=2, grid=(BH, itab.shape[0]),
            in_specs=[qspec, kspec, kspec],
            out_specs=[qspec, qspec, qspec],
            scratch_shapes=[pltpu.VMEM((tq, LANES), F32)] * (4 * nh)),
        out_shape=[jax.ShapeDtypeStruct((n_seq, S, width), BF16), wide, wide],
        compiler_params=_cparams(("parallel", "arbitrary")),
    )(itab, jtab, qa, ka, va)


def _fox_ds(q, k, v, do, ox, lse, on_diagonal):
    s = _qk(q, k)
    if on_diagonal:
        s = jnp.where(_diag_mask(s.shape[0]), s, NEG_INF)
    p = jnp.exp(s - lse[:, 0:1])
    delta = jnp.sum(do.astype(F32) * ox, axis=-1, keepdims=True)
    return p, p * (_qk(do, v) - delta)


def fox_bwd(qa, ka, va, do, ox, lse, *, name):
    n_seq, S, width = qa.shape
    fh = width // FOX_AUG
    BH = n_seq * fh
    tq = min(FOX_TQ, S)
    itab, jtab, _ = _causal_pairs(S // tq)

    def body(itab_ref, jtab_ref, q_ref, k_ref, v_ref, do_ref, ox_ref, lse_ref, dq_ref, dk_ref, dv_ref, dsum_ref):
        t = pl.program_id(1)
        i, j = itab_ref[t], jtab_ref[t]

        @pl.when(t == 0)
        def _():
            dq_ref[...] = jnp.zeros_like(dq_ref)
            dk_ref[...] = jnp.zeros_like(dk_ref)
            dv_ref[...] = jnp.zeros_like(dv_ref)
            dsum_ref[...] = jnp.zeros_like(dsum_ref)

        q_rows = pl.ds(pl.multiple_of(i * tq, tq), tq)
        k_rows = pl.ds(pl.multiple_of(j * tq, tq), tq)

        def step(on_diagonal):
            q, k, do = q_ref[...], k_ref[...], do_ref[...]
            p, ds = _fox_ds(q, k, v_ref[...], do, ox_ref[...], lse_ref[...], on_diagonal)
            ds_b = ds.astype(BF16)
            tn = (((0,), (0,)), ((), ()))
            dq_ref[q_rows, :] += jnp.dot(ds_b, k, preferred_element_type=F32)
            dk_ref[k_rows, :] += lax.dot_general(ds_b, q, tn, preferred_element_type=F32)
            dv_ref[k_rows, :] += lax.dot_general(p.astype(BF16), do, tn, preferred_element_type=F32)
            dsum_ref[:, k_rows] += _colsum(ds)

        @pl.when(j < i)
        def _():
            step(False)

        @pl.when(j == i)
        def _():
            step(True)

    qspec, kspec = _fox_specs(tq, fh)
    whole = pl.BlockSpec((None, S, FOX_AUG), lambda b, t, it, jt: (b // fh, 0, b % fh))
    wide = jax.ShapeDtypeStruct((n_seq, S, width), F32)
    return pl.pallas_call(
        body, name=name,
        grid_spec=pltpu.PrefetchScalarGridSpec(
            num_scalar_prefetch=2, grid=(BH, itab.shape[0]),
            in_specs=[qspec, kspec, kspec, qspec, qspec, qspec],
            out_specs=[whole, whole, whole, pl.BlockSpec((None, 1, S), lambda b, t, it, jt: (b, 0, 0))]),
        out_shape=[wide, wide, wide, jax.ShapeDtypeStruct((BH, 1, S), F32)],
        compiler_params=_cparams(("parallel", "arbitrary")),
    )(itab, jtab, qa, ka, va, do, ox, lse)


def seq_cumsum(x, n_seq, seq, *, reverse, name):
    T, C = x.shape
    tb = min(256, seq)
    n = seq // tb
    r = np.arange(tb)
    tri = (r[None, :] >= r[:, None]) if reverse else (r[None, :] <= r[:, None])
    tri = jnp.asarray(tri.astype(np.float32), dtype=BF16)

    def body(x_ref, tri_ref, o_ref, carry_s):
        @pl.when(pl.program_id(1) == 0)
        def _():
            carry_s[...] = jnp.zeros_like(carry_s)

        xv = x_ref[...]
        o_ref[...] = _dot3(tri_ref[...], xv) + carry_s[...]
        carry_s[...] += _colsum(xv)

    blk = (lambda s, t: (s * n + (n - 1 - t), 0)) if reverse else (lambda s, t: (s * n + t, 0))
    return pl.pallas_call(
        body, name=name,
        grid=(n_seq, n),
        in_specs=[pl.BlockSpec((tb, C), blk), pl.BlockSpec((tb, tb), lambda s, t: (0, 0))],
        out_specs=pl.BlockSpec((tb, C), blk),
        out_shape=jax.ShapeDtypeStruct((T, C), F32),
        scratch_shapes=[pltpu.VMEM((1, C), F32)],
        compiler_params=_cparams(("arbitrary", "arbitrary")),
    )(x, tri)


def _place():
    return lax.axis_index("x"), lax.axis_index("y"), lax.axis_index("c")


def _other_chips(x, y):
    return [(1 - x, y), (x, 1 - y), (1 - x, 1 - y)]


def _hbm_call(body, ins, out_shape, n_sems, *, name):
    hbm = pl.BlockSpec(memory_space=pl.ANY)
    return pl.pallas_call(
        body, name=name,
        in_specs=[hbm] * len(ins), out_specs=[hbm] * len(out_shape), out_shape=out_shape,
        scratch_shapes=[pltpu.SemaphoreType.DMA((n_sems,)), pltpu.SemaphoreType.DMA((n_sems,)),
                        pltpu.SemaphoreType.DMA((len(ins),))],
        compiler_params=pltpu.CompilerParams(has_side_effects=True),
    )(*ins)


def allgather_chips(shards, *, name):
    return _exchange_call(allgather_rider(shards), name=name)


def _allgather_ops(x_refs, o_refs, send_sems, recv_sems, local_sems):
    def copies():
        x, y, c = _place()
        me = 2 * x + y
        chips = _other_chips(x, y)
        own, first, passed, landed, handed = [], [], [], [], []
        for b, (x_ref, o_ref) in enumerate(zip(x_refs, o_refs)):
            half = x_ref.shape[0] // 2
            mine, theirs = pl.ds(c * half, half), pl.ds((1 - c) * half, half)
            own.append(pltpu.make_async_copy(x_ref, o_ref.at[me], local_sems.at[b]))

            def copy(k, src, chip, rows, to, o_ref=o_ref, b=b):
                return pltpu.make_async_remote_copy(src_ref=src, dst_ref=o_ref.at[2 * chip[0] + chip[1], rows],
                                                    send_sem=send_sems.at[6 * b + k], recv_sem=recv_sems.at[6 * b + k],
                                                    device_id=to, device_id_type=MESH)
            for j, chip in enumerate(chips):
                first.append(copy(j, x_ref.at[mine], (x, y), mine, (*chip, c)))
                landed.append(copy(j, x_ref.at[mine], chip, mine, (*chip, c)))
                passed.append(copy(3 + j, o_ref.at[2 * chip[0] + chip[1], mine], chip, mine, (x, y, 1 - c)))
                handed.append(copy(3 + j, x_ref.at[mine], chip, theirs, (x, y, 1 - c)))
        return own, first, passed, landed, handed

    def start():
        own, first, _, _, _ = copies()
        for cp in own + first:
            cp.start()

    def finish():
        own, first, passed, landed, handed = copies()
        for arrived, forward in zip(landed, passed):
            arrived.wait_recv()
            forward.start()
        for cp in handed:
            cp.wait_recv()
        for cp in first + passed:
            cp.wait_send()
        for cp in own:
            cp.wait()
    return start, finish


def _scatter_ops(x_refs, o_refs, send_sems, recv_sems, local_sems):
    def copies():
        x, y, c = _place()
        return [pltpu.make_async_remote_copy(
            src_ref=x_ref.at[2 * px + py], dst_ref=o_ref.at[j], send_sem=send_sems.at[3 * b + j],
            recv_sem=recv_sems.at[3 * b + j], device_id=(px, py, c), device_id_type=MESH)
            for b, (x_ref, o_ref) in enumerate(zip(x_refs, o_refs)) for j, (px, py) in enumerate(_other_chips(x, y))]

    def start():
        for cp in copies():
            cp.start()

    def finish():
        sends = copies()
        for cp in sends:
            cp.wait_recv()
        for cp in sends:
            cp.wait_send()
    return start, finish


class Rider(NamedTuple):
    ins: list
    out_shape: list
    n_sems: int
    ops: object

    def specs(self):
        hbm = pl.BlockSpec(memory_space=pl.ANY)
        sems = [pltpu.SemaphoreType.DMA((self.n_sems,)), pltpu.SemaphoreType.DMA((self.n_sems,)),
                pltpu.SemaphoreType.DMA((len(self.ins),))]
        return [hbm] * len(self.ins), [hbm] * len(self.out_shape), sems

    def wrap(self, body, n_in, n_out, grid_rank):
        k_in, k_out = len(self.ins), len(self.out_shape)

        def carried(*refs):
            ins, r_ins = refs[:n_in], refs[n_in:n_in + k_in]
            outs = refs[n_in + k_in:n_in + k_in + n_out]
            r_outs = refs[n_in + k_in + n_out:n_in + k_in + n_out + k_out]
            scratch, sems = refs[n_in + k_in + n_out + k_out:-3], refs[-3:]
            first = functools.reduce(jnp.logical_and, [pl.program_id(a) == 0 for a in range(grid_rank)])
            last = functools.reduce(jnp.logical_and,
                                    [pl.program_id(a) == pl.num_programs(a) - 1 for a in range(grid_rank)])
            pl.when(first)(lambda: self.ops(r_ins, r_outs, *sems)[0]())
            body(*ins, *outs, *scratch)
            pl.when(last)(lambda: self.ops(r_ins, r_outs, *sems)[1]())
        return carried


def _exchange_call(rider, *, name):
    def body(*refs):
        k = len(rider.ins)
        start, finish = rider.ops(refs[:k], refs[k:k + len(rider.out_shape)], *refs[-3:])
        start()
        finish()
    in_specs, out_specs, sems = rider.specs()
    return pl.pallas_call(body, name=name, in_specs=in_specs, out_specs=out_specs, out_shape=rider.out_shape,
                          scratch_shapes=sems, compiler_params=pltpu.CompilerParams(has_side_effects=True))(*rider.ins)


def allgather_rider(shards):
    assert all(s.shape[0] % (2 * ROW_ALIGN) == 0 for s in shards)
    return Rider(list(shards), [jax.ShapeDtypeStruct((4,) + s.shape, s.dtype) for s in shards], 6 * len(shards),
                 _allgather_ops)


def scatter_rider(parts):
    return Rider(list(parts), [jax.ShapeDtypeStruct((3,) + p.shape[1:], p.dtype) for p in parts], 3 * len(parts),
                 _scatter_ops)


def scatter_chips(parts, *, name):
    return _exchange_call(scatter_rider(parts), name=name)


def swap_cores(vs, *, name):
    nb = len(vs)

    def body(*refs):
        x_refs, o_refs = refs[:nb], refs[nb:2 * nb]
        send_sems, recv_sems, _ = refs[2 * nb:]
        x, y, c = _place()
        copies = [pltpu.make_async_remote_copy(src_ref=x_ref, dst_ref=o_ref, send_sem=send_sems.at[b],
                                               recv_sem=recv_sems.at[b], device_id=(x, y, 1 - c), device_id_type=MESH)
                  for b, (x_ref, o_ref) in enumerate(zip(x_refs, o_refs))]
        for cp in copies:
            cp.start()
        for cp in copies:
            cp.wait()

    return _hbm_call(body, vs, [jax.ShapeDtypeStruct(v.shape, v.dtype) for v in vs], nb, name=name)


def allreduce_small(v, *, name):
    R, C = v.shape

    def body(x_ref, o_ref, gath_ref, send_sems, recv_sems):
        x, y, c = _place()
        me = 4 * x + 2 * y + c
        gath_ref[me] = x_ref[...]
        flips = [(k >> 2 & 1, k >> 1 & 1, k & 1) for k in range(1, 8)]
        sends = []
        for j, (fx, fy, fc) in enumerate(flips):
            peer = (x ^ fx, y ^ fy, c ^ fc)
            cp = pltpu.make_async_remote_copy(src_ref=x_ref, dst_ref=gath_ref.at[me], send_sem=send_sems.at[j],
                                              recv_sem=recv_sems.at[j], device_id=peer, device_id_type=MESH)
            cp.start()
            sends.append(cp)
        for j, (fx, fy, fc) in enumerate(flips):
            peer = (x ^ fx, y ^ fy, c ^ fc)
            pltpu.make_async_remote_copy(src_ref=x_ref, dst_ref=gath_ref.at[4 * peer[0] + 2 * peer[1] + peer[2]],
                                         send_sem=send_sems.at[j], recv_sem=recv_sems.at[j], device_id=peer,
                                         device_id_type=MESH).wait_recv()
        for cp in sends:
            cp.wait_send()
        total = gath_ref[0]
        for d in range(1, 8):
            total = total + gath_ref[d]
        o_ref[...] = total

    vm = pl.BlockSpec(memory_space=pltpu.VMEM)
    out, _ = pl.pallas_call(
        body, name=name,
        in_specs=[vm], out_specs=[vm, vm],
        out_shape=[jax.ShapeDtypeStruct((R, C), F32), jax.ShapeDtypeStruct((8, R, C), F32)],
        scratch_shapes=[pltpu.SemaphoreType.DMA((7,)), pltpu.SemaphoreType.DMA((7,))],
        compiler_params=pltpu.CompilerParams(has_side_effects=True),
    )(v)
    return out


ROW_ALIGN = 16
PACK_W = 1024
SUM_TILE = 512
BIG_WEIGHTS = (("w_in", 1), ("w_a", 1), ("w_b", 1), ("w_o", 0), ("w_ff1", 1), ("w_ff2", 0), ("w_pg", 0), ("w_p", 1))


def _b_layout(d, ple):
    hw, q = d // 2, d // 4
    small = 2 * d + 2 * q
    lay = {"w_ff1": (0, 0, d, d), "w_ff2": (d, 0, d, d), "w_o": (2 * d, 0, q, d), "w_pg": (2 * d + q, 0, q, d),
           "w_a": (small, 0, hw, q), "w_b": (small, q, hw, q), "w_p": (small, 2 * q, ple, q)}
    return lay, small + hw


def pack_a(w_in_shard):
    rows, cols = w_in_shard.shape
    pad = -cols % LANES
    return jnp.concatenate([w_in_shard, jnp.zeros((rows, pad), w_in_shard.dtype)], axis=1)


def pack_b(shards, d):
    hw, q = d // 2, d // 4
    dt = shards["w_a"].dtype
    wp = shards["w_p"]
    wp = jnp.concatenate([wp, jnp.zeros((hw - wp.shape[0], q), dt)], axis=0)
    small = jnp.concatenate([shards["w_a"], shards["w_b"], wp, jnp.zeros((hw, d - 3 * q), dt)], axis=1)
    return jnp.concatenate([shards["w_ff1"], shards["w_ff2"], shards["w_o"], shards["w_pg"], small], axis=0)


def unpack_b(buf, lay):
    return {nm: buf[r0:r0 + rows, c0:c0 + cols] for nm, (r0, c0, rows, cols) in lay.items()}


def _win_layout(d):
    hw = d // 2
    fh = hw // FOX_HDIM
    orig = {"hq": (0, hw), "hf": (hw, hw), "hi": (2 * hw, hw), "hg": (3 * hw, hw), "fq": (4 * hw, hw),
            "fk": (5 * hw, hw), "fv": (6 * hw, hw), "ff": (7 * hw, fh), "ga": (7 * hw + fh, d), "gb": (7 * hw + fh + d, d)}
    order = ["ga", "gb", "hq", "hf", "hi", "hg", "fq", "fk", "fv", "ff"]
    mine, off = {}, 0
    for nm in order:
        width = orig[nm][1] if nm != "ff" else LANES
        mine[nm] = (off, width)
        off += width
    return orig, order, mine, off


def _adam_fn(rows, vecs):
    w, g, m, v = rows
    m2 = ADAM_B1 * m + (1.0 - ADAM_B1) * g
    v2 = ADAM_B2 * v + (1.0 - ADAM_B2) * (g * g)
    m_hat = m2 / (1.0 - ADAM_B1 ** ADAM_STEP)
    v_hat = v2 / (1.0 - ADAM_B2 ** ADAM_STEP)
    delta = -ADAM_LR * (m_hat / (jnp.sqrt(v_hat) + ADAM_EPS) + ADAM_WD * w)
    return [delta, m2, v2], []


def adamw_small(small, p0, ws, ms, vs, *, name):
    n = len(ws)
    hw = p0.shape[1]
    fh = ws[8].shape[1]

    def body(small_ref, p0_ref, *refs):
        w_refs, m_refs, v_refs = refs[:n], refs[n:2 * n], refs[2 * n:3 * n]
        g_out, d_out, m_out, v_out = (refs[(3 + k) * n:(4 + k) * n] for k in range(4))
        sm = small_ref[...]
        p = p0_ref[...]
        d_lb = sm[6:7, hw:2 * hw] * (p * (1.0 - p))
        grads = [sm[r:r + 1, :] for r in range(6)]
        grads += [jnp.concatenate([d_lb, -d_lb], axis=0), sm[6:7, :hw], sm[7:8, :fh]]
        for i in range(n):
            (delta, m2, v2), _ = _adam_fn([w_refs[i][...], grads[i], m_refs[i][...], v_refs[i][...]], [])
            g_out[i][...], d_out[i][...], m_out[i][...], v_out[i][...] = grads[i], delta, m2, v2

    shapes = [jax.ShapeDtypeStruct(w.shape, F32) for w in ws]
    return pl.pallas_call(body, name=name, out_shape=shapes * 4)(small, p0, *ws, *ms, *vs)


def adamw(w, g, m, v, *, name):
    c = w.shape[1]
    (delta, m2, v2), _ = rowwise(_adam_fn, [w, g, m, v], [], [(c, F32)] * 3, name=name, tm=256)
    return delta, m2, v2


def kernel(x, p, ln0_g, ln0_b, w_in, hg_lb, hg_norm_g, fox_fb, w_a, w_b, w_o, ln1_g, ln1_b, w_ff1, w_ff2, w_pg, w_p, ln2_g, ln2_b, loss_target, m_ln0_g, m_ln0_b, m_w_in, m_hg_lb, m_hg_norm_g, m_fox_fb, m_w_a, m_w_b, m_w_o, m_ln1_g, m_ln1_b, m_w_ff1, m_w_ff2, m_w_pg, m_w_p, m_ln2_g, m_ln2_b, v_ln0_g, v_ln0_b, v_w_in, v_hg_lb, v_hg_norm_g, v_fox_fb, v_w_a, v_w_b, v_w_o, v_ln1_g, v_ln1_b, v_w_ff1, v_w_ff2, v_w_pg, v_w_p, v_ln2_g, v_ln2_b):
    n_seq, seq, d = x.shape
    T = n_seq * seq
    hw = d // 2
    fh = hw // FOX_HDIM
    bh = n_seq * fh
    orig, order, mine, n_in = _win_layout(d)

    big = {"w_in": w_in[0], "w_a": w_a[0], "w_b": w_b[0], "w_o": w_o[0], "w_ff1": w_ff1[0], "w_ff2": w_ff2[0],
           "w_pg": w_pg[0], "w_p": w_p[0]}
    big_m = {"w_in": m_w_in[0], "w_a": m_w_a[0], "w_b": m_w_b[0], "w_o": m_w_o[0], "w_ff1": m_w_ff1[0],
             "w_ff2": m_w_ff2[0], "w_pg": m_w_pg[0], "w_p": m_w_p[0]}
    big_v = {"w_in": v_w_in[0], "w_a": v_w_a[0], "w_b": v_w_b[0], "w_o": v_w_o[0], "w_ff1": v_w_ff1[0],
             "w_ff2": v_w_ff2[0], "w_pg": v_w_pg[0], "w_p": v_w_p[0]}
    names = [nm for nm, _ in BIG_WEIGHTS]
    axis = dict(BIG_WEIGHTS)
    ple = w_p.shape[1]
    lay, b_rows = _b_layout(d, ple)
    in_cols = big["w_in"].shape[1]

    gather_w_in = allgather_rider([pack_a(big["w_in"].astype(BF16))])
    gather_rest = allgather_rider([pack_b({nm: big[nm].astype(BF16) for nm in names if nm != "w_in"}, d)])

    x2 = x.reshape(T, d)
    tgt = loss_target.reshape(T, d)
    p_b = p.reshape(T, p.shape[-1]).astype(BF16)
    vec = lambda a: a.reshape(1, -1)
    probs = jax.nn.softmax(hg_lb, axis=0)
    lb = vec(probs[0])

    def ln0_fn(rows, vecs):
        h = _ln_stats(rows[0]) * vecs[0] + vecs[1]
        return [h, h], []
    (h0, h0b), _, (a_all,) = rowwise(ln0_fn, [x2], [vec(ln0_g), vec(ln0_b)], [(d, F32), (d, BF16)], name="ln0_fwd",
                                     rider=gather_w_in)
    win = jnp.concatenate([a_all[s, :, :in_cols] for s in range(4)], axis=1)
    win_mine = jnp.concatenate(
        [win[:, orig[nm][0]:orig[nm][0] + orig[nm][1]] for nm in order]
        + [jnp.zeros((d, LANES - fh), BF16)], axis=1)
    proj = matmul_nn(h0b, win_mine, name="in_proj")

    o_raw, hg_states, (b_all,) = hgrn2_fwd(proj, [mine["hq"][0], mine["hf"][0], mine["hi"][0]], lb, n_seq, seq,
                                           name="hgrn2_fwd", rider=gather_rest)
    view = lambda nm, k, n: WView(b_all, lay[nm][0], lay[nm][1], k, n, axis[nm])
    w_ff1_v, w_ff2_v = view("w_ff1", d, 4 * d), view("w_ff2", 4 * d, d)

    def whole(nm):
        r0, c0, rows, cols = lay[nm]
        return jnp.concatenate([b_all[s, r0:r0 + rows, c0:c0 + cols] for s in range(4)], axis=axis[nm])
    w_o_v, w_pg_v, w_a_v, w_p_v, w_b_full = whole("w_o"), whole("w_pg"), whole("w_a"), whole("w_p"), whole("w_b")

    def ya_fn(rows, vecs):
        o, hg = rows
        outs = []
        for h in range(HG_HEADS):
            oh = o[:, h * HG_DIM:(h + 1) * HG_DIM]
            outs.append(oh * lax.rsqrt(jnp.mean(oh * oh, axis=-1, keepdims=True) + RMS_EPS))
        y = jnp.concatenate(outs, axis=1) * vecs[0] * (hg * _sigmoid(hg))
        return [y], []
    (y_a,), _ = rowwise(ya_fn, [o_raw, (proj,) + mine["hg"]], [hg_norm_g], [(hw, BF16)], name="hgrn2_out_fwd")

    fb_pad = jnp.concatenate([fox_fb, jnp.zeros((1, LANES - fh), F32)], axis=1)

    def lf_fn(rows, vecs):
        u = rows[0] + vecs[0]
        return [jnp.minimum(u, 0.0) - jnp.log(1.0 + jnp.exp(-jnp.abs(u)))], []
    (lf,), _ = rowwise(lf_fn, [(proj,) + mine["ff"]], [fb_pad], [(LANES, F32)], name="fox_logf")
    c_cum = seq_cumsum(lf, n_seq, seq, reverse=False, name="fox_cumsum")

    place = _fox_placement(fh)

    def prep_fn(rows, vecs):
        fq_, fk_, fv_, cc = rows
        pq, pk, aq, ak, oq, ok = vecs
        parts = jnp.concatenate(_split3(cc), axis=1)
        mm = lambda a_, b_: jnp.dot(a_, b_, preferred_element_type=F32)
        q_ = mm(fq_.astype(BF16), pq) + mm(parts, aq) + oq
        k_ = mm(fk_.astype(BF16), pk) + mm(parts, ak) + ok
        return [q_, k_, mm(fv_.astype(BF16), pk)], []
    wa = fh * FOX_AUG
    (qa, ka, va), _ = rowwise(prep_fn, [(proj,) + mine["fq"], (proj,) + mine["fk"], (proj,) + mine["fv"], c_cum],
                              [place[nm] for nm in ("pq", "pk", "aq", "ak", "oq", "ok")], [(wa, BF16)] * 3,
                              name="fox_prep")
    as_seq = lambda t2d: t2d.reshape(n_seq, seq, t2d.shape[1])
    o_fox, ox_fox, lse = fox_fwd(as_seq(qa), as_seq(ka), as_seq(va), name="fox_fwd")
    y_b = o_fox.reshape(T, wa)
    wb_pad = jnp.concatenate([w_b_full.reshape(fh, FOX_HDIM, d), jnp.zeros((fh, FOX_AUG - FOX_HDIM, d), BF16)],
                             axis=1).reshape(wa, d)

    pa = matmul_nn(y_a, w_a_v, name="proj_a")
    pb = matmul_nn(y_b, wb_pad, name="proj_b")

    def merge_fn(rows, vecs):
        ga, gb, a, b = rows
        return [_sigmoid(ga) * a + _sigmoid(gb) * b], []
    (merged,), _ = rowwise(merge_fn, [(proj,) + mine["ga"], (proj,) + mine["gb"], pa, pb], [], [(d, BF16)],
                           name="merge_fwd")
    fused_tm = 512

    def ln1_post(mix, aux, vecs):
        z = ALPHA * aux[0] + mix
        h = _ln_stats(z) * vecs[0] + vecs[1]
        return [z, h, h], []
    (z1, h1, h1b), _ = matmul_nn(merged, w_o_v, name="out_proj_ln1", tm=fused_tm, post=ln1_post, post_aux=[h0],
                                 post_vecs=[ln1_g, ln1_b], post_outs=[F32, F32, BF16])

    relu2 = lambda u: jnp.square(jnp.maximum(u, 0.0))
    act = matmul_nn(h1b, w_ff1_v, name="ff1", out_dtype=BF16, epilogue=relu2)
    pg = matmul_nn(h1b, w_pg_v, name="ple_gate")
    pe = matmul_nn(p_b, w_p_v, name="ple_embed")

    def head_post(ffv, aux, vecs):
        h1v, pgv, pev, t = aux
        g2, b2 = vecs
        sp = _sigmoid(pgv)
        z = ALPHA * h1v + ffv + sp * pev
        y = _ln_stats(z) * g2 + b2
        err = y - t
        loss_rows = 0.5 * jnp.mean(err * err, axis=-1, keepdims=True)
        dy = err * (1.0 / d)
        dz, dg2, db2 = _ln_bwd(z, dy, g2)
        loss_acc = jnp.broadcast_to(_colsum(loss_rows), (1, d))
        return [dz, dz, dz * pev * (sp * (1.0 - sp)), dz * sp], [dg2, db2, loss_acc]
    (dz2, dz2b, dpg, dpe), (g_ln2_g, g_ln2_b, loss_part) = matmul_nn(
        act, w_ff2_v, name="ff2_head", tm=fused_tm, post=head_post, post_aux=[h1, pg, pe, tgt],
        post_vecs=[ln2_g, ln2_b], post_outs=[F32, BF16, BF16, BF16], post_accs=[d, d, d])

    dact = lambda da, a: da * (2.0 * jnp.sqrt(a.astype(F32)))
    du = matmul_nn(dz2b, w_ff2_v, transpose_rhs=True, name="d_ff2", out_dtype=BF16, epilogue=dact, aux=act)
    dh1_pg = matmul_nn(dpg, w_pg_v, transpose_rhs=True, name="d_ple_gate")

    def ln1_bwd_post(dh1_ff, aux, vecs):
        dh1 = ALPHA * aux[0] + dh1_ff + aux[1]
        dz, dg, db = _ln_bwd(aux[2], dh1, vecs[0])
        return [dz, dz], [dg, db]
    (dz1, dz1b), (g_ln1_g, g_ln1_b) = matmul_nn(
        du, w_ff1_v, transpose_rhs=True, name="d_ff1_ln1", tm=fused_tm, post=ln1_bwd_post, post_aux=[dz2, dh1_pg, z1],
        post_vecs=[ln1_g], post_outs=[F32, BF16], post_accs=[d, d])

    def merge_bwd_post(dm, aux, vecs):
        ga, gb, a, b = aux
        sa, sb = _sigmoid(ga), _sigmoid(gb)
        return [dm * a * (sa * (1.0 - sa)), dm * b * (sb * (1.0 - sb)), dm * sa, dm * sb], []
    (dga, dgb, dma, dmb), _ = matmul_nn(
        dz1b, w_o_v, transpose_rhs=True, name="d_out_proj_merge", tm=fused_tm, post=merge_bwd_post,
        post_aux=[(proj,) + mine["ga"], (proj,) + mine["gb"], pa, pb], post_outs=[BF16] * 4)
    dya = matmul_nn(dma, w_a_v, transpose_rhs=True, name="d_proj_a")
    dyb = matmul_nn(dmb, wb_pad, transpose_rhs=True, name="d_proj_b", out_dtype=BF16)

    def ya_bwd_fn(rows, vecs):
        o, hg, dy = rows
        ng = vecs[0]
        sg = _sigmoid(hg)
        gate = hg * sg
        dn_parts, do_parts, n_parts = [], [], []
        for h in range(HG_HEADS):
            hs = slice(h * HG_DIM, (h + 1) * HG_DIM)
            oh = o[:, hs]
            r = lax.rsqrt(jnp.mean(oh * oh, axis=-1, keepdims=True) + RMS_EPS)
            nh = oh * r
            dn = dy[:, hs] * ng[:, hs] * gate[:, hs]
            do_parts.append(r * (dn - nh * jnp.mean(dn * nh, axis=-1, keepdims=True)))
            n_parts.append(nh)
        nrm = jnp.concatenate(n_parts, axis=1)
        dhg = dy * nrm * ng * (sg * (1.0 + hg * (1.0 - sg)))
        return [jnp.concatenate(do_parts, axis=1), dhg], [_colsum(dy * nrm * gate)]
    (do_raw, dhg), (g_norm_g,) = rowwise(ya_bwd_fn, [o_raw, (proj,) + mine["hg"], dya], [hg_norm_g],
                                         [(hw, F32), (hw, BF16)], [hw], name="hgrn2_out_bwd")
    dhq, dhf, dhi, g_lb = hgrn2_bwd(proj, [mine["hq"][0], mine["hf"][0], mine["hi"][0]], lb, do_raw, hg_states,
                                    n_seq, seq, name="hgrn2_bwd")

    do_fox = as_seq(dyb)
    dqa, dka, dva, dsum = fox_bwd(as_seq(qa), as_seq(ka), as_seq(va), do_fox, ox_fox, lse, name="fox_bwd")

    def unprep_fn(rows, vecs):
        mm = lambda a_, b_: jnp.dot(a_.astype(BF16), b_, preferred_element_type=F32)
        return [mm(rows[0], vecs[0]), mm(rows[1], vecs[1]), mm(rows[2], vecs[1])], []
    (dfq, dfk, dfv), _ = rowwise(unprep_fn, [dqa.reshape(T, wa), dka.reshape(T, wa), dva.reshape(T, wa)],
                                 [place["pqt"], place["pkt"]], [(hw, BF16)] * 3, name="fox_unprep")
    dc = -dsum.reshape(n_seq, fh, seq).transpose(0, 2, 1).reshape(T, fh)
    dc = jnp.concatenate([dc, jnp.zeros((T, LANES - fh), F32)], axis=1)
    dlf = seq_cumsum(dc, n_seq, seq, reverse=True, name="fox_cumsum_bwd")

    def lf_bwd_fn(rows, vecs):
        u = rows[0] + vecs[0]
        du_ = rows[1] * _sigmoid(-u)
        return [du_], [_colsum(du_)]
    (dff_,), (g_fb,) = rowwise(lf_bwd_fn, [(proj,) + mine["ff"], dlf], [fb_pad], [(LANES, BF16)], [LANES],
                               name="fox_logf_bwd")

    dproj = jnp.concatenate([dga, dgb, dhq, dhf, dhi, dhg, dfq, dfk, dfv, dff_], axis=1)

    gfull = {
        "w_a": matmul_tn(y_a, dma, name="g_w_a"),
        "w_b": matmul_tn(y_b, dmb, name="g_w_b").reshape(fh, FOX_AUG, d)[:, :FOX_HDIM].reshape(hw, d),
        "w_o": matmul_tn(merged, dz1b, name="g_w_o"),
        "w_ff1": matmul_tn(h1b, du, name="g_w_ff1"),
        "w_ff2": matmul_tn(act, dz2b, name="g_w_ff2"),
        "w_pg": matmul_tn(h1b, dpg, name="g_w_pg"),
        "w_p": matmul_tn(p_b, dpe, name="g_w_p"),
    }

    def chip_parts(nm, s):
        g = gfull[nm]
        n = g.shape[axis[nm]] // 4
        return lax.slice_in_dim(g, s * n, (s + 1) * n, axis=axis[nm])
    me = 2 * lax.axis_index("x") + lax.axis_index("y")
    core = lax.axis_index("c")

    def sum2_fn(rows, vecs):
        s = rows[0] + rows[1].astype(F32)
        return [s, s], []

    def sum4_fn(rows, vecs):
        a, r0, r1, r2 = rows
        return [((a + r0.astype(F32)) + r1.astype(F32)) + r2.astype(F32)], []

    def chip_pair_sum(g, tag):
        h, cols = g.shape[1] // 2, g.shape[2]
        keep = lax.dynamic_slice_in_dim(g, core * h, h, axis=1)
        give = lax.dynamic_slice_in_dim(g, (1 - core) * h, h, axis=1).astype(BF16)
        (from_core,) = swap_cores([give], name="swap_partials_" + tag)
        (s32, s16), _ = rowwise(sum2_fn, [keep.reshape(4 * h, cols), from_core.reshape(4 * h, cols)], [],
                                [(cols, F32), (cols, BF16)], name="sum_cores_" + tag, tm=SUM_TILE)
        return s32.reshape(4, h, cols), s16.reshape(4, h, cols)

    def chip_sum(pr, gt, tag):
        own = lax.dynamic_index_in_dim(pr, me, axis=0, keepdims=False)
        (q,), _ = rowwise(sum4_fn, [own, gt[0], gt[1], gt[2]], [], [(own.shape[1], F32)], name="sum_chips_" + tag,
                          tm=SUM_TILE)
        return q

    grads_b = jnp.stack([pack_b({nm: chip_parts(nm, s) for nm in names if nm != "w_in"}, d) for s in range(4)])
    pair_rest, pair_rest_b = chip_pair_sum(grads_b, "rest")
    gw_in_mine, (got_rest,) = matmul_tn(h0b, dproj, name="g_w_in", rider=scatter_rider([pair_rest_b]))
    gfull["w_in"] = jnp.concatenate([gw_in_mine[:, mine[nm][0]:mine[nm][0] + orig[nm][1]]
                                     for nm in ["hq", "hf", "hi", "hg", "fq", "fk", "fv", "ff", "ga", "gb"]], axis=1)
    grads_a = jnp.stack([pack_a(chip_parts("w_in", s)) for s in range(4)])
    pair_in, pair_in_b = chip_pair_sum(grads_a, "w_in")
    def ln0_bwd_post(dh0_in, aux, vecs):
        dx, dg, db = _ln_bwd(aux[1], dh0_in + ALPHA * aux[0], vecs[0])
        return [dx], [dg, db]
    ((dx,), (g_ln0_g, g_ln0_b)), (got_in,) = matmul_nn(
        dproj, win_mine, transpose_rhs=True, name="d_in_proj_ln0", tm=fused_tm, post=ln0_bwd_post,
        post_aux=[dz1, x2], post_vecs=[vec(ln0_g)], post_outs=[F32], post_accs=[d, d],
        rider=scatter_rider([pair_in_b]))
    q_half = [chip_sum(pair_in, got_in, "w_in"), chip_sum(pair_rest, got_rest, "rest")]
    q_other = swap_cores(q_half, name="swap_halves")
    g_a, g_b = [jnp.concatenate([jnp.where(core == 0, mine_, other), jnp.where(core == 0, other, mine_)], axis=0)
                for mine_, other in zip(q_half, q_other)]
    g_shards = unpack_b(g_b, lay)
    g_shards["w_in"] = g_a[:, :in_cols]

    assert d == PACK_W and 2 * hw == PACK_W and fh <= LANES
    small = allreduce_small(jnp.concatenate(
        [g_ln0_g, g_ln0_b, g_ln1_g, g_ln1_b, g_ln2_g, g_ln2_b, jnp.concatenate([g_norm_g, g_lb], axis=1),
         jnp.concatenate([g_fb, loss_part[:, LANES:]], axis=1)], axis=0), name="allreduce_small")
    loss = small[7, LANES]

    small_w = [vec(ln0_g), vec(ln0_b), ln1_g, ln1_b, ln2_g, ln2_b, hg_lb, hg_norm_g, fox_fb]
    small_m = [vec(m_ln0_g), vec(m_ln0_b), m_ln1_g, m_ln1_b, m_ln2_g, m_ln2_b, m_hg_lb, m_hg_norm_g, m_fox_fb]
    small_v = [vec(v_ln0_g), vec(v_ln0_b), v_ln1_g, v_ln1_b, v_ln2_g, v_ln2_b, v_hg_lb, v_hg_norm_g, v_fox_fb]
    small_out = adamw_small(small, probs[0:1], small_w, small_m, small_v, name="adamw_small")
    small_shapes = [ln0_g.shape, ln0_b.shape, ln1_g.shape, ln1_b.shape, ln2_g.shape, ln2_b.shape, hg_lb.shape,
                    hg_norm_g.shape, fox_fb.shape]
    sg_out, sd_out, sm_out, sv_out = [[a.reshape(shp) for a, shp in zip(small_out[9 * k:9 * k + 9], small_shapes)]
                                      for k in range(4)]

    big_out = {}
    for nm in names:
        delta, m2, v2 = adamw(big[nm], g_shards[nm], big_m[nm], big_v[nm], name="adamw_" + nm)
        big_out[nm] = (g_shards[nm][None], delta[None], m2[None], v2[None])

    def ordered(k):
        sm_ = [sg_out, sd_out, sm_out, sv_out][k]
        bg = lambda nm: big_out[nm][k]
        return [sm_[0], sm_[1], bg("w_in"), sm_[6], sm_[7], sm_[8], bg("w_a"), bg("w_b"), bg("w_o"), sm_[2], sm_[3],
                bg("w_ff1"), bg("w_ff2"), bg("w_pg"), bg("w_p"), sm_[4], sm_[5]]
    grad_x = dx.reshape(n_seq, seq, d)
    return (loss, grad_x, *ordered(0), *ordered(1), *ordered(2), *ordered(3))
```

```python
import functools
from typing import NamedTuple, Optional

import numpy as np
import jax
import jax.numpy as jnp
from jax import lax
from jax.experimental import pallas as pl
from jax.experimental.pallas import tpu as pltpu

F32 = jnp.float32
BF16 = jnp.bfloat16
MESH = pl.DeviceIdType.MESH

VMEM_LIMIT_BYTES = 48 * 1024 * 1024
LANES = 128
HG_HEADS = 4
HG_DIM = 128
HG_BLK = 16
HG_TILE = 256
HG_SLOTS = 4
FOX_HDIM = 64
FOX_AUG = 128
FOX_TQ = 1024
FOX_FWD_HEADS = 1
LN_EPS = 1e-5
RMS_EPS = 1e-6
DEPTH = 1
ALPHA = (2.0 * DEPTH) ** 0.25
ADAM_LR, ADAM_B1, ADAM_B2, ADAM_EPS, ADAM_WD, ADAM_STEP = 0.001, 0.9, 0.999, 1e-08, 0.01, 10
NEG_INF = -1e30


def _cparams(sem):
    return pltpu.CompilerParams(dimension_semantics=sem, vmem_limit_bytes=VMEM_LIMIT_BYTES)


def _tile(n, cap):
    if n <= cap:
        return n
    best = None
    for t in range(LANES, cap + 1, LANES):
        if n % t == 0:
            best = t
    assert best is not None, (n, cap)
    return best


class WView(NamedTuple):
    arr: jax.Array
    r0: int
    c0: int
    k: int
    n: int
    split: Optional[int]


def matmul_nn(a, w, *, name, transpose_rhs=False, out_dtype=F32, epilogue=None, aux=None, tm=1024, rider=None,
              post=None, post_aux=(), post_vecs=(), post_outs=(), post_accs=()):
    wv = w if isinstance(w, WView) else WView(w[None], 0, 0, w.shape[0], w.shape[1], None)
    rows_s = wv.k // 4 if wv.split == 0 else wv.k
    cols_s = wv.n // 4 if wv.split == 1 else wv.n
    tr, tc = _tile(rows_s, 1152), _tile(cols_s, 1152)
    assert wv.r0 % tr == 0 and wv.c0 % tc == 0
    T, K = a.shape
    N, tn, tk = (wv.k, tr, tc) if transpose_rhs else (wv.n, tc, tr)
    assert K == (wv.n if transpose_rhs else wv.k)
    tm = min(tm, T)
    assert T % tm == 0
    nk = K // tk

    def w_block(ri, ci):
        if wv.split == 0:
            return (ri * tr) // rows_s, (wv.r0 + (ri * tr) % rows_s) // tr, wv.c0 // tc + ci
        if wv.split == 1:
            return (ci * tc) // cols_s, wv.r0 // tr + ri, (wv.c0 + (ci * tc) % cols_s) // tc
        return 0, wv.r0 // tr + ri, wv.c0 // tc + ci

    fused = post is not None
    assert not fused or N == tn
    aux_list = list(post_aux) if fused else ([aux] if aux is not None else [])
    aux_list = [x if isinstance(x, tuple) else (x, 0, x.shape[1]) for x in aux_list]
    vec_list = list(post_vecs)
    out_dtypes = list(post_outs) if fused else [out_dtype]
    n_aux, n_vec, n_out, n_acc = len(aux_list), len(vec_list), len(out_dtypes), len(post_accs)

    def body(*refs):
        a_ref, w_ref = refs[:2]
        aux_refs = refs[2:2 + n_aux]
        vec_refs = refs[2 + n_aux:2 + n_aux + n_vec]
        out_refs = refs[2 + n_aux + n_vec:2 + n_aux + n_vec + n_out]
        sum_refs = refs[2 + n_aux + n_vec + n_out:2 + n_aux + n_vec + n_out + n_acc]
        acc_ref = refs[-1]
        m, k = pl.program_id(1), pl.program_id(2)
        if transpose_rhs:
            part = lax.dot_general(a_ref[...], w_ref[...], (((1,), (1,)), ((), ())), preferred_element_type=F32)
        else:
            part = jnp.dot(a_ref[...], w_ref[...], preferred_element_type=F32)

        def write(res):
            if not fused:
                if epilogue is not None:
                    res = epilogue(res) if not aux_refs else epilogue(res, aux_refs[0][...])
                out_refs[0][...] = res.astype(out_dtype)
                return
            outs, sums = post(res, [r[...] for r in aux_refs], [v[...] for v in vec_refs])
            assert len(outs) == n_out and len(sums) == n_acc
            for r, val in zip(out_refs, outs):
                r[...] = val.astype(r.dtype)
            for r, val in zip(sum_refs, sums):
                def first_rows(r=r, val=val):
                    r[...] = val

                def later_rows(r=r, val=val):
                    r[...] += val
                pl.when(m == 0)(first_rows)
                pl.when(m > 0)(later_rows)

        if nk == 1:
            write(part)
        else:
            @pl.when(k == 0)
            def _():
                acc_ref[...] = part

            @pl.when(k > 0)
            def _():
                acc_ref[...] += part

            @pl.when(k == nk - 1)
            def _():
                write(acc_ref[...])

    w_index = (lambda n, m, k: w_block(n, k)) if transpose_rhs else (lambda n, m, k: w_block(k, n))
    in_specs = [pl.BlockSpec((tm, tk), lambda n, m, k: (m, k)),
                pl.BlockSpec((None, tr, tc), w_index)]
    args = [a, wv.arr]
    for arr, off, width in aux_list:
        assert width == N and off % tn == 0
        in_specs.append(pl.BlockSpec((tm, tn), functools.partial(lambda n, m, k, blk: (m, blk + n), blk=off // tn)))
        args.append(arr)
    for v in vec_list:
        in_specs.append(pl.BlockSpec(v.shape, lambda n, m, k: (0, 0)))
        args.append(v)
    out_specs = [pl.BlockSpec((tm, tn), lambda n, m, k: (m, n)) for _ in out_dtypes]
    out_specs += [pl.BlockSpec((1, tn), lambda n, m, k: (0, 0)) for _ in post_accs]
    out_shape = [jax.ShapeDtypeStruct((T, N), dt) for dt in out_dtypes]
    out_shape += [jax.ShapeDtypeStruct((1, N), F32) for _ in post_accs]
    scratch = [pltpu.VMEM((tm, tn) if nk > 1 else (8, LANES), F32)]
    grid = (N // tn, T // tm, nk)
    sem = ("arbitrary",) * 3 if (n_acc or rider is not None) else ("parallel", "parallel", "arbitrary")
    params = pltpu.CompilerParams(dimension_semantics=sem, vmem_limit_bytes=VMEM_LIMIT_BYTES,
                                  has_side_effects=rider is not None)
    if rider is not None:
        r_in, r_out, r_sems = rider.specs()
        body = rider.wrap(body, len(in_specs), len(out_specs), 3)
        in_specs, out_specs, out_shape = in_specs + r_in, out_specs + r_out, out_shape + rider.out_shape
        scratch, args = scratch + r_sems, args + list(rider.ins)
    res = pl.pallas_call(body, name=name, grid=grid, in_specs=in_specs, out_specs=out_specs, out_shape=out_shape,
                         scratch_shapes=scratch, compiler_params=params)(*args)
    main = (list(res[:n_out]), list(res[n_out:n_out + n_acc])) if fused else res[0]
    return main if rider is None else (main, list(res[n_out + n_acc:]))


def matmul_tn(a, b, *, name, tk=1024, rider=None, into=None):
    T, M = a.shape
    T2, N = b.shape
    tk = min(tk, T)
    assert T == T2 and T % tk == 0
    if into is not None:
        assert rider is None
        buf, r0, c0, split = into
        rows_s, cols_s = (M // 4, N) if split == 0 else (M, N // 4)
        tm, tn = _tile(rows_s, 1024), _tile(cols_s, 1152)
        assert r0 % tm == 0 and c0 % tn == 0

        def part_block(m, n, k):
            if split == 0:
                return (m * tm) // rows_s, (r0 + (m * tm) % rows_s) // tm, c0 // tn + n
            return (n * tn) // cols_s, r0 // tm + m, (c0 + (n * tn) % cols_s) // tn

        def body_into(a_ref, b_ref, buf_ref, o_ref):
            k = pl.program_id(2)
            part = lax.dot_general(a_ref[...], b_ref[...], (((0,), (0,)), ((), ())), preferred_element_type=F32)

            @pl.when(k == 0)
            def _():
                o_ref[...] = part

            @pl.when(k > 0)
            def _():
                o_ref[...] += part

        return pl.pallas_call(
            body_into, name=name, grid=(M // tm, N // tn, T // tk),
            in_specs=[pl.BlockSpec((tk, tm), lambda m, n, k: (k, m)), pl.BlockSpec((tk, tn), lambda m, n, k: (k, n)),
                      pl.BlockSpec(memory_space=pl.ANY)],
            out_specs=pl.BlockSpec((None, tm, tn), part_block),
            out_shape=jax.ShapeDtypeStruct(buf.shape, buf.dtype), input_output_aliases={2: 0},
            compiler_params=_cparams(("parallel", "parallel", "arbitrary")))(a, b, buf)
    tm = _tile(M, 1024)
    tn = _tile(N, 1152)

    def body(a_ref, b_ref, o_ref):
        k = pl.program_id(2)
        part = lax.dot_general(a_ref[...], b_ref[...], (((0,), (0,)), ((), ())), preferred_element_type=F32)

        @pl.when(k == 0)
        def _():
            o_ref[...] = part

        @pl.when(k > 0)
        def _():
            o_ref[...] += part

    in_specs = [pl.BlockSpec((tk, tm), lambda m, n, k: (k, m)), pl.BlockSpec((tk, tn), lambda m, n, k: (k, n))]
    out_specs = [pl.BlockSpec((tm, tn), lambda m, n, k: (m, n))]
    out_shape = [jax.ShapeDtypeStruct((M, N), F32)]
    grid = (M // tm, N // tn, T // tk)
    if rider is None:
        return pl.pallas_call(body, name=name, grid=grid, in_specs=in_specs, out_specs=out_specs, out_shape=out_shape,
                              compiler_params=_cparams(("parallel", "parallel", "arbitrary")))(a, b)[0]
    r_in, r_out, r_sems = rider.specs()
    res = pl.pallas_call(
        rider.wrap(body, 2, 1, 3), name=name, grid=grid, in_specs=in_specs + r_in, out_specs=out_specs + r_out,
        out_shape=out_shape + rider.out_shape, scratch_shapes=r_sems,
        compiler_params=pltpu.CompilerParams(dimension_semantics=("arbitrary",) * 3,
                                             vmem_limit_bytes=VMEM_LIMIT_BYTES, has_side_effects=True),
    )(a, b, *rider.ins)
    return res[0], list(res[1:])


def rowwise(fn, rows, vecs, outs, accs=(), *, name, tm=512, rider=None):
    rows = [r if isinstance(r, tuple) else (r, 0, r.shape[1]) for r in rows]
    T = rows[0][0].shape[0]
    tm = min(tm, T)
    assert T % tm == 0
    n_rows, n_vecs, n_outs, n_accs = len(rows), len(vecs), len(outs), len(accs)

    def body(*refs):
        row_refs = refs[:n_rows]
        vec_refs = refs[n_rows:n_rows + n_vecs]
        out_refs = refs[n_rows + n_vecs:n_rows + n_vecs + n_outs]
        acc_refs = refs[n_rows + n_vecs + n_outs:]
        out_vals, acc_vals = fn([r[...] for r in row_refs], [v[...] for v in vec_refs])
        assert len(out_vals) == n_outs and len(acc_vals) == n_accs
        for r, val in zip(out_refs, out_vals):
            r[...] = val.astype(r.dtype)
        if n_accs:
            i = pl.program_id(0)

            @pl.when(i == 0)
            def _():
                for r in acc_refs:
                    r[...] = jnp.zeros_like(r)

            for r, val in zip(acc_refs, acc_vals):
                r[...] += val

    in_specs = []
    for arr, off, width in rows:
        assert off % width == 0
        in_specs.append(pl.BlockSpec((tm, width), functools.partial(lambda i, blk: (i, blk), blk=off // width)))
    for v in vecs:
        in_specs.append(pl.BlockSpec(v.shape, lambda i: (0, 0)))
    out_specs = [pl.BlockSpec((tm, w), lambda i: (i, 0)) for w, _ in outs]
    out_specs += [pl.BlockSpec((1, w), lambda i: (0, 0)) for w in accs]
    out_shape = [jax.ShapeDtypeStruct((T, w), dt) for w, dt in outs]
    out_shape += [jax.ShapeDtypeStruct((1, w), F32) for w in accs]
    args = [r[0] for r in rows] + list(vecs)
    if rider is None:
        res = pl.pallas_call(body, name=name, grid=(T // tm,), in_specs=in_specs, out_specs=out_specs,
                             out_shape=out_shape,
                             compiler_params=_cparams(("arbitrary",) if n_accs else ("parallel",)))(*args)
        return res[:n_outs], res[n_outs:]
    r_in, r_out, r_sems = rider.specs()
    res = pl.pallas_call(
        rider.wrap(body, len(in_specs), len(out_specs), 1), name=name, grid=(T // tm,), in_specs=in_specs + r_in,
        out_specs=out_specs + r_out, out_shape=out_shape + rider.out_shape, scratch_shapes=r_sems,
        compiler_params=pltpu.CompilerParams(dimension_semantics=("arbitrary",), vmem_limit_bytes=VMEM_LIMIT_BYTES,
                                             has_side_effects=True),
    )(*args, *rider.ins)
    return res[:n_outs], res[n_outs:n_outs + n_accs], list(res[n_outs + n_accs:])


def _colsum(x):
    return jnp.sum(x, axis=0, keepdims=True)


def _sigmoid(x):
    return 1.0 / (1.0 + jnp.exp(-x))


def _ln_stats(z):
    mu = jnp.mean(z, axis=-1, keepdims=True)
    zc = z - mu
    var = jnp.mean(zc * zc, axis=-1, keepdims=True)
    return zc * lax.rsqrt(var + LN_EPS)


def _ln_bwd(zhat_src, dy, g):
    mu = jnp.mean(zhat_src, axis=-1, keepdims=True)
    zc = zhat_src - mu
    var = jnp.mean(zc * zc, axis=-1, keepdims=True)
    rstd = lax.rsqrt(var + LN_EPS)
    zh = zc * rstd
    dzh = dy * g
    dz = rstd * (dzh - jnp.mean(dzh, axis=-1, keepdims=True) - zh * jnp.mean(dzh * zh, axis=-1, keepdims=True))
    return dz, _colsum(dy * zh), _colsum(dy)


def _hg_constants():
    r = np.arange(HG_TILE)
    same = (r[:, None] // HG_BLK) == (r[None, :] // HG_BLK)
    lower = (same & (r[None, :] <= r[:, None])).astype(np.float32)
    upper = (same & (r[None, :] >= r[:, None])).astype(np.float32)
    total = same.astype(np.float32)
    c = np.arange(2 * HG_DIM)
    bd = ((c[:, None] // HG_DIM) == (c[None, :] // HG_DIM)).astype(np.float32)
    pair_t = np.array([t for t, _ in _HG_PAIRS])
    pair_s = np.array([s for _, s in _HG_PAIRS])
    sel_t = (pair_t[None, :] == np.arange(HG_BLK)[:, None]).astype(np.float32)
    sel_s = (pair_s[None, :] == np.arange(HG_BLK)[:, None]).astype(np.float32)
    as_bf = lambda m: jnp.asarray(m, dtype=BF16)
    return as_bf(lower), as_bf(upper), as_bf(total), as_bf(bd), as_bf(sel_t), as_bf(sel_s)


_HG_HALF = HG_BLK // 2
_HG_PAIRS = ([(t, s) for t in range(_HG_HALF, HG_BLK) for s in range(HG_BLK)]
             + [(t, s) for t in range(_HG_HALF) for s in range(_HG_HALF)])
HG_STACK = len(_HG_PAIRS)
_HG_SLABS = ([((t - _HG_HALF) * HG_BLK, (t,), HG_BLK) for t in range(_HG_HALF, HG_BLK)]
             + [(_HG_HALF * HG_BLK + t * _HG_HALF, (t, t + 1), _HG_HALF) for t in range(0, _HG_HALF, 2)])


def _stack_by_s(x):
    return jnp.concatenate([x] * _HG_HALF + [x[:_HG_HALF]] * _HG_HALF, axis=0)


def _stack_by_t(x):
    w = x.shape[1]
    return jnp.concatenate([jnp.broadcast_to(x[t:t + 1], (HG_BLK, w)) for t in range(_HG_HALF, HG_BLK)]
                           + [jnp.broadcast_to(x[t:t + 1], (_HG_HALF, w)) for t in range(_HG_HALF)], axis=0)


def _keep_bf16_bits(x):
    bits = lax.bitcast_convert_type(x, jnp.int32) & jnp.int32(-65536)
    return lax.bitcast_convert_type(bits, F32)


def _head_sums(stack_ref, slot, bd):
    pair = bd.shape[0]
    return jnp.concatenate([jnp.dot(stack_ref[slot, :, c0:c0 + pair], bd, preferred_element_type=F32)
                            for c0 in range(0, stack_ref.shape[2], pair)], axis=1)


def _split3(x):
    hi = _keep_bf16_bits(x)
    r1 = x - hi
    mid = _keep_bf16_bits(r1)
    lo = _keep_bf16_bits(r1 - mid)
    return hi.astype(BF16), mid.astype(BF16), lo.astype(BF16)


def _dot3(m01, x):
    hi, mid, lo = _split3(x)
    d = lambda p: jnp.dot(m01, p, preferred_element_type=F32)
    return (d(lo) + d(mid)) + d(hi)


def _hg_prologue(hq, hf, lb, lower, total):
    sq = _sigmoid(hq)
    q = hq * sq
    sg = _sigmoid(hf)
    f = lb + (1.0 - lb) * sg
    g = jnp.log(f)
    k = 1.0 - f
    b = _dot3(lower, g)
    bl = _dot3(total, g)
    return q, k, f, sg, sq, b, bl


def _stack16(fn):
    return [fn(t) for t in range(HG_BLK)]


def hgrn2_fwd(proj, offs, lb, n_seq, seq, *, name, rider=None):
    T = n_seq * seq
    W = HG_HEADS * HG_DIM
    n_tiles = seq // HG_TILE
    nb = HG_TILE // HG_BLK
    lower, _, total, bd, sel_t, _ = _hg_constants()

    def body(hq_ref, hf_ref, hi_ref, lb_ref, lower_ref, total_ref, bd_ref, selt_ref,
             o_ref, st_out_ref,
             st_ref, q_s, k_s, v_s, b_s, qt_s, kt_s, d_s, p_s):
        @pl.when(pl.program_id(1) == 0)
        def _():
            st_ref[...] = jnp.zeros_like(st_ref)

        q, k, _, _, _, b, bl = _hg_prologue(hq_ref[...], hf_ref[...], lb_ref[...], lower_ref[...], total_ref[...])
        q_s[...] = q
        k_s[...] = k
        v_s[...] = hi_ref[...]
        b_s[...] = b
        qt_s[...] = q * jnp.exp(b)
        kt_s[...] = k * jnp.exp(jnp.minimum(bl - b, 0.0))
        d_s[...] = jnp.exp(bl)
        rowi = lax.broadcasted_iota(jnp.int32, (HG_BLK, W), 0)

        def block(i, slot):
            r0 = pl.multiple_of(i * HG_BLK, HG_BLK)
            rows = pl.ds(r0, HG_BLK)
            qi, ki, vi, bi = q_s[rows, :], k_s[rows, :], v_s[rows, :], b_s[rows, :]
            for off, ts, n in _HG_SLABS:
                slab = [jnp.where(rowi[:n] <= t, jnp.exp(jnp.minimum(bi[t:t + 1, :] - bi[:n], 0.0)), 0.0)
                        * qi[t:t + 1, :] * ki[:n] for t in ts]
                p_s[slot, pl.ds(off, HG_BLK), :] = jnp.concatenate(slab, axis=0).astype(BF16)
            a_b = _head_sums(p_s, slot, bd_ref[...])
            o_blk = jnp.dot(selt_ref[...], (a_b * _stack_by_s(vi)).astype(BF16), preferred_element_type=F32)
            qti, kti, di = qt_s[rows, :], kt_s[rows, :], d_s[rows, :]
            outs = []
            for h in range(HG_HEADS):
                hs = slice(h * HG_DIM, (h + 1) * HG_DIM)
                st_h = st_ref[hs, :]
                st_out_ref[i, hs, :] = st_h
                outs.append(lax.dot_general(qti[:, hs].astype(BF16), st_h.astype(BF16),
                                            (((1,), (1,)), ((), ())), preferred_element_type=F32))
                upd = lax.dot_general(vi[:, hs].astype(BF16), kti[:, hs].astype(BF16),
                                      (((0,), (0,)), ((), ())), preferred_element_type=F32)
                st_ref[hs, :] = st_h * di[0:1, hs] + upd
            o_ref[rows, :] = o_blk + jnp.concatenate(outs, axis=1)

        def some_blocks(jj, carry):
            for slot in range(HG_SLOTS):
                block(HG_SLOTS * jj + slot, slot)
            return carry

        lax.fori_loop(0, nb // HG_SLOTS, some_blocks, 0)

    col = lambda off: functools.partial(lambda s, t, blk: (s * n_tiles + t, blk), blk=off // W)
    const = lambda m: pl.BlockSpec(m.shape, lambda s, t: (0, 0))
    tile_f32 = pltpu.VMEM((HG_TILE, W), F32)
    in_specs = [pl.BlockSpec((HG_TILE, W), col(offs[0])), pl.BlockSpec((HG_TILE, W), col(offs[1])),
                pl.BlockSpec((HG_TILE, W), col(offs[2])), const(lb), const(lower), const(total), const(bd),
                const(sel_t)]
    out_specs = [pl.BlockSpec((HG_TILE, W), lambda s, t: (s * n_tiles + t, 0)),
                 pl.BlockSpec((nb, W, HG_DIM), lambda s, t: (s * n_tiles + t, 0, 0))]
    out_shape = [jax.ShapeDtypeStruct((T, W), F32), jax.ShapeDtypeStruct((T // HG_BLK, W, HG_DIM), F32)]
    scratch = [pltpu.VMEM((W, HG_DIM), F32)] + [tile_f32] * 7 + [pltpu.VMEM((HG_SLOTS, HG_STACK, W), BF16)]
    args = [proj, proj, proj, lb, lower, total, bd, sel_t]
    params = _cparams(("arbitrary", "arbitrary"))
    if rider is not None:
        r_in, r_out, r_sems = rider.specs()
        body = rider.wrap(body, len(in_specs), len(out_specs), 2)
        in_specs, out_specs, out_shape = in_specs + r_in, out_specs + r_out, out_shape + rider.out_shape
        scratch, args = scratch + r_sems, args + rider.ins
        params = pltpu.CompilerParams(dimension_semantics=("arbitrary", "arbitrary"),
                                      vmem_limit_bytes=VMEM_LIMIT_BYTES, has_side_effects=True)
    res = pl.pallas_call(body, name=name, grid=(n_seq, n_tiles), in_specs=in_specs, out_specs=out_specs,
                         out_shape=out_shape, scratch_shapes=scratch, compiler_params=params)(*args)
    return res[0], res[1], list(res[2:])


def hgrn2_bwd(proj, offs, lb, do, states, n_seq, seq, *, name):
    T = n_seq * seq
    W = HG_HEADS * HG_DIM
    n_tiles = seq // HG_TILE
    nb = HG_TILE // HG_BLK
    lower, upper, total, bd, sel_t, sel_s = _hg_constants()

    def body(hq_ref, hf_ref, hi_ref, do_ref, st_in_ref, lb_ref, lower_ref, upper_ref, total_ref, bd_ref,
             selt_ref, sels_ref,
             dhq_ref, dhf_ref, dhi_ref, dlb_ref,
             dst_ref, q_s, k_s, v_s, b_s, qt_s, kt_s, d_s, eb_s, ekb_s, dq_s, dk_s, db_s, dv_s,
             p_s, e_s, w_s):
        first = jnp.logical_and(pl.program_id(0) == 0, pl.program_id(1) == 0)

        @pl.when(first)
        def _():
            dlb_ref[...] = jnp.zeros_like(dlb_ref)

        @pl.when(pl.program_id(1) == 0)
        def _():
            dst_ref[...] = jnp.zeros_like(dst_ref)

        hq, lbv = hq_ref[...], lb_ref[...]
        q, k, f, sg, sq, b, bl = _hg_prologue(hq, hf_ref[...], lbv, lower_ref[...], total_ref[...])
        eb = jnp.exp(b)
        ekb = jnp.exp(jnp.minimum(bl - b, 0.0))
        q_s[...] = q
        k_s[...] = k
        v_s[...] = hi_ref[...]
        b_s[...] = b
        eb_s[...] = eb
        ekb_s[...] = ekb
        qt_s[...] = q * eb
        kt_s[...] = k * ekb
        d_s[...] = jnp.exp(bl)
        rowi = lax.broadcasted_iota(jnp.int32, (HG_BLK, W), 0)
        last_row = rowi == HG_BLK - 1

        def block(i, slot):
            r0 = pl.multiple_of(i * HG_BLK, HG_BLK)
            rows = pl.ds(r0, HG_BLK)
            qi, ki, vi, bi, doi = q_s[rows, :], k_s[rows, :], v_s[rows, :], b_s[rows, :], do_ref[rows, :]
            for off, ts, n in _HG_SLABS:
                es = [jnp.where(rowi[:n] <= t, jnp.exp(jnp.minimum(bi[t:t + 1, :] - bi[:n], 0.0)), 0.0) for t in ts]
                sl = pl.ds(off, HG_BLK)
                e_s[slot, sl, :] = jnp.concatenate(es, axis=0)
                p_s[slot, sl, :] = jnp.concatenate([e * qi[t:t + 1, :] * ki[:n] for e, t in zip(es, ts)],
                                                   axis=0).astype(BF16)
                w_s[slot, sl, :] = jnp.concatenate([doi[t:t + 1, :] * vi[:n] for t in ts], axis=0).astype(BF16)
            a_b = _head_sums(p_s, slot, bd_ref[...])
            da_b = _head_sums(w_s, slot, bd_ref[...])
            x = da_b * e_s[slot]
            dq_in = jnp.dot(selt_ref[...], (x * _stack_by_s(ki)).astype(BF16), preferred_element_type=F32)
            dk_in = jnp.dot(sels_ref[...], (x * _stack_by_t(qi)).astype(BF16), preferred_element_type=F32)
            dv_in = jnp.dot(sels_ref[...], (a_b * _stack_by_t(doi)).astype(BF16), preferred_element_type=F32)
            qti, kti, di = qt_s[rows, :], kt_s[rows, :], d_s[rows, :]
            dqt, dkt, dvt, dd = [], [], [], []
            for h in range(HG_HEADS):
                hs = slice(h * HG_DIM, (h + 1) * HG_DIM)
                st_h = st_in_ref[i, hs, :]
                dst_h = dst_ref[hs, :]
                do_h, v_h = doi[:, hs].astype(BF16), vi[:, hs].astype(BF16)
                dst_b = dst_h.astype(BF16)
                dqt.append(jnp.dot(do_h, st_h.astype(BF16), preferred_element_type=F32))
                dkt.append(jnp.dot(v_h, dst_b, preferred_element_type=F32))
                dvt.append(lax.dot_general(kti[:, hs].astype(BF16), dst_b, (((1,), (1,)), ((), ())),
                                           preferred_element_type=F32))
                dd.append(jnp.sum(dst_h * st_h, axis=0, keepdims=True))
                upd = lax.dot_general(do_h, qti[:, hs].astype(BF16), (((0,), (0,)), ((), ())),
                                      preferred_element_type=F32)
                dst_ref[hs, :] = dst_h * di[0:1, hs] + upd
            dqt = jnp.concatenate(dqt, axis=1)
            dkt = jnp.concatenate(dkt, axis=1)
            dvt = jnp.concatenate(dvt, axis=1)
            dd = jnp.concatenate(dd, axis=1)
            dbl = jnp.sum(dkt * kti, axis=0, keepdims=True) + dd * di[0:1, :]
            db = qi * dq_in - ki * dk_in + dqt * qti - dkt * kti
            db_s[rows, :] = db + jnp.where(last_row, dbl, 0.0)
            dq_s[rows, :] = dq_in + dqt * eb_s[rows, :]
            dk_s[rows, :] = dk_in + dkt * ekb_s[rows, :]
            dv_s[rows, :] = dv_in + dvt

        def some_blocks(jj, carry):
            for slot in range(HG_SLOTS):
                block(nb - 1 - slot - HG_SLOTS * jj, slot)
            return carry

        lax.fori_loop(0, nb // HG_SLOTS, some_blocks, 0)

        dg = _dot3(upper_ref[...], db_s[...])
        dhq_ref[...] = (dq_s[...] * (sq * (1.0 + hq * (1.0 - sq)))).astype(dhq_ref.dtype)
        df = dg / f - dk_s[...]
        dhf_ref[...] = (df * (1.0 - lbv) * (sg * (1.0 - sg))).astype(dhf_ref.dtype)
        dhi_ref[...] = dv_s[...].astype(dhi_ref.dtype)
        dlb_ref[...] += _colsum(df * (1.0 - sg))

    rev = lambda s, t: s * n_tiles + (n_tiles - 1 - t)
    col = lambda off: functools.partial(lambda s, t, blk: (rev(s, t), blk), blk=off // W)
    const = lambda m: pl.BlockSpec(m.shape, lambda s, t: (0, 0))
    row = pl.BlockSpec((HG_TILE, W), lambda s, t: (rev(s, t), 0))
    tile_f32 = pltpu.VMEM((HG_TILE, W), F32)
    n2 = HG_STACK
    return pl.pallas_call(
        body, name=name,
        grid=(n_seq, n_tiles),
        in_specs=[pl.BlockSpec((HG_TILE, W), col(offs[0])), pl.BlockSpec((HG_TILE, W), col(offs[1])),
                  pl.BlockSpec((HG_TILE, W), col(offs[2])), row,
                  pl.BlockSpec((nb, W, HG_DIM), lambda s, t: (rev(s, t), 0, 0)),
                  const(lb), const(lower), const(upper), const(total), const(bd), const(sel_t), const(sel_s)],
        out_specs=[row, row, row, pl.BlockSpec((1, W), lambda s, t: (0, 0))],
        out_shape=[jax.ShapeDtypeStruct((T, W), BF16)] * 3 + [jax.ShapeDtypeStruct((1, W), F32)],
        scratch_shapes=[pltpu.VMEM((W, HG_DIM), F32)] + [tile_f32] * 13
                       + [pltpu.VMEM((HG_SLOTS, n2, W), BF16), pltpu.VMEM((HG_SLOTS, n2, W), F32),
                          pltpu.VMEM((HG_SLOTS, n2, W), BF16)],
        compiler_params=_cparams(("arbitrary", "arbitrary")),
    )(proj, proj, proj, do, states, lb, lower, upper, total, bd, sel_t, sel_s)


def _diag_mask(tq):
    return lax.broadcasted_iota(jnp.int32, (tq, tq), 1) <= lax.broadcasted_iota(jnp.int32, (tq, tq), 0)


def _qk(q, k):
    return lax.dot_general(q, k, (((1,), (1,)), ((), ())), preferred_element_type=F32)


def _causal_pairs(n, sweeps=1, by_key=False):
    if by_key:
        rows = [(i, j, 0) for j in range(n) for i in range(j, n)]
    else:
        rows = [(i, j, s) for i in range(n) for s in range(sweeps) for j in range(i + 1)]
    return tuple(jnp.asarray(np.array([r[c] for r in rows], np.int32)) for c in range(3))


def _fox_placement(fh):
    hw, wa = fh * FOX_HDIM, fh * FOX_AUG
    pq, pk = np.zeros((hw, wa), np.float32), np.zeros((hw, wa), np.float32)
    aq, ak = np.zeros((3 * LANES, wa), np.float32), np.zeros((3 * LANES, wa), np.float32)
    oq, ok = np.zeros((1, wa), np.float32), np.zeros((1, wa), np.float32)
    for h in range(fh):
        src, dst = np.arange(h * FOX_HDIM, (h + 1) * FOX_HDIM), np.arange(h * FOX_AUG, h * FOX_AUG + FOX_HDIM)
        pq[src, dst] = FOX_HDIM ** -0.5
        pk[src, dst] = 1.0
        gate = h * FOX_AUG + FOX_HDIM
        for r in range(3):
            aq[r * LANES + h, gate + r] = 1.0
            ak[r * LANES + h, gate + 3 + r] = -1.0
        oq[0, gate + 3:gate + 6] = 1.0
        ok[0, gate:gate + 3] = 1.0
    bf = lambda m: jnp.asarray(m, dtype=BF16)
    return {"pq": bf(pq), "pk": bf(pk), "aq": bf(aq), "ak": bf(ak), "oq": jnp.asarray(oq), "ok": jnp.asarray(ok),
            "pqt": bf(pq.T), "pkt": bf(pk.T)}


def _fox_specs(tq, fh, heads=1):
    groups = fh // heads

    def spec(tab):
        return pl.BlockSpec((None, tq, heads * FOX_AUG), lambda b, t, *tabs: (b // groups, tabs[tab][t], b % groups))
    return spec(0), spec(1)


def fox_fwd(qa, ka, va, *, name):
    n_seq, S, width = qa.shape
    fh = width // FOX_AUG
    nh = FOX_FWD_HEADS
    BH = n_seq * fh // nh
    tq = min(FOX_TQ, S)
    itab, jtab, _ = _causal_pairs(S // tq)

    def body(itab_ref, jtab_ref, q_ref, k_ref, v_ref, o_ref, ox_ref, lse_ref, *scratch):
        t = pl.program_id(1)
        i, j = itab_ref[t], jtab_ref[t]
        per_head = [scratch[4 * h:4 * h + 4] for h in range(nh)]

        @pl.when(j == 0)
        def _():
            for m_s, l_s, acc_s, acc_lo_s in per_head:
                m_s[...] = jnp.full_like(m_s, NEG_INF)
                l_s[...] = jnp.zeros_like(l_s)
                acc_s[...] = jnp.zeros_like(acc_s)
                acc_lo_s[...] = jnp.zeros_like(acc_lo_s)

        def step(on_diagonal):
            for h, (m_s, l_s, acc_s, acc_lo_s) in enumerate(per_head):
                lanes = slice(h * FOX_AUG, (h + 1) * FOX_AUG)
                s = _qk(q_ref[:, lanes], k_ref[:, lanes])
                if on_diagonal:
                    s = jnp.where(_diag_mask(tq), s, NEG_INF)
                m_prev = m_s[...]
                m_new = jnp.maximum(m_prev, jnp.max(s, axis=-1, keepdims=True))
                alpha = jnp.exp(m_prev - m_new)
                p = jnp.exp(s - m_new[:, 0:1])
                p_hi = p.astype(BF16)
                p_lo = (p - p_hi.astype(F32)).astype(BF16)
                v = v_ref[:, lanes]
                l_s[...] = alpha * l_s[...] + jnp.sum(p, axis=-1, keepdims=True)
                acc_s[...] = alpha * acc_s[...] + jnp.dot(p_hi, v, preferred_element_type=F32)
                acc_lo_s[...] = alpha * acc_lo_s[...] + jnp.dot(p_lo, v, preferred_element_type=F32)
                m_s[...] = m_new

        @pl.when(j < i)
        def _():
            step(False)

        @pl.when(j == i)
        def _():
            step(True)
            for h, (m_s, l_s, acc_s, acc_lo_s) in enumerate(per_head):
                lanes = slice(h * FOX_AUG, (h + 1) * FOX_AUG)
                inv_l = 1.0 / l_s[...]
                o_ref[:, lanes] = (acc_s[...] * inv_l).astype(o_ref.dtype)
                ox_ref[:, lanes] = (acc_s[...] + acc_lo_s[...]) * inv_l
                lse_ref[:, lanes] = m_s[...] + jnp.log(l_s[...])

    qspec, kspec = _fox_specs(tq, fh, nh)
    wide = jax.ShapeDtypeStruct((n_seq, S, width), F32)
    return pl.pallas_call(
        body, name=name,
        grid_spec=pltpu.PrefetchScalarGridSpec(
            num_scalar_prefetch=2, grid=(BH, itab.shape[0]),
            in_specs=[qspec, kspec, kspec],
            out_specs=[qspec, qspec, qspec],
            scratch_shapes=[pltpu.VMEM((tq, LANES), F32)] * (4 * nh)),
        out_shape=[jax.ShapeDtypeStruct((n_seq, S, width), BF16), wide, wide],
        compiler_params=_cparams(("parallel", "arbitrary")),
    )(itab, jtab, qa, ka, va)


def _fox_ds(q, k, v, do, ox, lse, on_diagonal):
    s = _qk(q, k)
    if on_diagonal:
        s = jnp.where(_diag_mask(s.shape[0]), s, NEG_INF)
    p = jnp.exp(s - lse[:, 0:1])
    delta = jnp.sum(do.astype(F32) * ox, axis=-1, keepdims=True)
    return p, p * (_qk(do, v) - delta)


def fox_bwd(qa, ka, va, do, ox, lse, *, name):
    n_seq, S, width = qa.shape
    fh = width // FOX_AUG
    BH = n_seq * fh
    tq = min(FOX_TQ, S)
    itab, jtab, _ = _causal_pairs(S // tq)

    def body(itab_ref, jtab_ref, q_ref, k_ref, v_ref, do_ref, ox_ref, lse_ref, dq_ref, dk_ref, dv_ref, dsum_ref):
        t = pl.program_id(1)
        i, j = itab_ref[t], jtab_ref[t]

        @pl.when(t == 0)
        def _():
            dq_ref[...] = jnp.zeros_like(dq_ref)
            dk_ref[...] = jnp.zeros_like(dk_ref)
            dv_ref[...] = jnp.zeros_like(dv_ref)
            dsum_ref[...] = jnp.zeros_like(dsum_ref)

        q_rows = pl.ds(pl.multiple_of(i * tq, tq), tq)
        k_rows = pl.ds(pl.multiple_of(j * tq, tq), tq)

        def step(on_diagonal):
            q, k, do = q_ref[...], k_ref[...], do_ref[...]
            p, ds = _fox_ds(q, k, v_ref[...], do, ox_ref[...], lse_ref[...], on_diagonal)
            ds_b = ds.astype(BF16)
            tn = (((0,), (0,)), ((), ()))
            dq_ref[q_rows, :] += jnp.dot(ds_b, k, preferred_element_type=F32)
            dk_ref[k_rows, :] += lax.dot_general(ds_b, q, tn, preferred_element_type=F32)
            dv_ref[k_rows, :] += lax.dot_general(p.astype(BF16), do, tn, preferred_element_type=F32)
            dsum_ref[:, k_rows] += _colsum(ds)

        @pl.when(j < i)
        def _():
            step(False)

        @pl.when(j == i)
        def _():
            step(True)

    qspec, kspec = _fox_specs(tq, fh)
    whole = pl.BlockSpec((None, S, FOX_AUG), lambda b, t, it, jt: (b // fh, 0, b % fh))
    wide = jax.ShapeDtypeStruct((n_seq, S, width), F32)
    return pl.pallas_call(
        body, name=name,
        grid_spec=pltpu.PrefetchScalarGridSpec(
            num_scalar_prefetch=2, grid=(BH, itab.shape[0]),
            in_specs=[qspec, kspec, kspec, qspec, qspec, qspec],
            out_specs=[whole, whole, whole, pl.BlockSpec((None, 1, S), lambda b, t, it, jt: (b, 0, 0))]),
        out_shape=[wide, wide, wide, jax.ShapeDtypeStruct((BH, 1, S), F32)],
        compiler_params=_cparams(("parallel", "arbitrary")),
    )(itab, jtab, qa, ka, va, do, ox, lse)


def seq_cumsum(x, n_seq, seq, *, reverse, name):
    T, C = x.shape
    tb = min(256, seq)
    n = seq // tb
    r = np.arange(tb)
    tri = (r[None, :] >= r[:, None]) if reverse else (r[None, :] <= r[:, None])
    tri = jnp.asarray(tri.astype(np.float32), dtype=BF16)

    def body(x_ref, tri_ref, o_ref, carry_s):
        @pl.when(pl.program_id(1) == 0)
        def _():
            carry_s[...] = jnp.zeros_like(carry_s)

        xv = x_ref[...]
        o_ref[...] = _dot3(tri_ref[...], xv) + carry_s[...]
        carry_s[...] += _colsum(xv)

    blk = (lambda s, t: (s * n + (n - 1 - t), 0)) if reverse else (lambda s, t: (s * n + t, 0))
    return pl.pallas_call(
        body, name=name,
        grid=(n_seq, n),
        in_specs=[pl.BlockSpec((tb, C), blk), pl.BlockSpec((tb, tb), lambda s, t: (0, 0))],
        out_specs=pl.BlockSpec((tb, C), blk),
        out_shape=jax.ShapeDtypeStruct((T, C), F32),
        scratch_shapes=[pltpu.VMEM((1, C), F32)],
        compiler_params=_cparams(("arbitrary", "arbitrary")),
    )(x, tri)


def _place():
    return lax.axis_index("x"), lax.axis_index("y"), lax.axis_index("c")


def _other_chips(x, y):
    return [(1 - x, y), (x, 1 - y), (1 - x, 1 - y)]


def _hbm_call(body, ins, out_shape, n_sems, *, name):
    hbm = pl.BlockSpec(memory_space=pl.ANY)
    return pl.pallas_call(
        body, name=name,
        in_specs=[hbm] * len(ins), out_specs=[hbm] * len(out_shape), out_shape=out_shape,
        scratch_shapes=[pltpu.SemaphoreType.DMA((n_sems,)), pltpu.SemaphoreType.DMA((n_sems,)),
                        pltpu.SemaphoreType.DMA((len(ins),))],
        compiler_params=pltpu.CompilerParams(has_side_effects=True),
    )(*ins)


def allgather_chips(shards, *, name):
    return _exchange_call(allgather_rider(shards), name=name)


def _allgather_ops(x_refs, o_refs, send_sems, recv_sems, local_sems):
    def copies():
        x, y, c = _place()
        me = 2 * x + y
        chips = _other_chips(x, y)
        own, first, passed, landed, handed = [], [], [], [], []
        for b, (x_ref, o_ref) in enumerate(zip(x_refs, o_refs)):
            half = x_ref.shape[0] // 2
            mine, theirs = pl.ds(c * half, half), pl.ds((1 - c) * half, half)
            own.append(pltpu.make_async_copy(x_ref, o_ref.at[me], local_sems.at[b]))

            def copy(k, src, chip, rows, to, o_ref=o_ref, b=b):
                return pltpu.make_async_remote_copy(src_ref=src, dst_ref=o_ref.at[2 * chip[0] + chip[1], rows],
                                                    send_sem=send_sems.at[6 * b + k], recv_sem=recv_sems.at[6 * b + k],
                                                    device_id=to, device_id_type=MESH)
            for j, chip in enumerate(chips):
                first.append(copy(j, x_ref.at[mine], (x, y), mine, (*chip, c)))
                landed.append(copy(j, x_ref.at[mine], chip, mine, (*chip, c)))
                passed.append(copy(3 + j, o_ref.at[2 * chip[0] + chip[1], mine], chip, mine, (x, y, 1 - c)))
                handed.append(copy(3 + j, x_ref.at[mine], chip, theirs, (x, y, 1 - c)))
        return own, first, passed, landed, handed

    def start():
        own, first, _, _, _ = copies()
        for cp in own + first:
            cp.start()

    def finish():
        own, first, passed, landed, handed = copies()
        for arrived, forward in zip(landed, passed):
            arrived.wait_recv()
            forward.start()
        for cp in handed:
            cp.wait_recv()
        for cp in first + passed:
            cp.wait_send()
        for cp in own:
            cp.wait()
    return start, finish


def _scatter_ops(x_refs, o_refs, send_sems, recv_sems, local_sems):
    def copies():
        x, y, c = _place()
        return [pltpu.make_async_remote_copy(
            src_ref=x_ref.at[2 * px + py], dst_ref=o_ref.at[j], send_sem=send_sems.at[3 * b + j],
            recv_sem=recv_sems.at[3 * b + j], device_id=(px, py, c), device_id_type=MESH)
            for b, (x_ref, o_ref) in enumerate(zip(x_refs, o_refs)) for j, (px, py) in enumerate(_other_chips(x, y))]

    def start():
        for cp in copies():
            cp.start()

    def finish():
        sends = copies()
        for cp in sends:
            cp.wait_recv()
        for cp in sends:
            cp.wait_send()
    return start, finish


class Rider(NamedTuple):
    ins: list
    out_shape: list
    n_sems: int
    ops: object

    def specs(self):
        hbm = pl.BlockSpec(memory_space=pl.ANY)
        sems = [pltpu.SemaphoreType.DMA((self.n_sems,)), pltpu.SemaphoreType.DMA((self.n_sems,)),
                pltpu.SemaphoreType.DMA((len(self.ins),))]
        return [hbm] * len(self.ins), [hbm] * len(self.out_shape), sems

    def wrap(self, body, n_in, n_out, grid_rank):
        k_in, k_out = len(self.ins), len(self.out_shape)

        def carried(*refs):
            ins, r_ins = refs[:n_in], refs[n_in:n_in + k_in]
            outs = refs[n_in + k_in:n_in + k_in + n_out]
            r_outs = refs[n_in + k_in + n_out:n_in + k_in + n_out + k_out]
            scratch, sems = refs[n_in + k_in + n_out + k_out:-3], refs[-3:]
            first = functools.reduce(jnp.logical_and, [pl.program_id(a) == 0 for a in range(grid_rank)])
            last = functools.reduce(jnp.logical_and,
                                    [pl.program_id(a) == pl.num_programs(a) - 1 for a in range(grid_rank)])
            pl.when(first)(lambda: self.ops(r_ins, r_outs, *sems)[0]())
            body(*ins, *outs, *scratch)
            pl.when(last)(lambda: self.ops(r_ins, r_outs, *sems)[1]())
        return carried


def _exchange_call(rider, *, name):
    def body(*refs):
        k = len(rider.ins)
        start, finish = rider.ops(refs[:k], refs[k:k + len(rider.out_shape)], *refs[-3:])
        start()
        finish()
    in_specs, out_specs, sems = rider.specs()
    return pl.pallas_call(body, name=name, in_specs=in_specs, out_specs=out_specs, out_shape=rider.out_shape,
                          scratch_shapes=sems, compiler_params=pltpu.CompilerParams(has_side_effects=True))(*rider.ins)


def allgather_rider(shards):
    assert all(s.shape[0] % (2 * ROW_ALIGN) == 0 for s in shards)
    return Rider(list(shards), [jax.ShapeDtypeStruct((4,) + s.shape, s.dtype) for s in shards], 6 * len(shards),
                 _allgather_ops)


def scatter_rider(parts):
    return Rider(list(parts), [jax.ShapeDtypeStruct((3,) + p.shape[1:], p.dtype) for p in parts], 3 * len(parts),
                 _scatter_ops)


def scatter_chips(parts, *, name):
    return _exchange_call(scatter_rider(parts), name=name)


def swap_cores(vs, *, name):
    nb = len(vs)

    def body(*refs):
        x_refs, o_refs = refs[:nb], refs[nb:2 * nb]
        send_sems, recv_sems, _ = refs[2 * nb:]
        x, y, c = _place()
        copies = [pltpu.make_async_remote_copy(src_ref=x_ref, dst_ref=o_ref, send_sem=send_sems.at[b],
                                               recv_sem=recv_sems.at[b], device_id=(x, y, 1 - c), device_id_type=MESH)
                  for b, (x_ref, o_ref) in enumerate(zip(x_refs, o_refs))]
        for cp in copies:
            cp.start()
        for cp in copies:
            cp.wait()

    return _hbm_call(body, vs, [jax.ShapeDtypeStruct(v.shape, v.dtype) for v in vs], nb, name=name)


def allreduce_small(v, *, name):
    R, C = v.shape

    def body(x_ref, o_ref, gath_ref, send_sems, recv_sems):
        x, y, c = _place()
        me = 4 * x + 2 * y + c
        gath_ref[me] = x_ref[...]
        flips = [(k >> 2 & 1, k >> 1 & 1, k & 1) for k in range(1, 8)]
        sends = []
        for j, (fx, fy, fc) in enumerate(flips):
            peer = (x ^ fx, y ^ fy, c ^ fc)
            cp = pltpu.make_async_remote_copy(src_ref=x_ref, dst_ref=gath_ref.at[me], send_sem=send_sems.at[j],
                                              recv_sem=recv_sems.at[j], device_id=peer, device_id_type=MESH)
            cp.start()
            sends.append(cp)
        for j, (fx, fy, fc) in enumerate(flips):
            peer = (x ^ fx, y ^ fy, c ^ fc)
            pltpu.make_async_remote_copy(src_ref=x_ref, dst_ref=gath_ref.at[4 * peer[0] + 2 * peer[1] + peer[2]],
                                         send_sem=send_sems.at[j], recv_sem=recv_sems.at[j], device_id=peer,
                                         device_id_type=MESH).wait_recv()
        for cp in sends:
            cp.wait_send()
        total = gath_ref[0]
        for d in range(1, 8):
            total = total + gath_ref[d]
        o_ref[...] = total

    vm = pl.BlockSpec(memory_space=pltpu.VMEM)
    out, _ = pl.pallas_call(
        body, name=name,
        in_specs=[vm], out_specs=[vm, vm],
        out_shape=[jax.ShapeDtypeStruct((R, C), F32), jax.ShapeDtypeStruct((8, R, C), F32)],
        scratch_shapes=[pltpu.SemaphoreType.DMA((7,)), pltpu.SemaphoreType.DMA((7,))],
        compiler_params=pltpu.CompilerParams(has_side_effects=True),
    )(v)
    return out


ROW_ALIGN = 16
PACK_W = 1024
SUM_TILE = 512
BIG_WEIGHTS = (("w_in", 1), ("w_a", 1), ("w_b", 1), ("w_o", 0), ("w_ff1", 1), ("w_ff2", 0), ("w_pg", 0), ("w_p", 1))


def _b_layout(d, ple):
    hw, q = d // 2, d // 4
    small = 2 * d + 2 * q
    lay = {"w_ff1": (0, 0, d, d), "w_ff2": (d, 0, d, d), "w_o": (2 * d, 0, q, d), "w_pg": (2 * d + q, 0, q, d),
           "w_a": (small, 0, hw, q), "w_b": (small, q, hw, q), "w_p": (small, 2 * q, ple, q)}
    return lay, small + hw


def pack_a(w_in_shard):
    rows, cols = w_in_shard.shape
    pad = -cols % LANES
    return jnp.concatenate([w_in_shard, jnp.zeros((rows, pad), w_in_shard.dtype)], axis=1)


def pack_b(shards, d):
    hw, q = d // 2, d // 4
    dt = shards["w_a"].dtype
    wp = shards["w_p"]
    wp = jnp.concatenate([wp, jnp.zeros((hw - wp.shape[0], q), dt)], axis=0)
    small = jnp.concatenate([shards["w_a"], shards["w_b"], wp, jnp.zeros((hw, d - 3 * q), dt)], axis=1)
    return jnp.concatenate([shards["w_ff1"], shards["w_ff2"], shards["w_o"], shards["w_pg"], small], axis=0)


def unpack_b(buf, lay):
    return {nm: buf[r0:r0 + rows, c0:c0 + cols] for nm, (r0, c0, rows, cols) in lay.items()}


def _win_layout(d):
    hw = d // 2
    fh = hw // FOX_HDIM
    orig = {"hq": (0, hw), "hf": (hw, hw), "hi": (2 * hw, hw), "hg": (3 * hw, hw), "fq": (4 * hw, hw),
            "fk": (5 * hw, hw), "fv": (6 * hw, hw), "ff": (7 * hw, fh), "ga": (7 * hw + fh, d), "gb": (7 * hw + fh + d, d)}
    order = ["ga", "gb", "hq", "hf", "hi", "hg", "fq", "fk", "fv", "ff"]
    mine, off = {}, 0
    for nm in order:
        width = orig[nm][1] if nm != "ff" else LANES
        mine[nm] = (off, width)
        off += width
    return orig, order, mine, off


def _adam_fn(rows, vecs):
    w, g, m, v = rows
    m2 = ADAM_B1 * m + (1.0 - ADAM_B1) * g
    v2 = ADAM_B2 * v + (1.0 - ADAM_B2) * (g * g)
    m_hat = m2 / (1.0 - ADAM_B1 ** ADAM_STEP)
    v_hat = v2 / (1.0 - ADAM_B2 ** ADAM_STEP)
    delta = -ADAM_LR * (m_hat / (jnp.sqrt(v_hat) + ADAM_EPS) + ADAM_WD * w)
    return [delta, m2, v2], []


def adamw_small(small, p0, ws, ms, vs, *, name):
    n = len(ws)
    hw = p0.shape[1]
    fh = ws[8].shape[1]

    def body(small_ref, p0_ref, *refs):
        w_refs, m_refs, v_refs = refs[:n], refs[n:2 * n], refs[2 * n:3 * n]
        g_out, d_out, m_out, v_out = (refs[(3 + k) * n:(4 + k) * n] for k in range(4))
        sm = small_ref[...]
        p = p0_ref[...]
        d_lb = sm[6:7, hw:2 * hw] * (p * (1.0 - p))
        grads = [sm[r:r + 1, :] for r in range(6)]
        grads += [jnp.concatenate([d_lb, -d_lb], axis=0), sm[6:7, :hw], sm[7:8, :fh]]
        for i in range(n):
            (delta, m2, v2), _ = _adam_fn([w_refs[i][...], grads[i], m_refs[i][...], v_refs[i][...]], [])
            g_out[i][...], d_out[i][...], m_out[i][...], v_out[i][...] = grads[i], delta, m2, v2

    shapes = [jax.ShapeDtypeStruct(w.shape, F32) for w in ws]
    return pl.pallas_call(body, name=name, out_shape=shapes * 4)(small, p0, *ws, *ms, *vs)


def adamw(w, g, m, v, *, name):
    c = w.shape[1]
    (delta, m2, v2), _ = rowwise(_adam_fn, [w, g, m, v], [], [(c, F32)] * 3, name=name, tm=256)
    return delta, m2, v2


def kernel(x, p, ln0_g, ln0_b, w_in, hg_lb, hg_norm_g, fox_fb, w_a, w_b, w_o, ln1_g, ln1_b, w_ff1, w_ff2, w_pg, w_p, ln2_g, ln2_b, loss_target, m_ln0_g, m_ln0_b, m_w_in, m_hg_lb, m_hg_norm_g, m_fox_fb, m_w_a, m_w_b, m_w_o, m_ln1_g, m_ln1_b, m_w_ff1, m_w_ff2, m_w_pg, m_w_p, m_ln2_g, m_ln2_b, v_ln0_g, v_ln0_b, v_w_in, v_hg_lb, v_hg_norm_g, v_fox_fb, v_w_a, v_w_b, v_w_o, v_ln1_g, v_ln1_b, v_w_ff1, v_w_ff2, v_w_pg, v_w_p, v_ln2_g, v_ln2_b):
    n_seq, seq, d = x.shape
    T = n_seq * seq
    hw = d // 2
    fh = hw // FOX_HDIM
    bh = n_seq * fh
    orig, order, mine, n_in = _win_layout(d)

    big = {"w_in": w_in[0], "w_a": w_a[0], "w_b": w_b[0], "w_o": w_o[0], "w_ff1": w_ff1[0], "w_ff2": w_ff2[0],
           "w_pg": w_pg[0], "w_p": w_p[0]}
    big_m = {"w_in": m_w_in[0], "w_a": m_w_a[0], "w_b": m_w_b[0], "w_o": m_w_o[0], "w_ff1": m_w_ff1[0],
             "w_ff2": m_w_ff2[0], "w_pg": m_w_pg[0], "w_p": m_w_p[0]}
    big_v = {"w_in": v_w_in[0], "w_a": v_w_a[0], "w_b": v_w_b[0], "w_o": v_w_o[0], "w_ff1": v_w_ff1[0],
             "w_ff2": v_w_ff2[0], "w_pg": v_w_pg[0], "w_p": v_w_p[0]}
    names = [nm for nm, _ in BIG_WEIGHTS]
    axis = dict(BIG_WEIGHTS)
    ple = w_p.shape[1]
    lay, b_rows = _b_layout(d, ple)
    in_cols = big["w_in"].shape[1]

    gather_w_in = allgather_rider([pack_a(big["w_in"].astype(BF16))])
    gather_rest = allgather_rider([pack_b({nm: big[nm].astype(BF16) for nm in names if nm != "w_in"}, d)])

    x2 = x.reshape(T, d)
    tgt = loss_target.reshape(T, d)
    p_b = p.reshape(T, p.shape[-1]).astype(BF16)
    vec = lambda a: a.reshape(1, -1)
    probs = jax.nn.softmax(hg_lb, axis=0)
    lb = vec(probs[0])

    def ln0_fn(rows, vecs):
        h = _ln_stats(rows[0]) * vecs[0] + vecs[1]
        return [h, h], []
    (h0, h0b), _, (a_all,) = rowwise(ln0_fn, [x2], [vec(ln0_g), vec(ln0_b)], [(d, F32), (d, BF16)], name="ln0_fwd",
                                     rider=gather_w_in)
    win = jnp.concatenate([a_all[s, :, :in_cols] for s in range(4)], axis=1)
    win_mine = jnp.concatenate(
        [win[:, orig[nm][0]:orig[nm][0] + orig[nm][1]] for nm in order]
        + [jnp.zeros((d, LANES - fh), BF16)], axis=1)
    proj = matmul_nn(h0b, win_mine, name="in_proj")

    o_raw, hg_states, (b_all,) = hgrn2_fwd(proj, [mine["hq"][0], mine["hf"][0], mine["hi"][0]], lb, n_seq, seq,
                                           name="hgrn2_fwd", rider=gather_rest)
    view = lambda nm, k, n: WView(b_all, lay[nm][0], lay[nm][1], k, n, axis[nm])
    w_ff1_v, w_ff2_v = view("w_ff1", d, 4 * d), view("w_ff2", 4 * d, d)

    def whole(nm):
        r0, c0, rows, cols = lay[nm]
        return jnp.concatenate([b_all[s, r0:r0 + rows, c0:c0 + cols] for s in range(4)], axis=axis[nm])
    w_o_v, w_pg_v, w_a_v, w_p_v, w_b_full = whole("w_o"), whole("w_pg"), whole("w_a"), whole("w_p"), whole("w_b")

    def ya_fn(rows, vecs):
        o, hg = rows
        outs = []
        for h in range(HG_HEADS):
            oh = o[:, h * HG_DIM:(h + 1) * HG_DIM]
            outs.append(oh * lax.rsqrt(jnp.mean(oh * oh, axis=-1, keepdims=True) + RMS_EPS))
        y = jnp.concatenate(outs, axis=1) * vecs[0] * (hg * _sigmoid(hg))
        return [y], []
    (y_a,), _ = rowwise(ya_fn, [o_raw, (proj,) + mine["hg"]], [hg_norm_g], [(hw, BF16)], name="hgrn2_out_fwd")

    fb_pad = jnp.concatenate([fox_fb, jnp.zeros((1, LANES - fh), F32)], axis=1)

    def lf_fn(rows, vecs):
        u = rows[0] + vecs[0]
        return [jnp.minimum(u, 0.0) - jnp.log(1.0 + jnp.exp(-jnp.abs(u)))], []
    (lf,), _ = rowwise(lf_fn, [(proj,) + mine["ff"]], [fb_pad], [(LANES, F32)], name="fox_logf")
    c_cum = seq_cumsum(lf, n_seq, seq, reverse=False, name="fox_cumsum")

    place = _fox_placement(fh)

    def prep_fn(rows, vecs):
        fq_, fk_, fv_, cc = rows
        pq, pk, aq, ak, oq, ok = vecs
        parts = jnp.concatenate(_split3(cc), axis=1)
        mm = lambda a_, b_: jnp.dot(a_, b_, preferred_element_type=F32)
        q_ = mm(fq_.astype(BF16), pq) + mm(parts, aq) + oq
        k_ = mm(fk_.astype(BF16), pk) + mm(parts, ak) + ok
        return [q_, k_, mm(fv_.astype(BF16), pk)], []
    wa = fh * FOX_AUG
    (qa, ka, va), _ = rowwise(prep_fn, [(proj,) + mine["fq"], (proj,) + mine["fk"], (proj,) + mine["fv"], c_cum],
                              [place[nm] for nm in ("pq", "pk", "aq", "ak", "oq", "ok")], [(wa, BF16)] * 3,
                              name="fox_prep")
    as_seq = lambda t2d: t2d.reshape(n_seq, seq, t2d.shape[1])
    o_fox, ox_fox, lse = fox_fwd(as_seq(qa), as_seq(ka), as_seq(va), name="fox_fwd")
    y_b = o_fox.reshape(T, wa)
    wb_pad = jnp.concatenate([w_b_full.reshape(fh, FOX_HDIM, d), jnp.zeros((fh, FOX_AUG - FOX_HDIM, d), BF16)],
                             axis=1).reshape(wa, d)

    pa = matmul_nn(y_a, w_a_v, name="proj_a")
    pb = matmul_nn(y_b, wb_pad, name="proj_b")

    def merge_fn(rows, vecs):
        ga, gb, a, b = rows
        return [_sigmoid(ga) * a + _sigmoid(gb) * b], []
    (merged,), _ = rowwise(merge_fn, [(proj,) + mine["ga"], (proj,) + mine["gb"], pa, pb], [], [(d, BF16)],
                           name="merge_fwd")
    fused_tm = 512

    def ln1_post(mix, aux, vecs):
        z = ALPHA * aux[0] + mix
        h = _ln_stats(z) * vecs[0] + vecs[1]
        return [z, h, h], []
    (z1, h1, h1b), _ = matmul_nn(merged, w_o_v, name="out_proj_ln1", tm=fused_tm, post=ln1_post, post_aux=[h0],
                                 post_vecs=[ln1_g, ln1_b], post_outs=[F32, F32, BF16])

    relu2 = lambda u: jnp.square(jnp.maximum(u, 0.0))
    act = matmul_nn(h1b, w_ff1_v, name="ff1", out_dtype=BF16, epilogue=relu2)
    pg = matmul_nn(h1b, w_pg_v, name="ple_gate")
    pe = matmul_nn(p_b, w_p_v, name="ple_embed")

    def head_post(ffv, aux, vecs):
        h1v, pgv, pev, t = aux
        g2, b2 = vecs
        sp = _sigmoid(pgv)
        z = ALPHA * h1v + ffv + sp * pev
        y = _ln_stats(z) * g2 + b2
        err = y - t
        loss_rows = 0.5 * jnp.mean(err * err, axis=-1, keepdims=True)
        dy = err * (1.0 / d)
        dz, dg2, db2 = _ln_bwd(z, dy, g2)
        loss_acc = jnp.broadcast_to(_colsum(loss_rows), (1, d))
        return [dz, dz, dz * pev * (sp * (1.0 - sp)), dz * sp], [dg2, db2, loss_acc]
    (dz2, dz2b, dpg, dpe), (g_ln2_g, g_ln2_b, loss_part) = matmul_nn(
        act, w_ff2_v, name="ff2_head", tm=fused_tm, post=head_post, post_aux=[h1, pg, pe, tgt],
        post_vecs=[ln2_g, ln2_b], post_outs=[F32, BF16, BF16, BF16], post_accs=[d, d, d])

    dact = lambda da, a: da * (2.0 * jnp.sqrt(a.astype(F32)))
    du = matmul_nn(dz2b, w_ff2_v, transpose_rhs=True, name="d_ff2", out_dtype=BF16, epilogue=dact, aux=act)
    dh1_pg = matmul_nn(dpg, w_pg_v, transpose_rhs=True, name="d_ple_gate")

    def ln1_bwd_post(dh1_ff, aux, vecs):
        dh1 = ALPHA * aux[0] + dh1_ff + aux[1]
        dz, dg, db = _ln_bwd(aux[2], dh1, vecs[0])
        return [dz, dz], [dg, db]
    (dz1, dz1b), (g_ln1_g, g_ln1_b) = matmul_nn(
        du, w_ff1_v, transpose_rhs=True, name="d_ff1_ln1", tm=fused_tm, post=ln1_bwd_post, post_aux=[dz2, dh1_pg, z1],
        post_vecs=[ln1_g], post_outs=[F32, BF16], post_accs=[d, d])

    def merge_bwd_post(dm, aux, vecs):
        ga, gb, a, b = aux
        sa, sb = _sigmoid(ga), _sigmoid(gb)
        return [dm * a * (sa * (1.0 - sa)), dm * b * (sb * (1.0 - sb)), dm * sa, dm * sb], []
    (dga, dgb, dma, dmb), _ = matmul_nn(
        dz1b, w_o_v, transpose_rhs=True, name="d_out_proj_merge", tm=fused_tm, post=merge_bwd_post,
        post_aux=[(proj,) + mine["ga"], (proj,) + mine["gb"], pa, pb], post_outs=[BF16] * 4)
    dya = matmul_nn(dma, w_a_v, transpose_rhs=True, name="d_proj_a")
    dyb = matmul_nn(dmb, wb_pad, transpose_rhs=True, name="d_proj_b", out_dtype=BF16)

    def ya_bwd_fn(rows, vecs):
        o, hg, dy = rows
        ng = vecs[0]
        sg = _sigmoid(hg)
        gate = hg * sg
        dn_parts, do_parts, n_parts = [], [], []
        for h in range(HG_HEADS):
            hs = slice(h * HG_DIM, (h + 1) * HG_DIM)
            oh = o[:, hs]
            r = lax.rsqrt(jnp.mean(oh * oh, axis=-1, keepdims=True) + RMS_EPS)
            nh = oh * r
            dn = dy[:, hs] * ng[:, hs] * gate[:, hs]
            do_parts.append(r * (dn - nh * jnp.mean(dn * nh, axis=-1, keepdims=True)))
            n_parts.append(nh)
        nrm = jnp.concatenate(n_parts, axis=1)
        dhg = dy * nrm * ng * (sg * (1.0 + hg * (1.0 - sg)))
        return [jnp.concatenate(do_parts, axis=1), dhg], [_colsum(dy * nrm * gate)]
    (do_raw, dhg), (g_norm_g,) = rowwise(ya_bwd_fn, [o_raw, (proj,) + mine["hg"], dya], [hg_norm_g],
                                         [(hw, F32), (hw, BF16)], [hw], name="hgrn2_out_bwd")
    dhq, dhf, dhi, g_lb = hgrn2_bwd(proj, [mine["hq"][0], mine["hf"][0], mine["hi"][0]], lb, do_raw, hg_states,
                                    n_seq, seq, name="hgrn2_bwd")

    do_fox = as_seq(dyb)
    dqa, dka, dva, dsum = fox_bwd(as_seq(qa), as_seq(ka), as_seq(va), do_fox, ox_fox, lse, name="fox_bwd")

    def unprep_fn(rows, vecs):
        mm = lambda a_, b_: jnp.dot(a_.astype(BF16), b_, preferred_element_type=F32)
        return [mm(rows[0], vecs[0]), mm(rows[1], vecs[1]), mm(rows[2], vecs[1])], []
    (dfq, dfk, dfv), _ = rowwise(unprep_fn, [dqa.reshape(T, wa), dka.reshape(T, wa), dva.reshape(T, wa)],
                                 [place["pqt"], place["pkt"]], [(hw, BF16)] * 3, name="fox_unprep")
    dc = -dsum.reshape(n_seq, fh, seq).transpose(0, 2, 1).reshape(T, fh)
    dc = jnp.concatenate([dc, jnp.zeros((T, LANES - fh), F32)], axis=1)
    dlf = seq_cumsum(dc, n_seq, seq, reverse=True, name="fox_cumsum_bwd")

    def lf_bwd_fn(rows, vecs):
        u = rows[0] + vecs[0]
        du_ = rows[1] * _sigmoid(-u)
        return [du_], [_colsum(du_)]
    (dff_,), (g_fb,) = rowwise(lf_bwd_fn, [(proj,) + mine["ff"], dlf], [fb_pad], [(LANES, BF16)], [LANES],
                               name="fox_logf_bwd")

    dproj = jnp.concatenate([dga, dgb, dhq, dhf, dhi, dhg, dfq, dfk, dfv, dff_], axis=1)

    grads_b = jnp.zeros((4, b_rows, d), F32)
    for nm, lhs, rhs in (("w_ff1", h1b, du), ("w_ff2", act, dz2b), ("w_o", merged, dz1b), ("w_pg", h1b, dpg),
                         ("w_a", y_a, dma), ("w_p", p_b, dpe)):
        grads_b = matmul_tn(lhs, rhs, name="g_" + nm, into=(grads_b, lay[nm][0], lay[nm][1], axis[nm]))
    gfull = {"w_b": matmul_tn(y_b, dmb, name="g_w_b").reshape(fh, FOX_AUG, d)[:, :FOX_HDIM].reshape(hw, d)}

    def chip_parts(nm, s):
        g = gfull[nm]
        n = g.shape[axis[nm]] // 4
        return lax.slice_in_dim(g, s * n, (s + 1) * n, axis=axis[nm])
    rb, cb = lay["w_b"][0], lay["w_b"][1]
    grads_b = lax.dynamic_update_slice(grads_b, jnp.stack([chip_parts("w_b", s) for s in range(4)]), (0, rb, cb))
    me = 2 * lax.axis_index("x") + lax.axis_index("y")
    core = lax.axis_index("c")

    def sum2_fn(rows, vecs):
        s = rows[0] + rows[1].astype(F32)
        return [s, s], []

    def sum4_fn(rows, vecs):
        a, r0, r1, r2 = rows
        return [((a + r0.astype(F32)) + r1.astype(F32)) + r2.astype(F32)], []

    def chip_pair_sum(g, tag):
        h, cols = g.shape[1] // 2, g.shape[2]
        keep = lax.dynamic_slice_in_dim(g, core * h, h, axis=1)
        give = lax.dynamic_slice_in_dim(g, (1 - core) * h, h, axis=1).astype(BF16)
        (from_core,) = swap_cores([give], name="swap_partials_" + tag)
        (s32, s16), _ = rowwise(sum2_fn, [keep.reshape(4 * h, cols), from_core.reshape(4 * h, cols)], [],
                                [(cols, F32), (cols, BF16)], name="sum_cores_" + tag, tm=SUM_TILE)
        return s32.reshape(4, h, cols), s16.reshape(4, h, cols)

    def chip_sum(pr, gt, tag):
        own = lax.dynamic_index_in_dim(pr, me, axis=0, keepdims=False)
        (q,), _ = rowwise(sum4_fn, [own, gt[0], gt[1], gt[2]], [], [(own.shape[1], F32)], name="sum_chips_" + tag,
                          tm=SUM_TILE)
        return q

    pair_rest, pair_rest_b = chip_pair_sum(grads_b, "rest")
    gw_in_mine, (got_rest,) = matmul_tn(h0b, dproj, name="g_w_in", rider=scatter_rider([pair_rest_b]))
    gfull["w_in"] = jnp.concatenate([gw_in_mine[:, mine[nm][0]:mine[nm][0] + orig[nm][1]]
                                     for nm in ["hq", "hf", "hi", "hg", "fq", "fk", "fv", "ff", "ga", "gb"]], axis=1)
    grads_a = jnp.stack([pack_a(chip_parts("w_in", s)) for s in range(4)])
    pair_in, pair_in_b = chip_pair_sum(grads_a, "w_in")
    def ln0_bwd_post(dh0_in, aux, vecs):
        dx, dg, db = _ln_bwd(aux[1], dh0_in + ALPHA * aux[0], vecs[0])
        return [dx], [dg, db]
    ((dx,), (g_ln0_g, g_ln0_b)), (got_in,) = matmul_nn(
        dproj, win_mine, transpose_rhs=True, name="d_in_proj_ln0", tm=fused_tm, post=ln0_bwd_post,
        post_aux=[dz1, x2], post_vecs=[vec(ln0_g)], post_outs=[F32], post_accs=[d, d],
        rider=scatter_rider([pair_in_b]))
    q_half = [chip_sum(pair_in, got_in, "w_in"), chip_sum(pair_rest, got_rest, "rest")]
    q_other = swap_cores(q_half, name="swap_halves")
    g_a, g_b = [jnp.concatenate([jnp.where(core == 0, mine_, other), jnp.where(core == 0, other, mine_)], axis=0)
                for mine_, other in zip(q_half, q_other)]
    g_shards = unpack_b(g_b, lay)
    g_shards["w_in"] = g_a[:, :in_cols]

    assert d == PACK_W and 2 * hw == PACK_W and fh <= LANES
    small = allreduce_small(jnp.concatenate(
        [g_ln0_g, g_ln0_b, g_ln1_g, g_ln1_b, g_ln2_g, g_ln2_b, jnp.concatenate([g_norm_g, g_lb], axis=1),
         jnp.concatenate([g_fb, loss_part[:, LANES:]], axis=1)], axis=0), name="allreduce_small")
    loss = small[7, LANES]

    small_w = [vec(ln0_g), vec(ln0_b), ln1_g, ln1_b, ln2_g, ln2_b, hg_lb, hg_norm_g, fox_fb]
    small_m = [vec(m_ln0_g), vec(m_ln0_b), m_ln1_g, m_ln1_b, m_ln2_g, m_ln2_b, m_hg_lb, m_hg_norm_g, m_fox_fb]
    small_v = [vec(v_ln0_g), vec(v_ln0_b), v_ln1_g, v_ln1_b, v_ln2_g, v_ln2_b, v_hg_lb, v_hg_norm_g, v_fox_fb]
    small_out = adamw_small(small, probs[0:1], small_w, small_m, small_v, name="adamw_small")
    small_shapes = [ln0_g.shape, ln0_b.shape, ln1_g.shape, ln1_b.shape, ln2_g.shape, ln2_b.shape, hg_lb.shape,
                    hg_norm_g.shape, fox_fb.shape]
    sg_out, sd_out, sm_out, sv_out = [[a.reshape(shp) for a, shp in zip(small_out[9 * k:9 * k + 9], small_shapes)]
                                      for k in range(4)]

    big_out = {}
    for nm in names:
        delta, m2, v2 = adamw(big[nm], g_shards[nm], big_m[nm], big_v[nm], name="adamw_" + nm)
        big_out[nm] = (g_shards[nm][None], delta[None], m2[None], v2[None])

    def ordered(k):
        sm_ = [sg_out, sd_out, sm_out, sv_out][k]
        bg = lambda nm: big_out[nm][k]
        return [sm_[0], sm_[1], bg("w_in"), sm_[6], sm_[7], sm_[8], bg("w_a"), bg("w_b"), bg("w_o"), sm_[2], sm_[3],
                bg("w_ff1"), bg("w_ff2"), bg("w_pg"), bg("w_p"), sm_[4], sm_[5]]
    grad_x = dx.reshape(n_seq, seq, d)
    return (loss, grad_x, *ordered(0), *ordered(1), *ordered(2), *ordered(3))
```

```python
import functools
from typing import NamedTuple, Optional

import numpy as np
import jax
import jax.numpy as jnp
from jax import lax
from jax.experimental import pallas as pl
from jax.experimental.pallas import tpu as pltpu

F32 = jnp.float32
BF16 = jnp.bfloat16
MESH = pl.DeviceIdType.MESH

VMEM_LIMIT_BYTES = 48 * 1024 * 1024
LANES = 128
HG_HEADS = 4
HG_DIM = 128
HG_BLK = 16
HG_TILE = 256
HG_SLOTS = 4
FOX_HDIM = 64
FOX_AUG = 128
FOX_TQ = 1024
FOX_FWD_HEADS = 1
LN_EPS = 1e-5
RMS_EPS = 1e-6
DEPTH = 1
ALPHA = (2.0 * DEPTH) ** 0.25
ADAM_LR, ADAM_B1, ADAM_B2, ADAM_EPS, ADAM_WD, ADAM_STEP = 0.001, 0.9, 0.999, 1e-08, 0.01, 10
NEG_INF = -1e30


def _cparams(sem):
    return pltpu.CompilerParams(dimension_semantics=sem, vmem_limit_bytes=VMEM_LIMIT_BYTES)


def _tile(n, cap):
    if n <= cap:
        return n
    best = None
    for t in range(LANES, cap + 1, LANES):
        if n % t == 0:
            best = t
    assert best is not None, (n, cap)
    return best


class WView(NamedTuple):
    arr: jax.Array
    r0: int
    c0: int
    k: int
    n: int
    split: Optional[int]


def matmul_nn(a, w, *, name, transpose_rhs=False, out_dtype=F32, epilogue=None, aux=None, tm=1024, rider=None,
              post=None, post_aux=(), post_vecs=(), post_outs=(), post_accs=()):
    wv = w if isinstance(w, WView) else WView(w[None], 0, 0, w.shape[0], w.shape[1], None)
    rows_s = wv.k // 4 if wv.split == 0 else wv.k
    cols_s = wv.n // 4 if wv.split == 1 else wv.n
    tr, tc = _tile(rows_s, 1152), _tile(cols_s, 1152)
    assert wv.r0 % tr == 0 and wv.c0 % tc == 0
    T, K = a.shape
    N, tn, tk = (wv.k, tr, tc) if transpose_rhs else (wv.n, tc, tr)
    assert K == (wv.n if transpose_rhs else wv.k)
    tm = min(tm, T)
    assert T % tm == 0
    nk = K // tk

    def w_block(ri, ci):
        if wv.split == 0:
            return (ri * tr) // rows_s, (wv.r0 + (ri * tr) % rows_s) // tr, wv.c0 // tc + ci
        if wv.split == 1:
            return (ci * tc) // cols_s, wv.r0 // tr + ri, (wv.c0 + (ci * tc) % cols_s) // tc
        return 0, wv.r0 // tr + ri, wv.c0 // tc + ci

    fused = post is not None
    assert not fused or N == tn
    aux_list = list(post_aux) if fused else ([aux] if aux is not None else [])
    aux_list = [x if isinstance(x, tuple) else (x, 0, x.shape[1]) for x in aux_list]
    vec_list = list(post_vecs)
    out_dtypes = list(post_outs) if fused else [out_dtype]
    n_aux, n_vec, n_out, n_acc = len(aux_list), len(vec_list), len(out_dtypes), len(post_accs)

    def body(*refs):
        a_ref, w_ref = refs[:2]
        aux_refs = refs[2:2 + n_aux]
        vec_refs = refs[2 + n_aux:2 + n_aux + n_vec]
        out_refs = refs[2 + n_aux + n_vec:2 + n_aux + n_vec + n_out]
        sum_refs = refs[2 + n_aux + n_vec + n_out:2 + n_aux + n_vec + n_out + n_acc]
        acc_ref = refs[-1]
        m, k = pl.program_id(1), pl.program_id(2)
        if transpose_rhs:
            part = lax.dot_general(a_ref[...], w_ref[...], (((1,), (1,)), ((), ())), preferred_element_type=F32)
        else:
            part = jnp.dot(a_ref[...], w_ref[...], preferred_element_type=F32)

        def write(res):
            if not fused:
                if epilogue is not None:
                    res = epilogue(res) if not aux_refs else epilogue(res, aux_refs[0][...])
                out_refs[0][...] = res.astype(out_dtype)
                return
            outs, sums = post(res, [r[...] for r in aux_refs], [v[...] for v in vec_refs])
            assert len(outs) == n_out and len(sums) == n_acc
            for r, val in zip(out_refs, outs):
                r[...] = val.astype(r.dtype)
            for r, val in zip(sum_refs, sums):
                def first_rows(r=r, val=val):
                    r[...] = val

                def later_rows(r=r, val=val):
                    r[...] += val
                pl.when(m == 0)(first_rows)
                pl.when(m > 0)(later_rows)

        if nk == 1:
            write(part)
        else:
            @pl.when(k == 0)
            def _():
                acc_ref[...] = part

            @pl.when(k > 0)
            def _():
                acc_ref[...] += part

            @pl.when(k == nk - 1)
            def _():
                write(acc_ref[...])

    w_index = (lambda n, m, k: w_block(n, k)) if transpose_rhs else (lambda n, m, k: w_block(k, n))
    in_specs = [pl.BlockSpec((tm, tk), lambda n, m, k: (m, k)),
                pl.BlockSpec((None, tr, tc), w_index)]
    args = [a, wv.arr]
    for arr, off, width in aux_list:
        assert width == N and off % tn == 0
        in_specs.append(pl.BlockSpec((tm, tn), functools.partial(lambda n, m, k, blk: (m, blk + n), blk=off // tn)))
        args.append(arr)
    for v in vec_list:
        in_specs.append(pl.BlockSpec(v.shape, lambda n, m, k: (0, 0)))
        args.append(v)
    out_specs = [pl.BlockSpec((tm, tn), lambda n, m, k: (m, n)) for _ in out_dtypes]
    out_specs += [pl.BlockSpec((1, tn), lambda n, m, k: (0, 0)) for _ in post_accs]
    out_shape = [jax.ShapeDtypeStruct((T, N), dt) for dt in out_dtypes]
    out_shape += [jax.ShapeDtypeStruct((1, N), F32) for _ in post_accs]
    scratch = [pltpu.VMEM((tm, tn) if nk > 1 else (8, LANES), F32)]
    grid = (N // tn, T // tm, nk)
    sem = ("arbitrary",) * 3 if (n_acc or rider is not None) else ("parallel", "parallel", "arbitrary")
    params = pltpu.CompilerParams(dimension_semantics=sem, vmem_limit_bytes=VMEM_LIMIT_BYTES,
                                  has_side_effects=rider is not None)
    if rider is not None:
        r_in, r_out, r_sems = rider.specs()
        body = rider.wrap(body, len(in_specs), len(out_specs), 3)
        in_specs, out_specs, out_shape = in_specs + r_in, out_specs + r_out, out_shape + rider.out_shape
        scratch, args = scratch + r_sems, args + list(rider.ins)
    res = pl.pallas_call(body, name=name, grid=grid, in_specs=in_specs, out_specs=out_specs, out_shape=out_shape,
                         scratch_shapes=scratch, compiler_params=params)(*args)
    main = (list(res[:n_out]), list(res[n_out:n_out + n_acc])) if fused else res[0]
    return main if rider is None else (main, list(res[n_out + n_acc:]))


def matmul_tn(a, b, *, name, tk=1024, rider=None, into=None):
    T, M = a.shape
    T2, N = b.shape
    tk = min(tk, T)
    assert T == T2 and T % tk == 0
    if into is not None:
        assert rider is None
        buf, r0, c0, split = into
        rows_s, cols_s = (M // 4, N) if split == 0 else (M, N // 4)
        tm, tn = _tile(rows_s, 1024), _tile(cols_s, 1152)
        assert r0 % tm == 0 and c0 % tn == 0

        def part_block(m, n, k):
            if split == 0:
                return (m * tm) // rows_s, (r0 + (m * tm) % rows_s) // tm, c0 // tn + n
            return (n * tn) // cols_s, r0 // tm + m, (c0 + (n * tn) % cols_s) // tn

        def body_into(a_ref, b_ref, buf_ref, o_ref):
            k = pl.program_id(2)
            part = lax.dot_general(a_ref[...], b_ref[...], (((0,), (0,)), ((), ())), preferred_element_type=F32)

            @pl.when(k == 0)
            def _():
                o_ref[...] = part

            @pl.when(k > 0)
            def _():
                o_ref[...] += part

        return pl.pallas_call(
            body_into, name=name, grid=(M // tm, N // tn, T // tk),
            in_specs=[pl.BlockSpec((tk, tm), lambda m, n, k: (k, m)), pl.BlockSpec((tk, tn), lambda m, n, k: (k, n)),
                      pl.BlockSpec(memory_space=pl.ANY)],
            out_specs=pl.BlockSpec((None, tm, tn), part_block),
            out_shape=jax.ShapeDtypeStruct(buf.shape, buf.dtype), input_output_aliases={2: 0},
            compiler_params=_cparams(("parallel", "parallel", "arbitrary")))(a, b, buf)
    tm = _tile(M, 1024)
    tn = _tile(N, 1152)

    def body(a_ref, b_ref, o_ref):
        k = pl.program_id(2)
        part = lax.dot_general(a_ref[...], b_ref[...], (((0,), (0,)), ((), ())), preferred_element_type=F32)

        @pl.when(k == 0)
        def _():
            o_ref[...] = part

        @pl.when(k > 0)
        def _():
            o_ref[...] += part

    in_specs = [pl.BlockSpec((tk, tm), lambda m, n, k: (k, m)), pl.BlockSpec((tk, tn), lambda m, n, k: (k, n))]
    out_specs = [pl.BlockSpec((tm, tn), lambda m, n, k: (m, n))]
    out_shape = [jax.ShapeDtypeStruct((M, N), F32)]
    grid = (M // tm, N // tn, T // tk)
    if rider is None:
        return pl.pallas_call(body, name=name, grid=grid, in_specs=in_specs, out_specs=out_specs, out_shape=out_shape,
                              compiler_params=_cparams(("parallel", "parallel", "arbitrary")))(a, b)[0]
    r_in, r_out, r_sems = rider.specs()
    res = pl.pallas_call(
        rider.wrap(body, 2, 1, 3), name=name, grid=grid, in_specs=in_specs + r_in, out_specs=out_specs + r_out,
        out_shape=out_shape + rider.out_shape, scratch_shapes=r_sems,
        compiler_params=pltpu.CompilerParams(dimension_semantics=("arbitrary",) * 3,
                                             vmem_limit_bytes=VMEM_LIMIT_BYTES, has_side_effects=True),
    )(a, b, *rider.ins)
    return res[0], list(res[1:])


def rowwise(fn, rows, vecs, outs, accs=(), *, name, tm=512, rider=None):
    rows = [r if isinstance(r, tuple) else (r, 0, r.shape[1]) for r in rows]
    T = rows[0][0].shape[0]
    tm = min(tm, T)
    assert T % tm == 0
    n_rows, n_vecs, n_outs, n_accs = len(rows), len(vecs), len(outs), len(accs)

    def body(*refs):
        row_refs = refs[:n_rows]
        vec_refs = refs[n_rows:n_rows + n_vecs]
        out_refs = refs[n_rows + n_vecs:n_rows + n_vecs + n_outs]
        acc_refs = refs[n_rows + n_vecs + n_outs:]
        out_vals, acc_vals = fn([r[...] for r in row_refs], [v[...] for v in vec_refs])
        assert len(out_vals) == n_outs and len(acc_vals) == n_accs
        for r, val in zip(out_refs, out_vals):
            r[...] = val.astype(r.dtype)
        if n_accs:
            i = pl.program_id(0)

            @pl.when(i == 0)
            def _():
                for r in acc_refs:
                    r[...] = jnp.zeros_like(r)

            for r, val in zip(acc_refs, acc_vals):
                r[...] += val

    in_specs = []
    for arr, off, width in rows:
        assert off % width == 0
        in_specs.append(pl.BlockSpec((tm, width), functools.partial(lambda i, blk: (i, blk), blk=off // width)))
    for v in vecs:
        in_specs.append(pl.BlockSpec(v.shape, lambda i: (0, 0)))
    out_specs = [pl.BlockSpec((tm, w), lambda i: (i, 0)) for w, _ in outs]
    out_specs += [pl.BlockSpec((1, w), lambda i: (0, 0)) for w in accs]
    out_shape = [jax.ShapeDtypeStruct((T, w), dt) for w, dt in outs]
    out_shape += [jax.ShapeDtypeStruct((1, w), F32) for w in accs]
    args = [r[0] for r in rows] + list(vecs)
    if rider is None:
        res = pl.pallas_call(body, name=name, grid=(T // tm,), in_specs=in_specs, out_specs=out_specs,
                             out_shape=out_shape,
                             compiler_params=_cparams(("arbitrary",) if n_accs else ("parallel",)))(*args)
        return res[:n_outs], res[n_outs:]
    r_in, r_out, r_sems = rider.specs()
    res = pl.pallas_call(
        rider.wrap(body, len(in_specs), len(out_specs), 1), name=name, grid=(T // tm,), in_specs=in_specs + r_in,
        out_specs=out_specs + r_out, out_shape=out_shape + rider.out_shape, scratch_shapes=r_sems,
        compiler_params=pltpu.CompilerParams(dimension_semantics=("arbitrary",), vmem_limit_bytes=VMEM_LIMIT_BYTES,
                                             has_side_effects=True),
    )(*args, *rider.ins)
    return res[:n_outs], res[n_outs:n_outs + n_accs], list(res[n_outs + n_accs:])


def _colsum(x):
    return jnp.sum(x, axis=0, keepdims=True)


def _sigmoid(x):
    return 1.0 / (1.0 + jnp.exp(-x))


def _ln_stats(z):
    mu = jnp.mean(z, axis=-1, keepdims=True)
    zc = z - mu
    var = jnp.mean(zc * zc, axis=-1, keepdims=True)
    return zc * lax.rsqrt(var + LN_EPS)


def _ln_bwd(zhat_src, dy, g):
    mu = jnp.mean(zhat_src, axis=-1, keepdims=True)
    zc = zhat_src - mu
    var = jnp.mean(zc * zc, axis=-1, keepdims=True)
    rstd = lax.rsqrt(var + LN_EPS)
    zh = zc * rstd
    dzh = dy * g
    dz = rstd * (dzh - jnp.mean(dzh, axis=-1, keepdims=True) - zh * jnp.mean(dzh * zh, axis=-1, keepdims=True))
    return dz, _colsum(dy * zh), _colsum(dy)


def _hg_constants():
    r = np.arange(HG_TILE)
    same = (r[:, None] // HG_BLK) == (r[None, :] // HG_BLK)
    lower = (same & (r[None, :] <= r[:, None])).astype(np.float32)
    upper = (same & (r[None, :] >= r[:, None])).astype(np.float32)
    total = same.astype(np.float32)
    c = np.arange(2 * HG_DIM)
    bd = ((c[:, None] // HG_DIM) == (c[None, :] // HG_DIM)).astype(np.float32)
    pair_t = np.array([t for t, _ in _HG_PAIRS])
    pair_s = np.array([s for _, s in _HG_PAIRS])
    sel_t = (pair_t[None, :] == np.arange(HG_BLK)[:, None]).astype(np.float32)
    sel_s = (pair_s[None, :] == np.arange(HG_BLK)[:, None]).astype(np.float32)
    as_bf = lambda m: jnp.asarray(m, dtype=BF16)
    return as_bf(lower), as_bf(upper), as_bf(total), as_bf(bd), as_bf(sel_t), as_bf(sel_s)


_HG_HALF = HG_BLK // 2
_HG_PAIRS = ([(t, s) for t in range(_HG_HALF, HG_BLK) for s in range(HG_BLK)]
             + [(t, s) for t in range(_HG_HALF) for s in range(_HG_HALF)])
HG_STACK = len(_HG_PAIRS)
_HG_SLABS = ([((t - _HG_HALF) * HG_BLK, (t,), HG_BLK) for t in range(_HG_HALF, HG_BLK)]
             + [(_HG_HALF * HG_BLK + t * _HG_HALF, (t, t + 1), _HG_HALF) for t in range(0, _HG_HALF, 2)])


def _stack_by_s(x):
    return jnp.concatenate([x] * _HG_HALF + [x[:_HG_HALF]] * _HG_HALF, axis=0)


def _stack_by_t(x):
    w = x.shape[1]
    return jnp.concatenate([jnp.broadcast_to(x[t:t + 1], (HG_BLK, w)) for t in range(_HG_HALF, HG_BLK)]
                           + [jnp.broadcast_to(x[t:t + 1], (_HG_HALF, w)) for t in range(_HG_HALF)], axis=0)


def _keep_bf16_bits(x):
    bits = lax.bitcast_convert_type(x, jnp.int32) & jnp.int32(-65536)
    return lax.bitcast_convert_type(bits, F32)


def _head_sums(stack_ref, slot, bd):
    pair = bd.shape[0]
    return jnp.concatenate([jnp.dot(stack_ref[slot, :, c0:c0 + pair], bd, preferred_element_type=F32)
                            for c0 in range(0, stack_ref.shape[2], pair)], axis=1)


def _split3(x):
    hi = _keep_bf16_bits(x)
    r1 = x - hi
    mid = _keep_bf16_bits(r1)
    lo = _keep_bf16_bits(r1 - mid)
    return hi.astype(BF16), mid.astype(BF16), lo.astype(BF16)


def _dot3(m01, x):
    hi, mid, lo = _split3(x)
    d = lambda p: jnp.dot(m01, p, preferred_element_type=F32)
    return (d(lo) + d(mid)) + d(hi)


def _hg_prologue(hq, hf, lb, lower, total):
    sq = _sigmoid(hq)
    q = hq * sq
    sg = _sigmoid(hf)
    f = lb + (1.0 - lb) * sg
    g = jnp.log(f)
    k = 1.0 - f
    b = _dot3(lower, g)
    bl = _dot3(total, g)
    return q, k, f, sg, sq, b, bl


def _stack16(fn):
    return [fn(t) for t in range(HG_BLK)]


def hgrn2_fwd(proj, offs, lb, n_seq, seq, *, name, rider=None):
    T = n_seq * seq
    W = HG_HEADS * HG_DIM
    n_tiles = seq // HG_TILE
    nb = HG_TILE // HG_BLK
    lower, _, total, bd, sel_t, _ = _hg_constants()

    def body(hq_ref, hf_ref, hi_ref, lb_ref, lower_ref, total_ref, bd_ref, selt_ref,
             o_ref, st_out_ref,
             st_ref, q_s, k_s, v_s, b_s, qt_s, kt_s, d_s, p_s):
        @pl.when(pl.program_id(1) == 0)
        def _():
            st_ref[...] = jnp.zeros_like(st_ref)

        q, k, _, _, _, b, bl = _hg_prologue(hq_ref[...], hf_ref[...], lb_ref[...], lower_ref[...], total_ref[...])
        q_s[...] = q
        k_s[...] = k
        v_s[...] = hi_ref[...]
        b_s[...] = b
        qt_s[...] = q * jnp.exp(b)
        kt_s[...] = k * jnp.exp(jnp.minimum(bl - b, 0.0))
        d_s[...] = jnp.exp(bl)
        rowi = lax.broadcasted_iota(jnp.int32, (HG_BLK, W), 0)

        def block(i, slot):
            r0 = pl.multiple_of(i * HG_BLK, HG_BLK)
            rows = pl.ds(r0, HG_BLK)
            qi, ki, vi, bi = q_s[rows, :], k_s[rows, :], v_s[rows, :], b_s[rows, :]
            for off, ts, n in _HG_SLABS:
                slab = [jnp.where(rowi[:n] <= t, jnp.exp(jnp.minimum(bi[t:t + 1, :] - bi[:n], 0.0)), 0.0)
                        * qi[t:t + 1, :] * ki[:n] for t in ts]
                p_s[slot, pl.ds(off, HG_BLK), :] = jnp.concatenate(slab, axis=0).astype(BF16)
            a_b = _head_sums(p_s, slot, bd_ref[...])
            o_blk = jnp.dot(selt_ref[...], (a_b * _stack_by_s(vi)).astype(BF16), preferred_element_type=F32)
            qti, kti, di = qt_s[rows, :], kt_s[rows, :], d_s[rows, :]
            outs = []
            for h in range(HG_HEADS):
                hs = slice(h * HG_DIM, (h + 1) * HG_DIM)
                st_h = st_ref[hs, :]
                st_out_ref[i, hs, :] = st_h
                outs.append(lax.dot_general(qti[:, hs].astype(BF16), st_h.astype(BF16),
                                            (((1,), (1,)), ((), ())), preferred_element_type=F32))
                upd = lax.dot_general(vi[:, hs].astype(BF16), kti[:, hs].astype(BF16),
                                      (((0,), (0,)), ((), ())), preferred_element_type=F32)
                st_ref[hs, :] = st_h * di[0:1, hs] + upd
            o_ref[rows, :] = o_blk + jnp.concatenate(outs, axis=1)

        def some_blocks(jj, carry):
            for slot in range(HG_SLOTS):
                block(HG_SLOTS * jj + slot, slot)
            return carry

        lax.fori_loop(0, nb // HG_SLOTS, some_blocks, 0)

    col = lambda off: functools.partial(lambda s, t, blk: (s * n_tiles + t, blk), blk=off // W)
    const = lambda m: pl.BlockSpec(m.shape, lambda s, t: (0, 0))
    tile_f32 = pltpu.VMEM((HG_TILE, W), F32)
    in_specs = [pl.BlockSpec((HG_TILE, W), col(offs[0])), pl.BlockSpec((HG_TILE, W), col(offs[1])),
                pl.BlockSpec((HG_TILE, W), col(offs[2])), const(lb), const(lower), const(total), const(bd),
                const(sel_t)]
    out_specs = [pl.BlockSpec((HG_TILE, W), lambda s, t: (s * n_tiles + t, 0)),
                 pl.BlockSpec((nb, W, HG_DIM), lambda s, t: (s * n_tiles + t, 0, 0))]
    out_shape = [jax.ShapeDtypeStruct((T, W), F32), jax.ShapeDtypeStruct((T // HG_BLK, W, HG_DIM), F32)]
    scratch = [pltpu.VMEM((W, HG_DIM), F32)] + [tile_f32] * 7 + [pltpu.VMEM((HG_SLOTS, HG_STACK, W), BF16)]
    args = [proj, proj, proj, lb, lower, total, bd, sel_t]
    params = _cparams(("arbitrary", "arbitrary"))
    if rider is not None:
        r_in, r_out, r_sems = rider.specs()
        body = rider.wrap(body, len(in_specs), len(out_specs), 2)
        in_specs, out_specs, out_shape = in_specs + r_in, out_specs + r_out, out_shape + rider.out_shape
        scratch, args = scratch + r_sems, args + rider.ins
        params = pltpu.CompilerParams(dimension_semantics=("arbitrary", "arbitrary"),
                                      vmem_limit_bytes=VMEM_LIMIT_BYTES, has_side_effects=True)
    res = pl.pallas_call(body, name=name, grid=(n_seq, n_tiles), in_specs=in_specs, out_specs=out_specs,
                         out_shape=out_shape, scratch_shapes=scratch, compiler_params=params)(*args)
    return res[0], res[1], list(res[2:])


def hgrn2_bwd(proj, offs, lb, do, states, n_seq, seq, *, name):
    T = n_seq * seq
    W = HG_HEADS * HG_DIM
    n_tiles = seq // HG_TILE
    nb = HG_TILE // HG_BLK
    lower, upper, total, bd, sel_t, sel_s = _hg_constants()

    def body(hq_ref, hf_ref, hi_ref, do_ref, st_in_ref, lb_ref, lower_ref, upper_ref, total_ref, bd_ref,
             selt_ref, sels_ref,
             dhq_ref, dhf_ref, dhi_ref, dlb_ref,
             dst_ref, q_s, k_s, v_s, b_s, qt_s, kt_s, d_s, eb_s, ekb_s, dq_s, dk_s, db_s, dv_s,
             p_s, e_s, w_s):
        first = jnp.logical_and(pl.program_id(0) == 0, pl.program_id(1) == 0)

        @pl.when(first)
        def _():
            dlb_ref[...] = jnp.zeros_like(dlb_ref)

        @pl.when(pl.program_id(1) == 0)
        def _():
            dst_ref[...] = jnp.zeros_like(dst_ref)

        hq, lbv = hq_ref[...], lb_ref[...]
        q, k, f, sg, sq, b, bl = _hg_prologue(hq, hf_ref[...], lbv, lower_ref[...], total_ref[...])
        eb = jnp.exp(b)
        ekb = jnp.exp(jnp.minimum(bl - b, 0.0))
        q_s[...] = q
        k_s[...] = k
        v_s[...] = hi_ref[...]
        b_s[...] = b
        eb_s[...] = eb
        ekb_s[...] = ekb
        qt_s[...] = q * eb
        kt_s[...] = k * ekb
        d_s[...] = jnp.exp(bl)
        rowi = lax.broadcasted_iota(jnp.int32, (HG_BLK, W), 0)
        last_row = rowi == HG_BLK - 1

        def block(i, slot):
            r0 = pl.multiple_of(i * HG_BLK, HG_BLK)
            rows = pl.ds(r0, HG_BLK)
            qi, ki, vi, bi, doi = q_s[rows, :], k_s[rows, :], v_s[rows, :], b_s[rows, :], do_ref[rows, :]
            for off, ts, n in _HG_SLABS:
                es = [jnp.where(rowi[:n] <= t, jnp.exp(jnp.minimum(bi[t:t + 1, :] - bi[:n], 0.0)), 0.0) for t in ts]
                sl = pl.ds(off, HG_BLK)
                e_s[slot, sl, :] = jnp.concatenate(es, axis=0)
                p_s[slot, sl, :] = jnp.concatenate([e * qi[t:t + 1, :] * ki[:n] for e, t in zip(es, ts)],
                                                   axis=0).astype(BF16)
                w_s[slot, sl, :] = jnp.concatenate([doi[t:t + 1, :] * vi[:n] for t in ts], axis=0).astype(BF16)
            a_b = _head_sums(p_s, slot, bd_ref[...])
            da_b = _head_sums(w_s, slot, bd_ref[...])
            x = da_b * e_s[slot]
            dq_in = jnp.dot(selt_ref[...], (x * _stack_by_s(ki)).astype(BF16), preferred_element_type=F32)
            dk_in = jnp.dot(sels_ref[...], (x * _stack_by_t(qi)).astype(BF16), preferred_element_type=F32)
            dv_in = jnp.dot(sels_ref[...], (a_b * _stack_by_t(doi)).astype(BF16), preferred_element_type=F32)
            qti, kti, di = qt_s[rows, :], kt_s[rows, :], d_s[rows, :]
            dqt, dkt, dvt, dd = [], [], [], []
            for h in range(HG_HEADS):
                hs = slice(h * HG_DIM, (h + 1) * HG_DIM)
                st_h = st_in_ref[i, hs, :]
                dst_h = dst_ref[hs, :]
                do_h, v_h = doi[:, hs].astype(BF16), vi[:, hs].astype(BF16)
                dst_b = dst_h.astype(BF16)
                dqt.append(jnp.dot(do_h, st_h.astype(BF16), preferred_element_type=F32))
                dkt.append(jnp.dot(v_h, dst_b, preferred_element_type=F32))
                dvt.append(lax.dot_general(kti[:, hs].astype(BF16), dst_b, (((1,), (1,)), ((), ())),
                                           preferred_element_type=F32))
                dd.append(jnp.sum(dst_h * st_h, axis=0, keepdims=True))
                upd = lax.dot_general(do_h, qti[:, hs].astype(BF16), (((0,), (0,)), ((), ())),
                                      preferred_element_type=F32)
                dst_ref[hs, :] = dst_h * di[0:1, hs] + upd
            dqt = jnp.concatenate(dqt, axis=1)
            dkt = jnp.concatenate(dkt, axis=1)
            dvt = jnp.concatenate(dvt, axis=1)
            dd = jnp.concatenate(dd, axis=1)
            dbl = jnp.sum(dkt * kti, axis=0, keepdims=True) + dd * di[0:1, :]
            db = qi * dq_in - ki * dk_in + dqt * qti - dkt * kti
            db_s[rows, :] = db + jnp.where(last_row, dbl, 0.0)
            dq_s[rows, :] = dq_in + dqt * eb_s[rows, :]
            dk_s[rows, :] = dk_in + dkt * ekb_s[rows, :]
            dv_s[rows, :] = dv_in + dvt

        def some_blocks(jj, carry):
            for slot in range(HG_SLOTS):
                block(nb - 1 - slot - HG_SLOTS * jj, slot)
            return carry

        lax.fori_loop(0, nb // HG_SLOTS, some_blocks, 0)

        dg = _dot3(upper_ref[...], db_s[...])
        dhq_ref[...] = (dq_s[...] * (sq * (1.0 + hq * (1.0 - sq)))).astype(dhq_ref.dtype)
        df = dg / f - dk_s[...]
        dhf_ref[...] = (df * (1.0 - lbv) * (sg * (1.0 - sg))).astype(dhf_ref.dtype)
        dhi_ref[...] = dv_s[...].astype(dhi_ref.dtype)
        dlb_ref[...] += _colsum(df * (1.0 - sg))

    rev = lambda s, t: s * n_tiles + (n_tiles - 1 - t)
    col = lambda off: functools.partial(lambda s, t, blk: (rev(s, t), blk), blk=off // W)
    const = lambda m: pl.BlockSpec(m.shape, lambda s, t: (0, 0))
    row = pl.BlockSpec((HG_TILE, W), lambda s, t: (rev(s, t), 0))
    tile_f32 = pltpu.VMEM((HG_TILE, W), F32)
    n2 = HG_STACK
    return pl.pallas_call(
        body, name=name,
        grid=(n_seq, n_tiles),
        in_specs=[pl.BlockSpec((HG_TILE, W), col(offs[0])), pl.BlockSpec((HG_TILE, W), col(offs[1])),
                  pl.BlockSpec((HG_TILE, W), col(offs[2])), row,
                  pl.BlockSpec((nb, W, HG_DIM), lambda s, t: (rev(s, t), 0, 0)),
                  const(lb), const(lower), const(upper), const(total), const(bd), const(sel_t), const(sel_s)],
        out_specs=[row, row, row, pl.BlockSpec((1, W), lambda s, t: (0, 0))],
        out_shape=[jax.ShapeDtypeStruct((T, W), BF16)] * 3 + [jax.ShapeDtypeStruct((1, W), F32)],
        scratch_shapes=[pltpu.VMEM((W, HG_DIM), F32)] + [tile_f32] * 13
                       + [pltpu.VMEM((HG_SLOTS, n2, W), BF16), pltpu.VMEM((HG_SLOTS, n2, W), F32),
                          pltpu.VMEM((HG_SLOTS, n2, W), BF16)],
        compiler_params=_cparams(("arbitrary", "arbitrary")),
    )(proj, proj, proj, do, states, lb, lower, upper, total, bd, sel_t, sel_s)


def _diag_mask(tq):
    return lax.broadcasted_iota(jnp.int32, (tq, tq), 1) <= lax.broadcasted_iota(jnp.int32, (tq, tq), 0)


def _qk(q, k):
    return lax.dot_general(q, k, (((1,), (1,)), ((), ())), preferred_element_type=F32)


def _causal_pairs(n, sweeps=1, by_key=False):
    if by_key:
        rows = [(i, j, 0) for j in range(n) for i in range(j, n)]
    else:
        rows = [(i, j, s) for i in range(n) for s in range(sweeps) for j in range(i + 1)]
    return tuple(jnp.asarray(np.array([r[c] for r in rows], np.int32)) for c in range(3))


def _fox_placement(fh):
    hw, wa = fh * FOX_HDIM, fh * FOX_AUG
    pq, pk = np.zeros((hw, wa), np.float32), np.zeros((hw, wa), np.float32)
    aq, ak = np.zeros((3 * LANES, wa), np.float32), np.zeros((3 * LANES, wa), np.float32)
    oq, ok = np.zeros((1, wa), np.float32), np.zeros((1, wa), np.float32)
    for h in range(fh):
        src, dst = np.arange(h * FOX_HDIM, (h + 1) * FOX_HDIM), np.arange(h * FOX_AUG, h * FOX_AUG + FOX_HDIM)
        pq[src, dst] = FOX_HDIM ** -0.5
        pk[src, dst] = 1.0
        gate = h * FOX_AUG + FOX_HDIM
        for r in range(3):
            aq[r * LANES + h, gate + r] = 1.0
            ak[r * LANES + h, gate + 3 + r] = -1.0
        oq[0, gate + 3:gate + 6] = 1.0
        ok[0, gate:gate + 3] = 1.0
    bf = lambda m: jnp.asarray(m, dtype=BF16)
    return {"pq": bf(pq), "pk": bf(pk), "aq": bf(aq), "ak": bf(ak), "oq": jnp.asarray(oq), "ok": jnp.asarray(ok),
            "pqt": bf(pq.T), "pkt": bf(pk.T)}


def _fox_specs(tq, fh, heads=1):
    groups = fh // heads

    def spec(tab):
        return pl.BlockSpec((None, tq, heads * FOX_AUG), lambda b, t, *tabs: (b // groups, tabs[tab][t], b % groups))
    return spec(0), spec(1)


def fox_fwd(qa, ka, va, *, name):
    n_seq, S, width = qa.shape
    fh = width // FOX_AUG
    nh = FOX_FWD_HEADS
    BH = n_seq * fh // nh
    tq = min(FOX_TQ, S)
    itab, jtab, _ = _causal_pairs(S // tq)

    def body(itab_ref, jtab_ref, q_ref, k_ref, v_ref, o_ref, ox_ref, lse_ref, *scratch):
        t = pl.program_id(1)
        i, j = itab_ref[t], jtab_ref[t]
        per_head = [scratch[4 * h:4 * h + 4] for h in range(nh)]

        @pl.when(j == 0)
        def _():
            for m_s, l_s, acc_s, acc_lo_s in per_head:
                m_s[...] = jnp.full_like(m_s, NEG_INF)
                l_s[...] = jnp.zeros_like(l_s)
                acc_s[...] = jnp.zeros_like(acc_s)
                acc_lo_s[...] = jnp.zeros_like(acc_lo_s)

        def step(on_diagonal):
            for h, (m_s, l_s, acc_s, acc_lo_s) in enumerate(per_head):
                lanes = slice(h * FOX_AUG, (h + 1) * FOX_AUG)
                s = _qk(q_ref[:, lanes], k_ref[:, lanes])
                if on_diagonal:
                    s = jnp.where(_diag_mask(tq), s, NEG_INF)
                m_prev = m_s[...]
                m_new = jnp.maximum(m_prev, jnp.max(s, axis=-1, keepdims=True))
                alpha = jnp.exp(m_prev - m_new)
                p = jnp.exp(s - m_new[:, 0:1])
                p_hi = p.astype(BF16)
                p_lo = (p - p_hi.astype(F32)).astype(BF16)
                v = v_ref[:, lanes]
                l_s[...] = alpha * l_s[...] + jnp.sum(p, axis=-1, keepdims=True)
                acc_s[...] = alpha * acc_s[...] + jnp.dot(p_hi, v, preferred_element_type=F32)
                acc_lo_s[...] = alpha * acc_lo_s[...] + jnp.dot(p_lo, v, preferred_element_type=F32)
                m_s[...] = m_new

        @pl.when(j < i)
        def _():
            step(False)

        @pl.when(j == i)
        def _():
            step(True)
            for h, (m_s, l_s, acc_s, acc_lo_s) in enumerate(per_head):
                lanes = slice(h * FOX_AUG, (h + 1) * FOX_AUG)
                inv_l = 1.0 / l_s[...]
                o_ref[:, lanes] = (acc_s[...] * inv_l).astype(o_ref.dtype)
                ox_ref[:, lanes] = (acc_s[...] + acc_lo_s[...]) * inv_l
                lse_ref[:, lanes] = m_s[...] + jnp.log(l_s[...])

    qspec, kspec = _fox_specs(tq, fh, nh)
    wide = jax.ShapeDtypeStruct((n_seq, S, width), F32)
    return pl.pallas_call(
        body, name=name,
        grid_spec=pltpu.PrefetchScalarGridSpec(
            num_scalar_prefetch=2, grid=(BH, itab.shape[0]),
            in_specs=[qspec, kspec, kspec],
            out_specs=[qspec, qspec, qspec],
            scratch_shapes=[pltpu.VMEM((tq, LANES), F32)] * (4 * nh)),
        out_shape=[jax.ShapeDtypeStruct((n_seq, S, width), BF16), wide, wide],
        compiler_params=_cparams(("parallel", "arbitrary")),
    )(itab, jtab, qa, ka, va)


def _fox_ds(q, k, v, do, ox, lse, on_diagonal):
    s = _qk(q, k)
    if on_diagonal:
        s = jnp.where(_diag_mask(s.shape[0]), s, NEG_INF)
    p = jnp.exp(s - lse[:, 0:1])
    delta = jnp.sum(do.astype(F32) * ox, axis=-1, keepdims=True)
    return p, p * (_qk(do, v) - delta)


def fox_bwd(qa, ka, va, do, ox, lse, *, name):
    n_seq, S, width = qa.shape
    fh = width // FOX_AUG
    BH = n_seq * fh
    tq = min(FOX_TQ, S)
    itab, jtab, _ = _causal_pairs(S // tq)

    def body(itab_ref, jtab_ref, q_ref, k_ref, v_ref, do_ref, ox_ref, lse_ref, dq_ref, dk_ref, dv_ref, dsum_ref):
        t = pl.program_id(1)
        i, j = itab_ref[t], jtab_ref[t]

        @pl.when(t == 0)
        def _():
            dq_ref[...] = jnp.zeros_like(dq_ref)
            dk_ref[...] = jnp.zeros_like(dk_ref)
            dv_ref[...] = jnp.zeros_like(dv_ref)
            dsum_ref[...] = jnp.zeros_like(dsum_ref)

        q_rows = pl.ds(pl.multiple_of(i * tq, tq), tq)
        k_rows = pl.ds(pl.multiple_of(j * tq, tq), tq)

        def step(on_diagonal):
            q, k, do = q_ref[...], k_ref[...], do_ref[...]
            p, ds = _fox_ds(q, k, v_ref[...], do, ox_ref[...], lse_ref[...], on_diagonal)
            ds_b = ds.astype(BF16)
            tn = (((0,), (0,)), ((), ()))
            dq_ref[q_rows, :] += jnp.dot(ds_b, k, preferred_element_type=F32)
            dk_ref[k_rows, :] += lax.dot_general(ds_b, q, tn, preferred_element_type=F32)
            dv_ref[k_rows, :] += lax.dot_general(p.astype(BF16), do, tn, preferred_element_type=F32)
            dsum_ref[:, k_rows] += _colsum(ds)

        @pl.when(j < i)
        def _():
            step(False)

        @pl.when(j == i)
        def _():
            step(True)

    qspec, kspec = _fox_specs(tq, fh)
    whole = pl.BlockSpec((None, S, FOX_AUG), lambda b, t, it, jt: (b // fh, 0, b % fh))
    wide = jax.ShapeDtypeStruct((n_seq, S, width), F32)
    return pl.pallas_call(
        body, name=name,
        grid_spec=pltpu.PrefetchScalarGridSpec(
            num_scalar_prefetch=2, grid=(BH, itab.shape[0]),
            in_specs=[qspec, kspec, kspec, qspec, qspec, qspec],
            out_specs=[whole, whole, whole, pl.BlockSpec((None, 1, S), lambda b, t, it, jt: (b, 0, 0))]),
        out_shape=[wide, wide, wide, jax.ShapeDtypeStruct((BH, 1, S), F32)],
        compiler_params=_cparams(("parallel", "arbitrary")),
    )(itab, jtab, qa, ka, va, do, ox, lse)


def seq_cumsum(x, n_seq, seq, *, reverse, name):
    T, C = x.shape
    tb = min(256, seq)
    n = seq // tb
    r = np.arange(tb)
    tri = (r[None, :] >= r[:, None]) if reverse else (r[None, :] <= r[:, None])
    tri = jnp.asarray(tri.astype(np.float32), dtype=BF16)

    def body(x_ref, tri_ref, o_ref, carry_s):
        @pl.when(pl.program_id(1) == 0)
        def _():
            carry_s[...] = jnp.zeros_like(carry_s)

        xv = x_ref[...]
        o_ref[...] = _dot3(tri_ref[...], xv) + carry_s[...]
        carry_s[...] += _colsum(xv)

    blk = (lambda s, t: (s * n + (n - 1 - t), 0)) if reverse else (lambda s, t: (s * n + t, 0))
    return pl.pallas_call(
        body, name=name,
        grid=(n_seq, n),
        in_specs=[pl.BlockSpec((tb, C), blk), pl.BlockSpec((tb, tb), lambda s, t: (0, 0))],
        out_specs=pl.BlockSpec((tb, C), blk),
        out_shape=jax.ShapeDtypeStruct((T, C), F32),
        scratch_shapes=[pltpu.VMEM((1, C), F32)],
        compiler_params=_cparams(("arbitrary", "arbitrary")),
    )(x, tri)


def _place():
    return lax.axis_index("x"), lax.axis_index("y"), lax.axis_index("c")


def _other_chips(x, y):
    return [(1 - x, y), (x, 1 - y), (1 - x, 1 - y)]


def _hbm_call(body, ins, out_shape, n_sems, *, name):
    hbm = pl.BlockSpec(memory_space=pl.ANY)
    return pl.pallas_call(
        body, name=name,
        in_specs=[hbm] * len(ins), out_specs=[hbm] * len(out_shape), out_shape=out_shape,
        scratch_shapes=[pltpu.SemaphoreType.DMA((n_sems,)), pltpu.SemaphoreType.DMA((n_sems,)),
                        pltpu.SemaphoreType.DMA((len(ins),))],
        compiler_params=pltpu.CompilerParams(has_side_effects=True),
    )(*ins)


def allgather_chips(shards, *, name):
    return _exchange_call(allgather_rider(shards), name=name)


def _allgather_ops(x_refs, o_refs, send_sems, recv_sems, local_sems):
    def copies():
        x, y, c = _place()
        me = 2 * x + y
        chips = _other_chips(x, y)
        own, first, passed, landed, handed = [], [], [], [], []
        for b, (x_ref, o_ref) in enumerate(zip(x_refs, o_refs)):
            half = x_ref.shape[0] // 2
            mine, theirs = pl.ds(c * half, half), pl.ds((1 - c) * half, half)
            own.append(pltpu.make_async_copy(x_ref, o_ref.at[me], local_sems.at[b]))

            def copy(k, src, chip, rows, to, o_ref=o_ref, b=b):
                return pltpu.make_async_remote_copy(src_ref=src, dst_ref=o_ref.at[2 * chip[0] + chip[1], rows],
                                                    send_sem=send_sems.at[6 * b + k], recv_sem=recv_sems.at[6 * b + k],
                                                    device_id=to, device_id_type=MESH)
            for j, chip in enumerate(chips):
                first.append(copy(j, x_ref.at[mine], (x, y), mine, (*chip, c)))
                landed.append(copy(j, x_ref.at[mine], chip, mine, (*chip, c)))
                passed.append(copy(3 + j, o_ref.at[2 * chip[0] + chip[1], mine], chip, mine, (x, y, 1 - c)))
                handed.append(copy(3 + j, x_ref.at[mine], chip, theirs, (x, y, 1 - c)))
        return own, first, passed, landed, handed

    def start():
        own, first, _, _, _ = copies()
        for cp in own + first:
            cp.start()

    def finish():
        own, first, passed, landed, handed = copies()
        for arrived, forward in zip(landed, passed):
            arrived.wait_recv()
            forward.start()
        for cp in handed:
            cp.wait_recv()
        for cp in first + passed:
            cp.wait_send()
        for cp in own:
            cp.wait()
    return start, finish


def _scatter_ops(x_refs, o_refs, send_sems, recv_sems, local_sems):
    def copies():
        x, y, c = _place()
        return [pltpu.make_async_remote_copy(
            src_ref=x_ref.at[2 * px + py], dst_ref=o_ref.at[j], send_sem=send_sems.at[3 * b + j],
            recv_sem=recv_sems.at[3 * b + j], device_id=(px, py, c), device_id_type=MESH)
            for b, (x_ref, o_ref) in enumerate(zip(x_refs, o_refs)) for j, (px, py) in enumerate(_other_chips(x, y))]

    def start():
        for cp in copies():
            cp.start()

    def finish():
        sends = copies()
        for cp in sends:
            cp.wait_recv()
        for cp in sends:
            cp.wait_send()
    return start, finish


class Rider(NamedTuple):
    ins: list
    out_shape: list
    n_sems: int
    ops: object

    def specs(self):
        hbm = pl.BlockSpec(memory_space=pl.ANY)
        sems = [pltpu.SemaphoreType.DMA((self.n_sems,)), pltpu.SemaphoreType.DMA((self.n_sems,)),
                pltpu.SemaphoreType.DMA((len(self.ins),))]
        return [hbm] * len(self.ins), [hbm] * len(self.out_shape), sems

    def wrap(self, body, n_in, n_out, grid_rank):
        k_in, k_out = len(self.ins), len(self.out_shape)

        def carried(*refs):
            ins, r_ins = refs[:n_in], refs[n_in:n_in + k_in]
            outs = refs[n_in + k_in:n_in + k_in + n_out]
            r_outs = refs[n_in + k_in + n_out:n_in + k_in + n_out + k_out]
            scratch, sems = refs[n_in + k_in + n_out + k_out:-3], refs[-3:]
            first = functools.reduce(jnp.logical_and, [pl.program_id(a) == 0 for a in range(grid_rank)])
            last = functools.reduce(jnp.logical_and,
                                    [pl.program_id(a) == pl.num_programs(a) - 1 for a in range(grid_rank)])
            pl.when(first)(lambda: self.ops(r_ins, r_outs, *sems)[0]())
            body(*ins, *outs, *scratch)
            pl.when(last)(lambda: self.ops(r_ins, r_outs, *sems)[1]())
        return carried


def _exchange_call(rider, *, name):
    def body(*refs):
        k = len(rider.ins)
        start, finish = rider.ops(refs[:k], refs[k:k + len(rider.out_shape)], *refs[-3:])
        start()
        finish()
    in_specs, out_specs, sems = rider.specs()
    return pl.pallas_call(body, name=name, in_specs=in_specs, out_specs=out_specs, out_shape=rider.out_shape,
                          scratch_shapes=sems, compiler_params=pltpu.CompilerParams(has_side_effects=True))(*rider.ins)


def allgather_rider(shards):
    assert all(s.shape[0] % (2 * ROW_ALIGN) == 0 for s in shards)
    return Rider(list(shards), [jax.ShapeDtypeStruct((4,) + s.shape, s.dtype) for s in shards], 6 * len(shards),
                 _allgather_ops)


def scatter_rider(parts):
    return Rider(list(parts), [jax.ShapeDtypeStruct((3,) + p.shape[1:], p.dtype) for p in parts], 3 * len(parts),
                 _scatter_ops)


def scatter_chips(parts, *, name):
    return _exchange_call(scatter_rider(parts), name=name)


def swap_cores(vs, *, name):
    nb = len(vs)

    def body(*refs):
        x_refs, o_refs = refs[:nb], refs[nb:2 * nb]
        send_sems, recv_sems, _ = refs[2 * nb:]
        x, y, c = _place()
        copies = [pltpu.make_async_remote_copy(src_ref=x_ref, dst_ref=o_ref, send_sem=send_sems.at[b],
                                               recv_sem=recv_sems.at[b], device_id=(x, y, 1 - c), device_id_type=MESH)
                  for b, (x_ref, o_ref) in enumerate(zip(x_refs, o_refs))]
        for cp in copies:
            cp.start()
        for cp in copies:
            cp.wait()

    return _hbm_call(body, vs, [jax.ShapeDtypeStruct(v.shape, v.dtype) for v in vs], nb, name=name)


def allreduce_small(v, *, name):
    R, C = v.shape

    def body(x_ref, o_ref, gath_ref, send_sems, recv_sems):
        x, y, c = _place()
        me = 4 * x + 2 * y + c
        gath_ref[me] = x_ref[...]
        flips = [(k >> 2 & 1, k >> 1 & 1, k & 1) for k in range(1, 8)]
        sends = []
        for j, (fx, fy, fc) in enumerate(flips):
            peer = (x ^ fx, y ^ fy, c ^ fc)
            cp = pltpu.make_async_remote_copy(src_ref=x_ref, dst_ref=gath_ref.at[me], send_sem=send_sems.at[j],
                                              recv_sem=recv_sems.at[j], device_id=peer, device_id_type=MESH)
            cp.start()
            sends.append(cp)
        for j, (fx, fy, fc) in enumerate(flips):
            peer = (x ^ fx, y ^ fy, c ^ fc)
            pltpu.make_async_remote_copy(src_ref=x_ref, dst_ref=gath_ref.at[4 * peer[0] + 2 * peer[1] + peer[2]],
                                         send_sem=send_sems.at[j], recv_sem=recv_sems.at[j], device_id=peer,
                                         device_id_type=MESH).wait_recv()
        for cp in sends:
            cp.wait_send()
        total = gath_ref[0]
        for d in range(1, 8):
            total = total + gath_ref[d]
        o_ref[...] = total

    vm = pl.BlockSpec(memory_space=pltpu.VMEM)
    out, _ = pl.pallas_call(
        body, name=name,
        in_specs=[vm], out_specs=[vm, vm],
        out_shape=[jax.ShapeDtypeStruct((R, C), F32), jax.ShapeDtypeStruct((8, R, C), F32)],
        scratch_shapes=[pltpu.SemaphoreType.DMA((7,)), pltpu.SemaphoreType.DMA((7,))],
        compiler_params=pltpu.CompilerParams(has_side_effects=True),
    )(v)
    return out


ROW_ALIGN = 16
PACK_W = 1024
SUM_TILE = 512
BIG_WEIGHTS = (("w_in", 1), ("w_a", 1), ("w_b", 1), ("w_o", 0), ("w_ff1", 1), ("w_ff2", 0), ("w_pg", 0), ("w_p", 1))


def _b_layout(d, ple):
    hw, q = d // 2, d // 4
    small = 2 * d + 2 * q
    lay = {"w_ff1": (0, 0, d, d), "w_ff2": (d, 0, d, d), "w_o": (2 * d, 0, q, d), "w_pg": (2 * d + q, 0, q, d),
           "w_a": (small, 0, hw, q), "w_b": (small, q, hw, q), "w_p": (small, 2 * q, ple, q)}
    return lay, small + hw


def pack_a(w_in_shard):
    rows, cols = w_in_shard.shape
    pad = -cols % LANES
    return jnp.concatenate([w_in_shard, jnp.zeros((rows, pad), w_in_shard.dtype)], axis=1)


def pack_b(shards, d):
    hw, q = d // 2, d // 4
    dt = shards["w_a"].dtype
    wp = shards["w_p"]
    wp = jnp.concatenate([wp, jnp.zeros((hw - wp.shape[0], q), dt)], axis=0)
    small = jnp.concatenate([shards["w_a"], shards["w_b"], wp, jnp.zeros((hw, d - 3 * q), dt)], axis=1)
    return jnp.concatenate([shards["w_ff1"], shards["w_ff2"], shards["w_o"], shards["w_pg"], small], axis=0)


def unpack_b(buf, lay):
    return {nm: buf[r0:r0 + rows, c0:c0 + cols] for nm, (r0, c0, rows, cols) in lay.items()}


def _win_layout(d):
    hw = d // 2
    fh = hw // FOX_HDIM
    orig = {"hq": (0, hw), "hf": (hw, hw), "hi": (2 * hw, hw), "hg": (3 * hw, hw), "fq": (4 * hw, hw),
            "fk": (5 * hw, hw), "fv": (6 * hw, hw), "ff": (7 * hw, fh), "ga": (7 * hw + fh, d), "gb": (7 * hw + fh + d, d)}
    order = ["ga", "gb", "hq", "hf", "hi", "hg", "fq", "fk", "fv", "ff"]
    mine, off = {}, 0
    for nm in order:
        width = orig[nm][1] if nm != "ff" else LANES
        mine[nm] = (off, width)
        off += width
    return orig, order, mine, off


def _adam_fn(rows, vecs):
    w, g, m, v = rows
    m2 = ADAM_B1 * m + (1.0 - ADAM_B1) * g
    v2 = ADAM_B2 * v + (1.0 - ADAM_B2) * (g * g)
    m_hat = m2 / (1.0 - ADAM_B1 ** ADAM_STEP)
    v_hat = v2 / (1.0 - ADAM_B2 ** ADAM_STEP)
    delta = -ADAM_LR * (m_hat / (jnp.sqrt(v_hat) + ADAM_EPS) + ADAM_WD * w)
    return [delta, m2, v2], []


def adamw_small(small, p0, ws, ms, vs, *, name):
    n = len(ws)
    hw = p0.shape[1]
    fh = ws[8].shape[1]

    def body(small_ref, p0_ref, *refs):
        w_refs, m_refs, v_refs = refs[:n], refs[n:2 * n], refs[2 * n:3 * n]
        g_out, d_out, m_out, v_out = (refs[(3 + k) * n:(4 + k) * n] for k in range(4))
        sm = small_ref[...]
        p = p0_ref[...]
        d_lb = sm[6:7, hw:2 * hw] * (p * (1.0 - p))
        grads = [sm[r:r + 1, :] for r in range(6)]
        grads += [jnp.concatenate([d_lb, -d_lb], axis=0), sm[6:7, :hw], sm[7:8, :fh]]
        for i in range(n):
            (delta, m2, v2), _ = _adam_fn([w_refs[i][...], grads[i], m_refs[i][...], v_refs[i][...]], [])
            g_out[i][...], d_out[i][...], m_out[i][...], v_out[i][...] = grads[i], delta, m2, v2

    shapes = [jax.ShapeDtypeStruct(w.shape, F32) for w in ws]
    return pl.pallas_call(body, name=name, out_shape=shapes * 4)(small, p0, *ws, *ms, *vs)


def adamw(w, g, m, v, *, name):
    c = w.shape[1]
    (delta, m2, v2), _ = rowwise(_adam_fn, [w, g, m, v], [], [(c, F32)] * 3, name=name, tm=256)
    return delta, m2, v2


def kernel(x, p, ln0_g, ln0_b, w_in, hg_lb, hg_norm_g, fox_fb, w_a, w_b, w_o, ln1_g, ln1_b, w_ff1, w_ff2, w_pg, w_p, ln2_g, ln2_b, loss_target, m_ln0_g, m_ln0_b, m_w_in, m_hg_lb, m_hg_norm_g, m_fox_fb, m_w_a, m_w_b, m_w_o, m_ln1_g, m_ln1_b, m_w_ff1, m_w_ff2, m_w_pg, m_w_p, m_ln2_g, m_ln2_b, v_ln0_g, v_ln0_b, v_w_in, v_hg_lb, v_hg_norm_g, v_fox_fb, v_w_a, v_w_b, v_w_o, v_ln1_g, v_ln1_b, v_w_ff1, v_w_ff2, v_w_pg, v_w_p, v_ln2_g, v_ln2_b):
    n_seq, seq, d = x.shape
    T = n_seq * seq
    hw = d // 2
    fh = hw // FOX_HDIM
    bh = n_seq * fh
    orig, order, mine, n_in = _win_layout(d)

    big = {"w_in": w_in[0], "w_a": w_a[0], "w_b": w_b[0], "w_o": w_o[0], "w_ff1": w_ff1[0], "w_ff2": w_ff2[0],
           "w_pg": w_pg[0], "w_p": w_p[0]}
    big_m = {"w_in": m_w_in[0], "w_a": m_w_a[0], "w_b": m_w_b[0], "w_o": m_w_o[0], "w_ff1": m_w_ff1[0],
             "w_ff2": m_w_ff2[0], "w_pg": m_w_pg[0], "w_p": m_w_p[0]}
    big_v = {"w_in": v_w_in[0], "w_a": v_w_a[0], "w_b": v_w_b[0], "w_o": v_w_o[0], "w_ff1": v_w_ff1[0],
             "w_ff2": v_w_ff2[0], "w_pg": v_w_pg[0], "w_p": v_w_p[0]}
    names = [nm for nm, _ in BIG_WEIGHTS]
    axis = dict(BIG_WEIGHTS)
    ple = w_p.shape[1]
    lay, b_rows = _b_layout(d, ple)
    in_cols = big["w_in"].shape[1]

    gather_w_in = allgather_rider([pack_a(big["w_in"].astype(BF16))])
    gather_rest = allgather_rider([pack_b({nm: big[nm].astype(BF16) for nm in names if nm != "w_in"}, d)])

    x2 = x.reshape(T, d)
    tgt = loss_target.reshape(T, d)
    p_b = p.reshape(T, p.shape[-1]).astype(BF16)
    vec = lambda a: a.reshape(1, -1)
    probs = jax.nn.softmax(hg_lb, axis=0)
    lb = vec(probs[0])

    def ln0_fn(rows, vecs):
        h = _ln_stats(rows[0]) * vecs[0] + vecs[1]
        return [h, h], []
    (h0, h0b), _, (a_all,) = rowwise(ln0_fn, [x2], [vec(ln0_g), vec(ln0_b)], [(d, F32), (d, BF16)], name="ln0_fwd",
                                     rider=gather_w_in)
    win = jnp.concatenate([a_all[s, :, :in_cols] for s in range(4)], axis=1)
    win_mine = jnp.concatenate(
        [win[:, orig[nm][0]:orig[nm][0] + orig[nm][1]] for nm in order]
        + [jnp.zeros((d, LANES - fh), BF16)], axis=1)
    proj = matmul_nn(h0b, win_mine, name="in_proj")

    o_raw, hg_states, (b_all,) = hgrn2_fwd(proj, [mine["hq"][0], mine["hf"][0], mine["hi"][0]], lb, n_seq, seq,
                                           name="hgrn2_fwd", rider=gather_rest)
    view = lambda nm, k, n: WView(b_all, lay[nm][0], lay[nm][1], k, n, axis[nm])
    w_ff1_v, w_ff2_v = view("w_ff1", d, 4 * d), view("w_ff2", 4 * d, d)

    def whole(nm):
        r0, c0, rows, cols = lay[nm]
        return jnp.concatenate([b_all[s, r0:r0 + rows, c0:c0 + cols] for s in range(4)], axis=axis[nm])
    w_o_v, w_pg_v, w_a_v, w_p_v, w_b_full = whole("w_o"), whole("w_pg"), whole("w_a"), whole("w_p"), whole("w_b")

    def ya_fn(rows, vecs):
        o, hg = rows
        outs = []
        for h in range(HG_HEADS):
            oh = o[:, h * HG_DIM:(h + 1) * HG_DIM]
            outs.append(oh * lax.rsqrt(jnp.mean(oh * oh, axis=-1, keepdims=True) + RMS_EPS))
        y = jnp.concatenate(outs, axis=1) * vecs[0] * (hg * _sigmoid(hg))
        return [y], []
    (y_a,), _ = rowwise(ya_fn, [o_raw, (proj,) + mine["hg"]], [hg_norm_g], [(hw, BF16)], name="hgrn2_out_fwd")

    fb_pad = jnp.concatenate([fox_fb, jnp.zeros((1, LANES - fh), F32)], axis=1)

    def lf_fn(rows, vecs):
        u = rows[0] + vecs[0]
        return [jnp.minimum(u, 0.0) - jnp.log(1.0 + jnp.exp(-jnp.abs(u)))], []
    (lf,), _ = rowwise(lf_fn, [(proj,) + mine["ff"]], [fb_pad], [(LANES, F32)], name="fox_logf")
    c_cum = seq_cumsum(lf, n_seq, seq, reverse=False, name="fox_cumsum")

    place = _fox_placement(fh)

    def prep_fn(rows, vecs):
        fq_, fk_, fv_, cc = rows
        pq, pk, aq, ak, oq, ok = vecs
        parts = jnp.concatenate(_split3(cc), axis=1)
        mm = lambda a_, b_: jnp.dot(a_, b_, preferred_element_type=F32)
        q_ = mm(fq_.astype(BF16), pq) + mm(parts, aq) + oq
        k_ = mm(fk_.astype(BF16), pk) + mm(parts, ak) + ok
        return [q_, k_, mm(fv_.astype(BF16), pk)], []
    wa = fh * FOX_AUG
    (qa, ka, va), _ = rowwise(prep_fn, [(proj,) + mine["fq"], (proj,) + mine["fk"], (proj,) + mine["fv"], c_cum],
                              [place[nm] for nm in ("pq", "pk", "aq", "ak", "oq", "ok")], [(wa, BF16)] * 3,
                              name="fox_prep")
    as_seq = lambda t2d: t2d.reshape(n_seq, seq, t2d.shape[1])
    o_fox, ox_fox, lse = fox_fwd(as_seq(qa), as_seq(ka), as_seq(va), name="fox_fwd")
    y_b = o_fox.reshape(T, wa)
    wb_pad = jnp.concatenate([w_b_full.reshape(fh, FOX_HDIM, d), jnp.zeros((fh, FOX_AUG - FOX_HDIM, d), BF16)],
                             axis=1).reshape(wa, d)

    pa = matmul_nn(y_a, w_a_v, name="proj_a")
    pb = matmul_nn(y_b, wb_pad, name="proj_b")

    def merge_fn(rows, vecs):
        ga, gb, a, b = rows
        return [_sigmoid(ga) * a + _sigmoid(gb) * b], []
    (merged,), _ = rowwise(merge_fn, [(proj,) + mine["ga"], (proj,) + mine["gb"], pa, pb], [], [(d, BF16)],
                           name="merge_fwd")
    fused_tm = 512

    def ln1_post(mix, aux, vecs):
        z = ALPHA * aux[0] + mix
        h = _ln_stats(z) * vecs[0] + vecs[1]
        return [z, h, h], []
    (z1, h1, h1b), _ = matmul_nn(merged, w_o_v, name="out_proj_ln1", tm=fused_tm, post=ln1_post, post_aux=[h0],
                                 post_vecs=[ln1_g, ln1_b], post_outs=[F32, F32, BF16])

    relu2 = lambda u: jnp.square(jnp.maximum(u, 0.0))
    act = matmul_nn(h1b, w_ff1_v, name="ff1", out_dtype=BF16, epilogue=relu2)
    pg = matmul_nn(h1b, w_pg_v, name="ple_gate")
    pe = matmul_nn(p_b, w_p_v, name="ple_embed")

    def head_post(ffv, aux, vecs):
        h1v, pgv, pev, t = aux
        g2, b2 = vecs
        sp = _sigmoid(pgv)
        z = ALPHA * h1v + ffv + sp * pev
        y = _ln_stats(z) * g2 + b2
        err = y - t
        loss_rows = 0.5 * jnp.mean(err * err, axis=-1, keepdims=True)
        dy = err * (1.0 / d)
        dz, dg2, db2 = _ln_bwd(z, dy, g2)
        loss_acc = jnp.broadcast_to(_colsum(loss_rows), (1, d))
        return [dz, dz, dz * pev * (sp * (1.0 - sp)), dz * sp], [dg2, db2, loss_acc]
    (dz2, dz2b, dpg, dpe), (g_ln2_g, g_ln2_b, loss_part) = matmul_nn(
        act, w_ff2_v, name="ff2_head", tm=fused_tm, post=head_post, post_aux=[h1, pg, pe, tgt],
        post_vecs=[ln2_g, ln2_b], post_outs=[F32, BF16, BF16, BF16], post_accs=[d, d, d])

    dact = lambda da, a: da * (2.0 * jnp.sqrt(a.astype(F32)))
    du = matmul_nn(dz2b, w_ff2_v, transpose_rhs=True, name="d_ff2", out_dtype=BF16, epilogue=dact, aux=act)
    dh1_pg = matmul_nn(dpg, w_pg_v, transpose_rhs=True, name="d_ple_gate")

    def ln1_bwd_post(dh1_ff, aux, vecs):
        dh1 = ALPHA * aux[0] + dh1_ff + aux[1]
        dz, dg, db = _ln_bwd(aux[2], dh1, vecs[0])
        return [dz, dz], [dg, db]
    (dz1, dz1b), (g_ln1_g, g_ln1_b) = matmul_nn(
        du, w_ff1_v, transpose_rhs=True, name="d_ff1_ln1", tm=fused_tm, post=ln1_bwd_post, post_aux=[dz2, dh1_pg, z1],
        post_vecs=[ln1_g], post_outs=[F32, BF16], post_accs=[d, d])

    def merge_bwd_post(dm, aux, vecs):
        ga, gb, a, b = aux
        sa, sb = _sigmoid(ga), _sigmoid(gb)
        return [dm * a * (sa * (1.0 - sa)), dm * b * (sb * (1.0 - sb)), dm * sa, dm * sb], []
    (dga, dgb, dma, dmb), _ = matmul_nn(
        dz1b, w_o_v, transpose_rhs=True, name="d_out_proj_merge", tm=fused_tm, post=merge_bwd_post,
        post_aux=[(proj,) + mine["ga"], (proj,) + mine["gb"], pa, pb], post_outs=[BF16] * 4)
    dya = matmul_nn(dma, w_a_v, transpose_rhs=True, name="d_proj_a")
    dyb = matmul_nn(dmb, wb_pad, transpose_rhs=True, name="d_proj_b", out_dtype=BF16)

    def ya_bwd_fn(rows, vecs):
        o, hg, dy = rows
        ng = vecs[0]
        sg = _sigmoid(hg)
        gate = hg * sg
        dn_parts, do_parts, n_parts = [], [], []
        for h in range(HG_HEADS):
            hs = slice(h * HG_DIM, (h + 1) * HG_DIM)
            oh = o[:, hs]
            r = lax.rsqrt(jnp.mean(oh * oh, axis=-1, keepdims=True) + RMS_EPS)
            nh = oh * r
            dn = dy[:, hs] * ng[:, hs] * gate[:, hs]
            do_parts.append(r * (dn - nh * jnp.mean(dn * nh, axis=-1, keepdims=True)))
            n_parts.append(nh)
        nrm = jnp.concatenate(n_parts, axis=1)
        dhg = dy * nrm * ng * (sg * (1.0 + hg * (1.0 - sg)))
        return [jnp.concatenate(do_parts, axis=1), dhg], [_colsum(dy * nrm * gate)]
    (do_raw, dhg), (g_norm_g,) = rowwise(ya_bwd_fn, [o_raw, (proj,) + mine["hg"], dya], [hg_norm_g],
                                         [(hw, F32), (hw, BF16)], [hw], name="hgrn2_out_bwd")
    dhq, dhf, dhi, g_lb = hgrn2_bwd(proj, [mine["hq"][0], mine["hf"][0], mine["hi"][0]], lb, do_raw, hg_states,
                                    n_seq, seq, name="hgrn2_bwd")

    do_fox = as_seq(dyb)
    dqa, dka, dva, dsum = fox_bwd(as_seq(qa), as_seq(ka), as_seq(va), do_fox, ox_fox, lse, name="fox_bwd")

    def unprep_fn(rows, vecs):
        mm = lambda a_, b_: jnp.dot(a_.astype(BF16), b_, preferred_element_type=F32)
        return [mm(rows[0], vecs[0]), mm(rows[1], vecs[1]), mm(rows[2], vecs[1])], []
    (dfq, dfk, dfv), _ = rowwise(unprep_fn, [dqa.reshape(T, wa), dka.reshape(T, wa), dva.reshape(T, wa)],
                                 [place["pqt"], place["pkt"]], [(hw, BF16)] * 3, name="fox_unprep")
    dc = -dsum.reshape(n_seq, fh, seq).transpose(0, 2, 1).reshape(T, fh)
    dc = jnp.concatenate([dc, jnp.zeros((T, LANES - fh), F32)], axis=1)
    dlf = seq_cumsum(dc, n_seq, seq, reverse=True, name="fox_cumsum_bwd")

    def lf_bwd_fn(rows, vecs):
        u = rows[0] + vecs[0]
        du_ = rows[1] * _sigmoid(-u)
        return [du_], [_colsum(du_)]
    (dff_,), (g_fb,) = rowwise(lf_bwd_fn, [(proj,) + mine["ff"], dlf], [fb_pad], [(LANES, BF16)], [LANES],
                               name="fox_logf_bwd")

    dproj = jnp.concatenate([dga, dgb, dhq, dhf, dhi, dhg, dfq, dfk, dfv, dff_], axis=1)

    grads_b = jnp.zeros((4, b_rows, d), F32)
    for nm, lhs, rhs in (("w_ff1", h1b, du), ("w_ff2", act, dz2b)):
        grads_b = matmul_tn(lhs, rhs, name="g_" + nm, into=(grads_b, lay[nm][0], lay[nm][1], axis[nm]))
    gfull = {
        "w_a": matmul_tn(y_a, dma, name="g_w_a"),
        "w_b": matmul_tn(y_b, dmb, name="g_w_b").reshape(fh, FOX_AUG, d)[:, :FOX_HDIM].reshape(hw, d),
        "w_o": matmul_tn(merged, dz1b, name="g_w_o"),
        "w_pg": matmul_tn(h1b, dpg, name="g_w_pg"),
        "w_p": matmul_tn(p_b, dpe, name="g_w_p"),
    }

    def chip_parts(nm, s):
        g = gfull[nm]
        n = g.shape[axis[nm]] // 4
        return lax.slice_in_dim(g, s * n, (s + 1) * n, axis=axis[nm])
    for nm in gfull:
        grads_b = lax.dynamic_update_slice(grads_b, jnp.stack([chip_parts(nm, s) for s in range(4)]),
                                           (0, lay[nm][0], lay[nm][1]))
    me = 2 * lax.axis_index("x") + lax.axis_index("y")
    core = lax.axis_index("c")

    def sum2_fn(rows, vecs):
        s = rows[0] + rows[1].astype(F32)
        return [s, s], []

    def sum4_fn(rows, vecs):
        a, r0, r1, r2 = rows
        return [((a + r0.astype(F32)) + r1.astype(F32)) + r2.astype(F32)], []

    def chip_pair_sum(g, tag):
        h, cols = g.shape[1] // 2, g.shape[2]
        keep = lax.dynamic_slice_in_dim(g, core * h, h, axis=1)
        give = lax.dynamic_slice_in_dim(g, (1 - core) * h, h, axis=1).astype(BF16)
        (from_core,) = swap_cores([give], name="swap_partials_" + tag)
        (s32, s16), _ = rowwise(sum2_fn, [keep.reshape(4 * h, cols), from_core.reshape(4 * h, cols)], [],
                                [(cols, F32), (cols, BF16)], name="sum_cores_" + tag, tm=SUM_TILE)
        return s32.reshape(4, h, cols), s16.reshape(4, h, cols)

    def chip_sum(pr, gt, tag):
        own = lax.dynamic_index_in_dim(pr, me, axis=0, keepdims=False)
        (q,), _ = rowwise(sum4_fn, [own, gt[0], gt[1], gt[2]], [], [(own.shape[1], F32)], name="sum_chips_" + tag,
                          tm=SUM_TILE)
        return q

    pair_rest, pair_rest_b = chip_pair_sum(grads_b, "rest")
    gw_in_mine, (got_rest,) = matmul_tn(h0b, dproj, name="g_w_in", rider=scatter_rider([pair_rest_b]))
    gfull["w_in"] = jnp.concatenate([gw_in_mine[:, mine[nm][0]:mine[nm][0] + orig[nm][1]]
                                     for nm in ["hq", "hf", "hi", "hg", "fq", "fk", "fv", "ff", "ga", "gb"]], axis=1)
    grads_a = jnp.stack([pack_a(chip_parts("w_in", s)) for s in range(4)])
    pair_in, pair_in_b = chip_pair_sum(grads_a, "w_in")
    def ln0_bwd_post(dh0_in, aux, vecs):
        dx, dg, db = _ln_bwd(aux[1], dh0_in + ALPHA * aux[0], vecs[0])
        return [dx], [dg, db]
    ((dx,), (g_ln0_g, g_ln0_b)), (got_in,) = matmul_nn(
        dproj, win_mine, transpose_rhs=True, name="d_in_proj_ln0", tm=fused_tm, post=ln0_bwd_post,
        post_aux=[dz1, x2], post_vecs=[vec(ln0_g)], post_outs=[F32], post_accs=[d, d],
        rider=scatter_rider([pair_in_b]))
    q_half = [chip_sum(pair_in, got_in, "w_in"), chip_sum(pair_rest, got_rest, "rest")]
    q_other = swap_cores(q_half, name="swap_halves")
    g_a, g_b = [jnp.concatenate([jnp.where(core == 0, mine_, other), jnp.where(core == 0, other, mine_)], axis=0)
                for mine_, other in zip(q_half, q_other)]
    g_shards = unpack_b(g_b, lay)
    g_shards["w_in"] = g_a[:, :in_cols]

    assert d == PACK_W and 2 * hw == PACK_W and fh <= LANES
    small = allreduce_small(jnp.concatenate(
        [g_ln0_g, g_ln0_b, g_ln1_g, g_ln1_b, g_ln2_g, g_ln2_b, jnp.concatenate([g_norm_g, g_lb], axis=1),
         jnp.concatenate([g_fb, loss_part[:, LANES:]], axis=1)], axis=0), name="allreduce_small")
    loss = small[7, LANES]

    small_w = [vec(ln0_g), vec(ln0_b), ln1_g, ln1_b, ln2_g, ln2_b, hg_lb, hg_norm_g, fox_fb]
    small_m = [vec(m_ln0_g), vec(m_ln0_b), m_ln1_g, m_ln1_b, m_ln2_g, m_ln2_b, m_hg_lb, m_hg_norm_g, m_fox_fb]
    small_v = [vec(v_ln0_g), vec(v_ln0_b), v_ln1_g, v_ln1_b, v_ln2_g, v_ln2_b, v_hg_lb, v_hg_norm_g, v_fox_fb]
    small_out = adamw_small(small, probs[0:1], small_w, small_m, small_v, name="adamw_small")
    small_shapes = [ln0_g.shape, ln0_b.shape, ln1_g.shape, ln1_b.shape, ln2_g.shape, ln2_b.shape, hg_lb.shape,
                    hg_norm_g.shape, fox_fb.shape]
    sg_out, sd_out, sm_out, sv_out = [[a.reshape(shp) for a, shp in zip(small_out[9 * k:9 * k + 9], small_shapes)]
                                      for k in range(4)]

    big_out = {}
    for nm in names:
        delta, m2, v2 = adamw(big[nm], g_shards[nm], big_m[nm], big_v[nm], name="adamw_" + nm)
        big_out[nm] = (g_shards[nm][None], delta[None], m2[None], v2[None])

    def ordered(k):
        sm_ = [sg_out, sd_out, sm_out, sv_out][k]
        bg = lambda nm: big_out[nm][k]
        return [sm_[0], sm_[1], bg("w_in"), sm_[6], sm_[7], sm_[8], bg("w_a"), bg("w_b"), bg("w_o"), sm_[2], sm_[3],
                bg("w_ff1"), bg("w_ff2"), bg("w_pg"), bg("w_p"), sm_[4], sm_[5]]
    grad_x = dx.reshape(n_seq, seq, d)
    return (loss, grad_x, *ordered(0), *ordered(1), *ordered(2), *ordered(3))
```

```python
import functools
from typing import NamedTuple, Optional

import numpy as np
import jax
import jax.numpy as jnp
from jax import lax
from jax.experimental import pallas as pl
from jax.experimental.pallas import tpu as pltpu

F32 = jnp.float32
BF16 = jnp.bfloat16
MESH = pl.DeviceIdType.MESH

VMEM_LIMIT_BYTES = 48 * 1024 * 1024
LANES = 128
HG_HEADS = 4
HG_DIM = 128
HG_BLK = 16
HG_TILE = 256
HG_SLOTS = 4
FOX_HDIM = 64
FOX_AUG = 128
FOX_TQ = 1024
FOX_FWD_HEADS = 1
LN_EPS = 1e-5
RMS_EPS = 1e-6
DEPTH = 1
ALPHA = (2.0 * DEPTH) ** 0.25
ADAM_LR, ADAM_B1, ADAM_B2, ADAM_EPS, ADAM_WD, ADAM_STEP = 0.001, 0.9, 0.999, 1e-08, 0.01, 10
NEG_INF = -1e30


def _cparams(sem):
    return pltpu.CompilerParams(dimension_semantics=sem, vmem_limit_bytes=VMEM_LIMIT_BYTES)


def _tile(n, cap):
    if n <= cap:
        return n
    best = None
    for t in range(LANES, cap + 1, LANES):
        if n % t == 0:
            best = t
    assert best is not None, (n, cap)
    return best


class WView(NamedTuple):
    arr: jax.Array
    r0: int
    c0: int
    k: int
    n: int
    split: Optional[int]


def matmul_nn(a, w, *, name, transpose_rhs=False, out_dtype=F32, epilogue=None, aux=None, tm=1024, rider=None,
              post=None, post_aux=(), post_vecs=(), post_outs=(), post_accs=()):
    wv = w if isinstance(w, WView) else WView(w[None], 0, 0, w.shape[0], w.shape[1], None)
    rows_s = wv.k // 4 if wv.split == 0 else wv.k
    cols_s = wv.n // 4 if wv.split == 1 else wv.n
    tr, tc = _tile(rows_s, 1152), _tile(cols_s, 1152)
    assert wv.r0 % tr == 0 and wv.c0 % tc == 0
    T, K = a.shape
    N, tn, tk = (wv.k, tr, tc) if transpose_rhs else (wv.n, tc, tr)
    assert K == (wv.n if transpose_rhs else wv.k)
    tm = min(tm, T)
    assert T % tm == 0
    nk = K // tk

    def w_block(ri, ci):
        if wv.split == 0:
            return (ri * tr) // rows_s, (wv.r0 + (ri * tr) % rows_s) // tr, wv.c0 // tc + ci
        if wv.split == 1:
            return (ci * tc) // cols_s, wv.r0 // tr + ri, (wv.c0 + (ci * tc) % cols_s) // tc
        return 0, wv.r0 // tr + ri, wv.c0 // tc + ci

    fused = post is not None
    assert not fused or N == tn
    aux_list = list(post_aux) if fused else ([aux] if aux is not None else [])
    aux_list = [x if isinstance(x, tuple) else (x, 0, x.shape[1]) for x in aux_list]
    vec_list = list(post_vecs)
    out_dtypes = list(post_outs) if fused else [out_dtype]
    n_aux, n_vec, n_out, n_acc = len(aux_list), len(vec_list), len(out_dtypes), len(post_accs)

    def body(*refs):
        a_ref, w_ref = refs[:2]
        aux_refs = refs[2:2 + n_aux]
        vec_refs = refs[2 + n_aux:2 + n_aux + n_vec]
        out_refs = refs[2 + n_aux + n_vec:2 + n_aux + n_vec + n_out]
        sum_refs = refs[2 + n_aux + n_vec + n_out:2 + n_aux + n_vec + n_out + n_acc]
        acc_ref = refs[-1]
        m, k = pl.program_id(1), pl.program_id(2)
        if transpose_rhs:
            part = lax.dot_general(a_ref[...], w_ref[...], (((1,), (1,)), ((), ())), preferred_element_type=F32)
        else:
            part = jnp.dot(a_ref[...], w_ref[...], preferred_element_type=F32)

        def write(res):
            if not fused:
                if epilogue is not None:
                    res = epilogue(res) if not aux_refs else epilogue(res, aux_refs[0][...])
                out_refs[0][...] = res.astype(out_dtype)
                return
            outs, sums = post(res, [r[...] for r in aux_refs], [v[...] for v in vec_refs])
            assert len(outs) == n_out and len(sums) == n_acc
            for r, val in zip(out_refs, outs):
                r[...] = val.astype(r.dtype)
            for r, val in zip(sum_refs, sums):
                def first_rows(r=r, val=val):
                    r[...] = val

                def later_rows(r=r, val=val):
                    r[...] += val
                pl.when(m == 0)(first_rows)
                pl.when(m > 0)(later_rows)

        if nk == 1:
            write(part)
        else:
            @pl.when(k == 0)
            def _():
                acc_ref[...] = part

            @pl.when(k > 0)
            def _():
                acc_ref[...] += part

            @pl.when(k == nk - 1)
            def _():
                write(acc_ref[...])

    w_index = (lambda n, m, k: w_block(n, k)) if transpose_rhs else (lambda n, m, k: w_block(k, n))
    in_specs = [pl.BlockSpec((tm, tk), lambda n, m, k: (m, k)),
                pl.BlockSpec((None, tr, tc), w_index)]
    args = [a, wv.arr]
    for arr, off, width in aux_list:
        assert width == N and off % tn == 0
        in_specs.append(pl.BlockSpec((tm, tn), functools.partial(lambda n, m, k, blk: (m, blk + n), blk=off // tn)))
        args.append(arr)
    for v in vec_list:
        in_specs.append(pl.BlockSpec(v.shape, lambda n, m, k: (0, 0)))
        args.append(v)
    out_specs = [pl.BlockSpec((tm, tn), lambda n, m, k: (m, n)) for _ in out_dtypes]
    out_specs += [pl.BlockSpec((1, tn), lambda n, m, k: (0, 0)) for _ in post_accs]
    out_shape = [jax.ShapeDtypeStruct((T, N), dt) for dt in out_dtypes]
    out_shape += [jax.ShapeDtypeStruct((1, N), F32) for _ in post_accs]
    scratch = [pltpu.VMEM((tm, tn) if nk > 1 else (8, LANES), F32)]
    grid = (N // tn, T // tm, nk)
    sem = ("arbitrary",) * 3 if (n_acc or rider is not None) else ("parallel", "parallel", "arbitrary")
    params = pltpu.CompilerParams(dimension_semantics=sem, vmem_limit_bytes=VMEM_LIMIT_BYTES,
                                  has_side_effects=rider is not None)
    if rider is not None:
        r_in, r_out, r_sems = rider.specs()
        body = rider.wrap(body, len(in_specs), len(out_specs), 3)
        in_specs, out_specs, out_shape = in_specs + r_in, out_specs + r_out, out_shape + rider.out_shape
        scratch, args = scratch + r_sems, args + list(rider.ins)
    res = pl.pallas_call(body, name=name, grid=grid, in_specs=in_specs, out_specs=out_specs, out_shape=out_shape,
                         scratch_shapes=scratch, compiler_params=params)(*args)
    main = (list(res[:n_out]), list(res[n_out:n_out + n_acc])) if fused else res[0]
    return main if rider is None else (main, list(res[n_out + n_acc:]))


def matmul_tn(a, b, *, name, tk=1024, rider=None, into=None):
    T, M = a.shape
    T2, N = b.shape
    tk = min(tk, T)
    assert T == T2 and T % tk == 0
    if into is not None:
        assert rider is None
        buf, r0, c0, split = into
        rows_s, cols_s = (M // 4, N) if split == 0 else (M, N // 4)
        tm, tn = _tile(rows_s, 1024), _tile(cols_s, 1152)
        assert r0 % tm == 0 and c0 % tn == 0

        def part_block(m, n, k):
            if split == 0:
                return (m * tm) // rows_s, (r0 + (m * tm) % rows_s) // tm, c0 // tn + n
            return (n * tn) // cols_s, r0 // tm + m, (c0 + (n * tn) % cols_s) // tn

        def body_into(a_ref, b_ref, buf_ref, o_ref):
            k = pl.program_id(2)
            part = lax.dot_general(a_ref[...], b_ref[...], (((0,), (0,)), ((), ())), preferred_element_type=F32)

            @pl.when(k == 0)
            def _():
                o_ref[...] = part

            @pl.when(k > 0)
            def _():
                o_ref[...] += part

        return pl.pallas_call(
            body_into, name=name, grid=(M // tm, N // tn, T // tk),
            in_specs=[pl.BlockSpec((tk, tm), lambda m, n, k: (k, m)), pl.BlockSpec((tk, tn), lambda m, n, k: (k, n)),
                      pl.BlockSpec(memory_space=pl.ANY)],
            out_specs=pl.BlockSpec((None, tm, tn), part_block),
            out_shape=jax.ShapeDtypeStruct(buf.shape, buf.dtype), input_output_aliases={2: 0},
            compiler_params=_cparams(("parallel", "parallel", "arbitrary")))(a, b, buf)
    tm = _tile(M, 1024)
    tn = _tile(N, 1152)

    def body(a_ref, b_ref, o_ref):
        k = pl.program_id(2)
        part = lax.dot_general(a_ref[...], b_ref[...], (((0,), (0,)), ((), ())), preferred_element_type=F32)

        @pl.when(k == 0)
        def _():
            o_ref[...] = part

        @pl.when(k > 0)
        def _():
            o_ref[...] += part

    in_specs = [pl.BlockSpec((tk, tm), lambda m, n, k: (k, m)), pl.BlockSpec((tk, tn), lambda m, n, k: (k, n))]
    out_specs = [pl.BlockSpec((tm, tn), lambda m, n, k: (m, n))]
    out_shape = [jax.ShapeDtypeStruct((M, N), F32)]
    grid = (M // tm, N // tn, T // tk)
    if rider is None:
        return pl.pallas_call(body, name=name, grid=grid, in_specs=in_specs, out_specs=out_specs, out_shape=out_shape,
                              compiler_params=_cparams(("parallel", "parallel", "arbitrary")))(a, b)[0]
    r_in, r_out, r_sems = rider.specs()
    res = pl.pallas_call(
        rider.wrap(body, 2, 1, 3), name=name, grid=grid, in_specs=in_specs + r_in, out_specs=out_specs + r_out,
        out_shape=out_shape + rider.out_shape, scratch_shapes=r_sems,
        compiler_params=pltpu.CompilerParams(dimension_semantics=("arbitrary",) * 3,
                                             vmem_limit_bytes=VMEM_LIMIT_BYTES, has_side_effects=True),
    )(a, b, *rider.ins)
    return res[0], list(res[1:])


def rowwise(fn, rows, vecs, outs, accs=(), *, name, tm=512, rider=None):
    rows = [r if isinstance(r, tuple) else (r, 0, r.shape[1]) for r in rows]
    T = rows[0][0].shape[0]
    tm = min(tm, T)
    assert T % tm == 0
    n_rows, n_vecs, n_outs, n_accs = len(rows), len(vecs), len(outs), len(accs)

    def body(*refs):
        row_refs = refs[:n_rows]
        vec_refs = refs[n_rows:n_rows + n_vecs]
        out_refs = refs[n_rows + n_vecs:n_rows + n_vecs + n_outs]
        acc_refs = refs[n_rows + n_vecs + n_outs:]
        out_vals, acc_vals = fn([r[...] for r in row_refs], [v[...] for v in vec_refs])
        assert len(out_vals) == n_outs and len(acc_vals) == n_accs
        for r, val in zip(out_refs, out_vals):
            r[...] = val.astype(r.dtype)
        if n_accs:
            i = pl.program_id(0)

            @pl.when(i == 0)
            def _():
                for r in acc_refs:
                    r[...] = jnp.zeros_like(r)

            for r, val in zip(acc_refs, acc_vals):
                r[...] += val

    in_specs = []
    for arr, off, width in rows:
        assert off % width == 0
        in_specs.append(pl.BlockSpec((tm, width), functools.partial(lambda i, blk: (i, blk), blk=off // width)))
    for v in vecs:
        in_specs.append(pl.BlockSpec(v.shape, lambda i: (0, 0)))
    out_specs = [pl.BlockSpec((tm, w), lambda i: (i, 0)) for w, _ in outs]
    out_specs += [pl.BlockSpec((1, w), lambda i: (0, 0)) for w in accs]
    out_shape = [jax.ShapeDtypeStruct((T, w), dt) for w, dt in outs]
    out_shape += [jax.ShapeDtypeStruct((1, w), F32) for w in accs]
    args = [r[0] for r in rows] + list(vecs)
    if rider is None:
        res = pl.pallas_call(body, name=name, grid=(T // tm,), in_specs=in_specs, out_specs=out_specs,
                             out_shape=out_shape,
                             compiler_params=_cparams(("arbitrary",) if n_accs else ("parallel",)))(*args)
        return res[:n_outs], res[n_outs:]
    r_in, r_out, r_sems = rider.specs()
    res = pl.pallas_call(
        rider.wrap(body, len(in_specs), len(out_specs), 1), name=name, grid=(T // tm,), in_specs=in_specs + r_in,
        out_specs=out_specs + r_out, out_shape=out_shape + rider.out_shape, scratch_shapes=r_sems,
        compiler_params=pltpu.CompilerParams(dimension_semantics=("arbitrary",), vmem_limit_bytes=VMEM_LIMIT_BYTES,
                                             has_side_effects=True),
    )(*args, *rider.ins)
    return res[:n_outs], res[n_outs:n_outs + n_accs], list(res[n_outs + n_accs:])


def _colsum(x):
    return jnp.sum(x, axis=0, keepdims=True)


def _sigmoid(x):
    return 1.0 / (1.0 + jnp.exp(-x))


def _ln_stats(z):
    mu = jnp.mean(z, axis=-1, keepdims=True)
    zc = z - mu
    var = jnp.mean(zc * zc, axis=-1, keepdims=True)
    return zc * lax.rsqrt(var + LN_EPS)


def _ln_bwd(zhat_src, dy, g):
    mu = jnp.mean(zhat_src, axis=-1, keepdims=True)
    zc = zhat_src - mu
    var = jnp.mean(zc * zc, axis=-1, keepdims=True)
    rstd = lax.rsqrt(var + LN_EPS)
    zh = zc * rstd
    dzh = dy * g
    dz = rstd * (dzh - jnp.mean(dzh, axis=-1, keepdims=True) - zh * jnp.mean(dzh * zh, axis=-1, keepdims=True))
    return dz, _colsum(dy * zh), _colsum(dy)


def _hg_constants():
    r = np.arange(HG_TILE)
    same = (r[:, None] // HG_BLK) == (r[None, :] // HG_BLK)
    lower = (same & (r[None, :] <= r[:, None])).astype(np.float32)
    upper = (same & (r[None, :] >= r[:, None])).astype(np.float32)
    total = same.astype(np.float32)
    c = np.arange(2 * HG_DIM)
    bd = ((c[:, None] // HG_DIM) == (c[None, :] // HG_DIM)).astype(np.float32)
    pair_t = np.array([t for t, _ in _HG_PAIRS])
    pair_s = np.array([s for _, s in _HG_PAIRS])
    sel_t = (pair_t[None, :] == np.arange(HG_BLK)[:, None]).astype(np.float32)
    sel_s = (pair_s[None, :] == np.arange(HG_BLK)[:, None]).astype(np.float32)
    as_bf = lambda m: jnp.asarray(m, dtype=BF16)
    return as_bf(lower), as_bf(upper), as_bf(total), as_bf(bd), as_bf(sel_t), as_bf(sel_s)


_HG_HALF = HG_BLK // 2
_HG_PAIRS = ([(t, s) for t in range(_HG_HALF, HG_BLK) for s in range(HG_BLK)]
             + [(t, s) for t in range(_HG_HALF) for s in range(_HG_HALF)])
HG_STACK = len(_HG_PAIRS)
_HG_SLABS = ([((t - _HG_HALF) * HG_BLK, (t,), HG_BLK) for t in range(_HG_HALF, HG_BLK)]
             + [(_HG_HALF * HG_BLK + t * _HG_HALF, (t, t + 1), _HG_HALF) for t in range(0, _HG_HALF, 2)])


def _stack_by_s(x):
    return jnp.concatenate([x] * _HG_HALF + [x[:_HG_HALF]] * _HG_HALF, axis=0)


def _stack_by_t(x):
    w = x.shape[1]
    return jnp.concatenate([jnp.broadcast_to(x[t:t + 1], (HG_BLK, w)) for t in range(_HG_HALF, HG_BLK)]
                           + [jnp.broadcast_to(x[t:t + 1], (_HG_HALF, w)) for t in range(_HG_HALF)], axis=0)


def _keep_bf16_bits(x):
    bits = lax.bitcast_convert_type(x, jnp.int32) & jnp.int32(-65536)
    return lax.bitcast_convert_type(bits, F32)


def _head_sums(stack_ref, slot, bd):
    pair = bd.shape[0]
    return jnp.concatenate([jnp.dot(stack_ref[slot, :, c0:c0 + pair], bd, preferred_element_type=F32)
                            for c0 in range(0, stack_ref.shape[2], pair)], axis=1)


def _split3(x):
    hi = _keep_bf16_bits(x)
    r1 = x - hi
    mid = _keep_bf16_bits(r1)
    lo = _keep_bf16_bits(r1 - mid)
    return hi.astype(BF16), mid.astype(BF16), lo.astype(BF16)


def _dot3(m01, x):
    hi, mid, lo = _split3(x)
    d = lambda p: jnp.dot(m01, p, preferred_element_type=F32)
    return (d(lo) + d(mid)) + d(hi)


def _hg_prologue(hq, hf, lb, lower, total):
    sq = _sigmoid(hq)
    q = hq * sq
    sg = _sigmoid(hf)
    f = lb + (1.0 - lb) * sg
    g = jnp.log(f)
    k = 1.0 - f
    b = _dot3(lower, g)
    bl = _dot3(total, g)
    return q, k, f, sg, sq, b, bl


def _stack16(fn):
    return [fn(t) for t in range(HG_BLK)]


def hgrn2_fwd(proj, offs, lb, n_seq, seq, *, name, rider=None):
    T = n_seq * seq
    W = HG_HEADS * HG_DIM
    n_tiles = seq // HG_TILE
    nb = HG_TILE // HG_BLK
    lower, _, total, bd, sel_t, _ = _hg_constants()

    def body(hq_ref, hf_ref, hi_ref, lb_ref, lower_ref, total_ref, bd_ref, selt_ref,
             o_ref, st_out_ref,
             st_ref, q_s, k_s, v_s, b_s, qt_s, kt_s, d_s, p_s):
        @pl.when(pl.program_id(1) == 0)
        def _():
            st_ref[...] = jnp.zeros_like(st_ref)

        q, k, _, _, _, b, bl = _hg_prologue(hq_ref[...], hf_ref[...], lb_ref[...], lower_ref[...], total_ref[...])
        q_s[...] = q
        k_s[...] = k
        v_s[...] = hi_ref[...]
        b_s[...] = b
        qt_s[...] = q * jnp.exp(b)
        kt_s[...] = k * jnp.exp(jnp.minimum(bl - b, 0.0))
        d_s[...] = jnp.exp(bl)
        rowi = lax.broadcasted_iota(jnp.int32, (HG_BLK, W), 0)

        def block(i, slot):
            r0 = pl.multiple_of(i * HG_BLK, HG_BLK)
            rows = pl.ds(r0, HG_BLK)
            qi, ki, vi, bi = q_s[rows, :], k_s[rows, :], v_s[rows, :], b_s[rows, :]
            for off, ts, n in _HG_SLABS:
                slab = [jnp.where(rowi[:n] <= t, jnp.exp(jnp.minimum(bi[t:t + 1, :] - bi[:n], 0.0)), 0.0)
                        * qi[t:t + 1, :] * ki[:n] for t in ts]
                p_s[slot, pl.ds(off, HG_BLK), :] = jnp.concatenate(slab, axis=0).astype(BF16)
            a_b = _head_sums(p_s, slot, bd_ref[...])
            o_blk = jnp.dot(selt_ref[...], (a_b * _stack_by_s(vi)).astype(BF16), preferred_element_type=F32)
            qti, kti, di = qt_s[rows, :], kt_s[rows, :], d_s[rows, :]
            outs = []
            for h in range(HG_HEADS):
                hs = slice(h * HG_DIM, (h + 1) * HG_DIM)
                st_h = st_ref[hs, :]
                st_out_ref[i, hs, :] = st_h
                outs.append(lax.dot_general(qti[:, hs].astype(BF16), st_h.astype(BF16),
                                            (((1,), (1,)), ((), ())), preferred_element_type=F32))
                upd = lax.dot_general(vi[:, hs].astype(BF16), kti[:, hs].astype(BF16),
                                      (((0,), (0,)), ((), ())), preferred_element_type=F32)
                st_ref[hs, :] = st_h * di[0:1, hs] + upd
            o_ref[rows, :] = o_blk + jnp.concatenate(outs, axis=1)

        def some_blocks(jj, carry):
            for slot in range(HG_SLOTS):
                block(HG_SLOTS * jj + slot, slot)
            return carry

        lax.fori_loop(0, nb // HG_SLOTS, some_blocks, 0)

    col = lambda off: functools.partial(lambda s, t, blk: (s * n_tiles + t, blk), blk=off // W)
    const = lambda m: pl.BlockSpec(m.shape, lambda s, t: (0, 0))
    tile_f32 = pltpu.VMEM((HG_TILE, W), F32)
    in_specs = [pl.BlockSpec((HG_TILE, W), col(offs[0])), pl.BlockSpec((HG_TILE, W), col(offs[1])),
                pl.BlockSpec((HG_TILE, W), col(offs[2])), const(lb), const(lower), const(total), const(bd),
                const(sel_t)]
    out_specs = [pl.BlockSpec((HG_TILE, W), lambda s, t: (s * n_tiles + t, 0)),
                 pl.BlockSpec((nb, W, HG_DIM), lambda s, t: (s * n_tiles + t, 0, 0))]
    out_shape = [jax.ShapeDtypeStruct((T, W), F32), jax.ShapeDtypeStruct((T // HG_BLK, W, HG_DIM), F32)]
    scratch = [pltpu.VMEM((W, HG_DIM), F32)] + [tile_f32] * 7 + [pltpu.VMEM((HG_SLOTS, HG_STACK, W), BF16)]
    args = [proj, proj, proj, lb, lower, total, bd, sel_t]
    params = _cparams(("arbitrary", "arbitrary"))
    if rider is not None:
        r_in, r_out, r_sems = rider.specs()
        body = rider.wrap(body, len(in_specs), len(out_specs), 2)
        in_specs, out_specs, out_shape = in_specs + r_in, out_specs + r_out, out_shape + rider.out_shape
        scratch, args = scratch + r_sems, args + rider.ins
        params = pltpu.CompilerParams(dimension_semantics=("arbitrary", "arbitrary"),
                                      vmem_limit_bytes=VMEM_LIMIT_BYTES, has_side_effects=True)
    res = pl.pallas_call(body, name=name, grid=(n_seq, n_tiles), in_specs=in_specs, out_specs=out_specs,
                         out_shape=out_shape, scratch_shapes=scratch, compiler_params=params)(*args)
    return res[0], res[1], list(res[2:])


def hgrn2_bwd(proj, offs, lb, do, states, n_seq, seq, *, name):
    T = n_seq * seq
    W = HG_HEADS * HG_DIM
    n_tiles = seq // HG_TILE
    nb = HG_TILE // HG_BLK
    lower, upper, total, bd, sel_t, sel_s = _hg_constants()

    def body(hq_ref, hf_ref, hi_ref, do_ref, st_in_ref, lb_ref, lower_ref, upper_ref, total_ref, bd_ref,
             selt_ref, sels_ref,
             dhq_ref, dhf_ref, dhi_ref, dlb_ref,
             dst_ref, q_s, k_s, v_s, b_s, qt_s, kt_s, d_s, eb_s, ekb_s, dq_s, dk_s, db_s, dv_s,
             p_s, e_s, w_s):
        first = jnp.logical_and(pl.program_id(0) == 0, pl.program_id(1) == 0)

        @pl.when(first)
        def _():
            dlb_ref[...] = jnp.zeros_like(dlb_ref)

        @pl.when(pl.program_id(1) == 0)
        def _():
            dst_ref[...] = jnp.zeros_like(dst_ref)

        hq, lbv = hq_ref[...], lb_ref[...]
        q, k, f, sg, sq, b, bl = _hg_prologue(hq, hf_ref[...], lbv, lower_ref[...], total_ref[...])
        eb = jnp.exp(b)
        ekb = jnp.exp(jnp.minimum(bl - b, 0.0))
        q_s[...] = q
        k_s[...] = k
        v_s[...] = hi_ref[...]
        b_s[...] = b
        eb_s[...] = eb
        ekb_s[...] = ekb
        qt_s[...] = q * eb
        kt_s[...] = k * ekb
        d_s[...] = jnp.exp(bl)
        rowi = lax.broadcasted_iota(jnp.int32, (HG_BLK, W), 0)
        last_row = rowi == HG_BLK - 1

        def block(i, slot):
            r0 = pl.multiple_of(i * HG_BLK, HG_BLK)
            rows = pl.ds(r0, HG_BLK)
            qi, ki, vi, bi, doi = q_s[rows, :], k_s[rows, :], v_s[rows, :], b_s[rows, :], do_ref[rows, :]
            for off, ts, n in _HG_SLABS:
                es = [jnp.where(rowi[:n] <= t, jnp.exp(jnp.minimum(bi[t:t + 1, :] - bi[:n], 0.0)), 0.0) for t in ts]
                sl = pl.ds(off, HG_BLK)
                e_s[slot, sl, :] = jnp.concatenate(es, axis=0)
                p_s[slot, sl, :] = jnp.concatenate([e * qi[t:t + 1, :] * ki[:n] for e, t in zip(es, ts)],
                                                   axis=0).astype(BF16)
                w_s[slot, sl, :] = jnp.concatenate([doi[t:t + 1, :] * vi[:n] for t in ts], axis=0).astype(BF16)
            a_b = _head_sums(p_s, slot, bd_ref[...])
            da_b = _head_sums(w_s, slot, bd_ref[...])
            x = da_b * e_s[slot]
            dq_in = jnp.dot(selt_ref[...], (x * _stack_by_s(ki)).astype(BF16), preferred_element_type=F32)
            dk_in = jnp.dot(sels_ref[...], (x * _stack_by_t(qi)).astype(BF16), preferred_element_type=F32)
            dv_in = jnp.dot(sels_ref[...], (a_b * _stack_by_t(doi)).astype(BF16), preferred_element_type=F32)
            qti, kti, di = qt_s[rows, :], kt_s[rows, :], d_s[rows, :]
            dqt, dkt, dvt, dd = [], [], [], []
            for h in range(HG_HEADS):
                hs = slice(h * HG_DIM, (h + 1) * HG_DIM)
                st_h = st_in_ref[i, hs, :]
                dst_h = dst_ref[hs, :]
                do_h, v_h = doi[:, hs].astype(BF16), vi[:, hs].astype(BF16)
                dst_b = dst_h.astype(BF16)
                dqt.append(jnp.dot(do_h, st_h.astype(BF16), preferred_element_type=F32))
                dkt.append(jnp.dot(v_h, dst_b, preferred_element_type=F32))
                dvt.append(lax.dot_general(kti[:, hs].astype(BF16), dst_b, (((1,), (1,)), ((), ())),
                                           preferred_element_type=F32))
                dd.append(jnp.sum(dst_h * st_h, axis=0, keepdims=True))
                upd = lax.dot_general(do_h, qti[:, hs].astype(BF16), (((0,), (0,)), ((), ())),
                                      preferred_element_type=F32)
                dst_ref[hs, :] = dst_h * di[0:1, hs] + upd
            dqt = jnp.concatenate(dqt, axis=1)
            dkt = jnp.concatenate(dkt, axis=1)
            dvt = jnp.concatenate(dvt, axis=1)
            dd = jnp.concatenate(dd, axis=1)
            dbl = jnp.sum(dkt * kti, axis=0, keepdims=True) + dd * di[0:1, :]
            db = qi * dq_in - ki * dk_in + dqt * qti - dkt * kti
            db_s[rows, :] = db + jnp.where(last_row, dbl, 0.0)
            dq_s[rows, :] = dq_in + dqt * eb_s[rows, :]
            dk_s[rows, :] = dk_in + dkt * ekb_s[rows, :]
            dv_s[rows, :] = dv_in + dvt

        def some_blocks(jj, carry):
            for slot in range(HG_SLOTS):
                block(nb - 1 - slot - HG_SLOTS * jj, slot)
            return carry

        lax.fori_loop(0, nb // HG_SLOTS, some_blocks, 0)

        dg = _dot3(upper_ref[...], db_s[...])
        dhq_ref[...] = (dq_s[...] * (sq * (1.0 + hq * (1.0 - sq)))).astype(dhq_ref.dtype)
        df = dg / f - dk_s[...]
        dhf_ref[...] = (df * (1.0 - lbv) * (sg * (1.0 - sg))).astype(dhf_ref.dtype)
        dhi_ref[...] = dv_s[...].astype(dhi_ref.dtype)
        dlb_ref[...] += _colsum(df * (1.0 - sg))

    rev = lambda s, t: s * n_tiles + (n_tiles - 1 - t)
    col = lambda off: functools.partial(lambda s, t, blk: (rev(s, t), blk), blk=off // W)
    const = lambda m: pl.BlockSpec(m.shape, lambda s, t: (0, 0))
    row = pl.BlockSpec((HG_TILE, W), lambda s, t: (rev(s, t), 0))
    tile_f32 = pltpu.VMEM((HG_TILE, W), F32)
    n2 = HG_STACK
    return pl.pallas_call(
        body, name=name,
        grid=(n_seq, n_tiles),
        in_specs=[pl.BlockSpec((HG_TILE, W), col(offs[0])), pl.BlockSpec((HG_TILE, W), col(offs[1])),
                  pl.BlockSpec((HG_TILE, W), col(offs[2])), row,
                  pl.BlockSpec((nb, W, HG_DIM), lambda s, t: (rev(s, t), 0, 0)),
                  const(lb), const(lower), const(upper), const(total), const(bd), const(sel_t), const(sel_s)],
        out_specs=[row, row, row, pl.BlockSpec((1, W), lambda s, t: (0, 0))],
        out_shape=[jax.ShapeDtypeStruct((T, W), BF16)] * 3 + [jax.ShapeDtypeStruct((1, W), F32)],
        scratch_shapes=[pltpu.VMEM((W, HG_DIM), F32)] + [tile_f32] * 13
                       + [pltpu.VMEM((HG_SLOTS, n2, W), BF16), pltpu.VMEM((HG_SLOTS, n2, W), F32),
                          pltpu.VMEM((HG_SLOTS, n2, W), BF16)],
        compiler_params=_cparams(("arbitrary", "arbitrary")),
    )(proj, proj, proj, do, states, lb, lower, upper, total, bd, sel_t, sel_s)


def _diag_mask(tq):
    return lax.broadcasted_iota(jnp.int32, (tq, tq), 1) <= lax.broadcasted_iota(jnp.int32, (tq, tq), 0)


def _qk(q, k):
    return lax.dot_general(q, k, (((1,), (1,)), ((), ())), preferred_element_type=F32)


def _causal_pairs(n, sweeps=1, by_key=False):
    if by_key:
        rows = [(i, j, 0) for j in range(n) for i in range(j, n)]
    else:
        rows = [(i, j, s) for i in range(n) for s in range(sweeps) for j in range(i + 1)]
    return tuple(jnp.asarray(np.array([r[c] for r in rows], np.int32)) for c in range(3))


def _fox_placement(fh):
    hw, wa = fh * FOX_HDIM, fh * FOX_AUG
    pq, pk = np.zeros((hw, wa), np.float32), np.zeros((hw, wa), np.float32)
    aq, ak = np.zeros((3 * LANES, wa), np.float32), np.zeros((3 * LANES, wa), np.float32)
    oq, ok = np.zeros((1, wa), np.float32), np.zeros((1, wa), np.float32)
    for h in range(fh):
        src, dst = np.arange(h * FOX_HDIM, (h + 1) * FOX_HDIM), np.arange(h * FOX_AUG, h * FOX_AUG + FOX_HDIM)
        pq[src, dst] = FOX_HDIM ** -0.5
        pk[src, dst] = 1.0
        gate = h * FOX_AUG + FOX_HDIM
        for r in range(3):
            aq[r * LANES + h, gate + r] = 1.0
            ak[r * LANES + h, gate + 3 + r] = -1.0
        oq[0, gate + 3:gate + 6] = 1.0
        ok[0, gate:gate + 3] = 1.0
    bf = lambda m: jnp.asarray(m, dtype=BF16)
    return {"pq": bf(pq), "pk": bf(pk), "aq": bf(aq), "ak": bf(ak), "oq": jnp.asarray(oq), "ok": jnp.asarray(ok),
            "pqt": bf(pq.T), "pkt": bf(pk.T)}


def _fox_specs(tq, fh, heads=1):
    groups = fh // heads

    def spec(tab):
        return pl.BlockSpec((None, tq, heads * FOX_AUG), lambda b, t, *tabs: (b // groups, tabs[tab][t], b % groups))
    return spec(0), spec(1)


def fox_fwd(qa, ka, va, *, name):
    n_seq, S, width = qa.shape
    fh = width // FOX_AUG
    nh = FOX_FWD_HEADS
    BH = n_seq * fh // nh
    tq = min(FOX_TQ, S)
    itab, jtab, _ = _causal_pairs(S // tq)

    def body(itab_ref, jtab_ref, q_ref, k_ref, v_ref, o_ref, ox_ref, lse_ref, *scratch):
        t = pl.program_id(1)
        i, j = itab_ref[t], jtab_ref[t]
        per_head = [scratch[4 * h:4 * h + 4] for h in range(nh)]

        @pl.when(j == 0)
        def _():
            for m_s, l_s, acc_s, acc_lo_s in per_head:
                m_s[...] = jnp.full_like(m_s, NEG_INF)
                l_s[...] = jnp.zeros_like(l_s)
                acc_s[...] = jnp.zeros_like(acc_s)
                acc_lo_s[...] = jnp.zeros_like(acc_lo_s)

        def step(on_diagonal):
            for h, (m_s, l_s, acc_s, acc_lo_s) in enumerate(per_head):
                lanes = slice(h * FOX_AUG, (h + 1) * FOX_AUG)
                s = _qk(q_ref[:, lanes], k_ref[:, lanes])
                if on_diagonal:
                    s = jnp.where(_diag_mask(tq), s, NEG_INF)
                m_prev = m_s[...]
                m_new = jnp.maximum(m_prev, jnp.max(s, axis=-1, keepdims=True))
                alpha = jnp.exp(m_prev - m_new)
                p = jnp.exp(s - m_new[:, 0:1])
                p_hi = p.astype(BF16)
                p_lo = (p - p_hi.astype(F32)).astype(BF16)
                v = v_ref[:, lanes]
                l_s[...] = alpha * l_s[...] + jnp.sum(p, axis=-1, keepdims=True)
                acc_s[...] = alpha * acc_s[...] + jnp.dot(p_hi, v, preferred_element_type=F32)
                acc_lo_s[...] = alpha * acc_lo_s[...] + jnp.dot(p_lo, v, preferred_element_type=F32)
                m_s[...] = m_new

        @pl.when(j < i)
        def _():
            step(False)

        @pl.when(j == i)
        def _():
            step(True)
            for h, (m_s, l_s, acc_s, acc_lo_s) in enumerate(per_head):
                lanes = slice(h * FOX_AUG, (h + 1) * FOX_AUG)
                inv_l = 1.0 / l_s[...]
                o_ref[:, lanes] = (acc_s[...] * inv_l).astype(o_ref.dtype)
                ox_ref[:, lanes] = (acc_s[...] + acc_lo_s[...]) * inv_l
                lse_ref[:, lanes] = m_s[...] + jnp.log(l_s[...])

    qspec, kspec = _fox_specs(tq, fh, nh)
    wide = jax.ShapeDtypeStruct((n_seq, S, width), F32)
    return pl.pallas_call(
        body, name=name,
        grid_spec=pltpu.PrefetchScalarGridSpec(
            num_scalar_prefetch=2, grid=(BH, itab.shape[0]),
            in_specs=[qspec, kspec, kspec],
            out_specs=[qspec, qspec, qspec],
            scratch_shapes=[pltpu.VMEM((tq, LANES), F32)] * (4 * nh)),
        out_shape=[jax.ShapeDtypeStruct((n_seq, S, width), BF16), wide, wide],
        compiler_params=_cparams(("parallel", "arbitrary")),
    )(itab, jtab, qa, ka, va)


def _fox_ds(q, k, v, do, ox, lse, on_diagonal):
    s = _qk(q, k)
    if on_diagonal:
        s = jnp.where(_diag_mask(s.shape[0]), s, NEG_INF)
    p = jnp.exp(s - lse[:, 0:1])
    delta = jnp.sum(do.astype(F32) * ox, axis=-1, keepdims=True)
    return p, p * (_qk(do, v) - delta)


def fox_bwd(qa, ka, va, do, ox, lse, *, name):
    n_seq, S, width = qa.shape
    fh = width // FOX_AUG
    BH = n_seq * fh
    tq = min(FOX_TQ, S)
    itab, jtab, _ = _causal_pairs(S // tq)

    def body(itab_ref, jtab_ref, q_ref, k_ref, v_ref, do_ref, ox_ref, lse_ref, dq_ref, dk_ref, dv_ref, dsum_ref):
        t = pl.program_id(1)
        i, j = itab_ref[t], jtab_ref[t]

        @pl.when(t == 0)
        def _():
            dq_ref[...] = jnp.zeros_like(dq_ref)
            dk_ref[...] = jnp.zeros_like(dk_ref)
            dv_ref[...] = jnp.zeros_like(dv_ref)
            dsum_ref[...] = jnp.zeros_like(dsum_ref)

        q_rows = pl.ds(pl.multiple_of(i * tq, tq), tq)
        k_rows = pl.ds(pl.multiple_of(j * tq, tq), tq)

        def step(on_diagonal):
            q, k, do = q_ref[...], k_ref[...], do_ref[...]
            p, ds = _fox_ds(q, k, v_ref[...], do, ox_ref[...], lse_ref[...], on_diagonal)
            ds_b = ds.astype(BF16)
            tn = (((0,), (0,)), ((), ()))
            dq_ref[q_rows, :] += jnp.dot(ds_b, k, preferred_element_type=F32)
            dk_ref[k_rows, :] += lax.dot_general(ds_b, q, tn, preferred_element_type=F32)
            dv_ref[k_rows, :] += lax.dot_general(p.astype(BF16), do, tn, preferred_element_type=F32)
            dsum_ref[:, k_rows] += _colsum(ds)

        @pl.when(j < i)
        def _():
            step(False)

        @pl.when(j == i)
        def _():
            step(True)

    qspec, kspec = _fox_specs(tq, fh)
    whole = pl.BlockSpec((None, S, FOX_AUG), lambda b, t, it, jt: (b // fh, 0, b % fh))
    wide = jax.ShapeDtypeStruct((n_seq, S, width), F32)
    return pl.pallas_call(
        body, name=name,
        grid_spec=pltpu.PrefetchScalarGridSpec(
            num_scalar_prefetch=2, grid=(BH, itab.shape[0]),
            in_specs=[qspec, kspec, kspec, qspec, qspec, qspec],
            out_specs=[whole, whole, whole, pl.BlockSpec((None, 1, S), lambda b, t, it, jt: (b, 0, 0))]),
        out_shape=[wide, wide, wide, jax.ShapeDtypeStruct((BH, 1, S), F32)],
        compiler_params=_cparams(("parallel", "arbitrary")),
    )(itab, jtab, qa, ka, va, do, ox, lse)


def seq_cumsum(x, n_seq, seq, *, reverse, name):
    T, C = x.shape
    tb = min(256, seq)
    n = seq // tb
    r = np.arange(tb)
    tri = (r[None, :] >= r[:, None]) if reverse else (r[None, :] <= r[:, None])
    tri = jnp.asarray(tri.astype(np.float32), dtype=BF16)

    def body(x_ref, tri_ref, o_ref, carry_s):
        @pl.when(pl.program_id(1) == 0)
        def _():
            carry_s[...] = jnp.zeros_like(carry_s)

        xv = x_ref[...]
        o_ref[...] = _dot3(tri_ref[...], xv) + carry_s[...]
        carry_s[...] += _colsum(xv)

    blk = (lambda s, t: (s * n + (n - 1 - t), 0)) if reverse else (lambda s, t: (s * n + t, 0))
    return pl.pallas_call(
        body, name=name,
        grid=(n_seq, n),
        in_specs=[pl.BlockSpec((tb, C), blk), pl.BlockSpec((tb, tb), lambda s, t: (0, 0))],
        out_specs=pl.BlockSpec((tb, C), blk),
        out_shape=jax.ShapeDtypeStruct((T, C), F32),
        scratch_shapes=[pltpu.VMEM((1, C), F32)],
        compiler_params=_cparams(("arbitrary", "arbitrary")),
    )(x, tri)


def _place():
    return lax.axis_index("x"), lax.axis_index("y"), lax.axis_index("c")


def _other_chips(x, y):
    return [(1 - x, y), (x, 1 - y), (1 - x, 1 - y)]


def _hbm_call(body, ins, out_shape, n_sems, *, name):
    hbm = pl.BlockSpec(memory_space=pl.ANY)
    return pl.pallas_call(
        body, name=name,
        in_specs=[hbm] * len(ins), out_specs=[hbm] * len(out_shape), out_shape=out_shape,
        scratch_shapes=[pltpu.SemaphoreType.DMA((n_sems,)), pltpu.SemaphoreType.DMA((n_sems,)),
                        pltpu.SemaphoreType.DMA((len(ins),))],
        compiler_params=pltpu.CompilerParams(has_side_effects=True),
    )(*ins)


def allgather_chips(shards, *, name):
    return _exchange_call(allgather_rider(shards), name=name)


def _allgather_ops(x_refs, o_refs, send_sems, recv_sems, local_sems):
    def copies():
        x, y, c = _place()
        me = 2 * x + y
        chips = _other_chips(x, y)
        own, first, passed, landed, handed = [], [], [], [], []
        for b, (x_ref, o_ref) in enumerate(zip(x_refs, o_refs)):
            half = x_ref.shape[0] // 2
            mine, theirs = pl.ds(c * half, half), pl.ds((1 - c) * half, half)
            own.append(pltpu.make_async_copy(x_ref, o_ref.at[me], local_sems.at[b]))

            def copy(k, src, chip, rows, to, o_ref=o_ref, b=b):
                return pltpu.make_async_remote_copy(src_ref=src, dst_ref=o_ref.at[2 * chip[0] + chip[1], rows],
                                                    send_sem=send_sems.at[6 * b + k], recv_sem=recv_sems.at[6 * b + k],
                                                    device_id=to, device_id_type=MESH)
            for j, chip in enumerate(chips):
                first.append(copy(j, x_ref.at[mine], (x, y), mine, (*chip, c)))
                landed.append(copy(j, x_ref.at[mine], chip, mine, (*chip, c)))
                passed.append(copy(3 + j, o_ref.at[2 * chip[0] + chip[1], mine], chip, mine, (x, y, 1 - c)))
                handed.append(copy(3 + j, x_ref.at[mine], chip, theirs, (x, y, 1 - c)))
        return own, first, passed, landed, handed

    def start():
        own, first, _, _, _ = copies()
        for cp in own + first:
            cp.start()

    def finish():
        own, first, passed, landed, handed = copies()
        for arrived, forward in zip(landed, passed):
            arrived.wait_recv()
            forward.start()
        for cp in handed:
            cp.wait_recv()
        for cp in first + passed:
            cp.wait_send()
        for cp in own:
            cp.wait()
    return start, finish


def _scatter_ops(x_refs, o_refs, send_sems, recv_sems, local_sems):
    def copies():
        x, y, c = _place()
        return [pltpu.make_async_remote_copy(
            src_ref=x_ref.at[2 * px + py], dst_ref=o_ref.at[j], send_sem=send_sems.at[3 * b + j],
            recv_sem=recv_sems.at[3 * b + j], device_id=(px, py, c), device_id_type=MESH)
            for b, (x_ref, o_ref) in enumerate(zip(x_refs, o_refs)) for j, (px, py) in enumerate(_other_chips(x, y))]

    def start():
        for cp in copies():
            cp.start()

    def finish():
        sends = copies()
        for cp in sends:
            cp.wait_recv()
        for cp in sends:
            cp.wait_send()
    return start, finish


class Rider(NamedTuple):
    ins: list
    out_shape: list
    n_sems: int
    ops: object

    def specs(self):
        hbm = pl.BlockSpec(memory_space=pl.ANY)
        sems = [pltpu.SemaphoreType.DMA((self.n_sems,)), pltpu.SemaphoreType.DMA((self.n_sems,)),
                pltpu.SemaphoreType.DMA((len(self.ins),))]
        return [hbm] * len(self.ins), [hbm] * len(self.out_shape), sems

    def wrap(self, body, n_in, n_out, grid_rank):
        k_in, k_out = len(self.ins), len(self.out_shape)

        def carried(*refs):
            ins, r_ins = refs[:n_in], refs[n_in:n_in + k_in]
            outs = refs[n_in + k_in:n_in + k_in + n_out]
            r_outs = refs[n_in + k_in + n_out:n_in + k_in + n_out + k_out]
            scratch, sems = refs[n_in + k_in + n_out + k_out:-3], refs[-3:]
            first = functools.reduce(jnp.logical_and, [pl.program_id(a) == 0 for a in range(grid_rank)])
            last = functools.reduce(jnp.logical_and,
                                    [pl.program_id(a) == pl.num_programs(a) - 1 for a in range(grid_rank)])
            pl.when(first)(lambda: self.ops(r_ins, r_outs, *sems)[0]())
            body(*ins, *outs, *scratch)
            pl.when(last)(lambda: self.ops(r_ins, r_outs, *sems)[1]())
        return carried


def _exchange_call(rider, *, name):
    def body(*refs):
        k = len(rider.ins)
        start, finish = rider.ops(refs[:k], refs[k:k + len(rider.out_shape)], *refs[-3:])
        start()
        finish()
    in_specs, out_specs, sems = rider.specs()
    return pl.pallas_call(body, name=name, in_specs=in_specs, out_specs=out_specs, out_shape=rider.out_shape,
                          scratch_shapes=sems, compiler_params=pltpu.CompilerParams(has_side_effects=True))(*rider.ins)


def allgather_rider(shards):
    assert all(s.shape[0] % (2 * ROW_ALIGN) == 0 for s in shards)
    return Rider(list(shards), [jax.ShapeDtypeStruct((4,) + s.shape, s.dtype) for s in shards], 6 * len(shards),
                 _allgather_ops)


def scatter_rider(parts):
    return Rider(list(parts), [jax.ShapeDtypeStruct((3,) + p.shape[1:], p.dtype) for p in parts], 3 * len(parts),
                 _scatter_ops)


def scatter_chips(parts, *, name):
    return _exchange_call(scatter_rider(parts), name=name)


def swap_cores(vs, *, name):
    nb = len(vs)

    def body(*refs):
        x_refs, o_refs = refs[:nb], refs[nb:2 * nb]
        send_sems, recv_sems, _ = refs[2 * nb:]
        x, y, c = _place()
        copies = [pltpu.make_async_remote_copy(src_ref=x_ref, dst_ref=o_ref, send_sem=send_sems.at[b],
                                               recv_sem=recv_sems.at[b], device_id=(x, y, 1 - c), device_id_type=MESH)
                  for b, (x_ref, o_ref) in enumerate(zip(x_refs, o_refs))]
        for cp in copies:
            cp.start()
        for cp in copies:
            cp.wait()

    return _hbm_call(body, vs, [jax.ShapeDtypeStruct(v.shape, v.dtype) for v in vs], nb, name=name)


def join_halves(halves, *, name):
    nb = len(halves)

    def body(*refs):
        x_refs, o_refs = refs[:nb], refs[nb:2 * nb]
        send_sems, recv_sems, local_sems = refs[2 * nb:]
        x, y, c = _place()
        own, sent, landing = [], [], []
        for b, (x_ref, o_ref) in enumerate(zip(x_refs, o_refs)):
            h = x_ref.shape[0]
            mine, theirs = pl.ds(c * h, h), pl.ds((1 - c) * h, h)

            def to_other_core(rows, x_ref=x_ref, o_ref=o_ref, b=b):
                return pltpu.make_async_remote_copy(src_ref=x_ref, dst_ref=o_ref.at[rows], send_sem=send_sems.at[b],
                                                    recv_sem=recv_sems.at[b], device_id=(x, y, 1 - c),
                                                    device_id_type=MESH)
            own.append(pltpu.make_async_copy(x_ref, o_ref.at[mine], local_sems.at[b]))
            sent.append(to_other_core(mine))
            landing.append(to_other_core(theirs))
        for cp in own + sent:
            cp.start()
        for cp in landing:
            cp.wait_recv()
        for cp in sent:
            cp.wait_send()
        for cp in own:
            cp.wait()

    return _hbm_call(body, halves, [jax.ShapeDtypeStruct((2 * v.shape[0], v.shape[1]), v.dtype) for v in halves], nb,
                     name=name)


def allreduce_small(v, *, name):
    R, C = v.shape

    def body(x_ref, o_ref, gath_ref, send_sems, recv_sems):
        x, y, c = _place()
        me = 4 * x + 2 * y + c
        gath_ref[me] = x_ref[...]
        flips = [(k >> 2 & 1, k >> 1 & 1, k & 1) for k in range(1, 8)]
        sends = []
        for j, (fx, fy, fc) in enumerate(flips):
            peer = (x ^ fx, y ^ fy, c ^ fc)
            cp = pltpu.make_async_remote_copy(src_ref=x_ref, dst_ref=gath_ref.at[me], send_sem=send_sems.at[j],
                                              recv_sem=recv_sems.at[j], device_id=peer, device_id_type=MESH)
            cp.start()
            sends.append(cp)
        for j, (fx, fy, fc) in enumerate(flips):
            peer = (x ^ fx, y ^ fy, c ^ fc)
            pltpu.make_async_remote_copy(src_ref=x_ref, dst_ref=gath_ref.at[4 * peer[0] + 2 * peer[1] + peer[2]],
                                         send_sem=send_sems.at[j], recv_sem=recv_sems.at[j], device_id=peer,
                                         device_id_type=MESH).wait_recv()
        for cp in sends:
            cp.wait_send()
        total = gath_ref[0]
        for d in range(1, 8):
            total = total + gath_ref[d]
        o_ref[...] = total

    vm = pl.BlockSpec(memory_space=pltpu.VMEM)
    out, _ = pl.pallas_call(
        body, name=name,
        in_specs=[vm], out_specs=[vm, vm],
        out_shape=[jax.ShapeDtypeStruct((R, C), F32), jax.ShapeDtypeStruct((8, R, C), F32)],
        scratch_shapes=[pltpu.SemaphoreType.DMA((7,)), pltpu.SemaphoreType.DMA((7,))],
        compiler_params=pltpu.CompilerParams(has_side_effects=True),
    )(v)
    return out


ROW_ALIGN = 16
PACK_W = 1024
SUM_TILE = 512
BIG_WEIGHTS = (("w_in", 1), ("w_a", 1), ("w_b", 1), ("w_o", 0), ("w_ff1", 1), ("w_ff2", 0), ("w_pg", 0), ("w_p", 1))


def _b_layout(d, ple):
    hw, q = d // 2, d // 4
    small = 2 * d + 2 * q
    lay = {"w_ff1": (0, 0, d, d), "w_ff2": (d, 0, d, d), "w_o": (2 * d, 0, q, d), "w_pg": (2 * d + q, 0, q, d),
           "w_a": (small, 0, hw, q), "w_b": (small, q, hw, q), "w_p": (small, 2 * q, ple, q)}
    return lay, small + hw


def pack_a(w_in_shard):
    rows, cols = w_in_shard.shape
    pad = -cols % LANES
    return jnp.concatenate([w_in_shard, jnp.zeros((rows, pad), w_in_shard.dtype)], axis=1)


def pack_b(shards, d):
    hw, q = d // 2, d // 4
    dt = shards["w_a"].dtype
    wp = shards["w_p"]
    wp = jnp.concatenate([wp, jnp.zeros((hw - wp.shape[0], q), dt)], axis=0)
    small = jnp.concatenate([shards["w_a"], shards["w_b"], wp, jnp.zeros((hw, d - 3 * q), dt)], axis=1)
    return jnp.concatenate([shards["w_ff1"], shards["w_ff2"], shards["w_o"], shards["w_pg"], small], axis=0)


def unpack_b(buf, lay):
    return {nm: buf[r0:r0 + rows, c0:c0 + cols] for nm, (r0, c0, rows, cols) in lay.items()}


def _win_layout(d):
    hw = d // 2
    fh = hw // FOX_HDIM
    orig = {"hq": (0, hw), "hf": (hw, hw), "hi": (2 * hw, hw), "hg": (3 * hw, hw), "fq": (4 * hw, hw),
            "fk": (5 * hw, hw), "fv": (6 * hw, hw), "ff": (7 * hw, fh), "ga": (7 * hw + fh, d), "gb": (7 * hw + fh + d, d)}
    order = ["ga", "gb", "hq", "hf", "hi", "hg", "fq", "fk", "fv", "ff"]
    mine, off = {}, 0
    for nm in order:
        width = orig[nm][1] if nm != "ff" else LANES
        mine[nm] = (off, width)
        off += width
    return orig, order, mine, off


def _adam_fn(rows, vecs):
    w, g, m, v = rows
    m2 = ADAM_B1 * m + (1.0 - ADAM_B1) * g
    v2 = ADAM_B2 * v + (1.0 - ADAM_B2) * (g * g)
    m_hat = m2 / (1.0 - ADAM_B1 ** ADAM_STEP)
    v_hat = v2 / (1.0 - ADAM_B2 ** ADAM_STEP)
    delta = -ADAM_LR * (m_hat / (jnp.sqrt(v_hat) + ADAM_EPS) + ADAM_WD * w)
    return [delta, m2, v2], []


def adamw_small(small, p0, ws, ms, vs, *, name):
    n = len(ws)
    hw = p0.shape[1]
    fh = ws[8].shape[1]

    def body(small_ref, p0_ref, *refs):
        w_refs, m_refs, v_refs = refs[:n], refs[n:2 * n], refs[2 * n:3 * n]
        g_out, d_out, m_out, v_out = (refs[(3 + k) * n:(4 + k) * n] for k in range(4))
        sm = small_ref[...]
        p = p0_ref[...]
        d_lb = sm[6:7, hw:2 * hw] * (p * (1.0 - p))
        grads = [sm[r:r + 1, :] for r in range(6)]
        grads += [jnp.concatenate([d_lb, -d_lb], axis=0), sm[6:7, :hw], sm[7:8, :fh]]
        for i in range(n):
            (delta, m2, v2), _ = _adam_fn([w_refs[i][...], grads[i], m_refs[i][...], v_refs[i][...]], [])
            g_out[i][...], d_out[i][...], m_out[i][...], v_out[i][...] = grads[i], delta, m2, v2

    shapes = [jax.ShapeDtypeStruct(w.shape, F32) for w in ws]
    return pl.pallas_call(body, name=name, out_shape=shapes * 4)(small, p0, *ws, *ms, *vs)


def adamw(w, g, m, v, *, name):
    c = w.shape[1]
    (delta, m2, v2), _ = rowwise(_adam_fn, [w, g, m, v], [], [(c, F32)] * 3, name=name, tm=256)
    return delta, m2, v2


def kernel(x, p, ln0_g, ln0_b, w_in, hg_lb, hg_norm_g, fox_fb, w_a, w_b, w_o, ln1_g, ln1_b, w_ff1, w_ff2, w_pg, w_p, ln2_g, ln2_b, loss_target, m_ln0_g, m_ln0_b, m_w_in, m_hg_lb, m_hg_norm_g, m_fox_fb, m_w_a, m_w_b, m_w_o, m_ln1_g, m_ln1_b, m_w_ff1, m_w_ff2, m_w_pg, m_w_p, m_ln2_g, m_ln2_b, v_ln0_g, v_ln0_b, v_w_in, v_hg_lb, v_hg_norm_g, v_fox_fb, v_w_a, v_w_b, v_w_o, v_ln1_g, v_ln1_b, v_w_ff1, v_w_ff2, v_w_pg, v_w_p, v_ln2_g, v_ln2_b):
    n_seq, seq, d = x.shape
    T = n_seq * seq
    hw = d // 2
    fh = hw // FOX_HDIM
    bh = n_seq * fh
    orig, order, mine, n_in = _win_layout(d)

    big = {"w_in": w_in[0], "w_a": w_a[0], "w_b": w_b[0], "w_o": w_o[0], "w_ff1": w_ff1[0], "w_ff2": w_ff2[0],
           "w_pg": w_pg[0], "w_p": w_p[0]}
    big_m = {"w_in": m_w_in[0], "w_a": m_w_a[0], "w_b": m_w_b[0], "w_o": m_w_o[0], "w_ff1": m_w_ff1[0],
             "w_ff2": m_w_ff2[0], "w_pg": m_w_pg[0], "w_p": m_w_p[0]}
    big_v = {"w_in": v_w_in[0], "w_a": v_w_a[0], "w_b": v_w_b[0], "w_o": v_w_o[0], "w_ff1": v_w_ff1[0],
             "w_ff2": v_w_ff2[0], "w_pg": v_w_pg[0], "w_p": v_w_p[0]}
    names = [nm for nm, _ in BIG_WEIGHTS]
    axis = dict(BIG_WEIGHTS)
    ple = w_p.shape[1]
    lay, b_rows = _b_layout(d, ple)
    in_cols = big["w_in"].shape[1]

    gather_w_in = allgather_rider([pack_a(big["w_in"].astype(BF16))])
    gather_rest = allgather_rider([pack_b({nm: big[nm].astype(BF16) for nm in names if nm != "w_in"}, d)])

    x2 = x.reshape(T, d)
    tgt = loss_target.reshape(T, d)
    p_b = p.reshape(T, p.shape[-1]).astype(BF16)
    vec = lambda a: a.reshape(1, -1)
    probs = jax.nn.softmax(hg_lb, axis=0)
    lb = vec(probs[0])

    def ln0_fn(rows, vecs):
        h = _ln_stats(rows[0]) * vecs[0] + vecs[1]
        return [h, h], []
    (h0, h0b), _, (a_all,) = rowwise(ln0_fn, [x2], [vec(ln0_g), vec(ln0_b)], [(d, F32), (d, BF16)], name="ln0_fwd",
                                     rider=gather_w_in)
    win = jnp.concatenate([a_all[s, :, :in_cols] for s in range(4)], axis=1)
    win_mine = jnp.concatenate(
        [win[:, orig[nm][0]:orig[nm][0] + orig[nm][1]] for nm in order]
        + [jnp.zeros((d, LANES - fh), BF16)], axis=1)
    proj = matmul_nn(h0b, win_mine, name="in_proj")

    o_raw, hg_states, (b_all,) = hgrn2_fwd(proj, [mine["hq"][0], mine["hf"][0], mine["hi"][0]], lb, n_seq, seq,
                                           name="hgrn2_fwd", rider=gather_rest)
    view = lambda nm, k, n: WView(b_all, lay[nm][0], lay[nm][1], k, n, axis[nm])
    w_ff1_v, w_ff2_v = view("w_ff1", d, 4 * d), view("w_ff2", 4 * d, d)

    def whole(nm):
        r0, c0, rows, cols = lay[nm]
        return jnp.concatenate([b_all[s, r0:r0 + rows, c0:c0 + cols] for s in range(4)], axis=axis[nm])
    w_o_v, w_pg_v, w_a_v, w_p_v, w_b_full = whole("w_o"), whole("w_pg"), whole("w_a"), whole("w_p"), whole("w_b")

    def ya_fn(rows, vecs):
        o, hg = rows
        outs = []
        for h in range(HG_HEADS):
            oh = o[:, h * HG_DIM:(h + 1) * HG_DIM]
            outs.append(oh * lax.rsqrt(jnp.mean(oh * oh, axis=-1, keepdims=True) + RMS_EPS))
        y = jnp.concatenate(outs, axis=1) * vecs[0] * (hg * _sigmoid(hg))
        return [y], []
    (y_a,), _ = rowwise(ya_fn, [o_raw, (proj,) + mine["hg"]], [hg_norm_g], [(hw, BF16)], name="hgrn2_out_fwd")

    fb_pad = jnp.concatenate([fox_fb, jnp.zeros((1, LANES - fh), F32)], axis=1)

    def lf_fn(rows, vecs):
        u = rows[0] + vecs[0]
        return [jnp.minimum(u, 0.0) - jnp.log(1.0 + jnp.exp(-jnp.abs(u)))], []
    (lf,), _ = rowwise(lf_fn, [(proj,) + mine["ff"]], [fb_pad], [(LANES, F32)], name="fox_logf")
    c_cum = seq_cumsum(lf, n_seq, seq, reverse=False, name="fox_cumsum")

    place = _fox_placement(fh)

    def prep_fn(rows, vecs):
        fq_, fk_, fv_, cc = rows
        pq, pk, aq, ak, oq, ok = vecs
        parts = jnp.concatenate(_split3(cc), axis=1)
        mm = lambda a_, b_: jnp.dot(a_, b_, preferred_element_type=F32)
        q_ = mm(fq_.astype(BF16), pq) + mm(parts, aq) + oq
        k_ = mm(fk_.astype(BF16), pk) + mm(parts, ak) + ok
        return [q_, k_, mm(fv_.astype(BF16), pk)], []
    wa = fh * FOX_AUG
    (qa, ka, va), _ = rowwise(prep_fn, [(proj,) + mine["fq"], (proj,) + mine["fk"], (proj,) + mine["fv"], c_cum],
                              [place[nm] for nm in ("pq", "pk", "aq", "ak", "oq", "ok")], [(wa, BF16)] * 3,
                              name="fox_prep")
    as_seq = lambda t2d: t2d.reshape(n_seq, seq, t2d.shape[1])
    o_fox, ox_fox, lse = fox_fwd(as_seq(qa), as_seq(ka), as_seq(va), name="fox_fwd")
    y_b = o_fox.reshape(T, wa)
    wb_pad = jnp.concatenate([w_b_full.reshape(fh, FOX_HDIM, d), jnp.zeros((fh, FOX_AUG - FOX_HDIM, d), BF16)],
                             axis=1).reshape(wa, d)

    pa = matmul_nn(y_a, w_a_v, name="proj_a")
    pb = matmul_nn(y_b, wb_pad, name="proj_b")

    def merge_fn(rows, vecs):
        ga, gb, a, b = rows
        return [_sigmoid(ga) * a + _sigmoid(gb) * b], []
    (merged,), _ = rowwise(merge_fn, [(proj,) + mine["ga"], (proj,) + mine["gb"], pa, pb], [], [(d, BF16)],
                           name="merge_fwd")
    fused_tm = 512

    def ln1_post(mix, aux, vecs):
        z = ALPHA * aux[0] + mix
        h = _ln_stats(z) * vecs[0] + vecs[1]
        return [z, h, h], []
    (z1, h1, h1b), _ = matmul_nn(merged, w_o_v, name="out_proj_ln1", tm=fused_tm, post=ln1_post, post_aux=[h0],
                                 post_vecs=[ln1_g, ln1_b], post_outs=[F32, F32, BF16])

    relu2 = lambda u: jnp.square(jnp.maximum(u, 0.0))
    act = matmul_nn(h1b, w_ff1_v, name="ff1", out_dtype=BF16, epilogue=relu2)
    pg = matmul_nn(h1b, w_pg_v, name="ple_gate")
    pe = matmul_nn(p_b, w_p_v, name="ple_embed")

    def head_post(ffv, aux, vecs):
        h1v, pgv, pev, t = aux
        g2, b2 = vecs
        sp = _sigmoid(pgv)
        z = ALPHA * h1v + ffv + sp * pev
        y = _ln_stats(z) * g2 + b2
        err = y - t
        loss_rows = 0.5 * jnp.mean(err * err, axis=-1, keepdims=True)
        dy = err * (1.0 / d)
        dz, dg2, db2 = _ln_bwd(z, dy, g2)
        loss_acc = jnp.broadcast_to(_colsum(loss_rows), (1, d))
        return [dz, dz, dz * pev * (sp * (1.0 - sp)), dz * sp], [dg2, db2, loss_acc]
    (dz2, dz2b, dpg, dpe), (g_ln2_g, g_ln2_b, loss_part) = matmul_nn(
        act, w_ff2_v, name="ff2_head", tm=fused_tm, post=head_post, post_aux=[h1, pg, pe, tgt],
        post_vecs=[ln2_g, ln2_b], post_outs=[F32, BF16, BF16, BF16], post_accs=[d, d, d])

    dact = lambda da, a: da * (2.0 * jnp.sqrt(a.astype(F32)))
    du = matmul_nn(dz2b, w_ff2_v, transpose_rhs=True, name="d_ff2", out_dtype=BF16, epilogue=dact, aux=act)
    dh1_pg = matmul_nn(dpg, w_pg_v, transpose_rhs=True, name="d_ple_gate")

    def ln1_bwd_post(dh1_ff, aux, vecs):
        dh1 = ALPHA * aux[0] + dh1_ff + aux[1]
        dz, dg, db = _ln_bwd(aux[2], dh1, vecs[0])
        return [dz, dz], [dg, db]
    (dz1, dz1b), (g_ln1_g, g_ln1_b) = matmul_nn(
        du, w_ff1_v, transpose_rhs=True, name="d_ff1_ln1", tm=fused_tm, post=ln1_bwd_post, post_aux=[dz2, dh1_pg, z1],
        post_vecs=[ln1_g], post_outs=[F32, BF16], post_accs=[d, d])

    def merge_bwd_post(dm, aux, vecs):
        ga, gb, a, b = aux
        sa, sb = _sigmoid(ga), _sigmoid(gb)
        return [dm * a * (sa * (1.0 - sa)), dm * b * (sb * (1.0 - sb)), dm * sa, dm * sb], []
    (dga, dgb, dma, dmb), _ = matmul_nn(
        dz1b, w_o_v, transpose_rhs=True, name="d_out_proj_merge", tm=fused_tm, post=merge_bwd_post,
        post_aux=[(proj,) + mine["ga"], (proj,) + mine["gb"], pa, pb], post_outs=[BF16] * 4)
    dya = matmul_nn(dma, w_a_v, transpose_rhs=True, name="d_proj_a")
    dyb = matmul_nn(dmb, wb_pad, transpose_rhs=True, name="d_proj_b", out_dtype=BF16)

    def ya_bwd_fn(rows, vecs):
        o, hg, dy = rows
        ng = vecs[0]
        sg = _sigmoid(hg)
        gate = hg * sg
        dn_parts, do_parts, n_parts = [], [], []
        for h in range(HG_HEADS):
            hs = slice(h * HG_DIM, (h + 1) * HG_DIM)
            oh = o[:, hs]
            r = lax.rsqrt(jnp.mean(oh * oh, axis=-1, keepdims=True) + RMS_EPS)
            nh = oh * r
            dn = dy[:, hs] * ng[:, hs] * gate[:, hs]
            do_parts.append(r * (dn - nh * jnp.mean(dn * nh, axis=-1, keepdims=True)))
            n_parts.append(nh)
        nrm = jnp.concatenate(n_parts, axis=1)
        dhg = dy * nrm * ng * (sg * (1.0 + hg * (1.0 - sg)))
        return [jnp.concatenate(do_parts, axis=1), dhg], [_colsum(dy * nrm * gate)]
    (do_raw, dhg), (g_norm_g,) = rowwise(ya_bwd_fn, [o_raw, (proj,) + mine["hg"], dya], [hg_norm_g],
                                         [(hw, F32), (hw, BF16)], [hw], name="hgrn2_out_bwd")
    dhq, dhf, dhi, g_lb = hgrn2_bwd(proj, [mine["hq"][0], mine["hf"][0], mine["hi"][0]], lb, do_raw, hg_states,
                                    n_seq, seq, name="hgrn2_bwd")

    do_fox = as_seq(dyb)
    dqa, dka, dva, dsum = fox_bwd(as_seq(qa), as_seq(ka), as_seq(va), do_fox, ox_fox, lse, name="fox_bwd")

    def unprep_fn(rows, vecs):
        mm = lambda a_, b_: jnp.dot(a_.astype(BF16), b_, preferred_element_type=F32)
        return [mm(rows[0], vecs[0]), mm(rows[1], vecs[1]), mm(rows[2], vecs[1])], []
    (dfq, dfk, dfv), _ = rowwise(unprep_fn, [dqa.reshape(T, wa), dka.reshape(T, wa), dva.reshape(T, wa)],
                                 [place["pqt"], place["pkt"]], [(hw, BF16)] * 3, name="fox_unprep")
    dc = -dsum.reshape(n_seq, fh, seq).transpose(0, 2, 1).reshape(T, fh)
    dc = jnp.concatenate([dc, jnp.zeros((T, LANES - fh), F32)], axis=1)
    dlf = seq_cumsum(dc, n_seq, seq, reverse=True, name="fox_cumsum_bwd")

    def lf_bwd_fn(rows, vecs):
        u = rows[0] + vecs[0]
        du_ = rows[1] * _sigmoid(-u)
        return [du_], [_colsum(du_)]
    (dff_,), (g_fb,) = rowwise(lf_bwd_fn, [(proj,) + mine["ff"], dlf], [fb_pad], [(LANES, BF16)], [LANES],
                               name="fox_logf_bwd")

    dproj = jnp.concatenate([dga, dgb, dhq, dhf, dhi, dhg, dfq, dfk, dfv, dff_], axis=1)

    grads_b = jnp.zeros((4, b_rows, d), F32)
    for nm, lhs, rhs in (("w_ff1", h1b, du), ("w_ff2", act, dz2b)):
        grads_b = matmul_tn(lhs, rhs, name="g_" + nm, into=(grads_b, lay[nm][0], lay[nm][1], axis[nm]))
    gfull = {
        "w_a": matmul_tn(y_a, dma, name="g_w_a"),
        "w_b": matmul_tn(y_b, dmb, name="g_w_b").reshape(fh, FOX_AUG, d)[:, :FOX_HDIM].reshape(hw, d),
        "w_o": matmul_tn(merged, dz1b, name="g_w_o"),
        "w_pg": matmul_tn(h1b, dpg, name="g_w_pg"),
        "w_p": matmul_tn(p_b, dpe, name="g_w_p"),
    }

    def chip_parts(nm, s):
        g = gfull[nm]
        n = g.shape[axis[nm]] // 4
        return lax.slice_in_dim(g, s * n, (s + 1) * n, axis=axis[nm])
    for nm in gfull:
        grads_b = lax.dynamic_update_slice(grads_b, jnp.stack([chip_parts(nm, s) for s in range(4)]),
                                           (0, lay[nm][0], lay[nm][1]))
    me = 2 * lax.axis_index("x") + lax.axis_index("y")
    core = lax.axis_index("c")

    def sum2_fn(rows, vecs):
        s = rows[0] + rows[1].astype(F32)
        return [s, s], []

    def sum4_fn(rows, vecs):
        a, r0, r1, r2 = rows
        return [((a + r0.astype(F32)) + r1.astype(F32)) + r2.astype(F32)], []

    def chip_pair_sum(g, tag):
        h, cols = g.shape[1] // 2, g.shape[2]
        keep = lax.dynamic_slice_in_dim(g, core * h, h, axis=1)
        give = lax.dynamic_slice_in_dim(g, (1 - core) * h, h, axis=1).astype(BF16)
        (from_core,) = swap_cores([give], name="swap_partials_" + tag)
        (s32, s16), _ = rowwise(sum2_fn, [keep.reshape(4 * h, cols), from_core.reshape(4 * h, cols)], [],
                                [(cols, F32), (cols, BF16)], name="sum_cores_" + tag, tm=SUM_TILE)
        return s32.reshape(4, h, cols), s16.reshape(4, h, cols)

    def chip_sum(pr, gt, tag):
        own = lax.dynamic_index_in_dim(pr, me, axis=0, keepdims=False)
        (q,), _ = rowwise(sum4_fn, [own, gt[0], gt[1], gt[2]], [], [(own.shape[1], F32)], name="sum_chips_" + tag,
                          tm=SUM_TILE)
        return q

    pair_rest, pair_rest_b = chip_pair_sum(grads_b, "rest")
    gw_in_mine, (got_rest,) = matmul_tn(h0b, dproj, name="g_w_in", rider=scatter_rider([pair_rest_b]))
    gfull["w_in"] = jnp.concatenate([gw_in_mine[:, mine[nm][0]:mine[nm][0] + orig[nm][1]]
                                     for nm in ["hq", "hf", "hi", "hg", "fq", "fk", "fv", "ff", "ga", "gb"]], axis=1)
    grads_a = jnp.stack([pack_a(chip_parts("w_in", s)) for s in range(4)])
    pair_in, pair_in_b = chip_pair_sum(grads_a, "w_in")
    def ln0_bwd_post(dh0_in, aux, vecs):
        dx, dg, db = _ln_bwd(aux[1], dh0_in + ALPHA * aux[0], vecs[0])
        return [dx], [dg, db]
    ((dx,), (g_ln0_g, g_ln0_b)), (got_in,) = matmul_nn(
        dproj, win_mine, transpose_rhs=True, name="d_in_proj_ln0", tm=fused_tm, post=ln0_bwd_post,
        post_aux=[dz1, x2], post_vecs=[vec(ln0_g)], post_outs=[F32], post_accs=[d, d],
        rider=scatter_rider([pair_in_b]))
    q_half = [chip_sum(pair_in, got_in, "w_in"), chip_sum(pair_rest, got_rest, "rest")]
    g_a, g_b = join_halves(q_half, name="join_halves")
    g_shards = unpack_b(g_b, lay)
    g_shards["w_in"] = g_a[:, :in_cols]

    assert d == PACK_W and 2 * hw == PACK_W and fh <= LANES
    small = allreduce_small(jnp.concatenate(
        [g_ln0_g, g_ln0_b, g_ln1_g, g_ln1_b, g_ln2_g, g_ln2_b, jnp.concatenate([g_norm_g, g_lb], axis=1),
         jnp.concatenate([g_fb, loss_part[:, LANES:]], axis=1)], axis=0), name="allreduce_small")
    loss = small[7, LANES]

    small_w = [vec(ln0_g), vec(ln0_b), ln1_g, ln1_b, ln2_g, ln2_b, hg_lb, hg_norm_g, fox_fb]
    small_m = [vec(m_ln0_g), vec(m_ln0_b), m_ln1_g, m_ln1_b, m_ln2_g, m_ln2_b, m_hg_lb, m_hg_norm_g, m_fox_fb]
    small_v = [vec(v_ln0_g), vec(v_ln0_b), v_ln1_g, v_ln1_b, v_ln2_g, v_ln2_b, v_hg_lb, v_hg_norm_g, v_fox_fb]
    small_out = adamw_small(small, probs[0:1], small_w, small_m, small_v, name="adamw_small")
    small_shapes = [ln0_g.shape, ln0_b.shape, ln1_g.shape, ln1_b.shape, ln2_g.shape, ln2_b.shape, hg_lb.shape,
                    hg_norm_g.shape, fox_fb.shape]
    sg_out, sd_out, sm_out, sv_out = [[a.reshape(shp) for a, shp in zip(small_out[9 * k:9 * k + 9], small_shapes)]
                                      for k in range(4)]

    big_out = {}
    for nm in names:
        delta, m2, v2 = adamw(big[nm], g_shards[nm], big_m[nm], big_v[nm], name="adamw_" + nm)
        big_out[nm] = (g_shards[nm][None], delta[None], m2[None], v2[None])

    def ordered(k):
        sm_ = [sg_out, sd_out, sm_out, sv_out][k]
        bg = lambda nm: big_out[nm][k]
        return [sm_[0], sm_[1], bg("w_in"), sm_[6], sm_[7], sm_[8], bg("w_a"), bg("w_b"), bg("w_o"), sm_[2], sm_[3],
                bg("w_ff1"), bg("w_ff2"), bg("w_pg"), bg("w_p"), sm_[4], sm_[5]]
    grad_x = dx.reshape(n_seq, seq, d)
    return (loss, grad_x, *ordered(0), *ordered(1), *ordered(2), *ordered(3))
```

```python
import functools
from typing import NamedTuple, Optional

import numpy as np
import jax
import jax.numpy as jnp
from jax import lax
from jax.experimental import pallas as pl
from jax.experimental.pallas import tpu as pltpu

F32 = jnp.float32
BF16 = jnp.bfloat16
MESH = pl.DeviceIdType.MESH

VMEM_LIMIT_BYTES = 48 * 1024 * 1024
LANES = 128
HG_HEADS = 4
HG_DIM = 128
HG_BLK = 16
HG_TILE = 256
HG_SLOTS = 8
FOX_HDIM = 64
FOX_AUG = 128
FOX_TQ = 1024
FOX_FWD_HEADS = 1
LN_EPS = 1e-5
RMS_EPS = 1e-6
DEPTH = 1
ALPHA = (2.0 * DEPTH) ** 0.25
ADAM_LR, ADAM_B1, ADAM_B2, ADAM_EPS, ADAM_WD, ADAM_STEP = 0.001, 0.9, 0.999, 1e-08, 0.01, 10
NEG_INF = -1e30


def _cparams(sem):
    return pltpu.CompilerParams(dimension_semantics=sem, vmem_limit_bytes=VMEM_LIMIT_BYTES)


def _tile(n, cap):
    if n <= cap:
        return n
    best = None
    for t in range(LANES, cap + 1, LANES):
        if n % t == 0:
            best = t
    assert best is not None, (n, cap)
    return best


class WView(NamedTuple):
    arr: jax.Array
    r0: int
    c0: int
    k: int
    n: int
    split: Optional[int]


def matmul_nn(a, w, *, name, transpose_rhs=False, out_dtype=F32, epilogue=None, aux=None, tm=1024, rider=None,
              post=None, post_aux=(), post_vecs=(), post_outs=(), post_accs=()):
    wv = w if isinstance(w, WView) else WView(w[None], 0, 0, w.shape[0], w.shape[1], None)
    rows_s = wv.k // 4 if wv.split == 0 else wv.k
    cols_s = wv.n // 4 if wv.split == 1 else wv.n
    tr, tc = _tile(rows_s, 1152), _tile(cols_s, 1152)
    assert wv.r0 % tr == 0 and wv.c0 % tc == 0
    T, K = a.shape
    N, tn, tk = (wv.k, tr, tc) if transpose_rhs else (wv.n, tc, tr)
    assert K == (wv.n if transpose_rhs else wv.k)
    tm = min(tm, T)
    assert T % tm == 0
    nk = K // tk

    def w_block(ri, ci):
        if wv.split == 0:
            return (ri * tr) // rows_s, (wv.r0 + (ri * tr) % rows_s) // tr, wv.c0 // tc + ci
        if wv.split == 1:
            return (ci * tc) // cols_s, wv.r0 // tr + ri, (wv.c0 + (ci * tc) % cols_s) // tc
        return 0, wv.r0 // tr + ri, wv.c0 // tc + ci

    fused = post is not None
    assert not fused or N == tn
    aux_list = list(post_aux) if fused else ([aux] if aux is not None else [])
    aux_list = [x if isinstance(x, tuple) else (x, 0, x.shape[1]) for x in aux_list]
    vec_list = list(post_vecs)
    out_dtypes = list(post_outs) if fused else [out_dtype]
    n_aux, n_vec, n_out, n_acc = len(aux_list), len(vec_list), len(out_dtypes), len(post_accs)

    def body(*refs):
        a_ref, w_ref = refs[:2]
        aux_refs = refs[2:2 + n_aux]
        vec_refs = refs[2 + n_aux:2 + n_aux + n_vec]
        out_refs = refs[2 + n_aux + n_vec:2 + n_aux + n_vec + n_out]
        sum_refs = refs[2 + n_aux + n_vec + n_out:2 + n_aux + n_vec + n_out + n_acc]
        acc_ref = refs[-1]
        m, k = pl.program_id(1), pl.program_id(2)
        if transpose_rhs:
            part = lax.dot_general(a_ref[...], w_ref[...], (((1,), (1,)), ((), ())), preferred_element_type=F32)
        else:
            part = jnp.dot(a_ref[...], w_ref[...], preferred_element_type=F32)

        def write(res):
            if not fused:
                if epilogue is not None:
                    res = epilogue(res) if not aux_refs else epilogue(res, aux_refs[0][...])
                out_refs[0][...] = res.astype(out_dtype)
                return
            outs, sums = post(res, [r[...] for r in aux_refs], [v[...] for v in vec_refs])
            assert len(outs) == n_out and len(sums) == n_acc
            for r, val in zip(out_refs, outs):
                r[...] = val.astype(r.dtype)
            for r, val in zip(sum_refs, sums):
                def first_rows(r=r, val=val):
                    r[...] = val

                def later_rows(r=r, val=val):
                    r[...] += val
                pl.when(m == 0)(first_rows)
                pl.when(m > 0)(later_rows)

        if nk == 1:
            write(part)
        else:
            @pl.when(k == 0)
            def _():
                acc_ref[...] = part

            @pl.when(k > 0)
            def _():
                acc_ref[...] += part

            @pl.when(k == nk - 1)
            def _():
                write(acc_ref[...])

    w_index = (lambda n, m, k: w_block(n, k)) if transpose_rhs else (lambda n, m, k: w_block(k, n))
    in_specs = [pl.BlockSpec((tm, tk), lambda n, m, k: (m, k)),
                pl.BlockSpec((None, tr, tc), w_index)]
    args = [a, wv.arr]
    for arr, off, width in aux_list:
        assert width == N and off % tn == 0
        in_specs.append(pl.BlockSpec((tm, tn), functools.partial(lambda n, m, k, blk: (m, blk + n), blk=off // tn)))
        args.append(arr)
    for v in vec_list:
        in_specs.append(pl.BlockSpec(v.shape, lambda n, m, k: (0, 0)))
        args.append(v)
    out_specs = [pl.BlockSpec((tm, tn), lambda n, m, k: (m, n)) for _ in out_dtypes]
    out_specs += [pl.BlockSpec((1, tn), lambda n, m, k: (0, 0)) for _ in post_accs]
    out_shape = [jax.ShapeDtypeStruct((T, N), dt) for dt in out_dtypes]
    out_shape += [jax.ShapeDtypeStruct((1, N), F32) for _ in post_accs]
    scratch = [pltpu.VMEM((tm, tn) if nk > 1 else (8, LANES), F32)]
    grid = (N // tn, T // tm, nk)
    sem = ("arbitrary",) * 3 if (n_acc or rider is not None) else ("parallel", "parallel", "arbitrary")
    params = pltpu.CompilerParams(dimension_semantics=sem, vmem_limit_bytes=VMEM_LIMIT_BYTES,
                                  has_side_effects=rider is not None)
    if rider is not None:
        r_in, r_out, r_sems = rider.specs()
        body = rider.wrap(body, len(in_specs), len(out_specs), 3)
        in_specs, out_specs, out_shape = in_specs + r_in, out_specs + r_out, out_shape + rider.out_shape
        scratch, args = scratch + r_sems, args + list(rider.ins)
    res = pl.pallas_call(body, name=name, grid=grid, in_specs=in_specs, out_specs=out_specs, out_shape=out_shape,
                         scratch_shapes=scratch, compiler_params=params)(*args)
    main = (list(res[:n_out]), list(res[n_out:n_out + n_acc])) if fused else res[0]
    return main if rider is None else (main, list(res[n_out + n_acc:]))


def matmul_tn(a, b, *, name, tk=1024, rider=None, into=None):
    T, M = a.shape
    T2, N = b.shape
    tk = min(tk, T)
    assert T == T2 and T % tk == 0
    if into is not None:
        assert rider is None
        buf, r0, c0, split = into
        rows_s, cols_s = (M // 4, N) if split == 0 else (M, N // 4)
        tm, tn = _tile(rows_s, 1024), _tile(cols_s, 1152)
        assert r0 % tm == 0 and c0 % tn == 0

        def part_block(m, n, k):
            if split == 0:
                return (m * tm) // rows_s, (r0 + (m * tm) % rows_s) // tm, c0 // tn + n
            return (n * tn) // cols_s, r0 // tm + m, (c0 + (n * tn) % cols_s) // tn

        def body_into(a_ref, b_ref, buf_ref, o_ref):
            k = pl.program_id(2)
            part = lax.dot_general(a_ref[...], b_ref[...], (((0,), (0,)), ((), ())), preferred_element_type=F32)

            @pl.when(k == 0)
            def _():
                o_ref[...] = part

            @pl.when(k > 0)
            def _():
                o_ref[...] += part

        return pl.pallas_call(
            body_into, name=name, grid=(M // tm, N // tn, T // tk),
            in_specs=[pl.BlockSpec((tk, tm), lambda m, n, k: (k, m)), pl.BlockSpec((tk, tn), lambda m, n, k: (k, n)),
                      pl.BlockSpec(memory_space=pl.ANY)],
            out_specs=pl.BlockSpec((None, tm, tn), part_block),
            out_shape=jax.ShapeDtypeStruct(buf.shape, buf.dtype), input_output_aliases={2: 0},
            compiler_params=_cparams(("parallel", "parallel", "arbitrary")))(a, b, buf)
    tm = _tile(M, 1024)
    tn = _tile(N, 1152)

    def body(a_ref, b_ref, o_ref):
        k = pl.program_id(2)
        part = lax.dot_general(a_ref[...], b_ref[...], (((0,), (0,)), ((), ())), preferred_element_type=F32)

        @pl.when(k == 0)
        def _():
            o_ref[...] = part

        @pl.when(k > 0)
        def _():
            o_ref[...] += part

    in_specs = [pl.BlockSpec((tk, tm), lambda m, n, k: (k, m)), pl.BlockSpec((tk, tn), lambda m, n, k: (k, n))]
    out_specs = [pl.BlockSpec((tm, tn), lambda m, n, k: (m, n))]
    out_shape = [jax.ShapeDtypeStruct((M, N), F32)]
    grid = (M // tm, N // tn, T // tk)
    if rider is None:
        return pl.pallas_call(body, name=name, grid=grid, in_specs=in_specs, out_specs=out_specs, out_shape=out_shape,
                              compiler_params=_cparams(("parallel", "parallel", "arbitrary")))(a, b)[0]
    r_in, r_out, r_sems = rider.specs()
    res = pl.pallas_call(
        rider.wrap(body, 2, 1, 3), name=name, grid=grid, in_specs=in_specs + r_in, out_specs=out_specs + r_out,
        out_shape=out_shape + rider.out_shape, scratch_shapes=r_sems,
        compiler_params=pltpu.CompilerParams(dimension_semantics=("arbitrary",) * 3,
                                             vmem_limit_bytes=VMEM_LIMIT_BYTES, has_side_effects=True),
    )(a, b, *rider.ins)
    return res[0], list(res[1:])


def rowwise(fn, rows, vecs, outs, accs=(), *, name, tm=512, rider=None):
    rows = [r if isinstance(r, tuple) else (r, 0, r.shape[1]) for r in rows]
    T = rows[0][0].shape[0]
    tm = min(tm, T)
    assert T % tm == 0
    n_rows, n_vecs, n_outs, n_accs = len(rows), len(vecs), len(outs), len(accs)

    def body(*refs):
        row_refs = refs[:n_rows]
        vec_refs = refs[n_rows:n_rows + n_vecs]
        out_refs = refs[n_rows + n_vecs:n_rows + n_vecs + n_outs]
        acc_refs = refs[n_rows + n_vecs + n_outs:]
        out_vals, acc_vals = fn([r[...] for r in row_refs], [v[...] for v in vec_refs])
        assert len(out_vals) == n_outs and len(acc_vals) == n_accs
        for r, val in zip(out_refs, out_vals):
            r[...] = val.astype(r.dtype)
        if n_accs:
            i = pl.program_id(0)

            @pl.when(i == 0)
            def _():
                for r in acc_refs:
                    r[...] = jnp.zeros_like(r)

            for r, val in zip(acc_refs, acc_vals):
                r[...] += val

    in_specs = []
    for arr, off, width in rows:
        assert off % width == 0
        in_specs.append(pl.BlockSpec((tm, width), functools.partial(lambda i, blk: (i, blk), blk=off // width)))
    for v in vecs:
        in_specs.append(pl.BlockSpec(v.shape, lambda i: (0, 0)))
    out_specs = [pl.BlockSpec((tm, w), lambda i: (i, 0)) for w, _ in outs]
    out_specs += [pl.BlockSpec((1, w), lambda i: (0, 0)) for w in accs]
    out_shape = [jax.ShapeDtypeStruct((T, w), dt) for w, dt in outs]
    out_shape += [jax.ShapeDtypeStruct((1, w), F32) for w in accs]
    args = [r[0] for r in rows] + list(vecs)
    if rider is None:
        res = pl.pallas_call(body, name=name, grid=(T // tm,), in_specs=in_specs, out_specs=out_specs,
                             out_shape=out_shape,
                             compiler_params=_cparams(("arbitrary",) if n_accs else ("parallel",)))(*args)
        return res[:n_outs], res[n_outs:]
    r_in, r_out, r_sems = rider.specs()
    res = pl.pallas_call(
        rider.wrap(body, len(in_specs), len(out_specs), 1), name=name, grid=(T // tm,), in_specs=in_specs + r_in,
        out_specs=out_specs + r_out, out_shape=out_shape + rider.out_shape, scratch_shapes=r_sems,
        compiler_params=pltpu.CompilerParams(dimension_semantics=("arbitrary",), vmem_limit_bytes=VMEM_LIMIT_BYTES,
                                             has_side_effects=True),
    )(*args, *rider.ins)
    return res[:n_outs], res[n_outs:n_outs + n_accs], list(res[n_outs + n_accs:])


def _colsum(x):
    return jnp.sum(x, axis=0, keepdims=True)


def _sigmoid(x):
    return 1.0 / (1.0 + jnp.exp(-x))


def _ln_stats(z):
    mu = jnp.mean(z, axis=-1, keepdims=True)
    zc = z - mu
    var = jnp.mean(zc * zc, axis=-1, keepdims=True)
    return zc * lax.rsqrt(var + LN_EPS)


def _ln_bwd(zhat_src, dy, g):
    mu = jnp.mean(zhat_src, axis=-1, keepdims=True)
    zc = zhat_src - mu
    var = jnp.mean(zc * zc, axis=-1, keepdims=True)
    rstd = lax.rsqrt(var + LN_EPS)
    zh = zc * rstd
    dzh = dy * g
    dz = rstd * (dzh - jnp.mean(dzh, axis=-1, keepdims=True) - zh * jnp.mean(dzh * zh, axis=-1, keepdims=True))
    return dz, _colsum(dy * zh), _colsum(dy)


def _hg_constants():
    r = np.arange(HG_TILE)
    same = (r[:, None] // HG_BLK) == (r[None, :] // HG_BLK)
    lower = (same & (r[None, :] <= r[:, None])).astype(np.float32)
    upper = (same & (r[None, :] >= r[:, None])).astype(np.float32)
    total = same.astype(np.float32)
    c = np.arange(2 * HG_DIM)
    bd = ((c[:, None] // HG_DIM) == (c[None, :] // HG_DIM)).astype(np.float32)
    pair_t = np.array([t for t, _ in _HG_PAIRS])
    pair_s = np.array([s for _, s in _HG_PAIRS])
    sel_t = (pair_t[None, :] == np.arange(HG_BLK)[:, None]).astype(np.float32)
    sel_s = (pair_s[None, :] == np.arange(HG_BLK)[:, None]).astype(np.float32)
    as_bf = lambda m: jnp.asarray(m, dtype=BF16)
    return as_bf(lower), as_bf(upper), as_bf(total), as_bf(bd), as_bf(sel_t), as_bf(sel_s)


_HG_HALF = HG_BLK // 2
_HG_PAIRS = ([(t, s) for t in range(_HG_HALF, HG_BLK) for s in range(HG_BLK)]
             + [(t, s) for t in range(_HG_HALF) for s in range(_HG_HALF)])
HG_STACK = len(_HG_PAIRS)
_HG_SLABS = ([((t - _HG_HALF) * HG_BLK, (t,), HG_BLK) for t in range(_HG_HALF, HG_BLK)]
             + [(_HG_HALF * HG_BLK + t * _HG_HALF, (t, t + 1), _HG_HALF) for t in range(0, _HG_HALF, 2)])


def _stack_by_s(x):
    return jnp.concatenate([x] * _HG_HALF + [x[:_HG_HALF]] * _HG_HALF, axis=0)


def _stack_by_t(x):
    w = x.shape[1]
    return jnp.concatenate([jnp.broadcast_to(x[t:t + 1], (HG_BLK, w)) for t in range(_HG_HALF, HG_BLK)]
                           + [jnp.broadcast_to(x[t:t + 1], (_HG_HALF, w)) for t in range(_HG_HALF)], axis=0)


def _keep_bf16_bits(x):
    bits = lax.bitcast_convert_type(x, jnp.int32) & jnp.int32(-65536)
    return lax.bitcast_convert_type(bits, F32)


def _head_sums(stack_ref, slot, bd):
    pair = bd.shape[0]
    return jnp.concatenate([jnp.dot(stack_ref[slot, :, c0:c0 + pair], bd, preferred_element_type=F32)
                            for c0 in range(0, stack_ref.shape[2], pair)], axis=1)


def _split3(x):
    hi = _keep_bf16_bits(x)
    r1 = x - hi
    mid = _keep_bf16_bits(r1)
    lo = _keep_bf16_bits(r1 - mid)
    return hi.astype(BF16), mid.astype(BF16), lo.astype(BF16)


def _dot3(m01, x):
    hi, mid, lo = _split3(x)
    d = lambda p: jnp.dot(m01, p, preferred_element_type=F32)
    return (d(lo) + d(mid)) + d(hi)


def _hg_prologue(hq, hf, lb, lower, total):
    sq = _sigmoid(hq)
    q = hq * sq
    sg = _sigmoid(hf)
    f = lb + (1.0 - lb) * sg
    g = jnp.log(f)
    k = 1.0 - f
    b = _dot3(lower, g)
    bl = _dot3(total, g)
    return q, k, f, sg, sq, b, bl


def _stack16(fn):
    return [fn(t) for t in range(HG_BLK)]


def hgrn2_fwd(proj, offs, lb, n_seq, seq, *, name, rider=None):
    T = n_seq * seq
    W = HG_HEADS * HG_DIM
    n_tiles = seq // HG_TILE
    nb = HG_TILE // HG_BLK
    lower, _, total, bd, sel_t, _ = _hg_constants()

    def body(hq_ref, hf_ref, hi_ref, lb_ref, lower_ref, total_ref, bd_ref, selt_ref,
             o_ref, st_out_ref,
             st_ref, q_s, k_s, v_s, b_s, qt_s, kt_s, d_s, p_s):
        @pl.when(pl.program_id(1) == 0)
        def _():
            st_ref[...] = jnp.zeros_like(st_ref)

        q, k, _, _, _, b, bl = _hg_prologue(hq_ref[...], hf_ref[...], lb_ref[...], lower_ref[...], total_ref[...])
        q_s[...] = q
        k_s[...] = k
        v_s[...] = hi_ref[...]
        b_s[...] = b
        qt_s[...] = q * jnp.exp(b)
        kt_s[...] = k * jnp.exp(jnp.minimum(bl - b, 0.0))
        d_s[...] = jnp.exp(bl)
        rowi = lax.broadcasted_iota(jnp.int32, (HG_BLK, W), 0)

        def block(i, slot):
            r0 = pl.multiple_of(i * HG_BLK, HG_BLK)
            rows = pl.ds(r0, HG_BLK)
            qi, ki, vi, bi = q_s[rows, :], k_s[rows, :], v_s[rows, :], b_s[rows, :]
            for off, ts, n in _HG_SLABS:
                slab = [jnp.where(rowi[:n] <= t, jnp.exp(jnp.minimum(bi[t:t + 1, :] - bi[:n], 0.0)), 0.0)
                        * qi[t:t + 1, :] * ki[:n] for t in ts]
                p_s[slot, pl.ds(off, HG_BLK), :] = jnp.concatenate(slab, axis=0).astype(BF16)
            a_b = _head_sums(p_s, slot, bd_ref[...])
            o_blk = jnp.dot(selt_ref[...], (a_b * _stack_by_s(vi)).astype(BF16), preferred_element_type=F32)
            qti, kti, di = qt_s[rows, :], kt_s[rows, :], d_s[rows, :]
            outs = []
            for h in range(HG_HEADS):
                hs = slice(h * HG_DIM, (h + 1) * HG_DIM)
                st_h = st_ref[hs, :]
                st_out_ref[i, hs, :] = st_h
                outs.append(lax.dot_general(qti[:, hs].astype(BF16), st_h.astype(BF16),
                                            (((1,), (1,)), ((), ())), preferred_element_type=F32))
                upd = lax.dot_general(vi[:, hs].astype(BF16), kti[:, hs].astype(BF16),
                                      (((0,), (0,)), ((), ())), preferred_element_type=F32)
                st_ref[hs, :] = st_h * di[0:1, hs] + upd
            o_ref[rows, :] = o_blk + jnp.concatenate(outs, axis=1)

        def some_blocks(jj, carry):
            for slot in range(HG_SLOTS):
                block(HG_SLOTS * jj + slot, slot)
            return carry

        lax.fori_loop(0, nb // HG_SLOTS, some_blocks, 0)

    col = lambda off: functools.partial(lambda s, t, blk: (s * n_tiles + t, blk), blk=off // W)
    const = lambda m: pl.BlockSpec(m.shape, lambda s, t: (0, 0))
    tile_f32 = pltpu.VMEM((HG_TILE, W), F32)
    in_specs = [pl.BlockSpec((HG_TILE, W), col(offs[0])), pl.BlockSpec((HG_TILE, W), col(offs[1])),
                pl.BlockSpec((HG_TILE, W), col(offs[2])), const(lb), const(lower), const(total), const(bd),
                const(sel_t)]
    out_specs = [pl.BlockSpec((HG_TILE, W), lambda s, t: (s * n_tiles + t, 0)),
                 pl.BlockSpec((nb, W, HG_DIM), lambda s, t: (s * n_tiles + t, 0, 0))]
    out_shape = [jax.ShapeDtypeStruct((T, W), F32), jax.ShapeDtypeStruct((T // HG_BLK, W, HG_DIM), F32)]
    scratch = [pltpu.VMEM((W, HG_DIM), F32)] + [tile_f32] * 7 + [pltpu.VMEM((HG_SLOTS, HG_STACK, W), BF16)]
    args = [proj, proj, proj, lb, lower, total, bd, sel_t]
    params = _cparams(("arbitrary", "arbitrary"))
    if rider is not None:
        r_in, r_out, r_sems = rider.specs()
        body = rider.wrap(body, len(in_specs), len(out_specs), 2)
        in_specs, out_specs, out_shape = in_specs + r_in, out_specs + r_out, out_shape + rider.out_shape
        scratch, args = scratch + r_sems, args + rider.ins
        params = pltpu.CompilerParams(dimension_semantics=("arbitrary", "arbitrary"),
                                      vmem_limit_bytes=VMEM_LIMIT_BYTES, has_side_effects=True)
    res = pl.pallas_call(body, name=name, grid=(n_seq, n_tiles), in_specs=in_specs, out_specs=out_specs,
                         out_shape=out_shape, scratch_shapes=scratch, compiler_params=params)(*args)
    return res[0], res[1], list(res[2:])


def hgrn2_bwd(proj, offs, lb, do, states, n_seq, seq, *, name):
    T = n_seq * seq
    W = HG_HEADS * HG_DIM
    n_tiles = seq // HG_TILE
    nb = HG_TILE // HG_BLK
    lower, upper, total, bd, sel_t, sel_s = _hg_constants()

    def body(hq_ref, hf_ref, hi_ref, do_ref, st_in_ref, lb_ref, lower_ref, upper_ref, total_ref, bd_ref,
             selt_ref, sels_ref,
             dhq_ref, dhf_ref, dhi_ref, dlb_ref,
             dst_ref, q_s, k_s, v_s, b_s, qt_s, kt_s, d_s, eb_s, ekb_s, dq_s, dk_s, db_s, dv_s,
             p_s, e_s, w_s):
        first = jnp.logical_and(pl.program_id(0) == 0, pl.program_id(1) == 0)

        @pl.when(first)
        def _():
            dlb_ref[...] = jnp.zeros_like(dlb_ref)

        @pl.when(pl.program_id(1) == 0)
        def _():
            dst_ref[...] = jnp.zeros_like(dst_ref)

        hq, lbv = hq_ref[...], lb_ref[...]
        q, k, f, sg, sq, b, bl = _hg_prologue(hq, hf_ref[...], lbv, lower_ref[...], total_ref[...])
        eb = jnp.exp(b)
        ekb = jnp.exp(jnp.minimum(bl - b, 0.0))
        q_s[...] = q
        k_s[...] = k
        v_s[...] = hi_ref[...]
        b_s[...] = b
        eb_s[...] = eb
        ekb_s[...] = ekb
        qt_s[...] = q * eb
        kt_s[...] = k * ekb
        d_s[...] = jnp.exp(bl)
        rowi = lax.broadcasted_iota(jnp.int32, (HG_BLK, W), 0)
        last_row = rowi == HG_BLK - 1

        def block(i, slot):
            r0 = pl.multiple_of(i * HG_BLK, HG_BLK)
            rows = pl.ds(r0, HG_BLK)
            qi, ki, vi, bi, doi = q_s[rows, :], k_s[rows, :], v_s[rows, :], b_s[rows, :], do_ref[rows, :]
            for off, ts, n in _HG_SLABS:
                es = [jnp.where(rowi[:n] <= t, jnp.exp(jnp.minimum(bi[t:t + 1, :] - bi[:n], 0.0)), 0.0) for t in ts]
                sl = pl.ds(off, HG_BLK)
                e_s[slot, sl, :] = jnp.concatenate(es, axis=0)
                p_s[slot, sl, :] = jnp.concatenate([e * qi[t:t + 1, :] * ki[:n] for e, t in zip(es, ts)],
                                                   axis=0).astype(BF16)
                w_s[slot, sl, :] = jnp.concatenate([doi[t:t + 1, :] * vi[:n] for t in ts], axis=0).astype(BF16)
            a_b = _head_sums(p_s, slot, bd_ref[...])
            da_b = _head_sums(w_s, slot, bd_ref[...])
            x = da_b * e_s[slot]
            dq_in = jnp.dot(selt_ref[...], (x * _stack_by_s(ki)).astype(BF16), preferred_element_type=F32)
            dk_in = jnp.dot(sels_ref[...], (x * _stack_by_t(qi)).astype(BF16), preferred_element_type=F32)
            dv_in = jnp.dot(sels_ref[...], (a_b * _stack_by_t(doi)).astype(BF16), preferred_element_type=F32)
            qti, kti, di = qt_s[rows, :], kt_s[rows, :], d_s[rows, :]
            dqt, dkt, dvt, dd = [], [], [], []
            for h in range(HG_HEADS):
                hs = slice(h * HG_DIM, (h + 1) * HG_DIM)
                st_h = st_in_ref[i, hs, :]
                dst_h = dst_ref[hs, :]
                do_h, v_h = doi[:, hs].astype(BF16), vi[:, hs].astype(BF16)
                dst_b = dst_h.astype(BF16)
                dqt.append(jnp.dot(do_h, st_h.astype(BF16), preferred_element_type=F32))
                dkt.append(jnp.dot(v_h, dst_b, preferred_element_type=F32))
                dvt.append(lax.dot_general(kti[:, hs].astype(BF16), dst_b, (((1,), (1,)), ((), ())),
                                           preferred_element_type=F32))
                dd.append(jnp.sum(dst_h * st_h, axis=0, keepdims=True))
                upd = lax.dot_general(do_h, qti[:, hs].astype(BF16), (((0,), (0,)), ((), ())),
                                      preferred_element_type=F32)
                dst_ref[hs, :] = dst_h * di[0:1, hs] + upd
            dqt = jnp.concatenate(dqt, axis=1)
            dkt = jnp.concatenate(dkt, axis=1)
            dvt = jnp.concatenate(dvt, axis=1)
            dd = jnp.concatenate(dd, axis=1)
            dbl = jnp.sum(dkt * kti, axis=0, keepdims=True) + dd * di[0:1, :]
            db = qi * dq_in - ki * dk_in + dqt * qti - dkt * kti
            db_s[rows, :] = db + jnp.where(last_row, dbl, 0.0)
            dq_s[rows, :] = dq_in + dqt * eb_s[rows, :]
            dk_s[rows, :] = dk_in + dkt * ekb_s[rows, :]
            dv_s[rows, :] = dv_in + dvt

        def some_blocks(jj, carry):
            for slot in range(HG_SLOTS):
                block(nb - 1 - slot - HG_SLOTS * jj, slot)
            return carry

        lax.fori_loop(0, nb // HG_SLOTS, some_blocks, 0)

        dg = _dot3(upper_ref[...], db_s[...])
        dhq_ref[...] = (dq_s[...] * (sq * (1.0 + hq * (1.0 - sq)))).astype(dhq_ref.dtype)
        df = dg / f - dk_s[...]
        dhf_ref[...] = (df * (1.0 - lbv) * (sg * (1.0 - sg))).astype(dhf_ref.dtype)
        dhi_ref[...] = dv_s[...].astype(dhi_ref.dtype)
        dlb_ref[...] += _colsum(df * (1.0 - sg))

    rev = lambda s, t: s * n_tiles + (n_tiles - 1 - t)
    col = lambda off: functools.partial(lambda s, t, blk: (rev(s, t), blk), blk=off // W)
    const = lambda m: pl.BlockSpec(m.shape, lambda s, t: (0, 0))
    row = pl.BlockSpec((HG_TILE, W), lambda s, t: (rev(s, t), 0))
    tile_f32 = pltpu.VMEM((HG_TILE, W), F32)
    n2 = HG_STACK
    return pl.pallas_call(
        body, name=name,
        grid=(n_seq, n_tiles),
        in_specs=[pl.BlockSpec((HG_TILE, W), col(offs[0])), pl.BlockSpec((HG_TILE, W), col(offs[1])),
                  pl.BlockSpec((HG_TILE, W), col(offs[2])), row,
                  pl.BlockSpec((nb, W, HG_DIM), lambda s, t: (rev(s, t), 0, 0)),
                  const(lb), const(lower), const(upper), const(total), const(bd), const(sel_t), const(sel_s)],
        out_specs=[row, row, row, pl.BlockSpec((1, W), lambda s, t: (0, 0))],
        out_shape=[jax.ShapeDtypeStruct((T, W), BF16)] * 3 + [jax.ShapeDtypeStruct((1, W), F32)],
        scratch_shapes=[pltpu.VMEM((W, HG_DIM), F32)] + [tile_f32] * 13
                       + [pltpu.VMEM((HG_SLOTS, n2, W), BF16), pltpu.VMEM((HG_SLOTS, n2, W), F32),
                          pltpu.VMEM((HG_SLOTS, n2, W), BF16)],
        compiler_params=_cparams(("arbitrary", "arbitrary")),
    )(proj, proj, proj, do, states, lb, lower, upper, total, bd, sel_t, sel_s)


def _diag_mask(tq):
    return lax.broadcasted_iota(jnp.int32, (tq, tq), 1) <= lax.broadcasted_iota(jnp.int32, (tq, tq), 0)


def _qk(q, k):
    return lax.dot_general(q, k, (((1,), (1,)), ((), ())), preferred_element_type=F32)


def _causal_pairs(n, sweeps=1, by_key=False):
    if by_key:
        rows = [(i, j, 0) for j in range(n) for i in range(j, n)]
    else:
        rows = [(i, j, s) for i in range(n) for s in range(sweeps) for j in range(i + 1)]
    return tuple(jnp.asarray(np.array([r[c] for r in rows], np.int32)) for c in range(3))


def _fox_placement(fh):
    hw, wa = fh * FOX_HDIM, fh * FOX_AUG
    pq, pk = np.zeros((hw, wa), np.float32), np.zeros((hw, wa), np.float32)
    aq, ak = np.zeros((3 * LANES, wa), np.float32), np.zeros((3 * LANES, wa), np.float32)
    oq, ok = np.zeros((1, wa), np.float32), np.zeros((1, wa), np.float32)
    for h in range(fh):
        src, dst = np.arange(h * FOX_HDIM, (h + 1) * FOX_HDIM), np.arange(h * FOX_AUG, h * FOX_AUG + FOX_HDIM)
        pq[src, dst] = FOX_HDIM ** -0.5
        pk[src, dst] = 1.0
        gate = h * FOX_AUG + FOX_HDIM
        for r in range(3):
            aq[r * LANES + h, gate + r] = 1.0
            ak[r * LANES + h, gate + 3 + r] = -1.0
        oq[0, gate + 3:gate + 6] = 1.0
        ok[0, gate:gate + 3] = 1.0
    bf = lambda m: jnp.asarray(m, dtype=BF16)
    return {"pq": bf(pq), "pk": bf(pk), "aq": bf(aq), "ak": bf(ak), "oq": jnp.asarray(oq), "ok": jnp.asarray(ok),
            "pqt": bf(pq.T), "pkt": bf(pk.T)}


def _fox_specs(tq, fh, heads=1):
    groups = fh // heads

    def spec(tab):
        return pl.BlockSpec((None, tq, heads * FOX_AUG), lambda b, t, *tabs: (b // groups, tabs[tab][t], b % groups))
    return spec(0), spec(1)


def fox_fwd(qa, ka, va, *, name):
    n_seq, S, width = qa.shape
    fh = width // FOX_AUG
    nh = FOX_FWD_HEADS
    BH = n_seq * fh // nh
    tq = min(FOX_TQ, S)
    itab, jtab, _ = _causal_pairs(S // tq)

    def body(itab_ref, jtab_ref, q_ref, k_ref, v_ref, o_ref, ox_ref, lse_ref, *scratch):
        t = pl.program_id(1)
        i, j = itab_ref[t], jtab_ref[t]
        per_head = [scratch[4 * h:4 * h + 4] for h in range(nh)]

        @pl.when(j == 0)
        def _():
            for m_s, l_s, acc_s, acc_lo_s in per_head:
                m_s[...] = jnp.full_like(m_s, NEG_INF)
                l_s[...] = jnp.zeros_like(l_s)
                acc_s[...] = jnp.zeros_like(acc_s)
                acc_lo_s[...] = jnp.zeros_like(acc_lo_s)

        def step(on_diagonal):
            for h, (m_s, l_s, acc_s, acc_lo_s) in enumerate(per_head):
                lanes = slice(h * FOX_AUG, (h + 1) * FOX_AUG)
                s = _qk(q_ref[:, lanes], k_ref[:, lanes])
                if on_diagonal:
                    s = jnp.where(_diag_mask(tq), s, NEG_INF)
                m_prev = m_s[...]
                m_new = jnp.maximum(m_prev, jnp.max(s, axis=-1, keepdims=True))
                alpha = jnp.exp(m_prev - m_new)
                p = jnp.exp(s - m_new[:, 0:1])
                p_hi = p.astype(BF16)
                p_lo = (p - p_hi.astype(F32)).astype(BF16)
                v = v_ref[:, lanes]
                l_s[...] = alpha * l_s[...] + jnp.sum(p, axis=-1, keepdims=True)
                acc_s[...] = alpha * acc_s[...] + jnp.dot(p_hi, v, preferred_element_type=F32)
                acc_lo_s[...] = alpha * acc_lo_s[...] + jnp.dot(p_lo, v, preferred_element_type=F32)
                m_s[...] = m_new

        @pl.when(j < i)
        def _():
            step(False)

        @pl.when(j == i)
        def _():
            step(True)
            for h, (m_s, l_s, acc_s, acc_lo_s) in enumerate(per_head):
                lanes = slice(h * FOX_AUG, (h + 1) * FOX_AUG)
                inv_l = 1.0 / l_s[...]
                o_ref[:, lanes] = (acc_s[...] * inv_l).astype(o_ref.dtype)
                ox_ref[:, lanes] = (acc_s[...] + acc_lo_s[...]) * inv_l
                lse_ref[:, lanes] = m_s[...] + jnp.log(l_s[...])

    qspec, kspec = _fox_specs(tq, fh, nh)
    wide = jax.ShapeDtypeStruct((n_seq, S, width), F32)
    return pl.pallas_call(
        body, name=name,
        grid_spec=pltpu.PrefetchScalarGridSpec(
            num_scalar_prefetch=2, grid=(BH, itab.shape[0]),
            in_specs=[qspec, kspec, kspec],
            out_specs=[qspec, qspec, qspec],
            scratch_shapes=[pltpu.VMEM((tq, LANES), F32)] * (4 * nh)),
        out_shape=[jax.ShapeDtypeStruct((n_seq, S, width), BF16), wide, wide],
        compiler_params=_cparams(("parallel", "arbitrary")),
    )(itab, jtab, qa, ka, va)


def _fox_ds(q, k, v, do, ox, lse, on_diagonal):
    s = _qk(q, k)
    if on_diagonal:
        s = jnp.where(_diag_mask(s.shape[0]), s, NEG_INF)
    p = jnp.exp(s - lse[:, 0:1])
    delta = jnp.sum(do.astype(F32) * ox, axis=-1, keepdims=True)
    return p, p * (_qk(do, v) - delta)


def fox_bwd(qa, ka, va, do, ox, lse, *, name):
    n_seq, S, width = qa.shape
    fh = width // FOX_AUG
    BH = n_seq * fh
    tq = min(FOX_TQ, S)
    itab, jtab, _ = _causal_pairs(S // tq)

    def body(itab_ref, jtab_ref, q_ref, k_ref, v_ref, do_ref, ox_ref, lse_ref, dq_ref, dk_ref, dv_ref, dsum_ref):
        t = pl.program_id(1)
        i, j = itab_ref[t], jtab_ref[t]

        @pl.when(t == 0)
        def _():
            dq_ref[...] = jnp.zeros_like(dq_ref)
            dk_ref[...] = jnp.zeros_like(dk_ref)
            dv_ref[...] = jnp.zeros_like(dv_ref)
            dsum_ref[...] = jnp.zeros_like(dsum_ref)

        q_rows = pl.ds(pl.multiple_of(i * tq, tq), tq)
        k_rows = pl.ds(pl.multiple_of(j * tq, tq), tq)

        def step(on_diagonal):
            q, k, do = q_ref[...], k_ref[...], do_ref[...]
            p, ds = _fox_ds(q, k, v_ref[...], do, ox_ref[...], lse_ref[...], on_diagonal)
            ds_b = ds.astype(BF16)
            tn = (((0,), (0,)), ((), ()))
            dq_ref[q_rows, :] += jnp.dot(ds_b, k, preferred_element_type=F32)
            dk_ref[k_rows, :] += lax.dot_general(ds_b, q, tn, preferred_element_type=F32)
            dv_ref[k_rows, :] += lax.dot_general(p.astype(BF16), do, tn, preferred_element_type=F32)
            dsum_ref[:, k_rows] += _colsum(ds)

        @pl.when(j < i)
        def _():
            step(False)

        @pl.when(j == i)
        def _():
            step(True)

    qspec, kspec = _fox_specs(tq, fh)
    whole = pl.BlockSpec((None, S, FOX_AUG), lambda b, t, it, jt: (b // fh, 0, b % fh))
    wide = jax.ShapeDtypeStruct((n_seq, S, width), F32)
    return pl.pallas_call(
        body, name=name,
        grid_spec=pltpu.PrefetchScalarGridSpec(
            num_scalar_prefetch=2, grid=(BH, itab.shape[0]),
            in_specs=[qspec, kspec, kspec, qspec, qspec, qspec],
            out_specs=[whole, whole, whole, pl.BlockSpec((None, 1, S), lambda b, t, it, jt: (b, 0, 0))]),
        out_shape=[wide, wide, wide, jax.ShapeDtypeStruct((BH, 1, S), F32)],
        compiler_params=_cparams(("parallel", "arbitrary")),
    )(itab, jtab, qa, ka, va, do, ox, lse)


def seq_cumsum(x, n_seq, seq, *, reverse, name):
    T, C = x.shape
    tb = min(256, seq)
    n = seq // tb
    r = np.arange(tb)
    tri = (r[None, :] >= r[:, None]) if reverse else (r[None, :] <= r[:, None])
    tri = jnp.asarray(tri.astype(np.float32), dtype=BF16)

    def body(x_ref, tri_ref, o_ref, carry_s):
        @pl.when(pl.program_id(1) == 0)
        def _():
            carry_s[...] = jnp.zeros_like(carry_s)

        xv = x_ref[...]
        o_ref[...] = _dot3(tri_ref[...], xv) + carry_s[...]
        carry_s[...] += _colsum(xv)

    blk = (lambda s, t: (s * n + (n - 1 - t), 0)) if reverse else (lambda s, t: (s * n + t, 0))
    return pl.pallas_call(
        body, name=name,
        grid=(n_seq, n),
        in_specs=[pl.BlockSpec((tb, C), blk), pl.BlockSpec((tb, tb), lambda s, t: (0, 0))],
        out_specs=pl.BlockSpec((tb, C), blk),
        out_shape=jax.ShapeDtypeStruct((T, C), F32),
        scratch_shapes=[pltpu.VMEM((1, C), F32)],
        compiler_params=_cparams(("arbitrary", "arbitrary")),
    )(x, tri)


def _place():
    return lax.axis_index("x"), lax.axis_index("y"), lax.axis_index("c")


def _other_chips(x, y):
    return [(1 - x, y), (x, 1 - y), (1 - x, 1 - y)]


def _hbm_call(body, ins, out_shape, n_sems, *, name):
    hbm = pl.BlockSpec(memory_space=pl.ANY)
    return pl.pallas_call(
        body, name=name,
        in_specs=[hbm] * len(ins), out_specs=[hbm] * len(out_shape), out_shape=out_shape,
        scratch_shapes=[pltpu.SemaphoreType.DMA((n_sems,)), pltpu.SemaphoreType.DMA((n_sems,)),
                        pltpu.SemaphoreType.DMA((len(ins),))],
        compiler_params=pltpu.CompilerParams(has_side_effects=True),
    )(*ins)


def allgather_chips(shards, *, name):
    return _exchange_call(allgather_rider(shards), name=name)


def _allgather_ops(x_refs, o_refs, send_sems, recv_sems, local_sems):
    def copies():
        x, y, c = _place()
        me = 2 * x + y
        chips = _other_chips(x, y)
        own, first, passed, landed, handed = [], [], [], [], []
        for b, (x_ref, o_ref) in enumerate(zip(x_refs, o_refs)):
            half = x_ref.shape[0] // 2
            mine, theirs = pl.ds(c * half, half), pl.ds((1 - c) * half, half)
            own.append(pltpu.make_async_copy(x_ref, o_ref.at[me], local_sems.at[b]))

            def copy(k, src, chip, rows, to, o_ref=o_ref, b=b):
                return pltpu.make_async_remote_copy(src_ref=src, dst_ref=o_ref.at[2 * chip[0] + chip[1], rows],
                                                    send_sem=send_sems.at[6 * b + k], recv_sem=recv_sems.at[6 * b + k],
                                                    device_id=to, device_id_type=MESH)
            for j, chip in enumerate(chips):
                first.append(copy(j, x_ref.at[mine], (x, y), mine, (*chip, c)))
                landed.append(copy(j, x_ref.at[mine], chip, mine, (*chip, c)))
                passed.append(copy(3 + j, o_ref.at[2 * chip[0] + chip[1], mine], chip, mine, (x, y, 1 - c)))
                handed.append(copy(3 + j, x_ref.at[mine], chip, theirs, (x, y, 1 - c)))
        return own, first, passed, landed, handed

    def start():
        own, first, _, _, _ = copies()
        for cp in own + first:
            cp.start()

    def finish():
        own, first, passed, landed, handed = copies()
        for arrived, forward in zip(landed, passed):
            arrived.wait_recv()
            forward.start()
        for cp in handed:
            cp.wait_recv()
        for cp in first + passed:
            cp.wait_send()
        for cp in own:
            cp.wait()
    return start, finish


def _scatter_ops(x_refs, o_refs, send_sems, recv_sems, local_sems):
    def copies():
        x, y, c = _place()
        return [pltpu.make_async_remote_copy(
            src_ref=x_ref.at[2 * px + py], dst_ref=o_ref.at[j], send_sem=send_sems.at[3 * b + j],
            recv_sem=recv_sems.at[3 * b + j], device_id=(px, py, c), device_id_type=MESH)
            for b, (x_ref, o_ref) in enumerate(zip(x_refs, o_refs)) for j, (px, py) in enumerate(_other_chips(x, y))]

    def start():
        for cp in copies():
            cp.start()

    def finish():
        sends = copies()
        for cp in sends:
            cp.wait_recv()
        for cp in sends:
            cp.wait_send()
    return start, finish


class Rider(NamedTuple):
    ins: list
    out_shape: list
    n_sems: int
    ops: object

    def specs(self):
        hbm = pl.BlockSpec(memory_space=pl.ANY)
        sems = [pltpu.SemaphoreType.DMA((self.n_sems,)), pltpu.SemaphoreType.DMA((self.n_sems,)),
                pltpu.SemaphoreType.DMA((len(self.ins),))]
        return [hbm] * len(self.ins), [hbm] * len(self.out_shape), sems

    def wrap(self, body, n_in, n_out, grid_rank):
        k_in, k_out = len(self.ins), len(self.out_shape)

        def carried(*refs):
            ins, r_ins = refs[:n_in], refs[n_in:n_in + k_in]
            outs = refs[n_in + k_in:n_in + k_in + n_out]
            r_outs = refs[n_in + k_in + n_out:n_in + k_in + n_out + k_out]
            scratch, sems = refs[n_in + k_in + n_out + k_out:-3], refs[-3:]
            first = functools.reduce(jnp.logical_and, [pl.program_id(a) == 0 for a in range(grid_rank)])
            last = functools.reduce(jnp.logical_and,
                                    [pl.program_id(a) == pl.num_programs(a) - 1 for a in range(grid_rank)])
            pl.when(first)(lambda: self.ops(r_ins, r_outs, *sems)[0]())
            body(*ins, *outs, *scratch)
            pl.when(last)(lambda: self.ops(r_ins, r_outs, *sems)[1]())
        return carried


def _exchange_call(rider, *, name):
    def body(*refs):
        k = len(rider.ins)
        start, finish = rider.ops(refs[:k], refs[k:k + len(rider.out_shape)], *refs[-3:])
        start()
        finish()
    in_specs, out_specs, sems = rider.specs()
    return pl.pallas_call(body, name=name, in_specs=in_specs, out_specs=out_specs, out_shape=rider.out_shape,
                          scratch_shapes=sems, compiler_params=pltpu.CompilerParams(has_side_effects=True))(*rider.ins)


def allgather_rider(shards):
    assert all(s.shape[0] % (2 * ROW_ALIGN) == 0 for s in shards)
    return Rider(list(shards), [jax.ShapeDtypeStruct((4,) + s.shape, s.dtype) for s in shards], 6 * len(shards),
                 _allgather_ops)


def scatter_rider(parts):
    return Rider(list(parts), [jax.ShapeDtypeStruct((3,) + p.shape[1:], p.dtype) for p in parts], 3 * len(parts),
                 _scatter_ops)


def scatter_chips(parts, *, name):
    return _exchange_call(scatter_rider(parts), name=name)


def swap_cores(vs, *, name):
    nb = len(vs)

    def body(*refs):
        x_refs, o_refs = refs[:nb], refs[nb:2 * nb]
        send_sems, recv_sems, _ = refs[2 * nb:]
        x, y, c = _place()
        copies = [pltpu.make_async_remote_copy(src_ref=x_ref, dst_ref=o_ref, send_sem=send_sems.at[b],
                                               recv_sem=recv_sems.at[b], device_id=(x, y, 1 - c), device_id_type=MESH)
                  for b, (x_ref, o_ref) in enumerate(zip(x_refs, o_refs))]
        for cp in copies:
            cp.start()
        for cp in copies:
            cp.wait()

    return _hbm_call(body, vs, [jax.ShapeDtypeStruct(v.shape, v.dtype) for v in vs], nb, name=name)


def allreduce_small(v, *, name):
    R, C = v.shape

    def body(x_ref, o_ref, gath_ref, send_sems, recv_sems):
        x, y, c = _place()
        me = 4 * x + 2 * y + c
        gath_ref[me] = x_ref[...]
        flips = [(k >> 2 & 1, k >> 1 & 1, k & 1) for k in range(1, 8)]
        sends = []
        for j, (fx, fy, fc) in enumerate(flips):
            peer = (x ^ fx, y ^ fy, c ^ fc)
            cp = pltpu.make_async_remote_copy(src_ref=x_ref, dst_ref=gath_ref.at[me], send_sem=send_sems.at[j],
                                              recv_sem=recv_sems.at[j], device_id=peer, device_id_type=MESH)
            cp.start()
            sends.append(cp)
        for j, (fx, fy, fc) in enumerate(flips):
            peer = (x ^ fx, y ^ fy, c ^ fc)
            pltpu.make_async_remote_copy(src_ref=x_ref, dst_ref=gath_ref.at[4 * peer[0] + 2 * peer[1] + peer[2]],
                                         send_sem=send_sems.at[j], recv_sem=recv_sems.at[j], device_id=peer,
                                         device_id_type=MESH).wait_recv()
        for cp in sends:
            cp.wait_send()
        total = gath_ref[0]
        for d in range(1, 8):
            total = total + gath_ref[d]
        o_ref[...] = total

    vm = pl.BlockSpec(memory_space=pltpu.VMEM)
    out, _ = pl.pallas_call(
        body, name=name,
        in_specs=[vm], out_specs=[vm, vm],
        out_shape=[jax.ShapeDtypeStruct((R, C), F32), jax.ShapeDtypeStruct((8, R, C), F32)],
        scratch_shapes=[pltpu.SemaphoreType.DMA((7,)), pltpu.SemaphoreType.DMA((7,))],
        compiler_params=pltpu.CompilerParams(has_side_effects=True),
    )(v)
    return out


ROW_ALIGN = 16
PACK_W = 1024
SUM_TILE = 512
BIG_WEIGHTS = (("w_in", 1), ("w_a", 1), ("w_b", 1), ("w_o", 0), ("w_ff1", 1), ("w_ff2", 0), ("w_pg", 0), ("w_p", 1))


def _b_layout(d, ple):
    hw, q = d // 2, d // 4
    small = 2 * d + 2 * q
    lay = {"w_ff1": (0, 0, d, d), "w_ff2": (d, 0, d, d), "w_o": (2 * d, 0, q, d), "w_pg": (2 * d + q, 0, q, d),
           "w_a": (small, 0, hw, q), "w_b": (small, q, hw, q), "w_p": (small, 2 * q, ple, q)}
    return lay, small + hw


def pack_a(w_in_shard):
    rows, cols = w_in_shard.shape
    pad = -cols % LANES
    return jnp.concatenate([w_in_shard, jnp.zeros((rows, pad), w_in_shard.dtype)], axis=1)


def pack_b(shards, d):
    hw, q = d // 2, d // 4
    dt = shards["w_a"].dtype
    wp = shards["w_p"]
    wp = jnp.concatenate([wp, jnp.zeros((hw - wp.shape[0], q), dt)], axis=0)
    small = jnp.concatenate([shards["w_a"], shards["w_b"], wp, jnp.zeros((hw, d - 3 * q), dt)], axis=1)
    return jnp.concatenate([shards["w_ff1"], shards["w_ff2"], shards["w_o"], shards["w_pg"], small], axis=0)


def unpack_b(buf, lay):
    return {nm: buf[r0:r0 + rows, c0:c0 + cols] for nm, (r0, c0, rows, cols) in lay.items()}


def _win_layout(d):
    hw = d // 2
    fh = hw // FOX_HDIM
    orig = {"hq": (0, hw), "hf": (hw, hw), "hi": (2 * hw, hw), "hg": (3 * hw, hw), "fq": (4 * hw, hw),
            "fk": (5 * hw, hw), "fv": (6 * hw, hw), "ff": (7 * hw, fh), "ga": (7 * hw + fh, d), "gb": (7 * hw + fh + d, d)}
    order = ["ga", "gb", "hq", "hf", "hi", "hg", "fq", "fk", "fv", "ff"]
    mine, off = {}, 0
    for nm in order:
        width = orig[nm][1] if nm != "ff" else LANES
        mine[nm] = (off, width)
        off += width
    return orig, order, mine, off


def _adam_fn(rows, vecs):
    w, g, m, v = rows
    m2 = ADAM_B1 * m + (1.0 - ADAM_B1) * g
    v2 = ADAM_B2 * v + (1.0 - ADAM_B2) * (g * g)
    m_hat = m2 / (1.0 - ADAM_B1 ** ADAM_STEP)
    v_hat = v2 / (1.0 - ADAM_B2 ** ADAM_STEP)
    delta = -ADAM_LR * (m_hat / (jnp.sqrt(v_hat) + ADAM_EPS) + ADAM_WD * w)
    return [delta, m2, v2], []


def adamw_small(small, p0, ws, ms, vs, *, name):
    n = len(ws)
    hw = p0.shape[1]
    fh = ws[8].shape[1]

    def body(small_ref, p0_ref, *refs):
        w_refs, m_refs, v_refs = refs[:n], refs[n:2 * n], refs[2 * n:3 * n]
        g_out, d_out, m_out, v_out = (refs[(3 + k) * n:(4 + k) * n] for k in range(4))
        sm = small_ref[...]
        p = p0_ref[...]
        d_lb = sm[6:7, hw:2 * hw] * (p * (1.0 - p))
        grads = [sm[r:r + 1, :] for r in range(6)]
        grads += [jnp.concatenate([d_lb, -d_lb], axis=0), sm[6:7, :hw], sm[7:8, :fh]]
        for i in range(n):
            (delta, m2, v2), _ = _adam_fn([w_refs[i][...], grads[i], m_refs[i][...], v_refs[i][...]], [])
            g_out[i][...], d_out[i][...], m_out[i][...], v_out[i][...] = grads[i], delta, m2, v2

    shapes = [jax.ShapeDtypeStruct(w.shape, F32) for w in ws]
    return pl.pallas_call(body, name=name, out_shape=shapes * 4)(small, p0, *ws, *ms, *vs)


def adamw(w, g, m, v, *, name):
    c = w.shape[1]
    (delta, m2, v2), _ = rowwise(_adam_fn, [w, g, m, v], [], [(c, F32)] * 3, name=name, tm=256)
    return delta, m2, v2


def kernel(x, p, ln0_g, ln0_b, w_in, hg_lb, hg_norm_g, fox_fb, w_a, w_b, w_o, ln1_g, ln1_b, w_ff1, w_ff2, w_pg, w_p, ln2_g, ln2_b, loss_target, m_ln0_g, m_ln0_b, m_w_in, m_hg_lb, m_hg_norm_g, m_fox_fb, m_w_a, m_w_b, m_w_o, m_ln1_g, m_ln1_b, m_w_ff1, m_w_ff2, m_w_pg, m_w_p, m_ln2_g, m_ln2_b, v_ln0_g, v_ln0_b, v_w_in, v_hg_lb, v_hg_norm_g, v_fox_fb, v_w_a, v_w_b, v_w_o, v_ln1_g, v_ln1_b, v_w_ff1, v_w_ff2, v_w_pg, v_w_p, v_ln2_g, v_ln2_b):
    n_seq, seq, d = x.shape
    T = n_seq * seq
    hw = d // 2
    fh = hw // FOX_HDIM
    bh = n_seq * fh
    orig, order, mine, n_in = _win_layout(d)

    big = {"w_in": w_in[0], "w_a": w_a[0], "w_b": w_b[0], "w_o": w_o[0], "w_ff1": w_ff1[0], "w_ff2": w_ff2[0],
           "w_pg": w_pg[0], "w_p": w_p[0]}
    big_m = {"w_in": m_w_in[0], "w_a": m_w_a[0], "w_b": m_w_b[0], "w_o": m_w_o[0], "w_ff1": m_w_ff1[0],
             "w_ff2": m_w_ff2[0], "w_pg": m_w_pg[0], "w_p": m_w_p[0]}
    big_v = {"w_in": v_w_in[0], "w_a": v_w_a[0], "w_b": v_w_b[0], "w_o": v_w_o[0], "w_ff1": v_w_ff1[0],
             "w_ff2": v_w_ff2[0], "w_pg": v_w_pg[0], "w_p": v_w_p[0]}
    names = [nm for nm, _ in BIG_WEIGHTS]
    axis = dict(BIG_WEIGHTS)
    ple = w_p.shape[1]
    lay, b_rows = _b_layout(d, ple)
    in_cols = big["w_in"].shape[1]

    gather_w_in = allgather_rider([pack_a(big["w_in"].astype(BF16))])
    gather_rest = allgather_rider([pack_b({nm: big[nm].astype(BF16) for nm in names if nm != "w_in"}, d)])

    x2 = x.reshape(T, d)
    tgt = loss_target.reshape(T, d)
    p_b = p.reshape(T, p.shape[-1]).astype(BF16)
    vec = lambda a: a.reshape(1, -1)
    probs = jax.nn.softmax(hg_lb, axis=0)
    lb = vec(probs[0])

    def ln0_fn(rows, vecs):
        h = _ln_stats(rows[0]) * vecs[0] + vecs[1]
        return [h, h], []
    (h0, h0b), _, (a_all,) = rowwise(ln0_fn, [x2], [vec(ln0_g), vec(ln0_b)], [(d, F32), (d, BF16)], name="ln0_fwd",
                                     rider=gather_w_in)
    win = jnp.concatenate([a_all[s, :, :in_cols] for s in range(4)], axis=1)
    win_mine = jnp.concatenate(
        [win[:, orig[nm][0]:orig[nm][0] + orig[nm][1]] for nm in order]
        + [jnp.zeros((d, LANES - fh), BF16)], axis=1)
    proj = matmul_nn(h0b, win_mine, name="in_proj")

    o_raw, hg_states, (b_all,) = hgrn2_fwd(proj, [mine["hq"][0], mine["hf"][0], mine["hi"][0]], lb, n_seq, seq,
                                           name="hgrn2_fwd", rider=gather_rest)
    view = lambda nm, k, n: WView(b_all, lay[nm][0], lay[nm][1], k, n, axis[nm])
    w_ff1_v, w_ff2_v = view("w_ff1", d, 4 * d), view("w_ff2", 4 * d, d)

    def whole(nm):
        r0, c0, rows, cols = lay[nm]
        return jnp.concatenate([b_all[s, r0:r0 + rows, c0:c0 + cols] for s in range(4)], axis=axis[nm])
    w_o_v, w_pg_v, w_a_v, w_p_v, w_b_full = whole("w_o"), whole("w_pg"), whole("w_a"), whole("w_p"), whole("w_b")

    def ya_fn(rows, vecs):
        o, hg = rows
        outs = []
        for h in range(HG_HEADS):
            oh = o[:, h * HG_DIM:(h + 1) * HG_DIM]
            outs.append(oh * lax.rsqrt(jnp.mean(oh * oh, axis=-1, keepdims=True) + RMS_EPS))
        y = jnp.concatenate(outs, axis=1) * vecs[0] * (hg * _sigmoid(hg))
        return [y], []
    (y_a,), _ = rowwise(ya_fn, [o_raw, (proj,) + mine["hg"]], [hg_norm_g], [(hw, BF16)], name="hgrn2_out_fwd")

    fb_pad = jnp.concatenate([fox_fb, jnp.zeros((1, LANES - fh), F32)], axis=1)

    def lf_fn(rows, vecs):
        u = rows[0] + vecs[0]
        return [jnp.minimum(u, 0.0) - jnp.log(1.0 + jnp.exp(-jnp.abs(u)))], []
    (lf,), _ = rowwise(lf_fn, [(proj,) + mine["ff"]], [fb_pad], [(LANES, F32)], name="fox_logf")
    c_cum = seq_cumsum(lf, n_seq, seq, reverse=False, name="fox_cumsum")

    place = _fox_placement(fh)

    def prep_fn(rows, vecs):
        fq_, fk_, fv_, cc = rows
        pq, pk, aq, ak, oq, ok = vecs
        parts = jnp.concatenate(_split3(cc), axis=1)
        mm = lambda a_, b_: jnp.dot(a_, b_, preferred_element_type=F32)
        q_ = mm(fq_.astype(BF16), pq) + mm(parts, aq) + oq
        k_ = mm(fk_.astype(BF16), pk) + mm(parts, ak) + ok
        return [q_, k_, mm(fv_.astype(BF16), pk)], []
    wa = fh * FOX_AUG
    (qa, ka, va), _ = rowwise(prep_fn, [(proj,) + mine["fq"], (proj,) + mine["fk"], (proj,) + mine["fv"], c_cum],
                              [place[nm] for nm in ("pq", "pk", "aq", "ak", "oq", "ok")], [(wa, BF16)] * 3,
                              name="fox_prep")
    as_seq = lambda t2d: t2d.reshape(n_seq, seq, t2d.shape[1])
    o_fox, ox_fox, lse = fox_fwd(as_seq(qa), as_seq(ka), as_seq(va), name="fox_fwd")
    y_b = o_fox.reshape(T, wa)
    wb_pad = jnp.concatenate([w_b_full.reshape(fh, FOX_HDIM, d), jnp.zeros((fh, FOX_AUG - FOX_HDIM, d), BF16)],
                             axis=1).reshape(wa, d)

    fused_tm = 512
    pa = matmul_nn(y_a, w_a_v, name="proj_a")

    def merge_post(pb_, aux, vecs):
        ga, gb, a = aux
        return [_sigmoid(ga) * a + _sigmoid(gb) * pb_, pb_], []
    (merged, pb), _ = matmul_nn(y_b, wb_pad, name="proj_b_merge", tm=fused_tm, post=merge_post,
                                post_aux=[(proj,) + mine["ga"], (proj,) + mine["gb"], pa], post_outs=[BF16, F32])

    def ln1_post(mix, aux, vecs):
        z = ALPHA * aux[0] + mix
        h = _ln_stats(z) * vecs[0] + vecs[1]
        return [z, h, h], []
    (z1, h1, h1b), _ = matmul_nn(merged, w_o_v, name="out_proj_ln1", tm=fused_tm, post=ln1_post, post_aux=[h0],
                                 post_vecs=[ln1_g, ln1_b], post_outs=[F32, F32, BF16])

    relu2 = lambda u: jnp.square(jnp.maximum(u, 0.0))
    act = matmul_nn(h1b, w_ff1_v, name="ff1", out_dtype=BF16, epilogue=relu2)
    pg = matmul_nn(h1b, w_pg_v, name="ple_gate")
    pe = matmul_nn(p_b, w_p_v, name="ple_embed")

    def head_post(ffv, aux, vecs):
        h1v, pgv, pev, t = aux
        g2, b2 = vecs
        sp = _sigmoid(pgv)
        z = ALPHA * h1v + ffv + sp * pev
        y = _ln_stats(z) * g2 + b2
        err = y - t
        loss_rows = 0.5 * jnp.mean(err * err, axis=-1, keepdims=True)
        dy = err * (1.0 / d)
        dz, dg2, db2 = _ln_bwd(z, dy, g2)
        loss_acc = jnp.broadcast_to(_colsum(loss_rows), (1, d))
        return [dz, dz, dz * pev * (sp * (1.0 - sp)), dz * sp], [dg2, db2, loss_acc]
    (dz2, dz2b, dpg, dpe), (g_ln2_g, g_ln2_b, loss_part) = matmul_nn(
        act, w_ff2_v, name="ff2_head", tm=fused_tm, post=head_post, post_aux=[h1, pg, pe, tgt],
        post_vecs=[ln2_g, ln2_b], post_outs=[F32, BF16, BF16, BF16], post_accs=[d, d, d])

    dact = lambda da, a: da * (2.0 * jnp.sqrt(a.astype(F32)))
    du = matmul_nn(dz2b, w_ff2_v, transpose_rhs=True, name="d_ff2", out_dtype=BF16, epilogue=dact, aux=act)
    dh1_pg = matmul_nn(dpg, w_pg_v, transpose_rhs=True, name="d_ple_gate")

    def ln1_bwd_post(dh1_ff, aux, vecs):
        dh1 = ALPHA * aux[0] + dh1_ff + aux[1]
        dz, dg, db = _ln_bwd(aux[2], dh1, vecs[0])
        return [dz, dz], [dg, db]
    (dz1, dz1b), (g_ln1_g, g_ln1_b) = matmul_nn(
        du, w_ff1_v, transpose_rhs=True, name="d_ff1_ln1", tm=fused_tm, post=ln1_bwd_post, post_aux=[dz2, dh1_pg, z1],
        post_vecs=[ln1_g], post_outs=[F32, BF16], post_accs=[d, d])

    def merge_bwd_post(dm, aux, vecs):
        ga, gb, a, b = aux
        sa, sb = _sigmoid(ga), _sigmoid(gb)
        return [dm * a * (sa * (1.0 - sa)), dm * b * (sb * (1.0 - sb)), dm * sa, dm * sb], []
    (dga, dgb, dma, dmb), _ = matmul_nn(
        dz1b, w_o_v, transpose_rhs=True, name="d_out_proj_merge", tm=fused_tm, post=merge_bwd_post,
        post_aux=[(proj,) + mine["ga"], (proj,) + mine["gb"], pa, pb], post_outs=[BF16] * 4)
    dya = matmul_nn(dma, w_a_v, transpose_rhs=True, name="d_proj_a")
    dyb = matmul_nn(dmb, wb_pad, transpose_rhs=True, name="d_proj_b", out_dtype=BF16)

    def ya_bwd_fn(rows, vecs):
        o, hg, dy = rows
        ng = vecs[0]
        sg = _sigmoid(hg)
        gate = hg * sg
        dn_parts, do_parts, n_parts = [], [], []
        for h in range(HG_HEADS):
            hs = slice(h * HG_DIM, (h + 1) * HG_DIM)
            oh = o[:, hs]
            r = lax.rsqrt(jnp.mean(oh * oh, axis=-1, keepdims=True) + RMS_EPS)
            nh = oh * r
            dn = dy[:, hs] * ng[:, hs] * gate[:, hs]
            do_parts.append(r * (dn - nh * jnp.mean(dn * nh, axis=-1, keepdims=True)))
            n_parts.append(nh)
        nrm = jnp.concatenate(n_parts, axis=1)
        dhg = dy * nrm * ng * (sg * (1.0 + hg * (1.0 - sg)))
        return [jnp.concatenate(do_parts, axis=1), dhg], [_colsum(dy * nrm * gate)]
    (do_raw, dhg), (g_norm_g,) = rowwise(ya_bwd_fn, [o_raw, (proj,) + mine["hg"], dya], [hg_norm_g],
                                         [(hw, F32), (hw, BF16)], [hw], name="hgrn2_out_bwd")
    dhq, dhf, dhi, g_lb = hgrn2_bwd(proj, [mine["hq"][0], mine["hf"][0], mine["hi"][0]], lb, do_raw, hg_states,
                                    n_seq, seq, name="hgrn2_bwd")

    do_fox = as_seq(dyb)
    dqa, dka, dva, dsum = fox_bwd(as_seq(qa), as_seq(ka), as_seq(va), do_fox, ox_fox, lse, name="fox_bwd")

    def unprep_fn(rows, vecs):
        mm = lambda a_, b_: jnp.dot(a_.astype(BF16), b_, preferred_element_type=F32)
        return [mm(rows[0], vecs[0]), mm(rows[1], vecs[1]), mm(rows[2], vecs[1])], []
    (dfq, dfk, dfv), _ = rowwise(unprep_fn, [dqa.reshape(T, wa), dka.reshape(T, wa), dva.reshape(T, wa)],
                                 [place["pqt"], place["pkt"]], [(hw, BF16)] * 3, name="fox_unprep")
    dc = -dsum.reshape(n_seq, fh, seq).transpose(0, 2, 1).reshape(T, fh)
    dc = jnp.concatenate([dc, jnp.zeros((T, LANES - fh), F32)], axis=1)
    dlf = seq_cumsum(dc, n_seq, seq, reverse=True, name="fox_cumsum_bwd")

    def lf_bwd_fn(rows, vecs):
        u = rows[0] + vecs[0]
        du_ = rows[1] * _sigmoid(-u)
        return [du_], [_colsum(du_)]
    (dff_,), (g_fb,) = rowwise(lf_bwd_fn, [(proj,) + mine["ff"], dlf], [fb_pad], [(LANES, BF16)], [LANES],
                               name="fox_logf_bwd")

    dproj = jnp.concatenate([dga, dgb, dhq, dhf, dhi, dhg, dfq, dfk, dfv, dff_], axis=1)

    grads_b = jnp.zeros((4, b_rows, d), F32)
    for nm, lhs, rhs in (("w_ff1", h1b, du), ("w_ff2", act, dz2b)):
        grads_b = matmul_tn(lhs, rhs, name="g_" + nm, into=(grads_b, lay[nm][0], lay[nm][1], axis[nm]))
    gfull = {
        "w_a": matmul_tn(y_a, dma, name="g_w_a"),
        "w_b": matmul_tn(y_b, dmb, name="g_w_b").reshape(fh, FOX_AUG, d)[:, :FOX_HDIM].reshape(hw, d),
        "w_o": matmul_tn(merged, dz1b, name="g_w_o"),
        "w_pg": matmul_tn(h1b, dpg, name="g_w_pg"),
        "w_p": matmul_tn(p_b, dpe, name="g_w_p"),
    }

    def chip_parts(nm, s):
        g = gfull[nm]
        n = g.shape[axis[nm]] // 4
        return lax.slice_in_dim(g, s * n, (s + 1) * n, axis=axis[nm])
    for nm in gfull:
        grads_b = lax.dynamic_update_slice(grads_b, jnp.stack([chip_parts(nm, s) for s in range(4)]),
                                           (0, lay[nm][0], lay[nm][1]))
    me = 2 * lax.axis_index("x") + lax.axis_index("y")
    core = lax.axis_index("c")

    def sum2_fn(rows, vecs):
        s = rows[0] + rows[1].astype(F32)
        return [s, s], []

    def sum4_fn(rows, vecs):
        a, r0, r1, r2 = rows
        return [((a + r0.astype(F32)) + r1.astype(F32)) + r2.astype(F32)], []

    def chip_pair_sum(g, tag):
        h, cols = g.shape[1] // 2, g.shape[2]
        keep = lax.dynamic_slice_in_dim(g, core * h, h, axis=1)
        give = lax.dynamic_slice_in_dim(g, (1 - core) * h, h, axis=1).astype(BF16)
        (from_core,) = swap_cores([give], name="swap_partials_" + tag)
        (s32, s16), _ = rowwise(sum2_fn, [keep.reshape(4 * h, cols), from_core.reshape(4 * h, cols)], [],
                                [(cols, F32), (cols, BF16)], name="sum_cores_" + tag, tm=SUM_TILE)
        return s32.reshape(4, h, cols), s16.reshape(4, h, cols)

    def chip_sum(pr, gt, tag):
        own = lax.dynamic_index_in_dim(pr, me, axis=0, keepdims=False)
        (q,), _ = rowwise(sum4_fn, [own, gt[0], gt[1], gt[2]], [], [(own.shape[1], F32)], name="sum_chips_" + tag,
                          tm=SUM_TILE)
        return q

    pair_rest, pair_rest_b = chip_pair_sum(grads_b, "rest")
    gw_in_mine, (got_rest,) = matmul_tn(h0b, dproj, name="g_w_in", rider=scatter_rider([pair_rest_b]))
    gfull["w_in"] = jnp.concatenate([gw_in_mine[:, mine[nm][0]:mine[nm][0] + orig[nm][1]]
                                     for nm in ["hq", "hf", "hi", "hg", "fq", "fk", "fv", "ff", "ga", "gb"]], axis=1)
    grads_a = jnp.stack([pack_a(chip_parts("w_in", s)) for s in range(4)])
    pair_in, pair_in_b = chip_pair_sum(grads_a, "w_in")
    def ln0_bwd_post(dh0_in, aux, vecs):
        dx, dg, db = _ln_bwd(aux[1], dh0_in + ALPHA * aux[0], vecs[0])
        return [dx], [dg, db]
    ((dx,), (g_ln0_g, g_ln0_b)), (got_in,) = matmul_nn(
        dproj, win_mine, transpose_rhs=True, name="d_in_proj_ln0", tm=fused_tm, post=ln0_bwd_post,
        post_aux=[dz1, x2], post_vecs=[vec(ln0_g)], post_outs=[F32], post_accs=[d, d],
        rider=scatter_rider([pair_in_b]))
    q_half = [chip_sum(pair_in, got_in, "w_in"), chip_sum(pair_rest, got_rest, "rest")]
    q_other = swap_cores(q_half, name="swap_halves")
    g_a, g_b = [jnp.concatenate([jnp.where(core == 0, mine_, other), jnp.where(core == 0, other, mine_)], axis=0)
                for mine_, other in zip(q_half, q_other)]
    g_shards = unpack_b(g_b, lay)
    g_shards["w_in"] = g_a[:, :in_cols]

    assert d == PACK_W and 2 * hw == PACK_W and fh <= LANES
    small = allreduce_small(jnp.concatenate(
        [g_ln0_g, g_ln0_b, g_ln1_g, g_ln1_b, g_ln2_g, g_ln2_b, jnp.concatenate([g_norm_g, g_lb], axis=1),
         jnp.concatenate([g_fb, loss_part[:, LANES:]], axis=1)], axis=0), name="allreduce_small")
    loss = small[7, LANES]

    small_w = [vec(ln0_g), vec(ln0_b), ln1_g, ln1_b, ln2_g, ln2_b, hg_lb, hg_norm_g, fox_fb]
    small_m = [vec(m_ln0_g), vec(m_ln0_b), m_ln1_g, m_ln1_b, m_ln2_g, m_ln2_b, m_hg_lb, m_hg_norm_g, m_fox_fb]
    small_v = [vec(v_ln0_g), vec(v_ln0_b), v_ln1_g, v_ln1_b, v_ln2_g, v_ln2_b, v_hg_lb, v_hg_norm_g, v_fox_fb]
    small_out = adamw_small(small, probs[0:1], small_w, small_m, small_v, name="adamw_small")
    small_shapes = [ln0_g.shape, ln0_b.shape, ln1_g.shape, ln1_b.shape, ln2_g.shape, ln2_b.shape, hg_lb.shape,
                    hg_norm_g.shape, fox_fb.shape]
    sg_out, sd_out, sm_out, sv_out = [[a.reshape(shp) for a, shp in zip(small_out[9 * k:9 * k + 9], small_shapes)]
                                      for k in range(4)]

    big_out = {}
    for nm in names:
        delta, m2, v2 = adamw(big[nm], g_shards[nm], big_m[nm], big_v[nm], name="adamw_" + nm)
        big_out[nm] = (g_shards[nm][None], delta[None], m2[None], v2[None])

    def ordered(k):
        sm_ = [sg_out, sd_out, sm_out, sv_out][k]
        bg = lambda nm: big_out[nm][k]
        return [sm_[0], sm_[1], bg("w_in"), sm_[6], sm_[7], sm_[8], bg("w_a"), bg("w_b"), bg("w_o"), sm_[2], sm_[3],
                bg("w_ff1"), bg("w_ff2"), bg("w_pg"), bg("w_p"), sm_[4], sm_[5]]
    grad_x = dx.reshape(n_seq, seq, d)
    return (loss, grad_x, *ordered(0), *ordered(1), *ordered(2), *ordered(3))
```

```python
import functools
from typing import NamedTuple, Optional

import numpy as np
import jax
import jax.numpy as jnp
from jax import lax
from jax.experimental import pallas as pl
from jax.experimental.pallas import tpu as pltpu

F32 = jnp.float32
BF16 = jnp.bfloat16
MESH = pl.DeviceIdType.MESH

VMEM_LIMIT_BYTES = 48 * 1024 * 1024
LANES = 128
HG_HEADS = 4
HG_DIM = 128
HG_BLK = 16
HG_TILE = 256
HG_SLOTS = 8
FOX_HDIM = 64
FOX_AUG = 128
FOX_TQ = 1024
FOX_FWD_HEADS = 1
LN_EPS = 1e-5
RMS_EPS = 1e-6
DEPTH = 1
ALPHA = (2.0 * DEPTH) ** 0.25
ADAM_LR, ADAM_B1, ADAM_B2, ADAM_EPS, ADAM_WD, ADAM_STEP = 0.001, 0.9, 0.999, 1e-08, 0.01, 10
NEG_INF = -1e30


def _cparams(sem):
    return pltpu.CompilerParams(dimension_semantics=sem, vmem_limit_bytes=VMEM_LIMIT_BYTES)


def _tile(n, cap):
    if n <= cap:
        return n
    best = None
    for t in range(LANES, cap + 1, LANES):
        if n % t == 0:
            best = t
    assert best is not None, (n, cap)
    return best


class WView(NamedTuple):
    arr: jax.Array
    r0: int
    c0: int
    k: int
    n: int
    split: Optional[int]


def matmul_nn(a, w, *, name, transpose_rhs=False, out_dtype=F32, epilogue=None, aux=None, tm=1024, rider=None,
              post=None, post_aux=(), post_vecs=(), post_outs=(), post_accs=()):
    wv = w if isinstance(w, WView) else WView(w[None], 0, 0, w.shape[0], w.shape[1], None)
    rows_s = wv.k // 4 if wv.split == 0 else wv.k
    cols_s = wv.n // 4 if wv.split == 1 else wv.n
    tr, tc = _tile(rows_s, 1152), _tile(cols_s, 1152)
    assert wv.r0 % tr == 0 and wv.c0 % tc == 0
    T, K = a.shape
    N, tn, tk = (wv.k, tr, tc) if transpose_rhs else (wv.n, tc, tr)
    assert K == (wv.n if transpose_rhs else wv.k)
    tm = min(tm, T)
    assert T % tm == 0
    nk = K // tk

    def w_block(ri, ci):
        if wv.split == 0:
            return (ri * tr) // rows_s, (wv.r0 + (ri * tr) % rows_s) // tr, wv.c0 // tc + ci
        if wv.split == 1:
            return (ci * tc) // cols_s, wv.r0 // tr + ri, (wv.c0 + (ci * tc) % cols_s) // tc
        return 0, wv.r0 // tr + ri, wv.c0 // tc + ci

    fused = post is not None
    assert not fused or N == tn
    aux_list = list(post_aux) if fused else ([aux] if aux is not None else [])
    aux_list = [x if isinstance(x, tuple) else (x, 0, x.shape[1]) for x in aux_list]
    vec_list = list(post_vecs)
    out_dtypes = list(post_outs) if fused else [out_dtype]
    n_aux, n_vec, n_out, n_acc = len(aux_list), len(vec_list), len(out_dtypes), len(post_accs)

    def body(*refs):
        a_ref, w_ref = refs[:2]
        aux_refs = refs[2:2 + n_aux]
        vec_refs = refs[2 + n_aux:2 + n_aux + n_vec]
        out_refs = refs[2 + n_aux + n_vec:2 + n_aux + n_vec + n_out]
        sum_refs = refs[2 + n_aux + n_vec + n_out:2 + n_aux + n_vec + n_out + n_acc]
        acc_ref = refs[-1]
        m, k = pl.program_id(1), pl.program_id(2)
        if transpose_rhs:
            part = lax.dot_general(a_ref[...], w_ref[...], (((1,), (1,)), ((), ())), preferred_element_type=F32)
        else:
            part = jnp.dot(a_ref[...], w_ref[...], preferred_element_type=F32)

        def write(res):
            if not fused:
                if epilogue is not None:
                    res = epilogue(res) if not aux_refs else epilogue(res, aux_refs[0][...])
                out_refs[0][...] = res.astype(out_dtype)
                return
            outs, sums = post(res, [r[...] for r in aux_refs], [v[...] for v in vec_refs])
            assert len(outs) == n_out and len(sums) == n_acc
            for r, val in zip(out_refs, outs):
                r[...] = val.astype(r.dtype)
            for r, val in zip(sum_refs, sums):
                def first_rows(r=r, val=val):
                    r[...] = val

                def later_rows(r=r, val=val):
                    r[...] += val
                pl.when(m == 0)(first_rows)
                pl.when(m > 0)(later_rows)

        if nk == 1:
            write(part)
        else:
            @pl.when(k == 0)
            def _():
                acc_ref[...] = part

            @pl.when(k > 0)
            def _():
                acc_ref[...] += part

            @pl.when(k == nk - 1)
            def _():
                write(acc_ref[...])

    w_index = (lambda n, m, k: w_block(n, k)) if transpose_rhs else (lambda n, m, k: w_block(k, n))
    in_specs = [pl.BlockSpec((tm, tk), lambda n, m, k: (m, k)),
                pl.BlockSpec((None, tr, tc), w_index)]
    args = [a, wv.arr]
    for arr, off, width in aux_list:
        assert width == N and off % tn == 0
        in_specs.append(pl.BlockSpec((tm, tn), functools.partial(lambda n, m, k, blk: (m, blk + n), blk=off // tn)))
        args.append(arr)
    for v in vec_list:
        in_specs.append(pl.BlockSpec(v.shape, lambda n, m, k: (0, 0)))
        args.append(v)
    out_specs = [pl.BlockSpec((tm, tn), lambda n, m, k: (m, n)) for _ in out_dtypes]
    out_specs += [pl.BlockSpec((1, tn), lambda n, m, k: (0, 0)) for _ in post_accs]
    out_shape = [jax.ShapeDtypeStruct((T, N), dt) for dt in out_dtypes]
    out_shape += [jax.ShapeDtypeStruct((1, N), F32) for _ in post_accs]
    scratch = [pltpu.VMEM((tm, tn) if nk > 1 else (8, LANES), F32)]
    grid = (N // tn, T // tm, nk)
    sem = ("arbitrary",) * 3 if (n_acc or rider is not None) else ("parallel", "parallel", "arbitrary")
    params = pltpu.CompilerParams(dimension_semantics=sem, vmem_limit_bytes=VMEM_LIMIT_BYTES,
                                  has_side_effects=rider is not None)
    if rider is not None:
        r_in, r_out, r_sems = rider.specs()
        body = rider.wrap(body, len(in_specs), len(out_specs), 3)
        in_specs, out_specs, out_shape = in_specs + r_in, out_specs + r_out, out_shape + rider.out_shape
        scratch, args = scratch + r_sems, args + list(rider.ins)
    res = pl.pallas_call(body, name=name, grid=grid, in_specs=in_specs, out_specs=out_specs, out_shape=out_shape,
                         scratch_shapes=scratch, compiler_params=params)(*args)
    main = (list(res[:n_out]), list(res[n_out:n_out + n_acc])) if fused else res[0]
    return main if rider is None else (main, list(res[n_out + n_acc:]))


def matmul_tn(a, b, *, name, tk=1024, rider=None, into=None):
    T, M = a.shape
    T2, N = b.shape
    tk = min(tk, T)
    assert T == T2 and T % tk == 0
    if into is not None:
        assert rider is None
        buf, r0, c0, split = into
        rows_s, cols_s = (M // 4, N) if split == 0 else (M, N // 4)
        tm, tn = _tile(rows_s, 1024), _tile(cols_s, 1152)
        assert r0 % tm == 0 and c0 % tn == 0

        def part_block(m, n, k):
            if split == 0:
                return (m * tm) // rows_s, (r0 + (m * tm) % rows_s) // tm, c0 // tn + n
            return (n * tn) // cols_s, r0 // tm + m, (c0 + (n * tn) % cols_s) // tn

        def body_into(a_ref, b_ref, buf_ref, o_ref):
            k = pl.program_id(2)
            part = lax.dot_general(a_ref[...], b_ref[...], (((0,), (0,)), ((), ())), preferred_element_type=F32)

            @pl.when(k == 0)
            def _():
                o_ref[...] = part

            @pl.when(k > 0)
            def _():
                o_ref[...] += part

        return pl.pallas_call(
            body_into, name=name, grid=(M // tm, N // tn, T // tk),
            in_specs=[pl.BlockSpec((tk, tm), lambda m, n, k: (k, m)), pl.BlockSpec((tk, tn), lambda m, n, k: (k, n)),
                      pl.BlockSpec(memory_space=pl.ANY)],
            out_specs=pl.BlockSpec((None, tm, tn), part_block),
            out_shape=jax.ShapeDtypeStruct(buf.shape, buf.dtype), input_output_aliases={2: 0},
            compiler_params=_cparams(("parallel", "parallel", "arbitrary")))(a, b, buf)
    tm = _tile(M, 1024)
    tn = _tile(N, 1152)

    def body(a_ref, b_ref, o_ref):
        k = pl.program_id(2)
        part = lax.dot_general(a_ref[...], b_ref[...], (((0,), (0,)), ((), ())), preferred_element_type=F32)

        @pl.when(k == 0)
        def _():
            o_ref[...] = part

        @pl.when(k > 0)
        def _():
            o_ref[...] += part

    in_specs = [pl.BlockSpec((tk, tm), lambda m, n, k: (k, m)), pl.BlockSpec((tk, tn), lambda m, n, k: (k, n))]
    out_specs = [pl.BlockSpec((tm, tn), lambda m, n, k: (m, n))]
    out_shape = [jax.ShapeDtypeStruct((M, N), F32)]
    grid = (M // tm, N // tn, T // tk)
    if rider is None:
        return pl.pallas_call(body, name=name, grid=grid, in_specs=in_specs, out_specs=out_specs, out_shape=out_shape,
                              compiler_params=_cparams(("parallel", "parallel", "arbitrary")))(a, b)[0]
    r_in, r_out, r_sems = rider.specs()
    res = pl.pallas_call(
        rider.wrap(body, 2, 1, 3), name=name, grid=grid, in_specs=in_specs + r_in, out_specs=out_specs + r_out,
        out_shape=out_shape + rider.out_shape, scratch_shapes=r_sems,
        compiler_params=pltpu.CompilerParams(dimension_semantics=("arbitrary",) * 3,
                                             vmem_limit_bytes=VMEM_LIMIT_BYTES, has_side_effects=True),
    )(a, b, *rider.ins)
    return res[0], list(res[1:])


def rowwise(fn, rows, vecs, outs, accs=(), *, name, tm=512, rider=None):
    rows = [r if isinstance(r, tuple) else (r, 0, r.shape[1]) for r in rows]
    T = rows[0][0].shape[0]
    tm = min(tm, T)
    assert T % tm == 0
    n_rows, n_vecs, n_outs, n_accs = len(rows), len(vecs), len(outs), len(accs)

    def body(*refs):
        row_refs = refs[:n_rows]
        vec_refs = refs[n_rows:n_rows + n_vecs]
        out_refs = refs[n_rows + n_vecs:n_rows + n_vecs + n_outs]
        acc_refs = refs[n_rows + n_vecs + n_outs:]
        out_vals, acc_vals = fn([r[...] for r in row_refs], [v[...] for v in vec_refs])
        assert len(out_vals) == n_outs and len(acc_vals) == n_accs
        for r, val in zip(out_refs, out_vals):
            r[...] = val.astype(r.dtype)
        if n_accs:
            i = pl.program_id(0)

            @pl.when(i == 0)
            def _():
                for r in acc_refs:
                    r[...] = jnp.zeros_like(r)

            for r, val in zip(acc_refs, acc_vals):
                r[...] += val

    in_specs = []
    for arr, off, width in rows:
        assert off % width == 0
        in_specs.append(pl.BlockSpec((tm, width), functools.partial(lambda i, blk: (i, blk), blk=off // width)))
    for v in vecs:
        in_specs.append(pl.BlockSpec(v.shape, lambda i: (0, 0)))
    out_specs = [pl.BlockSpec((tm, w), lambda i: (i, 0)) for w, _ in outs]
    out_specs += [pl.BlockSpec((1, w), lambda i: (0, 0)) for w in accs]
    out_shape = [jax.ShapeDtypeStruct((T, w), dt) for w, dt in outs]
    out_shape += [jax.ShapeDtypeStruct((1, w), F32) for w in accs]
    args = [r[0] for r in rows] + list(vecs)
    if rider is None:
        res = pl.pallas_call(body, name=name, grid=(T // tm,), in_specs=in_specs, out_specs=out_specs,
                             out_shape=out_shape,
                             compiler_params=_cparams(("arbitrary",) if n_accs else ("parallel",)))(*args)
        return res[:n_outs], res[n_outs:]
    r_in, r_out, r_sems = rider.specs()
    res = pl.pallas_call(
        rider.wrap(body, len(in_specs), len(out_specs), 1), name=name, grid=(T // tm,), in_specs=in_specs + r_in,
        out_specs=out_specs + r_out, out_shape=out_shape + rider.out_shape, scratch_shapes=r_sems,
        compiler_params=pltpu.CompilerParams(dimension_semantics=("arbitrary",), vmem_limit_bytes=VMEM_LIMIT_BYTES,
                                             has_side_effects=True),
    )(*args, *rider.ins)
    return res[:n_outs], res[n_outs:n_outs + n_accs], list(res[n_outs + n_accs:])


def _colsum(x):
    return jnp.sum(x, axis=0, keepdims=True)


def _sigmoid(x):
    return 1.0 / (1.0 + jnp.exp(-x))


def _ln_stats(z):
    mu = jnp.mean(z, axis=-1, keepdims=True)
    zc = z - mu
    var = jnp.mean(zc * zc, axis=-1, keepdims=True)
    return zc * lax.rsqrt(var + LN_EPS)


def _ln_bwd(zhat_src, dy, g):
    mu = jnp.mean(zhat_src, axis=-1, keepdims=True)
    zc = zhat_src - mu
    var = jnp.mean(zc * zc, axis=-1, keepdims=True)
    rstd = lax.rsqrt(var + LN_EPS)
    zh = zc * rstd
    dzh = dy * g
    dz = rstd * (dzh - jnp.mean(dzh, axis=-1, keepdims=True) - zh * jnp.mean(dzh * zh, axis=-1, keepdims=True))
    return dz, _colsum(dy * zh), _colsum(dy)


def _hg_constants():
    r = np.arange(HG_TILE)
    same = (r[:, None] // HG_BLK) == (r[None, :] // HG_BLK)
    lower = (same & (r[None, :] <= r[:, None])).astype(np.float32)
    upper = (same & (r[None, :] >= r[:, None])).astype(np.float32)
    total = same.astype(np.float32)
    c = np.arange(2 * HG_DIM)
    bd = ((c[:, None] // HG_DIM) == (c[None, :] // HG_DIM)).astype(np.float32)
    pair_t = np.array([t for t, _ in _HG_PAIRS])
    pair_s = np.array([s for _, s in _HG_PAIRS])
    sel_t = (pair_t[None, :] == np.arange(HG_BLK)[:, None]).astype(np.float32)
    sel_s = (pair_s[None, :] == np.arange(HG_BLK)[:, None]).astype(np.float32)
    as_bf = lambda m: jnp.asarray(m, dtype=BF16)
    return as_bf(lower), as_bf(upper), as_bf(total), as_bf(bd), as_bf(sel_t), as_bf(sel_s)


_HG_HALF = HG_BLK // 2
_HG_PAIRS = ([(t, s) for t in range(_HG_HALF, HG_BLK) for s in range(HG_BLK)]
             + [(t, s) for t in range(_HG_HALF) for s in range(_HG_HALF)])
HG_STACK = len(_HG_PAIRS)
_HG_SLABS = ([((t - _HG_HALF) * HG_BLK, (t,), HG_BLK) for t in range(_HG_HALF, HG_BLK)]
             + [(_HG_HALF * HG_BLK + t * _HG_HALF, (t, t + 1), _HG_HALF) for t in range(0, _HG_HALF, 2)])


def _stack_by_s(x):
    return jnp.concatenate([x] * _HG_HALF + [x[:_HG_HALF]] * _HG_HALF, axis=0)


def _stack_by_t(x):
    w = x.shape[1]
    return jnp.concatenate([jnp.broadcast_to(x[t:t + 1], (HG_BLK, w)) for t in range(_HG_HALF, HG_BLK)]
                           + [jnp.broadcast_to(x[t:t + 1], (_HG_HALF, w)) for t in range(_HG_HALF)], axis=0)


def _keep_bf16_bits(x):
    bits = lax.bitcast_convert_type(x, jnp.int32) & jnp.int32(-65536)
    return lax.bitcast_convert_type(bits, F32)


def _head_sums(stack_ref, slot, bd):
    pair = bd.shape[0]
    return jnp.concatenate([jnp.dot(stack_ref[slot, :, c0:c0 + pair], bd, preferred_element_type=F32)
                            for c0 in range(0, stack_ref.shape[2], pair)], axis=1)


def _split3(x):
    hi = _keep_bf16_bits(x)
    r1 = x - hi
    mid = _keep_bf16_bits(r1)
    lo = _keep_bf16_bits(r1 - mid)
    return hi.astype(BF16), mid.astype(BF16), lo.astype(BF16)


def _dot3(m01, x):
    hi, mid, lo = _split3(x)
    d = lambda p: jnp.dot(m01, p, preferred_element_type=F32)
    return (d(lo) + d(mid)) + d(hi)


def _hg_prologue(hq, hf, lb, lower, total):
    sq = _sigmoid(hq)
    q = hq * sq
    sg = _sigmoid(hf)
    f = lb + (1.0 - lb) * sg
    g = jnp.log(f)
    k = 1.0 - f
    b = _dot3(lower, g)
    bl = _dot3(total, g)
    return q, k, f, sg, sq, b, bl


def _stack16(fn):
    return [fn(t) for t in range(HG_BLK)]


def hgrn2_fwd(proj, offs, lb, n_seq, seq, *, name, rider=None):
    T = n_seq * seq
    W = HG_HEADS * HG_DIM
    n_tiles = seq // HG_TILE
    nb = HG_TILE // HG_BLK
    lower, _, total, bd, sel_t, _ = _hg_constants()

    def body(hq_ref, hf_ref, hi_ref, lb_ref, lower_ref, total_ref, bd_ref, selt_ref,
             o_ref, st_out_ref,
             st_ref, q_s, k_s, v_s, b_s, qt_s, kt_s, d_s, p_s):
        @pl.when(pl.program_id(1) == 0)
        def _():
            st_ref[...] = jnp.zeros_like(st_ref)

        q, k, _, _, _, b, bl = _hg_prologue(hq_ref[...], hf_ref[...], lb_ref[...], lower_ref[...], total_ref[...])
        q_s[...] = q
        k_s[...] = k
        v_s[...] = hi_ref[...]
        b_s[...] = b
        qt_s[...] = q * jnp.exp(b)
        kt_s[...] = k * jnp.exp(jnp.minimum(bl - b, 0.0))
        d_s[...] = jnp.exp(bl)
        rowi = lax.broadcasted_iota(jnp.int32, (HG_BLK, W), 0)

        def block(i, slot):
            r0 = pl.multiple_of(i * HG_BLK, HG_BLK)
            rows = pl.ds(r0, HG_BLK)
            qi, ki, vi, bi = q_s[rows, :], k_s[rows, :], v_s[rows, :], b_s[rows, :]
            for off, ts, n in _HG_SLABS:
                slab = [jnp.where(rowi[:n] <= t, jnp.exp(jnp.minimum(bi[t:t + 1, :] - bi[:n], 0.0)), 0.0)
                        * qi[t:t + 1, :] * ki[:n] for t in ts]
                p_s[slot, pl.ds(off, HG_BLK), :] = jnp.concatenate(slab, axis=0).astype(BF16)
            a_b = _head_sums(p_s, slot, bd_ref[...])
            o_blk = jnp.dot(selt_ref[...], (a_b * _stack_by_s(vi)).astype(BF16), preferred_element_type=F32)
            qti, kti, di = qt_s[rows, :], kt_s[rows, :], d_s[rows, :]
            outs = []
            for h in range(HG_HEADS):
                hs = slice(h * HG_DIM, (h + 1) * HG_DIM)
                st_h = st_ref[hs, :]
                st_out_ref[i, hs, :] = st_h
                outs.append(lax.dot_general(qti[:, hs].astype(BF16), st_h.astype(BF16),
                                            (((1,), (1,)), ((), ())), preferred_element_type=F32))
                upd = lax.dot_general(vi[:, hs].astype(BF16), kti[:, hs].astype(BF16),
                                      (((0,), (0,)), ((), ())), preferred_element_type=F32)
                st_ref[hs, :] = st_h * di[0:1, hs] + upd
            o_ref[rows, :] = o_blk + jnp.concatenate(outs, axis=1)

        def some_blocks(jj, carry):
            for slot in range(HG_SLOTS):
                block(HG_SLOTS * jj + slot, slot)
            return carry

        lax.fori_loop(0, nb // HG_SLOTS, some_blocks, 0)

    col = lambda off: functools.partial(lambda s, t, blk: (s * n_tiles + t, blk), blk=off // W)
    const = lambda m: pl.BlockSpec(m.shape, lambda s, t: (0, 0))
    tile_f32 = pltpu.VMEM((HG_TILE, W), F32)
    in_specs = [pl.BlockSpec((HG_TILE, W), col(offs[0])), pl.BlockSpec((HG_TILE, W), col(offs[1])),
                pl.BlockSpec((HG_TILE, W), col(offs[2])), const(lb), const(lower), const(total), const(bd),
                const(sel_t)]
    out_specs = [pl.BlockSpec((HG_TILE, W), lambda s, t: (s * n_tiles + t, 0)),
                 pl.BlockSpec((nb, W, HG_DIM), lambda s, t: (s * n_tiles + t, 0, 0))]
    out_shape = [jax.ShapeDtypeStruct((T, W), F32), jax.ShapeDtypeStruct((T // HG_BLK, W, HG_DIM), F32)]
    scratch = [pltpu.VMEM((W, HG_DIM), F32)] + [tile_f32] * 7 + [pltpu.VMEM((HG_SLOTS, HG_STACK, W), BF16)]
    args = [proj, proj, proj, lb, lower, total, bd, sel_t]
    params = _cparams(("arbitrary", "arbitrary"))
    if rider is not None:
        r_in, r_out, r_sems = rider.specs()
        body = rider.wrap(body, len(in_specs), len(out_specs), 2)
        in_specs, out_specs, out_shape = in_specs + r_in, out_specs + r_out, out_shape + rider.out_shape
        scratch, args = scratch + r_sems, args + rider.ins
        params = pltpu.CompilerParams(dimension_semantics=("arbitrary", "arbitrary"),
                                      vmem_limit_bytes=VMEM_LIMIT_BYTES, has_side_effects=True)
    res = pl.pallas_call(body, name=name, grid=(n_seq, n_tiles), in_specs=in_specs, out_specs=out_specs,
                         out_shape=out_shape, scratch_shapes=scratch, compiler_params=params)(*args)
    return res[0], res[1], list(res[2:])


def hgrn2_bwd(proj, offs, lb, do, states, n_seq, seq, *, name):
    T = n_seq * seq
    W = HG_HEADS * HG_DIM
    n_tiles = seq // HG_TILE
    nb = HG_TILE // HG_BLK
    lower, upper, total, bd, sel_t, sel_s = _hg_constants()

    def body(hq_ref, hf_ref, hi_ref, do_ref, st_in_ref, lb_ref, lower_ref, upper_ref, total_ref, bd_ref,
             selt_ref, sels_ref,
             dhq_ref, dhf_ref, dhi_ref, dlb_ref,
             dst_ref, q_s, k_s, v_s, b_s, qt_s, kt_s, d_s, eb_s, ekb_s, dq_s, dk_s, db_s, dv_s,
             p_s, e_s, w_s):
        first = jnp.logical_and(pl.program_id(0) == 0, pl.program_id(1) == 0)

        @pl.when(first)
        def _():
            dlb_ref[...] = jnp.zeros_like(dlb_ref)

        @pl.when(pl.program_id(1) == 0)
        def _():
            dst_ref[...] = jnp.zeros_like(dst_ref)

        hq, lbv = hq_ref[...], lb_ref[...]
        q, k, f, sg, sq, b, bl = _hg_prologue(hq, hf_ref[...], lbv, lower_ref[...], total_ref[...])
        eb = jnp.exp(b)
        ekb = jnp.exp(jnp.minimum(bl - b, 0.0))
        q_s[...] = q
        k_s[...] = k
        v_s[...] = hi_ref[...]
        b_s[...] = b
        eb_s[...] = eb
        ekb_s[...] = ekb
        qt_s[...] = q * eb
        kt_s[...] = k * ekb
        d_s[...] = jnp.exp(bl)
        rowi = lax.broadcasted_iota(jnp.int32, (HG_BLK, W), 0)
        last_row = rowi == HG_BLK - 1

        def block(i, slot):
            r0 = pl.multiple_of(i * HG_BLK, HG_BLK)
            rows = pl.ds(r0, HG_BLK)
            qi, ki, vi, bi, doi = q_s[rows, :], k_s[rows, :], v_s[rows, :], b_s[rows, :], do_ref[rows, :]
            for off, ts, n in _HG_SLABS:
                es = [jnp.where(rowi[:n] <= t, jnp.exp(jnp.minimum(bi[t:t + 1, :] - bi[:n], 0.0)), 0.0) for t in ts]
                sl = pl.ds(off, HG_BLK)
                e_s[slot, sl, :] = jnp.concatenate(es, axis=0)
                p_s[slot, sl, :] = jnp.concatenate([e * qi[t:t + 1, :] * ki[:n] for e, t in zip(es, ts)],
                                                   axis=0).astype(BF16)
                w_s[slot, sl, :] = jnp.concatenate([doi[t:t + 1, :] * vi[:n] for t in ts], axis=0).astype(BF16)
            a_b = _head_sums(p_s, slot, bd_ref[...])
            da_b = _head_sums(w_s, slot, bd_ref[...])
            x = da_b * e_s[slot]
            dq_in = jnp.dot(selt_ref[...], (x * _stack_by_s(ki)).astype(BF16), preferred_element_type=F32)
            dk_in = jnp.dot(sels_ref[...], (x * _stack_by_t(qi)).astype(BF16), preferred_element_type=F32)
            dv_in = jnp.dot(sels_ref[...], (a_b * _stack_by_t(doi)).astype(BF16), preferred_element_type=F32)
            qti, kti, di = qt_s[rows, :], kt_s[rows, :], d_s[rows, :]
            dqt, dkt, dvt, dd = [], [], [], []
            for h in range(HG_HEADS):
                hs = slice(h * HG_DIM, (h + 1) * HG_DIM)
                st_h = st_in_ref[i, hs, :]
                dst_h = dst_ref[hs, :]
                do_h, v_h = doi[:, hs].astype(BF16), vi[:, hs].astype(BF16)
                dst_b = dst_h.astype(BF16)
                dqt.append(jnp.dot(do_h, st_h.astype(BF16), preferred_element_type=F32))
                dkt.append(jnp.dot(v_h, dst_b, preferred_element_type=F32))
                dvt.append(lax.dot_general(kti[:, hs].astype(BF16), dst_b, (((1,), (1,)), ((), ())),
                                           preferred_element_type=F32))
                dd.append(jnp.sum(dst_h * st_h, axis=0, keepdims=True))
                upd = lax.dot_general(do_h, qti[:, hs].astype(BF16), (((0,), (0,)), ((), ())),
                                      preferred_element_type=F32)
                dst_ref[hs, :] = dst_h * di[0:1, hs] + upd
            dqt = jnp.concatenate(dqt, axis=1)
            dkt = jnp.concatenate(dkt, axis=1)
            dvt = jnp.concatenate(dvt, axis=1)
            dd = jnp.concatenate(dd, axis=1)
            dbl = jnp.sum(dkt * kti, axis=0, keepdims=True) + dd * di[0:1, :]
            db = qi * dq_in - ki * dk_in + dqt * qti - dkt * kti
            db_s[rows, :] = db + jnp.where(last_row, dbl, 0.0)
            dq_s[rows, :] = dq_in + dqt * eb_s[rows, :]
            dk_s[rows, :] = dk_in + dkt * ekb_s[rows, :]
            dv_s[rows, :] = dv_in + dvt

        def some_blocks(jj, carry):
            for slot in range(HG_SLOTS):
                block(nb - 1 - slot - HG_SLOTS * jj, slot)
            return carry

        lax.fori_loop(0, nb // HG_SLOTS, some_blocks, 0)

        dg = _dot3(upper_ref[...], db_s[...])
        dhq_ref[...] = (dq_s[...] * (sq * (1.0 + hq * (1.0 - sq)))).astype(dhq_ref.dtype)
        df = dg / f - dk_s[...]
        dhf_ref[...] = (df * (1.0 - lbv) * (sg * (1.0 - sg))).astype(dhf_ref.dtype)
        dhi_ref[...] = dv_s[...].astype(dhi_ref.dtype)
        dlb_ref[...] += _colsum(df * (1.0 - sg))

    rev = lambda s, t: s * n_tiles + (n_tiles - 1 - t)
    col = lambda off: functools.partial(lambda s, t, blk: (rev(s, t), blk), blk=off // W)
    const = lambda m: pl.BlockSpec(m.shape, lambda s, t: (0, 0))
    row = pl.BlockSpec((HG_TILE, W), lambda s, t: (rev(s, t), 0))
    tile_f32 = pltpu.VMEM((HG_TILE, W), F32)
    n2 = HG_STACK
    return pl.pallas_call(
        body, name=name,
        grid=(n_seq, n_tiles),
        in_specs=[pl.BlockSpec((HG_TILE, W), col(offs[0])), pl.BlockSpec((HG_TILE, W), col(offs[1])),
                  pl.BlockSpec((HG_TILE, W), col(offs[2])), row,
                  pl.BlockSpec((nb, W, HG_DIM), lambda s, t: (rev(s, t), 0, 0)),
                  const(lb), const(lower), const(upper), const(total), const(bd), const(sel_t), const(sel_s)],
        out_specs=[row, row, row, pl.BlockSpec((1, W), lambda s, t: (0, 0))],
        out_shape=[jax.ShapeDtypeStruct((T, W), BF16)] * 3 + [jax.ShapeDtypeStruct((1, W), F32)],
        scratch_shapes=[pltpu.VMEM((W, HG_DIM), F32)] + [tile_f32] * 13
                       + [pltpu.VMEM((HG_SLOTS, n2, W), BF16), pltpu.VMEM((HG_SLOTS, n2, W), F32),
                          pltpu.VMEM((HG_SLOTS, n2, W), BF16)],
        compiler_params=_cparams(("arbitrary", "arbitrary")),
    )(proj, proj, proj, do, states, lb, lower, upper, total, bd, sel_t, sel_s)


def _diag_mask(tq):
    return lax.broadcasted_iota(jnp.int32, (tq, tq), 1) <= lax.broadcasted_iota(jnp.int32, (tq, tq), 0)


def _qk(q, k):
    return lax.dot_general(q, k, (((1,), (1,)), ((), ())), preferred_element_type=F32)


def _causal_pairs(n, sweeps=1, by_key=False):
    if by_key:
        rows = [(i, j, 0) for j in range(n) for i in range(j, n)]
    else:
        rows = [(i, j, s) for i in range(n) for s in range(sweeps) for j in range(i + 1)]
    return tuple(jnp.asarray(np.array([r[c] for r in rows], np.int32)) for c in range(3))


def _fox_placement(fh):
    hw, wa = fh * FOX_HDIM, fh * FOX_AUG
    pq, pk = np.zeros((hw, wa), np.float32), np.zeros((hw, wa), np.float32)
    aq, ak = np.zeros((3 * LANES, wa), np.float32), np.zeros((3 * LANES, wa), np.float32)
    oq, ok = np.zeros((1, wa), np.float32), np.zeros((1, wa), np.float32)
    for h in range(fh):
        src, dst = np.arange(h * FOX_HDIM, (h + 1) * FOX_HDIM), np.arange(h * FOX_AUG, h * FOX_AUG + FOX_HDIM)
        pq[src, dst] = FOX_HDIM ** -0.5
        pk[src, dst] = 1.0
        gate = h * FOX_AUG + FOX_HDIM
        for r in range(3):
            aq[r * LANES + h, gate + r] = 1.0
            ak[r * LANES + h, gate + 3 + r] = -1.0
        oq[0, gate + 3:gate + 6] = 1.0
        ok[0, gate:gate + 3] = 1.0
    bf = lambda m: jnp.asarray(m, dtype=BF16)
    return {"pq": bf(pq), "pk": bf(pk), "aq": bf(aq), "ak": bf(ak), "oq": jnp.asarray(oq), "ok": jnp.asarray(ok),
            "pqt": bf(pq.T), "pkt": bf(pk.T)}


def _fox_specs(tq, fh, heads=1):
    groups = fh // heads

    def spec(tab):
        return pl.BlockSpec((None, tq, heads * FOX_AUG), lambda b, t, *tabs: (b // groups, tabs[tab][t], b % groups))
    return spec(0), spec(1)


def fox_fwd(qa, ka, va, *, name):
    n_seq, S, width = qa.shape
    fh = width // FOX_AUG
    nh = FOX_FWD_HEADS
    BH = n_seq * fh // nh
    tq = min(FOX_TQ, S)
    itab, jtab, _ = _causal_pairs(S // tq)

    def body(itab_ref, jtab_ref, q_ref, k_ref, v_ref, o_ref, ox_ref, lse_ref, *scratch):
        t = pl.program_id(1)
        i, j = itab_ref[t], jtab_ref[t]
        per_head = [scratch[4 * h:4 * h + 4] for h in range(nh)]

        @pl.when(j == 0)
        def _():
            for m_s, l_s, acc_s, acc_lo_s in per_head:
                m_s[...] = jnp.full_like(m_s, NEG_INF)
                l_s[...] = jnp.zeros_like(l_s)
                acc_s[...] = jnp.zeros_like(acc_s)
                acc_lo_s[...] = jnp.zeros_like(acc_lo_s)

        def step(on_diagonal):
            for h, (m_s, l_s, acc_s, acc_lo_s) in enumerate(per_head):
                lanes = slice(h * FOX_AUG, (h + 1) * FOX_AUG)
                s = _qk(q_ref[:, lanes], k_ref[:, lanes])
                if on_diagonal:
                    s = jnp.where(_diag_mask(tq), s, NEG_INF)
                m_prev = m_s[...]
                m_new = jnp.maximum(m_prev, jnp.max(s, axis=-1, keepdims=True))
                alpha = jnp.exp(m_prev - m_new)
                p = jnp.exp(s - m_new[:, 0:1])
                p_hi = p.astype(BF16)
                p_lo = (p - p_hi.astype(F32)).astype(BF16)
                v = v_ref[:, lanes]
                l_s[...] = alpha * l_s[...] + jnp.sum(p, axis=-1, keepdims=True)
                acc_s[...] = alpha * acc_s[...] + jnp.dot(p_hi, v, preferred_element_type=F32)
                acc_lo_s[...] = alpha * acc_lo_s[...] + jnp.dot(p_lo, v, preferred_element_type=F32)
                m_s[...] = m_new

        @pl.when(j < i)
        def _():
            step(False)

        @pl.when(j == i)
        def _():
            step(True)
            for h, (m_s, l_s, acc_s, acc_lo_s) in enumerate(per_head):
                lanes = slice(h * FOX_AUG, (h + 1) * FOX_AUG)
                inv_l = 1.0 / l_s[...]
                o_ref[:, lanes] = (acc_s[...] * inv_l).astype(o_ref.dtype)
                ox_ref[:, lanes] = (acc_s[...] + acc_lo_s[...]) * inv_l
                lse_ref[:, lanes] = m_s[...] + jnp.log(l_s[...])

    qspec, kspec = _fox_specs(tq, fh, nh)
    wide = jax.ShapeDtypeStruct((n_seq, S, width), F32)
    return pl.pallas_call(
        body, name=name,
        grid_spec=pltpu.PrefetchScalarGridSpec(
            num_scalar_prefetch=2, grid=(BH, itab.shape[0]),
            in_specs=[qspec, kspec, kspec],
            out_specs=[qspec, qspec, qspec],
            scratch_shapes=[pltpu.VMEM((tq, LANES), F32)] * (4 * nh)),
        out_shape=[jax.ShapeDtypeStruct((n_seq, S, width), BF16), wide, wide],
        compiler_params=_cparams(("parallel", "arbitrary")),
    )(itab, jtab, qa, ka, va)


def _fox_ds(q, k, v, do, ox, lse, on_diagonal):
    s = _qk(q, k)
    if on_diagonal:
        s = jnp.where(_diag_mask(s.shape[0]), s, NEG_INF)
    p = jnp.exp(s - lse[:, 0:1])
    delta = jnp.sum(do.astype(F32) * ox, axis=-1, keepdims=True)
    return p, p * (_qk(do, v) - delta)


def fox_bwd(qa, ka, va, do, ox, lse, *, name):
    n_seq, S, width = qa.shape
    fh = width // FOX_AUG
    BH = n_seq * fh
    tq = min(FOX_TQ, S)
    itab, jtab, _ = _causal_pairs(S // tq)

    n_steps = itab.shape[0]

    def body(itab_ref, jtab_ref, q_ref, k_ref, v_ref, do_ref, ox_ref, lse_ref, dq_ref, dk_ref, dv_ref, dsum_ref,
             dq_s, dk_s, dv_s):
        t = pl.program_id(1)
        i, j = itab_ref[t], jtab_ref[t]

        @pl.when(t == 0)
        def _():
            dq_s[...] = jnp.zeros_like(dq_s)
            dk_s[...] = jnp.zeros_like(dk_s)
            dv_s[...] = jnp.zeros_like(dv_s)
            dsum_ref[...] = jnp.zeros_like(dsum_ref)

        q_rows = pl.ds(pl.multiple_of(i * tq, tq), tq)
        k_rows = pl.ds(pl.multiple_of(j * tq, tq), tq)

        def step(on_diagonal):
            q, k, do = q_ref[...], k_ref[...], do_ref[...]
            p, ds = _fox_ds(q, k, v_ref[...], do, ox_ref[...], lse_ref[...], on_diagonal)
            ds_b = ds.astype(BF16)
            tn = (((0,), (0,)), ((), ()))
            dq_s[q_rows, :] += jnp.dot(ds_b, k, preferred_element_type=F32)
            dk_s[k_rows, :] += lax.dot_general(ds_b, q, tn, preferred_element_type=F32)
            dv_s[k_rows, :] += lax.dot_general(p.astype(BF16), do, tn, preferred_element_type=F32)
            dsum_ref[:, k_rows] += _colsum(ds)

        @pl.when(j < i)
        def _():
            step(False)

        @pl.when(j == i)
        def _():
            step(True)

        @pl.when(t == n_steps - 1)
        def _():
            dq_ref[...] = dq_s[...].astype(dq_ref.dtype)
            dk_ref[...] = dk_s[...].astype(dk_ref.dtype)
            dv_ref[...] = dv_s[...].astype(dv_ref.dtype)

    qspec, kspec = _fox_specs(tq, fh)
    whole = pl.BlockSpec((None, S, FOX_AUG), lambda b, t, it, jt: (b // fh, 0, b % fh))
    wide = jax.ShapeDtypeStruct((n_seq, S, width), BF16)
    return pl.pallas_call(
        body, name=name,
        grid_spec=pltpu.PrefetchScalarGridSpec(
            num_scalar_prefetch=2, grid=(BH, n_steps),
            in_specs=[qspec, kspec, kspec, qspec, qspec, qspec],
            out_specs=[whole, whole, whole, pl.BlockSpec((None, 1, S), lambda b, t, it, jt: (b, 0, 0))],
            scratch_shapes=[pltpu.VMEM((S, FOX_AUG), F32)] * 3),
        out_shape=[wide, wide, wide, jax.ShapeDtypeStruct((BH, 1, S), F32)],
        compiler_params=_cparams(("parallel", "arbitrary")),
    )(itab, jtab, qa, ka, va, do, ox, lse)


def seq_cumsum(x, n_seq, seq, *, reverse, name):
    T, C = x.shape
    tb = min(256, seq)
    n = seq // tb
    r = np.arange(tb)
    tri = (r[None, :] >= r[:, None]) if reverse else (r[None, :] <= r[:, None])
    tri = jnp.asarray(tri.astype(np.float32), dtype=BF16)

    def body(x_ref, tri_ref, o_ref, carry_s):
        @pl.when(pl.program_id(1) == 0)
        def _():
            carry_s[...] = jnp.zeros_like(carry_s)

        xv = x_ref[...]
        o_ref[...] = _dot3(tri_ref[...], xv) + carry_s[...]
        carry_s[...] += _colsum(xv)

    blk = (lambda s, t: (s * n + (n - 1 - t), 0)) if reverse else (lambda s, t: (s * n + t, 0))
    return pl.pallas_call(
        body, name=name,
        grid=(n_seq, n),
        in_specs=[pl.BlockSpec((tb, C), blk), pl.BlockSpec((tb, tb), lambda s, t: (0, 0))],
        out_specs=pl.BlockSpec((tb, C), blk),
        out_shape=jax.ShapeDtypeStruct((T, C), F32),
        scratch_shapes=[pltpu.VMEM((1, C), F32)],
        compiler_params=_cparams(("arbitrary", "arbitrary")),
    )(x, tri)


def _place():
    return lax.axis_index("x"), lax.axis_index("y"), lax.axis_index("c")


def _other_chips(x, y):
    return [(1 - x, y), (x, 1 - y), (1 - x, 1 - y)]


def _hbm_call(body, ins, out_shape, n_sems, *, name):
    hbm = pl.BlockSpec(memory_space=pl.ANY)
    return pl.pallas_call(
        body, name=name,
        in_specs=[hbm] * len(ins), out_specs=[hbm] * len(out_shape), out_shape=out_shape,
        scratch_shapes=[pltpu.SemaphoreType.DMA((n_sems,)), pltpu.SemaphoreType.DMA((n_sems,)),
                        pltpu.SemaphoreType.DMA((len(ins),))],
        compiler_params=pltpu.CompilerParams(has_side_effects=True),
    )(*ins)


def allgather_chips(shards, *, name):
    return _exchange_call(allgather_rider(shards), name=name)


def _allgather_ops(x_refs, o_refs, send_sems, recv_sems, local_sems):
    def copies():
        x, y, c = _place()
        me = 2 * x + y
        chips = _other_chips(x, y)
        own, first, passed, landed, handed = [], [], [], [], []
        for b, (x_ref, o_ref) in enumerate(zip(x_refs, o_refs)):
            half = x_ref.shape[0] // 2
            mine, theirs = pl.ds(c * half, half), pl.ds((1 - c) * half, half)
            own.append(pltpu.make_async_copy(x_ref, o_ref.at[me], local_sems.at[b]))

            def copy(k, src, chip, rows, to, o_ref=o_ref, b=b):
                return pltpu.make_async_remote_copy(src_ref=src, dst_ref=o_ref.at[2 * chip[0] + chip[1], rows],
                                                    send_sem=send_sems.at[6 * b + k], recv_sem=recv_sems.at[6 * b + k],
                                                    device_id=to, device_id_type=MESH)
            for j, chip in enumerate(chips):
                first.append(copy(j, x_ref.at[mine], (x, y), mine, (*chip, c)))
                landed.append(copy(j, x_ref.at[mine], chip, mine, (*chip, c)))
                passed.append(copy(3 + j, o_ref.at[2 * chip[0] + chip[1], mine], chip, mine, (x, y, 1 - c)))
                handed.append(copy(3 + j, x_ref.at[mine], chip, theirs, (x, y, 1 - c)))
        return own, first, passed, landed, handed

    def start():
        own, first, _, _, _ = copies()
        for cp in own + first:
            cp.start()

    def finish():
        own, first, passed, landed, handed = copies()
        for arrived, forward in zip(landed, passed):
            arrived.wait_recv()
            forward.start()
        for cp in handed:
            cp.wait_recv()
        for cp in first + passed:
            cp.wait_send()
        for cp in own:
            cp.wait()
    return start, finish


def _scatter_ops(x_refs, o_refs, send_sems, recv_sems, local_sems):
    def copies():
        x, y, c = _place()
        return [pltpu.make_async_remote_copy(
            src_ref=x_ref.at[2 * px + py], dst_ref=o_ref.at[j], send_sem=send_sems.at[3 * b + j],
            recv_sem=recv_sems.at[3 * b + j], device_id=(px, py, c), device_id_type=MESH)
            for b, (x_ref, o_ref) in enumerate(zip(x_refs, o_refs)) for j, (px, py) in enumerate(_other_chips(x, y))]

    def start():
        for cp in copies():
            cp.start()

    def finish():
        sends = copies()
        for cp in sends:
            cp.wait_recv()
        for cp in sends:
            cp.wait_send()
    return start, finish


class Rider(NamedTuple):
    ins: list
    out_shape: list
    n_sems: int
    ops: object

    def specs(self):
        hbm = pl.BlockSpec(memory_space=pl.ANY)
        sems = [pltpu.SemaphoreType.DMA((self.n_sems,)), pltpu.SemaphoreType.DMA((self.n_sems,)),
                pltpu.SemaphoreType.DMA((len(self.ins),))]
        return [hbm] * len(self.ins), [hbm] * len(self.out_shape), sems

    def wrap(self, body, n_in, n_out, grid_rank):
        k_in, k_out = len(self.ins), len(self.out_shape)

        def carried(*refs):
            ins, r_ins = refs[:n_in], refs[n_in:n_in + k_in]
            outs = refs[n_in + k_in:n_in + k_in + n_out]
            r_outs = refs[n_in + k_in + n_out:n_in + k_in + n_out + k_out]
            scratch, sems = refs[n_in + k_in + n_out + k_out:-3], refs[-3:]
            first = functools.reduce(jnp.logical_and, [pl.program_id(a) == 0 for a in range(grid_rank)])
            last = functools.reduce(jnp.logical_and,
                                    [pl.program_id(a) == pl.num_programs(a) - 1 for a in range(grid_rank)])
            pl.when(first)(lambda: self.ops(r_ins, r_outs, *sems)[0]())
            body(*ins, *outs, *scratch)
            pl.when(last)(lambda: self.ops(r_ins, r_outs, *sems)[1]())
        return carried


def _exchange_call(rider, *, name):
    def body(*refs):
        k = len(rider.ins)
        start, finish = rider.ops(refs[:k], refs[k:k + len(rider.out_shape)], *refs[-3:])
        start()
        finish()
    in_specs, out_specs, sems = rider.specs()
    return pl.pallas_call(body, name=name, in_specs=in_specs, out_specs=out_specs, out_shape=rider.out_shape,
                          scratch_shapes=sems, compiler_params=pltpu.CompilerParams(has_side_effects=True))(*rider.ins)


def allgather_rider(shards):
    assert all(s.shape[0] % (2 * ROW_ALIGN) == 0 for s in shards)
    return Rider(list(shards), [jax.ShapeDtypeStruct((4,) + s.shape, s.dtype) for s in shards], 6 * len(shards),
                 _allgather_ops)


def scatter_rider(parts):
    return Rider(list(parts), [jax.ShapeDtypeStruct((3,) + p.shape[1:], p.dtype) for p in parts], 3 * len(parts),
                 _scatter_ops)


def scatter_chips(parts, *, name):
    return _exchange_call(scatter_rider(parts), name=name)


def swap_cores(vs, *, name):
    nb = len(vs)

    def body(*refs):
        x_refs, o_refs = refs[:nb], refs[nb:2 * nb]
        send_sems, recv_sems, _ = refs[2 * nb:]
        x, y, c = _place()
        copies = [pltpu.make_async_remote_copy(src_ref=x_ref, dst_ref=o_ref, send_sem=send_sems.at[b],
                                               recv_sem=recv_sems.at[b], device_id=(x, y, 1 - c), device_id_type=MESH)
                  for b, (x_ref, o_ref) in enumerate(zip(x_refs, o_refs))]
        for cp in copies:
            cp.start()
        for cp in copies:
            cp.wait()

    return _hbm_call(body, vs, [jax.ShapeDtypeStruct(v.shape, v.dtype) for v in vs], nb, name=name)


def allreduce_small(v, *, name):
    R, C = v.shape

    def body(x_ref, o_ref, gath_ref, send_sems, recv_sems):
        x, y, c = _place()
        me = 4 * x + 2 * y + c
        gath_ref[me] = x_ref[...]
        flips = [(k >> 2 & 1, k >> 1 & 1, k & 1) for k in range(1, 8)]
        sends = []
        for j, (fx, fy, fc) in enumerate(flips):
            peer = (x ^ fx, y ^ fy, c ^ fc)
            cp = pltpu.make_async_remote_copy(src_ref=x_ref, dst_ref=gath_ref.at[me], send_sem=send_sems.at[j],
                                              recv_sem=recv_sems.at[j], device_id=peer, device_id_type=MESH)
            cp.start()
            sends.append(cp)
        for j, (fx, fy, fc) in enumerate(flips):
            peer = (x ^ fx, y ^ fy, c ^ fc)
            pltpu.make_async_remote_copy(src_ref=x_ref, dst_ref=gath_ref.at[4 * peer[0] + 2 * peer[1] + peer[2]],
                                         send_sem=send_sems.at[j], recv_sem=recv_sems.at[j], device_id=peer,
                                         device_id_type=MESH).wait_recv()
        for cp in sends:
            cp.wait_send()
        total = gath_ref[0]
        for d in range(1, 8):
            total = total + gath_ref[d]
        o_ref[...] = total

    vm = pl.BlockSpec(memory_space=pltpu.VMEM)
    out, _ = pl.pallas_call(
        body, name=name,
        in_specs=[vm], out_specs=[vm, vm],
        out_shape=[jax.ShapeDtypeStruct((R, C), F32), jax.ShapeDtypeStruct((8, R, C), F32)],
        scratch_shapes=[pltpu.SemaphoreType.DMA((7,)), pltpu.SemaphoreType.DMA((7,))],
        compiler_params=pltpu.CompilerParams(has_side_effects=True),
    )(v)
    return out


ROW_ALIGN = 16
PACK_W = 1024
SUM_TILE = 512
BIG_WEIGHTS = (("w_in", 1), ("w_a", 1), ("w_b", 1), ("w_o", 0), ("w_ff1", 1), ("w_ff2", 0), ("w_pg", 0), ("w_p", 1))


def _b_layout(d, ple):
    hw, q = d // 2, d // 4
    small = 2 * d + 2 * q
    lay = {"w_ff1": (0, 0, d, d), "w_ff2": (d, 0, d, d), "w_o": (2 * d, 0, q, d), "w_pg": (2 * d + q, 0, q, d),
           "w_a": (small, 0, hw, q), "w_b": (small, q, hw, q), "w_p": (small, 2 * q, ple, q)}
    return lay, small + hw


def pack_a(w_in_shard):
    rows, cols = w_in_shard.shape
    pad = -cols % LANES
    return jnp.concatenate([w_in_shard, jnp.zeros((rows, pad), w_in_shard.dtype)], axis=1)


def pack_b(shards, d):
    hw, q = d // 2, d // 4
    dt = shards["w_a"].dtype
    wp = shards["w_p"]
    wp = jnp.concatenate([wp, jnp.zeros((hw - wp.shape[0], q), dt)], axis=0)
    small = jnp.concatenate([shards["w_a"], shards["w_b"], wp, jnp.zeros((hw, d - 3 * q), dt)], axis=1)
    return jnp.concatenate([shards["w_ff1"], shards["w_ff2"], shards["w_o"], shards["w_pg"], small], axis=0)


def unpack_b(buf, lay):
    return {nm: buf[r0:r0 + rows, c0:c0 + cols] for nm, (r0, c0, rows, cols) in lay.items()}


def _win_layout(d):
    hw = d // 2
    fh = hw // FOX_HDIM
    orig = {"hq": (0, hw), "hf": (hw, hw), "hi": (2 * hw, hw), "hg": (3 * hw, hw), "fq": (4 * hw, hw),
            "fk": (5 * hw, hw), "fv": (6 * hw, hw), "ff": (7 * hw, fh), "ga": (7 * hw + fh, d), "gb": (7 * hw + fh + d, d)}
    order = ["ga", "gb", "hq", "hf", "hi", "hg", "fq", "fk", "fv", "ff"]
    mine, off = {}, 0
    for nm in order:
        width = orig[nm][1] if nm != "ff" else LANES
        mine[nm] = (off, width)
        off += width
    return orig, order, mine, off


def _adam_fn(rows, vecs):
    w, g, m, v = rows
    m2 = ADAM_B1 * m + (1.0 - ADAM_B1) * g
    v2 = ADAM_B2 * v + (1.0 - ADAM_B2) * (g * g)
    m_hat = m2 / (1.0 - ADAM_B1 ** ADAM_STEP)
    v_hat = v2 / (1.0 - ADAM_B2 ** ADAM_STEP)
    delta = -ADAM_LR * (m_hat / (jnp.sqrt(v_hat) + ADAM_EPS) + ADAM_WD * w)
    return [delta, m2, v2], []


def adamw_small(small, p0, ws, ms, vs, *, name):
    n = len(ws)
    hw = p0.shape[1]
    fh = ws[8].shape[1]

    def body(small_ref, p0_ref, *refs):
        w_refs, m_refs, v_refs = refs[:n], refs[n:2 * n], refs[2 * n:3 * n]
        g_out, d_out, m_out, v_out = (refs[(3 + k) * n:(4 + k) * n] for k in range(4))
        sm = small_ref[...]
        p = p0_ref[...]
        d_lb = sm[6:7, hw:2 * hw] * (p * (1.0 - p))
        grads = [sm[r:r + 1, :] for r in range(6)]
        grads += [jnp.concatenate([d_lb, -d_lb], axis=0), sm[6:7, :hw], sm[7:8, :fh]]
        for i in range(n):
            (delta, m2, v2), _ = _adam_fn([w_refs[i][...], grads[i], m_refs[i][...], v_refs[i][...]], [])
            g_out[i][...], d_out[i][...], m_out[i][...], v_out[i][...] = grads[i], delta, m2, v2

    shapes = [jax.ShapeDtypeStruct(w.shape, F32) for w in ws]
    return pl.pallas_call(body, name=name, out_shape=shapes * 4)(small, p0, *ws, *ms, *vs)


def adamw(w, g, m, v, *, name):
    c = w.shape[1]
    (delta, m2, v2), _ = rowwise(_adam_fn, [w, g, m, v], [], [(c, F32)] * 3, name=name, tm=256)
    return delta, m2, v2


def kernel(x, p, ln0_g, ln0_b, w_in, hg_lb, hg_norm_g, fox_fb, w_a, w_b, w_o, ln1_g, ln1_b, w_ff1, w_ff2, w_pg, w_p, ln2_g, ln2_b, loss_target, m_ln0_g, m_ln0_b, m_w_in, m_hg_lb, m_hg_norm_g, m_fox_fb, m_w_a, m_w_b, m_w_o, m_ln1_g, m_ln1_b, m_w_ff1, m_w_ff2, m_w_pg, m_w_p, m_ln2_g, m_ln2_b, v_ln0_g, v_ln0_b, v_w_in, v_hg_lb, v_hg_norm_g, v_fox_fb, v_w_a, v_w_b, v_w_o, v_ln1_g, v_ln1_b, v_w_ff1, v_w_ff2, v_w_pg, v_w_p, v_ln2_g, v_ln2_b):
    n_seq, seq, d = x.shape
    T = n_seq * seq
    hw = d // 2
    fh = hw // FOX_HDIM
    bh = n_seq * fh
    orig, order, mine, n_in = _win_layout(d)

    big = {"w_in": w_in[0], "w_a": w_a[0], "w_b": w_b[0], "w_o": w_o[0], "w_ff1": w_ff1[0], "w_ff2": w_ff2[0],
           "w_pg": w_pg[0], "w_p": w_p[0]}
    big_m = {"w_in": m_w_in[0], "w_a": m_w_a[0], "w_b": m_w_b[0], "w_o": m_w_o[0], "w_ff1": m_w_ff1[0],
             "w_ff2": m_w_ff2[0], "w_pg": m_w_pg[0], "w_p": m_w_p[0]}
    big_v = {"w_in": v_w_in[0], "w_a": v_w_a[0], "w_b": v_w_b[0], "w_o": v_w_o[0], "w_ff1": v_w_ff1[0],
             "w_ff2": v_w_ff2[0], "w_pg": v_w_pg[0], "w_p": v_w_p[0]}
    names = [nm for nm, _ in BIG_WEIGHTS]
    axis = dict(BIG_WEIGHTS)
    ple = w_p.shape[1]
    lay, b_rows = _b_layout(d, ple)
    in_cols = big["w_in"].shape[1]

    gather_w_in = allgather_rider([pack_a(big["w_in"].astype(BF16))])
    gather_rest = allgather_rider([pack_b({nm: big[nm].astype(BF16) for nm in names if nm != "w_in"}, d)])

    x2 = x.reshape(T, d)
    tgt = loss_target.reshape(T, d)
    p_b = p.reshape(T, p.shape[-1]).astype(BF16)
    vec = lambda a: a.reshape(1, -1)
    probs = jax.nn.softmax(hg_lb, axis=0)
    lb = vec(probs[0])

    def ln0_fn(rows, vecs):
        h = _ln_stats(rows[0]) * vecs[0] + vecs[1]
        return [h, h], []
    (h0, h0b), _, (a_all,) = rowwise(ln0_fn, [x2], [vec(ln0_g), vec(ln0_b)], [(d, F32), (d, BF16)], name="ln0_fwd",
                                     rider=gather_w_in)
    win = jnp.concatenate([a_all[s, :, :in_cols] for s in range(4)], axis=1)
    win_mine = jnp.concatenate(
        [win[:, orig[nm][0]:orig[nm][0] + orig[nm][1]] for nm in order]
        + [jnp.zeros((d, LANES - fh), BF16)], axis=1)
    proj = matmul_nn(h0b, win_mine, name="in_proj")

    o_raw, hg_states, (b_all,) = hgrn2_fwd(proj, [mine["hq"][0], mine["hf"][0], mine["hi"][0]], lb, n_seq, seq,
                                           name="hgrn2_fwd", rider=gather_rest)
    view = lambda nm, k, n: WView(b_all, lay[nm][0], lay[nm][1], k, n, axis[nm])
    w_ff1_v, w_ff2_v = view("w_ff1", d, 4 * d), view("w_ff2", 4 * d, d)

    def whole(nm):
        r0, c0, rows, cols = lay[nm]
        return jnp.concatenate([b_all[s, r0:r0 + rows, c0:c0 + cols] for s in range(4)], axis=axis[nm])
    w_o_v, w_pg_v, w_a_v, w_p_v, w_b_full = whole("w_o"), whole("w_pg"), whole("w_a"), whole("w_p"), whole("w_b")

    def ya_fn(rows, vecs):
        o, hg = rows
        outs = []
        for h in range(HG_HEADS):
            oh = o[:, h * HG_DIM:(h + 1) * HG_DIM]
            outs.append(oh * lax.rsqrt(jnp.mean(oh * oh, axis=-1, keepdims=True) + RMS_EPS))
        y = jnp.concatenate(outs, axis=1) * vecs[0] * (hg * _sigmoid(hg))
        return [y], []
    (y_a,), _ = rowwise(ya_fn, [o_raw, (proj,) + mine["hg"]], [hg_norm_g], [(hw, BF16)], name="hgrn2_out_fwd")

    fb_pad = jnp.concatenate([fox_fb, jnp.zeros((1, LANES - fh), F32)], axis=1)

    def lf_fn(rows, vecs):
        u = rows[0] + vecs[0]
        return [jnp.minimum(u, 0.0) - jnp.log(1.0 + jnp.exp(-jnp.abs(u)))], []
    (lf,), _ = rowwise(lf_fn, [(proj,) + mine["ff"]], [fb_pad], [(LANES, F32)], name="fox_logf")
    c_cum = seq_cumsum(lf, n_seq, seq, reverse=False, name="fox_cumsum")

    place = _fox_placement(fh)

    def prep_fn(rows, vecs):
        fq_, fk_, fv_, cc = rows
        pq, pk, aq, ak, oq, ok = vecs
        parts = jnp.concatenate(_split3(cc), axis=1)
        mm = lambda a_, b_: jnp.dot(a_, b_, preferred_element_type=F32)
        q_ = mm(fq_.astype(BF16), pq) + mm(parts, aq) + oq
        k_ = mm(fk_.astype(BF16), pk) + mm(parts, ak) + ok
        return [q_, k_, mm(fv_.astype(BF16), pk)], []
    wa = fh * FOX_AUG
    (qa, ka, va), _ = rowwise(prep_fn, [(proj,) + mine["fq"], (proj,) + mine["fk"], (proj,) + mine["fv"], c_cum],
                              [place[nm] for nm in ("pq", "pk", "aq", "ak", "oq", "ok")], [(wa, BF16)] * 3,
                              name="fox_prep")
    as_seq = lambda t2d: t2d.reshape(n_seq, seq, t2d.shape[1])
    o_fox, ox_fox, lse = fox_fwd(as_seq(qa), as_seq(ka), as_seq(va), name="fox_fwd")
    y_b = o_fox.reshape(T, wa)
    wb_pad = jnp.concatenate([w_b_full.reshape(fh, FOX_HDIM, d), jnp.zeros((fh, FOX_AUG - FOX_HDIM, d), BF16)],
                             axis=1).reshape(wa, d)

    fused_tm = 512
    pa = matmul_nn(y_a, w_a_v, name="proj_a")

    def merge_post(pb_, aux, vecs):
        ga, gb, a = aux
        return [_sigmoid(ga) * a + _sigmoid(gb) * pb_, pb_], []
    (merged, pb), _ = matmul_nn(y_b, wb_pad, name="proj_b_merge", tm=fused_tm, post=merge_post,
                                post_aux=[(proj,) + mine["ga"], (proj,) + mine["gb"], pa], post_outs=[BF16, F32])

    def ln1_post(mix, aux, vecs):
        z = ALPHA * aux[0] + mix
        h = _ln_stats(z) * vecs[0] + vecs[1]
        return [z, h, h], []
    (z1, h1, h1b), _ = matmul_nn(merged, w_o_v, name="out_proj_ln1", tm=fused_tm, post=ln1_post, post_aux=[h0],
                                 post_vecs=[ln1_g, ln1_b], post_outs=[F32, F32, BF16])

    relu2 = lambda u: jnp.square(jnp.maximum(u, 0.0))
    act = matmul_nn(h1b, w_ff1_v, name="ff1", out_dtype=BF16, epilogue=relu2)
    pg = matmul_nn(h1b, w_pg_v, name="ple_gate")
    pe = matmul_nn(p_b, w_p_v, name="ple_embed")

    def head_post(ffv, aux, vecs):
        h1v, pgv, pev, t = aux
        g2, b2 = vecs
        sp = _sigmoid(pgv)
        z = ALPHA * h1v + ffv + sp * pev
        y = _ln_stats(z) * g2 + b2
        err = y - t
        loss_rows = 0.5 * jnp.mean(err * err, axis=-1, keepdims=True)
        dy = err * (1.0 / d)
        dz, dg2, db2 = _ln_bwd(z, dy, g2)
        loss_acc = jnp.broadcast_to(_colsum(loss_rows), (1, d))
        return [dz, dz, dz * pev * (sp * (1.0 - sp)), dz * sp], [dg2, db2, loss_acc]
    (dz2, dz2b, dpg, dpe), (g_ln2_g, g_ln2_b, loss_part) = matmul_nn(
        act, w_ff2_v, name="ff2_head", tm=fused_tm, post=head_post, post_aux=[h1, pg, pe, tgt],
        post_vecs=[ln2_g, ln2_b], post_outs=[F32, BF16, BF16, BF16], post_accs=[d, d, d])

    dact = lambda da, a: da * (2.0 * jnp.sqrt(a.astype(F32)))
    du = matmul_nn(dz2b, w_ff2_v, transpose_rhs=True, name="d_ff2", out_dtype=BF16, epilogue=dact, aux=act)
    dh1_pg = matmul_nn(dpg, w_pg_v, transpose_rhs=True, name="d_ple_gate")

    def ln1_bwd_post(dh1_ff, aux, vecs):
        dh1 = ALPHA * aux[0] + dh1_ff + aux[1]
        dz, dg, db = _ln_bwd(aux[2], dh1, vecs[0])
        return [dz, dz], [dg, db]
    w_ff1_t = WView(b_all[:, lay["w_ff1"][0]:lay["w_ff1"][0] + d, :].transpose(0, 2, 1), 0, 0, 4 * d, d, 0)
    (dz1, dz1b), (g_ln1_g, g_ln1_b) = matmul_nn(
        du, w_ff1_t, name="d_ff1_ln1", tm=fused_tm, post=ln1_bwd_post, post_aux=[dz2, dh1_pg, z1],
        post_vecs=[ln1_g], post_outs=[F32, BF16], post_accs=[d, d])

    def merge_bwd_post(dm, aux, vecs):
        ga, gb, a, b = aux
        sa, sb = _sigmoid(ga), _sigmoid(gb)
        return [dm * a * (sa * (1.0 - sa)), dm * b * (sb * (1.0 - sb)), dm * sa, dm * sb], []
    (dga, dgb, dma, dmb), _ = matmul_nn(
        dz1b, w_o_v, transpose_rhs=True, name="d_out_proj_merge", tm=fused_tm, post=merge_bwd_post,
        post_aux=[(proj,) + mine["ga"], (proj,) + mine["gb"], pa, pb], post_outs=[BF16] * 4)
    dya = matmul_nn(dma, w_a_v, transpose_rhs=True, name="d_proj_a")
    dyb = matmul_nn(dmb, wb_pad, transpose_rhs=True, name="d_proj_b", out_dtype=BF16)

    def ya_bwd_fn(rows, vecs):
        o, hg, dy = rows
        ng = vecs[0]
        sg = _sigmoid(hg)
        gate = hg * sg
        dn_parts, do_parts, n_parts = [], [], []
        for h in range(HG_HEADS):
            hs = slice(h * HG_DIM, (h + 1) * HG_DIM)
            oh = o[:, hs]
            r = lax.rsqrt(jnp.mean(oh * oh, axis=-1, keepdims=True) + RMS_EPS)
            nh = oh * r
            dn = dy[:, hs] * ng[:, hs] * gate[:, hs]
            do_parts.append(r * (dn - nh * jnp.mean(dn * nh, axis=-1, keepdims=True)))
            n_parts.append(nh)
        nrm = jnp.concatenate(n_parts, axis=1)
        dhg = dy * nrm * ng * (sg * (1.0 + hg * (1.0 - sg)))
        return [jnp.concatenate(do_parts, axis=1), dhg], [_colsum(dy * nrm * gate)]
    (do_raw, dhg), (g_norm_g,) = rowwise(ya_bwd_fn, [o_raw, (proj,) + mine["hg"], dya], [hg_norm_g],
                                         [(hw, F32), (hw, BF16)], [hw], name="hgrn2_out_bwd")
    dhq, dhf, dhi, g_lb = hgrn2_bwd(proj, [mine["hq"][0], mine["hf"][0], mine["hi"][0]], lb, do_raw, hg_states,
                                    n_seq, seq, name="hgrn2_bwd")

    do_fox = as_seq(dyb)
    dqa, dka, dva, dsum = fox_bwd(as_seq(qa), as_seq(ka), as_seq(va), do_fox, ox_fox, lse, name="fox_bwd")

    def unprep_fn(rows, vecs):
        mm = lambda a_, b_: jnp.dot(a_.astype(BF16), b_, preferred_element_type=F32)
        return [mm(rows[0], vecs[0]), mm(rows[1], vecs[1]), mm(rows[2], vecs[1])], []
    (dfq, dfk, dfv), _ = rowwise(unprep_fn, [dqa.reshape(T, wa), dka.reshape(T, wa), dva.reshape(T, wa)],
                                 [place["pqt"], place["pkt"]], [(hw, BF16)] * 3, name="fox_unprep")
    dc = -dsum.reshape(n_seq, fh, seq).transpose(0, 2, 1).reshape(T, fh)
    dc = jnp.concatenate([dc, jnp.zeros((T, LANES - fh), F32)], axis=1)
    dlf = seq_cumsum(dc, n_seq, seq, reverse=True, name="fox_cumsum_bwd")

    def lf_bwd_fn(rows, vecs):
        u = rows[0] + vecs[0]
        du_ = rows[1] * _sigmoid(-u)
        return [du_], [_colsum(du_)]
    (dff_,), (g_fb,) = rowwise(lf_bwd_fn, [(proj,) + mine["ff"], dlf], [fb_pad], [(LANES, BF16)], [LANES],
                               name="fox_logf_bwd")

    dproj = jnp.concatenate([dga, dgb, dhq, dhf, dhi, dhg, dfq, dfk, dfv, dff_], axis=1)

    grads_b = jnp.zeros((4, b_rows, d), F32)
    for nm, lhs, rhs in (("w_ff1", h1b, du), ("w_ff2", act, dz2b)):
        grads_b = matmul_tn(lhs, rhs, name="g_" + nm, into=(grads_b, lay[nm][0], lay[nm][1], axis[nm]))
    gfull = {
        "w_a": matmul_tn(y_a, dma, name="g_w_a"),
        "w_b": matmul_tn(y_b, dmb, name="g_w_b").reshape(fh, FOX_AUG, d)[:, :FOX_HDIM].reshape(hw, d),
        "w_o": matmul_tn(merged, dz1b, name="g_w_o"),
        "w_pg": matmul_tn(h1b, dpg, name="g_w_pg"),
        "w_p": matmul_tn(p_b, dpe, name="g_w_p"),
    }

    def chip_parts(nm, s):
        g = gfull[nm]
        n = g.shape[axis[nm]] // 4
        return lax.slice_in_dim(g, s * n, (s + 1) * n, axis=axis[nm])
    for nm in gfull:
        grads_b = lax.dynamic_update_slice(grads_b, jnp.stack([chip_parts(nm, s) for s in range(4)]),
                                           (0, lay[nm][0], lay[nm][1]))
    me = 2 * lax.axis_index("x") + lax.axis_index("y")
    core = lax.axis_index("c")

    def sum2_fn(rows, vecs):
        s = rows[0] + rows[1].astype(F32)
        return [s, s], []

    def sum4_fn(rows, vecs):
        a, r0, r1, r2 = rows
        return [((a + r0.astype(F32)) + r1.astype(F32)) + r2.astype(F32)], []

    def chip_pair_sum(g, tag):
        h, cols = g.shape[1] // 2, g.shape[2]
        keep = lax.dynamic_slice_in_dim(g, core * h, h, axis=1)
        give = lax.dynamic_slice_in_dim(g, (1 - core) * h, h, axis=1).astype(BF16)
        (from_core,) = swap_cores([give], name="swap_partials_" + tag)
        (s32, s16), _ = rowwise(sum2_fn, [keep.reshape(4 * h, cols), from_core.reshape(4 * h, cols)], [],
                                [(cols, F32), (cols, BF16)], name="sum_cores_" + tag, tm=SUM_TILE)
        return s32.reshape(4, h, cols), s16.reshape(4, h, cols)

    def chip_sum(pr, gt, tag):
        own = lax.dynamic_index_in_dim(pr, me, axis=0, keepdims=False)
        (q,), _ = rowwise(sum4_fn, [own, gt[0], gt[1], gt[2]], [], [(own.shape[1], F32)], name="sum_chips_" + tag,
                          tm=SUM_TILE)
        return q

    pair_rest, pair_rest_b = chip_pair_sum(grads_b, "rest")
    gw_in_mine, (got_rest,) = matmul_tn(h0b, dproj, name="g_w_in", rider=scatter_rider([pair_rest_b]))
    gfull["w_in"] = jnp.concatenate([gw_in_mine[:, mine[nm][0]:mine[nm][0] + orig[nm][1]]
                                     for nm in ["hq", "hf", "hi", "hg", "fq", "fk", "fv", "ff", "ga", "gb"]], axis=1)
    grads_a = jnp.stack([pack_a(chip_parts("w_in", s)) for s in range(4)])
    pair_in, pair_in_b = chip_pair_sum(grads_a, "w_in")
    def ln0_bwd_post(dh0_in, aux, vecs):
        dx, dg, db = _ln_bwd(aux[1], dh0_in + ALPHA * aux[0], vecs[0])
        return [dx], [dg, db]
    ((dx,), (g_ln0_g, g_ln0_b)), (got_in,) = matmul_nn(
        dproj, win_mine.T, name="d_in_proj_ln0", tm=fused_tm, post=ln0_bwd_post,
        post_aux=[dz1, x2], post_vecs=[vec(ln0_g)], post_outs=[F32], post_accs=[d, d],
        rider=scatter_rider([pair_in_b]))
    q_half = [chip_sum(pair_in, got_in, "w_in"), chip_sum(pair_rest, got_rest, "rest")]
    q_other = swap_cores(q_half, name="swap_halves")
    g_a, g_b = [jnp.concatenate([jnp.where(core == 0, mine_, other), jnp.where(core == 0, other, mine_)], axis=0)
                for mine_, other in zip(q_half, q_other)]
    g_shards = unpack_b(g_b, lay)
    g_shards["w_in"] = g_a[:, :in_cols]

    assert d == PACK_W and 2 * hw == PACK_W and fh <= LANES
    small = allreduce_small(jnp.concatenate(
        [g_ln0_g, g_ln0_b, g_ln1_g, g_ln1_b, g_ln2_g, g_ln2_b, jnp.concatenate([g_norm_g, g_lb], axis=1),
         jnp.concatenate([g_fb, loss_part[:, LANES:]], axis=1)], axis=0), name="allreduce_small")
    loss = small[7, LANES]

    small_w = [vec(ln0_g), vec(ln0_b), ln1_g, ln1_b, ln2_g, ln2_b, hg_lb, hg_norm_g, fox_fb]
    small_m = [vec(m_ln0_g), vec(m_ln0_b), m_ln1_g, m_ln1_b, m_ln2_g, m_ln2_b, m_hg_lb, m_hg_norm_g, m_fox_fb]
    small_v = [vec(v_ln0_g), vec(v_ln0_b), v_ln1_g, v_ln1_b, v_ln2_g, v_ln2_b, v_hg_lb, v_hg_norm_g, v_fox_fb]
    small_out = adamw_small(small, probs[0:1], small_w, small_m, small_v, name="adamw_small")
    small_shapes = [ln0_g.shape, ln0_b.shape, ln1_g.shape, ln1_b.shape, ln2_g.shape, ln2_b.shape, hg_lb.shape,
                    hg_norm_g.shape, fox_fb.shape]
    sg_out, sd_out, sm_out, sv_out = [[a.reshape(shp) for a, shp in zip(small_out[9 * k:9 * k + 9], small_shapes)]
                                      for k in range(4)]

    big_out = {}
    for nm in names:
        delta, m2, v2 = adamw(big[nm], g_shards[nm], big_m[nm], big_v[nm], name="adamw_" + nm)
        big_out[nm] = (g_shards[nm][None], delta[None], m2[None], v2[None])

    def ordered(k):
        sm_ = [sg_out, sd_out, sm_out, sv_out][k]
        bg = lambda nm: big_out[nm][k]
        return [sm_[0], sm_[1], bg("w_in"), sm_[6], sm_[7], sm_[8], bg("w_a"), bg("w_b"), bg("w_o"), sm_[2], sm_[3],
                bg("w_ff1"), bg("w_ff2"), bg("w_pg"), bg("w_p"), sm_[4], sm_[5]]
    grad_x = dx.reshape(n_seq, seq, d)
    return (loss, grad_x, *ordered(0), *ordered(1), *ordered(2), *ordered(3))
```

```python
import functools
from typing import NamedTuple, Optional

import numpy as np
import jax
import jax.numpy as jnp
from jax import lax
from jax.experimental import pallas as pl
from jax.experimental.pallas import tpu as pltpu

F32 = jnp.float32
BF16 = jnp.bfloat16
MESH = pl.DeviceIdType.MESH

VMEM_LIMIT_BYTES = 48 * 1024 * 1024
LANES = 128
HG_HEADS = 4
HG_DIM = 128
HG_BLK = 16
HG_TILE = 256
HG_SLOTS = 8
FOX_HDIM = 64
FOX_AUG = 128
FOX_TQ = 1024
FOX_FWD_HEADS = 1
LN_EPS = 1e-5
RMS_EPS = 1e-6
DEPTH = 1
ALPHA = (2.0 * DEPTH) ** 0.25
ADAM_LR, ADAM_B1, ADAM_B2, ADAM_EPS, ADAM_WD, ADAM_STEP = 0.001, 0.9, 0.999, 1e-08, 0.01, 10
NEG_INF = -1e30


def _cparams(sem):
    return pltpu.CompilerParams(dimension_semantics=sem, vmem_limit_bytes=VMEM_LIMIT_BYTES)


def _tile(n, cap):
    if n <= cap:
        return n
    best = None
    for t in range(LANES, cap + 1, LANES):
        if n % t == 0:
            best = t
    assert best is not None, (n, cap)
    return best


class WView(NamedTuple):
    arr: jax.Array
    r0: int
    c0: int
    k: int
    n: int
    split: Optional[int]


def matmul_nn(a, w, *, name, transpose_rhs=False, out_dtype=F32, epilogue=None, aux=None, tm=2048, rider=None,
              post=None, post_aux=(), post_vecs=(), post_outs=(), post_accs=()):
    wv = w if isinstance(w, WView) else WView(w[None], 0, 0, w.shape[0], w.shape[1], None)
    rows_s = wv.k // 4 if wv.split == 0 else wv.k
    cols_s = wv.n // 4 if wv.split == 1 else wv.n
    tr, tc = _tile(rows_s, 1152), _tile(cols_s, 1152)
    assert wv.r0 % tr == 0 and wv.c0 % tc == 0
    T, K = a.shape
    N, tn, tk = (wv.k, tr, tc) if transpose_rhs else (wv.n, tc, tr)
    assert K == (wv.n if transpose_rhs else wv.k)
    tm = min(tm, T)
    assert T % tm == 0
    nk = K // tk

    def w_block(ri, ci):
        if wv.split == 0:
            return (ri * tr) // rows_s, (wv.r0 + (ri * tr) % rows_s) // tr, wv.c0 // tc + ci
        if wv.split == 1:
            return (ci * tc) // cols_s, wv.r0 // tr + ri, (wv.c0 + (ci * tc) % cols_s) // tc
        return 0, wv.r0 // tr + ri, wv.c0 // tc + ci

    fused = post is not None
    assert not fused or N == tn
    aux_list = list(post_aux) if fused else ([aux] if aux is not None else [])
    aux_list = [x if isinstance(x, tuple) else (x, 0, x.shape[1]) for x in aux_list]
    vec_list = list(post_vecs)
    out_dtypes = list(post_outs) if fused else [out_dtype]
    n_aux, n_vec, n_out, n_acc = len(aux_list), len(vec_list), len(out_dtypes), len(post_accs)

    def body(*refs):
        a_ref, w_ref = refs[:2]
        aux_refs = refs[2:2 + n_aux]
        vec_refs = refs[2 + n_aux:2 + n_aux + n_vec]
        out_refs = refs[2 + n_aux + n_vec:2 + n_aux + n_vec + n_out]
        sum_refs = refs[2 + n_aux + n_vec + n_out:2 + n_aux + n_vec + n_out + n_acc]
        acc_ref = refs[-1]
        m, k = pl.program_id(1), pl.program_id(2)
        if transpose_rhs:
            part = lax.dot_general(a_ref[...], w_ref[...], (((1,), (1,)), ((), ())), preferred_element_type=F32)
        else:
            part = jnp.dot(a_ref[...], w_ref[...], preferred_element_type=F32)

        def write(res):
            if not fused:
                if epilogue is not None:
                    res = epilogue(res) if not aux_refs else epilogue(res, aux_refs[0][...])
                out_refs[0][...] = res.astype(out_dtype)
                return
            outs, sums = post(res, [r[...] for r in aux_refs], [v[...] for v in vec_refs])
            assert len(outs) == n_out and len(sums) == n_acc
            for r, val in zip(out_refs, outs):
                r[...] = val.astype(r.dtype)
            for r, val in zip(sum_refs, sums):
                def first_rows(r=r, val=val):
                    r[...] = val

                def later_rows(r=r, val=val):
                    r[...] += val
                pl.when(m == 0)(first_rows)
                pl.when(m > 0)(later_rows)

        if nk == 1:
            write(part)
        else:
            @pl.when(k == 0)
            def _():
                acc_ref[...] = part

            @pl.when(k > 0)
            def _():
                acc_ref[...] += part

            @pl.when(k == nk - 1)
            def _():
                write(acc_ref[...])

    w_index = (lambda n, m, k: w_block(n, k)) if transpose_rhs else (lambda n, m, k: w_block(k, n))
    in_specs = [pl.BlockSpec((tm, tk), lambda n, m, k: (m, k)),
                pl.BlockSpec((None, tr, tc), w_index)]
    args = [a, wv.arr]
    for arr, off, width in aux_list:
        assert width == N and off % tn == 0
        in_specs.append(pl.BlockSpec((tm, tn), functools.partial(lambda n, m, k, blk: (m, blk + n), blk=off // tn)))
        args.append(arr)
    for v in vec_list:
        in_specs.append(pl.BlockSpec(v.shape, lambda n, m, k: (0, 0)))
        args.append(v)
    out_specs = [pl.BlockSpec((tm, tn), lambda n, m, k: (m, n)) for _ in out_dtypes]
    out_specs += [pl.BlockSpec((1, tn), lambda n, m, k: (0, 0)) for _ in post_accs]
    out_shape = [jax.ShapeDtypeStruct((T, N), dt) for dt in out_dtypes]
    out_shape += [jax.ShapeDtypeStruct((1, N), F32) for _ in post_accs]
    scratch = [pltpu.VMEM((tm, tn) if nk > 1 else (8, LANES), F32)]
    grid = (N // tn, T // tm, nk)
    sem = ("arbitrary",) * 3 if (n_acc or rider is not None) else ("parallel", "parallel", "arbitrary")
    params = pltpu.CompilerParams(dimension_semantics=sem, vmem_limit_bytes=VMEM_LIMIT_BYTES,
                                  has_side_effects=rider is not None)
    if rider is not None:
        r_in, r_out, r_sems = rider.specs()
        body = rider.wrap(body, len(in_specs), len(out_specs), 3)
        in_specs, out_specs, out_shape = in_specs + r_in, out_specs + r_out, out_shape + rider.out_shape
        scratch, args = scratch + r_sems, args + list(rider.ins)
    res = pl.pallas_call(body, name=name, grid=grid, in_specs=in_specs, out_specs=out_specs, out_shape=out_shape,
                         scratch_shapes=scratch, compiler_params=params)(*args)
    main = (list(res[:n_out]), list(res[n_out:n_out + n_acc])) if fused else res[0]
    return main if rider is None else (main, list(res[n_out + n_acc:]))


def matmul_tn(a, b, *, name, tk=2048, rider=None, into=None):
    T, M = a.shape
    T2, N = b.shape
    tk = min(tk, T)
    assert T == T2 and T % tk == 0
    if into is not None:
        assert rider is None
        buf, r0, c0, split = into
        rows_s, cols_s = (M // 4, N) if split == 0 else (M, N // 4)
        tm, tn = _tile(rows_s, 1024), _tile(cols_s, 1152)
        assert r0 % tm == 0 and c0 % tn == 0

        def part_block(m, n, k):
            if split == 0:
                return (m * tm) // rows_s, (r0 + (m * tm) % rows_s) // tm, c0 // tn + n
            return (n * tn) // cols_s, r0 // tm + m, (c0 + (n * tn) % cols_s) // tn

        def body_into(a_ref, b_ref, buf_ref, o_ref):
            k = pl.program_id(2)
            part = lax.dot_general(a_ref[...], b_ref[...], (((0,), (0,)), ((), ())), preferred_element_type=F32)

            @pl.when(k == 0)
            def _():
                o_ref[...] = part

            @pl.when(k > 0)
            def _():
                o_ref[...] += part

        return pl.pallas_call(
            body_into, name=name, grid=(M // tm, N // tn, T // tk),
            in_specs=[pl.BlockSpec((tk, tm), lambda m, n, k: (k, m)), pl.BlockSpec((tk, tn), lambda m, n, k: (k, n)),
                      pl.BlockSpec(memory_space=pl.ANY)],
            out_specs=pl.BlockSpec((None, tm, tn), part_block),
            out_shape=jax.ShapeDtypeStruct(buf.shape, buf.dtype), input_output_aliases={2: 0},
            compiler_params=_cparams(("parallel", "parallel", "arbitrary")))(a, b, buf)
    tm = _tile(M, 1024)
    tn = _tile(N, 1152)

    def body(a_ref, b_ref, o_ref):
        k = pl.program_id(2)
        part = lax.dot_general(a_ref[...], b_ref[...], (((0,), (0,)), ((), ())), preferred_element_type=F32)

        @pl.when(k == 0)
        def _():
            o_ref[...] = part

        @pl.when(k > 0)
        def _():
            o_ref[...] += part

    in_specs = [pl.BlockSpec((tk, tm), lambda m, n, k: (k, m)), pl.BlockSpec((tk, tn), lambda m, n, k: (k, n))]
    out_specs = [pl.BlockSpec((tm, tn), lambda m, n, k: (m, n))]
    out_shape = [jax.ShapeDtypeStruct((M, N), F32)]
    grid = (M // tm, N // tn, T // tk)
    if rider is None:
        return pl.pallas_call(body, name=name, grid=grid, in_specs=in_specs, out_specs=out_specs, out_shape=out_shape,
                              compiler_params=_cparams(("parallel", "parallel", "arbitrary")))(a, b)[0]
    r_in, r_out, r_sems = rider.specs()
    res = pl.pallas_call(
        rider.wrap(body, 2, 1, 3), name=name, grid=grid, in_specs=in_specs + r_in, out_specs=out_specs + r_out,
        out_shape=out_shape + rider.out_shape, scratch_shapes=r_sems,
        compiler_params=pltpu.CompilerParams(dimension_semantics=("arbitrary",) * 3,
                                             vmem_limit_bytes=VMEM_LIMIT_BYTES, has_side_effects=True),
    )(a, b, *rider.ins)
    return res[0], list(res[1:])


def rowwise(fn, rows, vecs, outs, accs=(), *, name, tm=512, rider=None):
    rows = [r if isinstance(r, tuple) else (r, 0, r.shape[1]) for r in rows]
    T = rows[0][0].shape[0]
    tm = min(tm, T)
    assert T % tm == 0
    n_rows, n_vecs, n_outs, n_accs = len(rows), len(vecs), len(outs), len(accs)

    def body(*refs):
        row_refs = refs[:n_rows]
        vec_refs = refs[n_rows:n_rows + n_vecs]
        out_refs = refs[n_rows + n_vecs:n_rows + n_vecs + n_outs]
        acc_refs = refs[n_rows + n_vecs + n_outs:]
        out_vals, acc_vals = fn([r[...] for r in row_refs], [v[...] for v in vec_refs])
        assert len(out_vals) == n_outs and len(acc_vals) == n_accs
        for r, val in zip(out_refs, out_vals):
            r[...] = val.astype(r.dtype)
        if n_accs:
            i = pl.program_id(0)

            @pl.when(i == 0)
            def _():
                for r in acc_refs:
                    r[...] = jnp.zeros_like(r)

            for r, val in zip(acc_refs, acc_vals):
                r[...] += val

    in_specs = []
    for arr, off, width in rows:
        assert off % width == 0
        in_specs.append(pl.BlockSpec((tm, width), functools.partial(lambda i, blk: (i, blk), blk=off // width)))
    for v in vecs:
        in_specs.append(pl.BlockSpec(v.shape, lambda i: (0, 0)))
    out_specs = [pl.BlockSpec((tm, w), lambda i: (i, 0)) for w, _ in outs]
    out_specs += [pl.BlockSpec((1, w), lambda i: (0, 0)) for w in accs]
    out_shape = [jax.ShapeDtypeStruct((T, w), dt) for w, dt in outs]
    out_shape += [jax.ShapeDtypeStruct((1, w), F32) for w in accs]
    args = [r[0] for r in rows] + list(vecs)
    if rider is None:
        res = pl.pallas_call(body, name=name, grid=(T // tm,), in_specs=in_specs, out_specs=out_specs,
                             out_shape=out_shape,
                             compiler_params=_cparams(("arbitrary",) if n_accs else ("parallel",)))(*args)
        return res[:n_outs], res[n_outs:]
    r_in, r_out, r_sems = rider.specs()
    res = pl.pallas_call(
        rider.wrap(body, len(in_specs), len(out_specs), 1), name=name, grid=(T // tm,), in_specs=in_specs + r_in,
        out_specs=out_specs + r_out, out_shape=out_shape + rider.out_shape, scratch_shapes=r_sems,
        compiler_params=pltpu.CompilerParams(dimension_semantics=("arbitrary",), vmem_limit_bytes=VMEM_LIMIT_BYTES,
                                             has_side_effects=True),
    )(*args, *rider.ins)
    return res[:n_outs], res[n_outs:n_outs + n_accs], list(res[n_outs + n_accs:])


def _colsum(x):
    return jnp.sum(x, axis=0, keepdims=True)


def _sigmoid(x):
    return 1.0 / (1.0 + jnp.exp(-x))


def _ln_stats(z):
    mu = jnp.mean(z, axis=-1, keepdims=True)
    zc = z - mu
    var = jnp.mean(zc * zc, axis=-1, keepdims=True)
    return zc * lax.rsqrt(var + LN_EPS)


def _ln_bwd(zhat_src, dy, g):
    mu = jnp.mean(zhat_src, axis=-1, keepdims=True)
    zc = zhat_src - mu
    var = jnp.mean(zc * zc, axis=-1, keepdims=True)
    rstd = lax.rsqrt(var + LN_EPS)
    zh = zc * rstd
    dzh = dy * g
    dz = rstd * (dzh - jnp.mean(dzh, axis=-1, keepdims=True) - zh * jnp.mean(dzh * zh, axis=-1, keepdims=True))
    return dz, _colsum(dy * zh), _colsum(dy)


def _hg_constants():
    r = np.arange(HG_TILE)
    same = (r[:, None] // HG_BLK) == (r[None, :] // HG_BLK)
    lower = (same & (r[None, :] <= r[:, None])).astype(np.float32)
    upper = (same & (r[None, :] >= r[:, None])).astype(np.float32)
    total = same.astype(np.float32)
    c = np.arange(2 * HG_DIM)
    bd = ((c[:, None] // HG_DIM) == (c[None, :] // HG_DIM)).astype(np.float32)
    pair_t = np.array([t for t, _ in _HG_PAIRS])
    pair_s = np.array([s for _, s in _HG_PAIRS])
    sel_t = (pair_t[None, :] == np.arange(HG_BLK)[:, None]).astype(np.float32)
    sel_s = (pair_s[None, :] == np.arange(HG_BLK)[:, None]).astype(np.float32)
    as_bf = lambda m: jnp.asarray(m, dtype=BF16)
    return as_bf(lower), as_bf(upper), as_bf(total), as_bf(bd), as_bf(sel_t), as_bf(sel_s)


_HG_HALF = HG_BLK // 2
_HG_PAIRS = ([(t, s) for t in range(_HG_HALF, HG_BLK) for s in range(HG_BLK)]
             + [(t, s) for t in range(_HG_HALF) for s in range(_HG_HALF)])
HG_STACK = len(_HG_PAIRS)
_HG_SLABS = ([((t - _HG_HALF) * HG_BLK, (t,), HG_BLK) for t in range(_HG_HALF, HG_BLK)]
             + [(_HG_HALF * HG_BLK + t * _HG_HALF, (t, t + 1), _HG_HALF) for t in range(0, _HG_HALF, 2)])


def _stack_by_s(x):
    return jnp.concatenate([x] * _HG_HALF + [x[:_HG_HALF]] * _HG_HALF, axis=0)


def _stack_by_t(x):
    w = x.shape[1]
    return jnp.concatenate([jnp.broadcast_to(x[t:t + 1], (HG_BLK, w)) for t in range(_HG_HALF, HG_BLK)]
                           + [jnp.broadcast_to(x[t:t + 1], (_HG_HALF, w)) for t in range(_HG_HALF)], axis=0)


def _keep_bf16_bits(x):
    bits = lax.bitcast_convert_type(x, jnp.int32) & jnp.int32(-65536)
    return lax.bitcast_convert_type(bits, F32)


def _head_sums(stack_ref, slot, bd):
    pair = bd.shape[0]
    return jnp.concatenate([jnp.dot(stack_ref[slot, :, c0:c0 + pair], bd, preferred_element_type=F32)
                            for c0 in range(0, stack_ref.shape[2], pair)], axis=1)


def _split3(x):
    hi = _keep_bf16_bits(x)
    r1 = x - hi
    mid = _keep_bf16_bits(r1)
    lo = _keep_bf16_bits(r1 - mid)
    return hi.astype(BF16), mid.astype(BF16), lo.astype(BF16)


def _dot3(m01, x):
    hi, mid, lo = _split3(x)
    d = lambda p: jnp.dot(m01, p, preferred_element_type=F32)
    return (d(lo) + d(mid)) + d(hi)


def _hg_prologue(hq, hf, lb, lower, total):
    sq = _sigmoid(hq)
    q = hq * sq
    sg = _sigmoid(hf)
    f = lb + (1.0 - lb) * sg
    g = jnp.log(f)
    k = 1.0 - f
    b = _dot3(lower, g)
    bl = _dot3(total, g)
    return q, k, f, sg, sq, b, bl


def _stack16(fn):
    return [fn(t) for t in range(HG_BLK)]


def hgrn2_fwd(proj, offs, lb, n_seq, seq, *, name, rider=None):
    T = n_seq * seq
    W = HG_HEADS * HG_DIM
    n_tiles = seq // HG_TILE
    nb = HG_TILE // HG_BLK
    lower, _, total, bd, sel_t, _ = _hg_constants()

    def body(hq_ref, hf_ref, hi_ref, lb_ref, lower_ref, total_ref, bd_ref, selt_ref,
             o_ref, st_out_ref,
             st_ref, q_s, k_s, v_s, b_s, qt_s, kt_s, d_s, p_s):
        @pl.when(pl.program_id(1) == 0)
        def _():
            st_ref[...] = jnp.zeros_like(st_ref)

        q, k, _, _, _, b, bl = _hg_prologue(hq_ref[...], hf_ref[...], lb_ref[...], lower_ref[...], total_ref[...])
        q_s[...] = q
        k_s[...] = k
        v_s[...] = hi_ref[...]
        b_s[...] = b
        qt_s[...] = q * jnp.exp(b)
        kt_s[...] = k * jnp.exp(jnp.minimum(bl - b, 0.0))
        d_s[...] = jnp.exp(bl)
        rowi = lax.broadcasted_iota(jnp.int32, (HG_BLK, W), 0)

        def block(i, slot):
            r0 = pl.multiple_of(i * HG_BLK, HG_BLK)
            rows = pl.ds(r0, HG_BLK)
            qi, ki, vi, bi = q_s[rows, :], k_s[rows, :], v_s[rows, :], b_s[rows, :]
            for off, ts, n in _HG_SLABS:
                slab = [jnp.where(rowi[:n] <= t, jnp.exp(jnp.minimum(bi[t:t + 1, :] - bi[:n], 0.0)), 0.0)
                        * qi[t:t + 1, :] * ki[:n] for t in ts]
                p_s[slot, pl.ds(off, HG_BLK), :] = jnp.concatenate(slab, axis=0).astype(BF16)
            a_b = _head_sums(p_s, slot, bd_ref[...])
            o_blk = jnp.dot(selt_ref[...], (a_b * _stack_by_s(vi)).astype(BF16), preferred_element_type=F32)
            qti, kti, di = qt_s[rows, :], kt_s[rows, :], d_s[rows, :]
            outs = []
            for h in range(HG_HEADS):
                hs = slice(h * HG_DIM, (h + 1) * HG_DIM)
                st_h = st_ref[hs, :]
                st_out_ref[i, hs, :] = st_h
                outs.append(lax.dot_general(qti[:, hs].astype(BF16), st_h.astype(BF16),
                                            (((1,), (1,)), ((), ())), preferred_element_type=F32))
                upd = lax.dot_general(vi[:, hs].astype(BF16), kti[:, hs].astype(BF16),
                                      (((0,), (0,)), ((), ())), preferred_element_type=F32)
                st_ref[hs, :] = st_h * di[0:1, hs] + upd
            o_ref[rows, :] = o_blk + jnp.concatenate(outs, axis=1)

        def some_blocks(jj, carry):
            for slot in range(HG_SLOTS):
                block(HG_SLOTS * jj + slot, slot)
            return carry

        lax.fori_loop(0, nb // HG_SLOTS, some_blocks, 0)

    col = lambda off: functools.partial(lambda s, t, blk: (s * n_tiles + t, blk), blk=off // W)
    const = lambda m: pl.BlockSpec(m.shape, lambda s, t: (0, 0))
    tile_f32 = pltpu.VMEM((HG_TILE, W), F32)
    in_specs = [pl.BlockSpec((HG_TILE, W), col(offs[0])), pl.BlockSpec((HG_TILE, W), col(offs[1])),
                pl.BlockSpec((HG_TILE, W), col(offs[2])), const(lb), const(lower), const(total), const(bd),
                const(sel_t)]
    out_specs = [pl.BlockSpec((HG_TILE, W), lambda s, t: (s * n_tiles + t, 0)),
                 pl.BlockSpec((nb, W, HG_DIM), lambda s, t: (s * n_tiles + t, 0, 0))]
    out_shape = [jax.ShapeDtypeStruct((T, W), F32), jax.ShapeDtypeStruct((T // HG_BLK, W, HG_DIM), F32)]
    scratch = [pltpu.VMEM((W, HG_DIM), F32)] + [tile_f32] * 7 + [pltpu.VMEM((HG_SLOTS, HG_STACK, W), BF16)]
    args = [proj, proj, proj, lb, lower, total, bd, sel_t]
    params = _cparams(("arbitrary", "arbitrary"))
    if rider is not None:
        r_in, r_out, r_sems = rider.specs()
        body = rider.wrap(body, len(in_specs), len(out_specs), 2)
        in_specs, out_specs, out_shape = in_specs + r_in, out_specs + r_out, out_shape + rider.out_shape
        scratch, args = scratch + r_sems, args + rider.ins
        params = pltpu.CompilerParams(dimension_semantics=("arbitrary", "arbitrary"),
                                      vmem_limit_bytes=VMEM_LIMIT_BYTES, has_side_effects=True)
    res = pl.pallas_call(body, name=name, grid=(n_seq, n_tiles), in_specs=in_specs, out_specs=out_specs,
                         out_shape=out_shape, scratch_shapes=scratch, compiler_params=params)(*args)
    return res[0], res[1], list(res[2:])


def hgrn2_bwd(proj, offs, lb, do, states, n_seq, seq, *, name):
    T = n_seq * seq
    W = HG_HEADS * HG_DIM
    n_tiles = seq // HG_TILE
    nb = HG_TILE // HG_BLK
    lower, upper, total, bd, sel_t, sel_s = _hg_constants()

    def body(hq_ref, hf_ref, hi_ref, do_ref, st_in_ref, lb_ref, lower_ref, upper_ref, total_ref, bd_ref,
             selt_ref, sels_ref,
             dhq_ref, dhf_ref, dhi_ref, dlb_ref,
             dst_ref, q_s, k_s, v_s, b_s, qt_s, kt_s, d_s, eb_s, ekb_s, dq_s, dk_s, db_s, dv_s,
             p_s, e_s, w_s):
        first = jnp.logical_and(pl.program_id(0) == 0, pl.program_id(1) == 0)

        @pl.when(first)
        def _():
            dlb_ref[...] = jnp.zeros_like(dlb_ref)

        @pl.when(pl.program_id(1) == 0)
        def _():
            dst_ref[...] = jnp.zeros_like(dst_ref)

        hq, lbv = hq_ref[...], lb_ref[...]
        q, k, f, sg, sq, b, bl = _hg_prologue(hq, hf_ref[...], lbv, lower_ref[...], total_ref[...])
        eb = jnp.exp(b)
        ekb = jnp.exp(jnp.minimum(bl - b, 0.0))
        q_s[...] = q
        k_s[...] = k
        v_s[...] = hi_ref[...]
        b_s[...] = b
        eb_s[...] = eb
        ekb_s[...] = ekb
        qt_s[...] = q * eb
        kt_s[...] = k * ekb
        d_s[...] = jnp.exp(bl)
        rowi = lax.broadcasted_iota(jnp.int32, (HG_BLK, W), 0)
        last_row = rowi == HG_BLK - 1

        def block(i, slot):
            r0 = pl.multiple_of(i * HG_BLK, HG_BLK)
            rows = pl.ds(r0, HG_BLK)
            qi, ki, vi, bi, doi = q_s[rows, :], k_s[rows, :], v_s[rows, :], b_s[rows, :], do_ref[rows, :]
            for off, ts, n in _HG_SLABS:
                es = [jnp.where(rowi[:n] <= t, jnp.exp(jnp.minimum(bi[t:t + 1, :] - bi[:n], 0.0)), 0.0) for t in ts]
                sl = pl.ds(off, HG_BLK)
                e_s[slot, sl, :] = jnp.concatenate(es, axis=0)
                p_s[slot, sl, :] = jnp.concatenate([e * qi[t:t + 1, :] * ki[:n] for e, t in zip(es, ts)],
                                                   axis=0).astype(BF16)
                w_s[slot, sl, :] = jnp.concatenate([doi[t:t + 1, :] * vi[:n] for t in ts], axis=0).astype(BF16)
            a_b = _head_sums(p_s, slot, bd_ref[...])
            da_b = _head_sums(w_s, slot, bd_ref[...])
            x = da_b * e_s[slot]
            dq_in = jnp.dot(selt_ref[...], (x * _stack_by_s(ki)).astype(BF16), preferred_element_type=F32)
            dk_in = jnp.dot(sels_ref[...], (x * _stack_by_t(qi)).astype(BF16), preferred_element_type=F32)
            dv_in = jnp.dot(sels_ref[...], (a_b * _stack_by_t(doi)).astype(BF16), preferred_element_type=F32)
            qti, kti, di = qt_s[rows, :], kt_s[rows, :], d_s[rows, :]
            dqt, dkt, dvt, dd = [], [], [], []
            for h in range(HG_HEADS):
                hs = slice(h * HG_DIM, (h + 1) * HG_DIM)
                st_h = st_in_ref[i, hs, :]
                dst_h = dst_ref[hs, :]
                do_h, v_h = doi[:, hs].astype(BF16), vi[:, hs].astype(BF16)
                dst_b = dst_h.astype(BF16)
                dqt.append(jnp.dot(do_h, st_h.astype(BF16), preferred_element_type=F32))
                dkt.append(jnp.dot(v_h, dst_b, preferred_element_type=F32))
                dvt.append(lax.dot_general(kti[:, hs].astype(BF16), dst_b, (((1,), (1,)), ((), ())),
                                           preferred_element_type=F32))
                dd.append(jnp.sum(dst_h * st_h, axis=0, keepdims=True))
                upd = lax.dot_general(do_h, qti[:, hs].astype(BF16), (((0,), (0,)), ((), ())),
                                      preferred_element_type=F32)
                dst_ref[hs, :] = dst_h * di[0:1, hs] + upd
            dqt = jnp.concatenate(dqt, axis=1)
            dkt = jnp.concatenate(dkt, axis=1)
            dvt = jnp.concatenate(dvt, axis=1)
            dd = jnp.concatenate(dd, axis=1)
            dbl = jnp.sum(dkt * kti, axis=0, keepdims=True) + dd * di[0:1, :]
            db = qi * dq_in - ki * dk_in + dqt * qti - dkt * kti
            db_s[rows, :] = db + jnp.where(last_row, dbl, 0.0)
            dq_s[rows, :] = dq_in + dqt * eb_s[rows, :]
            dk_s[rows, :] = dk_in + dkt * ekb_s[rows, :]
            dv_s[rows, :] = dv_in + dvt

        def some_blocks(jj, carry):
            for slot in range(HG_SLOTS):
                block(nb - 1 - slot - HG_SLOTS * jj, slot)
            return carry

        lax.fori_loop(0, nb // HG_SLOTS, some_blocks, 0)

        dg = _dot3(upper_ref[...], db_s[...])
        dhq_ref[...] = (dq_s[...] * (sq * (1.0 + hq * (1.0 - sq)))).astype(dhq_ref.dtype)
        df = dg / f - dk_s[...]
        dhf_ref[...] = (df * (1.0 - lbv) * (sg * (1.0 - sg))).astype(dhf_ref.dtype)
        dhi_ref[...] = dv_s[...].astype(dhi_ref.dtype)
        dlb_ref[...] += _colsum(df * (1.0 - sg))

    rev = lambda s, t: s * n_tiles + (n_tiles - 1 - t)
    col = lambda off: functools.partial(lambda s, t, blk: (rev(s, t), blk), blk=off // W)
    const = lambda m: pl.BlockSpec(m.shape, lambda s, t: (0, 0))
    row = pl.BlockSpec((HG_TILE, W), lambda s, t: (rev(s, t), 0))
    tile_f32 = pltpu.VMEM((HG_TILE, W), F32)
    n2 = HG_STACK
    return pl.pallas_call(
        body, name=name,
        grid=(n_seq, n_tiles),
        in_specs=[pl.BlockSpec((HG_TILE, W), col(offs[0])), pl.BlockSpec((HG_TILE, W), col(offs[1])),
                  pl.BlockSpec((HG_TILE, W), col(offs[2])), row,
                  pl.BlockSpec((nb, W, HG_DIM), lambda s, t: (rev(s, t), 0, 0)),
                  const(lb), const(lower), const(upper), const(total), const(bd), const(sel_t), const(sel_s)],
        out_specs=[row, row, row, pl.BlockSpec((1, W), lambda s, t: (0, 0))],
        out_shape=[jax.ShapeDtypeStruct((T, W), BF16)] * 3 + [jax.ShapeDtypeStruct((1, W), F32)],
        scratch_shapes=[pltpu.VMEM((W, HG_DIM), F32)] + [tile_f32] * 13
                       + [pltpu.VMEM((HG_SLOTS, n2, W), BF16), pltpu.VMEM((HG_SLOTS, n2, W), F32),
                          pltpu.VMEM((HG_SLOTS, n2, W), BF16)],
        compiler_params=_cparams(("arbitrary", "arbitrary")),
    )(proj, proj, proj, do, states, lb, lower, upper, total, bd, sel_t, sel_s)


def _diag_mask(tq):
    return lax.broadcasted_iota(jnp.int32, (tq, tq), 1) <= lax.broadcasted_iota(jnp.int32, (tq, tq), 0)


def _qk(q, k):
    return lax.dot_general(q, k, (((1,), (1,)), ((), ())), preferred_element_type=F32)


def _causal_pairs(n, sweeps=1, by_key=False):
    if by_key:
        rows = [(i, j, 0) for j in range(n) for i in range(j, n)]
    else:
        rows = [(i, j, s) for i in range(n) for s in range(sweeps) for j in range(i + 1)]
    return tuple(jnp.asarray(np.array([r[c] for r in rows], np.int32)) for c in range(3))


def _fox_placement(fh):
    hw, wa = fh * FOX_HDIM, fh * FOX_AUG
    pq, pk = np.zeros((hw, wa), np.float32), np.zeros((hw, wa), np.float32)
    aq, ak = np.zeros((3 * LANES, wa), np.float32), np.zeros((3 * LANES, wa), np.float32)
    oq, ok = np.zeros((1, wa), np.float32), np.zeros((1, wa), np.float32)
    for h in range(fh):
        src, dst = np.arange(h * FOX_HDIM, (h + 1) * FOX_HDIM), np.arange(h * FOX_AUG, h * FOX_AUG + FOX_HDIM)
        pq[src, dst] = FOX_HDIM ** -0.5
        pk[src, dst] = 1.0
        gate = h * FOX_AUG + FOX_HDIM
        for r in range(3):
            aq[r * LANES + h, gate + r] = 1.0
            ak[r * LANES + h, gate + 3 + r] = -1.0
        oq[0, gate + 3:gate + 6] = 1.0
        ok[0, gate:gate + 3] = 1.0
    bf = lambda m: jnp.asarray(m, dtype=BF16)
    return {"pq": bf(pq), "pk": bf(pk), "aq": bf(aq), "ak": bf(ak), "oq": jnp.asarray(oq), "ok": jnp.asarray(ok),
            "pqt": bf(pq.T), "pkt": bf(pk.T)}


def _fox_specs(tq, fh, heads=1):
    groups = fh // heads

    def spec(tab):
        return pl.BlockSpec((None, tq, heads * FOX_AUG), lambda b, t, *tabs: (b // groups, tabs[tab][t], b % groups))
    return spec(0), spec(1)


def fox_fwd(qa, ka, va, *, name):
    n_seq, S, width = qa.shape
    fh = width // FOX_AUG
    nh = FOX_FWD_HEADS
    BH = n_seq * fh // nh
    tq = min(FOX_TQ, S)
    itab, jtab, _ = _causal_pairs(S // tq)

    def body(itab_ref, jtab_ref, q_ref, k_ref, v_ref, o_ref, ox_ref, lse_ref, *scratch):
        t = pl.program_id(1)
        i, j = itab_ref[t], jtab_ref[t]
        per_head = [scratch[4 * h:4 * h + 4] for h in range(nh)]

        @pl.when(j == 0)
        def _():
            for m_s, l_s, acc_s, acc_lo_s in per_head:
                m_s[...] = jnp.full_like(m_s, NEG_INF)
                l_s[...] = jnp.zeros_like(l_s)
                acc_s[...] = jnp.zeros_like(acc_s)
                acc_lo_s[...] = jnp.zeros_like(acc_lo_s)

        def step(on_diagonal):
            for h, (m_s, l_s, acc_s, acc_lo_s) in enumerate(per_head):
                lanes = slice(h * FOX_AUG, (h + 1) * FOX_AUG)
                s = _qk(q_ref[:, lanes], k_ref[:, lanes])
                if on_diagonal:
                    s = jnp.where(_diag_mask(tq), s, NEG_INF)
                m_prev = m_s[...]
                m_new = jnp.maximum(m_prev, jnp.max(s, axis=-1, keepdims=True))
                alpha = jnp.exp(m_prev - m_new)
                p = jnp.exp(s - m_new[:, 0:1])
                p_hi = p.astype(BF16)
                p_lo = (p - p_hi.astype(F32)).astype(BF16)
                v = v_ref[:, lanes]
                l_s[...] = alpha * l_s[...] + jnp.sum(p, axis=-1, keepdims=True)
                acc_s[...] = alpha * acc_s[...] + jnp.dot(p_hi, v, preferred_element_type=F32)
                acc_lo_s[...] = alpha * acc_lo_s[...] + jnp.dot(p_lo, v, preferred_element_type=F32)
                m_s[...] = m_new

        @pl.when(j < i)
        def _():
            step(False)

        @pl.when(j == i)
        def _():
            step(True)
            for h, (m_s, l_s, acc_s, acc_lo_s) in enumerate(per_head):
                lanes = slice(h * FOX_AUG, (h + 1) * FOX_AUG)
                inv_l = 1.0 / l_s[...]
                o_ref[:, lanes] = (acc_s[...] * inv_l).astype(o_ref.dtype)
                ox_ref[:, lanes] = (acc_s[...] + acc_lo_s[...]) * inv_l
                lse_ref[:, lanes] = m_s[...] + jnp.log(l_s[...])

    qspec, kspec = _fox_specs(tq, fh, nh)
    wide = jax.ShapeDtypeStruct((n_seq, S, width), F32)
    return pl.pallas_call(
        body, name=name,
        grid_spec=pltpu.PrefetchScalarGridSpec(
            num_scalar_prefetch=2, grid=(BH, itab.shape[0]),
            in_specs=[qspec, kspec, kspec],
            out_specs=[qspec, qspec, qspec],
            scratch_shapes=[pltpu.VMEM((tq, LANES), F32)] * (4 * nh)),
        out_shape=[jax.ShapeDtypeStruct((n_seq, S, width), BF16), wide, wide],
        compiler_params=_cparams(("parallel", "arbitrary")),
    )(itab, jtab, qa, ka, va)


def _fox_ds(q, k, v, do, ox, lse, on_diagonal):
    s = _qk(q, k)
    if on_diagonal:
        s = jnp.where(_diag_mask(s.shape[0]), s, NEG_INF)
    p = jnp.exp(s - lse[:, 0:1])
    delta = jnp.sum(do.astype(F32) * ox, axis=-1, keepdims=True)
    return p, p * (_qk(do, v) - delta)


def fox_bwd(qa, ka, va, do, ox, lse, *, name):
    n_seq, S, width = qa.shape
    fh = width // FOX_AUG
    BH = n_seq * fh
    tq = min(FOX_TQ, S)
    itab, jtab, _ = _causal_pairs(S // tq)

    def body(itab_ref, jtab_ref, q_ref, k_ref, v_ref, do_ref, ox_ref, lse_ref, dq_ref, dk_ref, dv_ref, dsum_ref):
        t = pl.program_id(1)
        i, j = itab_ref[t], jtab_ref[t]

        @pl.when(t == 0)
        def _():
            dq_ref[...] = jnp.zeros_like(dq_ref)
            dk_ref[...] = jnp.zeros_like(dk_ref)
            dv_ref[...] = jnp.zeros_like(dv_ref)
            dsum_ref[...] = jnp.zeros_like(dsum_ref)

        q_rows = pl.ds(pl.multiple_of(i * tq, tq), tq)
        k_rows = pl.ds(pl.multiple_of(j * tq, tq), tq)

        def step(on_diagonal):
            q, k, do = q_ref[...], k_ref[...], do_ref[...]
            p, ds = _fox_ds(q, k, v_ref[...], do, ox_ref[...], lse_ref[...], on_diagonal)
            ds_b = ds.astype(BF16)
            tn = (((0,), (0,)), ((), ()))
            dq_ref[q_rows, :] += jnp.dot(ds_b, k, preferred_element_type=F32)
            dk_ref[k_rows, :] += lax.dot_general(ds_b, q, tn, preferred_element_type=F32)
            dv_ref[k_rows, :] += lax.dot_general(p.astype(BF16), do, tn, preferred_element_type=F32)
            dsum_ref[:, k_rows] += _colsum(ds)

        @pl.when(j < i)
        def _():
            step(False)

        @pl.when(j == i)
        def _():
            step(True)

    qspec, kspec = _fox_specs(tq, fh)
    whole = pl.BlockSpec((None, S, FOX_AUG), lambda b, t, it, jt: (b // fh, 0, b % fh))
    wide = jax.ShapeDtypeStruct((n_seq, S, width), F32)
    return pl.pallas_call(
        body, name=name,
        grid_spec=pltpu.PrefetchScalarGridSpec(
            num_scalar_prefetch=2, grid=(BH, itab.shape[0]),
            in_specs=[qspec, kspec, kspec, qspec, qspec, qspec],
            out_specs=[whole, whole, whole, pl.BlockSpec((None, 1, S), lambda b, t, it, jt: (b, 0, 0))]),
        out_shape=[wide, wide, wide, jax.ShapeDtypeStruct((BH, 1, S), F32)],
        compiler_params=_cparams(("parallel", "arbitrary")),
    )(itab, jtab, qa, ka, va, do, ox, lse)


def seq_cumsum(x, n_seq, seq, *, reverse, name):
    T, C = x.shape
    tb = min(256, seq)
    n = seq // tb
    r = np.arange(tb)
    tri = (r[None, :] >= r[:, None]) if reverse else (r[None, :] <= r[:, None])
    tri = jnp.asarray(tri.astype(np.float32), dtype=BF16)

    def body(x_ref, tri_ref, o_ref, carry_s):
        @pl.when(pl.program_id(1) == 0)
        def _():
            carry_s[...] = jnp.zeros_like(carry_s)

        xv = x_ref[...]
        o_ref[...] = _dot3(tri_ref[...], xv) + carry_s[...]
        carry_s[...] += _colsum(xv)

    blk = (lambda s, t: (s * n + (n - 1 - t), 0)) if reverse else (lambda s, t: (s * n + t, 0))
    return pl.pallas_call(
        body, name=name,
        grid=(n_seq, n),
        in_specs=[pl.BlockSpec((tb, C), blk), pl.BlockSpec((tb, tb), lambda s, t: (0, 0))],
        out_specs=pl.BlockSpec((tb, C), blk),
        out_shape=jax.ShapeDtypeStruct((T, C), F32),
        scratch_shapes=[pltpu.VMEM((1, C), F32)],
        compiler_params=_cparams(("arbitrary", "arbitrary")),
    )(x, tri)


def _place():
    return lax.axis_index("x"), lax.axis_index("y"), lax.axis_index("c")


def _other_chips(x, y):
    return [(1 - x, y), (x, 1 - y), (1 - x, 1 - y)]


def _hbm_call(body, ins, out_shape, n_sems, *, name):
    hbm = pl.BlockSpec(memory_space=pl.ANY)
    return pl.pallas_call(
        body, name=name,
        in_specs=[hbm] * len(ins), out_specs=[hbm] * len(out_shape), out_shape=out_shape,
        scratch_shapes=[pltpu.SemaphoreType.DMA((n_sems,)), pltpu.SemaphoreType.DMA((n_sems,)),
                        pltpu.SemaphoreType.DMA((len(ins),))],
        compiler_params=pltpu.CompilerParams(has_side_effects=True),
    )(*ins)


def allgather_chips(shards, *, name):
    return _exchange_call(allgather_rider(shards), name=name)


def _allgather_ops(x_refs, o_refs, send_sems, recv_sems, local_sems):
    def copies():
        x, y, c = _place()
        me = 2 * x + y
        chips = _other_chips(x, y)
        own, first, passed, landed, handed = [], [], [], [], []
        for b, (x_ref, o_ref) in enumerate(zip(x_refs, o_refs)):
            half = x_ref.shape[0] // 2
            mine, theirs = pl.ds(c * half, half), pl.ds((1 - c) * half, half)
            own.append(pltpu.make_async_copy(x_ref, o_ref.at[me], local_sems.at[b]))

            def copy(k, src, chip, rows, to, o_ref=o_ref, b=b):
                return pltpu.make_async_remote_copy(src_ref=src, dst_ref=o_ref.at[2 * chip[0] + chip[1], rows],
                                                    send_sem=send_sems.at[6 * b + k], recv_sem=recv_sems.at[6 * b + k],
                                                    device_id=to, device_id_type=MESH)
            for j, chip in enumerate(chips):
                first.append(copy(j, x_ref.at[mine], (x, y), mine, (*chip, c)))
                landed.append(copy(j, x_ref.at[mine], chip, mine, (*chip, c)))
                passed.append(copy(3 + j, o_ref.at[2 * chip[0] + chip[1], mine], chip, mine, (x, y, 1 - c)))
                handed.append(copy(3 + j, x_ref.at[mine], chip, theirs, (x, y, 1 - c)))
        return own, first, passed, landed, handed

    def start():
        own, first, _, _, _ = copies()
        for cp in own + first:
            cp.start()

    def finish():
        own, first, passed, landed, handed = copies()
        for arrived, forward in zip(landed, passed):
            arrived.wait_recv()
            forward.start()
        for cp in handed:
            cp.wait_recv()
        for cp in first + passed:
            cp.wait_send()
        for cp in own:
            cp.wait()
    return start, finish


def _scatter_ops(x_refs, o_refs, send_sems, recv_sems, local_sems):
    def copies():
        x, y, c = _place()
        return [pltpu.make_async_remote_copy(
            src_ref=x_ref.at[2 * px + py], dst_ref=o_ref.at[j], send_sem=send_sems.at[3 * b + j],
            recv_sem=recv_sems.at[3 * b + j], device_id=(px, py, c), device_id_type=MESH)
            for b, (x_ref, o_ref) in enumerate(zip(x_refs, o_refs)) for j, (px, py) in enumerate(_other_chips(x, y))]

    def start():
        for cp in copies():
            cp.start()

    def finish():
        sends = copies()
        for cp in sends:
            cp.wait_recv()
        for cp in sends:
            cp.wait_send()
    return start, finish


class Rider(NamedTuple):
    ins: list
    out_shape: list
    n_sems: int
    ops: object

    def specs(self):
        hbm = pl.BlockSpec(memory_space=pl.ANY)
        sems = [pltpu.SemaphoreType.DMA((self.n_sems,)), pltpu.SemaphoreType.DMA((self.n_sems,)),
                pltpu.SemaphoreType.DMA((len(self.ins),))]
        return [hbm] * len(self.ins), [hbm] * len(self.out_shape), sems

    def wrap(self, body, n_in, n_out, grid_rank):
        k_in, k_out = len(self.ins), len(self.out_shape)

        def carried(*refs):
            ins, r_ins = refs[:n_in], refs[n_in:n_in + k_in]
            outs = refs[n_in + k_in:n_in + k_in + n_out]
            r_outs = refs[n_in + k_in + n_out:n_in + k_in + n_out + k_out]
            scratch, sems = refs[n_in + k_in + n_out + k_out:-3], refs[-3:]
            first = functools.reduce(jnp.logical_and, [pl.program_id(a) == 0 for a in range(grid_rank)])
            last = functools.reduce(jnp.logical_and,
                                    [pl.program_id(a) == pl.num_programs(a) - 1 for a in range(grid_rank)])
            pl.when(first)(lambda: self.ops(r_ins, r_outs, *sems)[0]())
            body(*ins, *outs, *scratch)
            pl.when(last)(lambda: self.ops(r_ins, r_outs, *sems)[1]())
        return carried


def _exchange_call(rider, *, name):
    def body(*refs):
        k = len(rider.ins)
        start, finish = rider.ops(refs[:k], refs[k:k + len(rider.out_shape)], *refs[-3:])
        start()
        finish()
    in_specs, out_specs, sems = rider.specs()
    return pl.pallas_call(body, name=name, in_specs=in_specs, out_specs=out_specs, out_shape=rider.out_shape,
                          scratch_shapes=sems, compiler_params=pltpu.CompilerParams(has_side_effects=True))(*rider.ins)


def allgather_rider(shards):
    assert all(s.shape[0] % (2 * ROW_ALIGN) == 0 for s in shards)
    return Rider(list(shards), [jax.ShapeDtypeStruct((4,) + s.shape, s.dtype) for s in shards], 6 * len(shards),
                 _allgather_ops)


def scatter_rider(parts):
    return Rider(list(parts), [jax.ShapeDtypeStruct((3,) + p.shape[1:], p.dtype) for p in parts], 3 * len(parts),
                 _scatter_ops)


def scatter_chips(parts, *, name):
    return _exchange_call(scatter_rider(parts), name=name)


def swap_cores(vs, *, name):
    nb = len(vs)

    def body(*refs):
        x_refs, o_refs = refs[:nb], refs[nb:2 * nb]
        send_sems, recv_sems, _ = refs[2 * nb:]
        x, y, c = _place()
        copies = [pltpu.make_async_remote_copy(src_ref=x_ref, dst_ref=o_ref, send_sem=send_sems.at[b],
                                               recv_sem=recv_sems.at[b], device_id=(x, y, 1 - c), device_id_type=MESH)
                  for b, (x_ref, o_ref) in enumerate(zip(x_refs, o_refs))]
        for cp in copies:
            cp.start()
        for cp in copies:
            cp.wait()

    return _hbm_call(body, vs, [jax.ShapeDtypeStruct(v.shape, v.dtype) for v in vs], nb, name=name)


def allreduce_small(v, *, name):
    R, C = v.shape

    def body(x_ref, o_ref, gath_ref, send_sems, recv_sems):
        x, y, c = _place()
        me = 4 * x + 2 * y + c
        gath_ref[me] = x_ref[...]
        flips = [(k >> 2 & 1, k >> 1 & 1, k & 1) for k in range(1, 8)]
        sends = []
        for j, (fx, fy, fc) in enumerate(flips):
            peer = (x ^ fx, y ^ fy, c ^ fc)
            cp = pltpu.make_async_remote_copy(src_ref=x_ref, dst_ref=gath_ref.at[me], send_sem=send_sems.at[j],
                                              recv_sem=recv_sems.at[j], device_id=peer, device_id_type=MESH)
            cp.start()
            sends.append(cp)
        for j, (fx, fy, fc) in enumerate(flips):
            peer = (x ^ fx, y ^ fy, c ^ fc)
            pltpu.make_async_remote_copy(src_ref=x_ref, dst_ref=gath_ref.at[4 * peer[0] + 2 * peer[1] + peer[2]],
                                         send_sem=send_sems.at[j], recv_sem=recv_sems.at[j], device_id=peer,
                                         device_id_type=MESH).wait_recv()
        for cp in sends:
            cp.wait_send()
        total = gath_ref[0]
        for d in range(1, 8):
            total = total + gath_ref[d]
        o_ref[...] = total

    vm = pl.BlockSpec(memory_space=pltpu.VMEM)
    out, _ = pl.pallas_call(
        body, name=name,
        in_specs=[vm], out_specs=[vm, vm],
        out_shape=[jax.ShapeDtypeStruct((R, C), F32), jax.ShapeDtypeStruct((8, R, C), F32)],
        scratch_shapes=[pltpu.SemaphoreType.DMA((7,)), pltpu.SemaphoreType.DMA((7,))],
        compiler_params=pltpu.CompilerParams(has_side_effects=True),
    )(v)
    return out


ROW_ALIGN = 16
PACK_W = 1024
SUM_TILE = 512
BIG_WEIGHTS = (("w_in", 1), ("w_a", 1), ("w_b", 1), ("w_o", 0), ("w_ff1", 1), ("w_ff2", 0), ("w_pg", 0), ("w_p", 1))


def _b_layout(d, ple):
    hw, q = d // 2, d // 4
    small = 2 * d + 2 * q
    lay = {"w_ff1": (0, 0, d, d), "w_ff2": (d, 0, d, d), "w_o": (2 * d, 0, q, d), "w_pg": (2 * d + q, 0, q, d),
           "w_a": (small, 0, hw, q), "w_b": (small, q, hw, q), "w_p": (small, 2 * q, ple, q)}
    return lay, small + hw


def pack_a(w_in_shard):
    rows, cols = w_in_shard.shape
    pad = -cols % LANES
    return jnp.concatenate([w_in_shard, jnp.zeros((rows, pad), w_in_shard.dtype)], axis=1)


def pack_b(shards, d):
    hw, q = d // 2, d // 4
    dt = shards["w_a"].dtype
    wp = shards["w_p"]
    wp = jnp.concatenate([wp, jnp.zeros((hw - wp.shape[0], q), dt)], axis=0)
    small = jnp.concatenate([shards["w_a"], shards["w_b"], wp, jnp.zeros((hw, d - 3 * q), dt)], axis=1)
    return jnp.concatenate([shards["w_ff1"], shards["w_ff2"], shards["w_o"], shards["w_pg"], small], axis=0)


def unpack_b(buf, lay):
    return {nm: buf[r0:r0 + rows, c0:c0 + cols] for nm, (r0, c0, rows, cols) in lay.items()}


def _win_layout(d):
    hw = d // 2
    fh = hw // FOX_HDIM
    orig = {"hq": (0, hw), "hf": (hw, hw), "hi": (2 * hw, hw), "hg": (3 * hw, hw), "fq": (4 * hw, hw),
            "fk": (5 * hw, hw), "fv": (6 * hw, hw), "ff": (7 * hw, fh), "ga": (7 * hw + fh, d), "gb": (7 * hw + fh + d, d)}
    order = ["ga", "gb", "hq", "hf", "hi", "hg", "fq", "fk", "fv", "ff"]
    mine, off = {}, 0
    for nm in order:
        width = orig[nm][1] if nm != "ff" else LANES
        mine[nm] = (off, width)
        off += width
    return orig, order, mine, off


def _adam_fn(rows, vecs):
    w, g, m, v = rows
    m2 = ADAM_B1 * m + (1.0 - ADAM_B1) * g
    v2 = ADAM_B2 * v + (1.0 - ADAM_B2) * (g * g)
    m_hat = m2 / (1.0 - ADAM_B1 ** ADAM_STEP)
    v_hat = v2 / (1.0 - ADAM_B2 ** ADAM_STEP)
    delta = -ADAM_LR * (m_hat / (jnp.sqrt(v_hat) + ADAM_EPS) + ADAM_WD * w)
    return [delta, m2, v2], []


def adamw_small(small, p0, ws, ms, vs, *, name):
    n = len(ws)
    hw = p0.shape[1]
    fh = ws[8].shape[1]

    def body(small_ref, p0_ref, *refs):
        w_refs, m_refs, v_refs = refs[:n], refs[n:2 * n], refs[2 * n:3 * n]
        g_out, d_out, m_out, v_out = (refs[(3 + k) * n:(4 + k) * n] for k in range(4))
        sm = small_ref[...]
        p = p0_ref[...]
        d_lb = sm[6:7, hw:2 * hw] * (p * (1.0 - p))
        grads = [sm[r:r + 1, :] for r in range(6)]
        grads += [jnp.concatenate([d_lb, -d_lb], axis=0), sm[6:7, :hw], sm[7:8, :fh]]
        for i in range(n):
            (delta, m2, v2), _ = _adam_fn([w_refs[i][...], grads[i], m_refs[i][...], v_refs[i][...]], [])
            g_out[i][...], d_out[i][...], m_out[i][...], v_out[i][...] = grads[i], delta, m2, v2

    shapes = [jax.ShapeDtypeStruct(w.shape, F32) for w in ws]
    return pl.pallas_call(body, name=name, out_shape=shapes * 4)(small, p0, *ws, *ms, *vs)


def adamw(w, g, m, v, *, name):
    c = w.shape[1]
    (delta, m2, v2), _ = rowwise(_adam_fn, [w, g, m, v], [], [(c, F32)] * 3, name=name, tm=256)
    return delta, m2, v2


def kernel(x, p, ln0_g, ln0_b, w_in, hg_lb, hg_norm_g, fox_fb, w_a, w_b, w_o, ln1_g, ln1_b, w_ff1, w_ff2, w_pg, w_p, ln2_g, ln2_b, loss_target, m_ln0_g, m_ln0_b, m_w_in, m_hg_lb, m_hg_norm_g, m_fox_fb, m_w_a, m_w_b, m_w_o, m_ln1_g, m_ln1_b, m_w_ff1, m_w_ff2, m_w_pg, m_w_p, m_ln2_g, m_ln2_b, v_ln0_g, v_ln0_b, v_w_in, v_hg_lb, v_hg_norm_g, v_fox_fb, v_w_a, v_w_b, v_w_o, v_ln1_g, v_ln1_b, v_w_ff1, v_w_ff2, v_w_pg, v_w_p, v_ln2_g, v_ln2_b):
    n_seq, seq, d = x.shape
    T = n_seq * seq
    hw = d // 2
    fh = hw // FOX_HDIM
    bh = n_seq * fh
    orig, order, mine, n_in = _win_layout(d)

    big = {"w_in": w_in[0], "w_a": w_a[0], "w_b": w_b[0], "w_o": w_o[0], "w_ff1": w_ff1[0], "w_ff2": w_ff2[0],
           "w_pg": w_pg[0], "w_p": w_p[0]}
    big_m = {"w_in": m_w_in[0], "w_a": m_w_a[0], "w_b": m_w_b[0], "w_o": m_w_o[0], "w_ff1": m_w_ff1[0],
             "w_ff2": m_w_ff2[0], "w_pg": m_w_pg[0], "w_p": m_w_p[0]}
    big_v = {"w_in": v_w_in[0], "w_a": v_w_a[0], "w_b": v_w_b[0], "w_o": v_w_o[0], "w_ff1": v_w_ff1[0],
             "w_ff2": v_w_ff2[0], "w_pg": v_w_pg[0], "w_p": v_w_p[0]}
    names = [nm for nm, _ in BIG_WEIGHTS]
    axis = dict(BIG_WEIGHTS)
    ple = w_p.shape[1]
    lay, b_rows = _b_layout(d, ple)
    in_cols = big["w_in"].shape[1]

    gather_w_in = allgather_rider([pack_a(big["w_in"].astype(BF16))])
    gather_rest = allgather_rider([pack_b({nm: big[nm].astype(BF16) for nm in names if nm != "w_in"}, d)])

    x2 = x.reshape(T, d)
    tgt = loss_target.reshape(T, d)
    p_b = p.reshape(T, p.shape[-1]).astype(BF16)
    vec = lambda a: a.reshape(1, -1)
    probs = jax.nn.softmax(hg_lb, axis=0)
    lb = vec(probs[0])

    def ln0_fn(rows, vecs):
        h = _ln_stats(rows[0]) * vecs[0] + vecs[1]
        return [h, h], []
    (h0, h0b), _, (a_all,) = rowwise(ln0_fn, [x2], [vec(ln0_g), vec(ln0_b)], [(d, F32), (d, BF16)], name="ln0_fwd",
                                     rider=gather_w_in)
    win = jnp.concatenate([a_all[s, :, :in_cols] for s in range(4)], axis=1)
    win_mine = jnp.concatenate(
        [win[:, orig[nm][0]:orig[nm][0] + orig[nm][1]] for nm in order]
        + [jnp.zeros((d, LANES - fh), BF16)], axis=1)
    proj = matmul_nn(h0b, win_mine, name="in_proj")

    o_raw, hg_states, (b_all,) = hgrn2_fwd(proj, [mine["hq"][0], mine["hf"][0], mine["hi"][0]], lb, n_seq, seq,
                                           name="hgrn2_fwd", rider=gather_rest)
    view = lambda nm, k, n: WView(b_all, lay[nm][0], lay[nm][1], k, n, axis[nm])
    w_ff1_v, w_ff2_v = view("w_ff1", d, 4 * d), view("w_ff2", 4 * d, d)

    def whole(nm):
        r0, c0, rows, cols = lay[nm]
        return jnp.concatenate([b_all[s, r0:r0 + rows, c0:c0 + cols] for s in range(4)], axis=axis[nm])
    w_o_v, w_pg_v, w_a_v, w_p_v, w_b_full = whole("w_o"), whole("w_pg"), whole("w_a"), whole("w_p"), whole("w_b")

    def ya_fn(rows, vecs):
        o, hg = rows
        outs = []
        for h in range(HG_HEADS):
            oh = o[:, h * HG_DIM:(h + 1) * HG_DIM]
            outs.append(oh * lax.rsqrt(jnp.mean(oh * oh, axis=-1, keepdims=True) + RMS_EPS))
        y = jnp.concatenate(outs, axis=1) * vecs[0] * (hg * _sigmoid(hg))
        return [y], []
    (y_a,), _ = rowwise(ya_fn, [o_raw, (proj,) + mine["hg"]], [hg_norm_g], [(hw, BF16)], name="hgrn2_out_fwd")

    fb_pad = jnp.concatenate([fox_fb, jnp.zeros((1, LANES - fh), F32)], axis=1)

    def lf_fn(rows, vecs):
        u = rows[0] + vecs[0]
        return [jnp.minimum(u, 0.0) - jnp.log(1.0 + jnp.exp(-jnp.abs(u)))], []
    (lf,), _ = rowwise(lf_fn, [(proj,) + mine["ff"]], [fb_pad], [(LANES, F32)], name="fox_logf")
    c_cum = seq_cumsum(lf, n_seq, seq, reverse=False, name="fox_cumsum")

    place = _fox_placement(fh)

    def prep_fn(rows, vecs):
        fq_, fk_, fv_, cc = rows
        pq, pk, aq, ak, oq, ok = vecs
        parts = jnp.concatenate(_split3(cc), axis=1)
        mm = lambda a_, b_: jnp.dot(a_, b_, preferred_element_type=F32)
        q_ = mm(fq_.astype(BF16), pq) + mm(parts, aq) + oq
        k_ = mm(fk_.astype(BF16), pk) + mm(parts, ak) + ok
        return [q_, k_, mm(fv_.astype(BF16), pk)], []
    wa = fh * FOX_AUG
    (qa, ka, va), _ = rowwise(prep_fn, [(proj,) + mine["fq"], (proj,) + mine["fk"], (proj,) + mine["fv"], c_cum],
                              [place[nm] for nm in ("pq", "pk", "aq", "ak", "oq", "ok")], [(wa, BF16)] * 3,
                              name="fox_prep")
    as_seq = lambda t2d: t2d.reshape(n_seq, seq, t2d.shape[1])
    o_fox, ox_fox, lse = fox_fwd(as_seq(qa), as_seq(ka), as_seq(va), name="fox_fwd")
    y_b = o_fox.reshape(T, wa)
    wb_pad = jnp.concatenate([w_b_full.reshape(fh, FOX_HDIM, d), jnp.zeros((fh, FOX_AUG - FOX_HDIM, d), BF16)],
                             axis=1).reshape(wa, d)

    fused_tm = 512
    pa = matmul_nn(y_a, w_a_v, name="proj_a")

    def merge_post(pb_, aux, vecs):
        ga, gb, a = aux
        return [_sigmoid(ga) * a + _sigmoid(gb) * pb_, pb_], []
    (merged, pb), _ = matmul_nn(y_b, wb_pad, name="proj_b_merge", tm=fused_tm, post=merge_post,
                                post_aux=[(proj,) + mine["ga"], (proj,) + mine["gb"], pa], post_outs=[BF16, F32])

    def ln1_post(mix, aux, vecs):
        z = ALPHA * aux[0] + mix
        h = _ln_stats(z) * vecs[0] + vecs[1]
        return [z, h, h], []
    (z1, h1, h1b), _ = matmul_nn(merged, w_o_v, name="out_proj_ln1", tm=fused_tm, post=ln1_post, post_aux=[h0],
                                 post_vecs=[ln1_g, ln1_b], post_outs=[F32, F32, BF16])

    relu2 = lambda u: jnp.square(jnp.maximum(u, 0.0))
    act = matmul_nn(h1b, w_ff1_v, name="ff1", out_dtype=BF16, epilogue=relu2)
    pg = matmul_nn(h1b, w_pg_v, name="ple_gate")
    pe = matmul_nn(p_b, w_p_v, name="ple_embed")

    def head_post(ffv, aux, vecs):
        h1v, pgv, pev, t = aux
        g2, b2 = vecs
        sp = _sigmoid(pgv)
        z = ALPHA * h1v + ffv + sp * pev
        y = _ln_stats(z) * g2 + b2
        err = y - t
        loss_rows = 0.5 * jnp.mean(err * err, axis=-1, keepdims=True)
        dy = err * (1.0 / d)
        dz, dg2, db2 = _ln_bwd(z, dy, g2)
        loss_acc = jnp.broadcast_to(_colsum(loss_rows), (1, d))
        return [dz, dz, dz * pev * (sp * (1.0 - sp)), dz * sp], [dg2, db2, loss_acc]
    (dz2, dz2b, dpg, dpe), (g_ln2_g, g_ln2_b, loss_part) = matmul_nn(
        act, w_ff2_v, name="ff2_head", tm=fused_tm, post=head_post, post_aux=[h1, pg, pe, tgt],
        post_vecs=[ln2_g, ln2_b], post_outs=[F32, BF16, BF16, BF16], post_accs=[d, d, d])

    dact = lambda da, a: da * (2.0 * jnp.sqrt(a.astype(F32)))
    du = matmul_nn(dz2b, w_ff2_v, transpose_rhs=True, name="d_ff2", out_dtype=BF16, epilogue=dact, aux=act)
    dh1_pg = matmul_nn(dpg, w_pg_v, transpose_rhs=True, name="d_ple_gate")

    def ln1_bwd_post(dh1_ff, aux, vecs):
        dh1 = ALPHA * aux[0] + dh1_ff + aux[1]
        dz, dg, db = _ln_bwd(aux[2], dh1, vecs[0])
        return [dz, dz], [dg, db]
    (dz1, dz1b), (g_ln1_g, g_ln1_b) = matmul_nn(
        du, w_ff1_v, transpose_rhs=True, name="d_ff1_ln1", tm=fused_tm, post=ln1_bwd_post, post_aux=[dz2, dh1_pg, z1],
        post_vecs=[ln1_g], post_outs=[F32, BF16], post_accs=[d, d])

    def merge_bwd_post(dm, aux, vecs):
        ga, gb, a, b = aux
        sa, sb = _sigmoid(ga), _sigmoid(gb)
        return [dm * a * (sa * (1.0 - sa)), dm * b * (sb * (1.0 - sb)), dm * sa, dm * sb], []
    (dga, dgb, dma, dmb), _ = matmul_nn(
        dz1b, w_o_v, transpose_rhs=True, name="d_out_proj_merge", tm=fused_tm, post=merge_bwd_post,
        post_aux=[(proj,) + mine["ga"], (proj,) + mine["gb"], pa, pb], post_outs=[BF16] * 4)
    dya = matmul_nn(dma, w_a_v, transpose_rhs=True, name="d_proj_a")
    dyb = matmul_nn(dmb, wb_pad, transpose_rhs=True, name="d_proj_b", out_dtype=BF16)

    def ya_bwd_fn(rows, vecs):
        o, hg, dy = rows
        ng = vecs[0]
        sg = _sigmoid(hg)
        gate = hg * sg
        dn_parts, do_parts, n_parts = [], [], []
        for h in range(HG_HEADS):
            hs = slice(h * HG_DIM, (h + 1) * HG_DIM)
            oh = o[:, hs]
            r = lax.rsqrt(jnp.mean(oh * oh, axis=-1, keepdims=True) + RMS_EPS)
            nh = oh * r
            dn = dy[:, hs] * ng[:, hs] * gate[:, hs]
            do_parts.append(r * (dn - nh * jnp.mean(dn * nh, axis=-1, keepdims=True)))
            n_parts.append(nh)
        nrm = jnp.concatenate(n_parts, axis=1)
        dhg = dy * nrm * ng * (sg * (1.0 + hg * (1.0 - sg)))
        return [jnp.concatenate(do_parts, axis=1), dhg], [_colsum(dy * nrm * gate)]
    (do_raw, dhg), (g_norm_g,) = rowwise(ya_bwd_fn, [o_raw, (proj,) + mine["hg"], dya], [hg_norm_g],
                                         [(hw, F32), (hw, BF16)], [hw], name="hgrn2_out_bwd")
    dhq, dhf, dhi, g_lb = hgrn2_bwd(proj, [mine["hq"][0], mine["hf"][0], mine["hi"][0]], lb, do_raw, hg_states,
                                    n_seq, seq, name="hgrn2_bwd")

    do_fox = as_seq(dyb)
    dqa, dka, dva, dsum = fox_bwd(as_seq(qa), as_seq(ka), as_seq(va), do_fox, ox_fox, lse, name="fox_bwd")

    def unprep_fn(rows, vecs):
        mm = lambda a_, b_: jnp.dot(a_.astype(BF16), b_, preferred_element_type=F32)
        return [mm(rows[0], vecs[0]), mm(rows[1], vecs[1]), mm(rows[2], vecs[1])], []
    (dfq, dfk, dfv), _ = rowwise(unprep_fn, [dqa.reshape(T, wa), dka.reshape(T, wa), dva.reshape(T, wa)],
                                 [place["pqt"], place["pkt"]], [(hw, BF16)] * 3, name="fox_unprep")
    dc = -dsum.reshape(n_seq, fh, seq).transpose(0, 2, 1).reshape(T, fh)
    dc = jnp.concatenate([dc, jnp.zeros((T, LANES - fh), F32)], axis=1)
    dlf = seq_cumsum(dc, n_seq, seq, reverse=True, name="fox_cumsum_bwd")

    def lf_bwd_fn(rows, vecs):
        u = rows[0] + vecs[0]
        du_ = rows[1] * _sigmoid(-u)
        return [du_], [_colsum(du_)]
    (dff_,), (g_fb,) = rowwise(lf_bwd_fn, [(proj,) + mine["ff"], dlf], [fb_pad], [(LANES, BF16)], [LANES],
                               name="fox_logf_bwd")

    dproj = jnp.concatenate([dga, dgb, dhq, dhf, dhi, dhg, dfq, dfk, dfv, dff_], axis=1)

    grads_b = jnp.zeros((4, b_rows, d), F32)
    for nm, lhs, rhs in (("w_ff1", h1b, du), ("w_ff2", act, dz2b)):
        grads_b = matmul_tn(lhs, rhs, name="g_" + nm, into=(grads_b, lay[nm][0], lay[nm][1], axis[nm]))
    gfull = {
        "w_a": matmul_tn(y_a, dma, name="g_w_a"),
        "w_b": matmul_tn(y_b, dmb, name="g_w_b").reshape(fh, FOX_AUG, d)[:, :FOX_HDIM].reshape(hw, d),
        "w_o": matmul_tn(merged, dz1b, name="g_w_o"),
        "w_pg": matmul_tn(h1b, dpg, name="g_w_pg"),
        "w_p": matmul_tn(p_b, dpe, name="g_w_p"),
    }

    def chip_parts(nm, s):
        g = gfull[nm]
        n = g.shape[axis[nm]] // 4
        return lax.slice_in_dim(g, s * n, (s + 1) * n, axis=axis[nm])
    for nm in gfull:
        grads_b = lax.dynamic_update_slice(grads_b, jnp.stack([chip_parts(nm, s) for s in range(4)]),
                                           (0, lay[nm][0], lay[nm][1]))
    me = 2 * lax.axis_index("x") + lax.axis_index("y")
    core = lax.axis_index("c")

    def sum2_fn(rows, vecs):
        s = rows[0] + rows[1].astype(F32)
        return [s, s], []

    def sum4_fn(rows, vecs):
        a, r0, r1, r2 = rows
        return [((a + r0.astype(F32)) + r1.astype(F32)) + r2.astype(F32)], []

    def chip_pair_sum(g, tag):
        h, cols = g.shape[1] // 2, g.shape[2]
        keep = lax.dynamic_slice_in_dim(g, core * h, h, axis=1)
        give = lax.dynamic_slice_in_dim(g, (1 - core) * h, h, axis=1).astype(BF16)
        (from_core,) = swap_cores([give], name="swap_partials_" + tag)
        (s32, s16), _ = rowwise(sum2_fn, [keep.reshape(4 * h, cols), from_core.reshape(4 * h, cols)], [],
                                [(cols, F32), (cols, BF16)], name="sum_cores_" + tag, tm=SUM_TILE)
        return s32.reshape(4, h, cols), s16.reshape(4, h, cols)

    def chip_sum(pr, gt, tag):
        own = lax.dynamic_index_in_dim(pr, me, axis=0, keepdims=False)
        (q,), _ = rowwise(sum4_fn, [own, gt[0], gt[1], gt[2]], [], [(own.shape[1], F32)], name="sum_chips_" + tag,
                          tm=SUM_TILE)
        return q

    pair_rest, pair_rest_b = chip_pair_sum(grads_b, "rest")
    gw_in_mine, (got_rest,) = matmul_tn(h0b, dproj, name="g_w_in", rider=scatter_rider([pair_rest_b]))
    gfull["w_in"] = jnp.concatenate([gw_in_mine[:, mine[nm][0]:mine[nm][0] + orig[nm][1]]
                                     for nm in ["hq", "hf", "hi", "hg", "fq", "fk", "fv", "ff", "ga", "gb"]], axis=1)
    grads_a = jnp.stack([pack_a(chip_parts("w_in", s)) for s in range(4)])
    pair_in, pair_in_b = chip_pair_sum(grads_a, "w_in")
    def ln0_bwd_post(dh0_in, aux, vecs):
        dx, dg, db = _ln_bwd(aux[1], dh0_in + ALPHA * aux[0], vecs[0])
        return [dx], [dg, db]
    ((dx,), (g_ln0_g, g_ln0_b)), (got_in,) = matmul_nn(
        dproj, win_mine, transpose_rhs=True, name="d_in_proj_ln0", tm=fused_tm, post=ln0_bwd_post,
        post_aux=[dz1, x2], post_vecs=[vec(ln0_g)], post_outs=[F32], post_accs=[d, d],
        rider=scatter_rider([pair_in_b]))
    q_half = [chip_sum(pair_in, got_in, "w_in"), chip_sum(pair_rest, got_rest, "rest")]
    q_other = swap_cores(q_half, name="swap_halves")
    g_a, g_b = [jnp.concatenate([jnp.where(core == 0, mine_, other), jnp.where(core == 0, other, mine_)], axis=0)
                for mine_, other in zip(q_half, q_other)]
    g_shards = unpack_b(g_b, lay)
    g_shards["w_in"] = g_a[:, :in_cols]

    assert d == PACK_W and 2 * hw == PACK_W and fh <= LANES
    small = allreduce_small(jnp.concatenate(
        [g_ln0_g, g_ln0_b, g_ln1_g, g_ln1_b, g_ln2_g, g_ln2_b, jnp.concatenate([g_norm_g, g_lb], axis=1),
         jnp.concatenate([g_fb, loss_part[:, LANES:]], axis=1)], axis=0), name="allreduce_small")
    loss = small[7, LANES]

    small_w = [vec(ln0_g), vec(ln0_b), ln1_g, ln1_b, ln2_g, ln2_b, hg_lb, hg_norm_g, fox_fb]
    small_m = [vec(m_ln0_g), vec(m_ln0_b), m_ln1_g, m_ln1_b, m_ln2_g, m_ln2_b, m_hg_lb, m_hg_norm_g, m_fox_fb]
    small_v = [vec(v_ln0_g), vec(v_ln0_b), v_ln1_g, v_ln1_b, v_ln2_g, v_ln2_b, v_hg_lb, v_hg_norm_g, v_fox_fb]
    small_out = adamw_small(small, probs[0:1], small_w, small_m, small_v, name="adamw_small")
    small_shapes = [ln0_g.shape, ln0_b.shape, ln1_g.shape, ln1_b.shape, ln2_g.shape, ln2_b.shape, hg_lb.shape,
                    hg_norm_g.shape, fox_fb.shape]
    sg_out, sd_out, sm_out, sv_out = [[a.reshape(shp) for a, shp in zip(small_out[9 * k:9 * k + 9], small_shapes)]
                                      for k in range(4)]

    big_out = {}
    for nm in names:
        delta, m2, v2 = adamw(big[nm], g_shards[nm], big_m[nm], big_v[nm], name="adamw_" + nm)
        big_out[nm] = (g_shards[nm][None], delta[None], m2[None], v2[None])

    def ordered(k):
        sm_ = [sg_out, sd_out, sm_out, sv_out][k]
        bg = lambda nm: big_out[nm][k]
        return [sm_[0], sm_[1], bg("w_in"), sm_[6], sm_[7], sm_[8], bg("w_a"), bg("w_b"), bg("w_o"), sm_[2], sm_[3],
                bg("w_ff1"), bg("w_ff2"), bg("w_pg"), bg("w_p"), sm_[4], sm_[5]]
    grad_x = dx.reshape(n_seq, seq, d)
    return (loss, grad_x, *ordered(0), *ordered(1), *ordered(2), *ordered(3))
```

```python
import functools
from typing import NamedTuple, Optional

import numpy as np
import jax
import jax.numpy as jnp
from jax import lax
from jax.experimental import pallas as pl
from jax.experimental.pallas import tpu as pltpu

F32 = jnp.float32
BF16 = jnp.bfloat16
MESH = pl.DeviceIdType.MESH

VMEM_LIMIT_BYTES = 48 * 1024 * 1024
LANES = 128
HG_HEADS = 4
HG_DIM = 128
HG_BLK = 16
HG_TILE = 256
HG_SLOTS = 8
FOX_HDIM = 64
FOX_AUG = 128
FOX_TQ = 1024
FOX_FWD_HEADS = 1
LN_EPS = 1e-5
RMS_EPS = 1e-6
DEPTH = 1
ALPHA = (2.0 * DEPTH) ** 0.25
ADAM_LR, ADAM_B1, ADAM_B2, ADAM_EPS, ADAM_WD, ADAM_STEP = 0.001, 0.9, 0.999, 1e-08, 0.01, 10
NEG_INF = -1e30


def _cparams(sem):
    return pltpu.CompilerParams(dimension_semantics=sem, vmem_limit_bytes=VMEM_LIMIT_BYTES)


def _tile(n, cap):
    if n <= cap:
        return n
    best = None
    for t in range(LANES, cap + 1, LANES):
        if n % t == 0:
            best = t
    assert best is not None, (n, cap)
    return best


class WView(NamedTuple):
    arr: jax.Array
    r0: int
    c0: int
    k: int
    n: int
    split: Optional[int]


def matmul_nn(a, w, *, name, transpose_rhs=False, out_dtype=F32, epilogue=None, aux=None, tm=2048, rider=None,
              post=None, post_aux=(), post_vecs=(), post_outs=(), post_accs=()):
    wv = w if isinstance(w, WView) else WView(w[None], 0, 0, w.shape[0], w.shape[1], None)
    rows_s = wv.k // 4 if wv.split == 0 else wv.k
    cols_s = wv.n // 4 if wv.split == 1 else wv.n
    tr, tc = _tile(rows_s, 1152), _tile(cols_s, 1152)
    assert wv.r0 % tr == 0 and wv.c0 % tc == 0
    T, K = a.shape
    N, tn, tk = (wv.k, tr, tc) if transpose_rhs else (wv.n, tc, tr)
    assert K == (wv.n if transpose_rhs else wv.k)
    tm = min(tm, T)
    assert T % tm == 0
    nk = K // tk

    def w_block(ri, ci):
        if wv.split == 0:
            return (ri * tr) // rows_s, (wv.r0 + (ri * tr) % rows_s) // tr, wv.c0 // tc + ci
        if wv.split == 1:
            return (ci * tc) // cols_s, wv.r0 // tr + ri, (wv.c0 + (ci * tc) % cols_s) // tc
        return 0, wv.r0 // tr + ri, wv.c0 // tc + ci

    fused = post is not None
    assert not fused or N == tn
    aux_list = list(post_aux) if fused else ([aux] if aux is not None else [])
    aux_list = [x if isinstance(x, tuple) else (x, 0, x.shape[1]) for x in aux_list]
    vec_list = list(post_vecs)
    out_dtypes = list(post_outs) if fused else [out_dtype]
    n_aux, n_vec, n_out, n_acc = len(aux_list), len(vec_list), len(out_dtypes), len(post_accs)

    def body(*refs):
        a_ref, w_ref = refs[:2]
        aux_refs = refs[2:2 + n_aux]
        vec_refs = refs[2 + n_aux:2 + n_aux + n_vec]
        out_refs = refs[2 + n_aux + n_vec:2 + n_aux + n_vec + n_out]
        sum_refs = refs[2 + n_aux + n_vec + n_out:2 + n_aux + n_vec + n_out + n_acc]
        acc_ref = refs[-1]
        m, k = pl.program_id(1), pl.program_id(2)
        if transpose_rhs:
            part = lax.dot_general(a_ref[...], w_ref[...], (((1,), (1,)), ((), ())), preferred_element_type=F32)
        else:
            part = jnp.dot(a_ref[...], w_ref[...], preferred_element_type=F32)

        def write(res):
            if not fused:
                if epilogue is not None:
                    res = epilogue(res) if not aux_refs else epilogue(res, aux_refs[0][...])
                out_refs[0][...] = res.astype(out_dtype)
                return
            outs, sums = post(res, [r[...] for r in aux_refs], [v[...] for v in vec_refs])
            assert len(outs) == n_out and len(sums) == n_acc
            for r, val in zip(out_refs, outs):
                r[...] = val.astype(r.dtype)
            for r, val in zip(sum_refs, sums):
                def first_rows(r=r, val=val):
                    r[...] = val

                def later_rows(r=r, val=val):
                    r[...] += val
                pl.when(m == 0)(first_rows)
                pl.when(m > 0)(later_rows)

        if nk == 1:
            write(part)
        else:
            @pl.when(k == 0)
            def _():
                acc_ref[...] = part

            @pl.when(k > 0)
            def _():
                acc_ref[...] += part

            @pl.when(k == nk - 1)
            def _():
                write(acc_ref[...])

    w_index = (lambda n, m, k: w_block(n, k)) if transpose_rhs else (lambda n, m, k: w_block(k, n))
    in_specs = [pl.BlockSpec((tm, tk), lambda n, m, k: (m, k)),
                pl.BlockSpec((None, tr, tc), w_index)]
    args = [a, wv.arr]
    for arr, off, width in aux_list:
        assert width == N and off % tn == 0
        in_specs.append(pl.BlockSpec((tm, tn), functools.partial(lambda n, m, k, blk: (m, blk + n), blk=off // tn)))
        args.append(arr)
    for v in vec_list:
        in_specs.append(pl.BlockSpec(v.shape, lambda n, m, k: (0, 0)))
        args.append(v)
    out_specs = [pl.BlockSpec((tm, tn), lambda n, m, k: (m, n)) for _ in out_dtypes]
    out_specs += [pl.BlockSpec((1, tn), lambda n, m, k: (0, 0)) for _ in post_accs]
    out_shape = [jax.ShapeDtypeStruct((T, N), dt) for dt in out_dtypes]
    out_shape += [jax.ShapeDtypeStruct((1, N), F32) for _ in post_accs]
    scratch = [pltpu.VMEM((tm, tn) if nk > 1 else (8, LANES), F32)]
    grid = (N // tn, T // tm, nk)
    sem = ("arbitrary",) * 3 if (n_acc or rider is not None) else ("parallel", "parallel", "arbitrary")
    params = pltpu.CompilerParams(dimension_semantics=sem, vmem_limit_bytes=VMEM_LIMIT_BYTES,
                                  has_side_effects=rider is not None)
    if rider is not None:
        r_in, r_out, r_sems = rider.specs()
        body = rider.wrap(body, len(in_specs), len(out_specs), 3)
        in_specs, out_specs, out_shape = in_specs + r_in, out_specs + r_out, out_shape + rider.out_shape
        scratch, args = scratch + r_sems, args + list(rider.ins)
    res = pl.pallas_call(body, name=name, grid=grid, in_specs=in_specs, out_specs=out_specs, out_shape=out_shape,
                         scratch_shapes=scratch, compiler_params=params)(*args)
    main = (list(res[:n_out]), list(res[n_out:n_out + n_acc])) if fused else res[0]
    return main if rider is None else (main, list(res[n_out + n_acc:]))


def matmul_tn(a, b, *, name, tk=2048, rider=None, into=None):
    T, M = a.shape
    T2, N = b.shape
    tk = min(tk, T)
    assert T == T2 and T % tk == 0
    if into is not None:
        assert rider is None
        buf, r0, c0, split = into
        rows_s, cols_s = (M // 4, N) if split == 0 else (M, N // 4)
        tm, tn = _tile(rows_s, 1024), _tile(cols_s, 1152)
        assert r0 % tm == 0 and c0 % tn == 0

        def part_block(m, n, k):
            if split == 0:
                return (m * tm) // rows_s, (r0 + (m * tm) % rows_s) // tm, c0 // tn + n
            return (n * tn) // cols_s, r0 // tm + m, (c0 + (n * tn) % cols_s) // tn

        def body_into(a_ref, b_ref, buf_ref, o_ref):
            k = pl.program_id(2)
            part = lax.dot_general(a_ref[...], b_ref[...], (((0,), (0,)), ((), ())), preferred_element_type=F32)

            @pl.when(k == 0)
            def _():
                o_ref[...] = part

            @pl.when(k > 0)
            def _():
                o_ref[...] += part

        return pl.pallas_call(
            body_into, name=name, grid=(M // tm, N // tn, T // tk),
            in_specs=[pl.BlockSpec((tk, tm), lambda m, n, k: (k, m)), pl.BlockSpec((tk, tn), lambda m, n, k: (k, n)),
                      pl.BlockSpec(memory_space=pl.ANY)],
            out_specs=pl.BlockSpec((None, tm, tn), part_block),
            out_shape=jax.ShapeDtypeStruct(buf.shape, buf.dtype), input_output_aliases={2: 0},
            compiler_params=_cparams(("parallel", "parallel", "arbitrary")))(a, b, buf)
    tm = _tile(M, 1024)
    tn = _tile(N, 1152)

    def body(a_ref, b_ref, o_ref):
        k = pl.program_id(2)
        part = lax.dot_general(a_ref[...], b_ref[...], (((0,), (0,)), ((), ())), preferred_element_type=F32)

        @pl.when(k == 0)
        def _():
            o_ref[...] = part

        @pl.when(k > 0)
        def _():
            o_ref[...] += part

    in_specs = [pl.BlockSpec((tk, tm), lambda m, n, k: (k, m)), pl.BlockSpec((tk, tn), lambda m, n, k: (k, n))]
    out_specs = [pl.BlockSpec((tm, tn), lambda m, n, k: (m, n))]
    out_shape = [jax.ShapeDtypeStruct((M, N), F32)]
    grid = (M // tm, N // tn, T // tk)
    if rider is None:
        return pl.pallas_call(body, name=name, grid=grid, in_specs=in_specs, out_specs=out_specs, out_shape=out_shape,
                              compiler_params=_cparams(("parallel", "parallel", "arbitrary")))(a, b)[0]
    r_in, r_out, r_sems = rider.specs()
    res = pl.pallas_call(
        rider.wrap(body, 2, 1, 3), name=name, grid=grid, in_specs=in_specs + r_in, out_specs=out_specs + r_out,
        out_shape=out_shape + rider.out_shape, scratch_shapes=r_sems,
        compiler_params=pltpu.CompilerParams(dimension_semantics=("arbitrary",) * 3,
                                             vmem_limit_bytes=VMEM_LIMIT_BYTES, has_side_effects=True),
    )(a, b, *rider.ins)
    return res[0], list(res[1:])


def rowwise(fn, rows, vecs, outs, accs=(), *, name, tm=1024, rider=None):
    rows = [r if isinstance(r, tuple) else (r, 0, r.shape[1]) for r in rows]
    T = rows[0][0].shape[0]
    tm = min(tm, T)
    assert T % tm == 0
    n_rows, n_vecs, n_outs, n_accs = len(rows), len(vecs), len(outs), len(accs)

    def body(*refs):
        row_refs = refs[:n_rows]
        vec_refs = refs[n_rows:n_rows + n_vecs]
        out_refs = refs[n_rows + n_vecs:n_rows + n_vecs + n_outs]
        acc_refs = refs[n_rows + n_vecs + n_outs:]
        out_vals, acc_vals = fn([r[...] for r in row_refs], [v[...] for v in vec_refs])
        assert len(out_vals) == n_outs and len(acc_vals) == n_accs
        for r, val in zip(out_refs, out_vals):
            r[...] = val.astype(r.dtype)
        if n_accs:
            i = pl.program_id(0)

            @pl.when(i == 0)
            def _():
                for r in acc_refs:
                    r[...] = jnp.zeros_like(r)

            for r, val in zip(acc_refs, acc_vals):
                r[...] += val

    in_specs = []
    for arr, off, width in rows:
        assert off % width == 0
        in_specs.append(pl.BlockSpec((tm, width), functools.partial(lambda i, blk: (i, blk), blk=off // width)))
    for v in vecs:
        in_specs.append(pl.BlockSpec(v.shape, lambda i: (0, 0)))
    out_specs = [pl.BlockSpec((tm, w), lambda i: (i, 0)) for w, _ in outs]
    out_specs += [pl.BlockSpec((1, w), lambda i: (0, 0)) for w in accs]
    out_shape = [jax.ShapeDtypeStruct((T, w), dt) for w, dt in outs]
    out_shape += [jax.ShapeDtypeStruct((1, w), F32) for w in accs]
    args = [r[0] for r in rows] + list(vecs)
    if rider is None:
        res = pl.pallas_call(body, name=name, grid=(T // tm,), in_specs=in_specs, out_specs=out_specs,
                             out_shape=out_shape,
                             compiler_params=_cparams(("arbitrary",) if n_accs else ("parallel",)))(*args)
        return res[:n_outs], res[n_outs:]
    r_in, r_out, r_sems = rider.specs()
    res = pl.pallas_call(
        rider.wrap(body, len(in_specs), len(out_specs), 1), name=name, grid=(T // tm,), in_specs=in_specs + r_in,
        out_specs=out_specs + r_out, out_shape=out_shape + rider.out_shape, scratch_shapes=r_sems,
        compiler_params=pltpu.CompilerParams(dimension_semantics=("arbitrary",), vmem_limit_bytes=VMEM_LIMIT_BYTES,
                                             has_side_effects=True),
    )(*args, *rider.ins)
    return res[:n_outs], res[n_outs:n_outs + n_accs], list(res[n_outs + n_accs:])


def _colsum(x):
    return jnp.sum(x, axis=0, keepdims=True)


def _sigmoid(x):
    return 1.0 / (1.0 + jnp.exp(-x))


def _ln_stats(z):
    mu = jnp.mean(z, axis=-1, keepdims=True)
    zc = z - mu
    var = jnp.mean(zc * zc, axis=-1, keepdims=True)
    return zc * lax.rsqrt(var + LN_EPS)


def _ln_bwd(zhat_src, dy, g):
    mu = jnp.mean(zhat_src, axis=-1, keepdims=True)
    zc = zhat_src - mu
    var = jnp.mean(zc * zc, axis=-1, keepdims=True)
    rstd = lax.rsqrt(var + LN_EPS)
    zh = zc * rstd
    dzh = dy * g
    dz = rstd * (dzh - jnp.mean(dzh, axis=-1, keepdims=True) - zh * jnp.mean(dzh * zh, axis=-1, keepdims=True))
    return dz, _colsum(dy * zh), _colsum(dy)


def _hg_constants():
    r = np.arange(HG_TILE)
    same = (r[:, None] // HG_BLK) == (r[None, :] // HG_BLK)
    lower = (same & (r[None, :] <= r[:, None])).astype(np.float32)
    upper = (same & (r[None, :] >= r[:, None])).astype(np.float32)
    total = same.astype(np.float32)
    c = np.arange(2 * HG_DIM)
    bd = ((c[:, None] // HG_DIM) == (c[None, :] // HG_DIM)).astype(np.float32)
    pair_t = np.array([t for t, _ in _HG_PAIRS])
    pair_s = np.array([s for _, s in _HG_PAIRS])
    sel_t = (pair_t[None, :] == np.arange(HG_BLK)[:, None]).astype(np.float32)
    sel_s = (pair_s[None, :] == np.arange(HG_BLK)[:, None]).astype(np.float32)
    as_bf = lambda m: jnp.asarray(m, dtype=BF16)
    return as_bf(lower), as_bf(upper), as_bf(total), as_bf(bd), as_bf(sel_t), as_bf(sel_s)


_HG_HALF = HG_BLK // 2
_HG_PAIRS = ([(t, s) for t in range(_HG_HALF, HG_BLK) for s in range(HG_BLK)]
             + [(t, s) for t in range(_HG_HALF) for s in range(_HG_HALF)])
HG_STACK = len(_HG_PAIRS)
_HG_SLABS = ([((t - _HG_HALF) * HG_BLK, (t,), HG_BLK) for t in range(_HG_HALF, HG_BLK)]
             + [(_HG_HALF * HG_BLK + t * _HG_HALF, (t, t + 1), _HG_HALF) for t in range(0, _HG_HALF, 2)])


def _stack_by_s(x):
    return jnp.concatenate([x] * _HG_HALF + [x[:_HG_HALF]] * _HG_HALF, axis=0)


def _stack_by_t(x):
    w = x.shape[1]
    return jnp.concatenate([jnp.broadcast_to(x[t:t + 1], (HG_BLK, w)) for t in range(_HG_HALF, HG_BLK)]
                           + [jnp.broadcast_to(x[t:t + 1], (_HG_HALF, w)) for t in range(_HG_HALF)], axis=0)


def _keep_bf16_bits(x):
    bits = lax.bitcast_convert_type(x, jnp.int32) & jnp.int32(-65536)
    return lax.bitcast_convert_type(bits, F32)


def _head_sums(stack_ref, slot, bd):
    pair = bd.shape[0]
    return jnp.concatenate([jnp.dot(stack_ref[slot, :, c0:c0 + pair], bd, preferred_element_type=F32)
                            for c0 in range(0, stack_ref.shape[2], pair)], axis=1)


def _split3(x):
    hi = _keep_bf16_bits(x)
    r1 = x - hi
    mid = _keep_bf16_bits(r1)
    lo = _keep_bf16_bits(r1 - mid)
    return hi.astype(BF16), mid.astype(BF16), lo.astype(BF16)


def _dot3(m01, x):
    hi, mid, lo = _split3(x)
    d = lambda p: jnp.dot(m01, p, preferred_element_type=F32)
    return (d(lo) + d(mid)) + d(hi)


def _hg_prologue(hq, hf, lb, lower, total):
    sq = _sigmoid(hq)
    q = hq * sq
    sg = _sigmoid(hf)
    f = lb + (1.0 - lb) * sg
    g = jnp.log(f)
    k = 1.0 - f
    b = _dot3(lower, g)
    bl = _dot3(total, g)
    return q, k, f, sg, sq, b, bl


def _stack16(fn):
    return [fn(t) for t in range(HG_BLK)]


def hgrn2_fwd(proj, offs, lb, n_seq, seq, *, name, rider=None):
    T = n_seq * seq
    W = HG_HEADS * HG_DIM
    n_tiles = seq // HG_TILE
    nb = HG_TILE // HG_BLK
    lower, _, total, bd, sel_t, _ = _hg_constants()

    def body(hq_ref, hf_ref, hi_ref, lb_ref, lower_ref, total_ref, bd_ref, selt_ref,
             o_ref, st_out_ref,
             st_ref, q_s, k_s, v_s, b_s, qt_s, kt_s, d_s, p_s):
        @pl.when(pl.program_id(1) == 0)
        def _():
            st_ref[...] = jnp.zeros_like(st_ref)

        q, k, _, _, _, b, bl = _hg_prologue(hq_ref[...], hf_ref[...], lb_ref[...], lower_ref[...], total_ref[...])
        q_s[...] = q
        k_s[...] = k
        v_s[...] = hi_ref[...]
        b_s[...] = b
        qt_s[...] = q * jnp.exp(b)
        kt_s[...] = k * jnp.exp(jnp.minimum(bl - b, 0.0))
        d_s[...] = jnp.exp(bl)
        rowi = lax.broadcasted_iota(jnp.int32, (HG_BLK, W), 0)

        def block(i, slot):
            r0 = pl.multiple_of(i * HG_BLK, HG_BLK)
            rows = pl.ds(r0, HG_BLK)
            qi, ki, vi, bi = q_s[rows, :], k_s[rows, :], v_s[rows, :], b_s[rows, :]
            for off, ts, n in _HG_SLABS:
                slab = [jnp.where(rowi[:n] <= t, jnp.exp(jnp.minimum(bi[t:t + 1, :] - bi[:n], 0.0)), 0.0)
                        * qi[t:t + 1, :] * ki[:n] for t in ts]
                p_s[slot, pl.ds(off, HG_BLK), :] = jnp.concatenate(slab, axis=0).astype(BF16)
            a_b = _head_sums(p_s, slot, bd_ref[...])
            o_blk = jnp.dot(selt_ref[...], (a_b * _stack_by_s(vi)).astype(BF16), preferred_element_type=F32)
            qti, kti, di = qt_s[rows, :], kt_s[rows, :], d_s[rows, :]
            outs = []
            for h in range(HG_HEADS):
                hs = slice(h * HG_DIM, (h + 1) * HG_DIM)
                st_h = st_ref[hs, :]
                st_out_ref[i, hs, :] = st_h
                outs.append(lax.dot_general(qti[:, hs].astype(BF16), st_h.astype(BF16),
                                            (((1,), (1,)), ((), ())), preferred_element_type=F32))
                upd = lax.dot_general(vi[:, hs].astype(BF16), kti[:, hs].astype(BF16),
                                      (((0,), (0,)), ((), ())), preferred_element_type=F32)
                st_ref[hs, :] = st_h * di[0:1, hs] + upd
            o_ref[rows, :] = o_blk + jnp.concatenate(outs, axis=1)

        def some_blocks(jj, carry):
            for slot in range(HG_SLOTS):
                block(HG_SLOTS * jj + slot, slot)
            return carry

        lax.fori_loop(0, nb // HG_SLOTS, some_blocks, 0)

    col = lambda off: functools.partial(lambda s, t, blk: (s * n_tiles + t, blk), blk=off // W)
    const = lambda m: pl.BlockSpec(m.shape, lambda s, t: (0, 0))
    tile_f32 = pltpu.VMEM((HG_TILE, W), F32)
    in_specs = [pl.BlockSpec((HG_TILE, W), col(offs[0])), pl.BlockSpec((HG_TILE, W), col(offs[1])),
                pl.BlockSpec((HG_TILE, W), col(offs[2])), const(lb), const(lower), const(total), const(bd),
                const(sel_t)]
    out_specs = [pl.BlockSpec((HG_TILE, W), lambda s, t: (s * n_tiles + t, 0)),
                 pl.BlockSpec((nb, W, HG_DIM), lambda s, t: (s * n_tiles + t, 0, 0))]
    out_shape = [jax.ShapeDtypeStruct((T, W), F32), jax.ShapeDtypeStruct((T // HG_BLK, W, HG_DIM), F32)]
    scratch = [pltpu.VMEM((W, HG_DIM), F32)] + [tile_f32] * 7 + [pltpu.VMEM((HG_SLOTS, HG_STACK, W), BF16)]
    args = [proj, proj, proj, lb, lower, total, bd, sel_t]
    params = _cparams(("arbitrary", "arbitrary"))
    if rider is not None:
        r_in, r_out, r_sems = rider.specs()
        body = rider.wrap(body, len(in_specs), len(out_specs), 2)
        in_specs, out_specs, out_shape = in_specs + r_in, out_specs + r_out, out_shape + rider.out_shape
        scratch, args = scratch + r_sems, args + rider.ins
        params = pltpu.CompilerParams(dimension_semantics=("arbitrary", "arbitrary"),
                                      vmem_limit_bytes=VMEM_LIMIT_BYTES, has_side_effects=True)
    res = pl.pallas_call(body, name=name, grid=(n_seq, n_tiles), in_specs=in_specs, out_specs=out_specs,
                         out_shape=out_shape, scratch_shapes=scratch, compiler_params=params)(*args)
    return res[0], res[1], list(res[2:])


def hgrn2_bwd(proj, offs, lb, do, states, n_seq, seq, *, name):
    T = n_seq * seq
    W = HG_HEADS * HG_DIM
    n_tiles = seq // HG_TILE
    nb = HG_TILE // HG_BLK
    lower, upper, total, bd, sel_t, sel_s = _hg_constants()

    def body(hq_ref, hf_ref, hi_ref, do_ref, st_in_ref, lb_ref, lower_ref, upper_ref, total_ref, bd_ref,
             selt_ref, sels_ref,
             dhq_ref, dhf_ref, dhi_ref, dlb_ref,
             dst_ref, q_s, k_s, v_s, b_s, qt_s, kt_s, d_s, eb_s, ekb_s, dq_s, dk_s, db_s, dv_s,
             p_s, e_s, w_s):
        first = jnp.logical_and(pl.program_id(0) == 0, pl.program_id(1) == 0)

        @pl.when(first)
        def _():
            dlb_ref[...] = jnp.zeros_like(dlb_ref)

        @pl.when(pl.program_id(1) == 0)
        def _():
            dst_ref[...] = jnp.zeros_like(dst_ref)

        hq, lbv = hq_ref[...], lb_ref[...]
        q, k, f, sg, sq, b, bl = _hg_prologue(hq, hf_ref[...], lbv, lower_ref[...], total_ref[...])
        eb = jnp.exp(b)
        ekb = jnp.exp(jnp.minimum(bl - b, 0.0))
        q_s[...] = q
        k_s[...] = k
        v_s[...] = hi_ref[...]
        b_s[...] = b
        eb_s[...] = eb
        ekb_s[...] = ekb
        qt_s[...] = q * eb
        kt_s[...] = k * ekb
        d_s[...] = jnp.exp(bl)
        rowi = lax.broadcasted_iota(jnp.int32, (HG_BLK, W), 0)
        last_row = rowi == HG_BLK - 1

        def block(i, slot):
            r0 = pl.multiple_of(i * HG_BLK, HG_BLK)
            rows = pl.ds(r0, HG_BLK)
            qi, ki, vi, bi, doi = q_s[rows, :], k_s[rows, :], v_s[rows, :], b_s[rows, :], do_ref[rows, :]
            for off, ts, n in _HG_SLABS:
                es = [jnp.where(rowi[:n] <= t, jnp.exp(jnp.minimum(bi[t:t + 1, :] - bi[:n], 0.0)), 0.0) for t in ts]
                sl = pl.ds(off, HG_BLK)
                e_s[slot, sl, :] = jnp.concatenate(es, axis=0)
                p_s[slot, sl, :] = jnp.concatenate([e * qi[t:t + 1, :] * ki[:n] for e, t in zip(es, ts)],
                                                   axis=0).astype(BF16)
                w_s[slot, sl, :] = jnp.concatenate([doi[t:t + 1, :] * vi[:n] for t in ts], axis=0).astype(BF16)
            a_b = _head_sums(p_s, slot, bd_ref[...])
            da_b = _head_sums(w_s, slot, bd_ref[...])
            x = da_b * e_s[slot]
            dq_in = jnp.dot(selt_ref[...], (x * _stack_by_s(ki)).astype(BF16), preferred_element_type=F32)
            dk_in = jnp.dot(sels_ref[...], (x * _stack_by_t(qi)).astype(BF16), preferred_element_type=F32)
            dv_in = jnp.dot(sels_ref[...], (a_b * _stack_by_t(doi)).astype(BF16), preferred_element_type=F32)
            qti, kti, di = qt_s[rows, :], kt_s[rows, :], d_s[rows, :]
            dqt, dkt, dvt, dd = [], [], [], []
            for h in range(HG_HEADS):
                hs = slice(h * HG_DIM, (h + 1) * HG_DIM)
                st_h = st_in_ref[i, hs, :]
                dst_h = dst_ref[hs, :]
                do_h, v_h = doi[:, hs].astype(BF16), vi[:, hs].astype(BF16)
                dst_b = dst_h.astype(BF16)
                dqt.append(jnp.dot(do_h, st_h.astype(BF16), preferred_element_type=F32))
                dkt.append(jnp.dot(v_h, dst_b, preferred_element_type=F32))
                dvt.append(lax.dot_general(kti[:, hs].astype(BF16), dst_b, (((1,), (1,)), ((), ())),
                                           preferred_element_type=F32))
                dd.append(jnp.sum(dst_h * st_h, axis=0, keepdims=True))
                upd = lax.dot_general(do_h, qti[:, hs].astype(BF16), (((0,), (0,)), ((), ())),
                                      preferred_element_type=F32)
                dst_ref[hs, :] = dst_h * di[0:1, hs] + upd
            dqt = jnp.concatenate(dqt, axis=1)
            dkt = jnp.concatenate(dkt, axis=1)
            dvt = jnp.concatenate(dvt, axis=1)
            dd = jnp.concatenate(dd, axis=1)
            dbl = jnp.sum(dkt * kti, axis=0, keepdims=True) + dd * di[0:1, :]
            db = qi * dq_in - ki * dk_in + dqt * qti - dkt * kti
            db_s[rows, :] = db + jnp.where(last_row, dbl, 0.0)
            dq_s[rows, :] = dq_in + dqt * eb_s[rows, :]
            dk_s[rows, :] = dk_in + dkt * ekb_s[rows, :]
            dv_s[rows, :] = dv_in + dvt

        def some_blocks(jj, carry):
            for slot in range(HG_SLOTS):
                block(nb - 1 - slot - HG_SLOTS * jj, slot)
            return carry

        lax.fori_loop(0, nb // HG_SLOTS, some_blocks, 0)

        dg = _dot3(upper_ref[...], db_s[...])
        dhq_ref[...] = (dq_s[...] * (sq * (1.0 + hq * (1.0 - sq)))).astype(dhq_ref.dtype)
        df = dg / f - dk_s[...]
        dhf_ref[...] = (df * (1.0 - lbv) * (sg * (1.0 - sg))).astype(dhf_ref.dtype)
        dhi_ref[...] = dv_s[...].astype(dhi_ref.dtype)
        dlb_ref[...] += _colsum(df * (1.0 - sg))

    rev = lambda s, t: s * n_tiles + (n_tiles - 1 - t)
    col = lambda off: functools.partial(lambda s, t, blk: (rev(s, t), blk), blk=off // W)
    const = lambda m: pl.BlockSpec(m.shape, lambda s, t: (0, 0))
    row = pl.BlockSpec((HG_TILE, W), lambda s, t: (rev(s, t), 0))
    tile_f32 = pltpu.VMEM((HG_TILE, W), F32)
    n2 = HG_STACK
    return pl.pallas_call(
        body, name=name,
        grid=(n_seq, n_tiles),
        in_specs=[pl.BlockSpec((HG_TILE, W), col(offs[0])), pl.BlockSpec((HG_TILE, W), col(offs[1])),
                  pl.BlockSpec((HG_TILE, W), col(offs[2])), row,
                  pl.BlockSpec((nb, W, HG_DIM), lambda s, t: (rev(s, t), 0, 0)),
                  const(lb), const(lower), const(upper), const(total), const(bd), const(sel_t), const(sel_s)],
        out_specs=[row, row, row, pl.BlockSpec((1, W), lambda s, t: (0, 0))],
        out_shape=[jax.ShapeDtypeStruct((T, W), BF16)] * 3 + [jax.ShapeDtypeStruct((1, W), F32)],
        scratch_shapes=[pltpu.VMEM((W, HG_DIM), F32)] + [tile_f32] * 13
                       + [pltpu.VMEM((HG_SLOTS, n2, W), BF16), pltpu.VMEM((HG_SLOTS, n2, W), F32),
                          pltpu.VMEM((HG_SLOTS, n2, W), BF16)],
        compiler_params=_cparams(("arbitrary", "arbitrary")),
    )(proj, proj, proj, do, states, lb, lower, upper, total, bd, sel_t, sel_s)


def _diag_mask(tq):
    return lax.broadcasted_iota(jnp.int32, (tq, tq), 1) <= lax.broadcasted_iota(jnp.int32, (tq, tq), 0)


def _qk(q, k):
    return lax.dot_general(q, k, (((1,), (1,)), ((), ())), preferred_element_type=F32)


def _causal_pairs(n, sweeps=1, by_key=False):
    if by_key:
        rows = [(i, j, 0) for j in range(n) for i in range(j, n)]
    else:
        rows = [(i, j, s) for i in range(n) for s in range(sweeps) for j in range(i + 1)]
    return tuple(jnp.asarray(np.array([r[c] for r in rows], np.int32)) for c in range(3))


def _fox_placement(fh):
    hw, wa = fh * FOX_HDIM, fh * FOX_AUG
    pq, pk = np.zeros((hw, wa), np.float32), np.zeros((hw, wa), np.float32)
    aq, ak = np.zeros((3 * LANES, wa), np.float32), np.zeros((3 * LANES, wa), np.float32)
    oq, ok = np.zeros((1, wa), np.float32), np.zeros((1, wa), np.float32)
    for h in range(fh):
        src, dst = np.arange(h * FOX_HDIM, (h + 1) * FOX_HDIM), np.arange(h * FOX_AUG, h * FOX_AUG + FOX_HDIM)
        pq[src, dst] = FOX_HDIM ** -0.5
        pk[src, dst] = 1.0
        gate = h * FOX_AUG + FOX_HDIM
        for r in range(3):
            aq[r * LANES + h, gate + r] = 1.0
            ak[r * LANES + h, gate + 3 + r] = -1.0
        oq[0, gate + 3:gate + 6] = 1.0
        ok[0, gate:gate + 3] = 1.0
    bf = lambda m: jnp.asarray(m, dtype=BF16)
    return {"pq": bf(pq), "pk": bf(pk), "aq": bf(aq), "ak": bf(ak), "oq": jnp.asarray(oq), "ok": jnp.asarray(ok),
            "pqt": bf(pq.T), "pkt": bf(pk.T)}


def _fox_specs(tq, fh, heads=1):
    groups = fh // heads

    def spec(tab):
        return pl.BlockSpec((None, tq, heads * FOX_AUG), lambda b, t, *tabs: (b // groups, tabs[tab][t], b % groups))
    return spec(0), spec(1)


def fox_fwd(qa, ka, va, *, name):
    n_seq, S, width = qa.shape
    fh = width // FOX_AUG
    nh = FOX_FWD_HEADS
    BH = n_seq * fh // nh
    tq = min(FOX_TQ, S)
    itab, jtab, _ = _causal_pairs(S // tq)

    def body(itab_ref, jtab_ref, q_ref, k_ref, v_ref, o_ref, ox_ref, lse_ref, *scratch):
        t = pl.program_id(1)
        i, j = itab_ref[t], jtab_ref[t]
        per_head = [scratch[4 * h:4 * h + 4] for h in range(nh)]

        @pl.when(j == 0)
        def _():
            for m_s, l_s, acc_s, acc_lo_s in per_head:
                m_s[...] = jnp.full_like(m_s, NEG_INF)
                l_s[...] = jnp.zeros_like(l_s)
                acc_s[...] = jnp.zeros_like(acc_s)
                acc_lo_s[...] = jnp.zeros_like(acc_lo_s)

        def step(on_diagonal):
            for h, (m_s, l_s, acc_s, acc_lo_s) in enumerate(per_head):
                lanes = slice(h * FOX_AUG, (h + 1) * FOX_AUG)
                s = _qk(q_ref[:, lanes], k_ref[:, lanes])
                if on_diagonal:
                    s = jnp.where(_diag_mask(tq), s, NEG_INF)
                m_prev = m_s[...]
                m_new = jnp.maximum(m_prev, jnp.max(s, axis=-1, keepdims=True))
                alpha = jnp.exp(m_prev - m_new)
                p = jnp.exp(s - m_new[:, 0:1])
                p_hi = p.astype(BF16)
                p_lo = (p - p_hi.astype(F32)).astype(BF16)
                v = v_ref[:, lanes]
                l_s[...] = alpha * l_s[...] + jnp.sum(p, axis=-1, keepdims=True)
                acc_s[...] = alpha * acc_s[...] + jnp.dot(p_hi, v, preferred_element_type=F32)
                acc_lo_s[...] = alpha * acc_lo_s[...] + jnp.dot(p_lo, v, preferred_element_type=F32)
                m_s[...] = m_new

        @pl.when(j < i)
        def _():
            step(False)

        @pl.when(j == i)
        def _():
            step(True)
            for h, (m_s, l_s, acc_s, acc_lo_s) in enumerate(per_head):
                lanes = slice(h * FOX_AUG, (h + 1) * FOX_AUG)
                inv_l = 1.0 / l_s[...]
                o_ref[:, lanes] = (acc_s[...] * inv_l).astype(o_ref.dtype)
                ox_ref[:, lanes] = (acc_s[...] + acc_lo_s[...]) * inv_l
                lse_ref[:, lanes] = m_s[...] + jnp.log(l_s[...])

    qspec, kspec = _fox_specs(tq, fh, nh)
    wide = jax.ShapeDtypeStruct((n_seq, S, width), F32)
    return pl.pallas_call(
        body, name=name,
        grid_spec=pltpu.PrefetchScalarGridSpec(
            num_scalar_prefetch=2, grid=(BH, itab.shape[0]),
            in_specs=[qspec, kspec, kspec],
            out_specs=[qspec, qspec, qspec],
            scratch_shapes=[pltpu.VMEM((tq, LANES), F32)] * (4 * nh)),
        out_shape=[jax.ShapeDtypeStruct((n_seq, S, width), BF16), wide, wide],
        compiler_params=_cparams(("parallel", "arbitrary")),
    )(itab, jtab, qa, ka, va)


def _fox_ds(q, k, v, do, ox, lse, on_diagonal):
    s = _qk(q, k)
    if on_diagonal:
        s = jnp.where(_diag_mask(s.shape[0]), s, NEG_INF)
    p = jnp.exp(s - lse[:, 0:1])
    delta = jnp.sum(do.astype(F32) * ox, axis=-1, keepdims=True)
    return p, p * (_qk(do, v) - delta)


def fox_bwd(qa, ka, va, do, ox, lse, *, name):
    n_seq, S, width = qa.shape
    fh = width // FOX_AUG
    BH = n_seq * fh
    tq = min(FOX_TQ, S)
    itab, jtab, _ = _causal_pairs(S // tq)

    def body(itab_ref, jtab_ref, q_ref, k_ref, v_ref, do_ref, ox_ref, lse_ref, dq_ref, dk_ref, dv_ref, dsum_ref):
        t = pl.program_id(1)
        i, j = itab_ref[t], jtab_ref[t]

        @pl.when(t == 0)
        def _():
            dq_ref[...] = jnp.zeros_like(dq_ref)
            dk_ref[...] = jnp.zeros_like(dk_ref)
            dv_ref[...] = jnp.zeros_like(dv_ref)
            dsum_ref[...] = jnp.zeros_like(dsum_ref)

        q_rows = pl.ds(pl.multiple_of(i * tq, tq), tq)
        k_rows = pl.ds(pl.multiple_of(j * tq, tq), tq)

        def step(on_diagonal):
            q, k, do = q_ref[...], k_ref[...], do_ref[...]
            p, ds = _fox_ds(q, k, v_ref[...], do, ox_ref[...], lse_ref[...], on_diagonal)
            ds_b = ds.astype(BF16)
            tn = (((0,), (0,)), ((), ()))
            dq_ref[q_rows, :] += jnp.dot(ds_b, k, preferred_element_type=F32)
            dk_ref[k_rows, :] += lax.dot_general(ds_b, q, tn, preferred_element_type=F32)
            dv_ref[k_rows, :] += lax.dot_general(p.astype(BF16), do, tn, preferred_element_type=F32)
            dsum_ref[:, k_rows] += _colsum(ds)

        @pl.when(j < i)
        def _():
            step(False)

        @pl.when(j == i)
        def _():
            step(True)

    qspec, kspec = _fox_specs(tq, fh)
    whole = pl.BlockSpec((None, S, FOX_AUG), lambda b, t, it, jt: (b // fh, 0, b % fh))
    wide = jax.ShapeDtypeStruct((n_seq, S, width), F32)
    return pl.pallas_call(
        body, name=name,
        grid_spec=pltpu.PrefetchScalarGridSpec(
            num_scalar_prefetch=2, grid=(BH, itab.shape[0]),
            in_specs=[qspec, kspec, kspec, qspec, qspec, qspec],
            out_specs=[whole, whole, whole, pl.BlockSpec((None, 1, S), lambda b, t, it, jt: (b, 0, 0))]),
        out_shape=[wide, wide, wide, jax.ShapeDtypeStruct((BH, 1, S), F32)],
        compiler_params=_cparams(("parallel", "arbitrary")),
    )(itab, jtab, qa, ka, va, do, ox, lse)


def seq_cumsum(x, n_seq, seq, *, reverse, name):
    T, C = x.shape
    tb = min(256, seq)
    n = seq // tb
    r = np.arange(tb)
    tri = (r[None, :] >= r[:, None]) if reverse else (r[None, :] <= r[:, None])
    tri = jnp.asarray(tri.astype(np.float32), dtype=BF16)

    def body(x_ref, tri_ref, o_ref, carry_s):
        @pl.when(pl.program_id(1) == 0)
        def _():
            carry_s[...] = jnp.zeros_like(carry_s)

        xv = x_ref[...]
        o_ref[...] = _dot3(tri_ref[...], xv) + carry_s[...]
        carry_s[...] += _colsum(xv)

    blk = (lambda s, t: (s * n + (n - 1 - t), 0)) if reverse else (lambda s, t: (s * n + t, 0))
    return pl.pallas_call(
        body, name=name,
        grid=(n_seq, n),
        in_specs=[pl.BlockSpec((tb, C), blk), pl.BlockSpec((tb, tb), lambda s, t: (0, 0))],
        out_specs=pl.BlockSpec((tb, C), blk),
        out_shape=jax.ShapeDtypeStruct((T, C), F32),
        scratch_shapes=[pltpu.VMEM((1, C), F32)],
        compiler_params=_cparams(("arbitrary", "arbitrary")),
    )(x, tri)


def _place():
    return lax.axis_index("x"), lax.axis_index("y"), lax.axis_index("c")


def _other_chips(x, y):
    return [(1 - x, y), (x, 1 - y), (1 - x, 1 - y)]


def _hbm_call(body, ins, out_shape, n_sems, *, name):
    hbm = pl.BlockSpec(memory_space=pl.ANY)
    return pl.pallas_call(
        body, name=name,
        in_specs=[hbm] * len(ins), out_specs=[hbm] * len(out_shape), out_shape=out_shape,
        scratch_shapes=[pltpu.SemaphoreType.DMA((n_sems,)), pltpu.SemaphoreType.DMA((n_sems,)),
                        pltpu.SemaphoreType.DMA((len(ins),))],
        compiler_params=pltpu.CompilerParams(has_side_effects=True),
    )(*ins)


def allgather_chips(shards, *, name):
    return _exchange_call(allgather_rider(shards), name=name)


def _allgather_ops(x_refs, o_refs, send_sems, recv_sems, local_sems):
    def copies():
        x, y, c = _place()
        me = 2 * x + y
        chips = _other_chips(x, y)
        own, first, passed, landed, handed = [], [], [], [], []
        for b, (x_ref, o_ref) in enumerate(zip(x_refs, o_refs)):
            half = x_ref.shape[0] // 2
            mine, theirs = pl.ds(c * half, half), pl.ds((1 - c) * half, half)
            own.append(pltpu.make_async_copy(x_ref, o_ref.at[me], local_sems.at[b]))

            def copy(k, src, chip, rows, to, o_ref=o_ref, b=b):
                return pltpu.make_async_remote_copy(src_ref=src, dst_ref=o_ref.at[2 * chip[0] + chip[1], rows],
                                                    send_sem=send_sems.at[6 * b + k], recv_sem=recv_sems.at[6 * b + k],
                                                    device_id=to, device_id_type=MESH)
            for j, chip in enumerate(chips):
                first.append(copy(j, x_ref.at[mine], (x, y), mine, (*chip, c)))
                landed.append(copy(j, x_ref.at[mine], chip, mine, (*chip, c)))
                passed.append(copy(3 + j, o_ref.at[2 * chip[0] + chip[1], mine], chip, mine, (x, y, 1 - c)))
                handed.append(copy(3 + j, x_ref.at[mine], chip, theirs, (x, y, 1 - c)))
        return own, first, passed, landed, handed

    def start():
        own, first, _, _, _ = copies()
        for cp in own + first:
            cp.start()

    def finish():
        own, first, passed, landed, handed = copies()
        for arrived, forward in zip(landed, passed):
            arrived.wait_recv()
            forward.start()
        for cp in handed:
            cp.wait_recv()
        for cp in first + passed:
            cp.wait_send()
        for cp in own:
            cp.wait()
    return start, finish


def _scatter_ops(x_refs, o_refs, send_sems, recv_sems, local_sems):
    def copies():
        x, y, c = _place()
        return [pltpu.make_async_remote_copy(
            src_ref=x_ref.at[2 * px + py], dst_ref=o_ref.at[j], send_sem=send_sems.at[3 * b + j],
            recv_sem=recv_sems.at[3 * b + j], device_id=(px, py, c), device_id_type=MESH)
            for b, (x_ref, o_ref) in enumerate(zip(x_refs, o_refs)) for j, (px, py) in enumerate(_other_chips(x, y))]

    def start():
        for cp in copies():
            cp.start()

    def finish():
        sends = copies()
        for cp in sends:
            cp.wait_recv()
        for cp in sends:
            cp.wait_send()
    return start, finish


class Rider(NamedTuple):
    ins: list
    out_shape: list
    n_sems: int
    ops: object

    def specs(self):
        hbm = pl.BlockSpec(memory_space=pl.ANY)
        sems = [pltpu.SemaphoreType.DMA((self.n_sems,)), pltpu.SemaphoreType.DMA((self.n_sems,)),
                pltpu.SemaphoreType.DMA((len(self.ins),))]
        return [hbm] * len(self.ins), [hbm] * len(self.out_shape), sems

    def wrap(self, body, n_in, n_out, grid_rank):
        k_in, k_out = len(self.ins), len(self.out_shape)

        def carried(*refs):
            ins, r_ins = refs[:n_in], refs[n_in:n_in + k_in]
            outs = refs[n_in + k_in:n_in + k_in + n_out]
            r_outs = refs[n_in + k_in + n_out:n_in + k_in + n_out + k_out]
            scratch, sems = refs[n_in + k_in + n_out + k_out:-3], refs[-3:]
            first = functools.reduce(jnp.logical_and, [pl.program_id(a) == 0 for a in range(grid_rank)])
            last = functools.reduce(jnp.logical_and,
                                    [pl.program_id(a) == pl.num_programs(a) - 1 for a in range(grid_rank)])
            pl.when(first)(lambda: self.ops(r_ins, r_outs, *sems)[0]())
            body(*ins, *outs, *scratch)
            pl.when(last)(lambda: self.ops(r_ins, r_outs, *sems)[1]())
        return carried


def _exchange_call(rider, *, name):
    def body(*refs):
        k = len(rider.ins)
        start, finish = rider.ops(refs[:k], refs[k:k + len(rider.out_shape)], *refs[-3:])
        start()
        finish()
    in_specs, out_specs, sems = rider.specs()
    return pl.pallas_call(body, name=name, in_specs=in_specs, out_specs=out_specs, out_shape=rider.out_shape,
                          scratch_shapes=sems, compiler_params=pltpu.CompilerParams(has_side_effects=True))(*rider.ins)


def allgather_rider(shards):
    assert all(s.shape[0] % (2 * ROW_ALIGN) == 0 for s in shards)
    return Rider(list(shards), [jax.ShapeDtypeStruct((4,) + s.shape, s.dtype) for s in shards], 6 * len(shards),
                 _allgather_ops)


def scatter_rider(parts):
    return Rider(list(parts), [jax.ShapeDtypeStruct((3,) + p.shape[1:], p.dtype) for p in parts], 3 * len(parts),
                 _scatter_ops)


def scatter_chips(parts, *, name):
    return _exchange_call(scatter_rider(parts), name=name)


def swap_cores(vs, *, name):
    nb = len(vs)

    def body(*refs):
        x_refs, o_refs = refs[:nb], refs[nb:2 * nb]
        send_sems, recv_sems, _ = refs[2 * nb:]
        x, y, c = _place()
        copies = [pltpu.make_async_remote_copy(src_ref=x_ref, dst_ref=o_ref, send_sem=send_sems.at[b],
                                               recv_sem=recv_sems.at[b], device_id=(x, y, 1 - c), device_id_type=MESH)
                  for b, (x_ref, o_ref) in enumerate(zip(x_refs, o_refs))]
        for cp in copies:
            cp.start()
        for cp in copies:
            cp.wait()

    return _hbm_call(body, vs, [jax.ShapeDtypeStruct(v.shape, v.dtype) for v in vs], nb, name=name)


def allreduce_small(v, *, name):
    R, C = v.shape

    def body(x_ref, o_ref, gath_ref, send_sems, recv_sems):
        x, y, c = _place()
        me = 4 * x + 2 * y + c
        gath_ref[me] = x_ref[...]
        flips = [(k >> 2 & 1, k >> 1 & 1, k & 1) for k in range(1, 8)]
        sends = []
        for j, (fx, fy, fc) in enumerate(flips):
            peer = (x ^ fx, y ^ fy, c ^ fc)
            cp = pltpu.make_async_remote_copy(src_ref=x_ref, dst_ref=gath_ref.at[me], send_sem=send_sems.at[j],
                                              recv_sem=recv_sems.at[j], device_id=peer, device_id_type=MESH)
            cp.start()
            sends.append(cp)
        for j, (fx, fy, fc) in enumerate(flips):
            peer = (x ^ fx, y ^ fy, c ^ fc)
            pltpu.make_async_remote_copy(src_ref=x_ref, dst_ref=gath_ref.at[4 * peer[0] + 2 * peer[1] + peer[2]],
                                         send_sem=send_sems.at[j], recv_sem=recv_sems.at[j], device_id=peer,
                                         device_id_type=MESH).wait_recv()
        for cp in sends:
            cp.wait_send()
        total = gath_ref[0]
        for d in range(1, 8):
            total = total + gath_ref[d]
        o_ref[...] = total

    vm = pl.BlockSpec(memory_space=pltpu.VMEM)
    out, _ = pl.pallas_call(
        body, name=name,
        in_specs=[vm], out_specs=[vm, vm],
        out_shape=[jax.ShapeDtypeStruct((R, C), F32), jax.ShapeDtypeStruct((8, R, C), F32)],
        scratch_shapes=[pltpu.SemaphoreType.DMA((7,)), pltpu.SemaphoreType.DMA((7,))],
        compiler_params=pltpu.CompilerParams(has_side_effects=True),
    )(v)
    return out


ROW_ALIGN = 16
PACK_W = 1024
SUM_TILE = 512
BIG_WEIGHTS = (("w_in", 1), ("w_a", 1), ("w_b", 1), ("w_o", 0), ("w_ff1", 1), ("w_ff2", 0), ("w_pg", 0), ("w_p", 1))


def _b_layout(d, ple):
    hw, q = d // 2, d // 4
    small = 2 * d + 2 * q
    lay = {"w_ff1": (0, 0, d, d), "w_ff2": (d, 0, d, d), "w_o": (2 * d, 0, q, d), "w_pg": (2 * d + q, 0, q, d),
           "w_a": (small, 0, hw, q), "w_b": (small, q, hw, q), "w_p": (small, 2 * q, ple, q)}
    return lay, small + hw


def pack_a(w_in_shard):
    rows, cols = w_in_shard.shape
    pad = -cols % LANES
    return jnp.concatenate([w_in_shard, jnp.zeros((rows, pad), w_in_shard.dtype)], axis=1)


def pack_b(shards, d):
    hw, q = d // 2, d // 4
    dt = shards["w_a"].dtype
    wp = shards["w_p"]
    wp = jnp.concatenate([wp, jnp.zeros((hw - wp.shape[0], q), dt)], axis=0)
    small = jnp.concatenate([shards["w_a"], shards["w_b"], wp, jnp.zeros((hw, d - 3 * q), dt)], axis=1)
    return jnp.concatenate([shards["w_ff1"], shards["w_ff2"], shards["w_o"], shards["w_pg"], small], axis=0)


def unpack_b(buf, lay):
    return {nm: buf[r0:r0 + rows, c0:c0 + cols] for nm, (r0, c0, rows, cols) in lay.items()}


def _win_layout(d):
    hw = d // 2
    fh = hw // FOX_HDIM
    orig = {"hq": (0, hw), "hf": (hw, hw), "hi": (2 * hw, hw), "hg": (3 * hw, hw), "fq": (4 * hw, hw),
            "fk": (5 * hw, hw), "fv": (6 * hw, hw), "ff": (7 * hw, fh), "ga": (7 * hw + fh, d), "gb": (7 * hw + fh + d, d)}
    order = ["ga", "gb", "hq", "hf", "hi", "hg", "fq", "fk", "fv", "ff"]
    mine, off = {}, 0
    for nm in order:
        width = orig[nm][1] if nm != "ff" else LANES
        mine[nm] = (off, width)
        off += width
    return orig, order, mine, off


def _adam_fn(rows, vecs):
    w, g, m, v = rows
    m2 = ADAM_B1 * m + (1.0 - ADAM_B1) * g
    v2 = ADAM_B2 * v + (1.0 - ADAM_B2) * (g * g)
    m_hat = m2 / (1.0 - ADAM_B1 ** ADAM_STEP)
    v_hat = v2 / (1.0 - ADAM_B2 ** ADAM_STEP)
    delta = -ADAM_LR * (m_hat / (jnp.sqrt(v_hat) + ADAM_EPS) + ADAM_WD * w)
    return [delta, m2, v2], []


def adamw_small(small, p0, ws, ms, vs, *, name):
    n = len(ws)
    hw = p0.shape[1]
    fh = ws[8].shape[1]

    def body(small_ref, p0_ref, *refs):
        w_refs, m_refs, v_refs = refs[:n], refs[n:2 * n], refs[2 * n:3 * n]
        g_out, d_out, m_out, v_out = (refs[(3 + k) * n:(4 + k) * n] for k in range(4))
        sm = small_ref[...]
        p = p0_ref[...]
        d_lb = sm[6:7, hw:2 * hw] * (p * (1.0 - p))
        grads = [sm[r:r + 1, :] for r in range(6)]
        grads += [jnp.concatenate([d_lb, -d_lb], axis=0), sm[6:7, :hw], sm[7:8, :fh]]
        for i in range(n):
            (delta, m2, v2), _ = _adam_fn([w_refs[i][...], grads[i], m_refs[i][...], v_refs[i][...]], [])
            g_out[i][...], d_out[i][...], m_out[i][...], v_out[i][...] = grads[i], delta, m2, v2

    shapes = [jax.ShapeDtypeStruct(w.shape, F32) for w in ws]
    return pl.pallas_call(body, name=name, out_shape=shapes * 4)(small, p0, *ws, *ms, *vs)


def adamw(w, g, m, v, *, name):
    c = w.shape[1]
    (delta, m2, v2), _ = rowwise(_adam_fn, [w, g, m, v], [], [(c, F32)] * 3, name=name, tm=256)
    return delta, m2, v2


def kernel(x, p, ln0_g, ln0_b, w_in, hg_lb, hg_norm_g, fox_fb, w_a, w_b, w_o, ln1_g, ln1_b, w_ff1, w_ff2, w_pg, w_p, ln2_g, ln2_b, loss_target, m_ln0_g, m_ln0_b, m_w_in, m_hg_lb, m_hg_norm_g, m_fox_fb, m_w_a, m_w_b, m_w_o, m_ln1_g, m_ln1_b, m_w_ff1, m_w_ff2, m_w_pg, m_w_p, m_ln2_g, m_ln2_b, v_ln0_g, v_ln0_b, v_w_in, v_hg_lb, v_hg_norm_g, v_fox_fb, v_w_a, v_w_b, v_w_o, v_ln1_g, v_ln1_b, v_w_ff1, v_w_ff2, v_w_pg, v_w_p, v_ln2_g, v_ln2_b):
    n_seq, seq, d = x.shape
    T = n_seq * seq
    hw = d // 2
    fh = hw // FOX_HDIM
    bh = n_seq * fh
    orig, order, mine, n_in = _win_layout(d)

    big = {"w_in": w_in[0], "w_a": w_a[0], "w_b": w_b[0], "w_o": w_o[0], "w_ff1": w_ff1[0], "w_ff2": w_ff2[0],
           "w_pg": w_pg[0], "w_p": w_p[0]}
    big_m = {"w_in": m_w_in[0], "w_a": m_w_a[0], "w_b": m_w_b[0], "w_o": m_w_o[0], "w_ff1": m_w_ff1[0],
             "w_ff2": m_w_ff2[0], "w_pg": m_w_pg[0], "w_p": m_w_p[0]}
    big_v = {"w_in": v_w_in[0], "w_a": v_w_a[0], "w_b": v_w_b[0], "w_o": v_w_o[0], "w_ff1": v_w_ff1[0],
             "w_ff2": v_w_ff2[0], "w_pg": v_w_pg[0], "w_p": v_w_p[0]}
    names = [nm for nm, _ in BIG_WEIGHTS]
    axis = dict(BIG_WEIGHTS)
    ple = w_p.shape[1]
    lay, b_rows = _b_layout(d, ple)
    in_cols = big["w_in"].shape[1]

    gather_w_in = allgather_rider([pack_a(big["w_in"].astype(BF16))])
    gather_rest = allgather_rider([pack_b({nm: big[nm].astype(BF16) for nm in names if nm != "w_in"}, d)])

    x2 = x.reshape(T, d)
    tgt = loss_target.reshape(T, d)
    p_b = p.reshape(T, p.shape[-1]).astype(BF16)
    vec = lambda a: a.reshape(1, -1)
    probs = jax.nn.softmax(hg_lb, axis=0)
    lb = vec(probs[0])

    def ln0_fn(rows, vecs):
        h = _ln_stats(rows[0]) * vecs[0] + vecs[1]
        return [h, h], []
    (h0, h0b), _, (a_all,) = rowwise(ln0_fn, [x2], [vec(ln0_g), vec(ln0_b)], [(d, F32), (d, BF16)], name="ln0_fwd",
                                     rider=gather_w_in)
    win = jnp.concatenate([a_all[s, :, :in_cols] for s in range(4)], axis=1)
    win_mine = jnp.concatenate(
        [win[:, orig[nm][0]:orig[nm][0] + orig[nm][1]] for nm in order]
        + [jnp.zeros((d, LANES - fh), BF16)], axis=1)
    proj = matmul_nn(h0b, win_mine, name="in_proj")

    o_raw, hg_states, (b_all,) = hgrn2_fwd(proj, [mine["hq"][0], mine["hf"][0], mine["hi"][0]], lb, n_seq, seq,
                                           name="hgrn2_fwd", rider=gather_rest)
    view = lambda nm, k, n: WView(b_all, lay[nm][0], lay[nm][1], k, n, axis[nm])
    w_ff1_v, w_ff2_v = view("w_ff1", d, 4 * d), view("w_ff2", 4 * d, d)

    def whole(nm):
        r0, c0, rows, cols = lay[nm]
        return jnp.concatenate([b_all[s, r0:r0 + rows, c0:c0 + cols] for s in range(4)], axis=axis[nm])
    w_o_v, w_pg_v, w_a_v, w_p_v, w_b_full = whole("w_o"), whole("w_pg"), whole("w_a"), whole("w_p"), whole("w_b")

    def ya_fn(rows, vecs):
        o, hg = rows
        outs = []
        for h in range(HG_HEADS):
            oh = o[:, h * HG_DIM:(h + 1) * HG_DIM]
            outs.append(oh * lax.rsqrt(jnp.mean(oh * oh, axis=-1, keepdims=True) + RMS_EPS))
        y = jnp.concatenate(outs, axis=1) * vecs[0] * (hg * _sigmoid(hg))
        return [y], []
    (y_a,), _ = rowwise(ya_fn, [o_raw, (proj,) + mine["hg"]], [hg_norm_g], [(hw, BF16)], name="hgrn2_out_fwd")

    fb_pad = jnp.concatenate([fox_fb, jnp.zeros((1, LANES - fh), F32)], axis=1)

    def lf_fn(rows, vecs):
        u = rows[0] + vecs[0]
        return [jnp.minimum(u, 0.0) - jnp.log(1.0 + jnp.exp(-jnp.abs(u)))], []
    (lf,), _ = rowwise(lf_fn, [(proj,) + mine["ff"]], [fb_pad], [(LANES, F32)], name="fox_logf")
    c_cum = seq_cumsum(lf, n_seq, seq, reverse=False, name="fox_cumsum")

    place = _fox_placement(fh)

    def prep_fn(rows, vecs):
        fq_, fk_, fv_, cc = rows
        pq, pk, aq, ak, oq, ok = vecs
        parts = jnp.concatenate(_split3(cc), axis=1)
        mm = lambda a_, b_: jnp.dot(a_, b_, preferred_element_type=F32)
        q_ = mm(fq_.astype(BF16), pq) + mm(parts, aq) + oq
        k_ = mm(fk_.astype(BF16), pk) + mm(parts, ak) + ok
        return [q_, k_, mm(fv_.astype(BF16), pk)], []
    wa = fh * FOX_AUG
    (qa, ka, va), _ = rowwise(prep_fn, [(proj,) + mine["fq"], (proj,) + mine["fk"], (proj,) + mine["fv"], c_cum],
                              [place[nm] for nm in ("pq", "pk", "aq", "ak", "oq", "ok")], [(wa, BF16)] * 3,
                              name="fox_prep")
    as_seq = lambda t2d: t2d.reshape(n_seq, seq, t2d.shape[1])
    o_fox, ox_fox, lse = fox_fwd(as_seq(qa), as_seq(ka), as_seq(va), name="fox_fwd")
    y_b = o_fox.reshape(T, wa)
    wb_pad = jnp.concatenate([w_b_full.reshape(fh, FOX_HDIM, d), jnp.zeros((fh, FOX_AUG - FOX_HDIM, d), BF16)],
                             axis=1).reshape(wa, d)

    fused_tm = 512
    pa = matmul_nn(y_a, w_a_v, name="proj_a")

    def merge_post(pb_, aux, vecs):
        ga, gb, a = aux
        return [_sigmoid(ga) * a + _sigmoid(gb) * pb_, pb_], []
    (merged, pb), _ = matmul_nn(y_b, wb_pad, name="proj_b_merge", tm=fused_tm, post=merge_post,
                                post_aux=[(proj,) + mine["ga"], (proj,) + mine["gb"], pa], post_outs=[BF16, F32])

    def ln1_post(mix, aux, vecs):
        z = ALPHA * aux[0] + mix
        h = _ln_stats(z) * vecs[0] + vecs[1]
        return [z, h, h], []
    (z1, h1, h1b), _ = matmul_nn(merged, w_o_v, name="out_proj_ln1", tm=fused_tm, post=ln1_post, post_aux=[h0],
                                 post_vecs=[ln1_g, ln1_b], post_outs=[F32, F32, BF16])

    relu2 = lambda u: jnp.square(jnp.maximum(u, 0.0))
    act = matmul_nn(h1b, w_ff1_v, name="ff1", out_dtype=BF16, epilogue=relu2)
    pg = matmul_nn(h1b, w_pg_v, name="ple_gate")
    pe = matmul_nn(p_b, w_p_v, name="ple_embed")

    def head_post(ffv, aux, vecs):
        h1v, pgv, pev, t = aux
        g2, b2 = vecs
        sp = _sigmoid(pgv)
        z = ALPHA * h1v + ffv + sp * pev
        y = _ln_stats(z) * g2 + b2
        err = y - t
        loss_rows = 0.5 * jnp.mean(err * err, axis=-1, keepdims=True)
        dy = err * (1.0 / d)
        dz, dg2, db2 = _ln_bwd(z, dy, g2)
        loss_acc = jnp.broadcast_to(_colsum(loss_rows), (1, d))
        return [dz, dz, dz * pev * (sp * (1.0 - sp)), dz * sp], [dg2, db2, loss_acc]
    (dz2, dz2b, dpg, dpe), (g_ln2_g, g_ln2_b, loss_part) = matmul_nn(
        act, w_ff2_v, name="ff2_head", tm=fused_tm, post=head_post, post_aux=[h1, pg, pe, tgt],
        post_vecs=[ln2_g, ln2_b], post_outs=[F32, BF16, BF16, BF16], post_accs=[d, d, d])

    dact = lambda da, a: da * (2.0 * jnp.sqrt(a.astype(F32)))
    du = matmul_nn(dz2b, w_ff2_v, transpose_rhs=True, name="d_ff2", out_dtype=BF16, epilogue=dact, aux=act)
    dh1_pg = matmul_nn(dpg, w_pg_v, transpose_rhs=True, name="d_ple_gate")

    def ln1_bwd_post(dh1_ff, aux, vecs):
        dh1 = ALPHA * aux[0] + dh1_ff + aux[1]
        dz, dg, db = _ln_bwd(aux[2], dh1, vecs[0])
        return [dz, dz], [dg, db]
    (dz1, dz1b), (g_ln1_g, g_ln1_b) = matmul_nn(
        du, w_ff1_v, transpose_rhs=True, name="d_ff1_ln1", tm=fused_tm, post=ln1_bwd_post, post_aux=[dz2, dh1_pg, z1],
        post_vecs=[ln1_g], post_outs=[F32, BF16], post_accs=[d, d])

    def merge_bwd_post(dm, aux, vecs):
        ga, gb, a, b = aux
        sa, sb = _sigmoid(ga), _sigmoid(gb)
        return [dm * a * (sa * (1.0 - sa)), dm * b * (sb * (1.0 - sb)), dm * sa, dm * sb], []
    (dga, dgb, dma, dmb), _ = matmul_nn(
        dz1b, w_o_v, transpose_rhs=True, name="d_out_proj_merge", tm=fused_tm, post=merge_bwd_post,
        post_aux=[(proj,) + mine["ga"], (proj,) + mine["gb"], pa, pb], post_outs=[BF16] * 4)
    dya = matmul_nn(dma, w_a_v, transpose_rhs=True, name="d_proj_a")
    dyb = matmul_nn(dmb, wb_pad, transpose_rhs=True, name="d_proj_b", out_dtype=BF16)

    def ya_bwd_fn(rows, vecs):
        o, hg, dy = rows
        ng = vecs[0]
        sg = _sigmoid(hg)
        gate = hg * sg
        dn_parts, do_parts, n_parts = [], [], []
        for h in range(HG_HEADS):
            hs = slice(h * HG_DIM, (h + 1) * HG_DIM)
            oh = o[:, hs]
            r = lax.rsqrt(jnp.mean(oh * oh, axis=-1, keepdims=True) + RMS_EPS)
            nh = oh * r
            dn = dy[:, hs] * ng[:, hs] * gate[:, hs]
            do_parts.append(r * (dn - nh * jnp.mean(dn * nh, axis=-1, keepdims=True)))
            n_parts.append(nh)
        nrm = jnp.concatenate(n_parts, axis=1)
        dhg = dy * nrm * ng * (sg * (1.0 + hg * (1.0 - sg)))
        return [jnp.concatenate(do_parts, axis=1), dhg], [_colsum(dy * nrm * gate)]
    (do_raw, dhg), (g_norm_g,) = rowwise(ya_bwd_fn, [o_raw, (proj,) + mine["hg"], dya], [hg_norm_g],
                                         [(hw, F32), (hw, BF16)], [hw], name="hgrn2_out_bwd")
    dhq, dhf, dhi, g_lb = hgrn2_bwd(proj, [mine["hq"][0], mine["hf"][0], mine["hi"][0]], lb, do_raw, hg_states,
                                    n_seq, seq, name="hgrn2_bwd")

    do_fox = as_seq(dyb)
    dqa, dka, dva, dsum = fox_bwd(as_seq(qa), as_seq(ka), as_seq(va), do_fox, ox_fox, lse, name="fox_bwd")

    def unprep_fn(rows, vecs):
        mm = lambda a_, b_: jnp.dot(a_.astype(BF16), b_, preferred_element_type=F32)
        return [mm(rows[0], vecs[0]), mm(rows[1], vecs[1]), mm(rows[2], vecs[1])], []
    (dfq, dfk, dfv), _ = rowwise(unprep_fn, [dqa.reshape(T, wa), dka.reshape(T, wa), dva.reshape(T, wa)],
                                 [place["pqt"], place["pkt"]], [(hw, BF16)] * 3, name="fox_unprep")
    dc = -dsum.reshape(n_seq, fh, seq).transpose(0, 2, 1).reshape(T, fh)
    dc = jnp.concatenate([dc, jnp.zeros((T, LANES - fh), F32)], axis=1)
    dlf = seq_cumsum(dc, n_seq, seq, reverse=True, name="fox_cumsum_bwd")

    def lf_bwd_fn(rows, vecs):
        u = rows[0] + vecs[0]
        du_ = rows[1] * _sigmoid(-u)
        return [du_], [_colsum(du_)]
    (dff_,), (g_fb,) = rowwise(lf_bwd_fn, [(proj,) + mine["ff"], dlf], [fb_pad], [(LANES, BF16)], [LANES],
                               name="fox_logf_bwd")

    dproj = jnp.concatenate([dga, dgb, dhq, dhf, dhi, dhg, dfq, dfk, dfv, dff_], axis=1)

    grads_b = jnp.zeros((4, b_rows, d), F32)
    for nm, lhs, rhs in (("w_ff1", h1b, du), ("w_ff2", act, dz2b)):
        grads_b = matmul_tn(lhs, rhs, name="g_" + nm, into=(grads_b, lay[nm][0], lay[nm][1], axis[nm]))
    gfull = {
        "w_a": matmul_tn(y_a, dma, name="g_w_a"),
        "w_b": matmul_tn(y_b, dmb, name="g_w_b").reshape(fh, FOX_AUG, d)[:, :FOX_HDIM].reshape(hw, d),
        "w_o": matmul_tn(merged, dz1b, name="g_w_o"),
        "w_pg": matmul_tn(h1b, dpg, name="g_w_pg"),
        "w_p": matmul_tn(p_b, dpe, name="g_w_p"),
    }

    def chip_parts(nm, s):
        g = gfull[nm]
        n = g.shape[axis[nm]] // 4
        return lax.slice_in_dim(g, s * n, (s + 1) * n, axis=axis[nm])
    for nm in gfull:
        grads_b = lax.dynamic_update_slice(grads_b, jnp.stack([chip_parts(nm, s) for s in range(4)]),
                                           (0, lay[nm][0], lay[nm][1]))
    me = 2 * lax.axis_index("x") + lax.axis_index("y")
    core = lax.axis_index("c")

    def sum2_fn(rows, vecs):
        s = rows[0] + rows[1].astype(F32)
        return [s, s], []

    def sum4_fn(rows, vecs):
        a, r0, r1, r2 = rows
        return [((a + r0.astype(F32)) + r1.astype(F32)) + r2.astype(F32)], []

    def chip_pair_sum(g, tag):
        h, cols = g.shape[1] // 2, g.shape[2]
        keep = lax.dynamic_slice_in_dim(g, core * h, h, axis=1)
        give = lax.dynamic_slice_in_dim(g, (1 - core) * h, h, axis=1).astype(BF16)
        (from_core,) = swap_cores([give], name="swap_partials_" + tag)
        (s32, s16), _ = rowwise(sum2_fn, [keep.reshape(4 * h, cols), from_core.reshape(4 * h, cols)], [],
                                [(cols, F32), (cols, BF16)], name="sum_cores_" + tag, tm=SUM_TILE)
        return s32.reshape(4, h, cols), s16.reshape(4, h, cols)

    def chip_sum(pr, gt, tag):
        own = lax.dynamic_index_in_dim(pr, me, axis=0, keepdims=False)
        (q,), _ = rowwise(sum4_fn, [own, gt[0], gt[1], gt[2]], [], [(own.shape[1], F32)], name="sum_chips_" + tag,
                          tm=SUM_TILE)
        return q

    pair_rest, pair_rest_b = chip_pair_sum(grads_b, "rest")
    gw_in_mine, (got_rest,) = matmul_tn(h0b, dproj, name="g_w_in", rider=scatter_rider([pair_rest_b]))
    gfull["w_in"] = jnp.concatenate([gw_in_mine[:, mine[nm][0]:mine[nm][0] + orig[nm][1]]
                                     for nm in ["hq", "hf", "hi", "hg", "fq", "fk", "fv", "ff", "ga", "gb"]], axis=1)
    grads_a = jnp.stack([pack_a(chip_parts("w_in", s)) for s in range(4)])
    pair_in, pair_in_b = chip_pair_sum(grads_a, "w_in")
    def ln0_bwd_post(dh0_in, aux, vecs):
        dx, dg, db = _ln_bwd(aux[1], dh0_in + ALPHA * aux[0], vecs[0])
        return [dx], [dg, db]
    ((dx,), (g_ln0_g, g_ln0_b)), (got_in,) = matmul_nn(
        dproj, win_mine, transpose_rhs=True, name="d_in_proj_ln0", tm=fused_tm, post=ln0_bwd_post,
        post_aux=[dz1, x2], post_vecs=[vec(ln0_g)], post_outs=[F32], post_accs=[d, d],
        rider=scatter_rider([pair_in_b]))
    q_half = [chip_sum(pair_in, got_in, "w_in"), chip_sum(pair_rest, got_rest, "rest")]
    q_other = swap_cores(q_half, name="swap_halves")
    g_a, g_b = [jnp.concatenate([jnp.where(core == 0, mine_, other), jnp.where(core == 0, other, mine_)], axis=0)
                for mine_, other in zip(q_half, q_other)]
    g_shards = unpack_b(g_b, lay)
    g_shards["w_in"] = g_a[:, :in_cols]

    assert d == PACK_W and 2 * hw == PACK_W and fh <= LANES
    small = allreduce_small(jnp.concatenate(
        [g_ln0_g, g_ln0_b, g_ln1_g, g_ln1_b, g_ln2_g, g_ln2_b, jnp.concatenate([g_norm_g, g_lb], axis=1),
         jnp.concatenate([g_fb, loss_part[:, LANES:]], axis=1)], axis=0), name="allreduce_small")
    loss = small[7, LANES]

    small_w = [vec(ln0_g), vec(ln0_b), ln1_g, ln1_b, ln2_g, ln2_b, hg_lb, hg_norm_g, fox_fb]
    small_m = [vec(m_ln0_g), vec(m_ln0_b), m_ln1_g, m_ln1_b, m_ln2_g, m_ln2_b, m_hg_lb, m_hg_norm_g, m_fox_fb]
    small_v = [vec(v_ln0_g), vec(v_ln0_b), v_ln1_g, v_ln1_b, v_ln2_g, v_ln2_b, v_hg_lb, v_hg_norm_g, v_fox_fb]
    small_out = adamw_small(small, probs[0:1], small_w, small_m, small_v, name="adamw_small")
    small_shapes = [ln0_g.shape, ln0_b.shape, ln1_g.shape, ln1_b.shape, ln2_g.shape, ln2_b.shape, hg_lb.shape,
                    hg_norm_g.shape, fox_fb.shape]
    sg_out, sd_out, sm_out, sv_out = [[a.reshape(shp) for a, shp in zip(small_out[9 * k:9 * k + 9], small_shapes)]
                                      for k in range(4)]

    big_out = {}
    for nm in names:
        delta, m2, v2 = adamw(big[nm], g_shards[nm], big_m[nm], big_v[nm], name="adamw_" + nm)
        big_out[nm] = (g_shards[nm][None], delta[None], m2[None], v2[None])

    def ordered(k):
        sm_ = [sg_out, sd_out, sm_out, sv_out][k]
        bg = lambda nm: big_out[nm][k]
        return [sm_[0], sm_[1], bg("w_in"), sm_[6], sm_[7], sm_[8], bg("w_a"), bg("w_b"), bg("w_o"), sm_[2], sm_[3],
                bg("w_ff1"), bg("w_ff2"), bg("w_pg"), bg("w_p"), sm_[4], sm_[5]]
    grad_x = dx.reshape(n_seq, seq, d)
    return (loss, grad_x, *ordered(0), *ordered(1), *ordered(2), *ordered(3))
```

```python
import functools
from typing import NamedTuple, Optional

import numpy as np
import jax
import jax.numpy as jnp
from jax import lax
from jax.experimental import pallas as pl
from jax.experimental.pallas import tpu as pltpu

F32 = jnp.float32
BF16 = jnp.bfloat16
MESH = pl.DeviceIdType.MESH

VMEM_LIMIT_BYTES = 48 * 1024 * 1024
LANES = 128
HG_HEADS = 4
HG_DIM = 128
HG_BLK = 16
HG_TILE = 256
HG_SLOTS = 16
FOX_HDIM = 64
FOX_AUG = 128
FOX_TQ = 1024
FOX_FWD_HEADS = 1
LN_EPS = 1e-5
RMS_EPS = 1e-6
DEPTH = 1
ALPHA = (2.0 * DEPTH) ** 0.25
ADAM_LR, ADAM_B1, ADAM_B2, ADAM_EPS, ADAM_WD, ADAM_STEP = 0.001, 0.9, 0.999, 1e-08, 0.01, 10
NEG_INF = -1e30


def _cparams(sem):
    return pltpu.CompilerParams(dimension_semantics=sem, vmem_limit_bytes=VMEM_LIMIT_BYTES)


def _tile(n, cap):
    if n <= cap:
        return n
    best = None
    for t in range(LANES, cap + 1, LANES):
        if n % t == 0:
            best = t
    assert best is not None, (n, cap)
    return best


class WView(NamedTuple):
    arr: jax.Array
    r0: int
    c0: int
    k: int
    n: int
    split: Optional[int]


def matmul_nn(a, w, *, name, transpose_rhs=False, out_dtype=F32, epilogue=None, aux=None, tm=2048, rider=None,
              post=None, post_aux=(), post_vecs=(), post_outs=(), post_accs=()):
    wv = w if isinstance(w, WView) else WView(w[None], 0, 0, w.shape[0], w.shape[1], None)
    rows_s = wv.k // 4 if wv.split == 0 else wv.k
    cols_s = wv.n // 4 if wv.split == 1 else wv.n
    tr, tc = _tile(rows_s, 1152), _tile(cols_s, 1152)
    assert wv.r0 % tr == 0 and wv.c0 % tc == 0
    T, K = a.shape
    N, tn, tk = (wv.k, tr, tc) if transpose_rhs else (wv.n, tc, tr)
    assert K == (wv.n if transpose_rhs else wv.k)
    tm = min(tm, T)
    assert T % tm == 0
    nk = K // tk

    def w_block(ri, ci):
        if wv.split == 0:
            return (ri * tr) // rows_s, (wv.r0 + (ri * tr) % rows_s) // tr, wv.c0 // tc + ci
        if wv.split == 1:
            return (ci * tc) // cols_s, wv.r0 // tr + ri, (wv.c0 + (ci * tc) % cols_s) // tc
        return 0, wv.r0 // tr + ri, wv.c0 // tc + ci

    fused = post is not None
    assert not fused or N == tn
    aux_list = list(post_aux) if fused else ([aux] if aux is not None else [])
    aux_list = [x if isinstance(x, tuple) else (x, 0, x.shape[1]) for x in aux_list]
    vec_list = list(post_vecs)
    out_dtypes = list(post_outs) if fused else [out_dtype]
    n_aux, n_vec, n_out, n_acc = len(aux_list), len(vec_list), len(out_dtypes), len(post_accs)

    def body(*refs):
        a_ref, w_ref = refs[:2]
        aux_refs = refs[2:2 + n_aux]
        vec_refs = refs[2 + n_aux:2 + n_aux + n_vec]
        out_refs = refs[2 + n_aux + n_vec:2 + n_aux + n_vec + n_out]
        sum_refs = refs[2 + n_aux + n_vec + n_out:2 + n_aux + n_vec + n_out + n_acc]
        acc_ref = refs[-1]
        m, k = pl.program_id(1), pl.program_id(2)
        if transpose_rhs:
            part = lax.dot_general(a_ref[...], w_ref[...], (((1,), (1,)), ((), ())), preferred_element_type=F32)
        else:
            part = jnp.dot(a_ref[...], w_ref[...], preferred_element_type=F32)

        def write(res):
            if not fused:
                if epilogue is not None:
                    res = epilogue(res) if not aux_refs else epilogue(res, aux_refs[0][...])
                out_refs[0][...] = res.astype(out_dtype)
                return
            outs, sums = post(res, [r[...] for r in aux_refs], [v[...] for v in vec_refs])
            assert len(outs) == n_out and len(sums) == n_acc
            for r, val in zip(out_refs, outs):
                r[...] = val.astype(r.dtype)
            for r, val in zip(sum_refs, sums):
                def first_rows(r=r, val=val):
                    r[...] = val

                def later_rows(r=r, val=val):
                    r[...] += val
                pl.when(m == 0)(first_rows)
                pl.when(m > 0)(later_rows)

        if nk == 1:
            write(part)
        else:
            @pl.when(k == 0)
            def _():
                acc_ref[...] = part

            @pl.when(k > 0)
            def _():
                acc_ref[...] += part

            @pl.when(k == nk - 1)
            def _():
                write(acc_ref[...])

    w_index = (lambda n, m, k: w_block(n, k)) if transpose_rhs else (lambda n, m, k: w_block(k, n))
    in_specs = [pl.BlockSpec((tm, tk), lambda n, m, k: (m, k)),
                pl.BlockSpec((None, tr, tc), w_index)]
    args = [a, wv.arr]
    for arr, off, width in aux_list:
        assert width == N and off % tn == 0
        in_specs.append(pl.BlockSpec((tm, tn), functools.partial(lambda n, m, k, blk: (m, blk + n), blk=off // tn)))
        args.append(arr)
    for v in vec_list:
        in_specs.append(pl.BlockSpec(v.shape, lambda n, m, k: (0, 0)))
        args.append(v)
    out_specs = [pl.BlockSpec((tm, tn), lambda n, m, k: (m, n)) for _ in out_dtypes]
    out_specs += [pl.BlockSpec((1, tn), lambda n, m, k: (0, 0)) for _ in post_accs]
    out_shape = [jax.ShapeDtypeStruct((T, N), dt) for dt in out_dtypes]
    out_shape += [jax.ShapeDtypeStruct((1, N), F32) for _ in post_accs]
    scratch = [pltpu.VMEM((tm, tn) if nk > 1 else (8, LANES), F32)]
    grid = (N // tn, T // tm, nk)
    sem = ("arbitrary",) * 3 if (n_acc or rider is not None) else ("parallel", "parallel", "arbitrary")
    params = pltpu.CompilerParams(dimension_semantics=sem, vmem_limit_bytes=VMEM_LIMIT_BYTES,
                                  has_side_effects=rider is not None)
    if rider is not None:
        r_in, r_out, r_sems = rider.specs()
        body = rider.wrap(body, len(in_specs), len(out_specs), 3)
        in_specs, out_specs, out_shape = in_specs + r_in, out_specs + r_out, out_shape + rider.out_shape
        scratch, args = scratch + r_sems, args + list(rider.ins)
    res = pl.pallas_call(body, name=name, grid=grid, in_specs=in_specs, out_specs=out_specs, out_shape=out_shape,
                         scratch_shapes=scratch, compiler_params=params)(*args)
    main = (list(res[:n_out]), list(res[n_out:n_out + n_acc])) if fused else res[0]
    return main if rider is None else (main, list(res[n_out + n_acc:]))


def matmul_tn(a, b, *, name, tk=2048, rider=None, into=None):
    T, M = a.shape
    T2, N = b.shape
    tk = min(tk, T)
    assert T == T2 and T % tk == 0
    if into is not None:
        assert rider is None
        buf, r0, c0, split = into
        rows_s, cols_s = (M // 4, N) if split == 0 else (M, N // 4)
        tm, tn = _tile(rows_s, 1024), _tile(cols_s, 1152)
        assert r0 % tm == 0 and c0 % tn == 0

        def part_block(m, n, k):
            if split == 0:
                return (m * tm) // rows_s, (r0 + (m * tm) % rows_s) // tm, c0 // tn + n
            return (n * tn) // cols_s, r0 // tm + m, (c0 + (n * tn) % cols_s) // tn

        def body_into(a_ref, b_ref, buf_ref, o_ref):
            k = pl.program_id(2)
            part = lax.dot_general(a_ref[...], b_ref[...], (((0,), (0,)), ((), ())), preferred_element_type=F32)

            @pl.when(k == 0)
            def _():
                o_ref[...] = part

            @pl.when(k > 0)
            def _():
                o_ref[...] += part

        return pl.pallas_call(
            body_into, name=name, grid=(M // tm, N // tn, T // tk),
            in_specs=[pl.BlockSpec((tk, tm), lambda m, n, k: (k, m)), pl.BlockSpec((tk, tn), lambda m, n, k: (k, n)),
                      pl.BlockSpec(memory_space=pl.ANY)],
            out_specs=pl.BlockSpec((None, tm, tn), part_block),
            out_shape=jax.ShapeDtypeStruct(buf.shape, buf.dtype), input_output_aliases={2: 0},
            compiler_params=_cparams(("parallel", "parallel", "arbitrary")))(a, b, buf)
    tm = _tile(M, 1024)
    tn = _tile(N, 1152)

    def body(a_ref, b_ref, o_ref):
        k = pl.program_id(2)
        part = lax.dot_general(a_ref[...], b_ref[...], (((0,), (0,)), ((), ())), preferred_element_type=F32)

        @pl.when(k == 0)
        def _():
            o_ref[...] = part

        @pl.when(k > 0)
        def _():
            o_ref[...] += part

    in_specs = [pl.BlockSpec((tk, tm), lambda m, n, k: (k, m)), pl.BlockSpec((tk, tn), lambda m, n, k: (k, n))]
    out_specs = [pl.BlockSpec((tm, tn), lambda m, n, k: (m, n))]
    out_shape = [jax.ShapeDtypeStruct((M, N), F32)]
    grid = (M // tm, N // tn, T // tk)
    if rider is None:
        return pl.pallas_call(body, name=name, grid=grid, in_specs=in_specs, out_specs=out_specs, out_shape=out_shape,
                              compiler_params=_cparams(("parallel", "parallel", "arbitrary")))(a, b)[0]
    r_in, r_out, r_sems = rider.specs()
    res = pl.pallas_call(
        rider.wrap(body, 2, 1, 3), name=name, grid=grid, in_specs=in_specs + r_in, out_specs=out_specs + r_out,
        out_shape=out_shape + rider.out_shape, scratch_shapes=r_sems,
        compiler_params=pltpu.CompilerParams(dimension_semantics=("arbitrary",) * 3,
                                             vmem_limit_bytes=VMEM_LIMIT_BYTES, has_side_effects=True),
    )(a, b, *rider.ins)
    return res[0], list(res[1:])


def rowwise(fn, rows, vecs, outs, accs=(), *, name, tm=1024, rider=None):
    rows = [r if isinstance(r, tuple) else (r, 0, r.shape[1]) for r in rows]
    T = rows[0][0].shape[0]
    tm = min(tm, T)
    assert T % tm == 0
    n_rows, n_vecs, n_outs, n_accs = len(rows), len(vecs), len(outs), len(accs)

    def body(*refs):
        row_refs = refs[:n_rows]
        vec_refs = refs[n_rows:n_rows + n_vecs]
        out_refs = refs[n_rows + n_vecs:n_rows + n_vecs + n_outs]
        acc_refs = refs[n_rows + n_vecs + n_outs:]
        out_vals, acc_vals = fn([r[...] for r in row_refs], [v[...] for v in vec_refs])
        assert len(out_vals) == n_outs and len(acc_vals) == n_accs
        for r, val in zip(out_refs, out_vals):
            r[...] = val.astype(r.dtype)
        if n_accs:
            i = pl.program_id(0)

            @pl.when(i == 0)
            def _():
                for r in acc_refs:
                    r[...] = jnp.zeros_like(r)

            for r, val in zip(acc_refs, acc_vals):
                r[...] += val

    in_specs = []
    for arr, off, width in rows:
        assert off % width == 0
        in_specs.append(pl.BlockSpec((tm, width), functools.partial(lambda i, blk: (i, blk), blk=off // width)))
    for v in vecs:
        in_specs.append(pl.BlockSpec(v.shape, lambda i: (0, 0)))
    out_specs = [pl.BlockSpec((tm, w), lambda i: (i, 0)) for w, _ in outs]
    out_specs += [pl.BlockSpec((1, w), lambda i: (0, 0)) for w in accs]
    out_shape = [jax.ShapeDtypeStruct((T, w), dt) for w, dt in outs]
    out_shape += [jax.ShapeDtypeStruct((1, w), F32) for w in accs]
    args = [r[0] for r in rows] + list(vecs)
    if rider is None:
        res = pl.pallas_call(body, name=name, grid=(T // tm,), in_specs=in_specs, out_specs=out_specs,
                             out_shape=out_shape,
                             compiler_params=_cparams(("arbitrary",) if n_accs else ("parallel",)))(*args)
        return res[:n_outs], res[n_outs:]
    r_in, r_out, r_sems = rider.specs()
    res = pl.pallas_call(
        rider.wrap(body, len(in_specs), len(out_specs), 1), name=name, grid=(T // tm,), in_specs=in_specs + r_in,
        out_specs=out_specs + r_out, out_shape=out_shape + rider.out_shape, scratch_shapes=r_sems,
        compiler_params=pltpu.CompilerParams(dimension_semantics=("arbitrary",), vmem_limit_bytes=VMEM_LIMIT_BYTES,
                                             has_side_effects=True),
    )(*args, *rider.ins)
    return res[:n_outs], res[n_outs:n_outs + n_accs], list(res[n_outs + n_accs:])


def _colsum(x):
    return jnp.sum(x, axis=0, keepdims=True)


def _sigmoid(x):
    return 1.0 / (1.0 + jnp.exp(-x))


def _ln_stats(z):
    mu = jnp.mean(z, axis=-1, keepdims=True)
    zc = z - mu
    var = jnp.mean(zc * zc, axis=-1, keepdims=True)
    return zc * lax.rsqrt(var + LN_EPS)


def _ln_bwd(zhat_src, dy, g):
    mu = jnp.mean(zhat_src, axis=-1, keepdims=True)
    zc = zhat_src - mu
    var = jnp.mean(zc * zc, axis=-1, keepdims=True)
    rstd = lax.rsqrt(var + LN_EPS)
    zh = zc * rstd
    dzh = dy * g
    dz = rstd * (dzh - jnp.mean(dzh, axis=-1, keepdims=True) - zh * jnp.mean(dzh * zh, axis=-1, keepdims=True))
    return dz, _colsum(dy * zh), _colsum(dy)


def _hg_constants():
    r = np.arange(HG_TILE)
    same = (r[:, None] // HG_BLK) == (r[None, :] // HG_BLK)
    lower = (same & (r[None, :] <= r[:, None])).astype(np.float32)
    upper = (same & (r[None, :] >= r[:, None])).astype(np.float32)
    total = same.astype(np.float32)
    c = np.arange(2 * HG_DIM)
    bd = ((c[:, None] // HG_DIM) == (c[None, :] // HG_DIM)).astype(np.float32)
    pair_t = np.array([t for t, _ in _HG_PAIRS])
    pair_s = np.array([s for _, s in _HG_PAIRS])
    sel_t = (pair_t[None, :] == np.arange(HG_BLK)[:, None]).astype(np.float32)
    sel_s = (pair_s[None, :] == np.arange(HG_BLK)[:, None]).astype(np.float32)
    as_bf = lambda m: jnp.asarray(m, dtype=BF16)
    return as_bf(lower), as_bf(upper), as_bf(total), as_bf(bd), as_bf(sel_t), as_bf(sel_s)


_HG_HALF = HG_BLK // 2
_HG_PAIRS = ([(t, s) for t in range(_HG_HALF, HG_BLK) for s in range(HG_BLK)]
             + [(t, s) for t in range(_HG_HALF) for s in range(_HG_HALF)])
HG_STACK = len(_HG_PAIRS)
_HG_SLABS = ([((t - _HG_HALF) * HG_BLK, (t,), HG_BLK) for t in range(_HG_HALF, HG_BLK)]
             + [(_HG_HALF * HG_BLK + t * _HG_HALF, (t, t + 1), _HG_HALF) for t in range(0, _HG_HALF, 2)])


def _stack_by_s(x):
    return jnp.concatenate([x] * _HG_HALF + [x[:_HG_HALF]] * _HG_HALF, axis=0)


def _stack_by_t(x):
    w = x.shape[1]
    return jnp.concatenate([jnp.broadcast_to(x[t:t + 1], (HG_BLK, w)) for t in range(_HG_HALF, HG_BLK)]
                           + [jnp.broadcast_to(x[t:t + 1], (_HG_HALF, w)) for t in range(_HG_HALF)], axis=0)


def _keep_bf16_bits(x):
    bits = lax.bitcast_convert_type(x, jnp.int32) & jnp.int32(-65536)
    return lax.bitcast_convert_type(bits, F32)


def _head_sums(stack_ref, slot, bd):
    pair = bd.shape[0]
    return jnp.concatenate([jnp.dot(stack_ref[slot, :, c0:c0 + pair], bd, preferred_element_type=F32)
                            for c0 in range(0, stack_ref.shape[2], pair)], axis=1)


def _split3(x):
    hi = _keep_bf16_bits(x)
    r1 = x - hi
    mid = _keep_bf16_bits(r1)
    lo = _keep_bf16_bits(r1 - mid)
    return hi.astype(BF16), mid.astype(BF16), lo.astype(BF16)


def _dot3(m01, x):
    hi, mid, lo = _split3(x)
    d = lambda p: jnp.dot(m01, p, preferred_element_type=F32)
    return (d(lo) + d(mid)) + d(hi)


def _hg_prologue(hq, hf, lb, lower, total):
    sq = _sigmoid(hq)
    q = hq * sq
    sg = _sigmoid(hf)
    f = lb + (1.0 - lb) * sg
    g = jnp.log(f)
    k = 1.0 - f
    b = _dot3(lower, g)
    bl = _dot3(total, g)
    return q, k, f, sg, sq, b, bl


def _stack16(fn):
    return [fn(t) for t in range(HG_BLK)]


def hgrn2_fwd(proj, offs, lb, n_seq, seq, *, name, rider=None):
    T = n_seq * seq
    W = HG_HEADS * HG_DIM
    n_tiles = seq // HG_TILE
    nb = HG_TILE // HG_BLK
    lower, _, total, bd, sel_t, _ = _hg_constants()

    def body(hq_ref, hf_ref, hi_ref, lb_ref, lower_ref, total_ref, bd_ref, selt_ref,
             o_ref, st_out_ref,
             st_ref, q_s, k_s, v_s, b_s, qt_s, kt_s, d_s, p_s):
        @pl.when(pl.program_id(1) == 0)
        def _():
            st_ref[...] = jnp.zeros_like(st_ref)

        q, k, _, _, _, b, bl = _hg_prologue(hq_ref[...], hf_ref[...], lb_ref[...], lower_ref[...], total_ref[...])
        q_s[...] = q
        k_s[...] = k
        v_s[...] = hi_ref[...]
        b_s[...] = b
        qt_s[...] = q * jnp.exp(b)
        kt_s[...] = k * jnp.exp(jnp.minimum(bl - b, 0.0))
        d_s[...] = jnp.exp(bl)
        rowi = lax.broadcasted_iota(jnp.int32, (HG_BLK, W), 0)

        def block(i, slot):
            r0 = pl.multiple_of(i * HG_BLK, HG_BLK)
            rows = pl.ds(r0, HG_BLK)
            qi, ki, vi, bi = q_s[rows, :], k_s[rows, :], v_s[rows, :], b_s[rows, :]
            for off, ts, n in _HG_SLABS:
                slab = [jnp.where(rowi[:n] <= t, jnp.exp(jnp.minimum(bi[t:t + 1, :] - bi[:n], 0.0)), 0.0)
                        * qi[t:t + 1, :] * ki[:n] for t in ts]
                p_s[slot, pl.ds(off, HG_BLK), :] = jnp.concatenate(slab, axis=0).astype(BF16)
            a_b = _head_sums(p_s, slot, bd_ref[...])
            o_blk = jnp.dot(selt_ref[...], (a_b * _stack_by_s(vi)).astype(BF16), preferred_element_type=F32)
            qti, kti, di = qt_s[rows, :], kt_s[rows, :], d_s[rows, :]
            outs = []
            for h in range(HG_HEADS):
                hs = slice(h * HG_DIM, (h + 1) * HG_DIM)
                st_h = st_ref[hs, :]
                st_out_ref[i, hs, :] = st_h
                outs.append(lax.dot_general(qti[:, hs].astype(BF16), st_h.astype(BF16),
                                            (((1,), (1,)), ((), ())), preferred_element_type=F32))
                upd = lax.dot_general(vi[:, hs].astype(BF16), kti[:, hs].astype(BF16),
                                      (((0,), (0,)), ((), ())), preferred_element_type=F32)
                st_ref[hs, :] = st_h * di[0:1, hs] + upd
            o_ref[rows, :] = o_blk + jnp.concatenate(outs, axis=1)

        def some_blocks(jj, carry):
            for slot in range(HG_SLOTS):
                block(HG_SLOTS * jj + slot, slot)
            return carry

        lax.fori_loop(0, nb // HG_SLOTS, some_blocks, 0)

    col = lambda off: functools.partial(lambda s, t, blk: (s * n_tiles + t, blk), blk=off // W)
    const = lambda m: pl.BlockSpec(m.shape, lambda s, t: (0, 0))
    tile_f32 = pltpu.VMEM((HG_TILE, W), F32)
    in_specs = [pl.BlockSpec((HG_TILE, W), col(offs[0])), pl.BlockSpec((HG_TILE, W), col(offs[1])),
                pl.BlockSpec((HG_TILE, W), col(offs[2])), const(lb), const(lower), const(total), const(bd),
                const(sel_t)]
    out_specs = [pl.BlockSpec((HG_TILE, W), lambda s, t: (s * n_tiles + t, 0)),
                 pl.BlockSpec((nb, W, HG_DIM), lambda s, t: (s * n_tiles + t, 0, 0))]
    out_shape = [jax.ShapeDtypeStruct((T, W), F32), jax.ShapeDtypeStruct((T // HG_BLK, W, HG_DIM), F32)]
    scratch = [pltpu.VMEM((W, HG_DIM), F32)] + [tile_f32] * 7 + [pltpu.VMEM((HG_SLOTS, HG_STACK, W), BF16)]
    args = [proj, proj, proj, lb, lower, total, bd, sel_t]
    params = _cparams(("arbitrary", "arbitrary"))
    if rider is not None:
        r_in, r_out, r_sems = rider.specs()
        body = rider.wrap(body, len(in_specs), len(out_specs), 2)
        in_specs, out_specs, out_shape = in_specs + r_in, out_specs + r_out, out_shape + rider.out_shape
        scratch, args = scratch + r_sems, args + rider.ins
        params = pltpu.CompilerParams(dimension_semantics=("arbitrary", "arbitrary"),
                                      vmem_limit_bytes=VMEM_LIMIT_BYTES, has_side_effects=True)
    res = pl.pallas_call(body, name=name, grid=(n_seq, n_tiles), in_specs=in_specs, out_specs=out_specs,
                         out_shape=out_shape, scratch_shapes=scratch, compiler_params=params)(*args)
    return res[0], res[1], list(res[2:])


def hgrn2_bwd(proj, offs, lb, do, states, n_seq, seq, *, name):
    T = n_seq * seq
    W = HG_HEADS * HG_DIM
    n_tiles = seq // HG_TILE
    nb = HG_TILE // HG_BLK
    lower, upper, total, bd, sel_t, sel_s = _hg_constants()

    def body(hq_ref, hf_ref, hi_ref, do_ref, st_in_ref, lb_ref, lower_ref, upper_ref, total_ref, bd_ref,
             selt_ref, sels_ref,
             dhq_ref, dhf_ref, dhi_ref, dlb_ref,
             dst_ref, q_s, k_s, v_s, b_s, qt_s, kt_s, d_s, eb_s, ekb_s, dq_s, dk_s, db_s, dv_s,
             p_s, e_s, w_s):
        first = jnp.logical_and(pl.program_id(0) == 0, pl.program_id(1) == 0)

        @pl.when(first)
        def _():
            dlb_ref[...] = jnp.zeros_like(dlb_ref)

        @pl.when(pl.program_id(1) == 0)
        def _():
            dst_ref[...] = jnp.zeros_like(dst_ref)

        hq, lbv = hq_ref[...], lb_ref[...]
        q, k, f, sg, sq, b, bl = _hg_prologue(hq, hf_ref[...], lbv, lower_ref[...], total_ref[...])
        eb = jnp.exp(b)
        ekb = jnp.exp(jnp.minimum(bl - b, 0.0))
        q_s[...] = q
        k_s[...] = k
        v_s[...] = hi_ref[...]
        b_s[...] = b
        eb_s[...] = eb
        ekb_s[...] = ekb
        qt_s[...] = q * eb
        kt_s[...] = k * ekb
        d_s[...] = jnp.exp(bl)
        rowi = lax.broadcasted_iota(jnp.int32, (HG_BLK, W), 0)
        last_row = rowi == HG_BLK - 1

        def block(i, slot):
            r0 = pl.multiple_of(i * HG_BLK, HG_BLK)
            rows = pl.ds(r0, HG_BLK)
            qi, ki, vi, bi, doi = q_s[rows, :], k_s[rows, :], v_s[rows, :], b_s[rows, :], do_ref[rows, :]
            for off, ts, n in _HG_SLABS:
                es = [jnp.where(rowi[:n] <= t, jnp.exp(jnp.minimum(bi[t:t + 1, :] - bi[:n], 0.0)), 0.0) for t in ts]
                sl = pl.ds(off, HG_BLK)
                e_s[slot, sl, :] = jnp.concatenate(es, axis=0)
                p_s[slot, sl, :] = jnp.concatenate([e * qi[t:t + 1, :] * ki[:n] for e, t in zip(es, ts)],
                                                   axis=0).astype(BF16)
                w_s[slot, sl, :] = jnp.concatenate([doi[t:t + 1, :] * vi[:n] for t in ts], axis=0).astype(BF16)
            a_b = _head_sums(p_s, slot, bd_ref[...])
            da_b = _head_sums(w_s, slot, bd_ref[...])
            x = da_b * e_s[slot]
            dq_in = jnp.dot(selt_ref[...], (x * _stack_by_s(ki)).astype(BF16), preferred_element_type=F32)
            dk_in = jnp.dot(sels_ref[...], (x * _stack_by_t(qi)).astype(BF16), preferred_element_type=F32)
            dv_in = jnp.dot(sels_ref[...], (a_b * _stack_by_t(doi)).astype(BF16), preferred_element_type=F32)
            qti, kti, di = qt_s[rows, :], kt_s[rows, :], d_s[rows, :]
            dqt, dkt, dvt, dd = [], [], [], []
            for h in range(HG_HEADS):
                hs = slice(h * HG_DIM, (h + 1) * HG_DIM)
                st_h = st_in_ref[i, hs, :]
                dst_h = dst_ref[hs, :]
                do_h, v_h = doi[:, hs].astype(BF16), vi[:, hs].astype(BF16)
                dst_b = dst_h.astype(BF16)
                dqt.append(jnp.dot(do_h, st_h.astype(BF16), preferred_element_type=F32))
                dkt.append(jnp.dot(v_h, dst_b, preferred_element_type=F32))
                dvt.append(lax.dot_general(kti[:, hs].astype(BF16), dst_b, (((1,), (1,)), ((), ())),
                                           preferred_element_type=F32))
                dd.append(jnp.sum(dst_h * st_h, axis=0, keepdims=True))
                upd = lax.dot_general(do_h, qti[:, hs].astype(BF16), (((0,), (0,)), ((), ())),
                                      preferred_element_type=F32)
                dst_ref[hs, :] = dst_h * di[0:1, hs] + upd
            dqt = jnp.concatenate(dqt, axis=1)
            dkt = jnp.concatenate(dkt, axis=1)
            dvt = jnp.concatenate(dvt, axis=1)
            dd = jnp.concatenate(dd, axis=1)
            dbl = jnp.sum(dkt * kti, axis=0, keepdims=True) + dd * di[0:1, :]
            db = qi * dq_in - ki * dk_in + dqt * qti - dkt * kti
            db_s[rows, :] = db + jnp.where(last_row, dbl, 0.0)
            dq_s[rows, :] = dq_in + dqt * eb_s[rows, :]
            dk_s[rows, :] = dk_in + dkt * ekb_s[rows, :]
            dv_s[rows, :] = dv_in + dvt

        def some_blocks(jj, carry):
            for slot in range(HG_SLOTS):
                block(nb - 1 - slot - HG_SLOTS * jj, slot)
            return carry

        lax.fori_loop(0, nb // HG_SLOTS, some_blocks, 0)

        dg = _dot3(upper_ref[...], db_s[...])
        dhq_ref[...] = (dq_s[...] * (sq * (1.0 + hq * (1.0 - sq)))).astype(dhq_ref.dtype)
        df = dg / f - dk_s[...]
        dhf_ref[...] = (df * (1.0 - lbv) * (sg * (1.0 - sg))).astype(dhf_ref.dtype)
        dhi_ref[...] = dv_s[...].astype(dhi_ref.dtype)
        dlb_ref[...] += _colsum(df * (1.0 - sg))

    rev = lambda s, t: s * n_tiles + (n_tiles - 1 - t)
    col = lambda off: functools.partial(lambda s, t, blk: (rev(s, t), blk), blk=off // W)
    const = lambda m: pl.BlockSpec(m.shape, lambda s, t: (0, 0))
    row = pl.BlockSpec((HG_TILE, W), lambda s, t: (rev(s, t), 0))
    tile_f32 = pltpu.VMEM((HG_TILE, W), F32)
    n2 = HG_STACK
    return pl.pallas_call(
        body, name=name,
        grid=(n_seq, n_tiles),
        in_specs=[pl.BlockSpec((HG_TILE, W), col(offs[0])), pl.BlockSpec((HG_TILE, W), col(offs[1])),
                  pl.BlockSpec((HG_TILE, W), col(offs[2])), row,
                  pl.BlockSpec((nb, W, HG_DIM), lambda s, t: (rev(s, t), 0, 0)),
                  const(lb), const(lower), const(upper), const(total), const(bd), const(sel_t), const(sel_s)],
        out_specs=[row, row, row, pl.BlockSpec((1, W), lambda s, t: (0, 0))],
        out_shape=[jax.ShapeDtypeStruct((T, W), BF16)] * 3 + [jax.ShapeDtypeStruct((1, W), F32)],
        scratch_shapes=[pltpu.VMEM((W, HG_DIM), F32)] + [tile_f32] * 13
                       + [pltpu.VMEM((HG_SLOTS, n2, W), BF16), pltpu.VMEM((HG_SLOTS, n2, W), F32),
                          pltpu.VMEM((HG_SLOTS, n2, W), BF16)],
        compiler_params=_cparams(("arbitrary", "arbitrary")),
    )(proj, proj, proj, do, states, lb, lower, upper, total, bd, sel_t, sel_s)


def _diag_mask(tq):
    return lax.broadcasted_iota(jnp.int32, (tq, tq), 1) <= lax.broadcasted_iota(jnp.int32, (tq, tq), 0)


def _qk(q, k):
    return lax.dot_general(q, k, (((1,), (1,)), ((), ())), preferred_element_type=F32)


def _causal_pairs(n, sweeps=1, by_key=False):
    if by_key:
        rows = [(i, j, 0) for j in range(n) for i in range(j, n)]
    else:
        rows = [(i, j, s) for i in range(n) for s in range(sweeps) for j in range(i + 1)]
    return tuple(jnp.asarray(np.array([r[c] for r in rows], np.int32)) for c in range(3))


def _fox_placement(fh):
    hw, wa = fh * FOX_HDIM, fh * FOX_AUG
    pq, pk = np.zeros((hw, wa), np.float32), np.zeros((hw, wa), np.float32)
    aq, ak = np.zeros((3 * LANES, wa), np.float32), np.zeros((3 * LANES, wa), np.float32)
    oq, ok = np.zeros((1, wa), np.float32), np.zeros((1, wa), np.float32)
    for h in range(fh):
        src, dst = np.arange(h * FOX_HDIM, (h + 1) * FOX_HDIM), np.arange(h * FOX_AUG, h * FOX_AUG + FOX_HDIM)
        pq[src, dst] = FOX_HDIM ** -0.5
        pk[src, dst] = 1.0
        gate = h * FOX_AUG + FOX_HDIM
        for r in range(3):
            aq[r * LANES + h, gate + r] = 1.0
            ak[r * LANES + h, gate + 3 + r] = -1.0
        oq[0, gate + 3:gate + 6] = 1.0
        ok[0, gate:gate + 3] = 1.0
    bf = lambda m: jnp.asarray(m, dtype=BF16)
    return {"pq": bf(pq), "pk": bf(pk), "aq": bf(aq), "ak": bf(ak), "oq": jnp.asarray(oq), "ok": jnp.asarray(ok),
            "pqt": bf(pq.T), "pkt": bf(pk.T)}


def _fox_specs(tq, fh, heads=1):
    groups = fh // heads

    def spec(tab):
        return pl.BlockSpec((None, tq, heads * FOX_AUG), lambda b, t, *tabs: (b // groups, tabs[tab][t], b % groups))
    return spec(0), spec(1)


def fox_fwd(qa, ka, va, *, name):
    n_seq, S, width = qa.shape
    fh = width // FOX_AUG
    nh = FOX_FWD_HEADS
    BH = n_seq * fh // nh
    tq = min(FOX_TQ, S)
    itab, jtab, _ = _causal_pairs(S // tq)

    def body(itab_ref, jtab_ref, q_ref, k_ref, v_ref, o_ref, ox_ref, lse_ref, *scratch):
        t = pl.program_id(1)
        i, j = itab_ref[t], jtab_ref[t]
        per_head = [scratch[4 * h:4 * h + 4] for h in range(nh)]

        @pl.when(j == 0)
        def _():
            for m_s, l_s, acc_s, acc_lo_s in per_head:
                m_s[...] = jnp.full_like(m_s, NEG_INF)
                l_s[...] = jnp.zeros_like(l_s)
                acc_s[...] = jnp.zeros_like(acc_s)
                acc_lo_s[...] = jnp.zeros_like(acc_lo_s)

        def step(on_diagonal):
            for h, (m_s, l_s, acc_s, acc_lo_s) in enumerate(per_head):
                lanes = slice(h * FOX_AUG, (h + 1) * FOX_AUG)
                s = _qk(q_ref[:, lanes], k_ref[:, lanes])
                if on_diagonal:
                    s = jnp.where(_diag_mask(tq), s, NEG_INF)
                m_prev = m_s[...]
                m_new = jnp.maximum(m_prev, jnp.max(s, axis=-1, keepdims=True))
                alpha = jnp.exp(m_prev - m_new)
                p = jnp.exp(s - m_new[:, 0:1])
                p_hi = p.astype(BF16)
                p_lo = (p - p_hi.astype(F32)).astype(BF16)
                v = v_ref[:, lanes]
                l_s[...] = alpha * l_s[...] + jnp.sum(p, axis=-1, keepdims=True)
                acc_s[...] = alpha * acc_s[...] + jnp.dot(p_hi, v, preferred_element_type=F32)
                acc_lo_s[...] = alpha * acc_lo_s[...] + jnp.dot(p_lo, v, preferred_element_type=F32)
                m_s[...] = m_new

        @pl.when(j < i)
        def _():
            step(False)

        @pl.when(j == i)
        def _():
            step(True)
            for h, (m_s, l_s, acc_s, acc_lo_s) in enumerate(per_head):
                lanes = slice(h * FOX_AUG, (h + 1) * FOX_AUG)
                inv_l = 1.0 / l_s[...]
                o_ref[:, lanes] = (acc_s[...] * inv_l).astype(o_ref.dtype)
                ox_ref[:, lanes] = (acc_s[...] + acc_lo_s[...]) * inv_l
                lse_ref[:, lanes] = m_s[...] + jnp.log(l_s[...])

    qspec, kspec = _fox_specs(tq, fh, nh)
    wide = jax.ShapeDtypeStruct((n_seq, S, width), F32)
    return pl.pallas_call(
        body, name=name,
        grid_spec=pltpu.PrefetchScalarGridSpec(
            num_scalar_prefetch=2, grid=(BH, itab.shape[0]),
            in_specs=[qspec, kspec, kspec],
            out_specs=[qspec, qspec, qspec],
            scratch_shapes=[pltpu.VMEM((tq, LANES), F32)] * (4 * nh)),
        out_shape=[jax.ShapeDtypeStruct((n_seq, S, width), BF16), wide, wide],
        compiler_params=_cparams(("parallel", "arbitrary")),
    )(itab, jtab, qa, ka, va)


def _fox_ds(q, k, v, do, ox, lse, on_diagonal):
    s = _qk(q, k)
    if on_diagonal:
        s = jnp.where(_diag_mask(s.shape[0]), s, NEG_INF)
    p = jnp.exp(s - lse[:, 0:1])
    delta = jnp.sum(do.astype(F32) * ox, axis=-1, keepdims=True)
    return p, p * (_qk(do, v) - delta)


def fox_bwd(qa, ka, va, do, ox, lse, *, name):
    n_seq, S, width = qa.shape
    fh = width // FOX_AUG
    BH = n_seq * fh
    tq = min(FOX_TQ, S)
    itab, jtab, _ = _causal_pairs(S // tq)

    def body(itab_ref, jtab_ref, q_ref, k_ref, v_ref, do_ref, ox_ref, lse_ref, dq_ref, dk_ref, dv_ref, dsum_ref):
        t = pl.program_id(1)
        i, j = itab_ref[t], jtab_ref[t]

        @pl.when(t == 0)
        def _():
            dq_ref[...] = jnp.zeros_like(dq_ref)
            dk_ref[...] = jnp.zeros_like(dk_ref)
            dv_ref[...] = jnp.zeros_like(dv_ref)
            dsum_ref[...] = jnp.zeros_like(dsum_ref)

        q_rows = pl.ds(pl.multiple_of(i * tq, tq), tq)
        k_rows = pl.ds(pl.multiple_of(j * tq, tq), tq)

        def step(on_diagonal):
            q, k, do = q_ref[...], k_ref[...], do_ref[...]
            p, ds = _fox_ds(q, k, v_ref[...], do, ox_ref[...], lse_ref[...], on_diagonal)
            ds_b = ds.astype(BF16)
            tn = (((0,), (0,)), ((), ()))
            dq_ref[q_rows, :] += jnp.dot(ds_b, k, preferred_element_type=F32)
            dk_ref[k_rows, :] += lax.dot_general(ds_b, q, tn, preferred_element_type=F32)
            dv_ref[k_rows, :] += lax.dot_general(p.astype(BF16), do, tn, preferred_element_type=F32)
            dsum_ref[:, k_rows] += _colsum(ds)

        @pl.when(j < i)
        def _():
            step(False)

        @pl.when(j == i)
        def _():
            step(True)

    qspec, kspec = _fox_specs(tq, fh)
    whole = pl.BlockSpec((None, S, FOX_AUG), lambda b, t, it, jt: (b // fh, 0, b % fh))
    wide = jax.ShapeDtypeStruct((n_seq, S, width), F32)
    return pl.pallas_call(
        body, name=name,
        grid_spec=pltpu.PrefetchScalarGridSpec(
            num_scalar_prefetch=2, grid=(BH, itab.shape[0]),
            in_specs=[qspec, kspec, kspec, qspec, qspec, qspec],
            out_specs=[whole, whole, whole, pl.BlockSpec((None, 1, S), lambda b, t, it, jt: (b, 0, 0))]),
        out_shape=[wide, wide, wide, jax.ShapeDtypeStruct((BH, 1, S), F32)],
        compiler_params=_cparams(("parallel", "arbitrary")),
    )(itab, jtab, qa, ka, va, do, ox, lse)


def seq_cumsum(x, n_seq, seq, *, reverse, name):
    T, C = x.shape
    tb = min(256, seq)
    n = seq // tb
    r = np.arange(tb)
    tri = (r[None, :] >= r[:, None]) if reverse else (r[None, :] <= r[:, None])
    tri = jnp.asarray(tri.astype(np.float32), dtype=BF16)

    def body(x_ref, tri_ref, o_ref, carry_s):
        @pl.when(pl.program_id(1) == 0)
        def _():
            carry_s[...] = jnp.zeros_like(carry_s)

        xv = x_ref[...]
        o_ref[...] = _dot3(tri_ref[...], xv) + carry_s[...]
        carry_s[...] += _colsum(xv)

    blk = (lambda s, t: (s * n + (n - 1 - t), 0)) if reverse else (lambda s, t: (s * n + t, 0))
    return pl.pallas_call(
        body, name=name,
        grid=(n_seq, n),
        in_specs=[pl.BlockSpec((tb, C), blk), pl.BlockSpec((tb, tb), lambda s, t: (0, 0))],
        out_specs=pl.BlockSpec((tb, C), blk),
        out_shape=jax.ShapeDtypeStruct((T, C), F32),
        scratch_shapes=[pltpu.VMEM((1, C), F32)],
        compiler_params=_cparams(("arbitrary", "arbitrary")),
    )(x, tri)


def _place():
    return lax.axis_index("x"), lax.axis_index("y"), lax.axis_index("c")


def _other_chips(x, y):
    return [(1 - x, y), (x, 1 - y), (1 - x, 1 - y)]


def _hbm_call(body, ins, out_shape, n_sems, *, name):
    hbm = pl.BlockSpec(memory_space=pl.ANY)
    return pl.pallas_call(
        body, name=name,
        in_specs=[hbm] * len(ins), out_specs=[hbm] * len(out_shape), out_shape=out_shape,
        scratch_shapes=[pltpu.SemaphoreType.DMA((n_sems,)), pltpu.SemaphoreType.DMA((n_sems,)),
                        pltpu.SemaphoreType.DMA((len(ins),))],
        compiler_params=pltpu.CompilerParams(has_side_effects=True),
    )(*ins)


def allgather_chips(shards, *, name):
    return _exchange_call(allgather_rider(shards), name=name)


def _allgather_ops(x_refs, o_refs, send_sems, recv_sems, local_sems):
    def copies():
        x, y, c = _place()
        me = 2 * x + y
        chips = _other_chips(x, y)
        own, first, passed, landed, handed = [], [], [], [], []
        for b, (x_ref, o_ref) in enumerate(zip(x_refs, o_refs)):
            half = x_ref.shape[0] // 2
            mine, theirs = pl.ds(c * half, half), pl.ds((1 - c) * half, half)
            own.append(pltpu.make_async_copy(x_ref, o_ref.at[me], local_sems.at[b]))

            def copy(k, src, chip, rows, to, o_ref=o_ref, b=b):
                return pltpu.make_async_remote_copy(src_ref=src, dst_ref=o_ref.at[2 * chip[0] + chip[1], rows],
                                                    send_sem=send_sems.at[6 * b + k], recv_sem=recv_sems.at[6 * b + k],
                                                    device_id=to, device_id_type=MESH)
            for j, chip in enumerate(chips):
                first.append(copy(j, x_ref.at[mine], (x, y), mine, (*chip, c)))
                landed.append(copy(j, x_ref.at[mine], chip, mine, (*chip, c)))
                passed.append(copy(3 + j, o_ref.at[2 * chip[0] + chip[1], mine], chip, mine, (x, y, 1 - c)))
                handed.append(copy(3 + j, x_ref.at[mine], chip, theirs, (x, y, 1 - c)))
        return own, first, passed, landed, handed

    def start():
        own, first, _, _, _ = copies()
        for cp in own + first:
            cp.start()

    def finish():
        own, first, passed, landed, handed = copies()
        for arrived, forward in zip(landed, passed):
            arrived.wait_recv()
            forward.start()
        for cp in handed:
            cp.wait_recv()
        for cp in first + passed:
            cp.wait_send()
        for cp in own:
            cp.wait()
    return start, finish


def _scatter_ops(x_refs, o_refs, send_sems, recv_sems, local_sems):
    def copies():
        x, y, c = _place()
        return [pltpu.make_async_remote_copy(
            src_ref=x_ref.at[2 * px + py], dst_ref=o_ref.at[j], send_sem=send_sems.at[3 * b + j],
            recv_sem=recv_sems.at[3 * b + j], device_id=(px, py, c), device_id_type=MESH)
            for b, (x_ref, o_ref) in enumerate(zip(x_refs, o_refs)) for j, (px, py) in enumerate(_other_chips(x, y))]

    def start():
        for cp in copies():
            cp.start()

    def finish():
        sends = copies()
        for cp in sends:
            cp.wait_recv()
        for cp in sends:
            cp.wait_send()
    return start, finish


class Rider(NamedTuple):
    ins: list
    out_shape: list
    n_sems: int
    ops: object

    def specs(self):
        hbm = pl.BlockSpec(memory_space=pl.ANY)
        sems = [pltpu.SemaphoreType.DMA((self.n_sems,)), pltpu.SemaphoreType.DMA((self.n_sems,)),
                pltpu.SemaphoreType.DMA((len(self.ins),))]
        return [hbm] * len(self.ins), [hbm] * len(self.out_shape), sems

    def wrap(self, body, n_in, n_out, grid_rank):
        k_in, k_out = len(self.ins), len(self.out_shape)

        def carried(*refs):
            ins, r_ins = refs[:n_in], refs[n_in:n_in + k_in]
            outs = refs[n_in + k_in:n_in + k_in + n_out]
            r_outs = refs[n_in + k_in + n_out:n_in + k_in + n_out + k_out]
            scratch, sems = refs[n_in + k_in + n_out + k_out:-3], refs[-3:]
            first = functools.reduce(jnp.logical_and, [pl.program_id(a) == 0 for a in range(grid_rank)])
            last = functools.reduce(jnp.logical_and,
                                    [pl.program_id(a) == pl.num_programs(a) - 1 for a in range(grid_rank)])
            pl.when(first)(lambda: self.ops(r_ins, r_outs, *sems)[0]())
            body(*ins, *outs, *scratch)
            pl.when(last)(lambda: self.ops(r_ins, r_outs, *sems)[1]())
        return carried


def _exchange_call(rider, *, name):
    def body(*refs):
        k = len(rider.ins)
        start, finish = rider.ops(refs[:k], refs[k:k + len(rider.out_shape)], *refs[-3:])
        start()
        finish()
    in_specs, out_specs, sems = rider.specs()
    return pl.pallas_call(body, name=name, in_specs=in_specs, out_specs=out_specs, out_shape=rider.out_shape,
                          scratch_shapes=sems, compiler_params=pltpu.CompilerParams(has_side_effects=True))(*rider.ins)


def allgather_rider(shards):
    assert all(s.shape[0] % (2 * ROW_ALIGN) == 0 for s in shards)
    return Rider(list(shards), [jax.ShapeDtypeStruct((4,) + s.shape, s.dtype) for s in shards], 6 * len(shards),
                 _allgather_ops)


def scatter_rider(parts):
    return Rider(list(parts), [jax.ShapeDtypeStruct((3,) + p.shape[1:], p.dtype) for p in parts], 3 * len(parts),
                 _scatter_ops)


def scatter_chips(parts, *, name):
    return _exchange_call(scatter_rider(parts), name=name)


def swap_cores(vs, *, name):
    nb = len(vs)

    def body(*refs):
        x_refs, o_refs = refs[:nb], refs[nb:2 * nb]
        send_sems, recv_sems, _ = refs[2 * nb:]
        x, y, c = _place()
        copies = [pltpu.make_async_remote_copy(src_ref=x_ref, dst_ref=o_ref, send_sem=send_sems.at[b],
                                               recv_sem=recv_sems.at[b], device_id=(x, y, 1 - c), device_id_type=MESH)
                  for b, (x_ref, o_ref) in enumerate(zip(x_refs, o_refs))]
        for cp in copies:
            cp.start()
        for cp in copies:
            cp.wait()

    return _hbm_call(body, vs, [jax.ShapeDtypeStruct(v.shape, v.dtype) for v in vs], nb, name=name)


def allreduce_small(v, *, name):
    R, C = v.shape

    def body(x_ref, o_ref, gath_ref, send_sems, recv_sems):
        x, y, c = _place()
        me = 4 * x + 2 * y + c
        gath_ref[me] = x_ref[...]
        flips = [(k >> 2 & 1, k >> 1 & 1, k & 1) for k in range(1, 8)]
        sends = []
        for j, (fx, fy, fc) in enumerate(flips):
            peer = (x ^ fx, y ^ fy, c ^ fc)
            cp = pltpu.make_async_remote_copy(src_ref=x_ref, dst_ref=gath_ref.at[me], send_sem=send_sems.at[j],
                                              recv_sem=recv_sems.at[j], device_id=peer, device_id_type=MESH)
            cp.start()
            sends.append(cp)
        for j, (fx, fy, fc) in enumerate(flips):
            peer = (x ^ fx, y ^ fy, c ^ fc)
            pltpu.make_async_remote_copy(src_ref=x_ref, dst_ref=gath_ref.at[4 * peer[0] + 2 * peer[1] + peer[2]],
                                         send_sem=send_sems.at[j], recv_sem=recv_sems.at[j], device_id=peer,
                                         device_id_type=MESH).wait_recv()
        for cp in sends:
            cp.wait_send()
        total = gath_ref[0]
        for d in range(1, 8):
            total = total + gath_ref[d]
        o_ref[...] = total

    vm = pl.BlockSpec(memory_space=pltpu.VMEM)
    out, _ = pl.pallas_call(
        body, name=name,
        in_specs=[vm], out_specs=[vm, vm],
        out_shape=[jax.ShapeDtypeStruct((R, C), F32), jax.ShapeDtypeStruct((8, R, C), F32)],
        scratch_shapes=[pltpu.SemaphoreType.DMA((7,)), pltpu.SemaphoreType.DMA((7,))],
        compiler_params=pltpu.CompilerParams(has_side_effects=True),
    )(v)
    return out


ROW_ALIGN = 16
PACK_W = 1024
SUM_TILE = 512
BIG_WEIGHTS = (("w_in", 1), ("w_a", 1), ("w_b", 1), ("w_o", 0), ("w_ff1", 1), ("w_ff2", 0), ("w_pg", 0), ("w_p", 1))


def _b_layout(d, ple):
    hw, q = d // 2, d // 4
    small = 2 * d + 2 * q
    lay = {"w_ff1": (0, 0, d, d), "w_ff2": (d, 0, d, d), "w_o": (2 * d, 0, q, d), "w_pg": (2 * d + q, 0, q, d),
           "w_a": (small, 0, hw, q), "w_b": (small, q, hw, q), "w_p": (small, 2 * q, ple, q)}
    return lay, small + hw


def pack_a(w_in_shard):
    rows, cols = w_in_shard.shape
    pad = -cols % LANES
    return jnp.concatenate([w_in_shard, jnp.zeros((rows, pad), w_in_shard.dtype)], axis=1)


def pack_b(shards, d):
    hw, q = d // 2, d // 4
    dt = shards["w_a"].dtype
    wp = shards["w_p"]
    wp = jnp.concatenate([wp, jnp.zeros((hw - wp.shape[0], q), dt)], axis=0)
    small = jnp.concatenate([shards["w_a"], shards["w_b"], wp, jnp.zeros((hw, d - 3 * q), dt)], axis=1)
    return jnp.concatenate([shards["w_ff1"], shards["w_ff2"], shards["w_o"], shards["w_pg"], small], axis=0)


def unpack_b(buf, lay):
    return {nm: buf[r0:r0 + rows, c0:c0 + cols] for nm, (r0, c0, rows, cols) in lay.items()}


def _win_layout(d):
    hw = d // 2
    fh = hw // FOX_HDIM
    orig = {"hq": (0, hw), "hf": (hw, hw), "hi": (2 * hw, hw), "hg": (3 * hw, hw), "fq": (4 * hw, hw),
            "fk": (5 * hw, hw), "fv": (6 * hw, hw), "ff": (7 * hw, fh), "ga": (7 * hw + fh, d), "gb": (7 * hw + fh + d, d)}
    order = ["ga", "gb", "hq", "hf", "hi", "hg", "fq", "fk", "fv", "ff"]
    mine, off = {}, 0
    for nm in order:
        width = orig[nm][1] if nm != "ff" else LANES
        mine[nm] = (off, width)
        off += width
    return orig, order, mine, off


def _adam_fn(rows, vecs):
    w, g, m, v = rows
    m2 = ADAM_B1 * m + (1.0 - ADAM_B1) * g
    v2 = ADAM_B2 * v + (1.0 - ADAM_B2) * (g * g)
    m_hat = m2 / (1.0 - ADAM_B1 ** ADAM_STEP)
    v_hat = v2 / (1.0 - ADAM_B2 ** ADAM_STEP)
    delta = -ADAM_LR * (m_hat / (jnp.sqrt(v_hat) + ADAM_EPS) + ADAM_WD * w)
    return [delta, m2, v2], []


def adamw_small(small, p0, ws, ms, vs, *, name):
    n = len(ws)
    hw = p0.shape[1]
    fh = ws[8].shape[1]

    def body(small_ref, p0_ref, *refs):
        w_refs, m_refs, v_refs = refs[:n], refs[n:2 * n], refs[2 * n:3 * n]
        g_out, d_out, m_out, v_out = (refs[(3 + k) * n:(4 + k) * n] for k in range(4))
        sm = small_ref[...]
        p = p0_ref[...]
        d_lb = sm[6:7, hw:2 * hw] * (p * (1.0 - p))
        grads = [sm[r:r + 1, :] for r in range(6)]
        grads += [jnp.concatenate([d_lb, -d_lb], axis=0), sm[6:7, :hw], sm[7:8, :fh]]
        for i in range(n):
            (delta, m2, v2), _ = _adam_fn([w_refs[i][...], grads[i], m_refs[i][...], v_refs[i][...]], [])
            g_out[i][...], d_out[i][...], m_out[i][...], v_out[i][...] = grads[i], delta, m2, v2

    shapes = [jax.ShapeDtypeStruct(w.shape, F32) for w in ws]
    return pl.pallas_call(body, name=name, out_shape=shapes * 4)(small, p0, *ws, *ms, *vs)


def adamw(w, g, m, v, *, name):
    c = w.shape[1]
    (delta, m2, v2), _ = rowwise(_adam_fn, [w, g, m, v], [], [(c, F32)] * 3, name=name, tm=256)
    return delta, m2, v2


def kernel(x, p, ln0_g, ln0_b, w_in, hg_lb, hg_norm_g, fox_fb, w_a, w_b, w_o, ln1_g, ln1_b, w_ff1, w_ff2, w_pg, w_p, ln2_g, ln2_b, loss_target, m_ln0_g, m_ln0_b, m_w_in, m_hg_lb, m_hg_norm_g, m_fox_fb, m_w_a, m_w_b, m_w_o, m_ln1_g, m_ln1_b, m_w_ff1, m_w_ff2, m_w_pg, m_w_p, m_ln2_g, m_ln2_b, v_ln0_g, v_ln0_b, v_w_in, v_hg_lb, v_hg_norm_g, v_fox_fb, v_w_a, v_w_b, v_w_o, v_ln1_g, v_ln1_b, v_w_ff1, v_w_ff2, v_w_pg, v_w_p, v_ln2_g, v_ln2_b):
    n_seq, seq, d = x.shape
    T = n_seq * seq
    hw = d // 2
    fh = hw // FOX_HDIM
    bh = n_seq * fh
    orig, order, mine, n_in = _win_layout(d)

    big = {"w_in": w_in[0], "w_a": w_a[0], "w_b": w_b[0], "w_o": w_o[0], "w_ff1": w_ff1[0], "w_ff2": w_ff2[0],
           "w_pg": w_pg[0], "w_p": w_p[0]}
    big_m = {"w_in": m_w_in[0], "w_a": m_w_a[0], "w_b": m_w_b[0], "w_o": m_w_o[0], "w_ff1": m_w_ff1[0],
             "w_ff2": m_w_ff2[0], "w_pg": m_w_pg[0], "w_p": m_w_p[0]}
    big_v = {"w_in": v_w_in[0], "w_a": v_w_a[0], "w_b": v_w_b[0], "w_o": v_w_o[0], "w_ff1": v_w_ff1[0],
             "w_ff2": v_w_ff2[0], "w_pg": v_w_pg[0], "w_p": v_w_p[0]}
    names = [nm for nm, _ in BIG_WEIGHTS]
    axis = dict(BIG_WEIGHTS)
    ple = w_p.shape[1]
    lay, b_rows = _b_layout(d, ple)
    in_cols = big["w_in"].shape[1]

    gather_w_in = allgather_rider([pack_a(big["w_in"].astype(BF16))])
    gather_rest = allgather_rider([pack_b({nm: big[nm].astype(BF16) for nm in names if nm != "w_in"}, d)])

    x2 = x.reshape(T, d)
    tgt = loss_target.reshape(T, d)
    p_b = p.reshape(T, p.shape[-1]).astype(BF16)
    vec = lambda a: a.reshape(1, -1)
    probs = jax.nn.softmax(hg_lb, axis=0)
    lb = vec(probs[0])

    def ln0_fn(rows, vecs):
        h = _ln_stats(rows[0]) * vecs[0] + vecs[1]
        return [h, h], []
    (h0, h0b), _, (a_all,) = rowwise(ln0_fn, [x2], [vec(ln0_g), vec(ln0_b)], [(d, F32), (d, BF16)], name="ln0_fwd",
                                     rider=gather_w_in)
    win = jnp.concatenate([a_all[s, :, :in_cols] for s in range(4)], axis=1)
    win_mine = jnp.concatenate(
        [win[:, orig[nm][0]:orig[nm][0] + orig[nm][1]] for nm in order]
        + [jnp.zeros((d, LANES - fh), BF16)], axis=1)
    proj = matmul_nn(h0b, win_mine, name="in_proj")

    o_raw, hg_states, (b_all,) = hgrn2_fwd(proj, [mine["hq"][0], mine["hf"][0], mine["hi"][0]], lb, n_seq, seq,
                                           name="hgrn2_fwd", rider=gather_rest)
    view = lambda nm, k, n: WView(b_all, lay[nm][0], lay[nm][1], k, n, axis[nm])
    w_ff1_v, w_ff2_v = view("w_ff1", d, 4 * d), view("w_ff2", 4 * d, d)

    def whole(nm):
        r0, c0, rows, cols = lay[nm]
        return jnp.concatenate([b_all[s, r0:r0 + rows, c0:c0 + cols] for s in range(4)], axis=axis[nm])
    w_o_v, w_pg_v, w_a_v, w_p_v, w_b_full = whole("w_o"), whole("w_pg"), whole("w_a"), whole("w_p"), whole("w_b")

    def ya_fn(rows, vecs):
        o, hg = rows
        outs = []
        for h in range(HG_HEADS):
            oh = o[:, h * HG_DIM:(h + 1) * HG_DIM]
            outs.append(oh * lax.rsqrt(jnp.mean(oh * oh, axis=-1, keepdims=True) + RMS_EPS))
        y = jnp.concatenate(outs, axis=1) * vecs[0] * (hg * _sigmoid(hg))
        return [y], []
    (y_a,), _ = rowwise(ya_fn, [o_raw, (proj,) + mine["hg"]], [hg_norm_g], [(hw, BF16)], name="hgrn2_out_fwd")

    fb_pad = jnp.concatenate([fox_fb, jnp.zeros((1, LANES - fh), F32)], axis=1)

    def lf_fn(rows, vecs):
        u = rows[0] + vecs[0]
        return [jnp.minimum(u, 0.0) - jnp.log(1.0 + jnp.exp(-jnp.abs(u)))], []
    (lf,), _ = rowwise(lf_fn, [(proj,) + mine["ff"]], [fb_pad], [(LANES, F32)], name="fox_logf")
    c_cum = seq_cumsum(lf, n_seq, seq, reverse=False, name="fox_cumsum")

    place = _fox_placement(fh)

    def prep_fn(rows, vecs):
        fq_, fk_, fv_, cc = rows
        pq, pk, aq, ak, oq, ok = vecs
        parts = jnp.concatenate(_split3(cc), axis=1)
        mm = lambda a_, b_: jnp.dot(a_, b_, preferred_element_type=F32)
        q_ = mm(fq_.astype(BF16), pq) + mm(parts, aq) + oq
        k_ = mm(fk_.astype(BF16), pk) + mm(parts, ak) + ok
        return [q_, k_, mm(fv_.astype(BF16), pk)], []
    wa = fh * FOX_AUG
    (qa, ka, va), _ = rowwise(prep_fn, [(proj,) + mine["fq"], (proj,) + mine["fk"], (proj,) + mine["fv"], c_cum],
                              [place[nm] for nm in ("pq", "pk", "aq", "ak", "oq", "ok")], [(wa, BF16)] * 3,
                              name="fox_prep")
    as_seq = lambda t2d: t2d.reshape(n_seq, seq, t2d.shape[1])
    o_fox, ox_fox, lse = fox_fwd(as_seq(qa), as_seq(ka), as_seq(va), name="fox_fwd")
    y_b = o_fox.reshape(T, wa)
    wb_pad = jnp.concatenate([w_b_full.reshape(fh, FOX_HDIM, d), jnp.zeros((fh, FOX_AUG - FOX_HDIM, d), BF16)],
                             axis=1).reshape(wa, d)

    fused_tm = 512
    pa = matmul_nn(y_a, w_a_v, name="proj_a")

    def merge_post(pb_, aux, vecs):
        ga, gb, a = aux
        return [_sigmoid(ga) * a + _sigmoid(gb) * pb_, pb_], []
    (merged, pb), _ = matmul_nn(y_b, wb_pad, name="proj_b_merge", tm=fused_tm, post=merge_post,
                                post_aux=[(proj,) + mine["ga"], (proj,) + mine["gb"], pa], post_outs=[BF16, F32])

    def ln1_post(mix, aux, vecs):
        z = ALPHA * aux[0] + mix
        h = _ln_stats(z) * vecs[0] + vecs[1]
        return [z, h, h], []
    (z1, h1, h1b), _ = matmul_nn(merged, w_o_v, name="out_proj_ln1", tm=fused_tm, post=ln1_post, post_aux=[h0],
                                 post_vecs=[ln1_g, ln1_b], post_outs=[F32, F32, BF16])

    relu2 = lambda u: jnp.square(jnp.maximum(u, 0.0))
    act = matmul_nn(h1b, w_ff1_v, name="ff1", out_dtype=BF16, epilogue=relu2)
    pg = matmul_nn(h1b, w_pg_v, name="ple_gate")
    pe = matmul_nn(p_b, w_p_v, name="ple_embed")

    def head_post(ffv, aux, vecs):
        h1v, pgv, pev, t = aux
        g2, b2 = vecs
        sp = _sigmoid(pgv)
        z = ALPHA * h1v + ffv + sp * pev
        y = _ln_stats(z) * g2 + b2
        err = y - t
        loss_rows = 0.5 * jnp.mean(err * err, axis=-1, keepdims=True)
        dy = err * (1.0 / d)
        dz, dg2, db2 = _ln_bwd(z, dy, g2)
        loss_acc = jnp.broadcast_to(_colsum(loss_rows), (1, d))
        return [dz, dz, dz * pev * (sp * (1.0 - sp)), dz * sp], [dg2, db2, loss_acc]
    (dz2, dz2b, dpg, dpe), (g_ln2_g, g_ln2_b, loss_part) = matmul_nn(
        act, w_ff2_v, name="ff2_head", tm=fused_tm, post=head_post, post_aux=[h1, pg, pe, tgt],
        post_vecs=[ln2_g, ln2_b], post_outs=[F32, BF16, BF16, BF16], post_accs=[d, d, d])

    dact = lambda da, a: da * (2.0 * jnp.sqrt(a.astype(F32)))
    du = matmul_nn(dz2b, w_ff2_v, transpose_rhs=True, name="d_ff2", out_dtype=BF16, epilogue=dact, aux=act)
    dh1_pg = matmul_nn(dpg, w_pg_v, transpose_rhs=True, name="d_ple_gate")

    def ln1_bwd_post(dh1_ff, aux, vecs):
        dh1 = ALPHA * aux[0] + dh1_ff + aux[1]
        dz, dg, db = _ln_bwd(aux[2], dh1, vecs[0])
        return [dz, dz], [dg, db]
    (dz1, dz1b), (g_ln1_g, g_ln1_b) = matmul_nn(
        du, w_ff1_v, transpose_rhs=True, name="d_ff1_ln1", tm=fused_tm, post=ln1_bwd_post, post_aux=[dz2, dh1_pg, z1],
        post_vecs=[ln1_g], post_outs=[F32, BF16], post_accs=[d, d])

    def merge_bwd_post(dm, aux, vecs):
        ga, gb, a, b = aux
        sa, sb = _sigmoid(ga), _sigmoid(gb)
        return [dm * a * (sa * (1.0 - sa)), dm * b * (sb * (1.0 - sb)), dm * sa, dm * sb], []
    (dga, dgb, dma, dmb), _ = matmul_nn(
        dz1b, w_o_v, transpose_rhs=True, name="d_out_proj_merge", tm=fused_tm, post=merge_bwd_post,
        post_aux=[(proj,) + mine["ga"], (proj,) + mine["gb"], pa, pb], post_outs=[BF16] * 4)
    dya = matmul_nn(dma, w_a_v, transpose_rhs=True, name="d_proj_a")
    dyb = matmul_nn(dmb, wb_pad, transpose_rhs=True, name="d_proj_b", out_dtype=BF16)

    def ya_bwd_fn(rows, vecs):
        o, hg, dy = rows
        ng = vecs[0]
        sg = _sigmoid(hg)
        gate = hg * sg
        dn_parts, do_parts, n_parts = [], [], []
        for h in range(HG_HEADS):
            hs = slice(h * HG_DIM, (h + 1) * HG_DIM)
            oh = o[:, hs]
            r = lax.rsqrt(jnp.mean(oh * oh, axis=-1, keepdims=True) + RMS_EPS)
            nh = oh * r
            dn = dy[:, hs] * ng[:, hs] * gate[:, hs]
            do_parts.append(r * (dn - nh * jnp.mean(dn * nh, axis=-1, keepdims=True)))
            n_parts.append(nh)
        nrm = jnp.concatenate(n_parts, axis=1)
        dhg = dy * nrm * ng * (sg * (1.0 + hg * (1.0 - sg)))
        return [jnp.concatenate(do_parts, axis=1), dhg], [_colsum(dy * nrm * gate)]
    (do_raw, dhg), (g_norm_g,) = rowwise(ya_bwd_fn, [o_raw, (proj,) + mine["hg"], dya], [hg_norm_g],
                                         [(hw, F32), (hw, BF16)], [hw], name="hgrn2_out_bwd")
    dhq, dhf, dhi, g_lb = hgrn2_bwd(proj, [mine["hq"][0], mine["hf"][0], mine["hi"][0]], lb, do_raw, hg_states,
                                    n_seq, seq, name="hgrn2_bwd")

    do_fox = as_seq(dyb)
    dqa, dka, dva, dsum = fox_bwd(as_seq(qa), as_seq(ka), as_seq(va), do_fox, ox_fox, lse, name="fox_bwd")

    def unprep_fn(rows, vecs):
        mm = lambda a_, b_: jnp.dot(a_.astype(BF16), b_, preferred_element_type=F32)
        return [mm(rows[0], vecs[0]), mm(rows[1], vecs[1]), mm(rows[2], vecs[1])], []
    (dfq, dfk, dfv), _ = rowwise(unprep_fn, [dqa.reshape(T, wa), dka.reshape(T, wa), dva.reshape(T, wa)],
                                 [place["pqt"], place["pkt"]], [(hw, BF16)] * 3, name="fox_unprep")
    dc = -dsum.reshape(n_seq, fh, seq).transpose(0, 2, 1).reshape(T, fh)
    dc = jnp.concatenate([dc, jnp.zeros((T, LANES - fh), F32)], axis=1)
    dlf = seq_cumsum(dc, n_seq, seq, reverse=True, name="fox_cumsum_bwd")

    def lf_bwd_fn(rows, vecs):
        u = rows[0] + vecs[0]
        du_ = rows[1] * _sigmoid(-u)
        return [du_], [_colsum(du_)]
    (dff_,), (g_fb,) = rowwise(lf_bwd_fn, [(proj,) + mine["ff"], dlf], [fb_pad], [(LANES, BF16)], [LANES],
                               name="fox_logf_bwd")

    dproj = jnp.concatenate([dga, dgb, dhq, dhf, dhi, dhg, dfq, dfk, dfv, dff_], axis=1)

    grads_b = jnp.zeros((4, b_rows, d), F32)
    for nm, lhs, rhs in (("w_ff1", h1b, du), ("w_ff2", act, dz2b)):
        grads_b = matmul_tn(lhs, rhs, name="g_" + nm, into=(grads_b, lay[nm][0], lay[nm][1], axis[nm]))
    gfull = {
        "w_a": matmul_tn(y_a, dma, name="g_w_a"),
        "w_b": matmul_tn(y_b, dmb, name="g_w_b").reshape(fh, FOX_AUG, d)[:, :FOX_HDIM].reshape(hw, d),
        "w_o": matmul_tn(merged, dz1b, name="g_w_o"),
        "w_pg": matmul_tn(h1b, dpg, name="g_w_pg"),
        "w_p": matmul_tn(p_b, dpe, name="g_w_p"),
    }

    def chip_parts(nm, s):
        g = gfull[nm]
        n = g.shape[axis[nm]] // 4
        return lax.slice_in_dim(g, s * n, (s + 1) * n, axis=axis[nm])
    for nm in gfull:
        grads_b = lax.dynamic_update_slice(grads_b, jnp.stack([chip_parts(nm, s) for s in range(4)]),
                                           (0, lay[nm][0], lay[nm][1]))
    me = 2 * lax.axis_index("x") + lax.axis_index("y")
    core = lax.axis_index("c")

    def sum2_fn(rows, vecs):
        s = rows[0] + rows[1].astype(F32)
        return [s, s], []

    def sum4_fn(rows, vecs):
        a, r0, r1, r2 = rows
        return [((a + r0.astype(F32)) + r1.astype(F32)) + r2.astype(F32)], []

    def chip_pair_sum(g, tag):
        h, cols = g.shape[1] // 2, g.shape[2]
        keep = lax.dynamic_slice_in_dim(g, core * h, h, axis=1)
        give = lax.dynamic_slice_in_dim(g, (1 - core) * h, h, axis=1).astype(BF16)
        (from_core,) = swap_cores([give], name="swap_partials_" + tag)
        (s32, s16), _ = rowwise(sum2_fn, [keep.reshape(4 * h, cols), from_core.reshape(4 * h, cols)], [],
                                [(cols, F32), (cols, BF16)], name="sum_cores_" + tag, tm=SUM_TILE)
        return s32.reshape(4, h, cols), s16.reshape(4, h, cols)

    def chip_sum(pr, gt, tag):
        own = lax.dynamic_index_in_dim(pr, me, axis=0, keepdims=False)
        (q,), _ = rowwise(sum4_fn, [own, gt[0], gt[1], gt[2]], [], [(own.shape[1], F32)], name="sum_chips_" + tag,
                          tm=SUM_TILE)
        return q

    pair_rest, pair_rest_b = chip_pair_sum(grads_b, "rest")
    gw_in_mine, (got_rest,) = matmul_tn(h0b, dproj, name="g_w_in", rider=scatter_rider([pair_rest_b]))
    gfull["w_in"] = jnp.concatenate([gw_in_mine[:, mine[nm][0]:mine[nm][0] + orig[nm][1]]
                                     for nm in ["hq", "hf", "hi", "hg", "fq", "fk", "fv", "ff", "ga", "gb"]], axis=1)
    grads_a = jnp.stack([pack_a(chip_parts("w_in", s)) for s in range(4)])
    pair_in, pair_in_b = chip_pair_sum(grads_a, "w_in")
    def ln0_bwd_post(dh0_in, aux, vecs):
        dx, dg, db = _ln_bwd(aux[1], dh0_in + ALPHA * aux[0], vecs[0])
        return [dx], [dg, db]
    ((dx,), (g_ln0_g, g_ln0_b)), (got_in,) = matmul_nn(
        dproj, win_mine, transpose_rhs=True, name="d_in_proj_ln0", tm=fused_tm, post=ln0_bwd_post,
        post_aux=[dz1, x2], post_vecs=[vec(ln0_g)], post_outs=[F32], post_accs=[d, d],
        rider=scatter_rider([pair_in_b]))
    q_half = [chip_sum(pair_in, got_in, "w_in"), chip_sum(pair_rest, got_rest, "rest")]
    q_other = swap_cores(q_half, name="swap_halves")
    g_a, g_b = [jnp.concatenate([jnp.where(core == 0, mine_, other), jnp.where(core == 0, other, mine_)], axis=0)
                for mine_, other in zip(q_half, q_other)]
    g_shards = unpack_b(g_b, lay)
    g_shards["w_in"] = g_a[:, :in_cols]

    assert d == PACK_W and 2 * hw == PACK_W and fh <= LANES
    small = allreduce_small(jnp.concatenate(
        [g_ln0_g, g_ln0_b, g_ln1_g, g_ln1_b, g_ln2_g, g_ln2_b, jnp.concatenate([g_norm_g, g_lb], axis=1),
         jnp.concatenate([g_fb, loss_part[:, LANES:]], axis=1)], axis=0), name="allreduce_small")
    loss = small[7, LANES]

    small_w = [vec(ln0_g), vec(ln0_b), ln1_g, ln1_b, ln2_g, ln2_b, hg_lb, hg_norm_g, fox_fb]
    small_m = [vec(m_ln0_g), vec(m_ln0_b), m_ln1_g, m_ln1_b, m_ln2_g, m_ln2_b, m_hg_lb, m_hg_norm_g, m_fox_fb]
    small_v = [vec(v_ln0_g), vec(v_ln0_b), v_ln1_g, v_ln1_b, v_ln2_g, v_ln2_b, v_hg_lb, v_hg_norm_g, v_fox_fb]
    small_out = adamw_small(small, probs[0:1], small_w, small_m, small_v, name="adamw_small")
    small_shapes = [ln0_g.shape, ln0_b.shape, ln1_g.shape, ln1_b.shape, ln2_g.shape, ln2_b.shape, hg_lb.shape,
                    hg_norm_g.shape, fox_fb.shape]
    sg_out, sd_out, sm_out, sv_out = [[a.reshape(shp) for a, shp in zip(small_out[9 * k:9 * k + 9], small_shapes)]
                                      for k in range(4)]

    big_out = {}
    for nm in names:
        delta, m2, v2 = adamw(big[nm], g_shards[nm], big_m[nm], big_v[nm], name="adamw_" + nm)
        big_out[nm] = (g_shards[nm][None], delta[None], m2[None], v2[None])

    def ordered(k):
        sm_ = [sg_out, sd_out, sm_out, sv_out][k]
        bg = lambda nm: big_out[nm][k]
        return [sm_[0], sm_[1], bg("w_in"), sm_[6], sm_[7], sm_[8], bg("w_a"), bg("w_b"), bg("w_o"), sm_[2], sm_[3],
                bg("w_ff1"), bg("w_ff2"), bg("w_pg"), bg("w_p"), sm_[4], sm_[5]]
    grad_x = dx.reshape(n_seq, seq, d)
    return (loss, grad_x, *ordered(0), *ordered(1), *ordered(2), *ordered(3))
```

```python
import functools
from typing import NamedTuple, Optional

import numpy as np
import jax
import jax.numpy as jnp
from jax import lax
from jax.experimental import pallas as pl
from jax.experimental.pallas import tpu as pltpu

F32 = jnp.float32
BF16 = jnp.bfloat16
MESH = pl.DeviceIdType.MESH

VMEM_LIMIT_BYTES = 48 * 1024 * 1024
LANES = 128
HG_HEADS = 4
HG_DIM = 128
HG_BLK = 16
HG_TILE = 256
HG_SLOTS = 16
FOX_HDIM = 64
FOX_AUG = 128
FOX_TQ = 1024
FOX_FWD_HEADS = 1
LN_EPS = 1e-5
RMS_EPS = 1e-6
DEPTH = 1
ALPHA = (2.0 * DEPTH) ** 0.25
ADAM_LR, ADAM_B1, ADAM_B2, ADAM_EPS, ADAM_WD, ADAM_STEP = 0.001, 0.9, 0.999, 1e-08, 0.01, 10
NEG_INF = -1e30


def _cparams(sem):
    return pltpu.CompilerParams(dimension_semantics=sem, vmem_limit_bytes=VMEM_LIMIT_BYTES)


def _tile(n, cap):
    if n <= cap:
        return n
    best = None
    for t in range(LANES, cap + 1, LANES):
        if n % t == 0:
            best = t
    assert best is not None, (n, cap)
    return best


class WView(NamedTuple):
    arr: jax.Array
    r0: int
    c0: int
    k: int
    n: int
    split: Optional[int]


def matmul_nn(a, w, *, name, transpose_rhs=False, out_dtype=F32, epilogue=None, aux=None, tm=2048, rider=None,
              post=None, post_aux=(), post_vecs=(), post_outs=(), post_accs=()):
    wv = w if isinstance(w, WView) else WView(w[None], 0, 0, w.shape[0], w.shape[1], None)
    rows_s = wv.k // 4 if wv.split == 0 else wv.k
    cols_s = wv.n // 4 if wv.split == 1 else wv.n
    tr, tc = _tile(rows_s, 1152), _tile(cols_s, 1152)
    assert wv.r0 % tr == 0 and wv.c0 % tc == 0
    T, K = a.shape
    N, tn, tk = (wv.k, tr, tc) if transpose_rhs else (wv.n, tc, tr)
    assert K == (wv.n if transpose_rhs else wv.k)
    tm = min(tm, T)
    assert T % tm == 0
    nk = K // tk

    def w_block(ri, ci):
        if wv.split == 0:
            return (ri * tr) // rows_s, (wv.r0 + (ri * tr) % rows_s) // tr, wv.c0 // tc + ci
        if wv.split == 1:
            return (ci * tc) // cols_s, wv.r0 // tr + ri, (wv.c0 + (ci * tc) % cols_s) // tc
        return 0, wv.r0 // tr + ri, wv.c0 // tc + ci

    fused = post is not None
    assert not fused or N == tn
    aux_list = list(post_aux) if fused else ([aux] if aux is not None else [])
    aux_list = [x if isinstance(x, tuple) else (x, 0, x.shape[1]) for x in aux_list]
    vec_list = list(post_vecs)
    out_dtypes = list(post_outs) if fused else [out_dtype]
    n_aux, n_vec, n_out, n_acc = len(aux_list), len(vec_list), len(out_dtypes), len(post_accs)

    def body(*refs):
        a_ref, w_ref = refs[:2]
        aux_refs = refs[2:2 + n_aux]
        vec_refs = refs[2 + n_aux:2 + n_aux + n_vec]
        out_refs = refs[2 + n_aux + n_vec:2 + n_aux + n_vec + n_out]
        sum_refs = refs[2 + n_aux + n_vec + n_out:2 + n_aux + n_vec + n_out + n_acc]
        acc_ref = refs[-1]
        m, k = pl.program_id(1), pl.program_id(2)
        if transpose_rhs:
            part = lax.dot_general(a_ref[...], w_ref[...], (((1,), (1,)), ((), ())), preferred_element_type=F32)
        else:
            part = jnp.dot(a_ref[...], w_ref[...], preferred_element_type=F32)

        def write(res):
            if not fused:
                if epilogue is not None:
                    res = epilogue(res) if not aux_refs else epilogue(res, aux_refs[0][...])
                out_refs[0][...] = res.astype(out_dtype)
                return
            outs, sums = post(res, [r[...] for r in aux_refs], [v[...] for v in vec_refs])
            assert len(outs) == n_out and len(sums) == n_acc
            for r, val in zip(out_refs, outs):
                r[...] = val.astype(r.dtype)
            for r, val in zip(sum_refs, sums):
                def first_rows(r=r, val=val):
                    r[...] = val

                def later_rows(r=r, val=val):
                    r[...] += val
                pl.when(m == 0)(first_rows)
                pl.when(m > 0)(later_rows)

        if nk == 1:
            write(part)
        else:
            @pl.when(k == 0)
            def _():
                acc_ref[...] = part

            @pl.when(k > 0)
            def _():
                acc_ref[...] += part

            @pl.when(k == nk - 1)
            def _():
                write(acc_ref[...])

    w_index = (lambda n, m, k: w_block(n, k)) if transpose_rhs else (lambda n, m, k: w_block(k, n))
    in_specs = [pl.BlockSpec((tm, tk), lambda n, m, k: (m, k)),
                pl.BlockSpec((None, tr, tc), w_index)]
    args = [a, wv.arr]
    for arr, off, width in aux_list:
        assert width == N and off % tn == 0
        in_specs.append(pl.BlockSpec((tm, tn), functools.partial(lambda n, m, k, blk: (m, blk + n), blk=off // tn)))
        args.append(arr)
    for v in vec_list:
        in_specs.append(pl.BlockSpec(v.shape, lambda n, m, k: (0, 0)))
        args.append(v)
    out_specs = [pl.BlockSpec((tm, tn), lambda n, m, k: (m, n)) for _ in out_dtypes]
    out_specs += [pl.BlockSpec((1, tn), lambda n, m, k: (0, 0)) for _ in post_accs]
    out_shape = [jax.ShapeDtypeStruct((T, N), dt) for dt in out_dtypes]
    out_shape += [jax.ShapeDtypeStruct((1, N), F32) for _ in post_accs]
    scratch = [pltpu.VMEM((tm, tn) if nk > 1 else (8, LANES), F32)]
    grid = (N // tn, T // tm, nk)
    sem = ("arbitrary",) * 3 if (n_acc or rider is not None) else ("parallel", "parallel", "arbitrary")
    params = pltpu.CompilerParams(dimension_semantics=sem, vmem_limit_bytes=VMEM_LIMIT_BYTES,
                                  has_side_effects=rider is not None)
    if rider is not None:
        r_in, r_out, r_sems = rider.specs()
        body = rider.wrap(body, len(in_specs), len(out_specs), 3)
        in_specs, out_specs, out_shape = in_specs + r_in, out_specs + r_out, out_shape + rider.out_shape
        scratch, args = scratch + r_sems, args + list(rider.ins)
    res = pl.pallas_call(body, name=name, grid=grid, in_specs=in_specs, out_specs=out_specs, out_shape=out_shape,
                         scratch_shapes=scratch, compiler_params=params)(*args)
    main = (list(res[:n_out]), list(res[n_out:n_out + n_acc])) if fused else res[0]
    return main if rider is None else (main, list(res[n_out + n_acc:]))


def matmul_tn(a, b, *, name, tk=2048, rider=None, into=None):
    T, M = a.shape
    T2, N = b.shape
    tk = min(tk, T)
    assert T == T2 and T % tk == 0
    if into is not None:
        assert rider is None
        buf, r0, c0, split = into
        rows_s, cols_s = (M // 4, N) if split == 0 else (M, N // 4)
        tm, tn = _tile(rows_s, 1024), _tile(cols_s, 1152)
        assert r0 % tm == 0 and c0 % tn == 0

        def part_block(m, n, k):
            if split == 0:
                return (m * tm) // rows_s, (r0 + (m * tm) % rows_s) // tm, c0 // tn + n
            return (n * tn) // cols_s, r0 // tm + m, (c0 + (n * tn) % cols_s) // tn

        def body_into(a_ref, b_ref, buf_ref, o_ref):
            k = pl.program_id(2)
            part = lax.dot_general(a_ref[...], b_ref[...], (((0,), (0,)), ((), ())), preferred_element_type=F32)

            @pl.when(k == 0)
            def _():
                o_ref[...] = part

            @pl.when(k > 0)
            def _():
                o_ref[...] += part

        return pl.pallas_call(
            body_into, name=name, grid=(M // tm, N // tn, T // tk),
            in_specs=[pl.BlockSpec((tk, tm), lambda m, n, k: (k, m)), pl.BlockSpec((tk, tn), lambda m, n, k: (k, n)),
                      pl.BlockSpec(memory_space=pl.ANY)],
            out_specs=pl.BlockSpec((None, tm, tn), part_block),
            out_shape=jax.ShapeDtypeStruct(buf.shape, buf.dtype), input_output_aliases={2: 0},
            compiler_params=_cparams(("parallel", "parallel", "arbitrary")))(a, b, buf)
    tm = _tile(M, 1024)
    tn = _tile(N, 1152)

    def body(a_ref, b_ref, o_ref):
        k = pl.program_id(2)
        part = lax.dot_general(a_ref[...], b_ref[...], (((0,), (0,)), ((), ())), preferred_element_type=F32)

        @pl.when(k == 0)
        def _():
            o_ref[...] = part

        @pl.when(k > 0)
        def _():
            o_ref[...] += part

    in_specs = [pl.BlockSpec((tk, tm), lambda m, n, k: (k, m)), pl.BlockSpec((tk, tn), lambda m, n, k: (k, n))]
    out_specs = [pl.BlockSpec((tm, tn), lambda m, n, k: (m, n))]
    out_shape = [jax.ShapeDtypeStruct((M, N), F32)]
    grid = (M // tm, N // tn, T // tk)
    if rider is None:
        return pl.pallas_call(body, name=name, grid=grid, in_specs=in_specs, out_specs=out_specs, out_shape=out_shape,
                              compiler_params=_cparams(("parallel", "parallel", "arbitrary")))(a, b)[0]
    r_in, r_out, r_sems = rider.specs()
    res = pl.pallas_call(
        rider.wrap(body, 2, 1, 3), name=name, grid=grid, in_specs=in_specs + r_in, out_specs=out_specs + r_out,
        out_shape=out_shape + rider.out_shape, scratch_shapes=r_sems,
        compiler_params=pltpu.CompilerParams(dimension_semantics=("arbitrary",) * 3,
                                             vmem_limit_bytes=VMEM_LIMIT_BYTES, has_side_effects=True),
    )(a, b, *rider.ins)
    return res[0], list(res[1:])


def rowwise(fn, rows, vecs, outs, accs=(), *, name, tm=1024, rider=None):
    rows = [r if isinstance(r, tuple) else (r, 0, r.shape[1]) for r in rows]
    T = rows[0][0].shape[0]
    tm = min(tm, T)
    assert T % tm == 0
    n_rows, n_vecs, n_outs, n_accs = len(rows), len(vecs), len(outs), len(accs)

    def body(*refs):
        row_refs = refs[:n_rows]
        vec_refs = refs[n_rows:n_rows + n_vecs]
        out_refs = refs[n_rows + n_vecs:n_rows + n_vecs + n_outs]
        acc_refs = refs[n_rows + n_vecs + n_outs:]
        out_vals, acc_vals = fn([r[...] for r in row_refs], [v[...] for v in vec_refs])
        assert len(out_vals) == n_outs and len(acc_vals) == n_accs
        for r, val in zip(out_refs, out_vals):
            r[...] = val.astype(r.dtype)
        if n_accs:
            i = pl.program_id(0)

            @pl.when(i == 0)
            def _():
                for r in acc_refs:
                    r[...] = jnp.zeros_like(r)

            for r, val in zip(acc_refs, acc_vals):
                r[...] += val

    in_specs = []
    for arr, off, width in rows:
        assert off % width == 0
        in_specs.append(pl.BlockSpec((tm, width), functools.partial(lambda i, blk: (i, blk), blk=off // width)))
    for v in vecs:
        in_specs.append(pl.BlockSpec(v.shape, lambda i: (0, 0)))
    out_specs = [pl.BlockSpec((tm, w), lambda i: (i, 0)) for w, _ in outs]
    out_specs += [pl.BlockSpec((1, w), lambda i: (0, 0)) for w in accs]
    out_shape = [jax.ShapeDtypeStruct((T, w), dt) for w, dt in outs]
    out_shape += [jax.ShapeDtypeStruct((1, w), F32) for w in accs]
    args = [r[0] for r in rows] + list(vecs)
    if rider is None:
        res = pl.pallas_call(body, name=name, grid=(T // tm,), in_specs=in_specs, out_specs=out_specs,
                             out_shape=out_shape,
                             compiler_params=_cparams(("arbitrary",) if n_accs else ("parallel",)))(*args)
        return res[:n_outs], res[n_outs:]
    r_in, r_out, r_sems = rider.specs()
    res = pl.pallas_call(
        rider.wrap(body, len(in_specs), len(out_specs), 1), name=name, grid=(T // tm,), in_specs=in_specs + r_in,
        out_specs=out_specs + r_out, out_shape=out_shape + rider.out_shape, scratch_shapes=r_sems,
        compiler_params=pltpu.CompilerParams(dimension_semantics=("arbitrary",), vmem_limit_bytes=VMEM_LIMIT_BYTES,
                                             has_side_effects=True),
    )(*args, *rider.ins)
    return res[:n_outs], res[n_outs:n_outs + n_accs], list(res[n_outs + n_accs:])


def _colsum(x):
    return jnp.sum(x, axis=0, keepdims=True)


def _sigmoid(x):
    return 1.0 / (1.0 + jnp.exp(-x))


def _ln_stats(z):
    mu = jnp.mean(z, axis=-1, keepdims=True)
    zc = z - mu
    var = jnp.mean(zc * zc, axis=-1, keepdims=True)
    return zc * lax.rsqrt(var + LN_EPS)


def _ln_bwd(zhat_src, dy, g):
    mu = jnp.mean(zhat_src, axis=-1, keepdims=True)
    zc = zhat_src - mu
    var = jnp.mean(zc * zc, axis=-1, keepdims=True)
    rstd = lax.rsqrt(var + LN_EPS)
    zh = zc * rstd
    dzh = dy * g
    dz = rstd * (dzh - jnp.mean(dzh, axis=-1, keepdims=True) - zh * jnp.mean(dzh * zh, axis=-1, keepdims=True))
    return dz, _colsum(dy * zh), _colsum(dy)


def _hg_constants():
    r = np.arange(HG_TILE)
    same = (r[:, None] // HG_BLK) == (r[None, :] // HG_BLK)
    lower = (same & (r[None, :] <= r[:, None])).astype(np.float32)
    upper = (same & (r[None, :] >= r[:, None])).astype(np.float32)
    total = same.astype(np.float32)
    c = np.arange(2 * HG_DIM)
    bd = ((c[:, None] // HG_DIM) == (c[None, :] // HG_DIM)).astype(np.float32)
    pair_t = np.array([t for t, _ in _HG_PAIRS])
    pair_s = np.array([s for _, s in _HG_PAIRS])
    sel_t = (pair_t[None, :] == np.arange(HG_BLK)[:, None]).astype(np.float32)
    sel_s = (pair_s[None, :] == np.arange(HG_BLK)[:, None]).astype(np.float32)
    as_bf = lambda m: jnp.asarray(m, dtype=BF16)
    return as_bf(lower), as_bf(upper), as_bf(total), as_bf(bd), as_bf(sel_t), as_bf(sel_s)


_HG_HALF = HG_BLK // 2
_HG_PAIRS = ([(t, s) for t in range(_HG_HALF, HG_BLK) for s in range(HG_BLK)]
             + [(t, s) for t in range(_HG_HALF) for s in range(_HG_HALF)])
HG_STACK = len(_HG_PAIRS)
_HG_SLABS = ([((t - _HG_HALF) * HG_BLK, (t,), HG_BLK) for t in range(_HG_HALF, HG_BLK)]
             + [(_HG_HALF * HG_BLK + t * _HG_HALF, (t, t + 1), _HG_HALF) for t in range(0, _HG_HALF, 2)])


def _stack_by_s(x):
    return jnp.concatenate([x] * _HG_HALF + [x[:_HG_HALF]] * _HG_HALF, axis=0)


def _stack_by_t(x):
    w = x.shape[1]
    return jnp.concatenate([jnp.broadcast_to(x[t:t + 1], (HG_BLK, w)) for t in range(_HG_HALF, HG_BLK)]
                           + [jnp.broadcast_to(x[t:t + 1], (_HG_HALF, w)) for t in range(_HG_HALF)], axis=0)


def _keep_bf16_bits(x):
    bits = lax.bitcast_convert_type(x, jnp.int32) & jnp.int32(-65536)
    return lax.bitcast_convert_type(bits, F32)


def _head_sums(stack_ref, slot, bd):
    pair = bd.shape[0]
    return jnp.concatenate([jnp.dot(stack_ref[slot, :, c0:c0 + pair], bd, preferred_element_type=F32)
                            for c0 in range(0, stack_ref.shape[2], pair)], axis=1)


def _split3(x):
    hi = _keep_bf16_bits(x)
    r1 = x - hi
    mid = _keep_bf16_bits(r1)
    lo = _keep_bf16_bits(r1 - mid)
    return hi.astype(BF16), mid.astype(BF16), lo.astype(BF16)


def _dot3(m01, x):
    hi, mid, lo = _split3(x)
    d = lambda p: jnp.dot(m01, p, preferred_element_type=F32)
    return (d(lo) + d(mid)) + d(hi)


def _hg_prologue(hq, hf, lb, lower, total):
    sq = _sigmoid(hq)
    q = hq * sq
    sg = _sigmoid(hf)
    f = lb + (1.0 - lb) * sg
    g = jnp.log(f)
    k = 1.0 - f
    both = _dot3(jnp.concatenate([lower, total], axis=0), g)
    b, bl = both[:lower.shape[0]], both[lower.shape[0]:]
    return q, k, f, sg, sq, b, bl


def _stack16(fn):
    return [fn(t) for t in range(HG_BLK)]


def hgrn2_fwd(proj, offs, lb, n_seq, seq, *, name, rider=None):
    T = n_seq * seq
    W = HG_HEADS * HG_DIM
    n_tiles = seq // HG_TILE
    nb = HG_TILE // HG_BLK
    lower, _, total, bd, sel_t, _ = _hg_constants()

    def body(hq_ref, hf_ref, hi_ref, lb_ref, lower_ref, total_ref, bd_ref, selt_ref,
             o_ref, st_out_ref,
             st_ref, q_s, k_s, v_s, b_s, qt_s, kt_s, d_s, p_s):
        @pl.when(pl.program_id(1) == 0)
        def _():
            st_ref[...] = jnp.zeros_like(st_ref)

        q, k, _, _, _, b, bl = _hg_prologue(hq_ref[...], hf_ref[...], lb_ref[...], lower_ref[...], total_ref[...])
        q_s[...] = q
        k_s[...] = k
        v_s[...] = hi_ref[...]
        b_s[...] = b
        qt_s[...] = q * jnp.exp(b)
        kt_s[...] = k * jnp.exp(jnp.minimum(bl - b, 0.0))
        d_s[...] = jnp.exp(bl)
        rowi = lax.broadcasted_iota(jnp.int32, (HG_BLK, W), 0)

        def block(i, slot):
            r0 = pl.multiple_of(i * HG_BLK, HG_BLK)
            rows = pl.ds(r0, HG_BLK)
            qi, ki, vi, bi = q_s[rows, :], k_s[rows, :], v_s[rows, :], b_s[rows, :]
            for off, ts, n in _HG_SLABS:
                slab = [jnp.where(rowi[:n] <= t, jnp.exp(jnp.minimum(bi[t:t + 1, :] - bi[:n], 0.0)), 0.0)
                        * qi[t:t + 1, :] * ki[:n] for t in ts]
                p_s[slot, pl.ds(off, HG_BLK), :] = jnp.concatenate(slab, axis=0).astype(BF16)
            a_b = _head_sums(p_s, slot, bd_ref[...])
            o_blk = jnp.dot(selt_ref[...], (a_b * _stack_by_s(vi)).astype(BF16), preferred_element_type=F32)
            qti, kti, di = qt_s[rows, :], kt_s[rows, :], d_s[rows, :]
            outs = []
            for h in range(HG_HEADS):
                hs = slice(h * HG_DIM, (h + 1) * HG_DIM)
                st_h = st_ref[hs, :]
                st_out_ref[i, hs, :] = st_h
                outs.append(lax.dot_general(qti[:, hs].astype(BF16), st_h.astype(BF16),
                                            (((1,), (1,)), ((), ())), preferred_element_type=F32))
                upd = lax.dot_general(vi[:, hs].astype(BF16), kti[:, hs].astype(BF16),
                                      (((0,), (0,)), ((), ())), preferred_element_type=F32)
                st_ref[hs, :] = st_h * di[0:1, hs] + upd
            o_ref[rows, :] = o_blk + jnp.concatenate(outs, axis=1)

        def some_blocks(jj, carry):
            for slot in range(HG_SLOTS):
                block(HG_SLOTS * jj + slot, slot)
            return carry

        lax.fori_loop(0, nb // HG_SLOTS, some_blocks, 0)

    col = lambda off: functools.partial(lambda s, t, blk: (s * n_tiles + t, blk), blk=off // W)
    const = lambda m: pl.BlockSpec(m.shape, lambda s, t: (0, 0))
    tile_f32 = pltpu.VMEM((HG_TILE, W), F32)
    in_specs = [pl.BlockSpec((HG_TILE, W), col(offs[0])), pl.BlockSpec((HG_TILE, W), col(offs[1])),
                pl.BlockSpec((HG_TILE, W), col(offs[2])), const(lb), const(lower), const(total), const(bd),
                const(sel_t)]
    out_specs = [pl.BlockSpec((HG_TILE, W), lambda s, t: (s * n_tiles + t, 0)),
                 pl.BlockSpec((nb, W, HG_DIM), lambda s, t: (s * n_tiles + t, 0, 0))]
    out_shape = [jax.ShapeDtypeStruct((T, W), F32), jax.ShapeDtypeStruct((T // HG_BLK, W, HG_DIM), F32)]
    scratch = [pltpu.VMEM((W, HG_DIM), F32)] + [tile_f32] * 7 + [pltpu.VMEM((HG_SLOTS, HG_STACK, W), BF16)]
    args = [proj, proj, proj, lb, lower, total, bd, sel_t]
    params = _cparams(("arbitrary", "arbitrary"))
    if rider is not None:
        r_in, r_out, r_sems = rider.specs()
        body = rider.wrap(body, len(in_specs), len(out_specs), 2)
        in_specs, out_specs, out_shape = in_specs + r_in, out_specs + r_out, out_shape + rider.out_shape
        scratch, args = scratch + r_sems, args + rider.ins
        params = pltpu.CompilerParams(dimension_semantics=("arbitrary", "arbitrary"),
                                      vmem_limit_bytes=VMEM_LIMIT_BYTES, has_side_effects=True)
    res = pl.pallas_call(body, name=name, grid=(n_seq, n_tiles), in_specs=in_specs, out_specs=out_specs,
                         out_shape=out_shape, scratch_shapes=scratch, compiler_params=params)(*args)
    return res[0], res[1], list(res[2:])


def hgrn2_bwd(proj, offs, lb, do, states, n_seq, seq, *, name):
    T = n_seq * seq
    W = HG_HEADS * HG_DIM
    n_tiles = seq // HG_TILE
    nb = HG_TILE // HG_BLK
    lower, upper, total, bd, sel_t, sel_s = _hg_constants()

    def body(hq_ref, hf_ref, hi_ref, do_ref, st_in_ref, lb_ref, lower_ref, upper_ref, total_ref, bd_ref,
             selt_ref, sels_ref,
             dhq_ref, dhf_ref, dhi_ref, dlb_ref,
             dst_ref, q_s, k_s, v_s, b_s, qt_s, kt_s, d_s, eb_s, ekb_s, dq_s, dk_s, db_s, dv_s,
             p_s, e_s, w_s):
        first = jnp.logical_and(pl.program_id(0) == 0, pl.program_id(1) == 0)

        @pl.when(first)
        def _():
            dlb_ref[...] = jnp.zeros_like(dlb_ref)

        @pl.when(pl.program_id(1) == 0)
        def _():
            dst_ref[...] = jnp.zeros_like(dst_ref)

        hq, lbv = hq_ref[...], lb_ref[...]
        q, k, f, sg, sq, b, bl = _hg_prologue(hq, hf_ref[...], lbv, lower_ref[...], total_ref[...])
        eb = jnp.exp(b)
        ekb = jnp.exp(jnp.minimum(bl - b, 0.0))
        q_s[...] = q
        k_s[...] = k
        v_s[...] = hi_ref[...]
        b_s[...] = b
        eb_s[...] = eb
        ekb_s[...] = ekb
        qt_s[...] = q * eb
        kt_s[...] = k * ekb
        d_s[...] = jnp.exp(bl)
        rowi = lax.broadcasted_iota(jnp.int32, (HG_BLK, W), 0)
        last_row = rowi == HG_BLK - 1

        def block(i, slot):
            r0 = pl.multiple_of(i * HG_BLK, HG_BLK)
            rows = pl.ds(r0, HG_BLK)
            qi, ki, vi, bi, doi = q_s[rows, :], k_s[rows, :], v_s[rows, :], b_s[rows, :], do_ref[rows, :]
            for off, ts, n in _HG_SLABS:
                es = [jnp.where(rowi[:n] <= t, jnp.exp(jnp.minimum(bi[t:t + 1, :] - bi[:n], 0.0)), 0.0) for t in ts]
                sl = pl.ds(off, HG_BLK)
                e_s[slot, sl, :] = jnp.concatenate(es, axis=0)
                p_s[slot, sl, :] = jnp.concatenate([e * qi[t:t + 1, :] * ki[:n] for e, t in zip(es, ts)],
                                                   axis=0).astype(BF16)
                w_s[slot, sl, :] = jnp.concatenate([doi[t:t + 1, :] * vi[:n] for t in ts], axis=0).astype(BF16)
            a_b = _head_sums(p_s, slot, bd_ref[...])
            da_b = _head_sums(w_s, slot, bd_ref[...])
            x = da_b * e_s[slot]
            dq_in = jnp.dot(selt_ref[...], (x * _stack_by_s(ki)).astype(BF16), preferred_element_type=F32)
            dk_in = jnp.dot(sels_ref[...], (x * _stack_by_t(qi)).astype(BF16), preferred_element_type=F32)
            dv_in = jnp.dot(sels_ref[...], (a_b * _stack_by_t(doi)).astype(BF16), preferred_element_type=F32)
            qti, kti, di = qt_s[rows, :], kt_s[rows, :], d_s[rows, :]
            dqt, dkt, dvt, dd = [], [], [], []
            for h in range(HG_HEADS):
                hs = slice(h * HG_DIM, (h + 1) * HG_DIM)
                st_h = st_in_ref[i, hs, :]
                dst_h = dst_ref[hs, :]
                do_h, v_h = doi[:, hs].astype(BF16), vi[:, hs].astype(BF16)
                dst_b = dst_h.astype(BF16)
                dqt.append(jnp.dot(do_h, st_h.astype(BF16), preferred_element_type=F32))
                dkt.append(jnp.dot(v_h, dst_b, preferred_element_type=F32))
                dvt.append(lax.dot_general(kti[:, hs].astype(BF16), dst_b, (((1,), (1,)), ((), ())),
                                           preferred_element_type=F32))
                dd.append(jnp.sum(dst_h * st_h, axis=0, keepdims=True))
                upd = lax.dot_general(do_h, qti[:, hs].astype(BF16), (((0,), (0,)), ((), ())),
                                      preferred_element_type=F32)
                dst_ref[hs, :] = dst_h * di[0:1, hs] + upd
            dqt = jnp.concatenate(dqt, axis=1)
            dkt = jnp.concatenate(dkt, axis=1)
            dvt = jnp.concatenate(dvt, axis=1)
            dd = jnp.concatenate(dd, axis=1)
            dbl = jnp.sum(dkt * kti, axis=0, keepdims=True) + dd * di[0:1, :]
            db = qi * dq_in - ki * dk_in + dqt * qti - dkt * kti
            db_s[rows, :] = db + jnp.where(last_row, dbl, 0.0)
            dq_s[rows, :] = dq_in + dqt * eb_s[rows, :]
            dk_s[rows, :] = dk_in + dkt * ekb_s[rows, :]
            dv_s[rows, :] = dv_in + dvt

        def some_blocks(jj, carry):
            for slot in range(HG_SLOTS):
                block(nb - 1 - slot - HG_SLOTS * jj, slot)
            return carry

        lax.fori_loop(0, nb // HG_SLOTS, some_blocks, 0)

        dg = _dot3(upper_ref[...], db_s[...])
        dhq_ref[...] = (dq_s[...] * (sq * (1.0 + hq * (1.0 - sq)))).astype(dhq_ref.dtype)
        df = dg / f - dk_s[...]
        dhf_ref[...] = (df * (1.0 - lbv) * (sg * (1.0 - sg))).astype(dhf_ref.dtype)
        dhi_ref[...] = dv_s[...].astype(dhi_ref.dtype)
        dlb_ref[...] += _colsum(df * (1.0 - sg))

    rev = lambda s, t: s * n_tiles + (n_tiles - 1 - t)
    col = lambda off: functools.partial(lambda s, t, blk: (rev(s, t), blk), blk=off // W)
    const = lambda m: pl.BlockSpec(m.shape, lambda s, t: (0, 0))
    row = pl.BlockSpec((HG_TILE, W), lambda s, t: (rev(s, t), 0))
    tile_f32 = pltpu.VMEM((HG_TILE, W), F32)
    n2 = HG_STACK
    return pl.pallas_call(
        body, name=name,
        grid=(n_seq, n_tiles),
        in_specs=[pl.BlockSpec((HG_TILE, W), col(offs[0])), pl.BlockSpec((HG_TILE, W), col(offs[1])),
                  pl.BlockSpec((HG_TILE, W), col(offs[2])), row,
                  pl.BlockSpec((nb, W, HG_DIM), lambda s, t: (rev(s, t), 0, 0)),
                  const(lb), const(lower), const(upper), const(total), const(bd), const(sel_t), const(sel_s)],
        out_specs=[row, row, row, pl.BlockSpec((1, W), lambda s, t: (0, 0))],
        out_shape=[jax.ShapeDtypeStruct((T, W), BF16)] * 3 + [jax.ShapeDtypeStruct((1, W), F32)],
        scratch_shapes=[pltpu.VMEM((W, HG_DIM), F32)] + [tile_f32] * 13
                       + [pltpu.VMEM((HG_SLOTS, n2, W), BF16), pltpu.VMEM((HG_SLOTS, n2, W), F32),
                          pltpu.VMEM((HG_SLOTS, n2, W), BF16)],
        compiler_params=_cparams(("arbitrary", "arbitrary")),
    )(proj, proj, proj, do, states, lb, lower, upper, total, bd, sel_t, sel_s)


def _diag_mask(tq):
    return lax.broadcasted_iota(jnp.int32, (tq, tq), 1) <= lax.broadcasted_iota(jnp.int32, (tq, tq), 0)


def _qk(q, k):
    return lax.dot_general(q, k, (((1,), (1,)), ((), ())), preferred_element_type=F32)


def _causal_pairs(n, sweeps=1, by_key=False):
    if by_key:
        rows = [(i, j, 0) for j in range(n) for i in range(j, n)]
    else:
        rows = [(i, j, s) for i in range(n) for s in range(sweeps) for j in range(i + 1)]
    return tuple(jnp.asarray(np.array([r[c] for r in rows], np.int32)) for c in range(3))


def _fox_placement(fh):
    hw, wa = fh * FOX_HDIM, fh * FOX_AUG
    pq, pk = np.zeros((hw, wa), np.float32), np.zeros((hw, wa), np.float32)
    aq, ak = np.zeros((3 * LANES, wa), np.float32), np.zeros((3 * LANES, wa), np.float32)
    oq, ok = np.zeros((1, wa), np.float32), np.zeros((1, wa), np.float32)
    for h in range(fh):
        src, dst = np.arange(h * FOX_HDIM, (h + 1) * FOX_HDIM), np.arange(h * FOX_AUG, h * FOX_AUG + FOX_HDIM)
        pq[src, dst] = FOX_HDIM ** -0.5
        pk[src, dst] = 1.0
        gate = h * FOX_AUG + FOX_HDIM
        for r in range(3):
            aq[r * LANES + h, gate + r] = 1.0
            ak[r * LANES + h, gate + 3 + r] = -1.0
        oq[0, gate + 3:gate + 6] = 1.0
        ok[0, gate:gate + 3] = 1.0
    bf = lambda m: jnp.asarray(m, dtype=BF16)
    return {"pq": bf(pq), "pk": bf(pk), "aq": bf(aq), "ak": bf(ak), "oq": jnp.asarray(oq), "ok": jnp.asarray(ok),
            "pqt": bf(pq.T), "pkt": bf(pk.T)}


def _fox_specs(tq, fh, heads=1):
    groups = fh // heads

    def spec(tab):
        return pl.BlockSpec((None, tq, heads * FOX_AUG), lambda b, t, *tabs: (b // groups, tabs[tab][t], b % groups))
    return spec(0), spec(1)


def fox_fwd(qa, ka, va, *, name):
    n_seq, S, width = qa.shape
    fh = width // FOX_AUG
    nh = FOX_FWD_HEADS
    BH = n_seq * fh // nh
    tq = min(FOX_TQ, S)
    itab, jtab, _ = _causal_pairs(S // tq)

    def body(itab_ref, jtab_ref, q_ref, k_ref, v_ref, o_ref, ox_ref, lse_ref, *scratch):
        t = pl.program_id(1)
        i, j = itab_ref[t], jtab_ref[t]
        per_head = [scratch[4 * h:4 * h + 4] for h in range(nh)]

        @pl.when(j == 0)
        def _():
            for m_s, l_s, acc_s, acc_lo_s in per_head:
                m_s[...] = jnp.full_like(m_s, NEG_INF)
                l_s[...] = jnp.zeros_like(l_s)
                acc_s[...] = jnp.zeros_like(acc_s)
                acc_lo_s[...] = jnp.zeros_like(acc_lo_s)

        def step(on_diagonal):
            for h, (m_s, l_s, acc_s, acc_lo_s) in enumerate(per_head):
                lanes = slice(h * FOX_AUG, (h + 1) * FOX_AUG)
                s = _qk(q_ref[:, lanes], k_ref[:, lanes])
                if on_diagonal:
                    s = jnp.where(_diag_mask(tq), s, NEG_INF)
                m_prev = m_s[...]
                m_new = jnp.maximum(m_prev, jnp.max(s, axis=-1, keepdims=True))
                alpha = jnp.exp(m_prev - m_new)
                p = jnp.exp(s - m_new[:, 0:1])
                p_hi = p.astype(BF16)
                p_lo = (p - p_hi.astype(F32)).astype(BF16)
                v = v_ref[:, lanes]
                l_s[...] = alpha * l_s[...] + jnp.sum(p, axis=-1, keepdims=True)
                acc_s[...] = alpha * acc_s[...] + jnp.dot(p_hi, v, preferred_element_type=F32)
                acc_lo_s[...] = alpha * acc_lo_s[...] + jnp.dot(p_lo, v, preferred_element_type=F32)
                m_s[...] = m_new

        @pl.when(j < i)
        def _():
            step(False)

        @pl.when(j == i)
        def _():
            step(True)
            for h, (m_s, l_s, acc_s, acc_lo_s) in enumerate(per_head):
                lanes = slice(h * FOX_AUG, (h + 1) * FOX_AUG)
                inv_l = 1.0 / l_s[...]
                o_ref[:, lanes] = (acc_s[...] * inv_l).astype(o_ref.dtype)
                ox_ref[:, lanes] = (acc_s[...] + acc_lo_s[...]) * inv_l
                lse_ref[:, lanes] = m_s[...] + jnp.log(l_s[...])

    qspec, kspec = _fox_specs(tq, fh, nh)
    wide = jax.ShapeDtypeStruct((n_seq, S, width), F32)
    return pl.pallas_call(
        body, name=name,
        grid_spec=pltpu.PrefetchScalarGridSpec(
            num_scalar_prefetch=2, grid=(BH, itab.shape[0]),
            in_specs=[qspec, kspec, kspec],
            out_specs=[qspec, qspec, qspec],
            scratch_shapes=[pltpu.VMEM((tq, LANES), F32)] * (4 * nh)),
        out_shape=[jax.ShapeDtypeStruct((n_seq, S, width), BF16), wide, wide],
        compiler_params=_cparams(("parallel", "arbitrary")),
    )(itab, jtab, qa, ka, va)


def _fox_ds(q, k, v, do, ox, lse, on_diagonal):
    s = _qk(q, k)
    if on_diagonal:
        s = jnp.where(_diag_mask(s.shape[0]), s, NEG_INF)
    p = jnp.exp(s - lse[:, 0:1])
    delta = jnp.sum(do.astype(F32) * ox, axis=-1, keepdims=True)
    return p, p * (_qk(do, v) - delta)


def fox_bwd(qa, ka, va, do, ox, lse, *, name):
    n_seq, S, width = qa.shape
    fh = width // FOX_AUG
    BH = n_seq * fh
    tq = min(FOX_TQ, S)
    itab, jtab, _ = _causal_pairs(S // tq)

    def body(itab_ref, jtab_ref, q_ref, k_ref, v_ref, do_ref, ox_ref, lse_ref, dq_ref, dk_ref, dv_ref, dsum_ref):
        t = pl.program_id(1)
        i, j = itab_ref[t], jtab_ref[t]

        @pl.when(t == 0)
        def _():
            dq_ref[...] = jnp.zeros_like(dq_ref)
            dk_ref[...] = jnp.zeros_like(dk_ref)
            dv_ref[...] = jnp.zeros_like(dv_ref)
            dsum_ref[...] = jnp.zeros_like(dsum_ref)

        q_rows = pl.ds(pl.multiple_of(i * tq, tq), tq)
        k_rows = pl.ds(pl.multiple_of(j * tq, tq), tq)

        def step(on_diagonal):
            q, k, do = q_ref[...], k_ref[...], do_ref[...]
            p, ds = _fox_ds(q, k, v_ref[...], do, ox_ref[...], lse_ref[...], on_diagonal)
            ds_b = ds.astype(BF16)
            tn = (((0,), (0,)), ((), ()))
            dq_ref[q_rows, :] += jnp.dot(ds_b, k, preferred_element_type=F32)
            dk_ref[k_rows, :] += lax.dot_general(ds_b, q, tn, preferred_element_type=F32)
            dv_ref[k_rows, :] += lax.dot_general(p.astype(BF16), do, tn, preferred_element_type=F32)
            dsum_ref[:, k_rows] += _colsum(ds)

        @pl.when(j < i)
        def _():
            step(False)

        @pl.when(j == i)
        def _():
            step(True)

    qspec, kspec = _fox_specs(tq, fh)
    whole = pl.BlockSpec((None, S, FOX_AUG), lambda b, t, it, jt: (b // fh, 0, b % fh))
    wide = jax.ShapeDtypeStruct((n_seq, S, width), F32)
    return pl.pallas_call(
        body, name=name,
        grid_spec=pltpu.PrefetchScalarGridSpec(
            num_scalar_prefetch=2, grid=(BH, itab.shape[0]),
            in_specs=[qspec, kspec, kspec, qspec, qspec, qspec],
            out_specs=[whole, whole, whole, pl.BlockSpec((None, 1, S), lambda b, t, it, jt: (b, 0, 0))]),
        out_shape=[wide, wide, wide, jax.ShapeDtypeStruct((BH, 1, S), F32)],
        compiler_params=_cparams(("parallel", "arbitrary")),
    )(itab, jtab, qa, ka, va, do, ox, lse)


def seq_cumsum(x, n_seq, seq, *, reverse, name):
    T, C = x.shape
    tb = min(256, seq)
    n = seq // tb
    r = np.arange(tb)
    tri = (r[None, :] >= r[:, None]) if reverse else (r[None, :] <= r[:, None])
    tri = jnp.asarray(tri.astype(np.float32), dtype=BF16)

    def body(x_ref, tri_ref, o_ref, carry_s):
        @pl.when(pl.program_id(1) == 0)
        def _():
            carry_s[...] = jnp.zeros_like(carry_s)

        xv = x_ref[...]
        o_ref[...] = _dot3(tri_ref[...], xv) + carry_s[...]
        carry_s[...] += _colsum(xv)

    blk = (lambda s, t: (s * n + (n - 1 - t), 0)) if reverse else (lambda s, t: (s * n + t, 0))
    return pl.pallas_call(
        body, name=name,
        grid=(n_seq, n),
        in_specs=[pl.BlockSpec((tb, C), blk), pl.BlockSpec((tb, tb), lambda s, t: (0, 0))],
        out_specs=pl.BlockSpec((tb, C), blk),
        out_shape=jax.ShapeDtypeStruct((T, C), F32),
        scratch_shapes=[pltpu.VMEM((1, C), F32)],
        compiler_params=_cparams(("arbitrary", "arbitrary")),
    )(x, tri)


def _place():
    return lax.axis_index("x"), lax.axis_index("y"), lax.axis_index("c")


def _other_chips(x, y):
    return [(1 - x, y), (x, 1 - y), (1 - x, 1 - y)]


def _hbm_call(body, ins, out_shape, n_sems, *, name):
    hbm = pl.BlockSpec(memory_space=pl.ANY)
    return pl.pallas_call(
        body, name=name,
        in_specs=[hbm] * len(ins), out_specs=[hbm] * len(out_shape), out_shape=out_shape,
        scratch_shapes=[pltpu.SemaphoreType.DMA((n_sems,)), pltpu.SemaphoreType.DMA((n_sems,)),
                        pltpu.SemaphoreType.DMA((len(ins),))],
        compiler_params=pltpu.CompilerParams(has_side_effects=True),
    )(*ins)


def allgather_chips(shards, *, name):
    return _exchange_call(allgather_rider(shards), name=name)


def _allgather_ops(x_refs, o_refs, send_sems, recv_sems, local_sems):
    def copies():
        x, y, c = _place()
        me = 2 * x + y
        chips = _other_chips(x, y)
        own, first, passed, landed, handed = [], [], [], [], []
        for b, (x_ref, o_ref) in enumerate(zip(x_refs, o_refs)):
            half = x_ref.shape[0] // 2
            mine, theirs = pl.ds(c * half, half), pl.ds((1 - c) * half, half)
            own.append(pltpu.make_async_copy(x_ref, o_ref.at[me], local_sems.at[b]))

            def copy(k, src, chip, rows, to, o_ref=o_ref, b=b):
                return pltpu.make_async_remote_copy(src_ref=src, dst_ref=o_ref.at[2 * chip[0] + chip[1], rows],
                                                    send_sem=send_sems.at[6 * b + k], recv_sem=recv_sems.at[6 * b + k],
                                                    device_id=to, device_id_type=MESH)
            for j, chip in enumerate(chips):
                first.append(copy(j, x_ref.at[mine], (x, y), mine, (*chip, c)))
                landed.append(copy(j, x_ref.at[mine], chip, mine, (*chip, c)))
                passed.append(copy(3 + j, o_ref.at[2 * chip[0] + chip[1], mine], chip, mine, (x, y, 1 - c)))
                handed.append(copy(3 + j, x_ref.at[mine], chip, theirs, (x, y, 1 - c)))
        return own, first, passed, landed, handed

    def start():
        own, first, _, _, _ = copies()
        for cp in own + first:
            cp.start()

    def finish():
        own, first, passed, landed, handed = copies()
        for arrived, forward in zip(landed, passed):
            arrived.wait_recv()
            forward.start()
        for cp in handed:
            cp.wait_recv()
        for cp in first + passed:
            cp.wait_send()
        for cp in own:
            cp.wait()
    return start, finish


def _scatter_ops(x_refs, o_refs, send_sems, recv_sems, local_sems):
    def copies():
        x, y, c = _place()
        return [pltpu.make_async_remote_copy(
            src_ref=x_ref.at[2 * px + py], dst_ref=o_ref.at[j], send_sem=send_sems.at[3 * b + j],
            recv_sem=recv_sems.at[3 * b + j], device_id=(px, py, c), device_id_type=MESH)
            for b, (x_ref, o_ref) in enumerate(zip(x_refs, o_refs)) for j, (px, py) in enumerate(_other_chips(x, y))]

    def start():
        for cp in copies():
            cp.start()

    def finish():
        sends = copies()
        for cp in sends:
            cp.wait_recv()
        for cp in sends:
            cp.wait_send()
    return start, finish


class Rider(NamedTuple):
    ins: list
    out_shape: list
    n_sems: int
    ops: object

    def specs(self):
        hbm = pl.BlockSpec(memory_space=pl.ANY)
        sems = [pltpu.SemaphoreType.DMA((self.n_sems,)), pltpu.SemaphoreType.DMA((self.n_sems,)),
                pltpu.SemaphoreType.DMA((len(self.ins),))]
        return [hbm] * len(self.ins), [hbm] * len(self.out_shape), sems

    def wrap(self, body, n_in, n_out, grid_rank):
        k_in, k_out = len(self.ins), len(self.out_shape)

        def carried(*refs):
            ins, r_ins = refs[:n_in], refs[n_in:n_in + k_in]
            outs = refs[n_in + k_in:n_in + k_in + n_out]
            r_outs = refs[n_in + k_in + n_out:n_in + k_in + n_out + k_out]
            scratch, sems = refs[n_in + k_in + n_out + k_out:-3], refs[-3:]
            first = functools.reduce(jnp.logical_and, [pl.program_id(a) == 0 for a in range(grid_rank)])
            last = functools.reduce(jnp.logical_and,
                                    [pl.program_id(a) == pl.num_programs(a) - 1 for a in range(grid_rank)])
            pl.when(first)(lambda: self.ops(r_ins, r_outs, *sems)[0]())
            body(*ins, *outs, *scratch)
            pl.when(last)(lambda: self.ops(r_ins, r_outs, *sems)[1]())
        return carried


def _exchange_call(rider, *, name):
    def body(*refs):
        k = len(rider.ins)
        start, finish = rider.ops(refs[:k], refs[k:k + len(rider.out_shape)], *refs[-3:])
        start()
        finish()
    in_specs, out_specs, sems = rider.specs()
    return pl.pallas_call(body, name=name, in_specs=in_specs, out_specs=out_specs, out_shape=rider.out_shape,
                          scratch_shapes=sems, compiler_params=pltpu.CompilerParams(has_side_effects=True))(*rider.ins)


def allgather_rider(shards):
    assert all(s.shape[0] % (2 * ROW_ALIGN) == 0 for s in shards)
    return Rider(list(shards), [jax.ShapeDtypeStruct((4,) + s.shape, s.dtype) for s in shards], 6 * len(shards),
                 _allgather_ops)


def scatter_rider(parts):
    return Rider(list(parts), [jax.ShapeDtypeStruct((3,) + p.shape[1:], p.dtype) for p in parts], 3 * len(parts),
                 _scatter_ops)


def scatter_chips(parts, *, name):
    return _exchange_call(scatter_rider(parts), name=name)


def swap_cores(vs, *, name):
    nb = len(vs)

    def body(*refs):
        x_refs, o_refs = refs[:nb], refs[nb:2 * nb]
        send_sems, recv_sems, _ = refs[2 * nb:]
        x, y, c = _place()
        copies = [pltpu.make_async_remote_copy(src_ref=x_ref, dst_ref=o_ref, send_sem=send_sems.at[b],
                                               recv_sem=recv_sems.at[b], device_id=(x, y, 1 - c), device_id_type=MESH)
                  for b, (x_ref, o_ref) in enumerate(zip(x_refs, o_refs))]
        for cp in copies:
            cp.start()
        for cp in copies:
            cp.wait()

    return _hbm_call(body, vs, [jax.ShapeDtypeStruct(v.shape, v.dtype) for v in vs], nb, name=name)


def allreduce_small(v, *, name):
    R, C = v.shape

    def body(x_ref, o_ref, gath_ref, send_sems, recv_sems):
        x, y, c = _place()
        me = 4 * x + 2 * y + c
        gath_ref[me] = x_ref[...]
        flips = [(k >> 2 & 1, k >> 1 & 1, k & 1) for k in range(1, 8)]
        sends = []
        for j, (fx, fy, fc) in enumerate(flips):
            peer = (x ^ fx, y ^ fy, c ^ fc)
            cp = pltpu.make_async_remote_copy(src_ref=x_ref, dst_ref=gath_ref.at[me], send_sem=send_sems.at[j],
                                              recv_sem=recv_sems.at[j], device_id=peer, device_id_type=MESH)
            cp.start()
            sends.append(cp)
        for j, (fx, fy, fc) in enumerate(flips):
            peer = (x ^ fx, y ^ fy, c ^ fc)
            pltpu.make_async_remote_copy(src_ref=x_ref, dst_ref=gath_ref.at[4 * peer[0] + 2 * peer[1] + peer[2]],
                                         send_sem=send_sems.at[j], recv_sem=recv_sems.at[j], device_id=peer,
                                         device_id_type=MESH).wait_recv()
        for cp in sends:
            cp.wait_send()
        total = gath_ref[0]
        for d in range(1, 8):
            total = total + gath_ref[d]
        o_ref[...] = total

    vm = pl.BlockSpec(memory_space=pltpu.VMEM)
    out, _ = pl.pallas_call(
        body, name=name,
        in_specs=[vm], out_specs=[vm, vm],
        out_shape=[jax.ShapeDtypeStruct((R, C), F32), jax.ShapeDtypeStruct((8, R, C), F32)],
        scratch_shapes=[pltpu.SemaphoreType.DMA((7,)), pltpu.SemaphoreType.DMA((7,))],
        compiler_params=pltpu.CompilerParams(has_side_effects=True),
    )(v)
    return out


ROW_ALIGN = 16
PACK_W = 1024
SUM_TILE = 512
BIG_WEIGHTS = (("w_in", 1), ("w_a", 1), ("w_b", 1), ("w_o", 0), ("w_ff1", 1), ("w_ff2", 0), ("w_pg", 0), ("w_p", 1))


def _b_layout(d, ple):
    hw, q = d // 2, d // 4
    small = 2 * d + 2 * q
    lay = {"w_ff1": (0, 0, d, d), "w_ff2": (d, 0, d, d), "w_o": (2 * d, 0, q, d), "w_pg": (2 * d + q, 0, q, d),
           "w_a": (small, 0, hw, q), "w_b": (small, q, hw, q), "w_p": (small, 2 * q, ple, q)}
    return lay, small + hw


def pack_a(w_in_shard):
    rows, cols = w_in_shard.shape
    pad = -cols % LANES
    return jnp.concatenate([w_in_shard, jnp.zeros((rows, pad), w_in_shard.dtype)], axis=1)


def pack_b(shards, d):
    hw, q = d // 2, d // 4
    dt = shards["w_a"].dtype
    wp = shards["w_p"]
    wp = jnp.concatenate([wp, jnp.zeros((hw - wp.shape[0], q), dt)], axis=0)
    small = jnp.concatenate([shards["w_a"], shards["w_b"], wp, jnp.zeros((hw, d - 3 * q), dt)], axis=1)
    return jnp.concatenate([shards["w_ff1"], shards["w_ff2"], shards["w_o"], shards["w_pg"], small], axis=0)


def unpack_b(buf, lay):
    return {nm: buf[r0:r0 + rows, c0:c0 + cols] for nm, (r0, c0, rows, cols) in lay.items()}


def _win_layout(d):
    hw = d // 2
    fh = hw // FOX_HDIM
    orig = {"hq": (0, hw), "hf": (hw, hw), "hi": (2 * hw, hw), "hg": (3 * hw, hw), "fq": (4 * hw, hw),
            "fk": (5 * hw, hw), "fv": (6 * hw, hw), "ff": (7 * hw, fh), "ga": (7 * hw + fh, d), "gb": (7 * hw + fh + d, d)}
    order = ["ga", "gb", "hq", "hf", "hi", "hg", "fq", "fk", "fv", "ff"]
    mine, off = {}, 0
    for nm in order:
        width = orig[nm][1] if nm != "ff" else LANES
        mine[nm] = (off, width)
        off += width
    return orig, order, mine, off


def _adam_fn(rows, vecs):
    w, g, m, v = rows
    m2 = ADAM_B1 * m + (1.0 - ADAM_B1) * g
    v2 = ADAM_B2 * v + (1.0 - ADAM_B2) * (g * g)
    m_hat = m2 / (1.0 - ADAM_B1 ** ADAM_STEP)
    v_hat = v2 / (1.0 - ADAM_B2 ** ADAM_STEP)
    delta = -ADAM_LR * (m_hat / (jnp.sqrt(v_hat) + ADAM_EPS) + ADAM_WD * w)
    return [delta, m2, v2], []


def adamw_small(small, p0, ws, ms, vs, *, name):
    n = len(ws)
    hw = p0.shape[1]
    fh = ws[8].shape[1]

    def body(small_ref, p0_ref, *refs):
        w_refs, m_refs, v_refs = refs[:n], refs[n:2 * n], refs[2 * n:3 * n]
        g_out, d_out, m_out, v_out = (refs[(3 + k) * n:(4 + k) * n] for k in range(4))
        sm = small_ref[...]
        p = p0_ref[...]
        d_lb = sm[6:7, hw:2 * hw] * (p * (1.0 - p))
        grads = [sm[r:r + 1, :] for r in range(6)]
        grads += [jnp.concatenate([d_lb, -d_lb], axis=0), sm[6:7, :hw], sm[7:8, :fh]]
        for i in range(n):
            (delta, m2, v2), _ = _adam_fn([w_refs[i][...], grads[i], m_refs[i][...], v_refs[i][...]], [])
            g_out[i][...], d_out[i][...], m_out[i][...], v_out[i][...] = grads[i], delta, m2, v2

    shapes = [jax.ShapeDtypeStruct(w.shape, F32) for w in ws]
    return pl.pallas_call(body, name=name, out_shape=shapes * 4)(small, p0, *ws, *ms, *vs)


def adamw(w, g, m, v, *, name):
    c = w.shape[1]
    (delta, m2, v2), _ = rowwise(_adam_fn, [w, g, m, v], [], [(c, F32)] * 3, name=name, tm=256)
    return delta, m2, v2


def kernel(x, p, ln0_g, ln0_b, w_in, hg_lb, hg_norm_g, fox_fb, w_a, w_b, w_o, ln1_g, ln1_b, w_ff1, w_ff2, w_pg, w_p, ln2_g, ln2_b, loss_target, m_ln0_g, m_ln0_b, m_w_in, m_hg_lb, m_hg_norm_g, m_fox_fb, m_w_a, m_w_b, m_w_o, m_ln1_g, m_ln1_b, m_w_ff1, m_w_ff2, m_w_pg, m_w_p, m_ln2_g, m_ln2_b, v_ln0_g, v_ln0_b, v_w_in, v_hg_lb, v_hg_norm_g, v_fox_fb, v_w_a, v_w_b, v_w_o, v_ln1_g, v_ln1_b, v_w_ff1, v_w_ff2, v_w_pg, v_w_p, v_ln2_g, v_ln2_b):
    n_seq, seq, d = x.shape
    T = n_seq * seq
    hw = d // 2
    fh = hw // FOX_HDIM
    bh = n_seq * fh
    orig, order, mine, n_in = _win_layout(d)

    big = {"w_in": w_in[0], "w_a": w_a[0], "w_b": w_b[0], "w_o": w_o[0], "w_ff1": w_ff1[0], "w_ff2": w_ff2[0],
           "w_pg": w_pg[0], "w_p": w_p[0]}
    big_m = {"w_in": m_w_in[0], "w_a": m_w_a[0], "w_b": m_w_b[0], "w_o": m_w_o[0], "w_ff1": m_w_ff1[0],
             "w_ff2": m_w_ff2[0], "w_pg": m_w_pg[0], "w_p": m_w_p[0]}
    big_v = {"w_in": v_w_in[0], "w_a": v_w_a[0], "w_b": v_w_b[0], "w_o": v_w_o[0], "w_ff1": v_w_ff1[0],
             "w_ff2": v_w_ff2[0], "w_pg": v_w_pg[0], "w_p": v_w_p[0]}
    names = [nm for nm, _ in BIG_WEIGHTS]
    axis = dict(BIG_WEIGHTS)
    ple = w_p.shape[1]
    lay, b_rows = _b_layout(d, ple)
    in_cols = big["w_in"].shape[1]

    gather_w_in = allgather_rider([pack_a(big["w_in"].astype(BF16))])
    gather_rest = allgather_rider([pack_b({nm: big[nm].astype(BF16) for nm in names if nm != "w_in"}, d)])

    x2 = x.reshape(T, d)
    tgt = loss_target.reshape(T, d)
    p_b = p.reshape(T, p.shape[-1]).astype(BF16)
    vec = lambda a: a.reshape(1, -1)
    probs = jax.nn.softmax(hg_lb, axis=0)
    lb = vec(probs[0])

    def ln0_fn(rows, vecs):
        h = _ln_stats(rows[0]) * vecs[0] + vecs[1]
        return [h, h], []
    (h0, h0b), _, (a_all,) = rowwise(ln0_fn, [x2], [vec(ln0_g), vec(ln0_b)], [(d, F32), (d, BF16)], name="ln0_fwd",
                                     rider=gather_w_in)
    win = jnp.concatenate([a_all[s, :, :in_cols] for s in range(4)], axis=1)
    win_mine = jnp.concatenate(
        [win[:, orig[nm][0]:orig[nm][0] + orig[nm][1]] for nm in order]
        + [jnp.zeros((d, LANES - fh), BF16)], axis=1)
    proj = matmul_nn(h0b, win_mine, name="in_proj")

    o_raw, hg_states, (b_all,) = hgrn2_fwd(proj, [mine["hq"][0], mine["hf"][0], mine["hi"][0]], lb, n_seq, seq,
                                           name="hgrn2_fwd", rider=gather_rest)
    view = lambda nm, k, n: WView(b_all, lay[nm][0], lay[nm][1], k, n, axis[nm])
    w_ff1_v, w_ff2_v = view("w_ff1", d, 4 * d), view("w_ff2", 4 * d, d)

    def whole(nm):
        r0, c0, rows, cols = lay[nm]
        return jnp.concatenate([b_all[s, r0:r0 + rows, c0:c0 + cols] for s in range(4)], axis=axis[nm])
    w_o_v, w_pg_v, w_a_v, w_p_v, w_b_full = whole("w_o"), whole("w_pg"), whole("w_a"), whole("w_p"), whole("w_b")

    def ya_fn(rows, vecs):
        o, hg = rows
        outs = []
        for h in range(HG_HEADS):
            oh = o[:, h * HG_DIM:(h + 1) * HG_DIM]
            outs.append(oh * lax.rsqrt(jnp.mean(oh * oh, axis=-1, keepdims=True) + RMS_EPS))
        y = jnp.concatenate(outs, axis=1) * vecs[0] * (hg * _sigmoid(hg))
        return [y], []
    (y_a,), _ = rowwise(ya_fn, [o_raw, (proj,) + mine["hg"]], [hg_norm_g], [(hw, BF16)], name="hgrn2_out_fwd")

    fb_pad = jnp.concatenate([fox_fb, jnp.zeros((1, LANES - fh), F32)], axis=1)

    def lf_fn(rows, vecs):
        u = rows[0] + vecs[0]
        return [jnp.minimum(u, 0.0) - jnp.log(1.0 + jnp.exp(-jnp.abs(u)))], []
    (lf,), _ = rowwise(lf_fn, [(proj,) + mine["ff"]], [fb_pad], [(LANES, F32)], name="fox_logf")
    c_cum = seq_cumsum(lf, n_seq, seq, reverse=False, name="fox_cumsum")

    place = _fox_placement(fh)

    def prep_fn(rows, vecs):
        fq_, fk_, fv_, cc = rows
        pq, pk, aq, ak, oq, ok = vecs
        parts = jnp.concatenate(_split3(cc), axis=1)
        mm = lambda a_, b_: jnp.dot(a_, b_, preferred_element_type=F32)
        q_ = mm(fq_.astype(BF16), pq) + mm(parts, aq) + oq
        k_ = mm(fk_.astype(BF16), pk) + mm(parts, ak) + ok
        return [q_, k_, mm(fv_.astype(BF16), pk)], []
    wa = fh * FOX_AUG
    (qa, ka, va), _ = rowwise(prep_fn, [(proj,) + mine["fq"], (proj,) + mine["fk"], (proj,) + mine["fv"], c_cum],
                              [place[nm] for nm in ("pq", "pk", "aq", "ak", "oq", "ok")], [(wa, BF16)] * 3,
                              name="fox_prep")
    as_seq = lambda t2d: t2d.reshape(n_seq, seq, t2d.shape[1])
    o_fox, ox_fox, lse = fox_fwd(as_seq(qa), as_seq(ka), as_seq(va), name="fox_fwd")
    y_b = o_fox.reshape(T, wa)
    wb_pad = jnp.concatenate([w_b_full.reshape(fh, FOX_HDIM, d), jnp.zeros((fh, FOX_AUG - FOX_HDIM, d), BF16)],
                             axis=1).reshape(wa, d)

    fused_tm = 512
    pa = matmul_nn(y_a, w_a_v, name="proj_a")

    def merge_post(pb_, aux, vecs):
        ga, gb, a = aux
        return [_sigmoid(ga) * a + _sigmoid(gb) * pb_, pb_], []
    (merged, pb), _ = matmul_nn(y_b, wb_pad, name="proj_b_merge", tm=fused_tm, post=merge_post,
                                post_aux=[(proj,) + mine["ga"], (proj,) + mine["gb"], pa], post_outs=[BF16, F32])

    def ln1_post(mix, aux, vecs):
        z = ALPHA * aux[0] + mix
        h = _ln_stats(z) * vecs[0] + vecs[1]
        return [z, h, h], []
    (z1, h1, h1b), _ = matmul_nn(merged, w_o_v, name="out_proj_ln1", tm=fused_tm, post=ln1_post, post_aux=[h0],
                                 post_vecs=[ln1_g, ln1_b], post_outs=[F32, F32, BF16])

    relu2 = lambda u: jnp.square(jnp.maximum(u, 0.0))
    act = matmul_nn(h1b, w_ff1_v, name="ff1", out_dtype=BF16, epilogue=relu2)
    pg = matmul_nn(h1b, w_pg_v, name="ple_gate")
    pe = matmul_nn(p_b, w_p_v, name="ple_embed")

    def head_post(ffv, aux, vecs):
        h1v, pgv, pev, t = aux
        g2, b2 = vecs
        sp = _sigmoid(pgv)
        z = ALPHA * h1v + ffv + sp * pev
        y = _ln_stats(z) * g2 + b2
        err = y - t
        loss_rows = 0.5 * jnp.mean(err * err, axis=-1, keepdims=True)
        dy = err * (1.0 / d)
        dz, dg2, db2 = _ln_bwd(z, dy, g2)
        loss_acc = jnp.broadcast_to(_colsum(loss_rows), (1, d))
        return [dz, dz, dz * pev * (sp * (1.0 - sp)), dz * sp], [dg2, db2, loss_acc]
    (dz2, dz2b, dpg, dpe), (g_ln2_g, g_ln2_b, loss_part) = matmul_nn(
        act, w_ff2_v, name="ff2_head", tm=fused_tm, post=head_post, post_aux=[h1, pg, pe, tgt],
        post_vecs=[ln2_g, ln2_b], post_outs=[F32, BF16, BF16, BF16], post_accs=[d, d, d])

    dact = lambda da, a: da * (2.0 * jnp.sqrt(a.astype(F32)))
    du = matmul_nn(dz2b, w_ff2_v, transpose_rhs=True, name="d_ff2", out_dtype=BF16, epilogue=dact, aux=act)
    dh1_pg = matmul_nn(dpg, w_pg_v, transpose_rhs=True, name="d_ple_gate")

    def ln1_bwd_post(dh1_ff, aux, vecs):
        dh1 = ALPHA * aux[0] + dh1_ff + aux[1]
        dz, dg, db = _ln_bwd(aux[2], dh1, vecs[0])
        return [dz, dz], [dg, db]
    (dz1, dz1b), (g_ln1_g, g_ln1_b) = matmul_nn(
        du, w_ff1_v, transpose_rhs=True, name="d_ff1_ln1", tm=fused_tm, post=ln1_bwd_post, post_aux=[dz2, dh1_pg, z1],
        post_vecs=[ln1_g], post_outs=[F32, BF16], post_accs=[d, d])

    def merge_bwd_post(dm, aux, vecs):
        ga, gb, a, b = aux
        sa, sb = _sigmoid(ga), _sigmoid(gb)
        return [dm * a * (sa * (1.0 - sa)), dm * b * (sb * (1.0 - sb)), dm * sa, dm * sb], []
    (dga, dgb, dma, dmb), _ = matmul_nn(
        dz1b, w_o_v, transpose_rhs=True, name="d_out_proj_merge", tm=fused_tm, post=merge_bwd_post,
        post_aux=[(proj,) + mine["ga"], (proj,) + mine["gb"], pa, pb], post_outs=[BF16] * 4)
    dya = matmul_nn(dma, w_a_v, transpose_rhs=True, name="d_proj_a")
    dyb = matmul_nn(dmb, wb_pad, transpose_rhs=True, name="d_proj_b", out_dtype=BF16)

    def ya_bwd_fn(rows, vecs):
        o, hg, dy = rows
        ng = vecs[0]
        sg = _sigmoid(hg)
        gate = hg * sg
        dn_parts, do_parts, n_parts = [], [], []
        for h in range(HG_HEADS):
            hs = slice(h * HG_DIM, (h + 1) * HG_DIM)
            oh = o[:, hs]
            r = lax.rsqrt(jnp.mean(oh * oh, axis=-1, keepdims=True) + RMS_EPS)
            nh = oh * r
            dn = dy[:, hs] * ng[:, hs] * gate[:, hs]
            do_parts.append(r * (dn - nh * jnp.mean(dn * nh, axis=-1, keepdims=True)))
            n_parts.append(nh)
        nrm = jnp.concatenate(n_parts, axis=1)
        dhg = dy * nrm * ng * (sg * (1.0 + hg * (1.0 - sg)))
        return [jnp.concatenate(do_parts, axis=1), dhg], [_colsum(dy * nrm * gate)]
    (do_raw, dhg), (g_norm_g,) = rowwise(ya_bwd_fn, [o_raw, (proj,) + mine["hg"], dya], [hg_norm_g],
                                         [(hw, F32), (hw, BF16)], [hw], name="hgrn2_out_bwd")
    dhq, dhf, dhi, g_lb = hgrn2_bwd(proj, [mine["hq"][0], mine["hf"][0], mine["hi"][0]], lb, do_raw, hg_states,
                                    n_seq, seq, name="hgrn2_bwd")

    do_fox = as_seq(dyb)
    dqa, dka, dva, dsum = fox_bwd(as_seq(qa), as_seq(ka), as_seq(va), do_fox, ox_fox, lse, name="fox_bwd")

    def unprep_fn(rows, vecs):
        mm = lambda a_, b_: jnp.dot(a_.astype(BF16), b_, preferred_element_type=F32)
        return [mm(rows[0], vecs[0]), mm(rows[1], vecs[1]), mm(rows[2], vecs[1])], []
    (dfq, dfk, dfv), _ = rowwise(unprep_fn, [dqa.reshape(T, wa), dka.reshape(T, wa), dva.reshape(T, wa)],
                                 [place["pqt"], place["pkt"]], [(hw, BF16)] * 3, name="fox_unprep")
    dc = -dsum.reshape(n_seq, fh, seq).transpose(0, 2, 1).reshape(T, fh)
    dc = jnp.concatenate([dc, jnp.zeros((T, LANES - fh), F32)], axis=1)
    dlf = seq_cumsum(dc, n_seq, seq, reverse=True, name="fox_cumsum_bwd")

    def lf_bwd_fn(rows, vecs):
        u = rows[0] + vecs[0]
        du_ = rows[1] * _sigmoid(-u)
        return [du_], [_colsum(du_)]
    (dff_,), (g_fb,) = rowwise(lf_bwd_fn, [(proj,) + mine["ff"], dlf], [fb_pad], [(LANES, BF16)], [LANES],
                               name="fox_logf_bwd")

    dproj = jnp.concatenate([dga, dgb, dhq, dhf, dhi, dhg, dfq, dfk, dfv, dff_], axis=1)

    grads_b = jnp.zeros((4, b_rows, d), F32)
    for nm, lhs, rhs in (("w_ff1", h1b, du), ("w_ff2", act, dz2b)):
        grads_b = matmul_tn(lhs, rhs, name="g_" + nm, into=(grads_b, lay[nm][0], lay[nm][1], axis[nm]))
    gfull = {
        "w_a": matmul_tn(y_a, dma, name="g_w_a"),
        "w_b": matmul_tn(y_b, dmb, name="g_w_b").reshape(fh, FOX_AUG, d)[:, :FOX_HDIM].reshape(hw, d),
        "w_o": matmul_tn(merged, dz1b, name="g_w_o"),
        "w_pg": matmul_tn(h1b, dpg, name="g_w_pg"),
        "w_p": matmul_tn(p_b, dpe, name="g_w_p"),
    }

    def chip_parts(nm, s):
        g = gfull[nm]
        n = g.shape[axis[nm]] // 4
        return lax.slice_in_dim(g, s * n, (s + 1) * n, axis=axis[nm])
    for nm in gfull:
        grads_b = lax.dynamic_update_slice(grads_b, jnp.stack([chip_parts(nm, s) for s in range(4)]),
                                           (0, lay[nm][0], lay[nm][1]))
    me = 2 * lax.axis_index("x") + lax.axis_index("y")
    core = lax.axis_index("c")

    def sum2_fn(rows, vecs):
        s = rows[0] + rows[1].astype(F32)
        return [s, s], []

    def sum4_fn(rows, vecs):
        a, r0, r1, r2 = rows
        return [((a + r0.astype(F32)) + r1.astype(F32)) + r2.astype(F32)], []

    def chip_pair_sum(g, tag):
        h, cols = g.shape[1] // 2, g.shape[2]
        keep = lax.dynamic_slice_in_dim(g, core * h, h, axis=1)
        give = lax.dynamic_slice_in_dim(g, (1 - core) * h, h, axis=1).astype(BF16)
        (from_core,) = swap_cores([give], name="swap_partials_" + tag)
        (s32, s16), _ = rowwise(sum2_fn, [keep.reshape(4 * h, cols), from_core.reshape(4 * h, cols)], [],
                                [(cols, F32), (cols, BF16)], name="sum_cores_" + tag, tm=SUM_TILE)
        return s32.reshape(4, h, cols), s16.reshape(4, h, cols)

    def chip_sum(pr, gt, tag):
        own = lax.dynamic_index_in_dim(pr, me, axis=0, keepdims=False)
        (q,), _ = rowwise(sum4_fn, [own, gt[0], gt[1], gt[2]], [], [(own.shape[1], F32)], name="sum_chips_" + tag,
                          tm=SUM_TILE)
        return q

    pair_rest, pair_rest_b = chip_pair_sum(grads_b, "rest")
    gw_in_mine, (got_rest,) = matmul_tn(h0b, dproj, name="g_w_in", rider=scatter_rider([pair_rest_b]))
    gfull["w_in"] = jnp.concatenate([gw_in_mine[:, mine[nm][0]:mine[nm][0] + orig[nm][1]]
                                     for nm in ["hq", "hf", "hi", "hg", "fq", "fk", "fv", "ff", "ga", "gb"]], axis=1)
    grads_a = jnp.stack([pack_a(chip_parts("w_in", s)) for s in range(4)])
    pair_in, pair_in_b = chip_pair_sum(grads_a, "w_in")
    def ln0_bwd_post(dh0_in, aux, vecs):
        dx, dg, db = _ln_bwd(aux[1], dh0_in + ALPHA * aux[0], vecs[0])
        return [dx], [dg, db]
    ((dx,), (g_ln0_g, g_ln0_b)), (got_in,) = matmul_nn(
        dproj, win_mine, transpose_rhs=True, name="d_in_proj_ln0", tm=fused_tm, post=ln0_bwd_post,
        post_aux=[dz1, x2], post_vecs=[vec(ln0_g)], post_outs=[F32], post_accs=[d, d],
        rider=scatter_rider([pair_in_b]))
    q_half = [chip_sum(pair_in, got_in, "w_in"), chip_sum(pair_rest, got_rest, "rest")]
    q_other = swap_cores(q_half, name="swap_halves")
    g_a, g_b = [jnp.concatenate([jnp.where(core == 0, mine_, other), jnp.where(core == 0, other, mine_)], axis=0)
                for mine_, other in zip(q_half, q_other)]
    g_shards = unpack_b(g_b, lay)
    g_shards["w_in"] = g_a[:, :in_cols]

    assert d == PACK_W and 2 * hw == PACK_W and fh <= LANES
    small = allreduce_small(jnp.concatenate(
        [g_ln0_g, g_ln0_b, g_ln1_g, g_ln1_b, g_ln2_g, g_ln2_b, jnp.concatenate([g_norm_g, g_lb], axis=1),
         jnp.concatenate([g_fb, loss_part[:, LANES:]], axis=1)], axis=0), name="allreduce_small")
    loss = small[7, LANES]

    small_w = [vec(ln0_g), vec(ln0_b), ln1_g, ln1_b, ln2_g, ln2_b, hg_lb, hg_norm_g, fox_fb]
    small_m = [vec(m_ln0_g), vec(m_ln0_b), m_ln1_g, m_ln1_b, m_ln2_g, m_ln2_b, m_hg_lb, m_hg_norm_g, m_fox_fb]
    small_v = [vec(v_ln0_g), vec(v_ln0_b), v_ln1_g, v_ln1_b, v_ln2_g, v_ln2_b, v_hg_lb, v_hg_norm_g, v_fox_fb]
    small_out = adamw_small(small, probs[0:1], small_w, small_m, small_v, name="adamw_small")
    small_shapes = [ln0_g.shape, ln0_b.shape, ln1_g.shape, ln1_b.shape, ln2_g.shape, ln2_b.shape, hg_lb.shape,
                    hg_norm_g.shape, fox_fb.shape]
    sg_out, sd_out, sm_out, sv_out = [[a.reshape(shp) for a, shp in zip(small_out[9 * k:9 * k + 9], small_shapes)]
                                      for k in range(4)]

    big_out = {}
    for nm in names:
        delta, m2, v2 = adamw(big[nm], g_shards[nm], big_m[nm], big_v[nm], name="adamw_" + nm)
        big_out[nm] = (g_shards[nm][None], delta[None], m2[None], v2[None])

    def ordered(k):
        sm_ = [sg_out, sd_out, sm_out, sv_out][k]
        bg = lambda nm: big_out[nm][k]
        return [sm_[0], sm_[1], bg("w_in"), sm_[6], sm_[7], sm_[8], bg("w_a"), bg("w_b"), bg("w_o"), sm_[2], sm_[3],
                bg("w_ff1"), bg("w_ff2"), bg("w_pg"), bg("w_p"), sm_[4], sm_[5]]
    grad_x = dx.reshape(n_seq, seq, d)
    return (loss, grad_x, *ordered(0), *ordered(1), *ordered(2), *ordered(3))
```
